```python
import math
import jax, jax.numpy as jnp
from jax import lax
import numpy as np

D_MODEL = 1024
BATCH = 2
SEQ = 8192
DEPTH = 1

MEM_LEN = 256
D_MIX = D_MODEL
HEAD_DIM = 64
NSA_HEADS = 8
NSA_KV_GROUPS = 2
NSA_HPG = NSA_HEADS // NSA_KV_GROUPS
CMP_BLOCK = 32
CMP_STRIDE = 16
SLC_BLOCK = 64
SLC_TOPK = 16
WINDOW = 512
Q_BLOCK = 128
RWKV_HEADS = 8
RWKV_DIM = RWKV_HEADS * HEAD_DIM
DECAY_LORA = 64
ICLR_LORA = 64
GATE_LORA = 128
GN_EPS = HEAD_DIM * 1e-5
X_HEADS = 4
X_HEAD_DIM = D_MODEL // X_HEADS
N_EXPERTS = 32
TOP_K = 4
D_EXPERT = D_MODEL
SWIGLU_LIMIT = 7.0
SWIGLU_ALPHA = 1.702
MOE_ROW_BLOCK = 256
RMS_EPS = 1e-6
NEG_INF = -1e30
BIG = 1e9

NSA_SPLITS = (NSA_HEADS * HEAD_DIM,) + (NSA_KV_GROUPS * HEAD_DIM,) * 6 + (NSA_HEADS * 3,)
RWKV_SPLITS = (RWKV_DIM, RWKV_DIM, RWKV_DIM, DECAY_LORA, ICLR_LORA, GATE_LORA)
NSA_PROJ = sum(NSA_SPLITS)
RWKV_PROJ = sum(RWKV_SPLITS)
P_TOTAL = NSA_PROJ + RWKV_PROJ

kernel_name = "nsa_rwkv7_hybrid_moe_block"


def rms_norm(x, g):
    xf = x.astype(jnp.float32)
    y = xf * lax.rsqrt(jnp.mean(xf * xf, axis=-1, keepdims=True) + RMS_EPS)
    return (y * g.astype(jnp.float32)).astype(x.dtype)


def split_cols(p, sizes):
    return jnp.split(p, np.cumsum(sizes)[:-1].tolist(), axis=-1)


def masked_softmax(s, mask):
    return jax.nn.softmax(jnp.where(mask, s, NEG_INF), axis=-1) * mask


def alibi_slopes(n):
    return (2.0 ** (-8.0 * np.arange(1, n + 1) / n)).astype(np.float32)


def nsa_mixer(p, q_g, kc_g, ks_g, kw_g, pe_k, pe_v, ck_w1, ck_w2, cv_w1, cv_w2):
    B, S, _ = p.shape
    H, G, HPG, Dh = NSA_HEADS, NSA_KV_GROUPS, NSA_HPG, HEAD_DIM
    scale = Dh ** -0.5
    n_sub = CMP_BLOCK // CMP_STRIDE
    n_cmp = S // CMP_STRIDE - n_sub + 1
    n_sel = S // SLC_BLOCK
    n_top = min(SLC_TOPK, n_sel)
    nq = S // Q_BLOCK

    q, kc, vc, ks, vs, kw, vw, gl = split_cols(p, NSA_SPLITS)
    q = rms_norm(q.reshape(B, S, H, Dh), q_g)

    def compress(t, pe, w1, w2):
        ch = t.reshape(B, S // CMP_STRIDE, CMP_STRIDE, G, Dh)
        blk = jnp.concatenate([ch[:, j:j + n_cmp] for j in range(n_sub)], axis=2)
        blk = blk + pe[None, None, :, None, :]
        blk = blk.transpose(0, 3, 1, 2, 4).reshape(B, G, n_cmp, CMP_BLOCK * Dh)
        return jax.nn.silu(blk @ w1) @ w2

    k_cmp = rms_norm(compress(kc.reshape(B, S, G, Dh), pe_k, ck_w1, ck_w2), kc_g)
    v_cmp = compress(vc.reshape(B, S, G, Dh), pe_v, cv_w1, cv_w2)
    k_slc = rms_norm(ks.reshape(B, S, G, Dh), ks_g).transpose(0, 2, 1, 3).reshape(B, G, n_sel, SLC_BLOCK, Dh)
    v_slc = vs.reshape(B, S, G, Dh).transpose(0, 2, 1, 3).reshape(B, G, n_sel, SLC_BLOCK, Dh)
    pad = ((0, 0), (0, 0), (WINDOW, 0), (0, 0))
    k_win = jnp.pad(rms_norm(kw.reshape(B, S, G, Dh), kw_g).transpose(0, 2, 1, 3), pad)
    v_win = jnp.pad(vw.reshape(B, S, G, Dh).transpose(0, 2, 1, 3), pad)
    gates = jax.nn.sigmoid(gl.astype(jnp.float32)).astype(p.dtype)

    q_blocks = q.reshape(B, nq, Q_BLOCK, G, HPG, Dh).transpose(1, 0, 3, 4, 2, 5)
    g_blocks = gates.reshape(B, nq, Q_BLOCK, G, HPG, 3).transpose(1, 0, 3, 4, 2, 5)
    t0s = jnp.arange(nq, dtype=jnp.int32) * Q_BLOCK

    slopes = jnp.asarray(alibi_slopes(H)).reshape(1, G, HPG, 1, 1)
    c_start = np.arange(n_cmp) * CMP_STRIDE
    c_end_np = c_start + CMP_BLOCK
    s_start = np.arange(n_sel) * SLC_BLOCK
    overlap = jnp.asarray((np.clip(np.minimum(c_end_np[:, None], s_start[None] + SLC_BLOCK)
                                   - np.maximum(c_start[:, None], s_start[None]), 0, None)
                           / CMP_BLOCK).astype(np.float32))
    c_last = jnp.asarray(c_end_np - 1, dtype=jnp.int32)
    b_ix = jnp.arange(B)[:, None, None, None]
    g_ix = jnp.arange(G)[None, :, None, None]
    bid = jnp.arange(n_sel, dtype=jnp.int32)
    win_off = jnp.arange(WINDOW + Q_BLOCK, dtype=jnp.int32)
    slc_off = jnp.arange(SLC_BLOCK, dtype=jnp.int32)

    def block(args):
        qb, gb, t0 = args
        t = t0 + jnp.arange(Q_BLOCK, dtype=jnp.int32)
        tq = t[:, None]
        dc = tq - c_last[None, :]
        s = jnp.einsum('bghqd,bgcd->bghqc', qb, k_cmp).astype(jnp.float32) * scale - slopes * dc.astype(jnp.float32)
        p_c = masked_softmax(s, dc >= 0)
        o_c = jnp.einsum('bghqc,bgcd->bghqd', p_c.astype(v_cmp.dtype), v_cmp)
        imp = jnp.einsum('bghqc,cn->bgqn', p_c, overlap)
        cur = tq // SLC_BLOCK
        forced = (bid[None] == 0) | (bid[None] == cur) | (bid[None] == cur - 1)
        score = jnp.where(forced, BIG, jnp.where(bid[None] * SLC_BLOCK <= tq, imp, -BIG))
        _, idx = lax.top_k(score, n_top)
        kg = k_slc[b_ix, g_ix, idx].reshape(B, G, Q_BLOCK, n_top * SLC_BLOCK, Dh)
        vg = v_slc[b_ix, g_ix, idx].reshape(B, G, Q_BLOCK, n_top * SLC_BLOCK, Dh)
        kpos = (idx[..., None] * SLC_BLOCK + slc_off).reshape(B, G, Q_BLOCK, n_top * SLC_BLOCK)
        ds = (t[None, None, :, None] - kpos)[:, :, None]
        s = jnp.einsum('bghqd,bgqkd->bghqk', qb, kg).astype(jnp.float32) * scale - slopes * ds.astype(jnp.float32)
        p_s = masked_softmax(s, ds >= 0)
        o_s = jnp.einsum('bghqk,bgqkd->bghqd', p_s.astype(vg.dtype), vg)
        kb = lax.dynamic_slice_in_dim(k_win, t0, WINDOW + Q_BLOCK, axis=2)
        vb = lax.dynamic_slice_in_dim(v_win, t0, WINDOW + Q_BLOCK, axis=2)
        spos = t0 - WINDOW + win_off
        dw = tq - spos[None]
        s = jnp.einsum('bghqd,bgkd->bghqk', qb, kb).astype(jnp.float32) * scale - slopes * dw.astype(jnp.float32)
        p_w = masked_softmax(s, (spos[None] >= 0) & (dw >= 0) & (dw < WINDOW))
        o_w = jnp.einsum('bghqk,bgkd->bghqd', p_w.astype(vb.dtype), vb)
        o = gb[..., 0:1] * o_c + gb[..., 1:2] * o_s + gb[..., 2:3] * o_w
        return o.astype(qb.dtype)

    outs = lax.map(block, (q_blocks, g_blocks, t0s))
    return outs.transpose(1, 0, 4, 2, 3, 5).reshape(B, S, H * Dh)


def wkv7_scan(r, w, k, v, a, b):
    B, S, H, N = r.shape
    xs = tuple(jnp.moveaxis(t.astype(jnp.float32), 1, 0) for t in (r, w, k, v, a, b))

    def step(state, inp):
        r_t, w_t, k_t, v_t, a_t, b_t = inp
        sa = jnp.einsum('bhij,bhj->bhi', state, a_t)
        state = state * w_t[:, :, None, :] + sa[..., None] * b_t[:, :, None, :] + v_t[..., None] * k_t[:, :, None, :]
        return state, jnp.einsum('bhij,bhj->bhi', state, r_t)

    _, ys = lax.scan(step, jnp.zeros((B, H, N, N), jnp.float32), xs)
    return jnp.moveaxis(ys, 0, 1)


def rwkv7_mixer(p, mu, w0, w_up, a0, a_up, g_up, k_k, k_a, r_k, ln_w, ln_b):
    B, S, _ = p.shape
    H, N = RWKV_HEADS, HEAD_DIM
    prev = jnp.pad(p, ((0, 0), (1, 0), (0, 0)))[:, :-1]
    p = p + (prev - p) * mu
    r, k, v, wd, ad, gd = split_cols(p, RWKV_SPLITS)
    w = -jax.nn.softplus(-(w0 + jnp.tanh(wd) @ w_up)) - 0.5
    a = jax.nn.sigmoid(a0 + ad @ a_up)
    g = jax.nn.sigmoid(gd) @ g_up
    hs = lambda t: t.reshape(B, S, H, N)
    kk = hs(k * k_k).astype(jnp.float32)
    kk = kk / jnp.maximum(jnp.sqrt(jnp.sum(kk * kk, axis=-1, keepdims=True)), 1e-12)
    k = k * (1.0 + (a - 1.0) * k_a)
    decay = jnp.exp(-jnp.exp(w.astype(jnp.float32)))
    r_, k_, v_, a_ = hs(r), hs(k), hs(v), hs(a).astype(jnp.float32)
    y = wkv7_scan(r_, hs(decay), k_, v_, -kk, kk * a_)
    mean = jnp.mean(y, axis=-1, keepdims=True)
    var = jnp.mean(jnp.square(y - mean), axis=-1, keepdims=True)
    y = ((y - mean) * lax.rsqrt(var + GN_EPS)).reshape(B, S, RWKV_DIM).astype(p.dtype) * ln_w + ln_b
    y = y + (jnp.sum(r_ * k_ * r_k, axis=-1, keepdims=True) * v_).reshape(B, S, RWKV_DIM)
    return y * g


def memory_cross_attention(hn, memn, xq_w, xk_w, xv_w, xq_g, xk_g, xo_w):
    B, S, D = hn.shape
    M = memn.shape[1]
    q = rms_norm((hn @ xq_w).reshape(B, S, X_HEADS, X_HEAD_DIM), xq_g)
    k = rms_norm((memn @ xk_w).reshape(B, M, X_HEADS, X_HEAD_DIM), xk_g)
    v = (memn @ xv_w).reshape(B, M, X_HEADS, X_HEAD_DIM)
    s = jnp.einsum('bshd,bmhd->bhsm', q, k).astype(jnp.float32) * (X_HEAD_DIM ** -0.5)
    pr = jax.nn.softmax(s, axis=-1)
    o = jnp.einsum('bhsm,bmhd->bshd', pr.astype(v.dtype), v).reshape(B, S, D)
    return o @ xo_w


def clamped_swiglu(h):
    hg = jnp.minimum(h[..., ::2], SWIGLU_LIMIT)
    hl = jnp.clip(h[..., 1::2], -SWIGLU_LIMIT, SWIGLU_LIMIT)
    return hg * jax.nn.sigmoid(SWIGLU_ALPHA * hg) * (hl + 1.0)


def moe_ffn(x, router_w, router_b, mlp1_w, mlp1_b, mlp2_w, mlp2_b):
    B, S, D = x.shape
    T = B * S
    A = T * TOP_K
    M = MOE_ROW_BLOCK
    xt = x.reshape(T, D)
    logits = (xt @ router_w + router_b).astype(jnp.float32)
    top_v, top_i = lax.top_k(logits, TOP_K)
    gate = jax.nn.softmax(top_v, axis=-1)
    e_flat = top_i.reshape(A)
    order = jnp.argsort(e_flat, stable=True)
    e_sorted = e_flat[order]
    counts = jnp.bincount(e_flat, length=N_EXPERTS)
    starts = jnp.cumsum(counts) - counts
    padded = (counts + M - 1) // M * M
    pends = jnp.cumsum(padded)
    pstarts = pends - padded
    dest = pstarts[e_sorted] + jnp.arange(A) - starts[e_sorted]
    n_blocks = -(-A // M) + N_EXPERTS
    R = n_blocks * M
    row_tok = jnp.full((R,), T, jnp.int32).at[dest].set((order // TOP_K).astype(jnp.int32))
    row_w = jnp.zeros((R,), jnp.float32).at[dest].set(gate.reshape(A)[order])
    blk_e = jnp.minimum(jnp.searchsorted(pends, jnp.arange(n_blocks) * M, side='right'), N_EXPERTS - 1)
    xpad = jnp.concatenate([xt, jnp.zeros((1, D), xt.dtype)], axis=0)

    def expert_block(args):
        tok, wr, e = args
        hb = xpad[tok] @ mlp1_w[e] + mlp1_b[e]
        yb = clamped_swiglu(hb) @ mlp2_w[e] + mlp2_b[e]
        return yb * wr[:, None].astype(yb.dtype)

    ys = lax.map(expert_block, (row_tok.reshape(n_blocks, M), row_w.reshape(n_blocks, M), blk_e))
    y = jax.ops.segment_sum(ys.reshape(R, D), row_tok, num_segments=T + 1)[:T]
    return y.reshape(B, S, D).astype(x.dtype)


def hybrid_layer(h, mem, norm_mix_g, w_in, q_norm_g, k_cmp_norm_g, k_slc_norm_g, k_win_norm_g,
                 cmp_pe_k, cmp_pe_v, cmp_k_w1, cmp_k_w2, cmp_v_w1, cmp_v_w2,
                 rwkv_mu, rwkv_w0, rwkv_w_up, rwkv_a0, rwkv_a_up, rwkv_g_up, rwkv_k_k, rwkv_k_a,
                 rwkv_r_k, rwkv_ln_w, rwkv_ln_b, w_out,
                 norm_x_g, norm_mem_g, xq_w, xk_w, xv_w, xq_norm_g, xk_norm_g, xo_w,
                 norm_ffn_g, router_w, router_b, mlp1_w, mlp1_b, mlp2_w, mlp2_b):
    p = rms_norm(h, norm_mix_g) @ w_in
    o_nsa = nsa_mixer(p[..., :NSA_PROJ], q_norm_g, k_cmp_norm_g, k_slc_norm_g, k_win_norm_g,
                      cmp_pe_k, cmp_pe_v, cmp_k_w1, cmp_k_w2, cmp_v_w1, cmp_v_w2)
    o_rwkv = rwkv7_mixer(p[..., NSA_PROJ:], rwkv_mu, rwkv_w0, rwkv_w_up, rwkv_a0, rwkv_a_up, rwkv_g_up,
                         rwkv_k_k, rwkv_k_a, rwkv_r_k, rwkv_ln_w, rwkv_ln_b)
    h = h + jnp.concatenate([o_nsa, o_rwkv], axis=-1) @ w_out
    h = h + memory_cross_attention(rms_norm(h, norm_x_g), rms_norm(mem, norm_mem_g),
                                   xq_w, xk_w, xv_w, xq_norm_g, xk_norm_g, xo_w)
    h = h + moe_ffn(rms_norm(h, norm_ffn_g), router_w, router_b, mlp1_w, mlp1_b, mlp2_w, mlp2_b)
    return h


def setup_inputs(seed: int = 0) -> dict:
    key = jax.random.key(seed)
    ks = iter(jax.random.split(key, 64))
    f32 = jnp.float32

    def nrm(shape, scale):
        return scale * jax.random.normal(next(ks), (DEPTH,) + shape, f32)

    def gain(n):
        return 1.0 + 0.02 * jax.random.normal(next(ks), (DEPTH, n), f32)

    def unif(shape, lo, hi):
        return jax.random.uniform(next(ks), (DEPTH,) + shape, f32, lo, hi)

    Dh, L, G = HEAD_DIM, CMP_BLOCK, NSA_KV_GROUPS
    return {
        "x": jax.random.normal(next(ks), (BATCH, SEQ, D_MODEL), f32),
        "mem": jax.random.normal(next(ks), (BATCH, MEM_LEN, D_MODEL), f32),
        "norm_mix_g": gain(D_MODEL),
        "w_in": nrm((D_MODEL, P_TOTAL), D_MODEL ** -0.5),
        "q_norm_g": gain(Dh),
        "k_cmp_norm_g": gain(Dh),
        "k_slc_norm_g": gain(Dh),
        "k_win_norm_g": gain(Dh),
        "cmp_pe_k": nrm((L, Dh), 0.1),
        "cmp_pe_v": nrm((L, Dh), 0.1),
        "cmp_k_w1": nrm((L * Dh, Dh), (L * Dh) ** -0.5),
        "cmp_k_w2": nrm((Dh, Dh), Dh ** -0.5),
        "cmp_v_w1": nrm((L * Dh, Dh), (L * Dh) ** -0.5),
        "cmp_v_w2": nrm((Dh, Dh), Dh ** -0.5),
        "rwkv_mu": unif((RWKV_PROJ,), 0.0, 1.0),
        "rwkv_w0": unif((RWKV_DIM,), -6.0, -1.0),
        "rwkv_w_up": nrm((DECAY_LORA, RWKV_DIM), 0.1 * DECAY_LORA ** -0.5),
        "rwkv_a0": nrm((RWKV_DIM,), 0.1),
        "rwkv_a_up": nrm((ICLR_LORA, RWKV_DIM), 0.5 * ICLR_LORA ** -0.5),
        "rwkv_g_up": nrm((GATE_LORA, RWKV_DIM), GATE_LORA ** -0.5),
        "rwkv_k_k": 0.85 + nrm((RWKV_DIM,), 0.02),
        "rwkv_k_a": gain(RWKV_DIM),
        "rwkv_r_k": nrm((RWKV_HEADS, HEAD_DIM), 0.1),
        "rwkv_ln_w": gain(RWKV_DIM),
        "rwkv_ln_b": nrm((RWKV_DIM,), 0.02),
        "w_out": nrm((D_MIX, D_MODEL), D_MIX ** -0.5),
        "norm_x_g": gain(D_MODEL),
        "norm_mem_g": gain(D_MODEL),
        "xq_w": nrm((D_MODEL, D_MODEL), D_MODEL ** -0.5),
        "xk_w": nrm((D_MODEL, D_MODEL), D_MODEL ** -0.5),
        "xv_w": nrm((D_MODEL, D_MODEL), D_MODEL ** -0.5),
        "xq_norm_g": gain(X_HEAD_DIM),
        "xk_norm_g": gain(X_HEAD_DIM),
        "xo_w": nrm((D_MODEL, D_MODEL), D_MODEL ** -0.5),
        "norm_ffn_g": gain(D_MODEL),
        "router_w": nrm((D_MODEL, N_EXPERTS), D_MODEL ** -0.5),
        "router_b": nrm((N_EXPERTS,), 0.01),
        "mlp1_w": nrm((N_EXPERTS, D_MODEL, 2 * D_EXPERT), D_MODEL ** -0.5),
        "mlp1_b": nrm((N_EXPERTS, 2 * D_EXPERT), 0.02),
        "mlp2_w": nrm((N_EXPERTS, D_EXPERT, D_MODEL), D_EXPERT ** -0.5),
        "mlp2_b": nrm((N_EXPERTS, D_MODEL), 0.02),
    }


def reference(x, mem, norm_mix_g, w_in, q_norm_g, k_cmp_norm_g, k_slc_norm_g, k_win_norm_g,
              cmp_pe_k, cmp_pe_v, cmp_k_w1, cmp_k_w2, cmp_v_w1, cmp_v_w2,
              rwkv_mu, rwkv_w0, rwkv_w_up, rwkv_a0, rwkv_a_up, rwkv_g_up, rwkv_k_k, rwkv_k_a,
              rwkv_r_k, rwkv_ln_w, rwkv_ln_b, w_out,
              norm_x_g, norm_mem_g, xq_w, xk_w, xv_w, xq_norm_g, xk_norm_g, xo_w,
              norm_ffn_g, router_w, router_b, mlp1_w, mlp1_b, mlp2_w, mlp2_b):
    layer_params = (norm_mix_g, w_in, q_norm_g, k_cmp_norm_g, k_slc_norm_g, k_win_norm_g,
                    cmp_pe_k, cmp_pe_v, cmp_k_w1, cmp_k_w2, cmp_v_w1, cmp_v_w2,
                    rwkv_mu, rwkv_w0, rwkv_w_up, rwkv_a0, rwkv_a_up, rwkv_g_up, rwkv_k_k, rwkv_k_a,
                    rwkv_r_k, rwkv_ln_w, rwkv_ln_b, w_out,
                    norm_x_g, norm_mem_g, xq_w, xk_w, xv_w, xq_norm_g, xk_norm_g, xo_w,
                    norm_ffn_g, router_w, router_b, mlp1_w, mlp1_b, mlp2_w, mlp2_b)
    h = x
    for l in range(DEPTH):
        h = hybrid_layer(h, mem, *[prm[l] for prm in layer_params])
    return h
```

```python
import functools

import numpy as np
import jax
import jax.numpy as jnp
from jax import lax
from jax.experimental import pallas as pl
from jax.experimental.pallas import tpu as pltpu

F32 = jnp.float32
BF16 = jnp.bfloat16
HIGHEST = lax.Precision.HIGHEST

V7X_VMEM_BYTES = 64 * 1024 * 1024
VMEM_LIMIT = V7X_VMEM_BYTES * 3 // 4

HEAD_DIM = 64
NSA_HEADS = 8
NSA_GROUPS = 2
NSA_HPG = NSA_HEADS // NSA_GROUPS
GROUP_W = NSA_HPG * HEAD_DIM
CMP_BLOCK = 32
CMP_STRIDE = 16
SLC_BLOCK = 64
SLC_TOPK = 16
WINDOW = 512
Q_BLOCK = 128
SEL_CHUNK = 512
RWKV_HEADS = 8
RWKV_DIM = RWKV_HEADS * HEAD_DIM
RWKV_CHUNK = 64
GN_EPS = HEAD_DIM * 1e-5
X_HEADS = 4
N_EXPERTS = 32
TOP_K = 4
SWIGLU_LIMIT = 7.0
SWIGLU_ALPHA = 1.702
MOE_ROW_BLOCK = 256
RMS_EPS = 1e-6
NEG_INF = -1e30
BIG = 1e9
REMOVED = -3e38
LANES = 128

NSA_PROJ = NSA_HEADS * HEAD_DIM + 6 * NSA_GROUPS * HEAD_DIM + NSA_HEADS * 3
NSA_PROJ_PAD = -(-NSA_PROJ // LANES) * LANES
RWKV_PROJ = 3 * RWKV_DIM + 64 + 64 + 128


def _cparams(*sem):
    return pltpu.CompilerParams(dimension_semantics=sem, vmem_limit_bytes=VMEM_LIMIT)


def _dot(a, b, **kw):
    return jnp.dot(a, b, preferred_element_type=F32, **kw)


def _dot_nt(a, b, **kw):
    return lax.dot_general(a, b, (((1,), (1,)), ((), ())), preferred_element_type=F32, **kw)


def _dot_tn(a, b, **kw):
    return lax.dot_general(a, b, (((0,), (0,)), ((), ())), preferred_element_type=F32, **kw)


def _rms(x, g):
    return x * lax.rsqrt(jnp.mean(x * x, axis=-1, keepdims=True) + RMS_EPS) * g


def _block_diag_ones(n, blk, scale=1.0):
    i = np.arange(n)
    return jnp.asarray(((i[:, None] // blk) == (i[None, :] // blk)).astype(np.float32) * scale)


def _norm_matmul_kernel(x_ref, g_ref, w_ref, o_ref):
    xn = _rms(x_ref[...], g_ref[...]).astype(BF16)
    o_ref[...] = _dot(xn, w_ref[...])


def norm_matmul(x, g, w, tm):
    m, d = x.shape
    n = w.shape[1]
    return pl.pallas_call(
        _norm_matmul_kernel,
        grid=(m // tm,),
        in_specs=[pl.BlockSpec((tm, d), lambda i: (i, 0)),
                  pl.BlockSpec((1, d), lambda i: (0, 0)),
                  pl.BlockSpec((d, n), lambda i: (0, 0))],
        out_specs=pl.BlockSpec((tm, n), lambda i: (i, 0)),
        out_shape=jax.ShapeDtypeStruct((m, n), F32),
        compiler_params=_cparams("parallel"),
        name="norm_matmul",
    )(x, g.reshape(1, d), w.astype(BF16))


def _nsa_prep_kernel(p_ref, qg_ref, ksg_ref, kwg_ref, bdq_ref, bdk_ref, ek_ref, eg_ref,
                     q_o, ks_o, vs_o, kw_o, vw_o, gc_o, gs_o, gw_o):
    p = p_ref[...]
    nq = NSA_HEADS * HEAD_DIM
    gw = NSA_GROUPS * HEAD_DIM
    q = p[:, :nq]
    msq = _dot(q * q, bdq_ref[...], precision=HIGHEST)
    q_o[...] = (q * lax.rsqrt(msq + RMS_EPS) * qg_ref[...] * (HEAD_DIM ** -0.5)).astype(BF16)

    def seg(k):
        return p[:, nq + k * gw: nq + (k + 1) * gw]

    def head_norm(t, g):
        ms = _dot(t * t, bdk_ref[...], precision=HIGHEST)
        return t * lax.rsqrt(ms + RMS_EPS) * g

    ek = ek_ref[...]
    widen = lambda t: _dot(t.astype(BF16), ek).astype(BF16)
    ks_o[...] = widen(head_norm(seg(2), ksg_ref[...]))
    vs_o[...] = widen(seg(3))
    kw_o[...] = widen(head_norm(seg(4), kwg_ref[...]))
    vw_o[...] = widen(seg(5))
    gates = jax.nn.sigmoid(p[:, nq + 6 * gw: nq + 6 * gw + LANES])
    for br, o in enumerate((gc_o, gs_o, gw_o)):
        o[...] = _dot(gates, eg_ref[br], precision=HIGHEST)


def nsa_prep(p_nsa, q_g, ks_g, kw_g, tm):
    t = p_nsa.shape[0]
    nq = NSA_HEADS * HEAD_DIM
    gw = NSA_GROUPS * HEAD_DIM
    wide = NSA_GROUPS * GROUP_W
    ek = np.zeros((gw, wide), np.float32)
    for g in range(NSA_GROUPS):
        for h in range(NSA_HPG):
            for d in range(HEAD_DIM):
                ek[g * HEAD_DIM + d, g * GROUP_W + h * HEAD_DIM + d] = 1.0
    eg = np.zeros((3, LANES, nq), np.float32)
    for br in range(3):
        for h in range(NSA_HEADS):
            eg[br, h * 3 + br, h * HEAD_DIM:(h + 1) * HEAD_DIM] = 1.0
    tile = lambda v, n: jnp.tile(v.reshape(1, HEAD_DIM), (1, n))
    row = lambda w: pl.BlockSpec((tm, w), lambda i: (i, 0))
    full = lambda *s: pl.BlockSpec(s, lambda i: (0,) * len(s))
    return pl.pallas_call(
        _nsa_prep_kernel,
        grid=(t // tm,),
        in_specs=[row(NSA_PROJ_PAD), full(1, nq), full(1, gw), full(1, gw), full(nq, nq), full(gw, gw),
                  full(gw, wide), full(3, LANES, nq)],
        out_specs=[row(nq)] + [row(wide)] * 4 + [row(nq)] * 3,
        out_shape=[jax.ShapeDtypeStruct((t, nq), BF16)] + [jax.ShapeDtypeStruct((t, wide), BF16)] * 4
                  + [jax.ShapeDtypeStruct((t, nq), F32)] * 3,
        compiler_params=_cparams("parallel"),
        name="nsa_prep",
    )(p_nsa, tile(q_g, NSA_HEADS), tile(ks_g, NSA_GROUPS), tile(kw_g, NSA_GROUPS),
      _block_diag_ones(nq, HEAD_DIM, 1.0 / HEAD_DIM), _block_diag_ones(gw, HEAD_DIM, 1.0 / HEAD_DIM),
      jnp.asarray(ek, BF16), jnp.asarray(eg))


def _compress_kernel(ch_ref, pe_ref, w1_ref, w2_ref, g_ref, e_ref, o_ref):
    ch = ch_ref[0, 0, 0]
    nc = ch.shape[0]
    half = CMP_STRIDE * HEAD_DIM
    nxt = pltpu.roll(ch, nc - 1, 0)
    w1 = w1_ref[0]
    h1 = (_dot(ch, w1[:half], precision=HIGHEST) + _dot(nxt, w1[half:], precision=HIGHEST)
          + _dot(pe_ref[0], w1, precision=HIGHEST))
    out = _dot(jax.nn.silu(h1), w2_ref[0], precision=HIGHEST)
    out = jnp.where(pl.program_id(0) == 0, _rms(out, g_ref[...]), out)
    o_ref[0, 0, 0] = _dot(out.astype(BF16), e_ref[...]).astype(BF16)


def nsa_compress(kc, vc, pe_k, pe_v, kw1, kw2, vw1, vw2, kc_g, b, s):
    nc = s // CMP_STRIDE
    half = CMP_STRIDE * HEAD_DIM

    def chunks(t):
        return t.reshape(b, nc, CMP_STRIDE, NSA_GROUPS, HEAD_DIM).transpose(0, 3, 1, 2, 4).reshape(
            b, NSA_GROUPS, nc, half)

    ch = jnp.stack([chunks(kc), chunks(vc)])
    pe = jnp.stack([pe_k.reshape(1, 2 * half), pe_v.reshape(1, 2 * half)])
    e = np.zeros((HEAD_DIM, GROUP_W), np.float32)
    for h in range(NSA_HPG):
        e[np.arange(HEAD_DIM), h * HEAD_DIM + np.arange(HEAD_DIM)] = 1.0
    return pl.pallas_call(
        _compress_kernel,
        grid=(2, b, NSA_GROUPS),
        in_specs=[pl.BlockSpec((1, 1, 1, nc, half), lambda kv, bi, g: (kv, bi, g, 0, 0)),
                  pl.BlockSpec((1, 1, 2 * half), lambda kv, bi, g: (kv, 0, 0)),
                  pl.BlockSpec((1, 2 * half, HEAD_DIM), lambda kv, bi, g: (kv, 0, 0)),
                  pl.BlockSpec((1, HEAD_DIM, HEAD_DIM), lambda kv, bi, g: (kv, 0, 0)),
                  pl.BlockSpec((1, HEAD_DIM), lambda kv, bi, g: (0, 0)),
                  pl.BlockSpec((HEAD_DIM, GROUP_W), lambda kv, bi, g: (0, 0))],
        out_specs=pl.BlockSpec((1, 1, 1, nc, GROUP_W), lambda kv, bi, g: (kv, bi, g, 0, 0)),
        out_shape=jax.ShapeDtypeStruct((2, b, NSA_GROUPS, nc, GROUP_W), BF16),
        compiler_params=_cparams("parallel", "parallel", "parallel"),
        name="nsa_compress",
    )(ch, pe, jnp.stack([kw1, vw1]), jnp.stack([kw2, vw2]), kc_g.reshape(1, HEAD_DIM), jnp.asarray(e, BF16))


def _masked_softmax_rows(s, mask):
    sm = jnp.where(mask, s, NEG_INF)
    m = jnp.max(sm, axis=-1, keepdims=True)
    p = jnp.where(mask, jnp.exp(sm - m), 0.0)
    l = jnp.sum(p, axis=-1, keepdims=True)
    return p / jnp.where(l > 0.0, l, 1.0)


def _nsa_attn_kernel(q_ref, kc_ref, vc_ref, ov_ref, ks_ref, vs_ref, kw_ref, vw_ref,
                     gc_ref, gs_ref, gw_ref, o_ref, *, n_sel, n_top, seq):
    g = pl.program_id(1)
    t0 = pl.program_id(2) * Q_BLOCK
    rows = NSA_HPG * Q_BLOCK
    q = q_ref[...]
    lane_head = lax.broadcasted_iota(jnp.int32, (Q_BLOCK, GROUP_W), 1) // HEAD_DIM
    qs = jnp.concatenate([jnp.where(lane_head == h, q, jnp.zeros_like(q)) for h in range(NSA_HPG)], axis=0)
    row = lax.broadcasted_iota(jnp.int32, (rows, 1), 0)
    tq = t0 + row % Q_BLOCK
    head = g * NSA_HPG + row // Q_BLOCK
    slope = lax.bitcast_convert_type((127 - (head + 1)) << 23, F32)

    def fold(o):
        acc = jnp.zeros((Q_BLOCK, GROUP_W), F32)
        for h in range(NSA_HPG):
            acc = acc + jnp.where(lane_head == h, o[h * Q_BLOCK:(h + 1) * Q_BLOCK], 0.0)
        return acc

    kc = kc_ref[0, 0, 0]
    nc = kc.shape[0]
    c_last = lax.broadcasted_iota(jnp.int32, (1, nc), 1) * CMP_STRIDE + (CMP_BLOCK - 1)
    dc = tq - c_last
    s = _dot_nt(qs, kc) - slope * dc.astype(F32)
    p_c = _masked_softmax_rows(s, dc >= 0)
    o_c = _dot(p_c.astype(BF16), vc_ref[0, 0, 0])
    p_grp = p_c[0:Q_BLOCK]
    for h in range(1, NSA_HPG):
        p_grp = p_grp + p_c[h * Q_BLOCK:(h + 1) * Q_BLOCK]
    imp = _dot(p_grp, ov_ref[...], precision=HIGHEST)

    bid = lax.broadcasted_iota(jnp.int32, (Q_BLOCK, LANES), 1)
    bidf = bid.astype(F32)
    tq1 = t0 + lax.broadcasted_iota(jnp.int32, (Q_BLOCK, 1), 0)
    cur = tq1 // SLC_BLOCK
    forced = (bid == 0) | (bid == cur) | (bid == cur - 1)
    score = jnp.where(forced, BIG, jnp.where(bid * SLC_BLOCK <= tq1, imp, -BIG))
    score = jnp.where(bid < n_sel, score, REMOVED)
    sel = jnp.zeros((Q_BLOCK, LANES), F32)
    for _ in range(n_top):
        m = jnp.max(score, axis=-1, keepdims=True)
        idx = jnp.min(jnp.where(score == m, bidf, float(LANES)), axis=-1, keepdims=True)
        pick = bidf == idx
        sel = jnp.where(pick, 1.0, sel)
        score = jnp.where(pick, REMOVED, score)
    sel_bf = sel.astype(BF16)

    blocks_per_chunk = SEL_CHUNK // SLC_BLOCK
    erow = lax.broadcasted_iota(jnp.int32, (LANES, SEL_CHUNK), 0)
    ecol = lax.broadcasted_iota(jnp.int32, (LANES, SEL_CHUNK), 1) // SLC_BLOCK
    kcol = lax.broadcasted_iota(jnp.int32, (1, SEL_CHUNK), 1)

    def sel_step(j, carry):
        m, l, acc = carry
        start = pl.multiple_of(j * SEL_CHUNK, SEL_CHUNK)
        k = ks_ref[pl.ds(start, SEL_CHUNK), :]
        v = vs_ref[pl.ds(start, SEL_CHUNK), :]
        ds_ = tq - (start + kcol)
        expand = jnp.where(erow == ecol + j * blocks_per_chunk, 1.0, 0.0).astype(BF16)
        selc = _dot(sel_bf, expand)
        selm = jnp.concatenate([selc] * NSA_HPG, axis=0)
        mask = (selm > 0.5) & (ds_ >= 0)
        sm = jnp.where(mask, _dot_nt(qs, k) - slope * ds_.astype(F32), NEG_INF)
        m_new = jnp.maximum(m, jnp.max(sm, axis=-1, keepdims=True))
        alpha = jnp.exp(m - m_new)
        p = jnp.where(mask, jnp.exp(sm - m_new), 0.0)
        l = alpha * l + jnp.sum(p, axis=-1, keepdims=True)
        acc = alpha * acc + _dot(p.astype(BF16), v)
        return m_new, l, acc

    n_chunks = (t0 + Q_BLOCK + SEL_CHUNK - 1) // SEL_CHUNK
    init = (jnp.full((rows, 1), NEG_INF, F32), jnp.zeros((rows, 1), F32), jnp.zeros((rows, GROUP_W), F32))
    _, l_s, acc_s = lax.fori_loop(0, n_chunks, sel_step, init)
    o_s = acc_s / l_s

    span = WINDOW + Q_BLOCK
    ws = pl.multiple_of(jnp.maximum(t0 - WINDOW, 0), Q_BLOCK)
    kw = kw_ref[pl.ds(ws, span), :]
    vw = vw_ref[pl.ds(ws, span), :]
    dw = tq - (ws + lax.broadcasted_iota(jnp.int32, (1, span), 1))
    s = _dot_nt(qs, kw) - slope * dw.astype(F32)
    p_w = _masked_softmax_rows(s, (dw >= 0) & (dw < WINDOW))
    o_w = _dot(p_w.astype(BF16), vw)

    o_ref[...] = gc_ref[...] * fold(o_c) + gs_ref[...] * fold(o_s) + gw_ref[...] * fold(o_w)


def nsa_attention(qn, kcv, ks, vs, kw, vw, gc, gs, gw, b, s):
    nq_blocks = s // Q_BLOCK
    nc = s // CMP_STRIDE
    n_cmp = nc - CMP_BLOCK // CMP_STRIDE + 1
    n_sel = s // SLC_BLOCK
    n_top = min(SLC_TOPK, n_sel)
    assert n_sel <= LANES and s % SEL_CHUNK == 0 and s >= WINDOW + Q_BLOCK
    c_start = np.arange(n_cmp) * CMP_STRIDE
    s_start = np.arange(n_sel) * SLC_BLOCK
    ov = np.zeros((nc, LANES), np.float32)
    ov[:n_cmp, :n_sel] = np.clip(np.minimum((c_start + CMP_BLOCK)[:, None], s_start[None] + SLC_BLOCK)
                                 - np.maximum(c_start[:, None], s_start[None]), 0, None) / CMP_BLOCK
    qrow = lambda: pl.BlockSpec((Q_BLOCK, GROUP_W), lambda bi, g, i: (bi * nq_blocks + i, g))
    seqkv = lambda: pl.BlockSpec((s, GROUP_W), lambda bi, g, i: (bi, g), pipeline_mode=pl.Buffered(1))
    cmp_spec = lambda kv: pl.BlockSpec((1, 1, 1, nc, GROUP_W), lambda bi, g, i: (kv, bi, g, 0, 0))
    return pl.pallas_call(
        functools.partial(_nsa_attn_kernel, n_sel=n_sel, n_top=n_top, seq=s),
        grid=(b, NSA_GROUPS, nq_blocks),
        in_specs=[qrow(), cmp_spec(0), cmp_spec(1), pl.BlockSpec((nc, LANES), lambda bi, g, i: (0, 0)),
                  seqkv(), seqkv(), seqkv(), seqkv(), qrow(), qrow(), qrow()],
        out_specs=qrow(),
        out_shape=jax.ShapeDtypeStruct((b * s, NSA_GROUPS * GROUP_W), F32),
        compiler_params=_cparams("parallel", "parallel", "arbitrary"),
        name="nsa_attention",
    )(qn, kcv, kcv, jnp.asarray(ov), ks, vs, kw, vw, gc, gs, gw)


def _rwkv_prep_kernel(p_ref, prev_ref, mu_ref, w0_ref, a0_ref, kk_ref, ka_ref, rk_ref,
                      wup_ref, aup_ref, gup_ref, bd_ref, ltri_ref, lones_ref,
                      at_o, bt_o, kt_o, rt_o, v_o, bw_o, kw_o, wc_o, g_o, bonus_o, *, tiles_per_seq):
    p = p_ref[...]
    tm = p.shape[0]
    first = pl.program_id(0) % tiles_per_seq == 0
    last_prev = jnp.where(first, 0.0, prev_ref[7:8, :])
    prev = pltpu.roll(p, 1, 0)
    prev = jnp.where(lax.broadcasted_iota(jnp.int32, (tm, 1), 0) == 0, last_prev, prev)
    pm = p + (prev - p) * mu_ref[...]
    d = RWKV_DIM
    r, k, v = pm[:, :d], pm[:, d:2 * d], pm[:, 2 * d:3 * d]
    lora = pm[:, 3 * d:3 * d + LANES]
    gd = pm[:, 3 * d + LANES:3 * d + 2 * LANES]
    z = -(w0_ref[...] + _dot(jnp.tanh(lora).astype(BF16), wup_ref[...]))
    softplus = jnp.maximum(z, 0.0) + jnp.log(1.0 + jnp.exp(-jnp.abs(z)))
    w = -softplus - 0.5
    a = jax.nn.sigmoid(a0_ref[...] + _dot(lora.astype(BF16), aup_ref[...]))
    g_o[...] = _dot(jax.nn.sigmoid(gd).astype(BF16), gup_ref[...])
    bd = bd_ref[...]
    kkr = k * kk_ref[...]
    kk = kkr / jnp.maximum(jnp.sqrt(_dot(kkr * kkr, bd, precision=HIGHEST)), 1e-12)
    k2 = k * (1.0 + (a - 1.0) * ka_ref[...])
    bonus_o[...] = _dot(r * k2 * rk_ref[...], bd, precision=HIGHEST) * v
    lw = -jnp.exp(w)
    cum = _dot(ltri_ref[...], lw, precision=HIGHEST)
    tot = _dot(lones_ref[...], lw, precision=HIGHEST)
    e_in = jnp.exp(cum)
    e_out = jnp.exp(-cum)
    e_end = jnp.exp(tot - cum)
    at_o[...] = (-kk * jnp.exp(cum - lw)).astype(BF16)
    bt_o[...] = (kk * a * e_out).astype(BF16)
    kt_o[...] = (k2 * e_out).astype(BF16)
    rt_o[...] = (r * e_in).astype(BF16)
    v_o[...] = v.astype(BF16)
    bw_o[...] = (kk * a * e_end).astype(BF16)
    kw_o[...] = (k2 * e_end).astype(BF16)
    wc_o[...] = jnp.exp(tot)


def rwkv_prep(p_rwkv, mu, w0, w_up, a0, a_up, g_up, k_k, k_a, r_k, s, tm):
    t = p_rwkv.shape[0]
    d = RWKV_DIM
    c = RWKV_CHUNK
    wup = jnp.concatenate([w_up, jnp.zeros_like(a_up)], axis=0).astype(BF16)
    aup = jnp.concatenate([jnp.zeros_like(w_up), a_up], axis=0).astype(BF16)
    i = np.arange(tm)
    same = (i[:, None] // c) == (i[None, :] // c)
    ltri = jnp.asarray((same & (i[:, None] >= i[None, :])).astype(np.float32))
    lones = jnp.asarray(same.astype(np.float32))
    row = lambda w: pl.BlockSpec((tm, w), lambda i: (i, 0))
    full = lambda *sh: pl.BlockSpec(sh, lambda i: (0,) * len(sh))
    vec = lambda x: x.reshape(1, -1)
    return pl.pallas_call(
        functools.partial(_rwkv_prep_kernel, tiles_per_seq=s // tm),
        grid=(t // tm,),
        in_specs=[row(RWKV_PROJ),
                  pl.BlockSpec((8, RWKV_PROJ), lambda i: (jnp.maximum(i * (tm // 8) - 1, 0), 0)),
                  full(1, RWKV_PROJ), full(1, d), full(1, d), full(1, d), full(1, d), full(1, d),
                  full(LANES, d), full(LANES, d), full(LANES, d), full(d, d), full(tm, tm), full(tm, tm)],
        out_specs=[row(d)] * 10,
        out_shape=[jax.ShapeDtypeStruct((t, d), BF16)] * 7 + [jax.ShapeDtypeStruct((t, d), F32)] * 3,
        compiler_params=_cparams("parallel"),
        name="rwkv_prep",
    )(p_rwkv, p_rwkv, vec(mu), vec(w0), vec(a0), vec(k_k), vec(k_a), vec(r_k), wup, aup, g_up.astype(BF16),
      _block_diag_ones(d, HEAD_DIM), ltri, lones)


def _rwkv_scan_kernel(at_ref, bt_ref, kt_ref, rt_ref, v_ref, bw_ref, kw_ref, wc_ref, y_ref, st_ref, *, chunks):
    c = RWKV_CHUNK

    @pl.when(pl.program_id(1) == 0)
    def _():
        st_ref[...] = jnp.zeros_like(st_ref)

    ri = lax.broadcasted_iota(jnp.int32, (c, c), 0)
    ci = lax.broadcasted_iota(jnp.int32, (c, c), 1)
    strict = ri > ci
    incl = ri >= ci
    eye = jnp.where(ri == ci, 1.0, 0.0)
    n_double = int(np.log2(c)) - 1

    def chunk_step(j, _):
        r0 = pl.multiple_of(j * c, c)
        for h in range(RWKV_HEADS):
            sl = (0, h, pl.ds(r0, c), slice(None))
            at, bt, kt, rt, v = at_ref[sl], bt_ref[sl], kt_ref[sl], rt_ref[sl], v_ref[sl]
            bw, kw = bw_ref[sl], kw_ref[sl]
            wc = wc_ref[0, h, pl.ds(j, 1), :]
            ar = jnp.concatenate([at, rt], axis=0)
            xb = _dot_nt(ar, bt)
            xk = _dot_nt(ar, kt)
            l_ab = jnp.where(strict, xb[:c], 0.0)
            a_ak = jnp.where(strict, xk[:c], 0.0)
            a_rb = jnp.where(incl, xb[c:], 0.0)
            a_rk = jnp.where(incl, xk[c:], 0.0)
            pw = l_ab
            tinv = eye + l_ab
            for _ in range(n_double):
                pw_b = pw.astype(BF16)
                pw = _dot(pw_b, pw_b)
                tinv = tinv + _dot(tinv.astype(BF16), pw.astype(BF16))
            tinv_b = tinv.astype(BF16)
            ta = _dot(tinv_b, at)
            tr = _dot(tinv_b, _dot(a_ak.astype(BF16), v).astype(BF16))
            st = st_ref[h]
            st_b = st.astype(BF16)
            u = _dot_nt(ta.astype(BF16), st_b) + tr
            u_b = u.astype(BF16)
            y = _dot_nt(rt, st_b) + _dot(a_rb.astype(BF16), u_b) + _dot(a_rk.astype(BF16), v)
            st_ref[h] = st * wc + _dot_tn(jnp.concatenate([u_b, v], axis=0), jnp.concatenate([bw, kw], axis=0))
            mean = jnp.mean(y, axis=-1, keepdims=True)
            yc = y - mean
            var = jnp.mean(yc * yc, axis=-1, keepdims=True)
            y_ref[sl] = yc * lax.rsqrt(var + GN_EPS)
        return 0

    lax.fori_loop(0, chunks, chunk_step, 0)


def rwkv_scan(at, bt, kt, rt, v, bw, kw, wc, ts):
    b, h, s, dh = at.shape
    n_tiles = s // ts
    cpt = ts // RWKV_CHUNK
    seq = lambda: pl.BlockSpec((1, h, ts, dh), lambda bi, i: (bi, 0, i, 0))
    return pl.pallas_call(
        functools.partial(_rwkv_scan_kernel, chunks=cpt),
        grid=(b, n_tiles),
        in_specs=[seq()] * 7 + [pl.BlockSpec((1, h, cpt, dh), lambda bi, i: (bi, 0, i, 0))],
        out_specs=seq(),
        out_shape=jax.ShapeDtypeStruct((b, h, s, dh), F32),
        scratch_shapes=[pltpu.VMEM((h, dh, dh), F32)],
        compiler_params=_cparams("parallel", "arbitrary"),
        name="rwkv_scan",
    )(at, bt, kt, rt, v, bw, kw, wc)


def _out_proj_kernel(x_ref, on_ref, yn_ref, bonus_ref, g_ref, lnw_ref, lnb_ref, wn_ref, wr_ref, o_ref):
    o_rwkv = (yn_ref[...] * lnw_ref[...] + lnb_ref[...] + bonus_ref[...]) * g_ref[...]
    o_ref[...] = (x_ref[...] + _dot(on_ref[...].astype(BF16), wn_ref[...])
                  + _dot(o_rwkv.astype(BF16), wr_ref[...]))


def out_proj(x, o_nsa, yn, bonus, g, ln_w, ln_b, w_out, tm):
    t, d = x.shape
    dn = o_nsa.shape[1]
    dr = yn.shape[1]
    row = lambda w: pl.BlockSpec((tm, w), lambda i: (i, 0))
    full = lambda *s: pl.BlockSpec(s, lambda i: (0,) * len(s))
    return pl.pallas_call(
        _out_proj_kernel,
        grid=(t // tm,),
        in_specs=[row(d), row(dn), row(dr), row(dr), row(dr), full(1, dr), full(1, dr), full(dn, d), full(dr, d)],
        out_specs=row(d),
        out_shape=jax.ShapeDtypeStruct((t, d), F32),
        compiler_params=_cparams("parallel"),
        name="out_proj",
    )(x, o_nsa, yn, bonus, g, ln_w.reshape(1, dr), ln_b.reshape(1, dr),
      w_out[:dn].astype(BF16), w_out[dn:].astype(BF16))


def _cross_attn_kernel(h_ref, g_ref, wq_ref, qg_ref, kv_ref, kg_ref, wo_ref, o_ref):
    h = h_ref[...]
    d = h.shape[1]
    xd = d // X_HEADS
    q = _dot(_rms(h, g_ref[...]).astype(BF16), wq_ref[...])
    kv = kv_ref[0]
    outs = []
    for hd in range(X_HEADS):
        qh = _rms(q[:, hd * xd:(hd + 1) * xd], qg_ref[...]) * (xd ** -0.5)
        kh = _rms(kv[:, hd * xd:(hd + 1) * xd], kg_ref[...])
        vh = kv[:, d + hd * xd:d + (hd + 1) * xd]
        s = _dot_nt(qh.astype(BF16), kh.astype(BF16))
        p = jnp.exp(s - jnp.max(s, axis=-1, keepdims=True))
        p = p / jnp.sum(p, axis=-1, keepdims=True)
        outs.append(_dot(p.astype(BF16), vh.astype(BF16)))
    o = jnp.concatenate(outs, axis=-1)
    o_ref[...] = h + _dot(o.astype(BF16), wo_ref[...])


def cross_attention(h, kv, norm_g, xq_w, xq_g, xk_g, xo_w, b, s, tm):
    t, d = h.shape
    m = kv.shape[1]
    xd = d // X_HEADS
    tiles = s // tm
    full = lambda *sh: pl.BlockSpec(sh, lambda i: (0,) * len(sh))
    return pl.pallas_call(
        _cross_attn_kernel,
        grid=(t // tm,),
        in_specs=[pl.BlockSpec((tm, d), lambda i: (i, 0)), full(1, d), full(d, d), full(1, xd),
                  pl.BlockSpec((1, m, 2 * d), lambda i: (i // tiles, 0, 0)), full(1, xd), full(d, d)],
        out_specs=pl.BlockSpec((tm, d), lambda i: (i, 0)),
        out_shape=jax.ShapeDtypeStruct((t, d), F32),
        compiler_params=_cparams("parallel"),
        name="cross_attention",
    )(h, norm_g.reshape(1, d), xq_w.astype(BF16), xq_g.reshape(1, xd), kv, xk_g.reshape(1, xd),
      xo_w.astype(BF16))


def _router_kernel(h_ref, g_ref, rw_ref, rb_ref, xn_o, idx_o, gate_o):
    xn = _rms(h_ref[...], g_ref[...])
    xn_o[...] = xn.astype(BF16)
    logits = _dot(xn, rw_ref[...], precision=HIGHEST) + rb_ref[...]
    tm = logits.shape[0]
    lane = lax.broadcasted_iota(jnp.int32, (tm, LANES), 1)
    lanef = lane.astype(F32)
    logits = jnp.where(lane < N_EXPERTS, logits, REMOVED)
    idx_acc = jnp.zeros((tm, LANES), F32)
    val_acc = jnp.zeros((tm, LANES), F32)
    top = None
    for k in range(TOP_K):
        m = jnp.max(logits, axis=-1, keepdims=True)
        idx = jnp.min(jnp.where(logits == m, lanef, float(LANES)), axis=-1, keepdims=True)
        logits = jnp.where(lanef == idx, REMOVED, logits)
        top = m if top is None else top
        idx_acc = jnp.where(lane == k, idx, idx_acc)
        val_acc = jnp.where(lane == k, jnp.exp(m - top), val_acc)
    idx_o[...] = idx_acc.astype(jnp.int32)
    gate_o[...] = val_acc / jnp.sum(val_acc, axis=-1, keepdims=True)


def moe_router(h, norm_g, router_w, router_b, tm):
    t, d = h.shape
    rw = jnp.zeros((d, LANES), F32).at[:, :N_EXPERTS].set(router_w)
    rb = jnp.zeros((1, LANES), F32).at[0, :N_EXPERTS].set(router_b)
    row = lambda w: pl.BlockSpec((tm, w), lambda i: (i, 0))
    full = lambda *s: pl.BlockSpec(s, lambda i: (0,) * len(s))
    return pl.pallas_call(
        _router_kernel,
        grid=(t // tm,),
        in_specs=[row(d), full(1, d), full(d, LANES), full(1, LANES)],
        out_specs=[row(d), row(LANES), row(LANES)],
        out_shape=[jax.ShapeDtypeStruct((t, d), BF16), jax.ShapeDtypeStruct((t, LANES), jnp.int32),
                   jax.ShapeDtypeStruct((t, LANES), F32)],
        compiler_params=_cparams("parallel"),
        name="moe_router",
    )(h, norm_g.reshape(1, d), rw, rb)


def _expert_kernel(blk_e_ref, n_used_ref, x_ref, w1g_ref, w1l_ref, b1g_ref, b1l_ref, w2_ref, b2_ref, rw_ref, o_ref):
    i = pl.program_id(0)

    @pl.when(i < n_used_ref[0])
    def _():
        x = x_ref[...]
        hg = jnp.minimum(_dot(x, w1g_ref[0]) + b1g_ref[0], SWIGLU_LIMIT)
        hl = jnp.clip(_dot(x, w1l_ref[0]) + b1l_ref[0], -SWIGLU_LIMIT, SWIGLU_LIMIT)
        act = hg * jax.nn.sigmoid(SWIGLU_ALPHA * hg) * (hl + 1.0)
        y = _dot(act.astype(BF16), w2_ref[0].astype(BF16)) + b2_ref[0]
        o_ref[...] = y * rw_ref[...]

    @pl.when(i >= n_used_ref[0])
    def _():
        o_ref[...] = jnp.zeros_like(o_ref)


def moe_experts(xs, row_w, blk_e, n_used, w1g, w1l, b1g, b1l, w2, b2):
    r, d = xs.shape
    f = w1g.shape[2]
    m = MOE_ROW_BLOCK
    ex = lambda *s: pl.BlockSpec((1,) + s, lambda i, be, nu: (be[i], 0, 0))
    grid_spec = pltpu.PrefetchScalarGridSpec(
        num_scalar_prefetch=2,
        grid=(r // m,),
        in_specs=[pl.BlockSpec((m, d), lambda i, be, nu: (i, 0)),
                  ex(d, f), ex(d, f), ex(1, f), ex(1, f), ex(f, d), ex(1, d),
                  pl.BlockSpec((m, 1), lambda i, be, nu: (i, 0))],
        out_specs=pl.BlockSpec((m, d), lambda i, be, nu: (i, 0)),
    )
    return pl.pallas_call(
        _expert_kernel,
        grid_spec=grid_spec,
        out_shape=jax.ShapeDtypeStruct((r, d), F32),
        compiler_params=_cparams("arbitrary"),
        name="moe_experts",
    )(blk_e, n_used, xs, w1g, w1l, b1g, b1l, w2, b2, row_w)


def _layer(x, mem, norm_mix_g, w_in, q_norm_g, k_cmp_norm_g, k_slc_norm_g, k_win_norm_g,
           cmp_pe_k, cmp_pe_v, cmp_k_w1, cmp_k_w2, cmp_v_w1, cmp_v_w2,
           rwkv_mu, rwkv_w0, rwkv_w_up, rwkv_a0, rwkv_a_up, rwkv_g_up, rwkv_k_k, rwkv_k_a,
           rwkv_r_k, rwkv_ln_w, rwkv_ln_b, w_out,
           norm_x_g, norm_mem_g, xq_w, xk_w, xv_w, xq_norm_g, xk_norm_g, xo_w,
           norm_ffn_g, router_w, router_b, mlp1_w, mlp1_b, mlp2_w, mlp2_b):
    b, s, d = x.shape
    t = b * s
    tm = 512
    xt = x.reshape(t, d)

    w_nsa = jnp.pad(w_in[:, :NSA_PROJ], ((0, 0), (0, NSA_PROJ_PAD - NSA_PROJ)))
    p_nsa = norm_matmul(xt, norm_mix_g, w_nsa, tm)
    p_rwkv = norm_matmul(xt, norm_mix_g, w_in[:, NSA_PROJ:], tm)

    nq = NSA_HEADS * HEAD_DIM
    gw = NSA_GROUPS * HEAD_DIM
    qn, ks, vs, kw, vw, gc, gs, gwin = nsa_prep(p_nsa, q_norm_g, k_slc_norm_g, k_win_norm_g, tm)
    kcv = nsa_compress(p_nsa[:, nq:nq + gw], p_nsa[:, nq + gw:nq + 2 * gw], cmp_pe_k, cmp_pe_v,
                       cmp_k_w1, cmp_k_w2, cmp_v_w1, cmp_v_w2, k_cmp_norm_g, b, s)
    o_nsa = nsa_attention(qn, kcv, ks, vs, kw, vw, gc, gs, gwin, b, s)

    outs = rwkv_prep(p_rwkv, rwkv_mu, rwkv_w0, rwkv_w_up, rwkv_a0, rwkv_a_up, rwkv_g_up,
                     rwkv_k_k, rwkv_k_a, rwkv_r_k, s, tm)
    *scan_in, wc, g_gate, bonus = outs
    heads = lambda a: a.reshape(b, s, RWKV_HEADS, HEAD_DIM).transpose(0, 2, 1, 3)
    wc_h = heads(wc)[:, :, ::RWKV_CHUNK]
    yn = rwkv_scan(*[heads(a) for a in scan_in], wc_h, ts=min(s, 512))
    yn = yn.transpose(0, 2, 1, 3).reshape(t, RWKV_DIM)

    h1 = out_proj(xt, o_nsa, yn, bonus, g_gate, rwkv_ln_w, rwkv_ln_b, w_out, tm)
    m = mem.shape[1]
    kv = norm_matmul(mem.reshape(b * m, d), norm_mem_g, jnp.concatenate([xk_w, xv_w], axis=1), m)
    h2 = cross_attention(h1, kv.reshape(b, m, 2 * d), norm_x_g, xq_w, xq_norm_g, xk_norm_g, xo_w, b, s, tm)

    xn, top_i, gate = moe_router(h2, norm_ffn_g, router_w, router_b, tm)
    top_i = top_i[:, :TOP_K]
    gate = gate[:, :TOP_K]
    a = t * TOP_K
    mb = MOE_ROW_BLOCK
    e_flat = top_i.reshape(a)
    order = jnp.argsort(e_flat, stable=True)
    e_sorted = e_flat[order]
    counts = jnp.bincount(e_flat, length=N_EXPERTS)
    starts = jnp.cumsum(counts) - counts
    padded = (counts + mb - 1) // mb * mb
    pends = jnp.cumsum(padded)
    pstarts = pends - padded
    dest = (pstarts[e_sorted] + jnp.arange(a) - starts[e_sorted]).astype(jnp.int32)
    n_blocks = -(-a // mb) + N_EXPERTS
    r = n_blocks * mb
    row_tok = jnp.full((r,), t, jnp.int32).at[dest].set((order // TOP_K).astype(jnp.int32))
    row_w = jnp.zeros((r,), F32).at[dest].set(gate.reshape(a)[order])
    blk_e = jnp.minimum(jnp.searchsorted(pends, jnp.arange(n_blocks) * mb, side='right'),
                        N_EXPERTS - 1).astype(jnp.int32)
    n_used = (pends[-1] // mb).astype(jnp.int32).reshape(1)
    xpad = jnp.concatenate([xn, jnp.zeros((1, d), xn.dtype)], axis=0)
    xs = xpad[row_tok]
    f = mlp1_w.shape[2] // 2
    w1 = mlp1_w.reshape(N_EXPERTS, d, f, 2)
    b1 = mlp1_b.reshape(N_EXPERTS, 1, f, 2)
    ys = moe_experts(xs, row_w.reshape(r, 1), blk_e, n_used,
                     w1[..., 0].astype(BF16), w1[..., 1].astype(BF16), b1[..., 0], b1[..., 1],
                     mlp2_w, mlp2_b.reshape(N_EXPERTS, 1, d))
    pos = jnp.zeros((a,), jnp.int32).at[order].set(dest).reshape(t, TOP_K)
    out = h2 + jnp.sum(ys[pos], axis=1)
    return out.reshape(b, s, d)


def kernel(x, mem, norm_mix_g, w_in, q_norm_g, k_cmp_norm_g, k_slc_norm_g, k_win_norm_g, cmp_pe_k, cmp_pe_v, cmp_k_w1, cmp_k_w2, cmp_v_w1, cmp_v_w2, rwkv_mu, rwkv_w0, rwkv_w_up, rwkv_a0, rwkv_a_up, rwkv_g_up, rwkv_k_k, rwkv_k_a, rwkv_r_k, rwkv_ln_w, rwkv_ln_b, w_out, norm_x_g, norm_mem_g, xq_w, xk_w, xv_w, xq_norm_g, xk_norm_g, xo_w, norm_ffn_g, router_w, router_b, mlp1_w, mlp1_b, mlp2_w, mlp2_b):
    params = (norm_mix_g, w_in, q_norm_g, k_cmp_norm_g, k_slc_norm_g, k_win_norm_g, cmp_pe_k, cmp_pe_v,
              cmp_k_w1, cmp_k_w2, cmp_v_w1, cmp_v_w2, rwkv_mu, rwkv_w0, rwkv_w_up, rwkv_a0, rwkv_a_up,
              rwkv_g_up, rwkv_k_k, rwkv_k_a, rwkv_r_k, rwkv_ln_w, rwkv_ln_b, w_out, norm_x_g, norm_mem_g,
              xq_w, xk_w, xv_w, xq_norm_g, xk_norm_g, xo_w, norm_ffn_g, router_w, router_b,
              mlp1_w, mlp1_b, mlp2_w, mlp2_b)
    h = x
    for layer in range(norm_mix_g.shape[0]):
        h = _layer(h, mem, *[prm[layer] for prm in params])
    return h
```

```python
import functools

import numpy as np
import jax
import jax.numpy as jnp
from jax import lax
from jax.experimental import pallas as pl
from jax.experimental.pallas import tpu as pltpu

F32 = jnp.float32
BF16 = jnp.bfloat16
HIGHEST = lax.Precision.HIGHEST

V7X_VMEM_BYTES = 64 * 1024 * 1024
VMEM_LIMIT = V7X_VMEM_BYTES * 3 // 4

HEAD_DIM = 64
NSA_HEADS = 8
NSA_GROUPS = 2
NSA_HPG = NSA_HEADS // NSA_GROUPS
GROUP_W = NSA_HPG * HEAD_DIM
CMP_BLOCK = 32
CMP_STRIDE = 16
SLC_BLOCK = 64
SLC_TOPK = 16
WINDOW = 512
Q_BLOCK = 128
SEL_CHUNK = 512
RWKV_HEADS = 8
RWKV_DIM = RWKV_HEADS * HEAD_DIM
RWKV_CHUNK = 64
GN_EPS = HEAD_DIM * 1e-5
X_HEADS = 4
N_EXPERTS = 32
TOP_K = 4
SWIGLU_LIMIT = 7.0
SWIGLU_ALPHA = 1.702
MOE_ROW_BLOCK = 256
RMS_EPS = 1e-6
NEG_INF = -1e30
BIG = 1e9
REMOVED = -3e38
LANES = 128

NSA_PROJ = NSA_HEADS * HEAD_DIM + 6 * NSA_GROUPS * HEAD_DIM + NSA_HEADS * 3
NSA_PROJ_PAD = -(-NSA_PROJ // LANES) * LANES
RWKV_PROJ = 3 * RWKV_DIM + 64 + 64 + 128


def _cparams(*sem):
    return pltpu.CompilerParams(dimension_semantics=sem, vmem_limit_bytes=VMEM_LIMIT)


def _dot(a, b, **kw):
    return jnp.dot(a, b, preferred_element_type=F32, **kw)


def _dot_nt(a, b, **kw):
    return lax.dot_general(a, b, (((1,), (1,)), ((), ())), preferred_element_type=F32, **kw)


def _dot_tn(a, b, **kw):
    return lax.dot_general(a, b, (((0,), (0,)), ((), ())), preferred_element_type=F32, **kw)


def _rms(x, g):
    return x * lax.rsqrt(jnp.mean(x * x, axis=-1, keepdims=True) + RMS_EPS) * g


def _block_diag_ones(n, blk, scale=1.0):
    i = np.arange(n)
    return jnp.asarray(((i[:, None] // blk) == (i[None, :] // blk)).astype(np.float32) * scale)


def _norm_matmul_kernel(x_ref, g_ref, w_ref, o_ref):
    xn = _rms(x_ref[...], g_ref[...]).astype(BF16)
    o_ref[...] = _dot(xn, w_ref[...])


def norm_matmul(x, g, w, tm):
    m, d = x.shape
    n = w.shape[1]
    return pl.pallas_call(
        _norm_matmul_kernel,
        grid=(m // tm,),
        in_specs=[pl.BlockSpec((tm, d), lambda i: (i, 0)),
                  pl.BlockSpec((1, d), lambda i: (0, 0)),
                  pl.BlockSpec((d, n), lambda i: (0, 0))],
        out_specs=pl.BlockSpec((tm, n), lambda i: (i, 0)),
        out_shape=jax.ShapeDtypeStruct((m, n), F32),
        compiler_params=_cparams("parallel"),
        name="norm_matmul",
    )(x, g.reshape(1, d), w.astype(BF16))


def _nsa_prep_kernel(p_ref, qg_ref, ksg_ref, kwg_ref, bdq_ref, bdk_ref, ek_ref, eg_ref,
                     q_o, ks_o, vs_o, kw_o, vw_o, gc_o, gs_o, gw_o):
    p = p_ref[...]
    nq = NSA_HEADS * HEAD_DIM
    gw = NSA_GROUPS * HEAD_DIM
    q = p[:, :nq]
    msq = _dot(q * q, bdq_ref[...], precision=HIGHEST)
    q_o[...] = (q * lax.rsqrt(msq + RMS_EPS) * qg_ref[...] * (HEAD_DIM ** -0.5)).astype(BF16)

    def seg(k):
        return p[:, nq + k * gw: nq + (k + 1) * gw]

    def head_norm(t, g):
        ms = _dot(t * t, bdk_ref[...], precision=HIGHEST)
        return t * lax.rsqrt(ms + RMS_EPS) * g

    ek = ek_ref[...]
    widen = lambda t: _dot(t.astype(BF16), ek).astype(BF16)
    ks_o[...] = widen(head_norm(seg(2), ksg_ref[...]))
    vs_o[...] = widen(seg(3))
    kw_o[...] = widen(head_norm(seg(4), kwg_ref[...]))
    vw_o[...] = widen(seg(5))
    gates = jax.nn.sigmoid(p[:, nq + 6 * gw: nq + 6 * gw + LANES])
    for br, o in enumerate((gc_o, gs_o, gw_o)):
        o[...] = _dot(gates, eg_ref[br], precision=HIGHEST)


def nsa_prep(p_nsa, q_g, ks_g, kw_g, tm):
    t = p_nsa.shape[0]
    nq = NSA_HEADS * HEAD_DIM
    gw = NSA_GROUPS * HEAD_DIM
    wide = NSA_GROUPS * GROUP_W
    ek = np.zeros((gw, wide), np.float32)
    for g in range(NSA_GROUPS):
        for h in range(NSA_HPG):
            for d in range(HEAD_DIM):
                ek[g * HEAD_DIM + d, g * GROUP_W + h * HEAD_DIM + d] = 1.0
    eg = np.zeros((3, LANES, nq), np.float32)
    for br in range(3):
        for h in range(NSA_HEADS):
            eg[br, h * 3 + br, h * HEAD_DIM:(h + 1) * HEAD_DIM] = 1.0
    tile = lambda v, n: jnp.tile(v.reshape(1, HEAD_DIM), (1, n))
    row = lambda w: pl.BlockSpec((tm, w), lambda i: (i, 0))
    full = lambda *s: pl.BlockSpec(s, lambda i: (0,) * len(s))
    return pl.pallas_call(
        _nsa_prep_kernel,
        grid=(t // tm,),
        in_specs=[row(NSA_PROJ_PAD), full(1, nq), full(1, gw), full(1, gw), full(nq, nq), full(gw, gw),
                  full(gw, wide), full(3, LANES, nq)],
        out_specs=[row(nq)] + [row(wide)] * 4 + [row(nq)] * 3,
        out_shape=[jax.ShapeDtypeStruct((t, nq), BF16)] + [jax.ShapeDtypeStruct((t, wide), BF16)] * 4
                  + [jax.ShapeDtypeStruct((t, nq), F32)] * 3,
        compiler_params=_cparams("parallel"),
        name="nsa_prep",
    )(p_nsa, tile(q_g, NSA_HEADS), tile(ks_g, NSA_GROUPS), tile(kw_g, NSA_GROUPS),
      _block_diag_ones(nq, HEAD_DIM, 1.0 / HEAD_DIM), _block_diag_ones(gw, HEAD_DIM, 1.0 / HEAD_DIM),
      jnp.asarray(ek, BF16), jnp.asarray(eg))


def _compress_kernel(ch_ref, pe_ref, w1_ref, w2_ref, g_ref, e_ref, o_ref):
    ch = ch_ref[0, 0, 0]
    nc = ch.shape[0]
    half = CMP_STRIDE * HEAD_DIM
    nxt = pltpu.roll(ch, nc - 1, 0)
    w1 = w1_ref[0]
    h1 = (_dot(ch, w1[:half], precision=HIGHEST) + _dot(nxt, w1[half:], precision=HIGHEST)
          + _dot(pe_ref[0], w1, precision=HIGHEST))
    out = _dot(jax.nn.silu(h1), w2_ref[0], precision=HIGHEST)
    out = jnp.where(pl.program_id(0) == 0, _rms(out, g_ref[...]), out)
    o_ref[0, 0, 0] = _dot(out.astype(BF16), e_ref[...]).astype(BF16)


def nsa_compress(kc, vc, pe_k, pe_v, kw1, kw2, vw1, vw2, kc_g, b, s):
    nc = s // CMP_STRIDE
    half = CMP_STRIDE * HEAD_DIM

    def chunks(t):
        return t.reshape(b, nc, CMP_STRIDE, NSA_GROUPS, HEAD_DIM).transpose(0, 3, 1, 2, 4).reshape(
            b, NSA_GROUPS, nc, half)

    ch = jnp.stack([chunks(kc), chunks(vc)])
    pe = jnp.stack([pe_k.reshape(1, 2 * half), pe_v.reshape(1, 2 * half)])
    e = np.zeros((HEAD_DIM, GROUP_W), np.float32)
    for h in range(NSA_HPG):
        e[np.arange(HEAD_DIM), h * HEAD_DIM + np.arange(HEAD_DIM)] = 1.0
    return pl.pallas_call(
        _compress_kernel,
        grid=(2, b, NSA_GROUPS),
        in_specs=[pl.BlockSpec((1, 1, 1, nc, half), lambda kv, bi, g: (kv, bi, g, 0, 0)),
                  pl.BlockSpec((1, 1, 2 * half), lambda kv, bi, g: (kv, 0, 0)),
                  pl.BlockSpec((1, 2 * half, HEAD_DIM), lambda kv, bi, g: (kv, 0, 0)),
                  pl.BlockSpec((1, HEAD_DIM, HEAD_DIM), lambda kv, bi, g: (kv, 0, 0)),
                  pl.BlockSpec((1, HEAD_DIM), lambda kv, bi, g: (0, 0)),
                  pl.BlockSpec((HEAD_DIM, GROUP_W), lambda kv, bi, g: (0, 0))],
        out_specs=pl.BlockSpec((1, 1, 1, nc, GROUP_W), lambda kv, bi, g: (kv, bi, g, 0, 0)),
        out_shape=jax.ShapeDtypeStruct((2, b, NSA_GROUPS, nc, GROUP_W), BF16),
        compiler_params=_cparams("parallel", "parallel", "parallel"),
        name="nsa_compress",
    )(ch, pe, jnp.stack([kw1, vw1]), jnp.stack([kw2, vw2]), kc_g.reshape(1, HEAD_DIM), jnp.asarray(e, BF16))


def _masked_softmax_rows(s, mask):
    sm = jnp.where(mask, s, NEG_INF)
    m = jnp.max(sm, axis=-1, keepdims=True)
    p = jnp.where(mask, jnp.exp(sm - m), 0.0)
    l = jnp.sum(p, axis=-1, keepdims=True)
    return p / jnp.where(l > 0.0, l, 1.0)


def _nsa_attn_kernel(q_ref, kc_ref, vc_ref, ov_ref, ks_ref, vs_ref, kw_ref, vw_ref,
                     gc_ref, gs_ref, gw_ref, o_ref, *, n_sel, n_top, seq):
    g = pl.program_id(1)
    t0 = pl.program_id(2) * Q_BLOCK
    rows = NSA_HPG * Q_BLOCK
    q = q_ref[...]
    lane_head = lax.broadcasted_iota(jnp.int32, (Q_BLOCK, GROUP_W), 1) // HEAD_DIM
    qs = jnp.concatenate([jnp.where(lane_head == h, q, jnp.zeros_like(q)) for h in range(NSA_HPG)], axis=0)
    row = lax.broadcasted_iota(jnp.int32, (rows, 1), 0)
    tq = t0 + row % Q_BLOCK
    head = g * NSA_HPG + row // Q_BLOCK
    slope = lax.bitcast_convert_type((127 - (head + 1)) << 23, F32)

    def fold(o):
        acc = jnp.zeros((Q_BLOCK, GROUP_W), F32)
        for h in range(NSA_HPG):
            acc = acc + jnp.where(lane_head == h, o[h * Q_BLOCK:(h + 1) * Q_BLOCK], 0.0)
        return acc

    kc = kc_ref[0, 0, 0]
    nc = kc.shape[0]
    c_last = lax.broadcasted_iota(jnp.int32, (1, nc), 1) * CMP_STRIDE + (CMP_BLOCK - 1)
    dc = tq - c_last
    s = _dot_nt(qs, kc) - slope * dc.astype(F32)
    p_c = _masked_softmax_rows(s, dc >= 0)
    o_c = _dot(p_c.astype(BF16), vc_ref[0, 0, 0])
    p_grp = p_c[0:Q_BLOCK]
    for h in range(1, NSA_HPG):
        p_grp = p_grp + p_c[h * Q_BLOCK:(h + 1) * Q_BLOCK]
    imp = _dot(p_grp, ov_ref[...], precision=HIGHEST)

    bid = lax.broadcasted_iota(jnp.int32, (Q_BLOCK, LANES), 1)
    bidf = bid.astype(F32)
    tq1 = t0 + lax.broadcasted_iota(jnp.int32, (Q_BLOCK, 1), 0)
    cur = tq1 // SLC_BLOCK
    forced = (bid == 0) | (bid == cur) | (bid == cur - 1)
    score = jnp.where(forced, BIG, jnp.where(bid * SLC_BLOCK <= tq1, imp, -BIG))
    score = jnp.where(bid < n_sel, score, REMOVED)
    sel = jnp.zeros((Q_BLOCK, LANES), F32)
    for _ in range(n_top):
        m = jnp.max(score, axis=-1, keepdims=True)
        idx = jnp.min(jnp.where(score == m, bidf, float(LANES)), axis=-1, keepdims=True)
        pick = bidf == idx
        sel = jnp.where(pick, 1.0, sel)
        score = jnp.where(pick, REMOVED, score)
    sel_bf = sel.astype(BF16)

    blocks_per_chunk = SEL_CHUNK // SLC_BLOCK
    erow = lax.broadcasted_iota(jnp.int32, (LANES, SEL_CHUNK), 0)
    ecol = lax.broadcasted_iota(jnp.int32, (LANES, SEL_CHUNK), 1) // SLC_BLOCK
    kcol = lax.broadcasted_iota(jnp.int32, (1, SEL_CHUNK), 1)

    def sel_step(j, carry):
        m, l, acc = carry
        start = pl.multiple_of(j * SEL_CHUNK, SEL_CHUNK)
        k = ks_ref[pl.ds(start, SEL_CHUNK), :]
        v = vs_ref[pl.ds(start, SEL_CHUNK), :]
        ds_ = tq - (start + kcol)
        expand = jnp.where(erow == ecol + j * blocks_per_chunk, 1.0, 0.0).astype(BF16)
        selc = _dot(sel_bf, expand)
        selm = jnp.concatenate([selc] * NSA_HPG, axis=0)
        mask = (selm > 0.5) & (ds_ >= 0)
        sm = jnp.where(mask, _dot_nt(qs, k) - slope * ds_.astype(F32), NEG_INF)
        m_new = jnp.maximum(m, jnp.max(sm, axis=-1, keepdims=True))
        alpha = jnp.exp(m - m_new)
        p = jnp.where(mask, jnp.exp(sm - m_new), 0.0)
        l = alpha * l + jnp.sum(p, axis=-1, keepdims=True)
        acc = alpha * acc + _dot(p.astype(BF16), v)
        return m_new, l, acc

    n_chunks = (t0 + Q_BLOCK + SEL_CHUNK - 1) // SEL_CHUNK
    init = (jnp.full((rows, 1), NEG_INF, F32), jnp.zeros((rows, 1), F32), jnp.zeros((rows, GROUP_W), F32))
    _, l_s, acc_s = lax.fori_loop(0, n_chunks, sel_step, init)
    o_s = acc_s / l_s

    span = WINDOW + Q_BLOCK
    ws = pl.multiple_of(jnp.maximum(t0 - WINDOW, 0), Q_BLOCK)
    kw = kw_ref[pl.ds(ws, span), :]
    vw = vw_ref[pl.ds(ws, span), :]
    dw = tq - (ws + lax.broadcasted_iota(jnp.int32, (1, span), 1))
    s = _dot_nt(qs, kw) - slope * dw.astype(F32)
    p_w = _masked_softmax_rows(s, (dw >= 0) & (dw < WINDOW))
    o_w = _dot(p_w.astype(BF16), vw)

    o_ref[...] = gc_ref[...] * fold(o_c) + gs_ref[...] * fold(o_s) + gw_ref[...] * fold(o_w)


def nsa_attention(qn, kcv, ks, vs, kw, vw, gc, gs, gw, b, s):
    nq_blocks = s // Q_BLOCK
    nc = s // CMP_STRIDE
    n_cmp = nc - CMP_BLOCK // CMP_STRIDE + 1
    n_sel = s // SLC_BLOCK
    n_top = min(SLC_TOPK, n_sel)
    assert n_sel <= LANES and s % SEL_CHUNK == 0 and s >= WINDOW + Q_BLOCK
    c_start = np.arange(n_cmp) * CMP_STRIDE
    s_start = np.arange(n_sel) * SLC_BLOCK
    ov = np.zeros((nc, LANES), np.float32)
    ov[:n_cmp, :n_sel] = np.clip(np.minimum((c_start + CMP_BLOCK)[:, None], s_start[None] + SLC_BLOCK)
                                 - np.maximum(c_start[:, None], s_start[None]), 0, None) / CMP_BLOCK
    qrow = lambda: pl.BlockSpec((Q_BLOCK, GROUP_W), lambda bi, g, i: (bi * nq_blocks + i, g))
    seqkv = lambda: pl.BlockSpec((s, GROUP_W), lambda bi, g, i: (bi, g), pipeline_mode=pl.Buffered(1))
    cmp_spec = lambda kv: pl.BlockSpec((1, 1, 1, nc, GROUP_W), lambda bi, g, i: (kv, bi, g, 0, 0))
    return pl.pallas_call(
        functools.partial(_nsa_attn_kernel, n_sel=n_sel, n_top=n_top, seq=s),
        grid=(b, NSA_GROUPS, nq_blocks),
        in_specs=[qrow(), cmp_spec(0), cmp_spec(1), pl.BlockSpec((nc, LANES), lambda bi, g, i: (0, 0)),
                  seqkv(), seqkv(), seqkv(), seqkv(), qrow(), qrow(), qrow()],
        out_specs=qrow(),
        out_shape=jax.ShapeDtypeStruct((b * s, NSA_GROUPS * GROUP_W), F32),
        compiler_params=_cparams("parallel", "parallel", "arbitrary"),
        name="nsa_attention",
    )(qn, kcv, kcv, jnp.asarray(ov), ks, vs, kw, vw, gc, gs, gw)


def _rwkv_prep_kernel(p_ref, prev_ref, mu_ref, w0_ref, a0_ref, kk_ref, ka_ref, rk_ref,
                      wup_ref, aup_ref, gup_ref, bd_ref, ltri_ref, lones_ref, csum_ref,
                      at_o, bt_o, kt_o, rt_o, v_o, bw_o, kw_o, wc_o, g_o, bonus_o, *, tiles_per_seq):
    p = p_ref[...]
    tm = p.shape[0]
    first = pl.program_id(0) % tiles_per_seq == 0
    last_prev = jnp.where(first, 0.0, prev_ref[7:8, :])
    prev = pltpu.roll(p, 1, 0)
    prev = jnp.where(lax.broadcasted_iota(jnp.int32, (tm, 1), 0) == 0, last_prev, prev)
    pm = p + (prev - p) * mu_ref[...]
    d = RWKV_DIM
    r, k, v = pm[:, :d], pm[:, d:2 * d], pm[:, 2 * d:3 * d]
    lora = pm[:, 3 * d:3 * d + LANES]
    gd = pm[:, 3 * d + LANES:3 * d + 2 * LANES]
    z = -(w0_ref[...] + _dot(jnp.tanh(lora).astype(BF16), wup_ref[...]))
    softplus = jnp.maximum(z, 0.0) + jnp.log(1.0 + jnp.exp(-jnp.abs(z)))
    w = -softplus - 0.5
    a = jax.nn.sigmoid(a0_ref[...] + _dot(lora.astype(BF16), aup_ref[...]))
    g_o[...] = _dot(jax.nn.sigmoid(gd).astype(BF16), gup_ref[...])
    bd = bd_ref[...]
    kkr = k * kk_ref[...]
    kk = kkr / jnp.maximum(jnp.sqrt(_dot(kkr * kkr, bd, precision=HIGHEST)), 1e-12)
    k2 = k * (1.0 + (a - 1.0) * ka_ref[...])
    bonus_o[...] = _dot(r * k2 * rk_ref[...], bd, precision=HIGHEST) * v
    lw = -jnp.exp(w)
    cum = _dot(ltri_ref[...], lw, precision=HIGHEST)
    tot = _dot(lones_ref[...], lw, precision=HIGHEST)
    e_in = jnp.exp(cum)
    e_out = jnp.exp(-cum)
    e_end = jnp.exp(tot - cum)

    def put_heads(o, val):
        for h in range(RWKV_HEADS):
            o[0, h] = val[:, h * HEAD_DIM:(h + 1) * HEAD_DIM].astype(o.dtype)

    put_heads(at_o, -kk * jnp.exp(cum - lw))
    put_heads(bt_o, kk * a * e_out)
    put_heads(kt_o, k2 * e_out)
    put_heads(rt_o, r * e_in)
    put_heads(v_o, v)
    put_heads(bw_o, kk * a * e_end)
    put_heads(kw_o, k2 * e_end)
    put_heads(wc_o, jnp.exp(_dot(csum_ref[...], lw, precision=HIGHEST)))


def rwkv_prep(p_rwkv, mu, w0, w_up, a0, a_up, g_up, k_k, k_a, r_k, b, s, tm):
    t = p_rwkv.shape[0]
    d = RWKV_DIM
    c = RWKV_CHUNK
    tps = s // tm
    cpt = tm // c
    wup = jnp.concatenate([w_up, jnp.zeros_like(a_up)], axis=0).astype(BF16)
    aup = jnp.concatenate([jnp.zeros_like(w_up), a_up], axis=0).astype(BF16)
    i = np.arange(tm)
    same = (i[:, None] // c) == (i[None, :] // c)
    ltri = jnp.asarray((same & (i[:, None] >= i[None, :])).astype(np.float32))
    lones = jnp.asarray(same.astype(np.float32))
    csum = jnp.asarray((np.arange(cpt)[:, None] == (i[None, :] // c)).astype(np.float32))
    row = lambda w: pl.BlockSpec((tm, w), lambda i: (i, 0))
    full = lambda *sh: pl.BlockSpec(sh, lambda i: (0,) * len(sh))
    heads = lambda n: pl.BlockSpec((1, RWKV_HEADS, n, HEAD_DIM), lambda i: (i // tps, 0, i % tps, 0))
    hshape = lambda n, dt: jax.ShapeDtypeStruct((b, RWKV_HEADS, n, HEAD_DIM), dt)
    vec = lambda x: x.reshape(1, -1)
    return pl.pallas_call(
        functools.partial(_rwkv_prep_kernel, tiles_per_seq=tps),
        grid=(t // tm,),
        in_specs=[row(RWKV_PROJ),
                  pl.BlockSpec((8, RWKV_PROJ), lambda i: (jnp.maximum(i * (tm // 8) - 1, 0), 0)),
                  full(1, RWKV_PROJ), full(1, d), full(1, d), full(1, d), full(1, d), full(1, d),
                  full(LANES, d), full(LANES, d), full(LANES, d), full(d, d), full(tm, tm), full(tm, tm),
                  full(cpt, tm)],
        out_specs=[heads(tm)] * 7 + [heads(cpt), row(d), row(d)],
        out_shape=[hshape(s, BF16)] * 7 + [hshape(s // c, F32)] + [jax.ShapeDtypeStruct((t, d), F32)] * 2,
        compiler_params=_cparams("parallel"),
        name="rwkv_prep",
    )(p_rwkv, p_rwkv, vec(mu), vec(w0), vec(a0), vec(k_k), vec(k_a), vec(r_k), wup, aup, g_up.astype(BF16),
      _block_diag_ones(d, HEAD_DIM), ltri, lones, csum)


def _bdot(a, b):
    return lax.dot_general(a, b, (((2,), (1,)), ((0,), (0,))), preferred_element_type=F32)


def _bdot_nt(a, b):
    return lax.dot_general(a, b, (((2,), (2,)), ((0,), (0,))), preferred_element_type=F32)


def _bdot_tn(a, b):
    return lax.dot_general(a, b, (((1,), (1,)), ((0,), (0,))), preferred_element_type=F32)


def _rwkv_intra_kernel(at_ref, bt_ref, kt_ref, rt_ref, v_ref, ta_o, tr_o, arb_o, yv_o):
    c = RWKV_CHUNK
    _, nh, ts, dh = at_ref.shape
    n = nh * (ts // c)
    chunked = lambda ref: ref[0].reshape(n, c, dh)
    at, bt, kt, rt, v = (chunked(r) for r in (at_ref, bt_ref, kt_ref, rt_ref, v_ref))
    ri = lax.broadcasted_iota(jnp.int32, (1, c, c), 1)
    ci = lax.broadcasted_iota(jnp.int32, (1, c, c), 2)
    strict = ri > ci
    incl = ri >= ci
    ar = jnp.concatenate([at, rt], axis=1)
    xb = _bdot_nt(ar, bt)
    xk = _bdot_nt(ar, kt)
    l_ab = jnp.where(strict, xb[:, :c], 0.0)
    a_ak = jnp.where(strict, xk[:, :c], 0.0)
    a_rb = jnp.where(incl, xb[:, c:], 0.0)
    a_rk = jnp.where(incl, xk[:, c:], 0.0)
    pw = l_ab
    tinv = jnp.where(ri == ci, 1.0, 0.0) + l_ab
    for _ in range(int(np.log2(c)) - 1):
        pw_b = pw.astype(BF16)
        pw = _bdot(pw_b, pw_b)
        tinv = tinv + _bdot(tinv.astype(BF16), pw.astype(BF16))
    tinv_b = tinv.astype(BF16)

    def put(o, val):
        o[0] = val.reshape(nh, ts, val.shape[-1]).astype(o.dtype)

    put(ta_o, _bdot(tinv_b, at))
    put(tr_o, _bdot(tinv_b, _bdot(a_ak.astype(BF16), v).astype(BF16)))
    put(arb_o, a_rb)
    put(yv_o, _bdot(a_rk.astype(BF16), v))


def rwkv_intra(at, bt, kt, rt, v, ts):
    b, h, s, dh = at.shape
    seq = lambda: pl.BlockSpec((1, h, ts, dh), lambda bi, i: (bi, 0, i, 0))
    shp = lambda dt: jax.ShapeDtypeStruct((b, h, s, dh), dt)
    return pl.pallas_call(
        _rwkv_intra_kernel,
        grid=(b, s // ts),
        in_specs=[seq()] * 5,
        out_specs=[seq()] * 4,
        out_shape=[shp(BF16), shp(F32), shp(BF16), shp(F32)],
        compiler_params=_cparams("parallel", "parallel"),
        name="rwkv_intra",
    )(at, bt, kt, rt, v)


def _rwkv_scan_kernel(ta_ref, tr_ref, arb_ref, yv_ref, rt_ref, v_ref, bw_ref, kw_ref, wc_ref, y_ref, st_ref):
    c = RWKV_CHUNK
    nb, nh, ts, dh = ta_ref.shape
    n = nb * nh

    @pl.when(pl.program_id(0) == 0)
    def _():
        st_ref[...] = jnp.zeros_like(st_ref)

    def chunk_step(j, _):
        sl = (slice(None), slice(None), pl.ds(pl.multiple_of(j * c, c), c), slice(None))
        get = lambda ref: ref[sl].reshape(n, c, dh)
        st = st_ref[...]
        st_b = st.astype(BF16)
        u = _bdot_nt(get(ta_ref), st_b) + get(tr_ref)
        u_b = u.astype(BF16)
        y = _bdot_nt(get(rt_ref), st_b) + _bdot(get(arb_ref), u_b) + get(yv_ref)
        wc = wc_ref[:, :, pl.ds(pl.program_id(0) * (ts // c) + j, 1), :].reshape(n, 1, dh)
        st_ref[...] = st * wc + _bdot_tn(jnp.concatenate([u_b, get(v_ref)], axis=1),
                                         jnp.concatenate([get(bw_ref), get(kw_ref)], axis=1))
        y_ref[sl] = y.reshape(nb, nh, c, dh)
        return 0

    lax.fori_loop(0, ts // c, chunk_step, 0)


def rwkv_scan(ta, tr, arb, yv, rt, v, bw, kw, wc, ts):
    b, h, s, dh = ta.shape
    seq = lambda n: pl.BlockSpec((b, h, n, dh), lambda i: (0, 0, i, 0))
    return pl.pallas_call(
        _rwkv_scan_kernel,
        grid=(s // ts,),
        in_specs=[seq(ts)] * 8 + [pl.BlockSpec(wc.shape, lambda i: (0, 0, 0, 0))],
        out_specs=seq(ts),
        out_shape=jax.ShapeDtypeStruct((b, h, s, dh), F32),
        scratch_shapes=[pltpu.VMEM((b * h, dh, dh), F32)],
        compiler_params=_cparams("arbitrary"),
        name="rwkv_scan",
    )(ta, tr, arb, yv, rt, v, bw, kw, wc)


def _out_proj_kernel(x_ref, on_ref, y_ref, bonus_ref, g_ref, lnw_ref, lnb_ref, bd_ref, wn_ref, wr_ref, o_ref):
    y = jnp.concatenate([y_ref[0, h] for h in range(RWKV_HEADS)], axis=-1)
    bd = bd_ref[...]
    yc = y - _dot(y, bd, precision=HIGHEST)
    yn = yc * lax.rsqrt(_dot(yc * yc, bd, precision=HIGHEST) + GN_EPS)
    o_rwkv = (yn * lnw_ref[...] + lnb_ref[...] + bonus_ref[...]) * g_ref[...]
    o_ref[...] = (x_ref[...] + _dot(on_ref[...].astype(BF16), wn_ref[...])
                  + _dot(o_rwkv.astype(BF16), wr_ref[...]))


def out_proj(x, o_nsa, y, bonus, g, ln_w, ln_b, w_out, s, tm):
    t, d = x.shape
    dn = o_nsa.shape[1]
    dr = bonus.shape[1]
    tps = s // tm
    row = lambda w: pl.BlockSpec((tm, w), lambda i: (i, 0))
    full = lambda *sh: pl.BlockSpec(sh, lambda i: (0,) * len(sh))
    return pl.pallas_call(
        _out_proj_kernel,
        grid=(t // tm,),
        in_specs=[row(d), row(dn),
                  pl.BlockSpec((1, RWKV_HEADS, tm, HEAD_DIM), lambda i: (i // tps, 0, i % tps, 0)),
                  row(dr), row(dr), full(1, dr), full(1, dr), full(dr, dr), full(dn, d), full(dr, d)],
        out_specs=row(d),
        out_shape=jax.ShapeDtypeStruct((t, d), F32),
        compiler_params=_cparams("parallel"),
        name="out_proj",
    )(x, o_nsa, y, bonus, g, ln_w.reshape(1, dr), ln_b.reshape(1, dr),
      _block_diag_ones(dr, HEAD_DIM, 1.0 / HEAD_DIM), w_out[:dn].astype(BF16), w_out[dn:].astype(BF16))


def _cross_attn_kernel(h_ref, g_ref, wq_ref, qg_ref, kv_ref, kg_ref, wo_ref, o_ref):
    h = h_ref[...]
    d = h.shape[1]
    xd = d // X_HEADS
    q = _dot(_rms(h, g_ref[...]).astype(BF16), wq_ref[...])
    kv = kv_ref[0]
    outs = []
    for hd in range(X_HEADS):
        qh = _rms(q[:, hd * xd:(hd + 1) * xd], qg_ref[...]) * (xd ** -0.5)
        kh = _rms(kv[:, hd * xd:(hd + 1) * xd], kg_ref[...])
        vh = kv[:, d + hd * xd:d + (hd + 1) * xd]
        s = _dot_nt(qh.astype(BF16), kh.astype(BF16))
        p = jnp.exp(s - jnp.max(s, axis=-1, keepdims=True))
        p = p / jnp.sum(p, axis=-1, keepdims=True)
        outs.append(_dot(p.astype(BF16), vh.astype(BF16)))
    o = jnp.concatenate(outs, axis=-1)
    o_ref[...] = h + _dot(o.astype(BF16), wo_ref[...])


def cross_attention(h, kv, norm_g, xq_w, xq_g, xk_g, xo_w, b, s, tm):
    t, d = h.shape
    m = kv.shape[1]
    xd = d // X_HEADS
    tiles = s // tm
    full = lambda *sh: pl.BlockSpec(sh, lambda i: (0,) * len(sh))
    return pl.pallas_call(
        _cross_attn_kernel,
        grid=(t // tm,),
        in_specs=[pl.BlockSpec((tm, d), lambda i: (i, 0)), full(1, d), full(d, d), full(1, xd),
                  pl.BlockSpec((1, m, 2 * d), lambda i: (i // tiles, 0, 0)), full(1, xd), full(d, d)],
        out_specs=pl.BlockSpec((tm, d), lambda i: (i, 0)),
        out_shape=jax.ShapeDtypeStruct((t, d), F32),
        compiler_params=_cparams("parallel"),
        name="cross_attention",
    )(h, norm_g.reshape(1, d), xq_w.astype(BF16), xq_g.reshape(1, xd), kv, xk_g.reshape(1, xd),
      xo_w.astype(BF16))


def _router_kernel(h_ref, g_ref, rw_ref, rb_ref, xn_o, idx_o, gate_o):
    xn = _rms(h_ref[...], g_ref[...])
    xn_o[...] = xn.astype(BF16)
    logits = _dot(xn, rw_ref[...], precision=HIGHEST) + rb_ref[...]
    tm = logits.shape[0]
    lane = lax.broadcasted_iota(jnp.int32, (tm, LANES), 1)
    lanef = lane.astype(F32)
    logits = jnp.where(lane < N_EXPERTS, logits, REMOVED)
    idx_acc = jnp.zeros((tm, LANES), F32)
    val_acc = jnp.zeros((tm, LANES), F32)
    top = None
    for k in range(TOP_K):
        m = jnp.max(logits, axis=-1, keepdims=True)
        idx = jnp.min(jnp.where(logits == m, lanef, float(LANES)), axis=-1, keepdims=True)
        logits = jnp.where(lanef == idx, REMOVED, logits)
        top = m if top is None else top
        idx_acc = jnp.where(lane == k, idx, idx_acc)
        val_acc = jnp.where(lane == k, jnp.exp(m - top), val_acc)
    idx_o[...] = idx_acc.astype(jnp.int32)
    gate_o[...] = val_acc / jnp.sum(val_acc, axis=-1, keepdims=True)


def moe_router(h, norm_g, router_w, router_b, tm):
    t, d = h.shape
    rw = jnp.zeros((d, LANES), F32).at[:, :N_EXPERTS].set(router_w)
    rb = jnp.zeros((1, LANES), F32).at[0, :N_EXPERTS].set(router_b)
    row = lambda w: pl.BlockSpec((tm, w), lambda i: (i, 0))
    full = lambda *s: pl.BlockSpec(s, lambda i: (0,) * len(s))
    return pl.pallas_call(
        _router_kernel,
        grid=(t // tm,),
        in_specs=[row(d), full(1, d), full(d, LANES), full(1, LANES)],
        out_specs=[row(d), row(LANES), row(LANES)],
        out_shape=[jax.ShapeDtypeStruct((t, d), BF16), jax.ShapeDtypeStruct((t, LANES), jnp.int32),
                   jax.ShapeDtypeStruct((t, LANES), F32)],
        compiler_params=_cparams("parallel"),
        name="moe_router",
    )(h, norm_g.reshape(1, d), rw, rb)


def _expert_kernel(blk_e_ref, n_used_ref, x_ref, w1g_ref, w1l_ref, b1g_ref, b1l_ref, w2_ref, b2_ref, rw_ref, o_ref):
    i = pl.program_id(0)

    @pl.when(i < n_used_ref[0])
    def _():
        x = x_ref[...]
        hg = jnp.minimum(_dot(x, w1g_ref[0]) + b1g_ref[0], SWIGLU_LIMIT)
        hl = jnp.clip(_dot(x, w1l_ref[0]) + b1l_ref[0], -SWIGLU_LIMIT, SWIGLU_LIMIT)
        act = hg * jax.nn.sigmoid(SWIGLU_ALPHA * hg) * (hl + 1.0)
        y = _dot(act.astype(BF16), w2_ref[0].astype(BF16)) + b2_ref[0]
        o_ref[...] = y * rw_ref[...]

    @pl.when(i >= n_used_ref[0])
    def _():
        o_ref[...] = jnp.zeros_like(o_ref)


def moe_experts(xs, row_w, blk_e, n_used, w1g, w1l, b1g, b1l, w2, b2):
    r, d = xs.shape
    f = w1g.shape[2]
    m = MOE_ROW_BLOCK
    ex = lambda *s: pl.BlockSpec((1,) + s, lambda i, be, nu: (be[i], 0, 0))
    grid_spec = pltpu.PrefetchScalarGridSpec(
        num_scalar_prefetch=2,
        grid=(r // m,),
        in_specs=[pl.BlockSpec((m, d), lambda i, be, nu: (i, 0)),
                  ex(d, f), ex(d, f), ex(1, f), ex(1, f), ex(f, d), ex(1, d),
                  pl.BlockSpec((m, 1), lambda i, be, nu: (i, 0))],
        out_specs=pl.BlockSpec((m, d), lambda i, be, nu: (i, 0)),
    )
    return pl.pallas_call(
        _expert_kernel,
        grid_spec=grid_spec,
        out_shape=jax.ShapeDtypeStruct((r, d), F32),
        compiler_params=_cparams("arbitrary"),
        name="moe_experts",
    )(blk_e, n_used, xs, w1g, w1l, b1g, b1l, w2, b2, row_w)


def _layer(x, mem, norm_mix_g, w_in, q_norm_g, k_cmp_norm_g, k_slc_norm_g, k_win_norm_g,
           cmp_pe_k, cmp_pe_v, cmp_k_w1, cmp_k_w2, cmp_v_w1, cmp_v_w2,
           rwkv_mu, rwkv_w0, rwkv_w_up, rwkv_a0, rwkv_a_up, rwkv_g_up, rwkv_k_k, rwkv_k_a,
           rwkv_r_k, rwkv_ln_w, rwkv_ln_b, w_out,
           norm_x_g, norm_mem_g, xq_w, xk_w, xv_w, xq_norm_g, xk_norm_g, xo_w,
           norm_ffn_g, router_w, router_b, mlp1_w, mlp1_b, mlp2_w, mlp2_b):
    b, s, d = x.shape
    t = b * s
    tm = 512
    xt = x.reshape(t, d)

    w_nsa = jnp.pad(w_in[:, :NSA_PROJ], ((0, 0), (0, NSA_PROJ_PAD - NSA_PROJ)))
    p_nsa = norm_matmul(xt, norm_mix_g, w_nsa, tm)
    p_rwkv = norm_matmul(xt, norm_mix_g, w_in[:, NSA_PROJ:], tm)

    nq = NSA_HEADS * HEAD_DIM
    gw = NSA_GROUPS * HEAD_DIM
    qn, ks, vs, kw, vw, gc, gs, gwin = nsa_prep(p_nsa, q_norm_g, k_slc_norm_g, k_win_norm_g, tm)
    kcv = nsa_compress(p_nsa[:, nq:nq + gw], p_nsa[:, nq + gw:nq + 2 * gw], cmp_pe_k, cmp_pe_v,
                       cmp_k_w1, cmp_k_w2, cmp_v_w1, cmp_v_w2, k_cmp_norm_g, b, s)
    o_nsa = nsa_attention(qn, kcv, ks, vs, kw, vw, gc, gs, gwin, b, s)

    at, bt, kt, rt, v, bw, kwd, wc, g_gate, bonus = rwkv_prep(
        p_rwkv, rwkv_mu, rwkv_w0, rwkv_w_up, rwkv_a0, rwkv_a_up, rwkv_g_up, rwkv_k_k, rwkv_k_a, rwkv_r_k, b, s, tm)
    ta, tr, arb, yv = rwkv_intra(at, bt, kt, rt, v, ts=256)
    y = rwkv_scan(ta, tr, arb, yv, rt, v, bw, kwd, wc, ts=256)

    h1 = out_proj(xt, o_nsa, y, bonus, g_gate, rwkv_ln_w, rwkv_ln_b, w_out, s, tm)
    m = mem.shape[1]
    kv = norm_matmul(mem.reshape(b * m, d), norm_mem_g, jnp.concatenate([xk_w, xv_w], axis=1), m)
    h2 = cross_attention(h1, kv.reshape(b, m, 2 * d), norm_x_g, xq_w, xq_norm_g, xk_norm_g, xo_w, b, s, tm)

    xn, top_i, gate = moe_router(h2, norm_ffn_g, router_w, router_b, tm)
    top_i = top_i[:, :TOP_K]
    gate = gate[:, :TOP_K]
    a = t * TOP_K
    mb = MOE_ROW_BLOCK
    e_flat = top_i.reshape(a)
    order = jnp.argsort(e_flat, stable=True)
    e_sorted = e_flat[order]
    counts = jnp.bincount(e_flat, length=N_EXPERTS)
    starts = jnp.cumsum(counts) - counts
    padded = (counts + mb - 1) // mb * mb
    pends = jnp.cumsum(padded)
    pstarts = pends - padded
    dest = (pstarts[e_sorted] + jnp.arange(a) - starts[e_sorted]).astype(jnp.int32)
    n_blocks = -(-a // mb) + N_EXPERTS
    r = n_blocks * mb
    row_tok = jnp.full((r,), t, jnp.int32).at[dest].set((order // TOP_K).astype(jnp.int32))
    row_w = jnp.zeros((r,), F32).at[dest].set(gate.reshape(a)[order])
    blk_e = jnp.minimum(jnp.searchsorted(pends, jnp.arange(n_blocks) * mb, side='right'),
                        N_EXPERTS - 1).astype(jnp.int32)
    n_used = (pends[-1] // mb).astype(jnp.int32).reshape(1)
    xpad = jnp.concatenate([xn, jnp.zeros((1, d), xn.dtype)], axis=0)
    xs = xpad[row_tok]
    f = mlp1_w.shape[2] // 2
    w1 = mlp1_w.reshape(N_EXPERTS, d, f, 2)
    b1 = mlp1_b.reshape(N_EXPERTS, 1, f, 2)
    ys = moe_experts(xs, row_w.reshape(r, 1), blk_e, n_used,
                     w1[..., 0].astype(BF16), w1[..., 1].astype(BF16), b1[..., 0], b1[..., 1],
                     mlp2_w, mlp2_b.reshape(N_EXPERTS, 1, d))
    pos = jnp.zeros((a,), jnp.int32).at[order].set(dest).reshape(t, TOP_K)
    out = h2 + jnp.sum(ys[pos], axis=1)
    return out.reshape(b, s, d)


def kernel(x, mem, norm_mix_g, w_in, q_norm_g, k_cmp_norm_g, k_slc_norm_g, k_win_norm_g, cmp_pe_k, cmp_pe_v, cmp_k_w1, cmp_k_w2, cmp_v_w1, cmp_v_w2, rwkv_mu, rwkv_w0, rwkv_w_up, rwkv_a0, rwkv_a_up, rwkv_g_up, rwkv_k_k, rwkv_k_a, rwkv_r_k, rwkv_ln_w, rwkv_ln_b, w_out, norm_x_g, norm_mem_g, xq_w, xk_w, xv_w, xq_norm_g, xk_norm_g, xo_w, norm_ffn_g, router_w, router_b, mlp1_w, mlp1_b, mlp2_w, mlp2_b):
    params = (norm_mix_g, w_in, q_norm_g, k_cmp_norm_g, k_slc_norm_g, k_win_norm_g, cmp_pe_k, cmp_pe_v,
              cmp_k_w1, cmp_k_w2, cmp_v_w1, cmp_v_w2, rwkv_mu, rwkv_w0, rwkv_w_up, rwkv_a0, rwkv_a_up,
              rwkv_g_up, rwkv_k_k, rwkv_k_a, rwkv_r_k, rwkv_ln_w, rwkv_ln_b, w_out, norm_x_g, norm_mem_g,
              xq_w, xk_w, xv_w, xq_norm_g, xk_norm_g, xo_w, norm_ffn_g, router_w, router_b,
              mlp1_w, mlp1_b, mlp2_w, mlp2_b)
    h = x
    for layer in range(norm_mix_g.shape[0]):
        h = _layer(h, mem, *[prm[layer] for prm in params])
    return h
```

```python
import functools

import numpy as np
import jax
import jax.numpy as jnp
from jax import lax
from jax.experimental import pallas as pl
from jax.experimental.pallas import tpu as pltpu

F32 = jnp.float32
BF16 = jnp.bfloat16
HIGHEST = lax.Precision.HIGHEST

V7X_VMEM_BYTES = 64 * 1024 * 1024
VMEM_LIMIT = V7X_VMEM_BYTES * 3 // 4

HEAD_DIM = 64
NSA_HEADS = 8
NSA_GROUPS = 2
NSA_HPG = NSA_HEADS // NSA_GROUPS
GROUP_W = NSA_HPG * HEAD_DIM
CMP_BLOCK = 32
CMP_STRIDE = 16
SLC_BLOCK = 64
SLC_TOPK = 16
WINDOW = 512
Q_BLOCK = 128
SEL_CHUNK = 512
RWKV_HEADS = 8
RWKV_DIM = RWKV_HEADS * HEAD_DIM
RWKV_CHUNK = 64
GN_EPS = HEAD_DIM * 1e-5
X_HEADS = 4
N_EXPERTS = 32
TOP_K = 4
SWIGLU_LIMIT = 7.0
SWIGLU_ALPHA = 1.702
MOE_ROW_BLOCK = 256
RMS_EPS = 1e-6
NEG_INF = -1e30
BIG = 1e9
REMOVED = -3e38
LANES = 128

NSA_PROJ = NSA_HEADS * HEAD_DIM + 6 * NSA_GROUPS * HEAD_DIM + NSA_HEADS * 3
NSA_PROJ_PAD = -(-NSA_PROJ // LANES) * LANES
RWKV_PROJ = 3 * RWKV_DIM + 64 + 64 + 128


def _cparams(*sem):
    return pltpu.CompilerParams(dimension_semantics=sem, vmem_limit_bytes=VMEM_LIMIT)


def _dot(a, b, **kw):
    return jnp.dot(a, b, preferred_element_type=F32, **kw)


def _dot_nt(a, b, **kw):
    return lax.dot_general(a, b, (((1,), (1,)), ((), ())), preferred_element_type=F32, **kw)


def _dot_tn(a, b, **kw):
    return lax.dot_general(a, b, (((0,), (0,)), ((), ())), preferred_element_type=F32, **kw)


def _rms(x, g):
    return x * lax.rsqrt(jnp.mean(x * x, axis=-1, keepdims=True) + RMS_EPS) * g


def _block_diag_ones(n, blk, scale=1.0):
    i = np.arange(n)
    return jnp.asarray(((i[:, None] // blk) == (i[None, :] // blk)).astype(np.float32) * scale)


def _norm_matmul_kernel(x_ref, g_ref, w_ref, o_ref):
    xn = _rms(x_ref[...], g_ref[...]).astype(BF16)
    o_ref[...] = _dot(xn, w_ref[...])


def norm_matmul(x, g, w, tm):
    m, d = x.shape
    n = w.shape[1]
    return pl.pallas_call(
        _norm_matmul_kernel,
        grid=(m // tm,),
        in_specs=[pl.BlockSpec((tm, d), lambda i: (i, 0)),
                  pl.BlockSpec((1, d), lambda i: (0, 0)),
                  pl.BlockSpec((d, n), lambda i: (0, 0))],
        out_specs=pl.BlockSpec((tm, n), lambda i: (i, 0)),
        out_shape=jax.ShapeDtypeStruct((m, n), F32),
        compiler_params=_cparams("parallel"),
        name="norm_matmul",
    )(x, g.reshape(1, d), w.astype(BF16))


def _nsa_prep_kernel(p_ref, qg_ref, ksg_ref, kwg_ref, bdq_ref, bdk_ref, q_o, ks_o, vs_o, kw_o, vw_o, gate_o):
    p = p_ref[...]
    nq = NSA_HEADS * HEAD_DIM
    gw = NSA_GROUPS * HEAD_DIM
    q = p[:, :nq]
    msq = _dot(q * q, bdq_ref[...], precision=HIGHEST)
    q_o[...] = (q * lax.rsqrt(msq + RMS_EPS) * qg_ref[...] * (HEAD_DIM ** -0.5)).astype(BF16)

    def seg(k):
        return p[:, nq + k * gw: nq + (k + 1) * gw]

    def head_norm(t, g):
        ms = _dot(t * t, bdk_ref[...], precision=HIGHEST)
        return t * lax.rsqrt(ms + RMS_EPS) * g

    ks_o[...] = head_norm(seg(2), ksg_ref[...]).astype(BF16)
    vs_o[...] = seg(3).astype(BF16)
    kw_o[...] = head_norm(seg(4), kwg_ref[...]).astype(BF16)
    vw_o[...] = seg(5).astype(BF16)
    gate_o[...] = jax.nn.sigmoid(p[:, nq + 6 * gw: nq + 6 * gw + LANES])


def nsa_prep(p_nsa, q_g, ks_g, kw_g, tm):
    t = p_nsa.shape[0]
    nq = NSA_HEADS * HEAD_DIM
    gw = NSA_GROUPS * HEAD_DIM
    tile = lambda v, n: jnp.tile(v.reshape(1, HEAD_DIM), (1, n))
    row = lambda w: pl.BlockSpec((tm, w), lambda i: (i, 0))
    full = lambda *s: pl.BlockSpec(s, lambda i: (0,) * len(s))
    return pl.pallas_call(
        _nsa_prep_kernel,
        grid=(t // tm,),
        in_specs=[row(NSA_PROJ_PAD), full(1, nq), full(1, gw), full(1, gw), full(nq, nq), full(gw, gw)],
        out_specs=[row(nq)] + [row(gw)] * 4 + [row(LANES)],
        out_shape=[jax.ShapeDtypeStruct((t, nq), BF16)] + [jax.ShapeDtypeStruct((t, gw), BF16)] * 4
                  + [jax.ShapeDtypeStruct((t, LANES), F32)],
        compiler_params=_cparams("parallel"),
        name="nsa_prep",
    )(p_nsa, tile(q_g, NSA_HEADS), tile(ks_g, NSA_GROUPS), tile(kw_g, NSA_GROUPS),
      _block_diag_ones(nq, HEAD_DIM, 1.0 / HEAD_DIM), _block_diag_ones(gw, HEAD_DIM, 1.0 / HEAD_DIM))


def _compress_kernel(ch_ref, pe_ref, w1_ref, w2_ref, g_ref, o_ref):
    ch = ch_ref[0, 0, 0]
    nc = ch.shape[0]
    half = CMP_STRIDE * HEAD_DIM
    nxt = pltpu.roll(ch, nc - 1, 0)
    w1 = w1_ref[0]
    h1 = (_dot(ch, w1[:half], precision=HIGHEST) + _dot(nxt, w1[half:], precision=HIGHEST)
          + _dot(pe_ref[0], w1, precision=HIGHEST))
    out = _dot(jax.nn.silu(h1), w2_ref[0], precision=HIGHEST)
    out = jnp.where(pl.program_id(0) == 0, _rms(out, g_ref[...]), out)
    o_ref[0, 0, 0] = out.astype(BF16)


def nsa_compress(kc, vc, pe_k, pe_v, kw1, kw2, vw1, vw2, kc_g, b, s):
    nc = s // CMP_STRIDE
    half = CMP_STRIDE * HEAD_DIM

    def chunks(t):
        return t.reshape(b, nc, CMP_STRIDE, NSA_GROUPS, HEAD_DIM).transpose(0, 3, 1, 2, 4).reshape(
            b, NSA_GROUPS, nc, half)

    ch = jnp.stack([chunks(kc), chunks(vc)])
    pe = jnp.stack([pe_k.reshape(1, 2 * half), pe_v.reshape(1, 2 * half)])
    return pl.pallas_call(
        _compress_kernel,
        grid=(2, b, NSA_GROUPS),
        in_specs=[pl.BlockSpec((1, 1, 1, nc, half), lambda kv, bi, g: (kv, bi, g, 0, 0)),
                  pl.BlockSpec((1, 1, 2 * half), lambda kv, bi, g: (kv, 0, 0)),
                  pl.BlockSpec((1, 2 * half, HEAD_DIM), lambda kv, bi, g: (kv, 0, 0)),
                  pl.BlockSpec((1, HEAD_DIM, HEAD_DIM), lambda kv, bi, g: (kv, 0, 0)),
                  pl.BlockSpec((1, HEAD_DIM), lambda kv, bi, g: (0, 0))],
        out_specs=pl.BlockSpec((1, 1, 1, nc, HEAD_DIM), lambda kv, bi, g: (kv, bi, g, 0, 0)),
        out_shape=jax.ShapeDtypeStruct((2, b, NSA_GROUPS, nc, HEAD_DIM), BF16),
        compiler_params=_cparams("parallel", "parallel", "parallel"),
        name="nsa_compress",
    )(ch, pe, jnp.stack([kw1, vw1]), jnp.stack([kw2, vw2]), kc_g.reshape(1, HEAD_DIM))


def _masked_exp_cols(s, mask):
    sm = jnp.where(mask, s, NEG_INF)
    p = jnp.where(mask, jnp.exp(sm - jnp.max(sm, axis=0, keepdims=True)), 0.0)
    l = jnp.sum(p, axis=0, keepdims=True)
    return p, 1.0 / jnp.where(l > 0.0, l, 1.0)


def _nsa_attn_kernel(qt_ref, kc_ref, vct_ref, ovt_ref, ks_ref, vst_ref, kw_ref, vwt_ref, gate_ref, o_ref,
                     *, n_sel, n_top):
    g = pl.program_id(1)
    t0 = pl.program_id(2) * Q_BLOCK
    ks_ref, vst_ref, kw_ref, vwt_ref = (r.at[0, 0] for r in (ks_ref, vst_ref, kw_ref, vwt_ref))
    cols = NSA_HPG * Q_BLOCK
    col = lax.broadcasted_iota(jnp.int32, (1, cols), 1)
    tq = t0 + col % Q_BLOCK
    head = g * NSA_HPG + col // Q_BLOCK
    slope = lax.bitcast_convert_type((127 - (head + 1)) << 23, F32)
    qt = jnp.concatenate([qt_ref[0, h] for h in range(NSA_HPG)], axis=1)
    frow = lax.broadcasted_iota(jnp.int32, (HEAD_DIM, cols), 0)
    tq_hi = (tq // SLC_BLOCK * SLC_BLOCK).astype(F32)
    tq_lo = (tq % SLC_BLOCK).astype(F32)
    qpos = jnp.where(frow < 2, slope, jnp.where(frow == 2, -slope * tq_hi, jnp.where(frow == 3, -slope * tq_lo, 0.0)))
    qc = jnp.concatenate([qt, qpos.astype(BF16)], axis=0)

    def heads_sum(x):
        acc = x[:, :Q_BLOCK]
        for h in range(1, NSA_HPG):
            acc = acc + x[:, h * Q_BLOCK:(h + 1) * Q_BLOCK]
        return acc

    kc = kc_ref[0, 0]
    nc = kc.shape[0]
    c_last = lax.broadcasted_iota(jnp.int32, (nc, 1), 0) * CMP_STRIDE + (CMP_BLOCK - 1)
    p_c, inv_c = _masked_exp_cols(_dot(kc, qc), c_last <= tq)
    p_c = p_c * inv_c
    o_c = _dot(vct_ref[0, 0], p_c.astype(BF16))
    imp = _dot(ovt_ref[...], heads_sum(p_c), precision=HIGHEST)

    bid = lax.broadcasted_iota(jnp.int32, (LANES, Q_BLOCK), 0)
    bidf = bid.astype(F32)
    tq1 = t0 + lax.broadcasted_iota(jnp.int32, (1, Q_BLOCK), 1)
    cur = tq1 // SLC_BLOCK
    forced = (bid == 0) | (bid == cur) | (bid == cur - 1)
    score = jnp.where(forced, BIG, jnp.where(bid * SLC_BLOCK <= tq1, imp, -BIG))
    score = jnp.where(bid < n_sel, score, REMOVED)
    picked = jnp.zeros((LANES, Q_BLOCK), jnp.bool_)
    for _ in range(n_top):
        m = jnp.max(score, axis=0, keepdims=True)
        pick = bidf == jnp.min(jnp.where(score == m, bidf, float(LANES)), axis=0, keepdims=True)
        picked = picked | pick
        score = jnp.where(pick, REMOVED, score)
    sel_neg = jnp.where(picked, 0.0, NEG_INF).astype(BF16)
    qx = jnp.concatenate([qc, jnp.concatenate([sel_neg] * NSA_HPG, axis=1)], axis=0)

    def values_t(ref, first_tile, n_tiles):
        return jnp.concatenate([ref[first_tile + i] for i in range(n_tiles)], axis=1)

    span = WINDOW + Q_BLOCK
    ws = pl.multiple_of(jnp.maximum(t0 - WINDOW, 0), Q_BLOCK)
    dw = tq - (ws + lax.broadcasted_iota(jnp.int32, (span, 1), 0))
    p_w, inv_w = _masked_exp_cols(_dot(kw_ref[pl.ds(ws, span), :], qc), (dw >= 0) & (dw < WINDOW))
    o_w = _dot(values_t(vwt_ref, ws // Q_BLOCK, span // Q_BLOCK), p_w.astype(BF16)) * inv_w

    krow = lax.broadcasted_iota(jnp.int32, (SEL_CHUNK, 1), 0)

    def flash(s, start, carry):
        m, l, acc = carry
        m_new = jnp.maximum(m, jnp.max(s, axis=0, keepdims=True))
        alpha = jnp.exp(m - m_new)
        p = jnp.exp(s - m_new)
        l = alpha * l + jnp.sum(p, axis=0, keepdims=True)
        v = values_t(vst_ref, start // Q_BLOCK, SEL_CHUNK // Q_BLOCK)
        return m_new, l, alpha * acc + _dot(v, p.astype(BF16))

    def full_step(j, carry):
        start = pl.multiple_of(j * SEL_CHUNK, SEL_CHUNK)
        return flash(_dot(ks_ref[pl.ds(start, SEL_CHUNK), :], qx), start, carry)

    n_full = t0 // SEL_CHUNK
    init = (jnp.full((1, cols), NEG_INF, F32), jnp.zeros((1, cols), F32), jnp.zeros((HEAD_DIM, cols), F32))
    carry = lax.fori_loop(0, n_full, full_step, init)
    start = pl.multiple_of(n_full * SEL_CHUNK, SEL_CHUNK)
    s_diag = jnp.where(start + krow <= tq, _dot(ks_ref[pl.ds(start, SEL_CHUNK), :], qx), NEG_INF)
    _, l_s, acc_s = flash(s_diag, start, carry)
    o_s = acc_s / l_s

    for h in range(NSA_HPG):
        hs = slice(h * Q_BLOCK, (h + 1) * Q_BLOCK)
        gate = lambda br: gate_ref[0, 0, br, h:h + 1, :]
        o_ref[0, h] = gate(0) * o_c[:, hs] + gate(1) * o_s[:, hs] + gate(2) * o_w[:, hs]


def nsa_attention(qn, kcv, ks, vs, kw, vw, gates, b, s):
    nq_blocks = s // Q_BLOCK
    nc = s // CMP_STRIDE
    n_cmp = nc - CMP_BLOCK // CMP_STRIDE + 1
    n_sel = s // SLC_BLOCK
    n_top = min(SLC_TOPK, n_sel)
    grp, dh = NSA_GROUPS, HEAD_DIM
    assert n_sel <= LANES and s % SEL_CHUNK == 0 and s >= WINDOW + Q_BLOCK
    c_start = np.arange(n_cmp) * CMP_STRIDE
    s_start = np.arange(n_sel) * SLC_BLOCK
    ovt = np.zeros((LANES, nc), np.float32)
    ovt[:n_sel, :n_cmp] = (np.clip(np.minimum((c_start + CMP_BLOCK)[:, None], s_start[None] + SLC_BLOCK)
                                   - np.maximum(c_start[:, None], s_start[None]), 0, None) / CMP_BLOCK).T

    def pos_features(pos):
        f = np.zeros((len(pos), dh), np.float32)
        f[:, 0] = pos // SLC_BLOCK * SLC_BLOCK
        f[:, 1] = pos % SLC_BLOCK
        f[:, 2:4] = 1.0
        return jnp.asarray(f, BF16)

    tok = np.arange(s)
    onehot = jnp.asarray(tok[:, None] // SLC_BLOCK == np.arange(LANES)[None], BF16)
    per_group = lambda a: a.reshape(b, s, grp, dh).transpose(0, 2, 1, 3)
    with_feats = lambda k, *f: jnp.concatenate(
        [k] + [jnp.broadcast_to(x, k.shape[:2] + x.shape) for x in f], axis=-1)
    tiles_t = lambda v: v.reshape(b, s // Q_BLOCK, Q_BLOCK, grp, dh).transpose(0, 3, 1, 4, 2)
    qt = qn.reshape(b, s, NSA_HEADS, dh).transpose(0, 2, 3, 1)
    ks_x = with_feats(per_group(ks), pos_features(tok), onehot)
    kw_x = with_feats(per_group(kw), pos_features(tok))
    kc_x = with_feats(kcv[0], pos_features(np.arange(nc) * CMP_STRIDE + (CMP_BLOCK - 1)))
    vct = kcv[1].transpose(0, 1, 3, 2)
    gates_t = gates[:, :NSA_HEADS * 3].reshape(b, s, grp, NSA_HPG, 3).transpose(0, 2, 4, 3, 1)

    grp_spec = lambda *shape: pl.BlockSpec((1, 1) + shape, lambda bi, g, i: (bi, g) + (0,) * len(shape),
                                           pipeline_mode=pl.Buffered(1))
    return pl.pallas_call(
        functools.partial(_nsa_attn_kernel, n_sel=n_sel, n_top=n_top),
        grid=(b, grp, nq_blocks),
        in_specs=[pl.BlockSpec((1, NSA_HPG, dh, Q_BLOCK), lambda bi, g, i: (bi, g, 0, i)),
                  grp_spec(nc, 2 * dh), grp_spec(dh, nc),
                  pl.BlockSpec((LANES, nc), lambda bi, g, i: (0, 0)),
                  grp_spec(s, 2 * dh + LANES), grp_spec(s // Q_BLOCK, dh, Q_BLOCK),
                  grp_spec(s, 2 * dh), grp_spec(s // Q_BLOCK, dh, Q_BLOCK),
                  pl.BlockSpec((1, 1, 3, NSA_HPG, Q_BLOCK), lambda bi, g, i: (bi, g, 0, 0, i))],
        out_specs=pl.BlockSpec((1, NSA_HPG, dh, Q_BLOCK), lambda bi, g, i: (bi, g, 0, i)),
        out_shape=jax.ShapeDtypeStruct((b, NSA_HEADS, dh, s), F32),
        compiler_params=_cparams("parallel", "parallel", "arbitrary"),
        name="nsa_attention",
    )(qt, kc_x, vct, jnp.asarray(ovt), ks_x, tiles_t(vs), kw_x, tiles_t(vw), gates_t)


def _rwkv_prep_kernel(p_ref, prev_ref, mu_ref, w0_ref, a0_ref, kk_ref, ka_ref, rk_ref,
                      wup_ref, aup_ref, gup_ref, bd_ref, ltri_ref, lones_ref, csum_ref,
                      at_o, bt_o, kt_o, rt_o, v_o, bw_o, kw_o, wc_o, g_o, bonus_o, *, tiles_per_seq):
    p = p_ref[...]
    tm = p.shape[0]
    first = pl.program_id(0) % tiles_per_seq == 0
    last_prev = jnp.where(first, 0.0, prev_ref[7:8, :])
    prev = pltpu.roll(p, 1, 0)
    prev = jnp.where(lax.broadcasted_iota(jnp.int32, (tm, 1), 0) == 0, last_prev, prev)
    pm = p + (prev - p) * mu_ref[...]
    d = RWKV_DIM
    r, k, v = pm[:, :d], pm[:, d:2 * d], pm[:, 2 * d:3 * d]
    lora = pm[:, 3 * d:3 * d + LANES]
    gd = pm[:, 3 * d + LANES:3 * d + 2 * LANES]
    z = -(w0_ref[...] + _dot(jnp.tanh(lora).astype(BF16), wup_ref[...]))
    softplus = jnp.maximum(z, 0.0) + jnp.log(1.0 + jnp.exp(-jnp.abs(z)))
    w = -softplus - 0.5
    a = jax.nn.sigmoid(a0_ref[...] + _dot(lora.astype(BF16), aup_ref[...]))
    g_o[...] = _dot(jax.nn.sigmoid(gd).astype(BF16), gup_ref[...])
    bd = bd_ref[...]
    kkr = k * kk_ref[...]
    kk = kkr / jnp.maximum(jnp.sqrt(_dot(kkr * kkr, bd, precision=HIGHEST)), 1e-12)
    k2 = k * (1.0 + (a - 1.0) * ka_ref[...])
    bonus_o[...] = _dot(r * k2 * rk_ref[...], bd, precision=HIGHEST) * v
    lw = -jnp.exp(w)
    cum = _dot(ltri_ref[...], lw, precision=HIGHEST)
    tot = _dot(lones_ref[...], lw, precision=HIGHEST)
    e_in = jnp.exp(cum)
    e_out = jnp.exp(-cum)
    e_end = jnp.exp(tot - cum)

    def put_heads(o, val):
        for h in range(RWKV_HEADS):
            o[0, h] = val[:, h * HEAD_DIM:(h + 1) * HEAD_DIM].astype(o.dtype)

    put_heads(at_o, -kk * jnp.exp(cum - lw))
    put_heads(bt_o, kk * a * e_out)
    put_heads(kt_o, k2 * e_out)
    put_heads(rt_o, r * e_in)
    put_heads(v_o, v)
    put_heads(bw_o, kk * a * e_end)
    put_heads(kw_o, k2 * e_end)
    put_heads(wc_o, jnp.exp(_dot(csum_ref[...], lw, precision=HIGHEST)))


def rwkv_prep(p_rwkv, mu, w0, w_up, a0, a_up, g_up, k_k, k_a, r_k, b, s, tm):
    t = p_rwkv.shape[0]
    d = RWKV_DIM
    c = RWKV_CHUNK
    tps = s // tm
    cpt = tm // c
    wup = jnp.concatenate([w_up, jnp.zeros_like(a_up)], axis=0).astype(BF16)
    aup = jnp.concatenate([jnp.zeros_like(w_up), a_up], axis=0).astype(BF16)
    i = np.arange(tm)
    same = (i[:, None] // c) == (i[None, :] // c)
    ltri = jnp.asarray((same & (i[:, None] >= i[None, :])).astype(np.float32))
    lones = jnp.asarray(same.astype(np.float32))
    csum = jnp.asarray((np.arange(cpt)[:, None] == (i[None, :] // c)).astype(np.float32))
    row = lambda w: pl.BlockSpec((tm, w), lambda i: (i, 0))
    full = lambda *sh: pl.BlockSpec(sh, lambda i: (0,) * len(sh))
    heads = lambda n: pl.BlockSpec((1, RWKV_HEADS, n, HEAD_DIM), lambda i: (i // tps, 0, i % tps, 0))
    hshape = lambda n, dt: jax.ShapeDtypeStruct((b, RWKV_HEADS, n, HEAD_DIM), dt)
    vec = lambda x: x.reshape(1, -1)
    return pl.pallas_call(
        functools.partial(_rwkv_prep_kernel, tiles_per_seq=tps),
        grid=(t // tm,),
        in_specs=[row(RWKV_PROJ),
                  pl.BlockSpec((8, RWKV_PROJ), lambda i: (jnp.maximum(i * (tm // 8) - 1, 0), 0)),
                  full(1, RWKV_PROJ), full(1, d), full(1, d), full(1, d), full(1, d), full(1, d),
                  full(LANES, d), full(LANES, d), full(LANES, d), full(d, d), full(tm, tm), full(tm, tm),
                  full(cpt, tm)],
        out_specs=[heads(tm)] * 7 + [heads(cpt), row(d), row(d)],
        out_shape=[hshape(s, BF16)] * 7 + [hshape(s // c, F32)] + [jax.ShapeDtypeStruct((t, d), F32)] * 2,
        compiler_params=_cparams("parallel"),
        name="rwkv_prep",
    )(p_rwkv, p_rwkv, vec(mu), vec(w0), vec(a0), vec(k_k), vec(k_a), vec(r_k), wup, aup, g_up.astype(BF16),
      _block_diag_ones(d, HEAD_DIM), ltri, lones, csum)


def _bdot(a, b):
    return lax.dot_general(a, b, (((2,), (1,)), ((0,), (0,))), preferred_element_type=F32)


def _bdot_nt(a, b):
    return lax.dot_general(a, b, (((2,), (2,)), ((0,), (0,))), preferred_element_type=F32)


def _bdot_tn(a, b):
    return lax.dot_general(a, b, (((1,), (1,)), ((0,), (0,))), preferred_element_type=F32)


def _rwkv_intra_kernel(at_ref, bt_ref, kt_ref, rt_ref, v_ref, ta_o, tr_o, arb_o, yv_o):
    c = RWKV_CHUNK
    _, nh, ts, dh = at_ref.shape
    n = nh * (ts // c)
    chunked = lambda ref: ref[0].reshape(n, c, dh)
    at, bt, kt, rt, v = (chunked(r) for r in (at_ref, bt_ref, kt_ref, rt_ref, v_ref))
    ri = lax.broadcasted_iota(jnp.int32, (1, c, c), 1)
    ci = lax.broadcasted_iota(jnp.int32, (1, c, c), 2)
    strict = ri > ci
    incl = ri >= ci
    ar = jnp.concatenate([at, rt], axis=1)
    xb = _bdot_nt(ar, bt)
    xk = _bdot_nt(ar, kt)
    l_ab = jnp.where(strict, xb[:, :c], 0.0)
    a_ak = jnp.where(strict, xk[:, :c], 0.0)
    a_rb = jnp.where(incl, xb[:, c:], 0.0)
    a_rk = jnp.where(incl, xk[:, c:], 0.0)
    pw = l_ab
    tinv = jnp.where(ri == ci, 1.0, 0.0) + l_ab
    for _ in range(int(np.log2(c)) - 1):
        pw_b = pw.astype(BF16)
        pw = _bdot(pw_b, pw_b)
        tinv = tinv + _bdot(tinv.astype(BF16), pw.astype(BF16))
    tinv_b = tinv.astype(BF16)

    def put(o, val):
        o[0] = val.reshape(nh, ts, val.shape[-1]).astype(o.dtype)

    put(ta_o, _bdot(tinv_b, at))
    put(tr_o, _bdot(tinv_b, _bdot(a_ak.astype(BF16), v).astype(BF16)))
    put(arb_o, a_rb)
    put(yv_o, _bdot(a_rk.astype(BF16), v))


def rwkv_intra(at, bt, kt, rt, v, ts):
    b, h, s, dh = at.shape
    seq = lambda: pl.BlockSpec((1, h, ts, dh), lambda bi, i: (bi, 0, i, 0))
    shp = lambda dt: jax.ShapeDtypeStruct((b, h, s, dh), dt)
    return pl.pallas_call(
        _rwkv_intra_kernel,
        grid=(b, s // ts),
        in_specs=[seq()] * 5,
        out_specs=[seq()] * 4,
        out_shape=[shp(BF16), shp(F32), shp(BF16), shp(F32)],
        compiler_params=_cparams("parallel", "parallel"),
        name="rwkv_intra",
    )(at, bt, kt, rt, v)


def _rwkv_scan_kernel(ta_ref, tr_ref, arb_ref, yv_ref, rt_ref, v_ref, bw_ref, kw_ref, wc_ref, y_ref, st_ref):
    c = RWKV_CHUNK
    nb, nh, ts, dh = ta_ref.shape
    n = nb * nh

    @pl.when(pl.program_id(0) == 0)
    def _():
        st_ref[...] = jnp.zeros_like(st_ref)

    def chunk_step(j, _):
        sl = (slice(None), slice(None), pl.ds(pl.multiple_of(j * c, c), c), slice(None))
        get = lambda ref: ref[sl].reshape(n, c, dh)
        st = st_ref[...]
        st_b = st.astype(BF16)
        u = _bdot_nt(get(ta_ref), st_b) + get(tr_ref)
        u_b = u.astype(BF16)
        y = _bdot_nt(get(rt_ref), st_b) + _bdot(get(arb_ref), u_b) + get(yv_ref)
        wc = wc_ref[:, :, pl.ds(pl.program_id(0) * (ts // c) + j, 1), :].reshape(n, 1, dh)
        st_ref[...] = st * wc + _bdot_tn(jnp.concatenate([u_b, get(v_ref)], axis=1),
                                         jnp.concatenate([get(bw_ref), get(kw_ref)], axis=1))
        y_ref[sl] = y.reshape(nb, nh, c, dh)
        return 0

    lax.fori_loop(0, ts // c, chunk_step, 0)


def rwkv_scan(ta, tr, arb, yv, rt, v, bw, kw, wc, ts):
    b, h, s, dh = ta.shape
    seq = lambda n: pl.BlockSpec((b, h, n, dh), lambda i: (0, 0, i, 0))
    return pl.pallas_call(
        _rwkv_scan_kernel,
        grid=(s // ts,),
        in_specs=[seq(ts)] * 8 + [pl.BlockSpec(wc.shape, lambda i: (0, 0, 0, 0))],
        out_specs=seq(ts),
        out_shape=jax.ShapeDtypeStruct((b, h, s, dh), F32),
        scratch_shapes=[pltpu.VMEM((b * h, dh, dh), F32)],
        compiler_params=_cparams("arbitrary"),
        name="rwkv_scan",
    )(ta, tr, arb, yv, rt, v, bw, kw, wc)


def _out_proj_kernel(x_ref, on_ref, y_ref, bonus_ref, g_ref, lnw_ref, lnb_ref, bd_ref, wn_ref, wr_ref, o_ref):
    y = jnp.concatenate([y_ref[0, h] for h in range(RWKV_HEADS)], axis=-1)
    bd = bd_ref[...]
    yc = y - _dot(y, bd, precision=HIGHEST)
    yn = yc * lax.rsqrt(_dot(yc * yc, bd, precision=HIGHEST) + GN_EPS)
    o_rwkv = (yn * lnw_ref[...] + lnb_ref[...] + bonus_ref[...]) * g_ref[...]
    o_ref[...] = (x_ref[...] + _dot(on_ref[...].astype(BF16), wn_ref[...])
                  + _dot(o_rwkv.astype(BF16), wr_ref[...]))


def out_proj(x, o_nsa, y, bonus, g, ln_w, ln_b, w_out, s, tm):
    t, d = x.shape
    dn = o_nsa.shape[1]
    dr = bonus.shape[1]
    tps = s // tm
    row = lambda w: pl.BlockSpec((tm, w), lambda i: (i, 0))
    full = lambda *sh: pl.BlockSpec(sh, lambda i: (0,) * len(sh))
    return pl.pallas_call(
        _out_proj_kernel,
        grid=(t // tm,),
        in_specs=[row(d), row(dn),
                  pl.BlockSpec((1, RWKV_HEADS, tm, HEAD_DIM), lambda i: (i // tps, 0, i % tps, 0)),
                  row(dr), row(dr), full(1, dr), full(1, dr), full(dr, dr), full(dn, d), full(dr, d)],
        out_specs=row(d),
        out_shape=jax.ShapeDtypeStruct((t, d), F32),
        compiler_params=_cparams("parallel"),
        name="out_proj",
    )(x, o_nsa, y, bonus, g, ln_w.reshape(1, dr), ln_b.reshape(1, dr),
      _block_diag_ones(dr, HEAD_DIM, 1.0 / HEAD_DIM), w_out[:dn].astype(BF16), w_out[dn:].astype(BF16))


def _cross_attn_kernel(h_ref, g_ref, wq_ref, qg_ref, kv_ref, kg_ref, wo_ref, o_ref):
    h = h_ref[...]
    d = h.shape[1]
    xd = d // X_HEADS
    q = _dot(_rms(h, g_ref[...]).astype(BF16), wq_ref[...])
    kv = kv_ref[0]
    outs = []
    for hd in range(X_HEADS):
        qh = _rms(q[:, hd * xd:(hd + 1) * xd], qg_ref[...]) * (xd ** -0.5)
        kh = _rms(kv[:, hd * xd:(hd + 1) * xd], kg_ref[...])
        vh = kv[:, d + hd * xd:d + (hd + 1) * xd]
        s = _dot_nt(qh.astype(BF16), kh.astype(BF16))
        p = jnp.exp(s - jnp.max(s, axis=-1, keepdims=True))
        p = p / jnp.sum(p, axis=-1, keepdims=True)
        outs.append(_dot(p.astype(BF16), vh.astype(BF16)))
    o = jnp.concatenate(outs, axis=-1)
    o_ref[...] = h + _dot(o.astype(BF16), wo_ref[...])


def cross_attention(h, kv, norm_g, xq_w, xq_g, xk_g, xo_w, b, s, tm):
    t, d = h.shape
    m = kv.shape[1]
    xd = d // X_HEADS
    tiles = s // tm
    full = lambda *sh: pl.BlockSpec(sh, lambda i: (0,) * len(sh))
    return pl.pallas_call(
        _cross_attn_kernel,
        grid=(t // tm,),
        in_specs=[pl.BlockSpec((tm, d), lambda i: (i, 0)), full(1, d), full(d, d), full(1, xd),
                  pl.BlockSpec((1, m, 2 * d), lambda i: (i // tiles, 0, 0)), full(1, xd), full(d, d)],
        out_specs=pl.BlockSpec((tm, d), lambda i: (i, 0)),
        out_shape=jax.ShapeDtypeStruct((t, d), F32),
        compiler_params=_cparams("parallel"),
        name="cross_attention",
    )(h, norm_g.reshape(1, d), xq_w.astype(BF16), xq_g.reshape(1, xd), kv, xk_g.reshape(1, xd),
      xo_w.astype(BF16))


def _router_kernel(h_ref, g_ref, rw_ref, rb_ref, xn_o, idx_o, gate_o):
    xn = _rms(h_ref[...], g_ref[...])
    xn_o[...] = xn.astype(BF16)
    logits = _dot(xn, rw_ref[...], precision=HIGHEST) + rb_ref[...]
    tm = logits.shape[0]
    lane = lax.broadcasted_iota(jnp.int32, (tm, LANES), 1)
    lanef = lane.astype(F32)
    logits = jnp.where(lane < N_EXPERTS, logits, REMOVED)
    idx_acc = jnp.zeros((tm, LANES), F32)
    val_acc = jnp.zeros((tm, LANES), F32)
    top = None
    for k in range(TOP_K):
        m = jnp.max(logits, axis=-1, keepdims=True)
        idx = jnp.min(jnp.where(logits == m, lanef, float(LANES)), axis=-1, keepdims=True)
        logits = jnp.where(lanef == idx, REMOVED, logits)
        top = m if top is None else top
        idx_acc = jnp.where(lane == k, idx, idx_acc)
        val_acc = jnp.where(lane == k, jnp.exp(m - top), val_acc)
    idx_o[...] = idx_acc.astype(jnp.int32)
    gate_o[...] = val_acc / jnp.sum(val_acc, axis=-1, keepdims=True)


def moe_router(h, norm_g, router_w, router_b, tm):
    t, d = h.shape
    rw = jnp.zeros((d, LANES), F32).at[:, :N_EXPERTS].set(router_w)
    rb = jnp.zeros((1, LANES), F32).at[0, :N_EXPERTS].set(router_b)
    row = lambda w: pl.BlockSpec((tm, w), lambda i: (i, 0))
    full = lambda *s: pl.BlockSpec(s, lambda i: (0,) * len(s))
    return pl.pallas_call(
        _router_kernel,
        grid=(t // tm,),
        in_specs=[row(d), full(1, d), full(d, LANES), full(1, LANES)],
        out_specs=[row(d), row(LANES), row(LANES)],
        out_shape=[jax.ShapeDtypeStruct((t, d), BF16), jax.ShapeDtypeStruct((t, LANES), jnp.int32),
                   jax.ShapeDtypeStruct((t, LANES), F32)],
        compiler_params=_cparams("parallel"),
        name="moe_router",
    )(h, norm_g.reshape(1, d), rw, rb)


def _expert_kernel(blk_e_ref, n_used_ref, x_ref, w1g_ref, w1l_ref, b1g_ref, b1l_ref, w2_ref, b2_ref, rw_ref, o_ref):
    i = pl.program_id(0)

    @pl.when(i < n_used_ref[0])
    def _():
        x = x_ref[...]
        hg = jnp.minimum(_dot(x, w1g_ref[0]) + b1g_ref[0], SWIGLU_LIMIT)
        hl = jnp.clip(_dot(x, w1l_ref[0]) + b1l_ref[0], -SWIGLU_LIMIT, SWIGLU_LIMIT)
        act = hg * jax.nn.sigmoid(SWIGLU_ALPHA * hg) * (hl + 1.0)
        y = _dot(act.astype(BF16), w2_ref[0].astype(BF16)) + b2_ref[0]
        o_ref[...] = y * rw_ref[...]

    @pl.when(i >= n_used_ref[0])
    def _():
        o_ref[...] = jnp.zeros_like(o_ref)


def moe_experts(xs, row_w, blk_e, n_used, w1g, w1l, b1g, b1l, w2, b2):
    r, d = xs.shape
    f = w1g.shape[2]
    m = MOE_ROW_BLOCK
    ex = lambda *s: pl.BlockSpec((1,) + s, lambda i, be, nu: (be[i], 0, 0))
    grid_spec = pltpu.PrefetchScalarGridSpec(
        num_scalar_prefetch=2,
        grid=(r // m,),
        in_specs=[pl.BlockSpec((m, d), lambda i, be, nu: (i, 0)),
                  ex(d, f), ex(d, f), ex(1, f), ex(1, f), ex(f, d), ex(1, d),
                  pl.BlockSpec((m, 1), lambda i, be, nu: (i, 0))],
        out_specs=pl.BlockSpec((m, d), lambda i, be, nu: (i, 0)),
    )
    return pl.pallas_call(
        _expert_kernel,
        grid_spec=grid_spec,
        out_shape=jax.ShapeDtypeStruct((r, d), F32),
        compiler_params=_cparams("arbitrary"),
        name="moe_experts",
    )(blk_e, n_used, xs, w1g, w1l, b1g, b1l, w2, b2, row_w)


def _layer(x, mem, norm_mix_g, w_in, q_norm_g, k_cmp_norm_g, k_slc_norm_g, k_win_norm_g,
           cmp_pe_k, cmp_pe_v, cmp_k_w1, cmp_k_w2, cmp_v_w1, cmp_v_w2,
           rwkv_mu, rwkv_w0, rwkv_w_up, rwkv_a0, rwkv_a_up, rwkv_g_up, rwkv_k_k, rwkv_k_a,
           rwkv_r_k, rwkv_ln_w, rwkv_ln_b, w_out,
           norm_x_g, norm_mem_g, xq_w, xk_w, xv_w, xq_norm_g, xk_norm_g, xo_w,
           norm_ffn_g, router_w, router_b, mlp1_w, mlp1_b, mlp2_w, mlp2_b):
    b, s, d = x.shape
    t = b * s
    tm = 512
    xt = x.reshape(t, d)

    w_nsa = jnp.pad(w_in[:, :NSA_PROJ], ((0, 0), (0, NSA_PROJ_PAD - NSA_PROJ)))
    p_nsa = norm_matmul(xt, norm_mix_g, w_nsa, tm)
    p_rwkv = norm_matmul(xt, norm_mix_g, w_in[:, NSA_PROJ:], tm)

    nq = NSA_HEADS * HEAD_DIM
    gw = NSA_GROUPS * HEAD_DIM
    qn, ks, vs, kw, vw, gates = nsa_prep(p_nsa, q_norm_g, k_slc_norm_g, k_win_norm_g, tm)
    kcv = nsa_compress(p_nsa[:, nq:nq + gw], p_nsa[:, nq + gw:nq + 2 * gw], cmp_pe_k, cmp_pe_v,
                       cmp_k_w1, cmp_k_w2, cmp_v_w1, cmp_v_w2, k_cmp_norm_g, b, s)
    o_nsa = nsa_attention(qn, kcv, ks, vs, kw, vw, gates, b, s).transpose(0, 3, 1, 2).reshape(t, nq)

    at, bt, kt, rt, v, bw, kwd, wc, g_gate, bonus = rwkv_prep(
        p_rwkv, rwkv_mu, rwkv_w0, rwkv_w_up, rwkv_a0, rwkv_a_up, rwkv_g_up, rwkv_k_k, rwkv_k_a, rwkv_r_k, b, s, tm)
    ta, tr, arb, yv = rwkv_intra(at, bt, kt, rt, v, ts=256)
    y = rwkv_scan(ta, tr, arb, yv, rt, v, bw, kwd, wc, ts=256)

    h1 = out_proj(xt, o_nsa, y, bonus, g_gate, rwkv_ln_w, rwkv_ln_b, w_out, s, tm)
    m = mem.shape[1]
    kv = norm_matmul(mem.reshape(b * m, d), norm_mem_g, jnp.concatenate([xk_w, xv_w], axis=1), m)
    h2 = cross_attention(h1, kv.reshape(b, m, 2 * d), norm_x_g, xq_w, xq_norm_g, xk_norm_g, xo_w, b, s, tm)

    xn, top_i, gate = moe_router(h2, norm_ffn_g, router_w, router_b, tm)
    top_i = top_i[:, :TOP_K]
    gate = gate[:, :TOP_K]
    a = t * TOP_K
    mb = MOE_ROW_BLOCK
    e_flat = top_i.reshape(a)
    order = jnp.argsort(e_flat, stable=True)
    e_sorted = e_flat[order]
    counts = jnp.bincount(e_flat, length=N_EXPERTS)
    starts = jnp.cumsum(counts) - counts
    padded = (counts + mb - 1) // mb * mb
    pends = jnp.cumsum(padded)
    pstarts = pends - padded
    dest = (pstarts[e_sorted] + jnp.arange(a) - starts[e_sorted]).astype(jnp.int32)
    n_blocks = -(-a // mb) + N_EXPERTS
    r = n_blocks * mb
    row_tok = jnp.full((r,), t, jnp.int32).at[dest].set((order // TOP_K).astype(jnp.int32))
    row_w = jnp.zeros((r,), F32).at[dest].set(gate.reshape(a)[order])
    blk_e = jnp.minimum(jnp.searchsorted(pends, jnp.arange(n_blocks) * mb, side='right'),
                        N_EXPERTS - 1).astype(jnp.int32)
    n_used = (pends[-1] // mb).astype(jnp.int32).reshape(1)
    xpad = jnp.concatenate([xn, jnp.zeros((1, d), xn.dtype)], axis=0)
    xs = xpad[row_tok]
    f = mlp1_w.shape[2] // 2
    w1 = mlp1_w.reshape(N_EXPERTS, d, f, 2)
    b1 = mlp1_b.reshape(N_EXPERTS, 1, f, 2)
    ys = moe_experts(xs, row_w.reshape(r, 1), blk_e, n_used,
                     w1[..., 0].astype(BF16), w1[..., 1].astype(BF16), b1[..., 0], b1[..., 1],
                     mlp2_w, mlp2_b.reshape(N_EXPERTS, 1, d))
    pos = jnp.zeros((a,), jnp.int32).at[order].set(dest).reshape(t, TOP_K)
    out = h2 + jnp.sum(ys[pos], axis=1)
    return out.reshape(b, s, d)


def kernel(x, mem, norm_mix_g, w_in, q_norm_g, k_cmp_norm_g, k_slc_norm_g, k_win_norm_g, cmp_pe_k, cmp_pe_v, cmp_k_w1, cmp_k_w2, cmp_v_w1, cmp_v_w2, rwkv_mu, rwkv_w0, rwkv_w_up, rwkv_a0, rwkv_a_up, rwkv_g_up, rwkv_k_k, rwkv_k_a, rwkv_r_k, rwkv_ln_w, rwkv_ln_b, w_out, norm_x_g, norm_mem_g, xq_w, xk_w, xv_w, xq_norm_g, xk_norm_g, xo_w, norm_ffn_g, router_w, router_b, mlp1_w, mlp1_b, mlp2_w, mlp2_b):
    params = (norm_mix_g, w_in, q_norm_g, k_cmp_norm_g, k_slc_norm_g, k_win_norm_g, cmp_pe_k, cmp_pe_v,
              cmp_k_w1, cmp_k_w2, cmp_v_w1, cmp_v_w2, rwkv_mu, rwkv_w0, rwkv_w_up, rwkv_a0, rwkv_a_up,
              rwkv_g_up, rwkv_k_k, rwkv_k_a, rwkv_r_k, rwkv_ln_w, rwkv_ln_b, w_out, norm_x_g, norm_mem_g,
              xq_w, xk_w, xv_w, xq_norm_g, xk_norm_g, xo_w, norm_ffn_g, router_w, router_b,
              mlp1_w, mlp1_b, mlp2_w, mlp2_b)
    h = x
    for layer in range(norm_mix_g.shape[0]):
        h = _layer(h, mem, *[prm[layer] for prm in params])
    return h
```

```python
import functools

import numpy as np
import jax
import jax.numpy as jnp
from jax import lax
from jax.experimental import pallas as pl
from jax.experimental.pallas import tpu as pltpu

F32 = jnp.float32
BF16 = jnp.bfloat16
HIGHEST = lax.Precision.HIGHEST

V7X_VMEM_BYTES = 64 * 1024 * 1024
VMEM_LIMIT = V7X_VMEM_BYTES * 3 // 4

HEAD_DIM = 64
NSA_HEADS = 8
NSA_GROUPS = 2
NSA_HPG = NSA_HEADS // NSA_GROUPS
GROUP_W = NSA_HPG * HEAD_DIM
CMP_BLOCK = 32
CMP_STRIDE = 16
SLC_BLOCK = 64
SLC_TOPK = 16
WINDOW = 512
Q_BLOCK = 128
SEL_CHUNK = 512
RWKV_HEADS = 8
RWKV_DIM = RWKV_HEADS * HEAD_DIM
RWKV_CHUNK = 64
GN_EPS = HEAD_DIM * 1e-5
X_HEADS = 4
N_EXPERTS = 32
TOP_K = 4
SWIGLU_LIMIT = 7.0
SWIGLU_ALPHA = 1.702
MOE_ROW_BLOCK = 256
RMS_EPS = 1e-6
NEG_INF = -1e30
BIG = 1e9
REMOVED = -3e38
LANES = 128

NSA_PROJ = NSA_HEADS * HEAD_DIM + 6 * NSA_GROUPS * HEAD_DIM + NSA_HEADS * 3
NSA_PROJ_PAD = -(-NSA_PROJ // LANES) * LANES
RWKV_PROJ = 3 * RWKV_DIM + 64 + 64 + 128


def _cparams(*sem):
    return pltpu.CompilerParams(dimension_semantics=sem, vmem_limit_bytes=VMEM_LIMIT)


def _dot(a, b, **kw):
    return jnp.dot(a, b, preferred_element_type=F32, **kw)


def _dot_nt(a, b, **kw):
    return lax.dot_general(a, b, (((1,), (1,)), ((), ())), preferred_element_type=F32, **kw)


def _dot_tn(a, b, **kw):
    return lax.dot_general(a, b, (((0,), (0,)), ((), ())), preferred_element_type=F32, **kw)


def _rms(x, g):
    return x * lax.rsqrt(jnp.mean(x * x, axis=-1, keepdims=True) + RMS_EPS) * g


def _block_diag_ones(n, blk, scale=1.0):
    i = np.arange(n)
    return jnp.asarray(((i[:, None] // blk) == (i[None, :] // blk)).astype(np.float32) * scale)


def _norm_matmul_kernel(x_ref, g_ref, w_ref, o_ref):
    xn = _rms(x_ref[...], g_ref[...]).astype(BF16)
    o_ref[...] = _dot(xn, w_ref[...])


def norm_matmul(x, g, w, tm):
    m, d = x.shape
    n = w.shape[1]
    return pl.pallas_call(
        _norm_matmul_kernel,
        grid=(m // tm,),
        in_specs=[pl.BlockSpec((tm, d), lambda i: (i, 0)),
                  pl.BlockSpec((1, d), lambda i: (0, 0)),
                  pl.BlockSpec((d, n), lambda i: (0, 0))],
        out_specs=pl.BlockSpec((tm, n), lambda i: (i, 0)),
        out_shape=jax.ShapeDtypeStruct((m, n), F32),
        compiler_params=_cparams("parallel"),
        name="norm_matmul",
    )(x, g.reshape(1, d), w.astype(BF16))


def _nsa_prep_kernel(p_ref, qg_ref, ksg_ref, kwg_ref, bdq_ref, bdk_ref, q_o, ks_o, vs_o, kw_o, vw_o, gate_o):
    p = p_ref[...]
    nq = NSA_HEADS * HEAD_DIM
    gw = NSA_GROUPS * HEAD_DIM
    q = p[:, :nq]
    msq = _dot(q * q, bdq_ref[...], precision=HIGHEST)
    q_o[...] = (q * lax.rsqrt(msq + RMS_EPS) * qg_ref[...] * (HEAD_DIM ** -0.5)).astype(BF16)

    def seg(k):
        return p[:, nq + k * gw: nq + (k + 1) * gw]

    def head_norm(t, g):
        ms = _dot(t * t, bdk_ref[...], precision=HIGHEST)
        return t * lax.rsqrt(ms + RMS_EPS) * g

    ks_o[...] = head_norm(seg(2), ksg_ref[...]).astype(BF16)
    vs_o[...] = seg(3).astype(BF16)
    kw_o[...] = head_norm(seg(4), kwg_ref[...]).astype(BF16)
    vw_o[...] = seg(5).astype(BF16)
    gate_o[...] = jax.nn.sigmoid(p[:, nq + 6 * gw: nq + 6 * gw + LANES])


def nsa_prep(p_nsa, q_g, ks_g, kw_g, tm):
    t = p_nsa.shape[0]
    nq = NSA_HEADS * HEAD_DIM
    gw = NSA_GROUPS * HEAD_DIM
    tile = lambda v, n: jnp.tile(v.reshape(1, HEAD_DIM), (1, n))
    row = lambda w: pl.BlockSpec((tm, w), lambda i: (i, 0))
    full = lambda *s: pl.BlockSpec(s, lambda i: (0,) * len(s))
    return pl.pallas_call(
        _nsa_prep_kernel,
        grid=(t // tm,),
        in_specs=[row(NSA_PROJ_PAD), full(1, nq), full(1, gw), full(1, gw), full(nq, nq), full(gw, gw)],
        out_specs=[row(nq)] + [row(gw)] * 4 + [row(LANES)],
        out_shape=[jax.ShapeDtypeStruct((t, nq), BF16)] + [jax.ShapeDtypeStruct((t, gw), BF16)] * 4
                  + [jax.ShapeDtypeStruct((t, LANES), F32)],
        compiler_params=_cparams("parallel"),
        name="nsa_prep",
    )(p_nsa, tile(q_g, NSA_HEADS), tile(ks_g, NSA_GROUPS), tile(kw_g, NSA_GROUPS),
      _block_diag_ones(nq, HEAD_DIM, 1.0 / HEAD_DIM), _block_diag_ones(gw, HEAD_DIM, 1.0 / HEAD_DIM))


def _compress_kernel(ch_ref, pe_ref, w1_ref, w2_ref, g_ref, o_ref):
    ch = ch_ref[0, 0, 0]
    nc = ch.shape[0]
    half = CMP_STRIDE * HEAD_DIM
    nxt = pltpu.roll(ch, nc - 1, 0)
    w1 = w1_ref[0]
    h1 = (_dot(ch, w1[:half], precision=HIGHEST) + _dot(nxt, w1[half:], precision=HIGHEST)
          + _dot(pe_ref[0], w1, precision=HIGHEST))
    out = _dot(jax.nn.silu(h1), w2_ref[0], precision=HIGHEST)
    out = jnp.where(pl.program_id(0) == 0, _rms(out, g_ref[...]), out)
    o_ref[0, 0, 0] = out.astype(BF16)


def nsa_compress(kc, vc, pe_k, pe_v, kw1, kw2, vw1, vw2, kc_g, b, s):
    nc = s // CMP_STRIDE
    half = CMP_STRIDE * HEAD_DIM

    def chunks(t):
        return t.reshape(b, nc, CMP_STRIDE, NSA_GROUPS, HEAD_DIM).transpose(0, 3, 1, 2, 4).reshape(
            b, NSA_GROUPS, nc, half)

    ch = jnp.stack([chunks(kc), chunks(vc)])
    pe = jnp.stack([pe_k.reshape(1, 2 * half), pe_v.reshape(1, 2 * half)])
    return pl.pallas_call(
        _compress_kernel,
        grid=(2, b, NSA_GROUPS),
        in_specs=[pl.BlockSpec((1, 1, 1, nc, half), lambda kv, bi, g: (kv, bi, g, 0, 0)),
                  pl.BlockSpec((1, 1, 2 * half), lambda kv, bi, g: (kv, 0, 0)),
                  pl.BlockSpec((1, 2 * half, HEAD_DIM), lambda kv, bi, g: (kv, 0, 0)),
                  pl.BlockSpec((1, HEAD_DIM, HEAD_DIM), lambda kv, bi, g: (kv, 0, 0)),
                  pl.BlockSpec((1, HEAD_DIM), lambda kv, bi, g: (0, 0))],
        out_specs=pl.BlockSpec((1, 1, 1, nc, HEAD_DIM), lambda kv, bi, g: (kv, bi, g, 0, 0)),
        out_shape=jax.ShapeDtypeStruct((2, b, NSA_GROUPS, nc, HEAD_DIM), BF16),
        compiler_params=_cparams("parallel", "parallel", "parallel"),
        name="nsa_compress",
    )(ch, pe, jnp.stack([kw1, vw1]), jnp.stack([kw2, vw2]), kc_g.reshape(1, HEAD_DIM))


def _masked_exp_cols(s, mask):
    sm = jnp.where(mask, s, NEG_INF)
    p = jnp.where(mask, jnp.exp(sm - jnp.max(sm, axis=0, keepdims=True)), 0.0)
    l = jnp.sum(p, axis=0, keepdims=True)
    return p, 1.0 / jnp.where(l > 0.0, l, 1.0)


def _nsa_attn_kernel(qt_ref, kc_ref, vct_ref, ovt_ref, ks_ref, vst_ref, kw_ref, vwt_ref, gate_ref, o_ref,
                     *, n_sel, n_top):
    g = pl.program_id(1)
    t0 = pl.program_id(2) * Q_BLOCK
    ks_ref, vst_ref, kw_ref, vwt_ref = (r.at[0, 0] for r in (ks_ref, vst_ref, kw_ref, vwt_ref))
    cols = NSA_HPG * Q_BLOCK
    col = lax.broadcasted_iota(jnp.int32, (1, cols), 1)
    tq = t0 + col % Q_BLOCK
    head = g * NSA_HPG + col // Q_BLOCK
    slope = lax.bitcast_convert_type((127 - (head + 1)) << 23, F32)
    qt = jnp.concatenate([qt_ref[0, h] for h in range(NSA_HPG)], axis=1)
    frow = lax.broadcasted_iota(jnp.int32, (HEAD_DIM, cols), 0)
    tq_hi = (tq // SLC_BLOCK * SLC_BLOCK).astype(F32)
    tq_lo = (tq % SLC_BLOCK).astype(F32)
    qpos = jnp.where(frow < 2, slope, jnp.where(frow == 2, -slope * tq_hi, jnp.where(frow == 3, -slope * tq_lo, 0.0)))
    qc = jnp.concatenate([qt, qpos.astype(BF16)], axis=0)

    def heads_sum(x):
        acc = x[:, :Q_BLOCK]
        for h in range(1, NSA_HPG):
            acc = acc + x[:, h * Q_BLOCK:(h + 1) * Q_BLOCK]
        return acc

    kc = kc_ref[0, 0]
    nc = kc.shape[0]
    c_last = lax.broadcasted_iota(jnp.int32, (nc, 1), 0) * CMP_STRIDE + (CMP_BLOCK - 1)
    p_c, inv_c = _masked_exp_cols(_dot(kc, qc), c_last <= tq)
    p_c = p_c * inv_c
    o_c = _dot(vct_ref[0, 0], p_c.astype(BF16))
    imp = _dot(ovt_ref[...], heads_sum(p_c), precision=HIGHEST)

    bid = lax.broadcasted_iota(jnp.int32, (LANES, Q_BLOCK), 0)
    bidf = bid.astype(F32)
    tq1 = t0 + lax.broadcasted_iota(jnp.int32, (1, Q_BLOCK), 1)
    cur = tq1 // SLC_BLOCK
    forced = (bid == 0) | (bid == cur) | (bid == cur - 1)
    score = jnp.where(forced, BIG, jnp.where(bid * SLC_BLOCK <= tq1, imp, -BIG))
    score = jnp.where(bid < n_sel, score, REMOVED)
    picked = jnp.zeros((LANES, Q_BLOCK), jnp.bool_)
    for _ in range(n_top):
        m = jnp.max(score, axis=0, keepdims=True)
        pick = bidf == jnp.min(jnp.where(score == m, bidf, float(LANES)), axis=0, keepdims=True)
        picked = picked | pick
        score = jnp.where(pick, REMOVED, score)
    sel_neg = jnp.where(picked, 0.0, NEG_INF).astype(BF16)
    qx = jnp.concatenate([qc, jnp.concatenate([sel_neg] * NSA_HPG, axis=1)], axis=0)

    def values_t(ref, first_tile, n_tiles):
        return jnp.concatenate([ref[first_tile + i] for i in range(n_tiles)], axis=1)

    span = WINDOW + Q_BLOCK
    ws = pl.multiple_of(jnp.maximum(t0 - WINDOW, 0), Q_BLOCK)
    dw = tq - (ws + lax.broadcasted_iota(jnp.int32, (span, 1), 0))
    p_w, inv_w = _masked_exp_cols(_dot(kw_ref[pl.ds(ws, span), :], qc), (dw >= 0) & (dw < WINDOW))
    o_w = _dot(values_t(vwt_ref, ws // Q_BLOCK, span // Q_BLOCK), p_w.astype(BF16)) * inv_w

    krow = lax.broadcasted_iota(jnp.int32, (SEL_CHUNK, 1), 0)

    def flash(s, start, carry):
        m, l, acc = carry
        m_new = jnp.maximum(m, jnp.max(s, axis=0, keepdims=True))
        alpha = jnp.exp(m - m_new)
        p = jnp.exp(s - m_new)
        l = alpha * l + jnp.sum(p, axis=0, keepdims=True)
        v = values_t(vst_ref, start // Q_BLOCK, SEL_CHUNK // Q_BLOCK)
        return m_new, l, alpha * acc + _dot(v, p.astype(BF16))

    def full_step(j, carry):
        start = pl.multiple_of(j * SEL_CHUNK, SEL_CHUNK)
        return flash(_dot(ks_ref[pl.ds(start, SEL_CHUNK), :], qx), start, carry)

    n_full = t0 // SEL_CHUNK
    init = (jnp.full((1, cols), NEG_INF, F32), jnp.zeros((1, cols), F32), jnp.zeros((HEAD_DIM, cols), F32))
    carry = lax.fori_loop(0, n_full, full_step, init)
    start = pl.multiple_of(n_full * SEL_CHUNK, SEL_CHUNK)
    s_diag = jnp.where(start + krow <= tq, _dot(ks_ref[pl.ds(start, SEL_CHUNK), :], qx), NEG_INF)
    _, l_s, acc_s = flash(s_diag, start, carry)
    o_s = acc_s / l_s

    for h in range(NSA_HPG):
        hs = slice(h * Q_BLOCK, (h + 1) * Q_BLOCK)
        gate = lambda br: gate_ref[0, 0, br, h:h + 1, :]
        o_ref[0, h] = gate(0) * o_c[:, hs] + gate(1) * o_s[:, hs] + gate(2) * o_w[:, hs]


def nsa_attention(qn, kcv, ks, vs, kw, vw, gates, b, s):
    nq_blocks = s // Q_BLOCK
    nc = s // CMP_STRIDE
    n_cmp = nc - CMP_BLOCK // CMP_STRIDE + 1
    n_sel = s // SLC_BLOCK
    n_top = min(SLC_TOPK, n_sel)
    grp, dh = NSA_GROUPS, HEAD_DIM
    assert n_sel <= LANES and s % SEL_CHUNK == 0 and s >= WINDOW + Q_BLOCK
    c_start = np.arange(n_cmp) * CMP_STRIDE
    s_start = np.arange(n_sel) * SLC_BLOCK
    ovt = np.zeros((LANES, nc), np.float32)
    ovt[:n_sel, :n_cmp] = (np.clip(np.minimum((c_start + CMP_BLOCK)[:, None], s_start[None] + SLC_BLOCK)
                                   - np.maximum(c_start[:, None], s_start[None]), 0, None) / CMP_BLOCK).T

    def pos_features(pos):
        f = np.zeros((len(pos), dh), np.float32)
        f[:, 0] = pos // SLC_BLOCK * SLC_BLOCK
        f[:, 1] = pos % SLC_BLOCK
        f[:, 2:4] = 1.0
        return jnp.asarray(f, BF16)

    tok = np.arange(s)
    onehot = jnp.asarray(tok[:, None] // SLC_BLOCK == np.arange(LANES)[None], BF16)
    per_group = lambda a: a.reshape(b, s, grp, dh).transpose(0, 2, 1, 3)
    with_feats = lambda k, *f: jnp.concatenate(
        [k] + [jnp.broadcast_to(x, k.shape[:2] + x.shape) for x in f], axis=-1)
    tiles_t = lambda v: v.reshape(b, s // Q_BLOCK, Q_BLOCK, grp, dh).transpose(0, 3, 1, 4, 2)
    qt = qn.reshape(b, s, NSA_HEADS, dh).transpose(0, 2, 3, 1)
    ks_x = with_feats(per_group(ks), pos_features(tok), onehot)
    kw_x = with_feats(per_group(kw), pos_features(tok))
    kc_x = with_feats(kcv[0], pos_features(np.arange(nc) * CMP_STRIDE + (CMP_BLOCK - 1)))
    vct = kcv[1].transpose(0, 1, 3, 2)
    gates_t = gates[:, :NSA_HEADS * 3].reshape(b, s, grp, NSA_HPG, 3).transpose(0, 2, 4, 3, 1)

    grp_spec = lambda *shape: pl.BlockSpec((1, 1) + shape, lambda bi, g, i: (bi, g) + (0,) * len(shape),
                                           pipeline_mode=pl.Buffered(1))
    return pl.pallas_call(
        functools.partial(_nsa_attn_kernel, n_sel=n_sel, n_top=n_top),
        grid=(b, grp, nq_blocks),
        in_specs=[pl.BlockSpec((1, NSA_HPG, dh, Q_BLOCK), lambda bi, g, i: (bi, g, 0, i)),
                  grp_spec(nc, 2 * dh), grp_spec(dh, nc),
                  pl.BlockSpec((LANES, nc), lambda bi, g, i: (0, 0)),
                  grp_spec(s, 2 * dh + LANES), grp_spec(s // Q_BLOCK, dh, Q_BLOCK),
                  grp_spec(s, 2 * dh), grp_spec(s // Q_BLOCK, dh, Q_BLOCK),
                  pl.BlockSpec((1, 1, 3, NSA_HPG, Q_BLOCK), lambda bi, g, i: (bi, g, 0, 0, i))],
        out_specs=pl.BlockSpec((1, NSA_HPG, dh, Q_BLOCK), lambda bi, g, i: (bi, g, 0, i)),
        out_shape=jax.ShapeDtypeStruct((b, NSA_HEADS, dh, s), F32),
        compiler_params=_cparams("parallel", "parallel", "arbitrary"),
        name="nsa_attention",
    )(qt, kc_x, vct, jnp.asarray(ovt), ks_x, tiles_t(vs), kw_x, tiles_t(vw), gates_t)


def _rwkv_prep_kernel(p_ref, prev_ref, mu_ref, w0_ref, a0_ref, kk_ref, ka_ref, rk_ref,
                      wup_ref, aup_ref, gup_ref, bd_ref, ltri_ref, lones_ref, csum_ref,
                      at_o, bt_o, kt_o, rt_o, v_o, bw_o, kw_o, wc_o, g_o, bonus_o, *, tiles_per_seq):
    p = p_ref[...]
    tm = p.shape[0]
    first = pl.program_id(0) % tiles_per_seq == 0
    last_prev = jnp.where(first, 0.0, prev_ref[7:8, :])
    prev = pltpu.roll(p, 1, 0)
    prev = jnp.where(lax.broadcasted_iota(jnp.int32, (tm, 1), 0) == 0, last_prev, prev)
    pm = p + (prev - p) * mu_ref[...]
    d = RWKV_DIM
    r, k, v = pm[:, :d], pm[:, d:2 * d], pm[:, 2 * d:3 * d]
    lora = pm[:, 3 * d:3 * d + LANES]
    gd = pm[:, 3 * d + LANES:3 * d + 2 * LANES]
    z = -(w0_ref[...] + _dot(jnp.tanh(lora).astype(BF16), wup_ref[...]))
    softplus = jnp.maximum(z, 0.0) + jnp.log(1.0 + jnp.exp(-jnp.abs(z)))
    w = -softplus - 0.5
    a = jax.nn.sigmoid(a0_ref[...] + _dot(lora.astype(BF16), aup_ref[...]))
    g_o[...] = _dot(jax.nn.sigmoid(gd).astype(BF16), gup_ref[...])
    bd = bd_ref[...]
    kkr = k * kk_ref[...]
    kk = kkr / jnp.maximum(jnp.sqrt(_dot(kkr * kkr, bd, precision=HIGHEST)), 1e-12)
    k2 = k * (1.0 + (a - 1.0) * ka_ref[...])
    bonus_o[...] = _dot(r * k2 * rk_ref[...], bd, precision=HIGHEST) * v
    lw = -jnp.exp(w)
    cum = _dot(ltri_ref[...], lw, precision=HIGHEST)
    tot = _dot(lones_ref[...], lw, precision=HIGHEST)
    e_in = jnp.exp(cum)
    e_out = jnp.exp(-cum)
    e_end = jnp.exp(tot - cum)

    def put_heads(o, val):
        for h in range(RWKV_HEADS):
            o[0, h] = val[:, h * HEAD_DIM:(h + 1) * HEAD_DIM].astype(o.dtype)

    put_heads(at_o, -kk * jnp.exp(cum - lw))
    put_heads(bt_o, kk * a * e_out)
    put_heads(kt_o, k2 * e_out)
    put_heads(rt_o, r * e_in)
    put_heads(v_o, v)
    put_heads(bw_o, kk * a * e_end)
    put_heads(kw_o, k2 * e_end)
    put_heads(wc_o, jnp.exp(_dot(csum_ref[...], lw, precision=HIGHEST)))


def rwkv_prep(p_rwkv, mu, w0, w_up, a0, a_up, g_up, k_k, k_a, r_k, b, s, tm):
    t = p_rwkv.shape[0]
    d = RWKV_DIM
    c = RWKV_CHUNK
    tps = s // tm
    cpt = tm // c
    wup = jnp.concatenate([w_up, jnp.zeros_like(a_up)], axis=0).astype(BF16)
    aup = jnp.concatenate([jnp.zeros_like(w_up), a_up], axis=0).astype(BF16)
    i = np.arange(tm)
    same = (i[:, None] // c) == (i[None, :] // c)
    ltri = jnp.asarray((same & (i[:, None] >= i[None, :])).astype(np.float32))
    lones = jnp.asarray(same.astype(np.float32))
    csum = jnp.asarray((np.arange(cpt)[:, None] == (i[None, :] // c)).astype(np.float32))
    row = lambda w: pl.BlockSpec((tm, w), lambda i: (i, 0))
    full = lambda *sh: pl.BlockSpec(sh, lambda i: (0,) * len(sh))
    heads = lambda n: pl.BlockSpec((1, RWKV_HEADS, n, HEAD_DIM), lambda i: (i // tps, 0, i % tps, 0))
    hshape = lambda n, dt: jax.ShapeDtypeStruct((b, RWKV_HEADS, n, HEAD_DIM), dt)
    vec = lambda x: x.reshape(1, -1)
    return pl.pallas_call(
        functools.partial(_rwkv_prep_kernel, tiles_per_seq=tps),
        grid=(t // tm,),
        in_specs=[row(RWKV_PROJ),
                  pl.BlockSpec((8, RWKV_PROJ), lambda i: (jnp.maximum(i * (tm // 8) - 1, 0), 0)),
                  full(1, RWKV_PROJ), full(1, d), full(1, d), full(1, d), full(1, d), full(1, d),
                  full(LANES, d), full(LANES, d), full(LANES, d), full(d, d), full(tm, tm), full(tm, tm),
                  full(cpt, tm)],
        out_specs=[heads(tm)] * 7 + [heads(cpt), row(d), row(d)],
        out_shape=[hshape(s, BF16)] * 7 + [hshape(s // c, F32)] + [jax.ShapeDtypeStruct((t, d), F32)] * 2,
        compiler_params=_cparams("parallel"),
        name="rwkv_prep",
    )(p_rwkv, p_rwkv, vec(mu), vec(w0), vec(a0), vec(k_k), vec(k_a), vec(r_k), wup, aup, g_up.astype(BF16),
      _block_diag_ones(d, HEAD_DIM), ltri, lones, csum)


def _bdot(a, b):
    return lax.dot_general(a, b, (((2,), (1,)), ((0,), (0,))), preferred_element_type=F32)


def _bdot_nt(a, b):
    return lax.dot_general(a, b, (((2,), (2,)), ((0,), (0,))), preferred_element_type=F32)


def _bdot_tn(a, b):
    return lax.dot_general(a, b, (((1,), (1,)), ((0,), (0,))), preferred_element_type=F32)


def _rwkv_intra_kernel(at_ref, bt_ref, kt_ref, rt_ref, v_ref, ta_o, tr_o, arb_o, yv_o):
    c = RWKV_CHUNK
    _, nh, ts, dh = at_ref.shape
    n = nh * (ts // c)
    chunked = lambda ref: ref[0].reshape(n, c, dh)
    at, bt, kt, rt, v = (chunked(r) for r in (at_ref, bt_ref, kt_ref, rt_ref, v_ref))
    ri = lax.broadcasted_iota(jnp.int32, (1, c, c), 1)
    ci = lax.broadcasted_iota(jnp.int32, (1, c, c), 2)
    strict = ri > ci
    incl = ri >= ci
    ar = jnp.concatenate([at, rt], axis=1)
    xb = _bdot_nt(ar, bt)
    xk = _bdot_nt(ar, kt)
    l_ab = jnp.where(strict, xb[:, :c], 0.0)
    a_ak = jnp.where(strict, xk[:, :c], 0.0)
    a_rb = jnp.where(incl, xb[:, c:], 0.0)
    a_rk = jnp.where(incl, xk[:, c:], 0.0)
    pw = l_ab
    tinv = jnp.where(ri == ci, 1.0, 0.0) + l_ab
    for _ in range(int(np.log2(c)) - 1):
        pw_b = pw.astype(BF16)
        pw = _bdot(pw_b, pw_b)
        tinv = tinv + _bdot(tinv.astype(BF16), pw.astype(BF16))
    tinv_b = tinv.astype(BF16)

    def put(o, val):
        o[0] = val.reshape(nh, ts, val.shape[-1]).astype(o.dtype)

    put(ta_o, _bdot(tinv_b, at))
    put(tr_o, _bdot(tinv_b, _bdot(a_ak.astype(BF16), v).astype(BF16)))
    put(arb_o, a_rb)
    put(yv_o, _bdot(a_rk.astype(BF16), v))


def rwkv_intra(at, bt, kt, rt, v, ts):
    b, h, s, dh = at.shape
    seq = lambda: pl.BlockSpec((1, h, ts, dh), lambda bi, i: (bi, 0, i, 0))
    shp = lambda dt: jax.ShapeDtypeStruct((b, h, s, dh), dt)
    return pl.pallas_call(
        _rwkv_intra_kernel,
        grid=(b, s // ts),
        in_specs=[seq()] * 5,
        out_specs=[seq()] * 4,
        out_shape=[shp(BF16), shp(F32), shp(BF16), shp(F32)],
        compiler_params=_cparams("parallel", "parallel"),
        name="rwkv_intra",
    )(at, bt, kt, rt, v)


def _rwkv_scan_kernel(ta_ref, tr_ref, arb_ref, yv_ref, rt_ref, v_ref, bw_ref, kw_ref, wc_ref, y_ref, st_ref):
    c = RWKV_CHUNK
    nb, nh, ts, dh = ta_ref.shape
    n = nb * nh

    @pl.when(pl.program_id(0) == 0)
    def _():
        st_ref[...] = jnp.zeros_like(st_ref)

    def chunk_step(j, _):
        sl = (slice(None), slice(None), pl.ds(pl.multiple_of(j * c, c), c), slice(None))
        get = lambda ref: ref[sl].reshape(n, c, dh)
        st = st_ref[...]
        st_b = st.astype(BF16)
        u = _bdot_nt(get(ta_ref), st_b) + get(tr_ref)
        u_b = u.astype(BF16)
        y = _bdot_nt(get(rt_ref), st_b) + _bdot(get(arb_ref), u_b) + get(yv_ref)
        wc = wc_ref[:, :, pl.ds(pl.program_id(0) * (ts // c) + j, 1), :].reshape(n, 1, dh)
        st_ref[...] = st * wc + _bdot_tn(jnp.concatenate([u_b, get(v_ref)], axis=1),
                                         jnp.concatenate([get(bw_ref), get(kw_ref)], axis=1))
        y_ref[sl] = y.reshape(nb, nh, c, dh)
        return 0

    lax.fori_loop(0, ts // c, chunk_step, 0)


def rwkv_scan(ta, tr, arb, yv, rt, v, bw, kw, wc, ts):
    b, h, s, dh = ta.shape
    seq = lambda n: pl.BlockSpec((b, h, n, dh), lambda i: (0, 0, i, 0))
    return pl.pallas_call(
        _rwkv_scan_kernel,
        grid=(s // ts,),
        in_specs=[seq(ts)] * 8 + [pl.BlockSpec(wc.shape, lambda i: (0, 0, 0, 0))],
        out_specs=seq(ts),
        out_shape=jax.ShapeDtypeStruct((b, h, s, dh), F32),
        scratch_shapes=[pltpu.VMEM((b * h, dh, dh), F32)],
        compiler_params=_cparams("arbitrary"),
        name="rwkv_scan",
    )(ta, tr, arb, yv, rt, v, bw, kw, wc)


def _out_proj_kernel(x_ref, on_ref, y_ref, bonus_ref, g_ref, lnw_ref, lnb_ref, bd_ref, wn_ref, wr_ref, o_ref):
    y = jnp.concatenate([y_ref[0, h] for h in range(RWKV_HEADS)], axis=-1)
    bd = bd_ref[...]
    yc = y - _dot(y, bd, precision=HIGHEST)
    yn = yc * lax.rsqrt(_dot(yc * yc, bd, precision=HIGHEST) + GN_EPS)
    o_rwkv = (yn * lnw_ref[...] + lnb_ref[...] + bonus_ref[...]) * g_ref[...]
    o_ref[...] = (x_ref[...] + _dot(on_ref[...].astype(BF16), wn_ref[...])
                  + _dot(o_rwkv.astype(BF16), wr_ref[...]))


def out_proj(x, o_nsa, y, bonus, g, ln_w, ln_b, w_out, s, tm):
    t, d = x.shape
    dn = o_nsa.shape[1]
    dr = bonus.shape[1]
    tps = s // tm
    row = lambda w: pl.BlockSpec((tm, w), lambda i: (i, 0))
    full = lambda *sh: pl.BlockSpec(sh, lambda i: (0,) * len(sh))
    return pl.pallas_call(
        _out_proj_kernel,
        grid=(t // tm,),
        in_specs=[row(d), row(dn),
                  pl.BlockSpec((1, RWKV_HEADS, tm, HEAD_DIM), lambda i: (i // tps, 0, i % tps, 0)),
                  row(dr), row(dr), full(1, dr), full(1, dr), full(dr, dr), full(dn, d), full(dr, d)],
        out_specs=row(d),
        out_shape=jax.ShapeDtypeStruct((t, d), F32),
        compiler_params=_cparams("parallel"),
        name="out_proj",
    )(x, o_nsa, y, bonus, g, ln_w.reshape(1, dr), ln_b.reshape(1, dr),
      _block_diag_ones(dr, HEAD_DIM, 1.0 / HEAD_DIM), w_out[:dn].astype(BF16), w_out[dn:].astype(BF16))


def _cross_attn_kernel(h_ref, g_ref, wq_ref, qg_ref, kv_ref, kg_ref, wo_ref, o_ref):
    h = h_ref[...]
    d = h.shape[1]
    xd = d // X_HEADS
    q = _dot(_rms(h, g_ref[...]).astype(BF16), wq_ref[...])
    kv = kv_ref[0]
    outs = []
    for hd in range(X_HEADS):
        qh = _rms(q[:, hd * xd:(hd + 1) * xd], qg_ref[...]) * (xd ** -0.5)
        kh = _rms(kv[:, hd * xd:(hd + 1) * xd], kg_ref[...])
        vh = kv[:, d + hd * xd:d + (hd + 1) * xd]
        s = _dot_nt(qh.astype(BF16), kh.astype(BF16))
        p = jnp.exp(s - jnp.max(s, axis=-1, keepdims=True))
        p = p / jnp.sum(p, axis=-1, keepdims=True)
        outs.append(_dot(p.astype(BF16), vh.astype(BF16)))
    o = jnp.concatenate(outs, axis=-1)
    o_ref[...] = h + _dot(o.astype(BF16), wo_ref[...])


def cross_attention(h, kv, norm_g, xq_w, xq_g, xk_g, xo_w, b, s, tm):
    t, d = h.shape
    m = kv.shape[1]
    xd = d // X_HEADS
    tiles = s // tm
    full = lambda *sh: pl.BlockSpec(sh, lambda i: (0,) * len(sh))
    return pl.pallas_call(
        _cross_attn_kernel,
        grid=(t // tm,),
        in_specs=[pl.BlockSpec((tm, d), lambda i: (i, 0)), full(1, d), full(d, d), full(1, xd),
                  pl.BlockSpec((1, m, 2 * d), lambda i: (i // tiles, 0, 0)), full(1, xd), full(d, d)],
        out_specs=pl.BlockSpec((tm, d), lambda i: (i, 0)),
        out_shape=jax.ShapeDtypeStruct((t, d), F32),
        compiler_params=_cparams("parallel"),
        name="cross_attention",
    )(h, norm_g.reshape(1, d), xq_w.astype(BF16), xq_g.reshape(1, xd), kv, xk_g.reshape(1, xd),
      xo_w.astype(BF16))


def _router_kernel(h_ref, g_ref, rw_ref, rb_ref, xn_o, idx_o, gate_o):
    xn = _rms(h_ref[...], g_ref[...])
    xn_o[...] = xn.astype(BF16)
    logits = _dot(xn, rw_ref[...], precision=HIGHEST) + rb_ref[...]
    tm = logits.shape[0]
    lane = lax.broadcasted_iota(jnp.int32, (tm, LANES), 1)
    lanef = lane.astype(F32)
    logits = jnp.where(lane < N_EXPERTS, logits, REMOVED)
    idx_acc = jnp.zeros((tm, LANES), F32)
    val_acc = jnp.zeros((tm, LANES), F32)
    top = None
    for k in range(TOP_K):
        m = jnp.max(logits, axis=-1, keepdims=True)
        idx = jnp.min(jnp.where(logits == m, lanef, float(LANES)), axis=-1, keepdims=True)
        logits = jnp.where(lanef == idx, REMOVED, logits)
        top = m if top is None else top
        idx_acc = jnp.where(lane == k, idx, idx_acc)
        val_acc = jnp.where(lane == k, jnp.exp(m - top), val_acc)
    idx_o[...] = idx_acc.astype(jnp.int32)
    gate_o[...] = val_acc / jnp.sum(val_acc, axis=-1, keepdims=True)


def moe_router(h, norm_g, router_w, router_b, tm):
    t, d = h.shape
    rw = jnp.zeros((d, LANES), F32).at[:, :N_EXPERTS].set(router_w)
    rb = jnp.zeros((1, LANES), F32).at[0, :N_EXPERTS].set(router_b)
    row = lambda w: pl.BlockSpec((tm, w), lambda i: (i, 0))
    full = lambda *s: pl.BlockSpec(s, lambda i: (0,) * len(s))
    return pl.pallas_call(
        _router_kernel,
        grid=(t // tm,),
        in_specs=[row(d), full(1, d), full(d, LANES), full(1, LANES)],
        out_specs=[row(d), row(LANES), row(LANES)],
        out_shape=[jax.ShapeDtypeStruct((t, d), BF16), jax.ShapeDtypeStruct((t, LANES), jnp.int32),
                   jax.ShapeDtypeStruct((t, LANES), F32)],
        compiler_params=_cparams("parallel"),
        name="moe_router",
    )(h, norm_g.reshape(1, d), rw, rb)


def _expert_kernel(blk_e_ref, n_valid_ref, dst_ref, x_ref, w1g_ref, w1l_ref, b1g_ref, b1l_ref, w2_ref, b2_ref,
                   out_hbm, ybuf, sem):
    i = pl.program_id(0)
    last = pl.num_programs(0) - 1
    m = x_ref.shape[0]
    slot = i % 2

    def row_copy(s, r, dst):
        return pltpu.make_async_copy(ybuf.at[s, pl.ds(r, 1), :], out_hbm.at[pl.ds(dst, 1), :], sem.at[s])

    def drain(s, count):
        def body(r, c):
            row_copy(s, 0, 0).wait()
            return c
        lax.fori_loop(0, count, body, 0)

    @pl.when(i >= 2)
    def _():
        drain(slot, n_valid_ref[i - 2])

    nv = n_valid_ref[i]

    @pl.when(nv > 0)
    def _():
        x = x_ref[...]
        hg = jnp.minimum(_dot(x, w1g_ref[0]) + b1g_ref[0], SWIGLU_LIMIT)
        hl = jnp.clip(_dot(x, w1l_ref[0]) + b1l_ref[0], -SWIGLU_LIMIT, SWIGLU_LIMIT)
        act = hg * jax.nn.sigmoid(SWIGLU_ALPHA * hg) * (hl + 1.0)
        ybuf[slot] = _dot(act.astype(BF16), w2_ref[0].astype(BF16)) + b2_ref[0]

        def send(r, c):
            row_copy(slot, r, dst_ref[i * m + r]).start()
            return c
        lax.fori_loop(0, nv, send, 0)

    @pl.when(i == last)
    def _():
        drain(1 - slot, n_valid_ref[jnp.maximum(i - 1, 0)])
        drain(slot, nv)


def moe_experts(xs, blk_e, n_valid, row_dst, n_out, w1g, w1l, b1g, b1l, w2, b2):
    r, d = xs.shape
    f = w1g.shape[2]
    m = MOE_ROW_BLOCK
    assert r // m >= 2
    ex = lambda *s: pl.BlockSpec((1,) + s, lambda i, be, nv, dst: (be[i], 0, 0))
    grid_spec = pltpu.PrefetchScalarGridSpec(
        num_scalar_prefetch=3,
        grid=(r // m,),
        in_specs=[pl.BlockSpec((m, d), lambda i, be, nv, dst: (i, 0)),
                  ex(d, f), ex(d, f), ex(1, f), ex(1, f), ex(f, d), ex(1, d)],
        out_specs=pl.BlockSpec(memory_space=pl.ANY),
        scratch_shapes=[pltpu.VMEM((2, m, d), F32), pltpu.SemaphoreType.DMA((2,))],
    )
    return pl.pallas_call(
        _expert_kernel,
        grid_spec=grid_spec,
        out_shape=jax.ShapeDtypeStruct((n_out, d), F32),
        compiler_params=_cparams("arbitrary"),
        name="moe_experts",
    )(blk_e, n_valid, row_dst, xs, w1g, w1l, b1g, b1l, w2, b2)


def _combine_kernel(h_ref, y_ref, gate_ref, o_ref):
    d = h_ref.shape[1]
    acc = h_ref[...]
    for k in range(TOP_K):
        acc = acc + gate_ref[:, k:k + 1] * y_ref[:, k * d:(k + 1) * d]
    o_ref[...] = acc


def moe_combine(h, y4, gate, tm):
    t, d = h.shape
    row = lambda w: pl.BlockSpec((tm, w), lambda i: (i, 0))
    return pl.pallas_call(
        _combine_kernel,
        grid=(t // tm,),
        in_specs=[row(d), row(TOP_K * d), row(LANES)],
        out_specs=row(d),
        out_shape=jax.ShapeDtypeStruct((t, d), F32),
        compiler_params=_cparams("parallel"),
        name="moe_combine",
    )(h, y4, gate)


def _layer(x, mem, norm_mix_g, w_in, q_norm_g, k_cmp_norm_g, k_slc_norm_g, k_win_norm_g,
           cmp_pe_k, cmp_pe_v, cmp_k_w1, cmp_k_w2, cmp_v_w1, cmp_v_w2,
           rwkv_mu, rwkv_w0, rwkv_w_up, rwkv_a0, rwkv_a_up, rwkv_g_up, rwkv_k_k, rwkv_k_a,
           rwkv_r_k, rwkv_ln_w, rwkv_ln_b, w_out,
           norm_x_g, norm_mem_g, xq_w, xk_w, xv_w, xq_norm_g, xk_norm_g, xo_w,
           norm_ffn_g, router_w, router_b, mlp1_w, mlp1_b, mlp2_w, mlp2_b):
    b, s, d = x.shape
    t = b * s
    tm = 512
    xt = x.reshape(t, d)

    w_nsa = jnp.pad(w_in[:, :NSA_PROJ], ((0, 0), (0, NSA_PROJ_PAD - NSA_PROJ)))
    p_nsa = norm_matmul(xt, norm_mix_g, w_nsa, tm)
    p_rwkv = norm_matmul(xt, norm_mix_g, w_in[:, NSA_PROJ:], tm)

    nq = NSA_HEADS * HEAD_DIM
    gw = NSA_GROUPS * HEAD_DIM
    qn, ks, vs, kw, vw, gates = nsa_prep(p_nsa, q_norm_g, k_slc_norm_g, k_win_norm_g, tm)
    kcv = nsa_compress(p_nsa[:, nq:nq + gw], p_nsa[:, nq + gw:nq + 2 * gw], cmp_pe_k, cmp_pe_v,
                       cmp_k_w1, cmp_k_w2, cmp_v_w1, cmp_v_w2, k_cmp_norm_g, b, s)
    o_nsa = nsa_attention(qn, kcv, ks, vs, kw, vw, gates, b, s).transpose(0, 3, 1, 2).reshape(t, nq)

    at, bt, kt, rt, v, bw, kwd, wc, g_gate, bonus = rwkv_prep(
        p_rwkv, rwkv_mu, rwkv_w0, rwkv_w_up, rwkv_a0, rwkv_a_up, rwkv_g_up, rwkv_k_k, rwkv_k_a, rwkv_r_k, b, s, tm)
    ta, tr, arb, yv = rwkv_intra(at, bt, kt, rt, v, ts=256)
    y = rwkv_scan(ta, tr, arb, yv, rt, v, bw, kwd, wc, ts=256)

    h1 = out_proj(xt, o_nsa, y, bonus, g_gate, rwkv_ln_w, rwkv_ln_b, w_out, s, tm)
    m = mem.shape[1]
    kv = norm_matmul(mem.reshape(b * m, d), norm_mem_g, jnp.concatenate([xk_w, xv_w], axis=1), m)
    h2 = cross_attention(h1, kv.reshape(b, m, 2 * d), norm_x_g, xq_w, xq_norm_g, xk_norm_g, xo_w, b, s, tm)

    xn, top_i, gate = moe_router(h2, norm_ffn_g, router_w, router_b, tm)
    top_i = top_i[:, :TOP_K]
    a = t * TOP_K
    mb = MOE_ROW_BLOCK
    e_flat = top_i.reshape(a)
    order = jnp.argsort(e_flat, stable=True)
    e_sorted = e_flat[order]
    counts = jnp.bincount(e_flat, length=N_EXPERTS)
    starts = jnp.cumsum(counts) - counts
    padded = (counts + mb - 1) // mb * mb
    pends = jnp.cumsum(padded)
    pstarts = pends - padded
    dest = (pstarts[e_sorted] + jnp.arange(a) - starts[e_sorted]).astype(jnp.int32)
    n_blocks = -(-a // mb) + N_EXPERTS
    r = n_blocks * mb
    row_dst = jnp.full((r,), -1, jnp.int32).at[dest].set(order.astype(jnp.int32))
    row_tok = jnp.where(row_dst >= 0, row_dst // TOP_K, t)
    blk_start = jnp.arange(n_blocks) * mb
    blk_e = jnp.minimum(jnp.searchsorted(pends, blk_start, side='right'), N_EXPERTS - 1).astype(jnp.int32)
    n_valid = jnp.clip((pstarts + counts)[blk_e] - blk_start, 0, mb).astype(jnp.int32)
    xpad = jnp.concatenate([xn, jnp.zeros((1, d), xn.dtype)], axis=0)
    xs = xpad[row_tok]
    f = mlp1_w.shape[2] // 2
    w1 = mlp1_w.reshape(N_EXPERTS, d, f, 2)
    b1 = mlp1_b.reshape(N_EXPERTS, 1, f, 2)
    y4 = moe_experts(xs, blk_e, n_valid, row_dst, a,
                     w1[..., 0].astype(BF16), w1[..., 1].astype(BF16), b1[..., 0], b1[..., 1],
                     mlp2_w, mlp2_b.reshape(N_EXPERTS, 1, d))
    out = moe_combine(h2, y4.reshape(t, TOP_K * d), gate, 256)
    return out.reshape(b, s, d)


def kernel(x, mem, norm_mix_g, w_in, q_norm_g, k_cmp_norm_g, k_slc_norm_g, k_win_norm_g, cmp_pe_k, cmp_pe_v, cmp_k_w1, cmp_k_w2, cmp_v_w1, cmp_v_w2, rwkv_mu, rwkv_w0, rwkv_w_up, rwkv_a0, rwkv_a_up, rwkv_g_up, rwkv_k_k, rwkv_k_a, rwkv_r_k, rwkv_ln_w, rwkv_ln_b, w_out, norm_x_g, norm_mem_g, xq_w, xk_w, xv_w, xq_norm_g, xk_norm_g, xo_w, norm_ffn_g, router_w, router_b, mlp1_w, mlp1_b, mlp2_w, mlp2_b):
    params = (norm_mix_g, w_in, q_norm_g, k_cmp_norm_g, k_slc_norm_g, k_win_norm_g, cmp_pe_k, cmp_pe_v,
              cmp_k_w1, cmp_k_w2, cmp_v_w1, cmp_v_w2, rwkv_mu, rwkv_w0, rwkv_w_up, rwkv_a0, rwkv_a_up,
              rwkv_g_up, rwkv_k_k, rwkv_k_a, rwkv_r_k, rwkv_ln_w, rwkv_ln_b, w_out, norm_x_g, norm_mem_g,
              xq_w, xk_w, xv_w, xq_norm_g, xk_norm_g, xo_w, norm_ffn_g, router_w, router_b,
              mlp1_w, mlp1_b, mlp2_w, mlp2_b)
    h = x
    for layer in range(norm_mix_g.shape[0]):
        h = _layer(h, mem, *[prm[layer] for prm in params])
    return h
```

```python
import functools

import numpy as np
import jax
import jax.numpy as jnp
from jax import lax
from jax.experimental import pallas as pl
from jax.experimental.pallas import tpu as pltpu

F32 = jnp.float32
BF16 = jnp.bfloat16
HIGHEST = lax.Precision.HIGHEST

V7X_VMEM_BYTES = 64 * 1024 * 1024
VMEM_LIMIT = V7X_VMEM_BYTES * 3 // 4

HEAD_DIM = 64
NSA_HEADS = 8
NSA_GROUPS = 2
NSA_HPG = NSA_HEADS // NSA_GROUPS
GROUP_W = NSA_HPG * HEAD_DIM
CMP_BLOCK = 32
CMP_STRIDE = 16
SLC_BLOCK = 64
SLC_TOPK = 16
WINDOW = 512
Q_BLOCK = 128
SEL_CHUNK = 512
RWKV_HEADS = 8
RWKV_DIM = RWKV_HEADS * HEAD_DIM
RWKV_CHUNK = 64
GN_EPS = HEAD_DIM * 1e-5
X_HEADS = 4
N_EXPERTS = 32
TOP_K = 4
SWIGLU_LIMIT = 7.0
SWIGLU_ALPHA = 1.702
MOE_ROW_BLOCK = 256
RMS_EPS = 1e-6
NEG_INF = -1e30
BIG = 1e9
REMOVED = -3e38
LANES = 128

NSA_PROJ = NSA_HEADS * HEAD_DIM + 6 * NSA_GROUPS * HEAD_DIM + NSA_HEADS * 3
NSA_PROJ_PAD = -(-NSA_PROJ // LANES) * LANES
RWKV_PROJ = 3 * RWKV_DIM + 64 + 64 + 128


def _cparams(*sem):
    return pltpu.CompilerParams(dimension_semantics=sem, vmem_limit_bytes=VMEM_LIMIT)


def _dot(a, b, **kw):
    return jnp.dot(a, b, preferred_element_type=F32, **kw)


def _dot_nt(a, b, **kw):
    return lax.dot_general(a, b, (((1,), (1,)), ((), ())), preferred_element_type=F32, **kw)


def _dot_tn(a, b, **kw):
    return lax.dot_general(a, b, (((0,), (0,)), ((), ())), preferred_element_type=F32, **kw)


def _rms(x, g):
    return x * lax.rsqrt(jnp.mean(x * x, axis=-1, keepdims=True) + RMS_EPS) * g


def _block_diag_ones(n, blk, scale=1.0):
    i = np.arange(n)
    return jnp.asarray(((i[:, None] // blk) == (i[None, :] // blk)).astype(np.float32) * scale)


def _norm_matmul_kernel(x_ref, g_ref, w_ref, o_ref):
    xn = _rms(x_ref[...], g_ref[...]).astype(BF16)
    o_ref[...] = _dot(xn, w_ref[...])


def norm_matmul(x, g, w, tm):
    m, d = x.shape
    n = w.shape[1]
    return pl.pallas_call(
        _norm_matmul_kernel,
        grid=(m // tm,),
        in_specs=[pl.BlockSpec((tm, d), lambda i: (i, 0)),
                  pl.BlockSpec((1, d), lambda i: (0, 0)),
                  pl.BlockSpec((d, n), lambda i: (0, 0))],
        out_specs=pl.BlockSpec((tm, n), lambda i: (i, 0)),
        out_shape=jax.ShapeDtypeStruct((m, n), F32),
        compiler_params=_cparams("parallel"),
        name="norm_matmul",
    )(x, g.reshape(1, d), w.astype(BF16))


def _nsa_prep_kernel(p_ref, qg_ref, ksg_ref, kwg_ref, bdq_ref, bdk_ref, q_o, ks_o, vs_o, kw_o, vw_o, gate_o):
    p = p_ref[...]
    nq = NSA_HEADS * HEAD_DIM
    gw = NSA_GROUPS * HEAD_DIM
    q = p[:, :nq]
    msq = _dot(q * q, bdq_ref[...], precision=HIGHEST)
    q_o[...] = (q * lax.rsqrt(msq + RMS_EPS) * qg_ref[...] * (HEAD_DIM ** -0.5)).astype(BF16)

    def seg(k):
        return p[:, nq + k * gw: nq + (k + 1) * gw]

    def head_norm(t, g):
        ms = _dot(t * t, bdk_ref[...], precision=HIGHEST)
        return t * lax.rsqrt(ms + RMS_EPS) * g

    ks_o[...] = head_norm(seg(2), ksg_ref[...]).astype(BF16)
    vs_o[...] = seg(3).astype(BF16)
    kw_o[...] = head_norm(seg(4), kwg_ref[...]).astype(BF16)
    vw_o[...] = seg(5).astype(BF16)
    gate_o[...] = jax.nn.sigmoid(p[:, nq + 6 * gw: nq + 6 * gw + LANES])


def nsa_prep(p_nsa, q_g, ks_g, kw_g, tm):
    t = p_nsa.shape[0]
    nq = NSA_HEADS * HEAD_DIM
    gw = NSA_GROUPS * HEAD_DIM
    tile = lambda v, n: jnp.tile(v.reshape(1, HEAD_DIM), (1, n))
    row = lambda w: pl.BlockSpec((tm, w), lambda i: (i, 0))
    full = lambda *s: pl.BlockSpec(s, lambda i: (0,) * len(s))
    return pl.pallas_call(
        _nsa_prep_kernel,
        grid=(t // tm,),
        in_specs=[row(NSA_PROJ_PAD), full(1, nq), full(1, gw), full(1, gw), full(nq, nq), full(gw, gw)],
        out_specs=[row(nq)] + [row(gw)] * 4 + [row(LANES)],
        out_shape=[jax.ShapeDtypeStruct((t, nq), BF16)] + [jax.ShapeDtypeStruct((t, gw), BF16)] * 4
                  + [jax.ShapeDtypeStruct((t, LANES), F32)],
        compiler_params=_cparams("parallel"),
        name="nsa_prep",
    )(p_nsa, tile(q_g, NSA_HEADS), tile(ks_g, NSA_GROUPS), tile(kw_g, NSA_GROUPS),
      _block_diag_ones(nq, HEAD_DIM, 1.0 / HEAD_DIM), _block_diag_ones(gw, HEAD_DIM, 1.0 / HEAD_DIM))


def _compress_kernel(ch_ref, pe_ref, w1_ref, w2_ref, g_ref, o_ref):
    ch = ch_ref[0, 0, 0]
    nc = ch.shape[0]
    half = CMP_STRIDE * HEAD_DIM
    nxt = pltpu.roll(ch, nc - 1, 0)
    w1 = w1_ref[0]
    h1 = (_dot(ch, w1[:half], precision=HIGHEST) + _dot(nxt, w1[half:], precision=HIGHEST)
          + _dot(pe_ref[0], w1, precision=HIGHEST))
    out = _dot(jax.nn.silu(h1), w2_ref[0], precision=HIGHEST)
    out = jnp.where(pl.program_id(0) == 0, _rms(out, g_ref[...]), out)
    o_ref[0, 0, 0] = out.astype(BF16)


def nsa_compress(kc, vc, pe_k, pe_v, kw1, kw2, vw1, vw2, kc_g, b, s):
    nc = s // CMP_STRIDE
    half = CMP_STRIDE * HEAD_DIM

    def chunks(t):
        return t.reshape(b, nc, CMP_STRIDE, NSA_GROUPS, HEAD_DIM).transpose(0, 3, 1, 2, 4).reshape(
            b, NSA_GROUPS, nc, half)

    ch = jnp.stack([chunks(kc), chunks(vc)])
    pe = jnp.stack([pe_k.reshape(1, 2 * half), pe_v.reshape(1, 2 * half)])
    return pl.pallas_call(
        _compress_kernel,
        grid=(2, b, NSA_GROUPS),
        in_specs=[pl.BlockSpec((1, 1, 1, nc, half), lambda kv, bi, g: (kv, bi, g, 0, 0)),
                  pl.BlockSpec((1, 1, 2 * half), lambda kv, bi, g: (kv, 0, 0)),
                  pl.BlockSpec((1, 2 * half, HEAD_DIM), lambda kv, bi, g: (kv, 0, 0)),
                  pl.BlockSpec((1, HEAD_DIM, HEAD_DIM), lambda kv, bi, g: (kv, 0, 0)),
                  pl.BlockSpec((1, HEAD_DIM), lambda kv, bi, g: (0, 0))],
        out_specs=pl.BlockSpec((1, 1, 1, nc, HEAD_DIM), lambda kv, bi, g: (kv, bi, g, 0, 0)),
        out_shape=jax.ShapeDtypeStruct((2, b, NSA_GROUPS, nc, HEAD_DIM), BF16),
        compiler_params=_cparams("parallel", "parallel", "parallel"),
        name="nsa_compress",
    )(ch, pe, jnp.stack([kw1, vw1]), jnp.stack([kw2, vw2]), kc_g.reshape(1, HEAD_DIM))


def _masked_exp_cols(s, mask):
    sm = jnp.where(mask, s, NEG_INF)
    p = jnp.where(mask, jnp.exp(sm - jnp.max(sm, axis=0, keepdims=True)), 0.0)
    l = jnp.sum(p, axis=0, keepdims=True)
    return p, 1.0 / jnp.where(l > 0.0, l, 1.0)


def _nsa_attn_kernel(qt_ref, kc_ref, vct_ref, ovt_ref, ks_ref, vst_ref, kw_ref, vwt_ref, gate_ref, o_ref,
                     *, n_sel, n_top):
    g = pl.program_id(1)
    t0 = pl.program_id(2) * Q_BLOCK
    ks_ref, vst_ref, kw_ref, vwt_ref = (r.at[0, 0] for r in (ks_ref, vst_ref, kw_ref, vwt_ref))
    cols = NSA_HPG * Q_BLOCK
    col = lax.broadcasted_iota(jnp.int32, (1, cols), 1)
    tq = t0 + col % Q_BLOCK
    head = g * NSA_HPG + col // Q_BLOCK
    slope = lax.bitcast_convert_type((127 - (head + 1)) << 23, F32)
    qt = jnp.concatenate([qt_ref[0, h] for h in range(NSA_HPG)], axis=1)
    frow = lax.broadcasted_iota(jnp.int32, (HEAD_DIM, cols), 0)
    tq_hi = (tq // SLC_BLOCK * SLC_BLOCK).astype(F32)
    tq_lo = (tq % SLC_BLOCK).astype(F32)
    qpos = jnp.where(frow < 2, slope, jnp.where(frow == 2, -slope * tq_hi, jnp.where(frow == 3, -slope * tq_lo, 0.0)))
    qc = jnp.concatenate([qt, qpos.astype(BF16)], axis=0)

    def heads_sum(x):
        acc = x[:, :Q_BLOCK]
        for h in range(1, NSA_HPG):
            acc = acc + x[:, h * Q_BLOCK:(h + 1) * Q_BLOCK]
        return acc

    kc = kc_ref[0, 0]
    nc = kc.shape[0]
    c_last = lax.broadcasted_iota(jnp.int32, (nc, 1), 0) * CMP_STRIDE + (CMP_BLOCK - 1)
    p_c, inv_c = _masked_exp_cols(_dot(kc, qc), c_last <= tq)
    p_c = p_c * inv_c
    o_c = _dot(vct_ref[0, 0], p_c.astype(BF16))
    imp = _dot(ovt_ref[...], heads_sum(p_c), precision=HIGHEST)

    bid = lax.broadcasted_iota(jnp.int32, (LANES, Q_BLOCK), 0)
    bidf = bid.astype(F32)
    tq1 = t0 + lax.broadcasted_iota(jnp.int32, (1, Q_BLOCK), 1)
    cur = tq1 // SLC_BLOCK
    forced = (bid == 0) | (bid == cur) | (bid == cur - 1)
    score = jnp.where(forced, BIG, jnp.where(bid * SLC_BLOCK <= tq1, imp, -BIG))
    score = jnp.where(bid < n_sel, score, REMOVED)
    picked = jnp.zeros((LANES, Q_BLOCK), jnp.bool_)
    for _ in range(n_top):
        m = jnp.max(score, axis=0, keepdims=True)
        pick = bidf == jnp.min(jnp.where(score == m, bidf, float(LANES)), axis=0, keepdims=True)
        picked = picked | pick
        score = jnp.where(pick, REMOVED, score)
    sel_neg = jnp.where(picked, 0.0, NEG_INF).astype(BF16)
    qx = jnp.concatenate([qc, jnp.concatenate([sel_neg] * NSA_HPG, axis=1)], axis=0)

    def values_t(ref, first_tile, n_tiles):
        return jnp.concatenate([ref[first_tile + i] for i in range(n_tiles)], axis=1)

    span = WINDOW + Q_BLOCK
    ws = pl.multiple_of(jnp.maximum(t0 - WINDOW, 0), Q_BLOCK)
    dw = tq - (ws + lax.broadcasted_iota(jnp.int32, (span, 1), 0))
    p_w, inv_w = _masked_exp_cols(_dot(kw_ref[pl.ds(ws, span), :], qc), (dw >= 0) & (dw < WINDOW))
    o_w = _dot(values_t(vwt_ref, ws // Q_BLOCK, span // Q_BLOCK), p_w.astype(BF16)) * inv_w

    krow = lax.broadcasted_iota(jnp.int32, (SEL_CHUNK, 1), 0)

    def flash(s, start, carry):
        m, l, acc = carry
        m_new = jnp.maximum(m, jnp.max(s, axis=0, keepdims=True))
        alpha = jnp.exp(m - m_new)
        p = jnp.exp(s - m_new)
        l = alpha * l + jnp.sum(p, axis=0, keepdims=True)
        v = values_t(vst_ref, start // Q_BLOCK, SEL_CHUNK // Q_BLOCK)
        return m_new, l, alpha * acc + _dot(v, p.astype(BF16))

    def full_step(j, carry):
        start = pl.multiple_of(j * SEL_CHUNK, SEL_CHUNK)
        return flash(_dot(ks_ref[pl.ds(start, SEL_CHUNK), :], qx), start, carry)

    n_full = t0 // SEL_CHUNK
    init = (jnp.full((1, cols), NEG_INF, F32), jnp.zeros((1, cols), F32), jnp.zeros((HEAD_DIM, cols), F32))
    carry = lax.fori_loop(0, n_full, full_step, init)
    start = pl.multiple_of(n_full * SEL_CHUNK, SEL_CHUNK)
    s_diag = jnp.where(start + krow <= tq, _dot(ks_ref[pl.ds(start, SEL_CHUNK), :], qx), NEG_INF)
    _, l_s, acc_s = flash(s_diag, start, carry)
    o_s = acc_s / l_s

    for h in range(NSA_HPG):
        hs = slice(h * Q_BLOCK, (h + 1) * Q_BLOCK)
        gate = lambda br: gate_ref[0, 0, br, h:h + 1, :]
        o_ref[0, h] = gate(0) * o_c[:, hs] + gate(1) * o_s[:, hs] + gate(2) * o_w[:, hs]


def nsa_attention(qn, kcv, ks, vs, kw, vw, gates, b, s):
    nq_blocks = s // Q_BLOCK
    nc = s // CMP_STRIDE
    n_cmp = nc - CMP_BLOCK // CMP_STRIDE + 1
    n_sel = s // SLC_BLOCK
    n_top = min(SLC_TOPK, n_sel)
    grp, dh = NSA_GROUPS, HEAD_DIM
    assert n_sel <= LANES and s % SEL_CHUNK == 0 and s >= WINDOW + Q_BLOCK
    c_start = np.arange(n_cmp) * CMP_STRIDE
    s_start = np.arange(n_sel) * SLC_BLOCK
    ovt = np.zeros((LANES, nc), np.float32)
    ovt[:n_sel, :n_cmp] = (np.clip(np.minimum((c_start + CMP_BLOCK)[:, None], s_start[None] + SLC_BLOCK)
                                   - np.maximum(c_start[:, None], s_start[None]), 0, None) / CMP_BLOCK).T

    def pos_features(pos):
        f = np.zeros((len(pos), dh), np.float32)
        f[:, 0] = pos // SLC_BLOCK * SLC_BLOCK
        f[:, 1] = pos % SLC_BLOCK
        f[:, 2:4] = 1.0
        return jnp.asarray(f, BF16)

    tok = np.arange(s)
    onehot = jnp.asarray(tok[:, None] // SLC_BLOCK == np.arange(LANES)[None], BF16)
    per_group = lambda a: a.reshape(b, s, grp, dh).transpose(0, 2, 1, 3)
    with_feats = lambda k, *f: jnp.concatenate(
        [k] + [jnp.broadcast_to(x, k.shape[:2] + x.shape) for x in f], axis=-1)
    tiles_t = lambda v: v.reshape(b, s // Q_BLOCK, Q_BLOCK, grp, dh).transpose(0, 3, 1, 4, 2)
    qt = qn.reshape(b, s, NSA_HEADS, dh).transpose(0, 2, 3, 1)
    ks_x = with_feats(per_group(ks), pos_features(tok), onehot)
    kw_x = with_feats(per_group(kw), pos_features(tok))
    kc_x = with_feats(kcv[0], pos_features(np.arange(nc) * CMP_STRIDE + (CMP_BLOCK - 1)))
    vct = kcv[1].transpose(0, 1, 3, 2)
    gates_t = gates[:, :NSA_HEADS * 3].reshape(b, s, grp, NSA_HPG, 3).transpose(0, 2, 4, 3, 1)

    grp_spec = lambda *shape: pl.BlockSpec((1, 1) + shape, lambda bi, g, i: (bi, g) + (0,) * len(shape),
                                           pipeline_mode=pl.Buffered(1))
    return pl.pallas_call(
        functools.partial(_nsa_attn_kernel, n_sel=n_sel, n_top=n_top),
        grid=(b, grp, nq_blocks),
        in_specs=[pl.BlockSpec((1, NSA_HPG, dh, Q_BLOCK), lambda bi, g, i: (bi, g, 0, i)),
                  grp_spec(nc, 2 * dh), grp_spec(dh, nc),
                  pl.BlockSpec((LANES, nc), lambda bi, g, i: (0, 0)),
                  grp_spec(s, 2 * dh + LANES), grp_spec(s // Q_BLOCK, dh, Q_BLOCK),
                  grp_spec(s, 2 * dh), grp_spec(s // Q_BLOCK, dh, Q_BLOCK),
                  pl.BlockSpec((1, 1, 3, NSA_HPG, Q_BLOCK), lambda bi, g, i: (bi, g, 0, 0, i))],
        out_specs=pl.BlockSpec((1, NSA_HPG, dh, Q_BLOCK), lambda bi, g, i: (bi, g, 0, i)),
        out_shape=jax.ShapeDtypeStruct((b, NSA_HEADS, dh, s), F32),
        compiler_params=_cparams("parallel", "parallel", "arbitrary"),
        name="nsa_attention",
    )(qt, kc_x, vct, jnp.asarray(ovt), ks_x, tiles_t(vs), kw_x, tiles_t(vw), gates_t)


def _rwkv_prep_kernel(p_ref, prev_ref, mu_ref, w0_ref, a0_ref, kk_ref, ka_ref, rk_ref,
                      wup_ref, aup_ref, gup_ref, bd_ref, ltri_ref, lones_ref, csum_ref,
                      at_o, bt_o, kt_o, rt_o, v_o, bw_o, kw_o, wc_o, g_o, bonus_o, *, tiles_per_seq):
    p = p_ref[...]
    tm = p.shape[0]
    first = pl.program_id(0) % tiles_per_seq == 0
    last_prev = jnp.where(first, 0.0, prev_ref[7:8, :])
    prev = pltpu.roll(p, 1, 0)
    prev = jnp.where(lax.broadcasted_iota(jnp.int32, (tm, 1), 0) == 0, last_prev, prev)
    pm = p + (prev - p) * mu_ref[...]
    d = RWKV_DIM
    r, k, v = pm[:, :d], pm[:, d:2 * d], pm[:, 2 * d:3 * d]
    lora = pm[:, 3 * d:3 * d + LANES]
    gd = pm[:, 3 * d + LANES:3 * d + 2 * LANES]
    z = -(w0_ref[...] + _dot(jnp.tanh(lora).astype(BF16), wup_ref[...]))
    softplus = jnp.maximum(z, 0.0) + jnp.log(1.0 + jnp.exp(-jnp.abs(z)))
    w = -softplus - 0.5
    a = jax.nn.sigmoid(a0_ref[...] + _dot(lora.astype(BF16), aup_ref[...]))
    g_o[...] = _dot(jax.nn.sigmoid(gd).astype(BF16), gup_ref[...])
    bd = bd_ref[...]
    kkr = k * kk_ref[...]
    kk = kkr / jnp.maximum(jnp.sqrt(_dot(kkr * kkr, bd, precision=HIGHEST)), 1e-12)
    k2 = k * (1.0 + (a - 1.0) * ka_ref[...])
    bonus_o[...] = _dot(r * k2 * rk_ref[...], bd, precision=HIGHEST) * v
    lw = -jnp.exp(w)
    cum = _dot(ltri_ref[...], lw, precision=HIGHEST)
    tot = _dot(lones_ref[...], lw, precision=HIGHEST)
    e_in = jnp.exp(cum)
    e_out = jnp.exp(-cum)
    e_end = jnp.exp(tot - cum)

    def put_heads(o, val):
        for h in range(RWKV_HEADS):
            o[0, h] = val[:, h * HEAD_DIM:(h + 1) * HEAD_DIM].astype(o.dtype)

    put_heads(at_o, -kk * jnp.exp(cum - lw))
    put_heads(bt_o, kk * a * e_out)
    put_heads(kt_o, k2 * e_out)
    put_heads(rt_o, r * e_in)
    put_heads(v_o, v)
    put_heads(bw_o, kk * a * e_end)
    put_heads(kw_o, k2 * e_end)
    put_heads(wc_o, jnp.exp(_dot(csum_ref[...], lw, precision=HIGHEST)))


def rwkv_prep(p_rwkv, mu, w0, w_up, a0, a_up, g_up, k_k, k_a, r_k, b, s, tm):
    t = p_rwkv.shape[0]
    d = RWKV_DIM
    c = RWKV_CHUNK
    tps = s // tm
    cpt = tm // c
    wup = jnp.concatenate([w_up, jnp.zeros_like(a_up)], axis=0).astype(BF16)
    aup = jnp.concatenate([jnp.zeros_like(w_up), a_up], axis=0).astype(BF16)
    i = np.arange(tm)
    same = (i[:, None] // c) == (i[None, :] // c)
    ltri = jnp.asarray((same & (i[:, None] >= i[None, :])).astype(np.float32))
    lones = jnp.asarray(same.astype(np.float32))
    csum = jnp.asarray((np.arange(cpt)[:, None] == (i[None, :] // c)).astype(np.float32))
    row = lambda w: pl.BlockSpec((tm, w), lambda i: (i, 0))
    full = lambda *sh: pl.BlockSpec(sh, lambda i: (0,) * len(sh))
    heads = lambda n: pl.BlockSpec((1, RWKV_HEADS, n, HEAD_DIM), lambda i: (i // tps, 0, i % tps, 0))
    hshape = lambda n, dt: jax.ShapeDtypeStruct((b, RWKV_HEADS, n, HEAD_DIM), dt)
    vec = lambda x: x.reshape(1, -1)
    return pl.pallas_call(
        functools.partial(_rwkv_prep_kernel, tiles_per_seq=tps),
        grid=(t // tm,),
        in_specs=[row(RWKV_PROJ),
                  pl.BlockSpec((8, RWKV_PROJ), lambda i: (jnp.maximum(i * (tm // 8) - 1, 0), 0)),
                  full(1, RWKV_PROJ), full(1, d), full(1, d), full(1, d), full(1, d), full(1, d),
                  full(LANES, d), full(LANES, d), full(LANES, d), full(d, d), full(tm, tm), full(tm, tm),
                  full(cpt, tm)],
        out_specs=[heads(tm)] * 7 + [heads(cpt), row(d), row(d)],
        out_shape=[hshape(s, BF16)] * 7 + [hshape(s // c, F32)] + [jax.ShapeDtypeStruct((t, d), F32)] * 2,
        compiler_params=_cparams("parallel"),
        name="rwkv_prep",
    )(p_rwkv, p_rwkv, vec(mu), vec(w0), vec(a0), vec(k_k), vec(k_a), vec(r_k), wup, aup, g_up.astype(BF16),
      _block_diag_ones(d, HEAD_DIM), ltri, lones, csum)


def _bdot(a, b):
    return lax.dot_general(a, b, (((2,), (1,)), ((0,), (0,))), preferred_element_type=F32)


def _bdot_nt(a, b):
    return lax.dot_general(a, b, (((2,), (2,)), ((0,), (0,))), preferred_element_type=F32)


def _bdot_tn(a, b):
    return lax.dot_general(a, b, (((1,), (1,)), ((0,), (0,))), preferred_element_type=F32)


def _rwkv_intra_kernel(at_ref, bt_ref, kt_ref, rt_ref, v_ref, ta_o, tr_o, arb_o, yv_o):
    c = RWKV_CHUNK
    _, nh, ts, dh = at_ref.shape
    n = nh * (ts // c)
    chunked = lambda ref: ref[0].reshape(n, c, dh)
    at, bt, kt, rt, v = (chunked(r) for r in (at_ref, bt_ref, kt_ref, rt_ref, v_ref))
    ri = lax.broadcasted_iota(jnp.int32, (1, c, c), 1)
    ci = lax.broadcasted_iota(jnp.int32, (1, c, c), 2)
    strict = ri > ci
    incl = ri >= ci
    ar = jnp.concatenate([at, rt], axis=1)
    xb = _bdot_nt(ar, bt)
    xk = _bdot_nt(ar, kt)
    l_ab = jnp.where(strict, xb[:, :c], 0.0)
    a_ak = jnp.where(strict, xk[:, :c], 0.0)
    a_rb = jnp.where(incl, xb[:, c:], 0.0)
    a_rk = jnp.where(incl, xk[:, c:], 0.0)
    pw = l_ab
    tinv = jnp.where(ri == ci, 1.0, 0.0) + l_ab
    for _ in range(int(np.log2(c)) - 1):
        pw_b = pw.astype(BF16)
        pw = _bdot(pw_b, pw_b)
        tinv = tinv + _bdot(tinv.astype(BF16), pw.astype(BF16))
    tinv_b = tinv.astype(BF16)

    def put(o, val):
        o[0] = val.reshape(nh, ts, val.shape[-1]).astype(o.dtype)

    put(ta_o, _bdot(tinv_b, at))
    put(tr_o, _bdot(tinv_b, _bdot(a_ak.astype(BF16), v).astype(BF16)))
    put(arb_o, a_rb)
    put(yv_o, _bdot(a_rk.astype(BF16), v))


def rwkv_intra(at, bt, kt, rt, v, ts):
    b, h, s, dh = at.shape
    seq = lambda: pl.BlockSpec((1, h, ts, dh), lambda bi, i: (bi, 0, i, 0))
    shp = lambda dt: jax.ShapeDtypeStruct((b, h, s, dh), dt)
    return pl.pallas_call(
        _rwkv_intra_kernel,
        grid=(b, s // ts),
        in_specs=[seq()] * 5,
        out_specs=[seq()] * 4,
        out_shape=[shp(BF16), shp(F32), shp(BF16), shp(F32)],
        compiler_params=_cparams("parallel", "parallel"),
        name="rwkv_intra",
    )(at, bt, kt, rt, v)


def _rwkv_scan_kernel(ta_ref, tr_ref, arb_ref, yv_ref, rt_ref, v_ref, bw_ref, kw_ref, wc_ref, y_ref, st_ref):
    c = RWKV_CHUNK
    nb, nh, ts, dh = ta_ref.shape
    n = nb * nh

    @pl.when(pl.program_id(0) == 0)
    def _():
        st_ref[...] = jnp.zeros_like(st_ref)

    def chunk_step(j, _):
        sl = (slice(None), slice(None), pl.ds(pl.multiple_of(j * c, c), c), slice(None))
        get = lambda ref: ref[sl].reshape(n, c, dh)
        st = st_ref[...]
        st_b = st.astype(BF16)
        u = _bdot_nt(get(ta_ref), st_b) + get(tr_ref)
        u_b = u.astype(BF16)
        y = _bdot_nt(get(rt_ref), st_b) + _bdot(get(arb_ref), u_b) + get(yv_ref)
        wc = wc_ref[:, :, pl.ds(pl.program_id(0) * (ts // c) + j, 1), :].reshape(n, 1, dh)
        st_ref[...] = st * wc + _bdot_tn(jnp.concatenate([u_b, get(v_ref)], axis=1),
                                         jnp.concatenate([get(bw_ref), get(kw_ref)], axis=1))
        y_ref[sl] = y.reshape(nb, nh, c, dh)
        return 0

    lax.fori_loop(0, ts // c, chunk_step, 0)


def rwkv_scan(ta, tr, arb, yv, rt, v, bw, kw, wc, ts):
    b, h, s, dh = ta.shape
    seq = lambda n: pl.BlockSpec((b, h, n, dh), lambda i: (0, 0, i, 0))
    return pl.pallas_call(
        _rwkv_scan_kernel,
        grid=(s // ts,),
        in_specs=[seq(ts)] * 8 + [pl.BlockSpec(wc.shape, lambda i: (0, 0, 0, 0))],
        out_specs=seq(ts),
        out_shape=jax.ShapeDtypeStruct((b, h, s, dh), F32),
        scratch_shapes=[pltpu.VMEM((b * h, dh, dh), F32)],
        compiler_params=_cparams("arbitrary"),
        name="rwkv_scan",
    )(ta, tr, arb, yv, rt, v, bw, kw, wc)


def _out_proj_kernel(x_ref, on_ref, y_ref, bonus_ref, g_ref, lnw_ref, lnb_ref, bd_ref, wn_ref, wr_ref, o_ref):
    y = jnp.concatenate([y_ref[0, h] for h in range(RWKV_HEADS)], axis=-1)
    bd = bd_ref[...]
    yc = y - _dot(y, bd, precision=HIGHEST)
    yn = yc * lax.rsqrt(_dot(yc * yc, bd, precision=HIGHEST) + GN_EPS)
    o_rwkv = (yn * lnw_ref[...] + lnb_ref[...] + bonus_ref[...]) * g_ref[...]
    o_ref[...] = (x_ref[...] + _dot(on_ref[...].astype(BF16), wn_ref[...])
                  + _dot(o_rwkv.astype(BF16), wr_ref[...]))


def out_proj(x, o_nsa, y, bonus, g, ln_w, ln_b, w_out, s, tm):
    t, d = x.shape
    dn = o_nsa.shape[1]
    dr = bonus.shape[1]
    tps = s // tm
    row = lambda w: pl.BlockSpec((tm, w), lambda i: (i, 0))
    full = lambda *sh: pl.BlockSpec(sh, lambda i: (0,) * len(sh))
    return pl.pallas_call(
        _out_proj_kernel,
        grid=(t // tm,),
        in_specs=[row(d), row(dn),
                  pl.BlockSpec((1, RWKV_HEADS, tm, HEAD_DIM), lambda i: (i // tps, 0, i % tps, 0)),
                  row(dr), row(dr), full(1, dr), full(1, dr), full(dr, dr), full(dn, d), full(dr, d)],
        out_specs=row(d),
        out_shape=jax.ShapeDtypeStruct((t, d), F32),
        compiler_params=_cparams("parallel"),
        name="out_proj",
    )(x, o_nsa, y, bonus, g, ln_w.reshape(1, dr), ln_b.reshape(1, dr),
      _block_diag_ones(dr, HEAD_DIM, 1.0 / HEAD_DIM), w_out[:dn].astype(BF16), w_out[dn:].astype(BF16))


def _cross_attn_kernel(h_ref, g_ref, wq_ref, qg_ref, kv_ref, kg_ref, wo_ref, o_ref):
    h = h_ref[...]
    d = h.shape[1]
    xd = d // X_HEADS
    q = _dot(_rms(h, g_ref[...]).astype(BF16), wq_ref[...])
    kv = kv_ref[0]
    outs = []
    for hd in range(X_HEADS):
        qh = _rms(q[:, hd * xd:(hd + 1) * xd], qg_ref[...]) * (xd ** -0.5)
        kh = _rms(kv[:, hd * xd:(hd + 1) * xd], kg_ref[...])
        vh = kv[:, d + hd * xd:d + (hd + 1) * xd]
        s = _dot_nt(qh.astype(BF16), kh.astype(BF16))
        p = jnp.exp(s - jnp.max(s, axis=-1, keepdims=True))
        p = p / jnp.sum(p, axis=-1, keepdims=True)
        outs.append(_dot(p.astype(BF16), vh.astype(BF16)))
    o = jnp.concatenate(outs, axis=-1)
    o_ref[...] = h + _dot(o.astype(BF16), wo_ref[...])


def cross_attention(h, kv, norm_g, xq_w, xq_g, xk_g, xo_w, b, s, tm):
    t, d = h.shape
    m = kv.shape[1]
    xd = d // X_HEADS
    tiles = s // tm
    full = lambda *sh: pl.BlockSpec(sh, lambda i: (0,) * len(sh))
    return pl.pallas_call(
        _cross_attn_kernel,
        grid=(t // tm,),
        in_specs=[pl.BlockSpec((tm, d), lambda i: (i, 0)), full(1, d), full(d, d), full(1, xd),
                  pl.BlockSpec((1, m, 2 * d), lambda i: (i // tiles, 0, 0)), full(1, xd), full(d, d)],
        out_specs=pl.BlockSpec((tm, d), lambda i: (i, 0)),
        out_shape=jax.ShapeDtypeStruct((t, d), F32),
        compiler_params=_cparams("parallel"),
        name="cross_attention",
    )(h, norm_g.reshape(1, d), xq_w.astype(BF16), xq_g.reshape(1, xd), kv, xk_g.reshape(1, xd),
      xo_w.astype(BF16))


def _router_kernel(h_ref, g_ref, rw_ref, rb_ref, xn_o, idx_o, gate_o):
    xn = _rms(h_ref[...], g_ref[...])
    xn_o[...] = xn
    logits = _dot(xn, rw_ref[...], precision=HIGHEST) + rb_ref[...]
    tm = logits.shape[0]
    lane = lax.broadcasted_iota(jnp.int32, (tm, LANES), 1)
    lanef = lane.astype(F32)
    logits = jnp.where(lane < N_EXPERTS, logits, REMOVED)
    idx_acc = jnp.zeros((tm, LANES), F32)
    val_acc = jnp.zeros((tm, LANES), F32)
    top = None
    for k in range(TOP_K):
        m = jnp.max(logits, axis=-1, keepdims=True)
        idx = jnp.min(jnp.where(logits == m, lanef, float(LANES)), axis=-1, keepdims=True)
        logits = jnp.where(lanef == idx, REMOVED, logits)
        top = m if top is None else top
        idx_acc = jnp.where(lane == k, idx, idx_acc)
        val_acc = jnp.where(lane == k, jnp.exp(m - top), val_acc)
    idx_o[...] = idx_acc.astype(jnp.int32)
    gate_o[...] = val_acc / jnp.sum(val_acc, axis=-1, keepdims=True)


def moe_router(h, norm_g, router_w, router_b, tm):
    t, d = h.shape
    rw = jnp.zeros((d, LANES), F32).at[:, :N_EXPERTS].set(router_w)
    rb = jnp.zeros((1, LANES), F32).at[0, :N_EXPERTS].set(router_b)
    row = lambda w: pl.BlockSpec((tm, w), lambda i: (i, 0))
    full = lambda *s: pl.BlockSpec(s, lambda i: (0,) * len(s))
    return pl.pallas_call(
        _router_kernel,
        grid=(t // tm,),
        in_specs=[row(d), full(1, d), full(d, LANES), full(1, LANES)],
        out_specs=[row(d), row(LANES), row(LANES)],
        out_shape=[jax.ShapeDtypeStruct((t, d), F32), jax.ShapeDtypeStruct((t, LANES), jnp.int32),
                   jax.ShapeDtypeStruct((t, LANES), F32)],
        compiler_params=_cparams("parallel"),
        name="moe_router",
    )(h, norm_g.reshape(1, d), rw, rb)


def _expert_kernel(blk_e_ref, n_used_ref, x_ref, w1g_ref, w1l_ref, b1g_ref, b1l_ref, w2_ref, b2_ref, o_ref):
    i = pl.program_id(0)

    @pl.when(i < n_used_ref[0])
    def _():
        x = x_ref[...].astype(BF16)
        hg = jnp.minimum(_dot(x, w1g_ref[0]) + b1g_ref[0], SWIGLU_LIMIT)
        hl = jnp.clip(_dot(x, w1l_ref[0]) + b1l_ref[0], -SWIGLU_LIMIT, SWIGLU_LIMIT)
        act = hg * jax.nn.sigmoid(SWIGLU_ALPHA * hg) * (hl + 1.0)
        o_ref[...] = _dot(act.astype(BF16), w2_ref[0].astype(BF16)) + b2_ref[0]

    @pl.when(i >= n_used_ref[0])
    def _():
        o_ref[...] = jnp.zeros_like(o_ref)


def moe_experts(xs, blk_e, n_used, w1g, w1l, b1g, b1l, w2, b2):
    r, d = xs.shape
    f = w1g.shape[2]
    m = MOE_ROW_BLOCK
    ex = lambda *s: pl.BlockSpec((1,) + s, lambda i, be, nu: (be[i], 0, 0))
    grid_spec = pltpu.PrefetchScalarGridSpec(
        num_scalar_prefetch=2,
        grid=(r // m,),
        in_specs=[pl.BlockSpec((m, d), lambda i, be, nu: (i, 0)),
                  ex(d, f), ex(d, f), ex(1, f), ex(1, f), ex(f, d), ex(1, d)],
        out_specs=pl.BlockSpec((m, d), lambda i, be, nu: (i, 0)),
    )
    return pl.pallas_call(
        _expert_kernel,
        grid_spec=grid_spec,
        out_shape=jax.ShapeDtypeStruct((r, d), F32),
        compiler_params=_cparams("arbitrary"),
        name="moe_experts",
    )(blk_e, n_used, xs, w1g, w1l, b1g, b1l, w2, b2)


def _combine_kernel(h_ref, y_ref, gate_ref, o_ref):
    d = h_ref.shape[1]
    acc = h_ref[...]
    for k in range(TOP_K):
        acc = acc + gate_ref[:, k:k + 1] * y_ref[:, k * d:(k + 1) * d]
    o_ref[...] = acc


def moe_combine(h, y4, gate, tm):
    t, d = h.shape
    row = lambda w: pl.BlockSpec((tm, w), lambda i: (i, 0))
    return pl.pallas_call(
        _combine_kernel,
        grid=(t // tm,),
        in_specs=[row(d), row(TOP_K * d), row(LANES)],
        out_specs=row(d),
        out_shape=jax.ShapeDtypeStruct((t, d), F32),
        compiler_params=_cparams("parallel"),
        name="moe_combine",
    )(h, y4, gate)


def _layer(x, mem, norm_mix_g, w_in, q_norm_g, k_cmp_norm_g, k_slc_norm_g, k_win_norm_g,
           cmp_pe_k, cmp_pe_v, cmp_k_w1, cmp_k_w2, cmp_v_w1, cmp_v_w2,
           rwkv_mu, rwkv_w0, rwkv_w_up, rwkv_a0, rwkv_a_up, rwkv_g_up, rwkv_k_k, rwkv_k_a,
           rwkv_r_k, rwkv_ln_w, rwkv_ln_b, w_out,
           norm_x_g, norm_mem_g, xq_w, xk_w, xv_w, xq_norm_g, xk_norm_g, xo_w,
           norm_ffn_g, router_w, router_b, mlp1_w, mlp1_b, mlp2_w, mlp2_b):
    b, s, d = x.shape
    t = b * s
    tm = 512
    xt = x.reshape(t, d)

    w_nsa = jnp.pad(w_in[:, :NSA_PROJ], ((0, 0), (0, NSA_PROJ_PAD - NSA_PROJ)))
    p_nsa = norm_matmul(xt, norm_mix_g, w_nsa, tm)
    p_rwkv = norm_matmul(xt, norm_mix_g, w_in[:, NSA_PROJ:], tm)

    nq = NSA_HEADS * HEAD_DIM
    gw = NSA_GROUPS * HEAD_DIM
    qn, ks, vs, kw, vw, gates = nsa_prep(p_nsa, q_norm_g, k_slc_norm_g, k_win_norm_g, tm)
    kcv = nsa_compress(p_nsa[:, nq:nq + gw], p_nsa[:, nq + gw:nq + 2 * gw], cmp_pe_k, cmp_pe_v,
                       cmp_k_w1, cmp_k_w2, cmp_v_w1, cmp_v_w2, k_cmp_norm_g, b, s)
    o_nsa = nsa_attention(qn, kcv, ks, vs, kw, vw, gates, b, s).transpose(0, 3, 1, 2).reshape(t, nq)

    at, bt, kt, rt, v, bw, kwd, wc, g_gate, bonus = rwkv_prep(
        p_rwkv, rwkv_mu, rwkv_w0, rwkv_w_up, rwkv_a0, rwkv_a_up, rwkv_g_up, rwkv_k_k, rwkv_k_a, rwkv_r_k, b, s, tm)
    ta, tr, arb, yv = rwkv_intra(at, bt, kt, rt, v, ts=256)
    y = rwkv_scan(ta, tr, arb, yv, rt, v, bw, kwd, wc, ts=256)

    h1 = out_proj(xt, o_nsa, y, bonus, g_gate, rwkv_ln_w, rwkv_ln_b, w_out, s, tm)
    m = mem.shape[1]
    kv = norm_matmul(mem.reshape(b * m, d), norm_mem_g, jnp.concatenate([xk_w, xv_w], axis=1), m)
    h2 = cross_attention(h1, kv.reshape(b, m, 2 * d), norm_x_g, xq_w, xq_norm_g, xk_norm_g, xo_w, b, s, tm)

    xn, top_i, gate = moe_router(h2, norm_ffn_g, router_w, router_b, tm)
    top_i = top_i[:, :TOP_K]
    a = t * TOP_K
    mb = MOE_ROW_BLOCK
    e_flat = top_i.reshape(a)
    order = jnp.argsort(e_flat, stable=True)
    e_sorted = e_flat[order]
    counts = jnp.bincount(e_flat, length=N_EXPERTS)
    starts = jnp.cumsum(counts) - counts
    padded = (counts + mb - 1) // mb * mb
    pends = jnp.cumsum(padded)
    pstarts = pends - padded
    dest = (pstarts[e_sorted] + jnp.arange(a) - starts[e_sorted]).astype(jnp.int32)
    n_blocks = -(-a // mb) + N_EXPERTS
    r = n_blocks * mb
    row_src = jnp.zeros((r,), jnp.int32).at[dest].set((order // TOP_K).astype(jnp.int32))
    blk_e = jnp.minimum(jnp.searchsorted(pends, jnp.arange(n_blocks) * mb, side='right'),
                        N_EXPERTS - 1).astype(jnp.int32)
    n_used = (pends[-1] // mb).astype(jnp.int32).reshape(1)
    xs = jnp.take(xn, row_src, axis=0)
    f = mlp1_w.shape[2] // 2
    w1 = mlp1_w.reshape(N_EXPERTS, d, f, 2)
    b1 = mlp1_b.reshape(N_EXPERTS, 1, f, 2)
    ys = moe_experts(xs, blk_e, n_used,
                     w1[..., 0].astype(BF16), w1[..., 1].astype(BF16), b1[..., 0], b1[..., 1],
                     mlp2_w, mlp2_b.reshape(N_EXPERTS, 1, d))
    pos = jnp.zeros((a,), jnp.int32).at[order].set(dest)
    y4 = jnp.take(ys, pos, axis=0).reshape(t, TOP_K * d)
    out = moe_combine(h2, y4, gate, 256)
    return out.reshape(b, s, d)


def kernel(x, mem, norm_mix_g, w_in, q_norm_g, k_cmp_norm_g, k_slc_norm_g, k_win_norm_g, cmp_pe_k, cmp_pe_v, cmp_k_w1, cmp_k_w2, cmp_v_w1, cmp_v_w2, rwkv_mu, rwkv_w0, rwkv_w_up, rwkv_a0, rwkv_a_up, rwkv_g_up, rwkv_k_k, rwkv_k_a, rwkv_r_k, rwkv_ln_w, rwkv_ln_b, w_out, norm_x_g, norm_mem_g, xq_w, xk_w, xv_w, xq_norm_g, xk_norm_g, xo_w, norm_ffn_g, router_w, router_b, mlp1_w, mlp1_b, mlp2_w, mlp2_b):
    params = (norm_mix_g, w_in, q_norm_g, k_cmp_norm_g, k_slc_norm_g, k_win_norm_g, cmp_pe_k, cmp_pe_v,
              cmp_k_w1, cmp_k_w2, cmp_v_w1, cmp_v_w2, rwkv_mu, rwkv_w0, rwkv_w_up, rwkv_a0, rwkv_a_up,
              rwkv_g_up, rwkv_k_k, rwkv_k_a, rwkv_r_k, rwkv_ln_w, rwkv_ln_b, w_out, norm_x_g, norm_mem_g,
              xq_w, xk_w, xv_w, xq_norm_g, xk_norm_g, xo_w, norm_ffn_g, router_w, router_b,
              mlp1_w, mlp1_b, mlp2_w, mlp2_b)
    h = x
    for layer in range(norm_mix_g.shape[0]):
        h = _layer(h, mem, *[prm[layer] for prm in params])
    return h
```

```python
import functools

import numpy as np
import jax
import jax.numpy as jnp
from jax import lax
from jax.experimental import pallas as pl
from jax.experimental.pallas import tpu as pltpu

F32 = jnp.float32
BF16 = jnp.bfloat16
HIGHEST = lax.Precision.HIGHEST

V7X_VMEM_BYTES = 64 * 1024 * 1024
VMEM_LIMIT = V7X_VMEM_BYTES * 3 // 4

HEAD_DIM = 64
NSA_HEADS = 8
NSA_GROUPS = 2
NSA_HPG = NSA_HEADS // NSA_GROUPS
GROUP_W = NSA_HPG * HEAD_DIM
CMP_BLOCK = 32
CMP_STRIDE = 16
SLC_BLOCK = 64
SLC_TOPK = 16
WINDOW = 512
Q_BLOCK = 128
SEL_CHUNK = 512
RWKV_HEADS = 8
RWKV_DIM = RWKV_HEADS * HEAD_DIM
RWKV_CHUNK = 64
GN_EPS = HEAD_DIM * 1e-5
X_HEADS = 4
N_EXPERTS = 32
TOP_K = 4
SWIGLU_LIMIT = 7.0
SWIGLU_ALPHA = 1.702
MOE_ROW_BLOCK = 256
RMS_EPS = 1e-6
NEG_INF = -1e30
BIG = 1e9
REMOVED = -3e38
LANES = 128

NSA_PROJ = NSA_HEADS * HEAD_DIM + 6 * NSA_GROUPS * HEAD_DIM + NSA_HEADS * 3
NSA_PROJ_PAD = -(-NSA_PROJ // LANES) * LANES
RWKV_PROJ = 3 * RWKV_DIM + 64 + 64 + 128


def _cparams(*sem):
    return pltpu.CompilerParams(dimension_semantics=sem, vmem_limit_bytes=VMEM_LIMIT)


def _dot(a, b, **kw):
    return jnp.dot(a, b, preferred_element_type=F32, **kw)


def _dot_nt(a, b, **kw):
    return lax.dot_general(a, b, (((1,), (1,)), ((), ())), preferred_element_type=F32, **kw)


def _dot_tn(a, b, **kw):
    return lax.dot_general(a, b, (((0,), (0,)), ((), ())), preferred_element_type=F32, **kw)


def _rms(x, g):
    return x * lax.rsqrt(jnp.mean(x * x, axis=-1, keepdims=True) + RMS_EPS) * g


def _split_bf16(x, terms=3):
    parts = []
    for _ in range(terms):
        hi = x.astype(BF16)
        parts.append(hi)
        x = x - hi.astype(F32)
    return parts


def _dot_x_sel(x, sel):
    return sum(_dot(p, sel) for p in _split_bf16(x))


def _dot_sel_x(sel, x):
    return sum(_dot(sel, p) for p in _split_bf16(x))


def _block_diag_ones(n, blk, scale=1.0):
    i = np.arange(n)
    return jnp.asarray(((i[:, None] // blk) == (i[None, :] // blk)).astype(np.float32) * scale, BF16)


def _norm_matmul_kernel(x_ref, g_ref, w_ref, o_ref):
    xn = _rms(x_ref[...], g_ref[...]).astype(BF16)
    o_ref[...] = _dot(xn, w_ref[...])


def norm_matmul(x, g, w, tm):
    m, d = x.shape
    n = w.shape[1]
    return pl.pallas_call(
        _norm_matmul_kernel,
        grid=(m // tm,),
        in_specs=[pl.BlockSpec((tm, d), lambda i: (i, 0)),
                  pl.BlockSpec((1, d), lambda i: (0, 0)),
                  pl.BlockSpec((d, n), lambda i: (0, 0))],
        out_specs=pl.BlockSpec((tm, n), lambda i: (i, 0)),
        out_shape=jax.ShapeDtypeStruct((m, n), F32),
        compiler_params=_cparams("parallel"),
        name="norm_matmul",
    )(x, g.reshape(1, d), w.astype(BF16))


def _nsa_prep_kernel(p_ref, qg_ref, ksg_ref, kwg_ref, bdq_ref, bdk_ref, q_o, ks_o, vs_o, kw_o, vw_o, gate_o):
    p = p_ref[...]
    nq = NSA_HEADS * HEAD_DIM
    gw = NSA_GROUPS * HEAD_DIM
    q = p[:, :nq]
    msq = _dot_x_sel(q * q, bdq_ref[...])
    q_o[...] = (q * lax.rsqrt(msq + RMS_EPS) * qg_ref[...] * (HEAD_DIM ** -0.5)).astype(BF16)

    def seg(k):
        return p[:, nq + k * gw: nq + (k + 1) * gw]

    def head_norm(t, g):
        ms = _dot_x_sel(t * t, bdk_ref[...])
        return t * lax.rsqrt(ms + RMS_EPS) * g

    ks_o[...] = head_norm(seg(2), ksg_ref[...]).astype(BF16)
    vs_o[...] = seg(3).astype(BF16)
    kw_o[...] = head_norm(seg(4), kwg_ref[...]).astype(BF16)
    vw_o[...] = seg(5).astype(BF16)
    gate_o[...] = jax.nn.sigmoid(p[:, nq + 6 * gw: nq + 6 * gw + LANES])


def nsa_prep(p_nsa, q_g, ks_g, kw_g, tm):
    t = p_nsa.shape[0]
    nq = NSA_HEADS * HEAD_DIM
    gw = NSA_GROUPS * HEAD_DIM
    tile = lambda v, n: jnp.tile(v.reshape(1, HEAD_DIM), (1, n))
    row = lambda w: pl.BlockSpec((tm, w), lambda i: (i, 0))
    full = lambda *s: pl.BlockSpec(s, lambda i: (0,) * len(s))
    return pl.pallas_call(
        _nsa_prep_kernel,
        grid=(t // tm,),
        in_specs=[row(NSA_PROJ_PAD), full(1, nq), full(1, gw), full(1, gw), full(nq, nq), full(gw, gw)],
        out_specs=[row(nq)] + [row(gw)] * 4 + [row(LANES)],
        out_shape=[jax.ShapeDtypeStruct((t, nq), BF16)] + [jax.ShapeDtypeStruct((t, gw), BF16)] * 4
                  + [jax.ShapeDtypeStruct((t, LANES), F32)],
        compiler_params=_cparams("parallel"),
        name="nsa_prep",
    )(p_nsa, tile(q_g, NSA_HEADS), tile(ks_g, NSA_GROUPS), tile(kw_g, NSA_GROUPS),
      _block_diag_ones(nq, HEAD_DIM, 1.0 / HEAD_DIM), _block_diag_ones(gw, HEAD_DIM, 1.0 / HEAD_DIM))


def _compress_kernel(ch_ref, pe_ref, w1_ref, w2_ref, g_ref, o_ref):
    ch = ch_ref[0, 0, 0]
    nc = ch.shape[0]
    half = CMP_STRIDE * HEAD_DIM
    nxt = pltpu.roll(ch, nc - 1, 0)
    w1 = w1_ref[0]
    h1 = (_dot(ch, w1[:half], precision=HIGHEST) + _dot(nxt, w1[half:], precision=HIGHEST)
          + _dot(pe_ref[0], w1, precision=HIGHEST))
    out = _dot(jax.nn.silu(h1), w2_ref[0], precision=HIGHEST)
    out = jnp.where(pl.program_id(0) == 0, _rms(out, g_ref[...]), out)
    o_ref[0, 0, 0] = out.astype(BF16)


def nsa_compress(kc, vc, pe_k, pe_v, kw1, kw2, vw1, vw2, kc_g, b, s):
    nc = s // CMP_STRIDE
    half = CMP_STRIDE * HEAD_DIM

    def chunks(t):
        return t.reshape(b, nc, CMP_STRIDE, NSA_GROUPS, HEAD_DIM).transpose(0, 3, 1, 2, 4).reshape(
            b, NSA_GROUPS, nc, half)

    ch = jnp.stack([chunks(kc), chunks(vc)])
    pe = jnp.stack([pe_k.reshape(1, 2 * half), pe_v.reshape(1, 2 * half)])
    return pl.pallas_call(
        _compress_kernel,
        grid=(2, b, NSA_GROUPS),
        in_specs=[pl.BlockSpec((1, 1, 1, nc, half), lambda kv, bi, g: (kv, bi, g, 0, 0)),
                  pl.BlockSpec((1, 1, 2 * half), lambda kv, bi, g: (kv, 0, 0)),
                  pl.BlockSpec((1, 2 * half, HEAD_DIM), lambda kv, bi, g: (kv, 0, 0)),
                  pl.BlockSpec((1, HEAD_DIM, HEAD_DIM), lambda kv, bi, g: (kv, 0, 0)),
                  pl.BlockSpec((1, HEAD_DIM), lambda kv, bi, g: (0, 0))],
        out_specs=pl.BlockSpec((1, 1, 1, nc, HEAD_DIM), lambda kv, bi, g: (kv, bi, g, 0, 0)),
        out_shape=jax.ShapeDtypeStruct((2, b, NSA_GROUPS, nc, HEAD_DIM), BF16),
        compiler_params=_cparams("parallel", "parallel", "parallel"),
        name="nsa_compress",
    )(ch, pe, jnp.stack([kw1, vw1]), jnp.stack([kw2, vw2]), kc_g.reshape(1, HEAD_DIM))


def _masked_exp_cols(s, mask):
    sm = jnp.where(mask, s, NEG_INF)
    p = jnp.where(mask, jnp.exp(sm - jnp.max(sm, axis=0, keepdims=True)), 0.0)
    l = jnp.sum(p, axis=0, keepdims=True)
    return p, 1.0 / jnp.where(l > 0.0, l, 1.0)


def _nsa_attn_kernel(qt_ref, kc_ref, vct_ref, ovt_ref, ks_ref, vst_ref, kw_ref, vwt_ref, gate_ref, o_ref,
                     *, n_sel, n_top):
    g = pl.program_id(1)
    t0 = pl.program_id(2) * Q_BLOCK
    ks_ref, vst_ref, kw_ref, vwt_ref = (r.at[0, 0] for r in (ks_ref, vst_ref, kw_ref, vwt_ref))
    cols = NSA_HPG * Q_BLOCK
    col = lax.broadcasted_iota(jnp.int32, (1, cols), 1)
    tq = t0 + col % Q_BLOCK
    head = g * NSA_HPG + col // Q_BLOCK
    slope = lax.bitcast_convert_type((127 - (head + 1)) << 23, F32)
    qt = jnp.concatenate([qt_ref[0, h] for h in range(NSA_HPG)], axis=1)
    frow = lax.broadcasted_iota(jnp.int32, (HEAD_DIM, cols), 0)
    tq_hi = (tq // SLC_BLOCK * SLC_BLOCK).astype(F32)
    tq_lo = (tq % SLC_BLOCK).astype(F32)
    qpos = jnp.where(frow < 2, slope, jnp.where(frow == 2, -slope * tq_hi, jnp.where(frow == 3, -slope * tq_lo, 0.0)))
    qc = jnp.concatenate([qt, qpos.astype(BF16)], axis=0)

    def heads_sum(x):
        acc = x[:, :Q_BLOCK]
        for h in range(1, NSA_HPG):
            acc = acc + x[:, h * Q_BLOCK:(h + 1) * Q_BLOCK]
        return acc

    kc = kc_ref[0, 0]
    nc = kc.shape[0]
    c_last = lax.broadcasted_iota(jnp.int32, (nc, 1), 0) * CMP_STRIDE + (CMP_BLOCK - 1)
    p_c, inv_c = _masked_exp_cols(_dot(kc, qc), c_last <= tq)
    p_c = p_c * inv_c
    o_c = _dot(vct_ref[0, 0], p_c.astype(BF16))
    imp = _dot_sel_x(ovt_ref[...], heads_sum(p_c))

    bid = lax.broadcasted_iota(jnp.int32, (LANES, Q_BLOCK), 0)
    bidf = bid.astype(F32)
    tq1 = t0 + lax.broadcasted_iota(jnp.int32, (1, Q_BLOCK), 1)
    cur = tq1 // SLC_BLOCK
    forced = (bid == 0) | (bid == cur) | (bid == cur - 1)
    score = jnp.where(forced, BIG, jnp.where(bid * SLC_BLOCK <= tq1, imp, -BIG))
    score = jnp.where(bid < n_sel, score, REMOVED)
    picked = jnp.zeros((LANES, Q_BLOCK), jnp.bool_)
    for _ in range(n_top):
        m = jnp.max(score, axis=0, keepdims=True)
        pick = bidf == jnp.min(jnp.where(score == m, bidf, float(LANES)), axis=0, keepdims=True)
        picked = picked | pick
        score = jnp.where(pick, REMOVED, score)
    sel_neg = jnp.where(picked, 0.0, NEG_INF).astype(BF16)
    qx = jnp.concatenate([qc, jnp.concatenate([sel_neg] * NSA_HPG, axis=1)], axis=0)

    def values_t(ref, first_tile, n_tiles):
        return jnp.concatenate([ref[first_tile + i] for i in range(n_tiles)], axis=1)

    span = WINDOW + Q_BLOCK
    ws = pl.multiple_of(jnp.maximum(t0 - WINDOW, 0), Q_BLOCK)
    dw = tq - (ws + lax.broadcasted_iota(jnp.int32, (span, 1), 0))
    p_w, inv_w = _masked_exp_cols(_dot(kw_ref[pl.ds(ws, span), :], qc), (dw >= 0) & (dw < WINDOW))
    o_w = _dot(values_t(vwt_ref, ws // Q_BLOCK, span // Q_BLOCK), p_w.astype(BF16)) * inv_w

    krow = lax.broadcasted_iota(jnp.int32, (SEL_CHUNK, 1), 0)

    def flash(s, start, carry):
        m, l, acc = carry
        m_new = jnp.maximum(m, jnp.max(s, axis=0, keepdims=True))
        alpha = jnp.exp(m - m_new)
        p = jnp.exp(s - m_new)
        l = alpha * l + jnp.sum(p, axis=0, keepdims=True)
        v = values_t(vst_ref, start // Q_BLOCK, SEL_CHUNK // Q_BLOCK)
        return m_new, l, alpha * acc + _dot(v, p.astype(BF16))

    def full_step(j, carry):
        start = pl.multiple_of(j * SEL_CHUNK, SEL_CHUNK)
        return flash(_dot(ks_ref[pl.ds(start, SEL_CHUNK), :], qx), start, carry)

    def pair_step(j, carry):
        start = pl.multiple_of(j * (2 * SEL_CHUNK), 2 * SEL_CHUNK)
        s0 = _dot(ks_ref[pl.ds(start, SEL_CHUNK), :], qx)
        s1 = _dot(ks_ref[pl.ds(start + SEL_CHUNK, SEL_CHUNK), :], qx)
        return flash(s1, start + SEL_CHUNK, flash(s0, start, carry))

    n_full = t0 // SEL_CHUNK
    init = (jnp.full((1, cols), NEG_INF, F32), jnp.zeros((1, cols), F32), jnp.zeros((HEAD_DIM, cols), F32))
    carry = lax.fori_loop(0, n_full // 2, pair_step, init)
    carry = lax.fori_loop(n_full // 2 * 2, n_full, full_step, carry)
    start = pl.multiple_of(n_full * SEL_CHUNK, SEL_CHUNK)
    s_diag = jnp.where(start + krow <= tq, _dot(ks_ref[pl.ds(start, SEL_CHUNK), :], qx), NEG_INF)
    _, l_s, acc_s = flash(s_diag, start, carry)
    o_s = acc_s / l_s

    for h in range(NSA_HPG):
        hs = slice(h * Q_BLOCK, (h + 1) * Q_BLOCK)
        gate = lambda br: gate_ref[0, 0, br, h:h + 1, :]
        o_ref[0, h] = gate(0) * o_c[:, hs] + gate(1) * o_s[:, hs] + gate(2) * o_w[:, hs]


def nsa_attention(qn, kcv, ks, vs, kw, vw, gates, b, s):
    nq_blocks = s // Q_BLOCK
    nc = s // CMP_STRIDE
    n_cmp = nc - CMP_BLOCK // CMP_STRIDE + 1
    n_sel = s // SLC_BLOCK
    n_top = min(SLC_TOPK, n_sel)
    grp, dh = NSA_GROUPS, HEAD_DIM
    assert n_sel <= LANES and s % SEL_CHUNK == 0 and s >= WINDOW + Q_BLOCK
    c_start = np.arange(n_cmp) * CMP_STRIDE
    s_start = np.arange(n_sel) * SLC_BLOCK
    ovt = np.zeros((LANES, nc), np.float32)
    ovt[:n_sel, :n_cmp] = (np.clip(np.minimum((c_start + CMP_BLOCK)[:, None], s_start[None] + SLC_BLOCK)
                                   - np.maximum(c_start[:, None], s_start[None]), 0, None) / CMP_BLOCK).T

    def pos_features(pos):
        f = np.zeros((len(pos), dh), np.float32)
        f[:, 0] = pos // SLC_BLOCK * SLC_BLOCK
        f[:, 1] = pos % SLC_BLOCK
        f[:, 2:4] = 1.0
        return jnp.asarray(f, BF16)

    tok = np.arange(s)
    onehot = jnp.asarray(tok[:, None] // SLC_BLOCK == np.arange(LANES)[None], BF16)
    per_group = lambda a: a.reshape(b, s, grp, dh).transpose(0, 2, 1, 3)
    with_feats = lambda k, *f: jnp.concatenate(
        [k] + [jnp.broadcast_to(x, k.shape[:2] + x.shape) for x in f], axis=-1)
    tiles_t = lambda v: v.reshape(b, s // Q_BLOCK, Q_BLOCK, grp, dh).transpose(0, 3, 1, 4, 2)
    qt = qn.reshape(b, s, NSA_HEADS, dh).transpose(0, 2, 3, 1)
    ks_x = with_feats(per_group(ks), pos_features(tok), onehot)
    kw_x = with_feats(per_group(kw), pos_features(tok))
    kc_x = with_feats(kcv[0], pos_features(np.arange(nc) * CMP_STRIDE + (CMP_BLOCK - 1)))
    vct = kcv[1].transpose(0, 1, 3, 2)
    gates_t = gates[:, :NSA_HEADS * 3].reshape(b, s, grp, NSA_HPG, 3).transpose(0, 2, 4, 3, 1)

    grp_spec = lambda *shape: pl.BlockSpec((1, 1) + shape, lambda bi, g, i: (bi, g) + (0,) * len(shape),
                                           pipeline_mode=pl.Buffered(1))
    return pl.pallas_call(
        functools.partial(_nsa_attn_kernel, n_sel=n_sel, n_top=n_top),
        grid=(b, grp, nq_blocks),
        in_specs=[pl.BlockSpec((1, NSA_HPG, dh, Q_BLOCK), lambda bi, g, i: (bi, g, 0, i)),
                  grp_spec(nc, 2 * dh), grp_spec(dh, nc),
                  pl.BlockSpec((LANES, nc), lambda bi, g, i: (0, 0)),
                  grp_spec(s, 2 * dh + LANES), grp_spec(s // Q_BLOCK, dh, Q_BLOCK),
                  grp_spec(s, 2 * dh), grp_spec(s // Q_BLOCK, dh, Q_BLOCK),
                  pl.BlockSpec((1, 1, 3, NSA_HPG, Q_BLOCK), lambda bi, g, i: (bi, g, 0, 0, i))],
        out_specs=pl.BlockSpec((1, NSA_HPG, dh, Q_BLOCK), lambda bi, g, i: (bi, g, 0, i)),
        out_shape=jax.ShapeDtypeStruct((b, NSA_HEADS, dh, s), F32),
        compiler_params=_cparams("parallel", "parallel", "arbitrary"),
        name="nsa_attention",
    )(qt, kc_x, vct, jnp.asarray(ovt, BF16), ks_x, tiles_t(vs), kw_x, tiles_t(vw), gates_t)


def _rwkv_prep_kernel(p_ref, prev_ref, mu_ref, w0_ref, a0_ref, kk_ref, ka_ref, rk_ref,
                      wup_ref, aup_ref, gup_ref, bd_ref, ltri_ref, lones_ref, csum_ref,
                      at_o, bt_o, kt_o, rt_o, v_o, bw_o, kw_o, wc_o, g_o, bonus_o, *, tiles_per_seq):
    p = p_ref[...]
    tm = p.shape[0]
    first = pl.program_id(0) % tiles_per_seq == 0
    last_prev = jnp.where(first, 0.0, prev_ref[7:8, :])
    prev = pltpu.roll(p, 1, 0)
    prev = jnp.where(lax.broadcasted_iota(jnp.int32, (tm, 1), 0) == 0, last_prev, prev)
    pm = p + (prev - p) * mu_ref[...]
    d = RWKV_DIM
    r, k, v = pm[:, :d], pm[:, d:2 * d], pm[:, 2 * d:3 * d]
    lora = pm[:, 3 * d:3 * d + LANES]
    gd = pm[:, 3 * d + LANES:3 * d + 2 * LANES]
    z = -(w0_ref[...] + _dot(jnp.tanh(lora).astype(BF16), wup_ref[...]))
    softplus = jnp.maximum(z, 0.0) + jnp.log(1.0 + jnp.exp(-jnp.abs(z)))
    w = -softplus - 0.5
    a = jax.nn.sigmoid(a0_ref[...] + _dot(lora.astype(BF16), aup_ref[...]))
    g_o[...] = _dot(jax.nn.sigmoid(gd).astype(BF16), gup_ref[...])
    bd = bd_ref[...]
    kkr = k * kk_ref[...]
    kk = kkr / jnp.maximum(jnp.sqrt(_dot_x_sel(kkr * kkr, bd)), 1e-12)
    k2 = k * (1.0 + (a - 1.0) * ka_ref[...])
    bonus_o[...] = _dot_x_sel(r * k2 * rk_ref[...], bd) * v
    lw = -jnp.exp(w)
    lw_parts = _split_bf16(lw)
    cum = sum(_dot(ltri_ref[...], p) for p in lw_parts)
    tot = sum(_dot(lones_ref[...], p) for p in lw_parts)
    e_in = jnp.exp(cum)
    e_out = jnp.exp(-cum)
    e_end = jnp.exp(tot - cum)

    def put_heads(o, val):
        for h in range(RWKV_HEADS):
            o[0, h] = val[:, h * HEAD_DIM:(h + 1) * HEAD_DIM].astype(o.dtype)

    put_heads(at_o, -kk * jnp.exp(cum - lw))
    put_heads(bt_o, kk * a * e_out)
    put_heads(kt_o, k2 * e_out)
    put_heads(rt_o, r * e_in)
    put_heads(v_o, v)
    put_heads(bw_o, kk * a * e_end)
    put_heads(kw_o, k2 * e_end)
    put_heads(wc_o, jnp.exp(sum(_dot(csum_ref[...], p) for p in lw_parts)))


def rwkv_prep(p_rwkv, mu, w0, w_up, a0, a_up, g_up, k_k, k_a, r_k, b, s, tm):
    t = p_rwkv.shape[0]
    d = RWKV_DIM
    c = RWKV_CHUNK
    tps = s // tm
    cpt = tm // c
    wup = jnp.concatenate([w_up, jnp.zeros_like(a_up)], axis=0).astype(BF16)
    aup = jnp.concatenate([jnp.zeros_like(w_up), a_up], axis=0).astype(BF16)
    i = np.arange(tm)
    same = (i[:, None] // c) == (i[None, :] // c)
    ltri = jnp.asarray(same & (i[:, None] >= i[None, :]), BF16)
    lones = jnp.asarray(same, BF16)
    csum = jnp.asarray(np.arange(cpt)[:, None] == (i[None, :] // c), BF16)
    row = lambda w: pl.BlockSpec((tm, w), lambda i: (i, 0))
    full = lambda *sh: pl.BlockSpec(sh, lambda i: (0,) * len(sh))
    heads = lambda n: pl.BlockSpec((1, RWKV_HEADS, n, HEAD_DIM), lambda i: (i // tps, 0, i % tps, 0))
    hshape = lambda n, dt: jax.ShapeDtypeStruct((b, RWKV_HEADS, n, HEAD_DIM), dt)
    vec = lambda x: x.reshape(1, -1)
    return pl.pallas_call(
        functools.partial(_rwkv_prep_kernel, tiles_per_seq=tps),
        grid=(t // tm,),
        in_specs=[row(RWKV_PROJ),
                  pl.BlockSpec((8, RWKV_PROJ), lambda i: (jnp.maximum(i * (tm // 8) - 1, 0), 0)),
                  full(1, RWKV_PROJ), full(1, d), full(1, d), full(1, d), full(1, d), full(1, d),
                  full(LANES, d), full(LANES, d), full(LANES, d), full(d, d), full(tm, tm), full(tm, tm),
                  full(cpt, tm)],
        out_specs=[heads(tm)] * 7 + [heads(cpt), row(d), row(d)],
        out_shape=[hshape(s, BF16)] * 7 + [hshape(s // c, F32)] + [jax.ShapeDtypeStruct((t, d), F32)] * 2,
        compiler_params=_cparams("parallel"),
        name="rwkv_prep",
    )(p_rwkv, p_rwkv, vec(mu), vec(w0), vec(a0), vec(k_k), vec(k_a), vec(r_k), wup, aup, g_up.astype(BF16),
      _block_diag_ones(d, HEAD_DIM), ltri, lones, csum)


def _bdot(a, b):
    return lax.dot_general(a, b, (((2,), (1,)), ((0,), (0,))), preferred_element_type=F32)


def _bdot_nt(a, b):
    return lax.dot_general(a, b, (((2,), (2,)), ((0,), (0,))), preferred_element_type=F32)


def _bdot_tn(a, b):
    return lax.dot_general(a, b, (((1,), (1,)), ((0,), (0,))), preferred_element_type=F32)


def _rwkv_intra_kernel(at_ref, bt_ref, kt_ref, rt_ref, v_ref, ta_o, tr_o, arb_o, yv_o):
    c = RWKV_CHUNK
    _, nh, ts, dh = at_ref.shape
    n = nh * (ts // c)
    chunked = lambda ref: ref[0].reshape(n, c, dh)
    at, bt, kt, rt, v = (chunked(r) for r in (at_ref, bt_ref, kt_ref, rt_ref, v_ref))
    ri = lax.broadcasted_iota(jnp.int32, (1, c, c), 1)
    ci = lax.broadcasted_iota(jnp.int32, (1, c, c), 2)
    strict = ri > ci
    incl = ri >= ci
    ar = jnp.concatenate([at, rt], axis=1)
    xb = _bdot_nt(ar, bt)
    xk = _bdot_nt(ar, kt)
    l_ab = jnp.where(strict, xb[:, :c], 0.0)
    a_ak = jnp.where(strict, xk[:, :c], 0.0)
    a_rb = jnp.where(incl, xb[:, c:], 0.0)
    a_rk = jnp.where(incl, xk[:, c:], 0.0)
    pw = l_ab
    tinv = jnp.where(ri == ci, 1.0, 0.0) + l_ab
    for _ in range(int(np.log2(c)) - 1):
        pw_b = pw.astype(BF16)
        pw = _bdot(pw_b, pw_b)
        tinv = tinv + _bdot(tinv.astype(BF16), pw.astype(BF16))
    tinv_b = tinv.astype(BF16)

    def put(o, val):
        o[0] = val.reshape(nh, ts, val.shape[-1]).astype(o.dtype)

    put(ta_o, _bdot(tinv_b, at))
    put(tr_o, _bdot(tinv_b, _bdot(a_ak.astype(BF16), v).astype(BF16)))
    put(arb_o, a_rb)
    put(yv_o, _bdot(a_rk.astype(BF16), v))


def rwkv_intra(at, bt, kt, rt, v, ts):
    b, h, s, dh = at.shape
    seq = lambda: pl.BlockSpec((1, h, ts, dh), lambda bi, i: (bi, 0, i, 0))
    shp = lambda dt: jax.ShapeDtypeStruct((b, h, s, dh), dt)
    return pl.pallas_call(
        _rwkv_intra_kernel,
        grid=(b, s // ts),
        in_specs=[seq()] * 5,
        out_specs=[seq()] * 4,
        out_shape=[shp(BF16), shp(F32), shp(BF16), shp(F32)],
        compiler_params=_cparams("parallel", "parallel"),
        name="rwkv_intra",
    )(at, bt, kt, rt, v)


def _rwkv_scan_kernel(ta_ref, tr_ref, arb_ref, yv_ref, rt_ref, v_ref, bw_ref, kw_ref, wc_ref, y_ref, st_ref):
    c = RWKV_CHUNK
    nb, nh, ts, dh = ta_ref.shape
    n = nb * nh

    @pl.when(pl.program_id(0) == 0)
    def _():
        st_ref[...] = jnp.zeros_like(st_ref)

    def chunk_step(j, _):
        sl = (slice(None), slice(None), pl.ds(pl.multiple_of(j * c, c), c), slice(None))
        get = lambda ref: ref[sl].reshape(n, c, dh)
        st = st_ref[...]
        st_b = st.astype(BF16)
        u = _bdot_nt(get(ta_ref), st_b) + get(tr_ref)
        u_b = u.astype(BF16)
        y = _bdot_nt(get(rt_ref), st_b) + _bdot(get(arb_ref), u_b) + get(yv_ref)
        wc = wc_ref[:, :, pl.ds(pl.program_id(0) * (ts // c) + j, 1), :].reshape(n, 1, dh)
        st_ref[...] = st * wc + _bdot_tn(jnp.concatenate([u_b, get(v_ref)], axis=1),
                                         jnp.concatenate([get(bw_ref), get(kw_ref)], axis=1))
        y_ref[sl] = y.reshape(nb, nh, c, dh)
        return 0

    lax.fori_loop(0, ts // c, chunk_step, 0)


def rwkv_scan(ta, tr, arb, yv, rt, v, bw, kw, wc, ts):
    b, h, s, dh = ta.shape
    seq = lambda n: pl.BlockSpec((b, h, n, dh), lambda i: (0, 0, i, 0))
    return pl.pallas_call(
        _rwkv_scan_kernel,
        grid=(s // ts,),
        in_specs=[seq(ts)] * 8 + [pl.BlockSpec(wc.shape, lambda i: (0, 0, 0, 0))],
        out_specs=seq(ts),
        out_shape=jax.ShapeDtypeStruct((b, h, s, dh), F32),
        scratch_shapes=[pltpu.VMEM((b * h, dh, dh), F32)],
        compiler_params=_cparams("arbitrary"),
        name="rwkv_scan",
    )(ta, tr, arb, yv, rt, v, bw, kw, wc)


def _out_proj_kernel(x_ref, on_ref, y_ref, bonus_ref, g_ref, lnw_ref, lnb_ref, bd_ref, wn_ref, wr_ref, o_ref):
    y = jnp.concatenate([y_ref[0, h] for h in range(RWKV_HEADS)], axis=-1)
    bd = bd_ref[...]
    yc = y - _dot_x_sel(y, bd)
    yn = yc * lax.rsqrt(_dot_x_sel(yc * yc, bd) + GN_EPS)
    o_rwkv = (yn * lnw_ref[...] + lnb_ref[...] + bonus_ref[...]) * g_ref[...]
    o_ref[...] = (x_ref[...] + _dot(on_ref[...].astype(BF16), wn_ref[...])
                  + _dot(o_rwkv.astype(BF16), wr_ref[...]))


def out_proj(x, o_nsa, y, bonus, g, ln_w, ln_b, w_out, s, tm):
    t, d = x.shape
    dn = o_nsa.shape[1]
    dr = bonus.shape[1]
    tps = s // tm
    row = lambda w: pl.BlockSpec((tm, w), lambda i: (i, 0))
    full = lambda *sh: pl.BlockSpec(sh, lambda i: (0,) * len(sh))
    return pl.pallas_call(
        _out_proj_kernel,
        grid=(t // tm,),
        in_specs=[row(d), row(dn),
                  pl.BlockSpec((1, RWKV_HEADS, tm, HEAD_DIM), lambda i: (i // tps, 0, i % tps, 0)),
                  row(dr), row(dr), full(1, dr), full(1, dr), full(dr, dr), full(dn, d), full(dr, d)],
        out_specs=row(d),
        out_shape=jax.ShapeDtypeStruct((t, d), F32),
        compiler_params=_cparams("parallel"),
        name="out_proj",
    )(x, o_nsa, y, bonus, g, ln_w.reshape(1, dr), ln_b.reshape(1, dr),
      _block_diag_ones(dr, HEAD_DIM, 1.0 / HEAD_DIM), w_out[:dn].astype(BF16), w_out[dn:].astype(BF16))


def _cross_attn_kernel(h_ref, g_ref, wq_ref, qg_ref, kv_ref, kg_ref, wo_ref, o_ref):
    h = h_ref[...]
    d = h.shape[1]
    xd = d // X_HEADS
    q = _dot(_rms(h, g_ref[...]).astype(BF16), wq_ref[...])
    kv = kv_ref[0]
    outs = []
    for hd in range(X_HEADS):
        qh = _rms(q[:, hd * xd:(hd + 1) * xd], qg_ref[...]) * (xd ** -0.5)
        kh = _rms(kv[:, hd * xd:(hd + 1) * xd], kg_ref[...])
        vh = kv[:, d + hd * xd:d + (hd + 1) * xd]
        s = _dot_nt(qh.astype(BF16), kh.astype(BF16))
        p = jnp.exp(s - jnp.max(s, axis=-1, keepdims=True))
        p = p / jnp.sum(p, axis=-1, keepdims=True)
        outs.append(_dot(p.astype(BF16), vh.astype(BF16)))
    o = jnp.concatenate(outs, axis=-1)
    o_ref[...] = h + _dot(o.astype(BF16), wo_ref[...])


def cross_attention(h, kv, norm_g, xq_w, xq_g, xk_g, xo_w, b, s, tm):
    t, d = h.shape
    m = kv.shape[1]
    xd = d // X_HEADS
    tiles = s // tm
    full = lambda *sh: pl.BlockSpec(sh, lambda i: (0,) * len(sh))
    return pl.pallas_call(
        _cross_attn_kernel,
        grid=(t // tm,),
        in_specs=[pl.BlockSpec((tm, d), lambda i: (i, 0)), full(1, d), full(d, d), full(1, xd),
                  pl.BlockSpec((1, m, 2 * d), lambda i: (i // tiles, 0, 0)), full(1, xd), full(d, d)],
        out_specs=pl.BlockSpec((tm, d), lambda i: (i, 0)),
        out_shape=jax.ShapeDtypeStruct((t, d), F32),
        compiler_params=_cparams("parallel"),
        name="cross_attention",
    )(h, norm_g.reshape(1, d), xq_w.astype(BF16), xq_g.reshape(1, xd), kv, xk_g.reshape(1, xd),
      xo_w.astype(BF16))


def _router_kernel(h_ref, g_ref, rw_ref, rb_ref, xn_o, idx_o, gate_o):
    xn = _rms(h_ref[...], g_ref[...])
    xn_o[...] = xn
    logits = _dot(xn, rw_ref[...], precision=HIGHEST) + rb_ref[...]
    tm = logits.shape[0]
    lane = lax.broadcasted_iota(jnp.int32, (tm, LANES), 1)
    lanef = lane.astype(F32)
    logits = jnp.where(lane < N_EXPERTS, logits, REMOVED)
    idx_acc = jnp.zeros((tm, LANES), F32)
    val_acc = jnp.zeros((tm, LANES), F32)
    top = None
    for k in range(TOP_K):
        m = jnp.max(logits, axis=-1, keepdims=True)
        idx = jnp.min(jnp.where(logits == m, lanef, float(LANES)), axis=-1, keepdims=True)
        logits = jnp.where(lanef == idx, REMOVED, logits)
        top = m if top is None else top
        idx_acc = jnp.where(lane == k, idx, idx_acc)
        val_acc = jnp.where(lane == k, jnp.exp(m - top), val_acc)
    idx_o[...] = idx_acc.astype(jnp.int32)
    gate_o[...] = val_acc / jnp.sum(val_acc, axis=-1, keepdims=True)


def moe_router(h, norm_g, router_w, router_b, tm):
    t, d = h.shape
    rw = jnp.zeros((d, LANES), F32).at[:, :N_EXPERTS].set(router_w)
    rb = jnp.zeros((1, LANES), F32).at[0, :N_EXPERTS].set(router_b)
    row = lambda w: pl.BlockSpec((tm, w), lambda i: (i, 0))
    full = lambda *s: pl.BlockSpec(s, lambda i: (0,) * len(s))
    return pl.pallas_call(
        _router_kernel,
        grid=(t // tm,),
        in_specs=[row(d), full(1, d), full(d, LANES), full(1, LANES)],
        out_specs=[row(d), row(LANES), row(LANES)],
        out_shape=[jax.ShapeDtypeStruct((t, d), F32), jax.ShapeDtypeStruct((t, LANES), jnp.int32),
                   jax.ShapeDtypeStruct((t, LANES), F32)],
        compiler_params=_cparams("parallel"),
        name="moe_router",
    )(h, norm_g.reshape(1, d), rw, rb)


def _expert_kernel(blk_e_ref, n_used_ref, x_ref, w1g_ref, w1l_ref, b1g_ref, b1l_ref, w2_ref, b2_ref, o_ref):
    i = pl.program_id(0)

    @pl.when(i < n_used_ref[0])
    def _():
        x = x_ref[...].astype(BF16)
        hg = jnp.minimum(_dot(x, w1g_ref[0]) + b1g_ref[0], SWIGLU_LIMIT)
        hl = jnp.clip(_dot(x, w1l_ref[0]) + b1l_ref[0], -SWIGLU_LIMIT, SWIGLU_LIMIT)
        act = hg * jax.nn.sigmoid(SWIGLU_ALPHA * hg) * (hl + 1.0)
        o_ref[...] = _dot(act.astype(BF16), w2_ref[0].astype(BF16)) + b2_ref[0]

    @pl.when(i >= n_used_ref[0])
    def _():
        o_ref[...] = jnp.zeros_like(o_ref)


def moe_experts(xs, blk_e, n_used, w1g, w1l, b1g, b1l, w2, b2):
    r, d = xs.shape
    f = w1g.shape[2]
    m = MOE_ROW_BLOCK
    ex = lambda *s: pl.BlockSpec((1,) + s, lambda i, be, nu: (be[i], 0, 0))
    grid_spec = pltpu.PrefetchScalarGridSpec(
        num_scalar_prefetch=2,
        grid=(r // m,),
        in_specs=[pl.BlockSpec((m, d), lambda i, be, nu: (i, 0)),
                  ex(d, f), ex(d, f), ex(1, f), ex(1, f), ex(f, d), ex(1, d)],
        out_specs=pl.BlockSpec((m, d), lambda i, be, nu: (i, 0)),
    )
    return pl.pallas_call(
        _expert_kernel,
        grid_spec=grid_spec,
        out_shape=jax.ShapeDtypeStruct((r, d), F32),
        compiler_params=_cparams("arbitrary"),
        name="moe_experts",
    )(blk_e, n_used, xs, w1g, w1l, b1g, b1l, w2, b2)


def _combine_kernel(h_ref, y_ref, gate_ref, o_ref):
    d = h_ref.shape[1]
    acc = h_ref[...]
    for k in range(TOP_K):
        acc = acc + gate_ref[:, k:k + 1] * y_ref[:, k * d:(k + 1) * d]
    o_ref[...] = acc


def moe_combine(h, y4, gate, tm):
    t, d = h.shape
    row = lambda w: pl.BlockSpec((tm, w), lambda i: (i, 0))
    return pl.pallas_call(
        _combine_kernel,
        grid=(t // tm,),
        in_specs=[row(d), row(TOP_K * d), row(LANES)],
        out_specs=row(d),
        out_shape=jax.ShapeDtypeStruct((t, d), F32),
        compiler_params=_cparams("parallel"),
        name="moe_combine",
    )(h, y4, gate)


def _layer(x, mem, norm_mix_g, w_in, q_norm_g, k_cmp_norm_g, k_slc_norm_g, k_win_norm_g,
           cmp_pe_k, cmp_pe_v, cmp_k_w1, cmp_k_w2, cmp_v_w1, cmp_v_w2,
           rwkv_mu, rwkv_w0, rwkv_w_up, rwkv_a0, rwkv_a_up, rwkv_g_up, rwkv_k_k, rwkv_k_a,
           rwkv_r_k, rwkv_ln_w, rwkv_ln_b, w_out,
           norm_x_g, norm_mem_g, xq_w, xk_w, xv_w, xq_norm_g, xk_norm_g, xo_w,
           norm_ffn_g, router_w, router_b, mlp1_w, mlp1_b, mlp2_w, mlp2_b):
    b, s, d = x.shape
    t = b * s
    tm = 512
    xt = x.reshape(t, d)

    w_nsa = jnp.pad(w_in[:, :NSA_PROJ], ((0, 0), (0, NSA_PROJ_PAD - NSA_PROJ)))
    p_nsa = norm_matmul(xt, norm_mix_g, w_nsa, tm)
    p_rwkv = norm_matmul(xt, norm_mix_g, w_in[:, NSA_PROJ:], tm)

    nq = NSA_HEADS * HEAD_DIM
    gw = NSA_GROUPS * HEAD_DIM
    qn, ks, vs, kw, vw, gates = nsa_prep(p_nsa, q_norm_g, k_slc_norm_g, k_win_norm_g, tm)
    kcv = nsa_compress(p_nsa[:, nq:nq + gw], p_nsa[:, nq + gw:nq + 2 * gw], cmp_pe_k, cmp_pe_v,
                       cmp_k_w1, cmp_k_w2, cmp_v_w1, cmp_v_w2, k_cmp_norm_g, b, s)
    o_nsa = nsa_attention(qn, kcv, ks, vs, kw, vw, gates, b, s).transpose(0, 3, 1, 2).reshape(t, nq)

    at, bt, kt, rt, v, bw, kwd, wc, g_gate, bonus = rwkv_prep(
        p_rwkv, rwkv_mu, rwkv_w0, rwkv_w_up, rwkv_a0, rwkv_a_up, rwkv_g_up, rwkv_k_k, rwkv_k_a, rwkv_r_k, b, s, tm)
    ta, tr, arb, yv = rwkv_intra(at, bt, kt, rt, v, ts=256)
    y = rwkv_scan(ta, tr, arb, yv, rt, v, bw, kwd, wc, ts=256)

    h1 = out_proj(xt, o_nsa, y, bonus, g_gate, rwkv_ln_w, rwkv_ln_b, w_out, s, tm)
    m = mem.shape[1]
    kv = norm_matmul(mem.reshape(b * m, d), norm_mem_g, jnp.concatenate([xk_w, xv_w], axis=1), m)
    h2 = cross_attention(h1, kv.reshape(b, m, 2 * d), norm_x_g, xq_w, xq_norm_g, xk_norm_g, xo_w, b, s, tm)

    xn, top_i, gate = moe_router(h2, norm_ffn_g, router_w, router_b, tm)
    top_i = top_i[:, :TOP_K]
    a = t * TOP_K
    mb = MOE_ROW_BLOCK
    e_flat = top_i.reshape(a)
    order = jnp.argsort(e_flat, stable=True)
    e_sorted = e_flat[order]
    counts = jnp.bincount(e_flat, length=N_EXPERTS)
    starts = jnp.cumsum(counts) - counts
    padded = (counts + mb - 1) // mb * mb
    pends = jnp.cumsum(padded)
    pstarts = pends - padded
    dest = (pstarts[e_sorted] + jnp.arange(a) - starts[e_sorted]).astype(jnp.int32)
    n_blocks = -(-a // mb) + N_EXPERTS
    r = n_blocks * mb
    blk_e = jnp.minimum(jnp.searchsorted(pends, jnp.arange(n_blocks) * mb, side='right'),
                        N_EXPERTS - 1).astype(jnp.int32)
    row_e = jnp.repeat(blk_e, mb)
    src_i = jnp.arange(r) - (pstarts - starts)[row_e]
    row_src = jnp.where(src_i < (starts + counts)[row_e], order[jnp.minimum(src_i, a - 1)] // TOP_K, 0)
    row_src = row_src.astype(jnp.int32)
    n_used = (pends[-1] // mb).astype(jnp.int32).reshape(1)
    xs = xn.at[row_src].get(mode="promise_in_bounds")
    f = mlp1_w.shape[2] // 2
    w1 = mlp1_w.reshape(N_EXPERTS, d, f, 2)
    b1 = mlp1_b.reshape(N_EXPERTS, 1, f, 2)
    ys = moe_experts(xs, blk_e, n_used,
                     w1[..., 0].astype(BF16), w1[..., 1].astype(BF16), b1[..., 0], b1[..., 1],
                     mlp2_w, mlp2_b.reshape(N_EXPERTS, 1, d))
    pos = jnp.zeros((a,), jnp.int32).at[order].set(dest, unique_indices=True, mode="promise_in_bounds")
    y4 = ys.at[pos].get(mode="promise_in_bounds").reshape(t, TOP_K * d)
    out = moe_combine(h2, y4, gate, 256)
    return out.reshape(b, s, d)


def kernel(x, mem, norm_mix_g, w_in, q_norm_g, k_cmp_norm_g, k_slc_norm_g, k_win_norm_g, cmp_pe_k, cmp_pe_v, cmp_k_w1, cmp_k_w2, cmp_v_w1, cmp_v_w2, rwkv_mu, rwkv_w0, rwkv_w_up, rwkv_a0, rwkv_a_up, rwkv_g_up, rwkv_k_k, rwkv_k_a, rwkv_r_k, rwkv_ln_w, rwkv_ln_b, w_out, norm_x_g, norm_mem_g, xq_w, xk_w, xv_w, xq_norm_g, xk_norm_g, xo_w, norm_ffn_g, router_w, router_b, mlp1_w, mlp1_b, mlp2_w, mlp2_b):
    params = (norm_mix_g, w_in, q_norm_g, k_cmp_norm_g, k_slc_norm_g, k_win_norm_g, cmp_pe_k, cmp_pe_v,
              cmp_k_w1, cmp_k_w2, cmp_v_w1, cmp_v_w2, rwkv_mu, rwkv_w0, rwkv_w_up, rwkv_a0, rwkv_a_up,
              rwkv_g_up, rwkv_k_k, rwkv_k_a, rwkv_r_k, rwkv_ln_w, rwkv_ln_b, w_out, norm_x_g, norm_mem_g,
              xq_w, xk_w, xv_w, xq_norm_g, xk_norm_g, xo_w, norm_ffn_g, router_w, router_b,
              mlp1_w, mlp1_b, mlp2_w, mlp2_b)
    h = x
    for layer in range(norm_mix_g.shape[0]):
        h = _layer(h, mem, *[prm[layer] for prm in params])
    return h
```

```python
import functools

import numpy as np
import jax
import jax.numpy as jnp
from jax import lax
from jax.experimental import pallas as pl
from jax.experimental.pallas import tpu as pltpu

F32 = jnp.float32
BF16 = jnp.bfloat16
HIGHEST = lax.Precision.HIGHEST

V7X_VMEM_BYTES = 64 * 1024 * 1024
VMEM_LIMIT = V7X_VMEM_BYTES * 3 // 4

HEAD_DIM = 64
NSA_HEADS = 8
NSA_GROUPS = 2
NSA_HPG = NSA_HEADS // NSA_GROUPS
GROUP_W = NSA_HPG * HEAD_DIM
CMP_BLOCK = 32
CMP_STRIDE = 16
SLC_BLOCK = 64
SLC_TOPK = 16
WINDOW = 512
Q_BLOCK = 128
SEL_CHUNK = 512
RWKV_HEADS = 8
RWKV_DIM = RWKV_HEADS * HEAD_DIM
RWKV_CHUNK = 64
GN_EPS = HEAD_DIM * 1e-5
X_HEADS = 4
N_EXPERTS = 32
TOP_K = 4
SWIGLU_LIMIT = 7.0
SWIGLU_ALPHA = 1.702
MOE_ROW_BLOCK = 256
RMS_EPS = 1e-6
NEG_INF = -1e30
BIG = 1e9
REMOVED = -3e38
LANES = 128

NSA_PROJ = NSA_HEADS * HEAD_DIM + 6 * NSA_GROUPS * HEAD_DIM + NSA_HEADS * 3
NSA_PROJ_PAD = -(-NSA_PROJ // LANES) * LANES
RWKV_PROJ = 3 * RWKV_DIM + 64 + 64 + 128


def _cparams(*sem):
    return pltpu.CompilerParams(dimension_semantics=sem, vmem_limit_bytes=VMEM_LIMIT)


def _dot(a, b, **kw):
    return jnp.dot(a, b, preferred_element_type=F32, **kw)


def _dot_nt(a, b, **kw):
    return lax.dot_general(a, b, (((1,), (1,)), ((), ())), preferred_element_type=F32, **kw)


def _dot_tn(a, b, **kw):
    return lax.dot_general(a, b, (((0,), (0,)), ((), ())), preferred_element_type=F32, **kw)


def _rms(x, g):
    return x * lax.rsqrt(jnp.mean(x * x, axis=-1, keepdims=True) + RMS_EPS) * g


def _split_bf16(x, terms=3):
    parts = []
    for _ in range(terms):
        hi = x.astype(BF16)
        parts.append(hi)
        x = x - hi.astype(F32)
    return parts


def _dot_x_sel(x, sel):
    return sum(_dot(p, sel) for p in _split_bf16(x))


def _dot_sel_x(sel, x):
    return sum(_dot(sel, p) for p in _split_bf16(x))


def _block_diag_ones(n, blk, scale=1.0):
    i = np.arange(n)
    return jnp.asarray(((i[:, None] // blk) == (i[None, :] // blk)).astype(np.float32) * scale, BF16)


def _norm_matmul_kernel(x_ref, g_ref, w_ref, o_ref):
    xn = _rms(x_ref[...], g_ref[...]).astype(BF16)
    o_ref[...] = _dot(xn, w_ref[...])


def norm_matmul(x, g, w, tm):
    m, d = x.shape
    n = w.shape[1]
    return pl.pallas_call(
        _norm_matmul_kernel,
        grid=(m // tm,),
        in_specs=[pl.BlockSpec((tm, d), lambda i: (i, 0)),
                  pl.BlockSpec((1, d), lambda i: (0, 0)),
                  pl.BlockSpec((d, n), lambda i: (0, 0))],
        out_specs=pl.BlockSpec((tm, n), lambda i: (i, 0)),
        out_shape=jax.ShapeDtypeStruct((m, n), F32),
        compiler_params=_cparams("parallel"),
        name="norm_matmul",
    )(x, g.reshape(1, d), w.astype(BF16))


def _nsa_prep_kernel(p_ref, qg_ref, ksg_ref, kwg_ref, bdq_ref, bdk_ref, q_o, ks_o, vs_o, kw_o, vw_o, gate_o):
    p = p_ref[...]
    nq = NSA_HEADS * HEAD_DIM
    gw = NSA_GROUPS * HEAD_DIM
    q = p[:, :nq]
    msq = _dot_x_sel(q * q, bdq_ref[...])
    q_o[...] = (q * lax.rsqrt(msq + RMS_EPS) * qg_ref[...] * (HEAD_DIM ** -0.5)).astype(BF16)

    def seg(k):
        return p[:, nq + k * gw: nq + (k + 1) * gw]

    def head_norm(t, g):
        ms = _dot_x_sel(t * t, bdk_ref[...])
        return t * lax.rsqrt(ms + RMS_EPS) * g

    ks_o[...] = head_norm(seg(2), ksg_ref[...]).astype(BF16)
    vs_o[...] = seg(3).astype(BF16)
    kw_o[...] = head_norm(seg(4), kwg_ref[...]).astype(BF16)
    vw_o[...] = seg(5).astype(BF16)
    gate_o[...] = jax.nn.sigmoid(p[:, nq + 6 * gw: nq + 6 * gw + LANES])


def nsa_prep(p_nsa, q_g, ks_g, kw_g, tm):
    t = p_nsa.shape[0]
    nq = NSA_HEADS * HEAD_DIM
    gw = NSA_GROUPS * HEAD_DIM
    tile = lambda v, n: jnp.tile(v.reshape(1, HEAD_DIM), (1, n))
    row = lambda w: pl.BlockSpec((tm, w), lambda i: (i, 0))
    full = lambda *s: pl.BlockSpec(s, lambda i: (0,) * len(s))
    return pl.pallas_call(
        _nsa_prep_kernel,
        grid=(t // tm,),
        in_specs=[row(NSA_PROJ_PAD), full(1, nq), full(1, gw), full(1, gw), full(nq, nq), full(gw, gw)],
        out_specs=[row(nq)] + [row(gw)] * 4 + [row(LANES)],
        out_shape=[jax.ShapeDtypeStruct((t, nq), BF16)] + [jax.ShapeDtypeStruct((t, gw), BF16)] * 4
                  + [jax.ShapeDtypeStruct((t, LANES), F32)],
        compiler_params=_cparams("parallel"),
        name="nsa_prep",
    )(p_nsa, tile(q_g, NSA_HEADS), tile(ks_g, NSA_GROUPS), tile(kw_g, NSA_GROUPS),
      _block_diag_ones(nq, HEAD_DIM, 1.0 / HEAD_DIM), _block_diag_ones(gw, HEAD_DIM, 1.0 / HEAD_DIM))


def _compress_kernel(ch_ref, pe_ref, w1_ref, w2_ref, g_ref, o_ref):
    ch = ch_ref[0, 0, 0]
    nc = ch.shape[0]
    half = CMP_STRIDE * HEAD_DIM
    nxt = pltpu.roll(ch, nc - 1, 0)
    w1 = w1_ref[0]
    h1 = (_dot(ch, w1[:half], precision=HIGHEST) + _dot(nxt, w1[half:], precision=HIGHEST)
          + _dot(pe_ref[0], w1, precision=HIGHEST))
    out = _dot(jax.nn.silu(h1), w2_ref[0], precision=HIGHEST)
    out = jnp.where(pl.program_id(0) == 0, _rms(out, g_ref[...]), out)
    o_ref[0, 0, 0] = out.astype(BF16)


def nsa_compress(kc, vc, pe_k, pe_v, kw1, kw2, vw1, vw2, kc_g, b, s):
    nc = s // CMP_STRIDE
    half = CMP_STRIDE * HEAD_DIM

    def chunks(t):
        return t.reshape(b, nc, CMP_STRIDE, NSA_GROUPS, HEAD_DIM).transpose(0, 3, 1, 2, 4).reshape(
            b, NSA_GROUPS, nc, half)

    ch = jnp.stack([chunks(kc), chunks(vc)])
    pe = jnp.stack([pe_k.reshape(1, 2 * half), pe_v.reshape(1, 2 * half)])
    return pl.pallas_call(
        _compress_kernel,
        grid=(2, b, NSA_GROUPS),
        in_specs=[pl.BlockSpec((1, 1, 1, nc, half), lambda kv, bi, g: (kv, bi, g, 0, 0)),
                  pl.BlockSpec((1, 1, 2 * half), lambda kv, bi, g: (kv, 0, 0)),
                  pl.BlockSpec((1, 2 * half, HEAD_DIM), lambda kv, bi, g: (kv, 0, 0)),
                  pl.BlockSpec((1, HEAD_DIM, HEAD_DIM), lambda kv, bi, g: (kv, 0, 0)),
                  pl.BlockSpec((1, HEAD_DIM), lambda kv, bi, g: (0, 0))],
        out_specs=pl.BlockSpec((1, 1, 1, nc, HEAD_DIM), lambda kv, bi, g: (kv, bi, g, 0, 0)),
        out_shape=jax.ShapeDtypeStruct((2, b, NSA_GROUPS, nc, HEAD_DIM), BF16),
        compiler_params=_cparams("parallel", "parallel", "parallel"),
        name="nsa_compress",
    )(ch, pe, jnp.stack([kw1, vw1]), jnp.stack([kw2, vw2]), kc_g.reshape(1, HEAD_DIM))


def _masked_exp_cols(s, mask):
    sm = jnp.where(mask, s, NEG_INF)
    m = jnp.max(sm, axis=0, keepdims=True)
    p = jnp.exp(sm - jnp.where(m > 0.5 * NEG_INF, m, 0.0))
    l = jnp.sum(p, axis=0, keepdims=True)
    return p, 1.0 / jnp.where(l > 0.0, l, 1.0)


def _nsa_attn_kernel(qt_ref, kc_ref, vct_ref, ovt_ref, ks_ref, vst_ref, kw_ref, vwt_ref, gate_ref, o_ref,
                     *, n_sel, n_top):
    g = pl.program_id(1)
    t0 = pl.program_id(2) * Q_BLOCK
    ks_ref, vst_ref, kw_ref, vwt_ref = (r.at[0, 0] for r in (ks_ref, vst_ref, kw_ref, vwt_ref))
    cols = NSA_HPG * Q_BLOCK
    col = lax.broadcasted_iota(jnp.int32, (1, cols), 1)
    tq = t0 + col % Q_BLOCK
    head = g * NSA_HPG + col // Q_BLOCK
    slope = lax.bitcast_convert_type((127 - (head + 1)) << 23, F32)
    qt = jnp.concatenate([qt_ref[0, h] for h in range(NSA_HPG)], axis=1)
    frow = lax.broadcasted_iota(jnp.int32, (HEAD_DIM, cols), 0)
    tq_hi = (tq // SLC_BLOCK * SLC_BLOCK).astype(F32)
    tq_lo = (tq % SLC_BLOCK).astype(F32)
    qpos = jnp.where(frow < 2, slope, jnp.where(frow == 2, -slope * tq_hi, jnp.where(frow == 3, -slope * tq_lo, 0.0)))
    qc = jnp.concatenate([qt, qpos.astype(BF16)], axis=0)

    def heads_sum(x):
        acc = x[:, :Q_BLOCK]
        for h in range(1, NSA_HPG):
            acc = acc + x[:, h * Q_BLOCK:(h + 1) * Q_BLOCK]
        return acc

    kc = kc_ref[0, 0]
    nc = kc.shape[0]
    c_last = lax.broadcasted_iota(jnp.int32, (nc, 1), 0) * CMP_STRIDE + (CMP_BLOCK - 1)
    p_c, inv_c = _masked_exp_cols(_dot(kc, qc), c_last <= tq)
    p_c = p_c * inv_c
    o_c = _dot(vct_ref[0, 0], p_c.astype(BF16))
    imp = _dot_sel_x(ovt_ref[...], heads_sum(p_c))

    bid = lax.broadcasted_iota(jnp.int32, (LANES, Q_BLOCK), 0)
    bidf = bid.astype(F32)
    tq1 = t0 + lax.broadcasted_iota(jnp.int32, (1, Q_BLOCK), 1)
    cur = tq1 // SLC_BLOCK
    forced = (bid == 0) | (bid == cur) | (bid == cur - 1)
    score = jnp.where(forced, BIG, jnp.where(bid * SLC_BLOCK <= tq1, imp, -BIG))
    score = jnp.where(bid < n_sel, score, REMOVED)
    picked = jnp.zeros((LANES, Q_BLOCK), jnp.bool_)
    for _ in range(n_top):
        m = jnp.max(score, axis=0, keepdims=True)
        pick = bidf == jnp.min(jnp.where(score == m, bidf, float(LANES)), axis=0, keepdims=True)
        picked = picked | pick
        score = jnp.where(pick, REMOVED, score)
    sel_neg = jnp.where(picked, 0.0, NEG_INF).astype(BF16)
    qx = jnp.concatenate([qc, jnp.concatenate([sel_neg] * NSA_HPG, axis=1)], axis=0)

    def values_t(ref, first_tile, n_tiles):
        return jnp.concatenate([ref[first_tile + i] for i in range(n_tiles)], axis=1)

    span = WINDOW + Q_BLOCK
    ws = pl.multiple_of(jnp.maximum(t0 - WINDOW, 0), Q_BLOCK)
    dw = tq - (ws + lax.broadcasted_iota(jnp.int32, (span, 1), 0))
    in_window = dw.astype(jnp.uint32) < WINDOW
    p_w, inv_w = _masked_exp_cols(_dot(kw_ref[pl.ds(ws, span), :], qc), in_window)
    o_w = _dot(values_t(vwt_ref, ws // Q_BLOCK, span // Q_BLOCK), p_w.astype(BF16)) * inv_w

    krow = lax.broadcasted_iota(jnp.int32, (SEL_CHUNK, 1), 0)

    def flash(s, start, carry):
        m, l, acc = carry
        m_new = jnp.maximum(m, jnp.max(s, axis=0, keepdims=True))
        alpha = jnp.exp(m - m_new)
        p = jnp.exp(s - m_new)
        l = alpha * l + jnp.sum(p, axis=0, keepdims=True)
        v = values_t(vst_ref, start // Q_BLOCK, SEL_CHUNK // Q_BLOCK)
        return m_new, l, alpha * acc + _dot(v, p.astype(BF16))

    bpc = SEL_CHUNK // SLC_BLOCK
    blk_any = jnp.max(jnp.where(picked, 1.0, 0.0), axis=1, keepdims=True)
    chunk_bit = lax.bitcast_convert_type((bid[:, :1] // bpc + 127) << 23, F32)
    bits = jnp.max((blk_any * chunk_bit).reshape(LANES // bpc, bpc, 1), axis=1)
    active = jnp.sum(bits, axis=0, keepdims=True)[0, 0].astype(jnp.int32)

    def full_step(j, carry):
        start = pl.multiple_of(j * SEL_CHUNK, SEL_CHUNK)
        return lax.cond((active >> j) & 1 == 1,
                        lambda c: flash(_dot(ks_ref[pl.ds(start, SEL_CHUNK), :], qx), start, c),
                        lambda c: c, carry)

    n_full = t0 // SEL_CHUNK
    init = (jnp.full((1, cols), NEG_INF, F32), jnp.zeros((1, cols), F32), jnp.zeros((HEAD_DIM, cols), F32))
    carry = lax.fori_loop(0, n_full, full_step, init)
    start = pl.multiple_of(n_full * SEL_CHUNK, SEL_CHUNK)
    s_diag = jnp.where(start + krow <= tq, _dot(ks_ref[pl.ds(start, SEL_CHUNK), :], qx), NEG_INF)
    _, l_s, acc_s = flash(s_diag, start, carry)
    o_s = acc_s / l_s

    for h in range(NSA_HPG):
        hs = slice(h * Q_BLOCK, (h + 1) * Q_BLOCK)
        gate = lambda br: gate_ref[0, 0, br, h:h + 1, :]
        o_ref[0, h] = gate(0) * o_c[:, hs] + gate(1) * o_s[:, hs] + gate(2) * o_w[:, hs]


def nsa_attention(qn, kcv, ks, vs, kw, vw, gates, b, s):
    nq_blocks = s // Q_BLOCK
    nc = s // CMP_STRIDE
    n_cmp = nc - CMP_BLOCK // CMP_STRIDE + 1
    n_sel = s // SLC_BLOCK
    n_top = min(SLC_TOPK, n_sel)
    grp, dh = NSA_GROUPS, HEAD_DIM
    assert n_sel <= LANES and s % SEL_CHUNK == 0 and s >= WINDOW + Q_BLOCK
    c_start = np.arange(n_cmp) * CMP_STRIDE
    s_start = np.arange(n_sel) * SLC_BLOCK
    ovt = np.zeros((LANES, nc), np.float32)
    ovt[:n_sel, :n_cmp] = (np.clip(np.minimum((c_start + CMP_BLOCK)[:, None], s_start[None] + SLC_BLOCK)
                                   - np.maximum(c_start[:, None], s_start[None]), 0, None) / CMP_BLOCK).T

    def pos_features(pos):
        f = np.zeros((len(pos), dh), np.float32)
        f[:, 0] = pos // SLC_BLOCK * SLC_BLOCK
        f[:, 1] = pos % SLC_BLOCK
        f[:, 2:4] = 1.0
        return jnp.asarray(f, BF16)

    tok = np.arange(s)
    onehot = jnp.asarray(tok[:, None] // SLC_BLOCK == np.arange(LANES)[None], BF16)
    per_group = lambda a: a.reshape(b, s, grp, dh).transpose(0, 2, 1, 3)
    with_feats = lambda k, *f: jnp.concatenate(
        [k] + [jnp.broadcast_to(x, k.shape[:2] + x.shape) for x in f], axis=-1)
    tiles_t = lambda v: v.reshape(b, s // Q_BLOCK, Q_BLOCK, grp, dh).transpose(0, 3, 1, 4, 2)
    qt = qn.reshape(b, s, NSA_HEADS, dh).transpose(0, 2, 3, 1)
    ks_x = with_feats(per_group(ks), pos_features(tok), onehot)
    kw_x = with_feats(per_group(kw), pos_features(tok))
    kc_x = with_feats(kcv[0], pos_features(np.arange(nc) * CMP_STRIDE + (CMP_BLOCK - 1)))
    vct = kcv[1].transpose(0, 1, 3, 2)
    gates_t = gates[:, :NSA_HEADS * 3].reshape(b, s, grp, NSA_HPG, 3).transpose(0, 2, 4, 3, 1)

    grp_spec = lambda *shape: pl.BlockSpec((1, 1) + shape, lambda bi, g, i: (bi, g) + (0,) * len(shape),
                                           pipeline_mode=pl.Buffered(1))
    return pl.pallas_call(
        functools.partial(_nsa_attn_kernel, n_sel=n_sel, n_top=n_top),
        grid=(b, grp, nq_blocks),
        in_specs=[pl.BlockSpec((1, NSA_HPG, dh, Q_BLOCK), lambda bi, g, i: (bi, g, 0, i)),
                  grp_spec(nc, 2 * dh), grp_spec(dh, nc),
                  pl.BlockSpec((LANES, nc), lambda bi, g, i: (0, 0)),
                  grp_spec(s, 2 * dh + LANES), grp_spec(s // Q_BLOCK, dh, Q_BLOCK),
                  grp_spec(s, 2 * dh), grp_spec(s // Q_BLOCK, dh, Q_BLOCK),
                  pl.BlockSpec((1, 1, 3, NSA_HPG, Q_BLOCK), lambda bi, g, i: (bi, g, 0, 0, i))],
        out_specs=pl.BlockSpec((1, NSA_HPG, dh, Q_BLOCK), lambda bi, g, i: (bi, g, 0, i)),
        out_shape=jax.ShapeDtypeStruct((b, NSA_HEADS, dh, s), F32),
        compiler_params=_cparams("parallel", "parallel", "arbitrary"),
        name="nsa_attention",
    )(qt, kc_x, vct, jnp.asarray(ovt, BF16), ks_x, tiles_t(vs), kw_x, tiles_t(vw), gates_t)


def _rwkv_prep_kernel(p_ref, prev_ref, mu_ref, w0_ref, a0_ref, kk_ref, ka_ref, rk_ref,
                      wup_ref, aup_ref, gup_ref, bd_ref, ltri_ref, lones_ref, csum_ref,
                      at_o, bt_o, kt_o, rt_o, v_o, bw_o, kw_o, wc_o, g_o, bonus_o, *, tiles_per_seq):
    p = p_ref[...]
    tm = p.shape[0]
    first = pl.program_id(0) % tiles_per_seq == 0
    last_prev = jnp.where(first, 0.0, prev_ref[7:8, :])
    prev = pltpu.roll(p, 1, 0)
    prev = jnp.where(lax.broadcasted_iota(jnp.int32, (tm, 1), 0) == 0, last_prev, prev)
    pm = p + (prev - p) * mu_ref[...]
    d = RWKV_DIM
    r, k, v = pm[:, :d], pm[:, d:2 * d], pm[:, 2 * d:3 * d]
    lora = pm[:, 3 * d:3 * d + LANES]
    gd = pm[:, 3 * d + LANES:3 * d + 2 * LANES]
    z = -(w0_ref[...] + _dot(jnp.tanh(lora).astype(BF16), wup_ref[...]))
    softplus = jnp.maximum(z, 0.0) + jnp.log(1.0 + jnp.exp(-jnp.abs(z)))
    w = -softplus - 0.5
    a = jax.nn.sigmoid(a0_ref[...] + _dot(lora.astype(BF16), aup_ref[...]))
    g_o[...] = _dot(jax.nn.sigmoid(gd).astype(BF16), gup_ref[...])
    bd = bd_ref[...]
    kkr = k * kk_ref[...]
    kk = kkr / jnp.maximum(jnp.sqrt(_dot_x_sel(kkr * kkr, bd)), 1e-12)
    k2 = k * (1.0 + (a - 1.0) * ka_ref[...])
    bonus_o[...] = _dot_x_sel(r * k2 * rk_ref[...], bd) * v
    lw = -jnp.exp(w)
    lw_parts = _split_bf16(lw)
    cum = sum(_dot(ltri_ref[...], p) for p in lw_parts)
    tot = sum(_dot(lones_ref[...], p) for p in lw_parts)
    e_in = jnp.exp(cum)
    e_out = jnp.exp(-cum)
    e_end = jnp.exp(tot - cum)

    def put_heads(o, val):
        for h in range(RWKV_HEADS):
            o[0, h] = val[:, h * HEAD_DIM:(h + 1) * HEAD_DIM].astype(o.dtype)

    put_heads(at_o, -kk * jnp.exp(cum - lw))
    put_heads(bt_o, kk * a * e_out)
    put_heads(kt_o, k2 * e_out)
    put_heads(rt_o, r * e_in)
    put_heads(v_o, v)
    put_heads(bw_o, kk * a * e_end)
    put_heads(kw_o, k2 * e_end)
    put_heads(wc_o, jnp.exp(sum(_dot(csum_ref[...], p) for p in lw_parts)))


def rwkv_prep(p_rwkv, mu, w0, w_up, a0, a_up, g_up, k_k, k_a, r_k, b, s, tm):
    t = p_rwkv.shape[0]
    d = RWKV_DIM
    c = RWKV_CHUNK
    tps = s // tm
    cpt = tm // c
    wup = jnp.concatenate([w_up, jnp.zeros_like(a_up)], axis=0).astype(BF16)
    aup = jnp.concatenate([jnp.zeros_like(w_up), a_up], axis=0).astype(BF16)
    i = np.arange(tm)
    same = (i[:, None] // c) == (i[None, :] // c)
    ltri = jnp.asarray(same & (i[:, None] >= i[None, :]), BF16)
    lones = jnp.asarray(same, BF16)
    csum = jnp.asarray(np.arange(cpt)[:, None] == (i[None, :] // c), BF16)
    row = lambda w: pl.BlockSpec((tm, w), lambda i: (i, 0))
    full = lambda *sh: pl.BlockSpec(sh, lambda i: (0,) * len(sh))
    heads = lambda n: pl.BlockSpec((1, RWKV_HEADS, n, HEAD_DIM), lambda i: (i // tps, 0, i % tps, 0))
    hshape = lambda n, dt: jax.ShapeDtypeStruct((b, RWKV_HEADS, n, HEAD_DIM), dt)
    vec = lambda x: x.reshape(1, -1)
    return pl.pallas_call(
        functools.partial(_rwkv_prep_kernel, tiles_per_seq=tps),
        grid=(t // tm,),
        in_specs=[row(RWKV_PROJ),
                  pl.BlockSpec((8, RWKV_PROJ), lambda i: (jnp.maximum(i * (tm // 8) - 1, 0), 0)),
                  full(1, RWKV_PROJ), full(1, d), full(1, d), full(1, d), full(1, d), full(1, d),
                  full(LANES, d), full(LANES, d), full(LANES, d), full(d, d), full(tm, tm), full(tm, tm),
                  full(cpt, tm)],
        out_specs=[heads(tm)] * 7 + [heads(cpt), row(d), row(d)],
        out_shape=[hshape(s, BF16)] * 7 + [hshape(s // c, F32)] + [jax.ShapeDtypeStruct((t, d), F32)] * 2,
        compiler_params=_cparams("parallel"),
        name="rwkv_prep",
    )(p_rwkv, p_rwkv, vec(mu), vec(w0), vec(a0), vec(k_k), vec(k_a), vec(r_k), wup, aup, g_up.astype(BF16),
      _block_diag_ones(d, HEAD_DIM), ltri, lones, csum)


def _bdot(a, b):
    return lax.dot_general(a, b, (((2,), (1,)), ((0,), (0,))), preferred_element_type=F32)


def _bdot_nt(a, b):
    return lax.dot_general(a, b, (((2,), (2,)), ((0,), (0,))), preferred_element_type=F32)


def _bdot_tn(a, b):
    return lax.dot_general(a, b, (((1,), (1,)), ((0,), (0,))), preferred_element_type=F32)


def _rwkv_intra_kernel(at_ref, bt_ref, kt_ref, rt_ref, v_ref, ta_o, tr_o, arb_o, yv_o):
    c = RWKV_CHUNK
    _, nh, ts, dh = at_ref.shape
    n = nh * (ts // c)
    chunked = lambda ref: ref[0].reshape(n, c, dh)
    at, bt, kt, rt, v = (chunked(r) for r in (at_ref, bt_ref, kt_ref, rt_ref, v_ref))
    ri = lax.broadcasted_iota(jnp.int32, (1, c, c), 1)
    ci = lax.broadcasted_iota(jnp.int32, (1, c, c), 2)
    strict = ri > ci
    incl = ri >= ci
    ar = jnp.concatenate([at, rt], axis=1)
    xb = _bdot_nt(ar, bt)
    xk = _bdot_nt(ar, kt)
    l_ab = jnp.where(strict, xb[:, :c], 0.0)
    a_ak = jnp.where(strict, xk[:, :c], 0.0)
    a_rb = jnp.where(incl, xb[:, c:], 0.0)
    a_rk = jnp.where(incl, xk[:, c:], 0.0)
    pw = l_ab
    tinv = jnp.where(ri == ci, 1.0, 0.0) + l_ab
    for _ in range(int(np.log2(c)) - 1):
        pw_b = pw.astype(BF16)
        pw = _bdot(pw_b, pw_b)
        tinv = tinv + _bdot(tinv.astype(BF16), pw.astype(BF16))
    tinv_b = tinv.astype(BF16)

    def put(o, val):
        o[0] = val.reshape(nh, ts, val.shape[-1]).astype(o.dtype)

    put(ta_o, _bdot(tinv_b, at))
    put(tr_o, _bdot(tinv_b, _bdot(a_ak.astype(BF16), v).astype(BF16)))
    put(arb_o, a_rb)
    put(yv_o, _bdot(a_rk.astype(BF16), v))


def rwkv_intra(at, bt, kt, rt, v, ts):
    b, h, s, dh = at.shape
    seq = lambda: pl.BlockSpec((1, h, ts, dh), lambda bi, i: (bi, 0, i, 0))
    shp = lambda dt: jax.ShapeDtypeStruct((b, h, s, dh), dt)
    return pl.pallas_call(
        _rwkv_intra_kernel,
        grid=(b, s // ts),
        in_specs=[seq()] * 5,
        out_specs=[seq()] * 4,
        out_shape=[shp(BF16), shp(F32), shp(BF16), shp(F32)],
        compiler_params=_cparams("parallel", "parallel"),
        name="rwkv_intra",
    )(at, bt, kt, rt, v)


def _rwkv_scan_kernel(ta_ref, tr_ref, arb_ref, yv_ref, rt_ref, v_ref, bw_ref, kw_ref, wc_ref, y_ref, st_ref):
    c = RWKV_CHUNK
    nb, nh, ts, dh = ta_ref.shape
    n = nb * nh

    @pl.when(pl.program_id(0) == 0)
    def _():
        st_ref[...] = jnp.zeros_like(st_ref)

    def chunk_step(j, _):
        sl = (slice(None), slice(None), pl.ds(pl.multiple_of(j * c, c), c), slice(None))
        get = lambda ref: ref[sl].reshape(n, c, dh)
        st = st_ref[...]
        st_b = st.astype(BF16)
        u = _bdot_nt(get(ta_ref), st_b) + get(tr_ref)
        u_b = u.astype(BF16)
        y = _bdot_nt(get(rt_ref), st_b) + _bdot(get(arb_ref), u_b) + get(yv_ref)
        wc = wc_ref[:, :, pl.ds(pl.program_id(0) * (ts // c) + j, 1), :].reshape(n, 1, dh)
        st_ref[...] = st * wc + _bdot_tn(jnp.concatenate([u_b, get(v_ref)], axis=1),
                                         jnp.concatenate([get(bw_ref), get(kw_ref)], axis=1))
        y_ref[sl] = y.reshape(nb, nh, c, dh)
        return 0

    lax.fori_loop(0, ts // c, chunk_step, 0)


def rwkv_scan(ta, tr, arb, yv, rt, v, bw, kw, wc, ts):
    b, h, s, dh = ta.shape
    seq = lambda n: pl.BlockSpec((b, h, n, dh), lambda i: (0, 0, i, 0))
    return pl.pallas_call(
        _rwkv_scan_kernel,
        grid=(s // ts,),
        in_specs=[seq(ts)] * 8 + [pl.BlockSpec(wc.shape, lambda i: (0, 0, 0, 0))],
        out_specs=seq(ts),
        out_shape=jax.ShapeDtypeStruct((b, h, s, dh), F32),
        scratch_shapes=[pltpu.VMEM((b * h, dh, dh), F32)],
        compiler_params=_cparams("arbitrary"),
        name="rwkv_scan",
    )(ta, tr, arb, yv, rt, v, bw, kw, wc)


def _out_proj_kernel(x_ref, on_ref, y_ref, bonus_ref, g_ref, lnw_ref, lnb_ref, bd_ref, wn_ref, wr_ref, o_ref):
    y = jnp.concatenate([y_ref[0, h] for h in range(RWKV_HEADS)], axis=-1)
    bd = bd_ref[...]
    yc = y - _dot_x_sel(y, bd)
    yn = yc * lax.rsqrt(_dot_x_sel(yc * yc, bd) + GN_EPS)
    o_rwkv = (yn * lnw_ref[...] + lnb_ref[...] + bonus_ref[...]) * g_ref[...]
    o_ref[...] = (x_ref[...] + _dot(on_ref[...].astype(BF16), wn_ref[...])
                  + _dot(o_rwkv.astype(BF16), wr_ref[...]))


def out_proj(x, o_nsa, y, bonus, g, ln_w, ln_b, w_out, s, tm):
    t, d = x.shape
    dn = o_nsa.shape[1]
    dr = bonus.shape[1]
    tps = s // tm
    row = lambda w: pl.BlockSpec((tm, w), lambda i: (i, 0))
    full = lambda *sh: pl.BlockSpec(sh, lambda i: (0,) * len(sh))
    return pl.pallas_call(
        _out_proj_kernel,
        grid=(t // tm,),
        in_specs=[row(d), row(dn),
                  pl.BlockSpec((1, RWKV_HEADS, tm, HEAD_DIM), lambda i: (i // tps, 0, i % tps, 0)),
                  row(dr), row(dr), full(1, dr), full(1, dr), full(dr, dr), full(dn, d), full(dr, d)],
        out_specs=row(d),
        out_shape=jax.ShapeDtypeStruct((t, d), F32),
        compiler_params=_cparams("parallel"),
        name="out_proj",
    )(x, o_nsa, y, bonus, g, ln_w.reshape(1, dr), ln_b.reshape(1, dr),
      _block_diag_ones(dr, HEAD_DIM, 1.0 / HEAD_DIM), w_out[:dn].astype(BF16), w_out[dn:].astype(BF16))


def _cross_attn_kernel(h_ref, g_ref, wq_ref, qg_ref, kv_ref, kg_ref, wo_ref, o_ref):
    h = h_ref[...]
    d = h.shape[1]
    xd = d // X_HEADS
    q = _dot(_rms(h, g_ref[...]).astype(BF16), wq_ref[...])
    kv = kv_ref[0]
    outs = []
    for hd in range(X_HEADS):
        qh = _rms(q[:, hd * xd:(hd + 1) * xd], qg_ref[...]) * (xd ** -0.5)
        kh = _rms(kv[:, hd * xd:(hd + 1) * xd], kg_ref[...])
        vh = kv[:, d + hd * xd:d + (hd + 1) * xd]
        s = _dot_nt(qh.astype(BF16), kh.astype(BF16))
        p = jnp.exp(s - jnp.max(s, axis=-1, keepdims=True))
        p = p / jnp.sum(p, axis=-1, keepdims=True)
        outs.append(_dot(p.astype(BF16), vh.astype(BF16)))
    o = jnp.concatenate(outs, axis=-1)
    o_ref[...] = h + _dot(o.astype(BF16), wo_ref[...])


def cross_attention(h, kv, norm_g, xq_w, xq_g, xk_g, xo_w, b, s, tm):
    t, d = h.shape
    m = kv.shape[1]
    xd = d // X_HEADS
    tiles = s // tm
    full = lambda *sh: pl.BlockSpec(sh, lambda i: (0,) * len(sh))
    return pl.pallas_call(
        _cross_attn_kernel,
        grid=(t // tm,),
        in_specs=[pl.BlockSpec((tm, d), lambda i: (i, 0)), full(1, d), full(d, d), full(1, xd),
                  pl.BlockSpec((1, m, 2 * d), lambda i: (i // tiles, 0, 0)), full(1, xd), full(d, d)],
        out_specs=pl.BlockSpec((tm, d), lambda i: (i, 0)),
        out_shape=jax.ShapeDtypeStruct((t, d), F32),
        compiler_params=_cparams("parallel"),
        name="cross_attention",
    )(h, norm_g.reshape(1, d), xq_w.astype(BF16), xq_g.reshape(1, xd), kv, xk_g.reshape(1, xd),
      xo_w.astype(BF16))


def _router_kernel(h_ref, g_ref, rw_ref, rb_ref, ltri_ref, xn_o, idx_o, gate_o, rank_o, count_o, seen_ref):
    @pl.when(pl.program_id(0) == 0)
    def _():
        seen_ref[...] = jnp.zeros_like(seen_ref)

    xn = _rms(h_ref[...], g_ref[...])
    xn_o[...] = xn
    logits = _dot(xn, rw_ref[...], precision=HIGHEST) + rb_ref[...]
    tm = logits.shape[0]
    lane = lax.broadcasted_iota(jnp.int32, (tm, LANES), 1)
    lanef = lane.astype(F32)
    logits = jnp.where(lane < N_EXPERTS, logits, REMOVED)
    idx_acc = jnp.zeros((tm, LANES), F32)
    val_acc = jnp.zeros((tm, LANES), F32)
    chosen = jnp.zeros((tm, LANES), F32)
    picks = []
    top = None
    for k in range(TOP_K):
        m = jnp.max(logits, axis=-1, keepdims=True)
        idx = jnp.min(jnp.where(logits == m, lanef, float(LANES)), axis=-1, keepdims=True)
        pick = lanef == idx
        picks.append(pick)
        chosen = jnp.where(pick, 1.0, chosen)
        logits = jnp.where(pick, REMOVED, logits)
        top = m if top is None else top
        idx_acc = jnp.where(lane == k, idx, idx_acc)
        val_acc = jnp.where(lane == k, jnp.exp(m - top), val_acc)
    idx_o[...] = idx_acc.astype(jnp.int32)
    gate_o[...] = val_acc / jnp.sum(val_acc, axis=-1, keepdims=True)
    before = seen_ref[0:1, :] + _dot(ltri_ref[...], chosen.astype(BF16))
    rank_acc = jnp.zeros((tm, LANES), F32)
    for k, pick in enumerate(picks):
        rank_acc = jnp.where(lane == k, jnp.sum(jnp.where(pick, before, 0.0), axis=-1, keepdims=True), rank_acc)
    rank_o[...] = rank_acc.astype(jnp.int32)
    seen_ref[...] = seen_ref[...] + jnp.sum(chosen, axis=0, keepdims=True)
    count_o[...] = seen_ref[...]


def moe_router(h, norm_g, router_w, router_b, tm):
    t, d = h.shape
    rw = jnp.zeros((d, LANES), F32).at[:, :N_EXPERTS].set(router_w)
    rb = jnp.zeros((1, LANES), F32).at[0, :N_EXPERTS].set(router_b)
    i = np.arange(tm)
    ltri = jnp.asarray(i[:, None] > i[None, :], BF16)
    row = lambda w: pl.BlockSpec((tm, w), lambda i: (i, 0))
    full = lambda *s: pl.BlockSpec(s, lambda i: (0,) * len(s))
    return pl.pallas_call(
        _router_kernel,
        grid=(t // tm,),
        in_specs=[row(d), full(1, d), full(d, LANES), full(1, LANES), full(tm, tm)],
        out_specs=[row(d), row(LANES), row(LANES), row(LANES), full(8, LANES)],
        out_shape=[jax.ShapeDtypeStruct((t, d), F32), jax.ShapeDtypeStruct((t, LANES), jnp.int32),
                   jax.ShapeDtypeStruct((t, LANES), F32), jax.ShapeDtypeStruct((t, LANES), jnp.int32),
                   jax.ShapeDtypeStruct((8, LANES), F32)],
        scratch_shapes=[pltpu.VMEM((8, LANES), F32)],
        compiler_params=_cparams("arbitrary"),
        name="moe_router",
    )(h, norm_g.reshape(1, d), rw, rb, ltri)


def _expert_kernel(blk_e_ref, n_used_ref, x_ref, w1_ref, b1_ref, w2_ref, b2_ref, o_ref):
    i = pl.program_id(0)

    @pl.when(i < n_used_ref[0])
    def _():
        x = x_ref[...].astype(BF16)
        hg = jnp.minimum(_dot(x, w1_ref[0, 0]) + b1_ref[0, 0], SWIGLU_LIMIT)
        hl = jnp.clip(_dot(x, w1_ref[0, 1]) + b1_ref[0, 1], -SWIGLU_LIMIT, SWIGLU_LIMIT)
        act = hg * jax.nn.sigmoid(SWIGLU_ALPHA * hg) * (hl + 1.0)
        o_ref[...] = _dot(act.astype(BF16), w2_ref[0].astype(BF16)) + b2_ref[0]

    @pl.when(i >= n_used_ref[0])
    def _():
        o_ref[...] = jnp.zeros_like(o_ref)


def moe_experts(xs, blk_e, n_used, w1, b1, w2, b2):
    r, d = xs.shape
    f = w1.shape[3]
    m = MOE_ROW_BLOCK
    ex = lambda *s: pl.BlockSpec((1,) + s, lambda i, be, nu: (be[i],) + (0,) * len(s))
    grid_spec = pltpu.PrefetchScalarGridSpec(
        num_scalar_prefetch=2,
        grid=(r // m,),
        in_specs=[pl.BlockSpec((m, d), lambda i, be, nu: (i, 0)),
                  ex(2, d, f), ex(2, 1, f), ex(f, d), ex(1, d)],
        out_specs=pl.BlockSpec((m, d), lambda i, be, nu: (i, 0)),
    )
    return pl.pallas_call(
        _expert_kernel,
        grid_spec=grid_spec,
        out_shape=jax.ShapeDtypeStruct((r, d), F32),
        compiler_params=_cparams("arbitrary"),
        name="moe_experts",
    )(blk_e, n_used, xs, w1, b1, w2, b2)


def _combine_kernel(h_ref, y_ref, gate_ref, o_ref):
    d = h_ref.shape[1]
    acc = h_ref[...]
    for k in range(TOP_K):
        acc = acc + gate_ref[:, k:k + 1] * y_ref[:, k * d:(k + 1) * d]
    o_ref[...] = acc


def moe_combine(h, y4, gate, tm):
    t, d = h.shape
    row = lambda w: pl.BlockSpec((tm, w), lambda i: (i, 0))
    return pl.pallas_call(
        _combine_kernel,
        grid=(t // tm,),
        in_specs=[row(d), row(TOP_K * d), row(LANES)],
        out_specs=row(d),
        out_shape=jax.ShapeDtypeStruct((t, d), F32),
        compiler_params=_cparams("parallel"),
        name="moe_combine",
    )(h, y4, gate)


def _layer(x, mem, norm_mix_g, w_in, q_norm_g, k_cmp_norm_g, k_slc_norm_g, k_win_norm_g,
           cmp_pe_k, cmp_pe_v, cmp_k_w1, cmp_k_w2, cmp_v_w1, cmp_v_w2,
           rwkv_mu, rwkv_w0, rwkv_w_up, rwkv_a0, rwkv_a_up, rwkv_g_up, rwkv_k_k, rwkv_k_a,
           rwkv_r_k, rwkv_ln_w, rwkv_ln_b, w_out,
           norm_x_g, norm_mem_g, xq_w, xk_w, xv_w, xq_norm_g, xk_norm_g, xo_w,
           norm_ffn_g, router_w, router_b, mlp1_w, mlp1_b, mlp2_w, mlp2_b):
    b, s, d = x.shape
    t = b * s
    tm = 512
    xt = x.reshape(t, d)

    w_nsa = jnp.pad(w_in[:, :NSA_PROJ], ((0, 0), (0, NSA_PROJ_PAD - NSA_PROJ)))
    p_nsa = norm_matmul(xt, norm_mix_g, w_nsa, tm)
    p_rwkv = norm_matmul(xt, norm_mix_g, w_in[:, NSA_PROJ:], tm)

    nq = NSA_HEADS * HEAD_DIM
    gw = NSA_GROUPS * HEAD_DIM
    qn, ks, vs, kw, vw, gates = nsa_prep(p_nsa, q_norm_g, k_slc_norm_g, k_win_norm_g, tm)
    kcv = nsa_compress(p_nsa[:, nq:nq + gw], p_nsa[:, nq + gw:nq + 2 * gw], cmp_pe_k, cmp_pe_v,
                       cmp_k_w1, cmp_k_w2, cmp_v_w1, cmp_v_w2, k_cmp_norm_g, b, s)
    o_nsa = nsa_attention(qn, kcv, ks, vs, kw, vw, gates, b, s).transpose(0, 3, 1, 2).reshape(t, nq)

    at, bt, kt, rt, v, bw, kwd, wc, g_gate, bonus = rwkv_prep(
        p_rwkv, rwkv_mu, rwkv_w0, rwkv_w_up, rwkv_a0, rwkv_a_up, rwkv_g_up, rwkv_k_k, rwkv_k_a, rwkv_r_k, b, s, tm)
    ta, tr, arb, yv = rwkv_intra(at, bt, kt, rt, v, ts=256)
    y = rwkv_scan(ta, tr, arb, yv, rt, v, bw, kwd, wc, ts=256)

    h1 = out_proj(xt, o_nsa, y, bonus, g_gate, rwkv_ln_w, rwkv_ln_b, w_out, s, tm)
    m = mem.shape[1]
    kv = norm_matmul(mem.reshape(b * m, d), norm_mem_g, jnp.concatenate([xk_w, xv_w], axis=1), m)
    h2 = cross_attention(h1, kv.reshape(b, m, 2 * d), norm_x_g, xq_w, xq_norm_g, xk_norm_g, xo_w, b, s, tm)

    xn, top_i, gate, rank, seen = moe_router(h2, norm_ffn_g, router_w, router_b, tm)
    top_i = top_i[:, :TOP_K]
    a = t * TOP_K
    mb = MOE_ROW_BLOCK
    order = jnp.argsort(top_i.reshape(a), stable=True)
    counts = seen[0, :N_EXPERTS].astype(jnp.int32)
    starts = jnp.cumsum(counts) - counts
    padded = (counts + mb - 1) // mb * mb
    pends = jnp.cumsum(padded)
    pstarts = pends - padded
    pos = (pstarts[top_i] + rank[:, :TOP_K]).reshape(a)
    n_blocks = -(-a // mb) + N_EXPERTS
    r = n_blocks * mb
    blk_e = jnp.minimum(jnp.searchsorted(pends, jnp.arange(n_blocks) * mb, side='right'),
                        N_EXPERTS - 1).astype(jnp.int32)
    row_e = jnp.repeat(blk_e, mb)
    src_i = jnp.arange(r) - (pstarts - starts)[row_e]
    row_src = jnp.where(src_i < (starts + counts)[row_e], order[jnp.minimum(src_i, a - 1)] // TOP_K, 0)
    row_src = row_src.astype(jnp.int32)
    n_used = (pends[-1] // mb).astype(jnp.int32).reshape(1)
    xs = xn.at[row_src].get(mode="promise_in_bounds")
    f = mlp1_w.shape[2] // 2
    w1 = mlp1_w.reshape(N_EXPERTS, d, f, 2).transpose(0, 3, 1, 2).astype(BF16)
    b1 = mlp1_b.reshape(N_EXPERTS, 1, f, 2).transpose(0, 3, 1, 2)
    ys = moe_experts(xs, blk_e, n_used, w1, b1, mlp2_w, mlp2_b.reshape(N_EXPERTS, 1, d))
    y4 = ys.at[pos].get(mode="promise_in_bounds").reshape(t, TOP_K * d)
    out = moe_combine(h2, y4, gate, 256)
    return out.reshape(b, s, d)


def kernel(x, mem, norm_mix_g, w_in, q_norm_g, k_cmp_norm_g, k_slc_norm_g, k_win_norm_g, cmp_pe_k, cmp_pe_v, cmp_k_w1, cmp_k_w2, cmp_v_w1, cmp_v_w2, rwkv_mu, rwkv_w0, rwkv_w_up, rwkv_a0, rwkv_a_up, rwkv_g_up, rwkv_k_k, rwkv_k_a, rwkv_r_k, rwkv_ln_w, rwkv_ln_b, w_out, norm_x_g, norm_mem_g, xq_w, xk_w, xv_w, xq_norm_g, xk_norm_g, xo_w, norm_ffn_g, router_w, router_b, mlp1_w, mlp1_b, mlp2_w, mlp2_b):
    params = (norm_mix_g, w_in, q_norm_g, k_cmp_norm_g, k_slc_norm_g, k_win_norm_g, cmp_pe_k, cmp_pe_v,
              cmp_k_w1, cmp_k_w2, cmp_v_w1, cmp_v_w2, rwkv_mu, rwkv_w0, rwkv_w_up, rwkv_a0, rwkv_a_up,
              rwkv_g_up, rwkv_k_k, rwkv_k_a, rwkv_r_k, rwkv_ln_w, rwkv_ln_b, w_out, norm_x_g, norm_mem_g,
              xq_w, xk_w, xv_w, xq_norm_g, xk_norm_g, xo_w, norm_ffn_g, router_w, router_b,
              mlp1_w, mlp1_b, mlp2_w, mlp2_b)
    h = x
    for layer in range(norm_mix_g.shape[0]):
        h = _layer(h, mem, *[prm[layer] for prm in params])
    return h
```

```python
import functools

import numpy as np
import jax
import jax.numpy as jnp
from jax import lax
from jax.experimental import pallas as pl
from jax.experimental.pallas import tpu as pltpu

F32 = jnp.float32
BF16 = jnp.bfloat16
HIGHEST = lax.Precision.HIGHEST

V7X_VMEM_BYTES = 64 * 1024 * 1024
VMEM_LIMIT = V7X_VMEM_BYTES * 3 // 4

HEAD_DIM = 64
NSA_HEADS = 8
NSA_GROUPS = 2
NSA_HPG = NSA_HEADS // NSA_GROUPS
GROUP_W = NSA_HPG * HEAD_DIM
CMP_BLOCK = 32
CMP_STRIDE = 16
SLC_BLOCK = 64
SLC_TOPK = 16
WINDOW = 512
Q_BLOCK = 128
SEL_CHUNK = 512
RWKV_HEADS = 8
RWKV_DIM = RWKV_HEADS * HEAD_DIM
RWKV_CHUNK = 64
GN_EPS = HEAD_DIM * 1e-5
X_HEADS = 4
N_EXPERTS = 32
TOP_K = 4
SWIGLU_LIMIT = 7.0
SWIGLU_ALPHA = 1.702
MOE_ROW_BLOCK = 256
RMS_EPS = 1e-6
NEG_INF = -1e30
BIG = 1e9
REMOVED = -3e38
LANES = 128

NSA_PROJ = NSA_HEADS * HEAD_DIM + 6 * NSA_GROUPS * HEAD_DIM + NSA_HEADS * 3
NSA_PROJ_PAD = -(-NSA_PROJ // LANES) * LANES
RWKV_PROJ = 3 * RWKV_DIM + 64 + 64 + 128


def _cparams(*sem):
    return pltpu.CompilerParams(dimension_semantics=sem, vmem_limit_bytes=VMEM_LIMIT)


def _dot(a, b, **kw):
    return jnp.dot(a, b, preferred_element_type=F32, **kw)


def _dot_nt(a, b, **kw):
    return lax.dot_general(a, b, (((1,), (1,)), ((), ())), preferred_element_type=F32, **kw)


def _dot_tn(a, b, **kw):
    return lax.dot_general(a, b, (((0,), (0,)), ((), ())), preferred_element_type=F32, **kw)


def _rms(x, g):
    return x * lax.rsqrt(jnp.mean(x * x, axis=-1, keepdims=True) + RMS_EPS) * g


def _split_bf16(x, terms=3):
    parts = []
    for _ in range(terms):
        hi = x.astype(BF16)
        parts.append(hi)
        x = x - hi.astype(F32)
    return parts


def _dot_x_sel(x, sel):
    return sum(_dot(p, sel) for p in _split_bf16(x))


def _dot_sel_x(sel, x):
    return sum(_dot(sel, p) for p in _split_bf16(x))


def _block_diag_ones(n, blk, scale=1.0):
    i = np.arange(n)
    return jnp.asarray(((i[:, None] // blk) == (i[None, :] // blk)).astype(np.float32) * scale, BF16)


def _norm_matmul_kernel(x_ref, g_ref, w_ref, o_ref):
    xn = _rms(x_ref[...], g_ref[...]).astype(BF16)
    o_ref[...] = _dot(xn, w_ref[...])


def norm_matmul(x, g, w, tm):
    m, d = x.shape
    n = w.shape[1]
    return pl.pallas_call(
        _norm_matmul_kernel,
        grid=(m // tm,),
        in_specs=[pl.BlockSpec((tm, d), lambda i: (i, 0)),
                  pl.BlockSpec((1, d), lambda i: (0, 0)),
                  pl.BlockSpec((d, n), lambda i: (0, 0))],
        out_specs=pl.BlockSpec((tm, n), lambda i: (i, 0)),
        out_shape=jax.ShapeDtypeStruct((m, n), F32),
        compiler_params=_cparams("parallel"),
        name="norm_matmul",
    )(x, g.reshape(1, d), w.astype(BF16))


def _nsa_prep_kernel(p_ref, qg_ref, ksg_ref, kwg_ref, bdq_ref, bdk_ref, place_ref, feat_ref,
                     qt_o, ks_o, vst_o, kw_o, vwt_o, gate_o):
    p = p_ref[...]
    tm = p.shape[0]
    nq = NSA_HEADS * HEAD_DIM
    gw = NSA_GROUPS * HEAD_DIM
    q = p[:, :nq]
    msq = _dot_x_sel(q * q, bdq_ref[...])
    qn = q * lax.rsqrt(msq + RMS_EPS) * qg_ref[...] * (HEAD_DIM ** -0.5)
    qt_o[0] = qn.T.reshape(NSA_HEADS, HEAD_DIM, tm).astype(BF16)

    def seg(k):
        return p[:, nq + k * gw: nq + (k + 1) * gw]

    def head_norm(t, g):
        ms = _dot_x_sel(t * t, bdk_ref[...])
        return t * lax.rsqrt(ms + RMS_EPS) * g

    feat = feat_ref[...].astype(F32)
    ks = head_norm(seg(2), ksg_ref[...]).astype(BF16)
    kw = head_norm(seg(4), kwg_ref[...]).astype(BF16)
    vs_t = seg(3).T
    vw_t = seg(5).T
    for g in range(NSA_GROUPS):
        ks_o[0, g] = (_dot(ks, place_ref[g]) + feat).astype(BF16)
        kw_o[0, g] = (_dot(kw, place_ref[g])[:, :2 * HEAD_DIM] + feat[:, :2 * HEAD_DIM]).astype(BF16)
        for j in range(tm // Q_BLOCK):
            tile = (slice(g * HEAD_DIM, (g + 1) * HEAD_DIM), slice(j * Q_BLOCK, (j + 1) * Q_BLOCK))
            vst_o[0, g, j] = vs_t[tile].astype(BF16)
            vwt_o[0, g, j] = vw_t[tile].astype(BF16)
    gate_o[...] = jax.nn.sigmoid(p[:, nq + 6 * gw: nq + 6 * gw + LANES])


def nsa_prep(p_nsa, q_g, ks_g, kw_g, b, s, tm):
    t = p_nsa.shape[0]
    nq = NSA_HEADS * HEAD_DIM
    gw = NSA_GROUPS * HEAD_DIM
    grp, dh = NSA_GROUPS, HEAD_DIM
    tps = s // tm
    kx = 2 * dh + LANES
    assert s // SLC_BLOCK <= LANES
    tok = np.arange(s)
    feat = np.zeros((s, kx), np.float32)
    feat[:, dh] = tok // SLC_BLOCK * SLC_BLOCK
    feat[:, dh + 1] = tok % SLC_BLOCK
    feat[:, dh + 2:dh + 4] = 1.0
    feat[tok, 2 * dh + tok // SLC_BLOCK] = 1.0
    place = np.zeros((grp, gw, kx), np.float32)
    for g in range(grp):
        place[g, g * dh + np.arange(dh), np.arange(dh)] = 1.0
    tile = lambda v, n: jnp.tile(v.reshape(1, HEAD_DIM), (1, n))
    row = lambda w: pl.BlockSpec((tm, w), lambda i: (i, 0))
    full = lambda *sh: pl.BlockSpec(sh, lambda i: (0,) * len(sh))
    seq = lambda *sh: pl.BlockSpec((1, grp) + sh, lambda i: (i // tps, 0, i % tps) + (0,) * (len(sh) - 1))
    return pl.pallas_call(
        _nsa_prep_kernel,
        grid=(t // tm,),
        in_specs=[row(NSA_PROJ_PAD), full(1, nq), full(1, gw), full(1, gw), full(nq, nq), full(gw, gw),
                  full(grp, gw, kx), pl.BlockSpec((tm, kx), lambda i: (i % tps, 0))],
        out_specs=[pl.BlockSpec((1, NSA_HEADS, dh, tm), lambda i: (i // tps, 0, 0, i % tps)),
                   seq(tm, kx), seq(tm // Q_BLOCK, dh, Q_BLOCK), seq(tm, 2 * dh), seq(tm // Q_BLOCK, dh, Q_BLOCK),
                   row(LANES)],
        out_shape=[jax.ShapeDtypeStruct((b, NSA_HEADS, dh, s), BF16),
                   jax.ShapeDtypeStruct((b, grp, s, kx), BF16),
                   jax.ShapeDtypeStruct((b, grp, s // Q_BLOCK, dh, Q_BLOCK), BF16),
                   jax.ShapeDtypeStruct((b, grp, s, 2 * dh), BF16),
                   jax.ShapeDtypeStruct((b, grp, s // Q_BLOCK, dh, Q_BLOCK), BF16),
                   jax.ShapeDtypeStruct((t, LANES), F32)],
        compiler_params=_cparams("parallel"),
        name="nsa_prep",
    )(p_nsa, tile(q_g, NSA_HEADS), tile(ks_g, NSA_GROUPS), tile(kw_g, NSA_GROUPS),
      _block_diag_ones(nq, HEAD_DIM, 1.0 / HEAD_DIM), _block_diag_ones(gw, HEAD_DIM, 1.0 / HEAD_DIM),
      jnp.asarray(place, BF16), jnp.asarray(feat, BF16))


def _compress_kernel(ch_ref, pe_ref, w1_ref, w2_ref, g_ref, o_ref):
    ch = ch_ref[0, 0, 0]
    nc = ch.shape[0]
    half = CMP_STRIDE * HEAD_DIM
    nxt = pltpu.roll(ch, nc - 1, 0)
    w1 = w1_ref[0]
    h1 = (_dot(ch, w1[:half], precision=HIGHEST) + _dot(nxt, w1[half:], precision=HIGHEST)
          + _dot(pe_ref[0], w1, precision=HIGHEST))
    out = _dot(jax.nn.silu(h1), w2_ref[0], precision=HIGHEST)
    out = jnp.where(pl.program_id(0) == 0, _rms(out, g_ref[...]), out)
    o_ref[0, 0, 0] = out.astype(BF16)


def nsa_compress(kc, vc, pe_k, pe_v, kw1, kw2, vw1, vw2, kc_g, b, s):
    nc = s // CMP_STRIDE
    half = CMP_STRIDE * HEAD_DIM

    def chunks(t):
        return t.reshape(b, nc, CMP_STRIDE, NSA_GROUPS, HEAD_DIM).transpose(0, 3, 1, 2, 4).reshape(
            b, NSA_GROUPS, nc, half)

    ch = jnp.stack([chunks(kc), chunks(vc)])
    pe = jnp.stack([pe_k.reshape(1, 2 * half), pe_v.reshape(1, 2 * half)])
    return pl.pallas_call(
        _compress_kernel,
        grid=(2, b, NSA_GROUPS),
        in_specs=[pl.BlockSpec((1, 1, 1, nc, half), lambda kv, bi, g: (kv, bi, g, 0, 0)),
                  pl.BlockSpec((1, 1, 2 * half), lambda kv, bi, g: (kv, 0, 0)),
                  pl.BlockSpec((1, 2 * half, HEAD_DIM), lambda kv, bi, g: (kv, 0, 0)),
                  pl.BlockSpec((1, HEAD_DIM, HEAD_DIM), lambda kv, bi, g: (kv, 0, 0)),
                  pl.BlockSpec((1, HEAD_DIM), lambda kv, bi, g: (0, 0))],
        out_specs=pl.BlockSpec((1, 1, 1, nc, HEAD_DIM), lambda kv, bi, g: (kv, bi, g, 0, 0)),
        out_shape=jax.ShapeDtypeStruct((2, b, NSA_GROUPS, nc, HEAD_DIM), BF16),
        compiler_params=_cparams("parallel", "parallel", "parallel"),
        name="nsa_compress",
    )(ch, pe, jnp.stack([kw1, vw1]), jnp.stack([kw2, vw2]), kc_g.reshape(1, HEAD_DIM))


def _masked_exp_cols(s, mask):
    sm = jnp.where(mask, s, NEG_INF)
    m = jnp.max(sm, axis=0, keepdims=True)
    p = jnp.exp(sm - jnp.where(m > 0.5 * NEG_INF, m, 0.0))
    l = jnp.sum(p, axis=0, keepdims=True)
    return p, 1.0 / jnp.where(l > 0.0, l, 1.0)


def _nsa_attn_kernel(qt_ref, kc_ref, vct_ref, ovt_ref, ks_ref, vst_ref, kw_ref, vwt_ref, gate_ref, o_ref,
                     *, n_sel, n_top):
    g = pl.program_id(1)
    t0 = pl.program_id(2) * Q_BLOCK
    ks_ref, vst_ref, kw_ref, vwt_ref = (r.at[0, 0] for r in (ks_ref, vst_ref, kw_ref, vwt_ref))
    cols = NSA_HPG * Q_BLOCK
    col = lax.broadcasted_iota(jnp.int32, (1, cols), 1)
    tq = t0 + col % Q_BLOCK
    head = g * NSA_HPG + col // Q_BLOCK
    slope = lax.bitcast_convert_type((127 - (head + 1)) << 23, F32)
    qt = jnp.concatenate([qt_ref[0, h] for h in range(NSA_HPG)], axis=1)
    frow = lax.broadcasted_iota(jnp.int32, (HEAD_DIM, cols), 0)
    tq_hi = (tq // SLC_BLOCK * SLC_BLOCK).astype(F32)
    tq_lo = (tq % SLC_BLOCK).astype(F32)
    qpos = jnp.where(frow < 2, slope, jnp.where(frow == 2, -slope * tq_hi, jnp.where(frow == 3, -slope * tq_lo, 0.0)))
    qc = jnp.concatenate([qt, qpos.astype(BF16)], axis=0)

    def heads_sum(x):
        acc = x[:, :Q_BLOCK]
        for h in range(1, NSA_HPG):
            acc = acc + x[:, h * Q_BLOCK:(h + 1) * Q_BLOCK]
        return acc

    kc = kc_ref[0, 0]
    nc = kc.shape[0]
    c_last = lax.broadcasted_iota(jnp.int32, (nc, 1), 0) * CMP_STRIDE + (CMP_BLOCK - 1)
    p_c, inv_c = _masked_exp_cols(_dot(kc, qc), c_last <= tq)
    p_c = p_c * inv_c
    o_c = _dot(vct_ref[0, 0], p_c.astype(BF16))
    imp = _dot_sel_x(ovt_ref[...], heads_sum(p_c))

    bid = lax.broadcasted_iota(jnp.int32, (LANES, Q_BLOCK), 0)
    bidf = bid.astype(F32)
    tq1 = t0 + lax.broadcasted_iota(jnp.int32, (1, Q_BLOCK), 1)
    cur = tq1 // SLC_BLOCK
    forced = (bid == 0) | (bid == cur) | (bid == cur - 1)
    score = jnp.where(forced, BIG, jnp.where(bid * SLC_BLOCK <= tq1, imp, -BIG))
    score = jnp.where(bid < n_sel, score, REMOVED)
    picked = jnp.zeros((LANES, Q_BLOCK), jnp.bool_)
    for _ in range(n_top):
        m = jnp.max(score, axis=0, keepdims=True)
        pick = bidf == jnp.min(jnp.where(score == m, bidf, float(LANES)), axis=0, keepdims=True)
        picked = picked | pick
        score = jnp.where(pick, REMOVED, score)
    sel_neg = jnp.where(picked, 0.0, NEG_INF).astype(BF16)
    qx = jnp.concatenate([qc, jnp.concatenate([sel_neg] * NSA_HPG, axis=1)], axis=0)

    def values_t(ref, first_tile, n_tiles):
        return jnp.concatenate([ref[first_tile + i] for i in range(n_tiles)], axis=1)

    span = WINDOW + Q_BLOCK
    ws = pl.multiple_of(jnp.maximum(t0 - WINDOW, 0), Q_BLOCK)
    dw = tq - (ws + lax.broadcasted_iota(jnp.int32, (span, 1), 0))
    in_window = dw.astype(jnp.uint32) < WINDOW
    p_w, inv_w = _masked_exp_cols(_dot(kw_ref[pl.ds(ws, span), :], qc), in_window)
    o_w = _dot(values_t(vwt_ref, ws // Q_BLOCK, span // Q_BLOCK), p_w.astype(BF16)) * inv_w

    krow = lax.broadcasted_iota(jnp.int32, (SEL_CHUNK, 1), 0)

    def flash(s, start, carry):
        m, l, acc = carry
        m_new = jnp.maximum(m, jnp.max(s, axis=0, keepdims=True))
        alpha = jnp.exp(m - m_new)
        p = jnp.exp(s - m_new)
        l = alpha * l + jnp.sum(p, axis=0, keepdims=True)
        v = values_t(vst_ref, start // Q_BLOCK, SEL_CHUNK // Q_BLOCK)
        return m_new, l, alpha * acc + _dot(v, p.astype(BF16))

    bpc = SEL_CHUNK // SLC_BLOCK
    blk_any = jnp.max(jnp.where(picked, 1.0, 0.0), axis=1, keepdims=True)
    chunk_bit = lax.bitcast_convert_type((bid[:, :1] // bpc + 127) << 23, F32)
    bits = jnp.max((blk_any * chunk_bit).reshape(LANES // bpc, bpc, 1), axis=1)
    active = jnp.sum(bits, axis=0, keepdims=True)[0, 0].astype(jnp.int32)

    def full_step(j, carry):
        start = pl.multiple_of(j * SEL_CHUNK, SEL_CHUNK)
        return lax.cond((active >> j) & 1 == 1,
                        lambda c: flash(_dot(ks_ref[pl.ds(start, SEL_CHUNK), :], qx), start, c),
                        lambda c: c, carry)

    n_full = t0 // SEL_CHUNK
    init = (jnp.full((1, cols), NEG_INF, F32), jnp.zeros((1, cols), F32), jnp.zeros((HEAD_DIM, cols), F32))
    carry = lax.fori_loop(0, n_full, full_step, init)
    start = pl.multiple_of(n_full * SEL_CHUNK, SEL_CHUNK)
    s_diag = jnp.where(start + krow <= tq, _dot(ks_ref[pl.ds(start, SEL_CHUNK), :], qx), NEG_INF)
    _, l_s, acc_s = flash(s_diag, start, carry)
    o_s = acc_s / l_s

    for h in range(NSA_HPG):
        hs = slice(h * Q_BLOCK, (h + 1) * Q_BLOCK)
        gate = lambda br: gate_ref[0, 0, br, h:h + 1, :]
        o_ref[0, h] = gate(0) * o_c[:, hs] + gate(1) * o_s[:, hs] + gate(2) * o_w[:, hs]


def nsa_attention(qt, kcv, ks_x, vs_t, kw_x, vw_t, gates, b, s):
    nq_blocks = s // Q_BLOCK
    nc = s // CMP_STRIDE
    n_cmp = nc - CMP_BLOCK // CMP_STRIDE + 1
    n_sel = s // SLC_BLOCK
    n_top = min(SLC_TOPK, n_sel)
    grp, dh = NSA_GROUPS, HEAD_DIM
    assert n_sel <= LANES and s % SEL_CHUNK == 0 and s >= WINDOW + Q_BLOCK
    c_start = np.arange(n_cmp) * CMP_STRIDE
    s_start = np.arange(n_sel) * SLC_BLOCK
    ovt = np.zeros((LANES, nc), np.float32)
    ovt[:n_sel, :n_cmp] = (np.clip(np.minimum((c_start + CMP_BLOCK)[:, None], s_start[None] + SLC_BLOCK)
                                   - np.maximum(c_start[:, None], s_start[None]), 0, None) / CMP_BLOCK).T

    c_pos = np.arange(nc) * CMP_STRIDE + (CMP_BLOCK - 1)
    c_feat = np.zeros((nc, dh), np.float32)
    c_feat[:, 0] = c_pos // SLC_BLOCK * SLC_BLOCK
    c_feat[:, 1] = c_pos % SLC_BLOCK
    c_feat[:, 2:4] = 1.0
    kc_x = jnp.concatenate([kcv[0], jnp.broadcast_to(jnp.asarray(c_feat, BF16), kcv[0].shape)], axis=-1)
    vct = kcv[1].transpose(0, 1, 3, 2)
    gates_t = gates[:, :NSA_HEADS * 3].reshape(b, s, grp, NSA_HPG, 3).transpose(0, 2, 4, 3, 1)

    grp_spec = lambda *shape: pl.BlockSpec((1, 1) + shape, lambda bi, g, i: (bi, g) + (0,) * len(shape),
                                           pipeline_mode=pl.Buffered(1))
    return pl.pallas_call(
        functools.partial(_nsa_attn_kernel, n_sel=n_sel, n_top=n_top),
        grid=(b, grp, nq_blocks),
        in_specs=[pl.BlockSpec((1, NSA_HPG, dh, Q_BLOCK), lambda bi, g, i: (bi, g, 0, i)),
                  grp_spec(nc, 2 * dh), grp_spec(dh, nc),
                  pl.BlockSpec((LANES, nc), lambda bi, g, i: (0, 0)),
                  grp_spec(s, 2 * dh + LANES), grp_spec(s // Q_BLOCK, dh, Q_BLOCK),
                  grp_spec(s, 2 * dh), grp_spec(s // Q_BLOCK, dh, Q_BLOCK),
                  pl.BlockSpec((1, 1, 3, NSA_HPG, Q_BLOCK), lambda bi, g, i: (bi, g, 0, 0, i))],
        out_specs=pl.BlockSpec((1, NSA_HPG, dh, Q_BLOCK), lambda bi, g, i: (bi, g, 0, i)),
        out_shape=jax.ShapeDtypeStruct((b, NSA_HEADS, dh, s), F32),
        compiler_params=_cparams("parallel", "parallel", "arbitrary"),
        name="nsa_attention",
    )(qt, kc_x, vct, jnp.asarray(ovt, BF16), ks_x, vs_t, kw_x, vw_t, gates_t)


def _rwkv_prep_kernel(p_ref, prev_ref, mu_ref, w0_ref, a0_ref, kk_ref, ka_ref, rk_ref,
                      wup_ref, aup_ref, gup_ref, bd_ref, ltri_ref, lones_ref, csum_ref,
                      at_o, bt_o, kt_o, rt_o, v_o, bw_o, kw_o, wc_o, g_o, bonus_o, *, tiles_per_seq):
    p = p_ref[...]
    tm = p.shape[0]
    first = pl.program_id(0) % tiles_per_seq == 0
    last_prev = jnp.where(first, 0.0, prev_ref[7:8, :])
    prev = pltpu.roll(p, 1, 0)
    prev = jnp.where(lax.broadcasted_iota(jnp.int32, (tm, 1), 0) == 0, last_prev, prev)
    pm = p + (prev - p) * mu_ref[...]
    d = RWKV_DIM
    r, k, v = pm[:, :d], pm[:, d:2 * d], pm[:, 2 * d:3 * d]
    lora = pm[:, 3 * d:3 * d + LANES]
    gd = pm[:, 3 * d + LANES:3 * d + 2 * LANES]
    z = -(w0_ref[...] + _dot(jnp.tanh(lora).astype(BF16), wup_ref[...]))
    softplus = jnp.maximum(z, 0.0) + jnp.log(1.0 + jnp.exp(-jnp.abs(z)))
    w = -softplus - 0.5
    a = jax.nn.sigmoid(a0_ref[...] + _dot(lora.astype(BF16), aup_ref[...]))
    g_o[...] = _dot(jax.nn.sigmoid(gd).astype(BF16), gup_ref[...])
    bd = bd_ref[...]
    kkr = k * kk_ref[...]
    kk = kkr / jnp.maximum(jnp.sqrt(_dot_x_sel(kkr * kkr, bd)), 1e-12)
    k2 = k * (1.0 + (a - 1.0) * ka_ref[...])
    bonus_o[...] = _dot_x_sel(r * k2 * rk_ref[...], bd) * v
    lw = -jnp.exp(w)
    lw_parts = _split_bf16(lw)
    cum = sum(_dot(ltri_ref[...], p) for p in lw_parts)
    tot = sum(_dot(lones_ref[...], p) for p in lw_parts)
    e_in = jnp.exp(cum)
    e_out = jnp.exp(-cum)
    e_end = jnp.exp(tot - cum)

    def put_heads(o, val):
        for h in range(RWKV_HEADS):
            o[0, h] = val[:, h * HEAD_DIM:(h + 1) * HEAD_DIM].astype(o.dtype)

    put_heads(at_o, -kk * jnp.exp(cum - lw))
    put_heads(bt_o, kk * a * e_out)
    put_heads(kt_o, k2 * e_out)
    put_heads(rt_o, r * e_in)
    put_heads(v_o, v)
    put_heads(bw_o, kk * a * e_end)
    put_heads(kw_o, k2 * e_end)
    put_heads(wc_o, jnp.exp(sum(_dot(csum_ref[...], p) for p in lw_parts)))


def rwkv_prep(p_rwkv, mu, w0, w_up, a0, a_up, g_up, k_k, k_a, r_k, b, s, tm):
    t = p_rwkv.shape[0]
    d = RWKV_DIM
    c = RWKV_CHUNK
    tps = s // tm
    cpt = tm // c
    wup = jnp.concatenate([w_up, jnp.zeros_like(a_up)], axis=0).astype(BF16)
    aup = jnp.concatenate([jnp.zeros_like(w_up), a_up], axis=0).astype(BF16)
    i = np.arange(tm)
    same = (i[:, None] // c) == (i[None, :] // c)
    ltri = jnp.asarray(same & (i[:, None] >= i[None, :]), BF16)
    lones = jnp.asarray(same, BF16)
    csum = jnp.asarray(np.arange(cpt)[:, None] == (i[None, :] // c), BF16)
    row = lambda w: pl.BlockSpec((tm, w), lambda i: (i, 0))
    full = lambda *sh: pl.BlockSpec(sh, lambda i: (0,) * len(sh))
    heads = lambda n: pl.BlockSpec((1, RWKV_HEADS, n, HEAD_DIM), lambda i: (i // tps, 0, i % tps, 0))
    hshape = lambda n, dt: jax.ShapeDtypeStruct((b, RWKV_HEADS, n, HEAD_DIM), dt)
    vec = lambda x: x.reshape(1, -1)
    return pl.pallas_call(
        functools.partial(_rwkv_prep_kernel, tiles_per_seq=tps),
        grid=(t // tm,),
        in_specs=[row(RWKV_PROJ),
                  pl.BlockSpec((8, RWKV_PROJ), lambda i: (jnp.maximum(i * (tm // 8) - 1, 0), 0)),
                  full(1, RWKV_PROJ), full(1, d), full(1, d), full(1, d), full(1, d), full(1, d),
                  full(LANES, d), full(LANES, d), full(LANES, d), full(d, d), full(tm, tm), full(tm, tm),
                  full(cpt, tm)],
        out_specs=[heads(tm)] * 7 + [heads(cpt), row(d), row(d)],
        out_shape=[hshape(s, BF16)] * 7 + [hshape(s // c, F32)] + [jax.ShapeDtypeStruct((t, d), F32)] * 2,
        compiler_params=_cparams("parallel"),
        name="rwkv_prep",
    )(p_rwkv, p_rwkv, vec(mu), vec(w0), vec(a0), vec(k_k), vec(k_a), vec(r_k), wup, aup, g_up.astype(BF16),
      _block_diag_ones(d, HEAD_DIM), ltri, lones, csum)


def _bdot(a, b):
    return lax.dot_general(a, b, (((2,), (1,)), ((0,), (0,))), preferred_element_type=F32)


def _bdot_nt(a, b):
    return lax.dot_general(a, b, (((2,), (2,)), ((0,), (0,))), preferred_element_type=F32)


def _bdot_tn(a, b):
    return lax.dot_general(a, b, (((1,), (1,)), ((0,), (0,))), preferred_element_type=F32)


def _rwkv_intra_kernel(at_ref, bt_ref, kt_ref, rt_ref, v_ref, ta_o, tr_o, arb_o, yv_o):
    c = RWKV_CHUNK
    _, nh, ts, dh = at_ref.shape
    n = nh * (ts // c)
    chunked = lambda ref: ref[0].reshape(n, c, dh)
    at, bt, kt, rt, v = (chunked(r) for r in (at_ref, bt_ref, kt_ref, rt_ref, v_ref))
    ri = lax.broadcasted_iota(jnp.int32, (1, c, c), 1)
    ci = lax.broadcasted_iota(jnp.int32, (1, c, c), 2)
    strict = ri > ci
    incl = ri >= ci
    ar = jnp.concatenate([at, rt], axis=1)
    xb = _bdot_nt(ar, bt)
    xk = _bdot_nt(ar, kt)
    l_ab = jnp.where(strict, xb[:, :c], 0.0)
    a_ak = jnp.where(strict, xk[:, :c], 0.0)
    a_rb = jnp.where(incl, xb[:, c:], 0.0)
    a_rk = jnp.where(incl, xk[:, c:], 0.0)
    pw = l_ab
    tinv = jnp.where(ri == ci, 1.0, 0.0) + l_ab
    for _ in range(int(np.log2(c)) - 1):
        pw_b = pw.astype(BF16)
        pw = _bdot(pw_b, pw_b)
        tinv = tinv + _bdot(tinv.astype(BF16), pw.astype(BF16))
    tinv_b = tinv.astype(BF16)

    def put(o, val):
        o[0] = val.reshape(nh, ts, val.shape[-1]).astype(o.dtype)

    put(ta_o, _bdot(tinv_b, at))
    put(tr_o, _bdot(tinv_b, _bdot(a_ak.astype(BF16), v).astype(BF16)))
    put(arb_o, a_rb)
    put(yv_o, _bdot(a_rk.astype(BF16), v))


def rwkv_intra(at, bt, kt, rt, v, ts):
    b, h, s, dh = at.shape
    seq = lambda: pl.BlockSpec((1, h, ts, dh), lambda bi, i: (bi, 0, i, 0))
    shp = lambda dt: jax.ShapeDtypeStruct((b, h, s, dh), dt)
    return pl.pallas_call(
        _rwkv_intra_kernel,
        grid=(b, s // ts),
        in_specs=[seq()] * 5,
        out_specs=[seq()] * 4,
        out_shape=[shp(BF16), shp(F32), shp(BF16), shp(F32)],
        compiler_params=_cparams("parallel", "parallel"),
        name="rwkv_intra",
    )(at, bt, kt, rt, v)


def _rwkv_scan_kernel(ta_ref, tr_ref, arb_ref, yv_ref, rt_ref, v_ref, bw_ref, kw_ref, wc_ref, y_ref, st_ref):
    c = RWKV_CHUNK
    nb, nh, ts, dh = ta_ref.shape
    n = nb * nh

    @pl.when(pl.program_id(0) == 0)
    def _():
        st_ref[...] = jnp.zeros_like(st_ref)

    def chunk_step(j, _):
        sl = (slice(None), slice(None), pl.ds(pl.multiple_of(j * c, c), c), slice(None))
        get = lambda ref: ref[sl].reshape(n, c, dh)
        st = st_ref[...]
        st_b = st.astype(BF16)
        u = _bdot_nt(get(ta_ref), st_b) + get(tr_ref)
        u_b = u.astype(BF16)
        y = _bdot_nt(get(rt_ref), st_b) + _bdot(get(arb_ref), u_b) + get(yv_ref)
        wc = wc_ref[:, :, pl.ds(pl.program_id(0) * (ts // c) + j, 1), :].reshape(n, 1, dh)
        st_ref[...] = st * wc + _bdot_tn(jnp.concatenate([u_b, get(v_ref)], axis=1),
                                         jnp.concatenate([get(bw_ref), get(kw_ref)], axis=1))
        y_ref[sl] = y.reshape(nb, nh, c, dh)
        return 0

    lax.fori_loop(0, ts // c, chunk_step, 0)


def rwkv_scan(ta, tr, arb, yv, rt, v, bw, kw, wc, ts):
    b, h, s, dh = ta.shape
    seq = lambda n: pl.BlockSpec((b, h, n, dh), lambda i: (0, 0, i, 0))
    return pl.pallas_call(
        _rwkv_scan_kernel,
        grid=(s // ts,),
        in_specs=[seq(ts)] * 8 + [pl.BlockSpec(wc.shape, lambda i: (0, 0, 0, 0))],
        out_specs=seq(ts),
        out_shape=jax.ShapeDtypeStruct((b, h, s, dh), F32),
        scratch_shapes=[pltpu.VMEM((b * h, dh, dh), F32)],
        compiler_params=_cparams("arbitrary"),
        name="rwkv_scan",
    )(ta, tr, arb, yv, rt, v, bw, kw, wc)


def _out_proj_kernel(x_ref, on_ref, y_ref, bonus_ref, g_ref, lnw_ref, lnb_ref, bd_ref, wn_ref, wr_ref, o_ref):
    y = jnp.concatenate([y_ref[0, h] for h in range(RWKV_HEADS)], axis=-1)
    bd = bd_ref[...]
    yc = y - _dot_x_sel(y, bd)
    yn = yc * lax.rsqrt(_dot_x_sel(yc * yc, bd) + GN_EPS)
    o_rwkv = (yn * lnw_ref[...] + lnb_ref[...] + bonus_ref[...]) * g_ref[...]
    tm = y.shape[0]
    o_nsa_t = on_ref[0].reshape(NSA_HEADS * HEAD_DIM, tm)
    o_ref[...] = (x_ref[...] + _dot_tn(o_nsa_t.astype(BF16), wn_ref[...])
                  + _dot(o_rwkv.astype(BF16), wr_ref[...]))


def out_proj(x, o_nsa_t, y, bonus, g, ln_w, ln_b, w_out, s, tm):
    t, d = x.shape
    dn = o_nsa_t.shape[1] * o_nsa_t.shape[2]
    dr = bonus.shape[1]
    tps = s // tm
    row = lambda w: pl.BlockSpec((tm, w), lambda i: (i, 0))
    full = lambda *sh: pl.BlockSpec(sh, lambda i: (0,) * len(sh))
    return pl.pallas_call(
        _out_proj_kernel,
        grid=(t // tm,),
        in_specs=[row(d), pl.BlockSpec((1, NSA_HEADS, HEAD_DIM, tm), lambda i: (i // tps, 0, 0, i % tps)),
                  pl.BlockSpec((1, RWKV_HEADS, tm, HEAD_DIM), lambda i: (i // tps, 0, i % tps, 0)),
                  row(dr), row(dr), full(1, dr), full(1, dr), full(dr, dr), full(dn, d), full(dr, d)],
        out_specs=row(d),
        out_shape=jax.ShapeDtypeStruct((t, d), F32),
        compiler_params=_cparams("parallel"),
        name="out_proj",
    )(x, o_nsa_t, y, bonus, g, ln_w.reshape(1, dr), ln_b.reshape(1, dr),
      _block_diag_ones(dr, HEAD_DIM, 1.0 / HEAD_DIM), w_out[:dn].astype(BF16), w_out[dn:].astype(BF16))


def _cross_attn_kernel(h_ref, g_ref, wq_ref, qg_ref, kv_ref, kg_ref, wo_ref, o_ref):
    h = h_ref[...]
    d = h.shape[1]
    xd = d // X_HEADS
    q = _dot(_rms(h, g_ref[...]).astype(BF16), wq_ref[...])
    kv = kv_ref[0]
    outs = []
    for hd in range(X_HEADS):
        qh = _rms(q[:, hd * xd:(hd + 1) * xd], qg_ref[...]) * (xd ** -0.5)
        kh = _rms(kv[:, hd * xd:(hd + 1) * xd], kg_ref[...])
        vh = kv[:, d + hd * xd:d + (hd + 1) * xd]
        s = _dot_nt(qh.astype(BF16), kh.astype(BF16))
        p = jnp.exp(s - jnp.max(s, axis=-1, keepdims=True))
        p = p / jnp.sum(p, axis=-1, keepdims=True)
        outs.append(_dot(p.astype(BF16), vh.astype(BF16)))
    o = jnp.concatenate(outs, axis=-1)
    o_ref[...] = h + _dot(o.astype(BF16), wo_ref[...])


def cross_attention(h, kv, norm_g, xq_w, xq_g, xk_g, xo_w, b, s, tm):
    t, d = h.shape
    m = kv.shape[1]
    xd = d // X_HEADS
    tiles = s // tm
    full = lambda *sh: pl.BlockSpec(sh, lambda i: (0,) * len(sh))
    return pl.pallas_call(
        _cross_attn_kernel,
        grid=(t // tm,),
        in_specs=[pl.BlockSpec((tm, d), lambda i: (i, 0)), full(1, d), full(d, d), full(1, xd),
                  pl.BlockSpec((1, m, 2 * d), lambda i: (i // tiles, 0, 0)), full(1, xd), full(d, d)],
        out_specs=pl.BlockSpec((tm, d), lambda i: (i, 0)),
        out_shape=jax.ShapeDtypeStruct((t, d), F32),
        compiler_params=_cparams("parallel"),
        name="cross_attention",
    )(h, norm_g.reshape(1, d), xq_w.astype(BF16), xq_g.reshape(1, xd), kv, xk_g.reshape(1, xd),
      xo_w.astype(BF16))


def _router_kernel(h_ref, g_ref, rw_ref, rb_ref, ltri_ref, xn_o, idx_o, gate_o, rank_o, count_o, seen_ref):
    @pl.when(pl.program_id(0) == 0)
    def _():
        seen_ref[...] = jnp.zeros_like(seen_ref)

    xn = _rms(h_ref[...], g_ref[...])
    xn_o[...] = xn
    logits = _dot(xn, rw_ref[...], precision=HIGHEST) + rb_ref[...]
    tm = logits.shape[0]
    lane = lax.broadcasted_iota(jnp.int32, (tm, LANES), 1)
    lanef = lane.astype(F32)
    logits = jnp.where(lane < N_EXPERTS, logits, REMOVED)
    idx_acc = jnp.zeros((tm, LANES), F32)
    val_acc = jnp.zeros((tm, LANES), F32)
    chosen = jnp.zeros((tm, LANES), F32)
    picks = []
    top = None
    for k in range(TOP_K):
        m = jnp.max(logits, axis=-1, keepdims=True)
        idx = jnp.min(jnp.where(logits == m, lanef, float(LANES)), axis=-1, keepdims=True)
        pick = lanef == idx
        picks.append(pick)
        chosen = jnp.where(pick, 1.0, chosen)
        logits = jnp.where(pick, REMOVED, logits)
        top = m if top is None else top
        idx_acc = jnp.where(lane == k, idx, idx_acc)
        val_acc = jnp.where(lane == k, jnp.exp(m - top), val_acc)
    idx_o[...] = idx_acc.astype(jnp.int32)
    gate_o[...] = val_acc / jnp.sum(val_acc, axis=-1, keepdims=True)
    before = seen_ref[0:1, :] + _dot(ltri_ref[...], chosen.astype(BF16))
    rank_acc = jnp.zeros((tm, LANES), F32)
    for k, pick in enumerate(picks):
        rank_acc = jnp.where(lane == k, jnp.sum(jnp.where(pick, before, 0.0), axis=-1, keepdims=True), rank_acc)
    rank_o[...] = rank_acc.astype(jnp.int32)
    seen_ref[...] = seen_ref[...] + jnp.sum(chosen, axis=0, keepdims=True)
    count_o[...] = seen_ref[...]


def moe_router(h, norm_g, router_w, router_b, tm):
    t, d = h.shape
    rw = jnp.zeros((d, LANES), F32).at[:, :N_EXPERTS].set(router_w)
    rb = jnp.zeros((1, LANES), F32).at[0, :N_EXPERTS].set(router_b)
    i = np.arange(tm)
    ltri = jnp.asarray(i[:, None] > i[None, :], BF16)
    row = lambda w: pl.BlockSpec((tm, w), lambda i: (i, 0))
    full = lambda *s: pl.BlockSpec(s, lambda i: (0,) * len(s))
    return pl.pallas_call(
        _router_kernel,
        grid=(t // tm,),
        in_specs=[row(d), full(1, d), full(d, LANES), full(1, LANES), full(tm, tm)],
        out_specs=[row(d), row(LANES), row(LANES), row(LANES), full(8, LANES)],
        out_shape=[jax.ShapeDtypeStruct((t, d), F32), jax.ShapeDtypeStruct((t, LANES), jnp.int32),
                   jax.ShapeDtypeStruct((t, LANES), F32), jax.ShapeDtypeStruct((t, LANES), jnp.int32),
                   jax.ShapeDtypeStruct((8, LANES), F32)],
        scratch_shapes=[pltpu.VMEM((8, LANES), F32)],
        compiler_params=_cparams("arbitrary"),
        name="moe_router",
    )(h, norm_g.reshape(1, d), rw, rb, ltri)


def _expert_kernel(blk_e_ref, n_used_ref, x_ref, w1_ref, b1_ref, w2_ref, b2_ref, o_ref):
    i = pl.program_id(0)

    @pl.when(i < n_used_ref[0])
    def _():
        x = x_ref[...].astype(BF16)
        hg = jnp.minimum(_dot(x, w1_ref[0, 0]) + b1_ref[0, 0], SWIGLU_LIMIT)
        hl = jnp.clip(_dot(x, w1_ref[0, 1]) + b1_ref[0, 1], -SWIGLU_LIMIT, SWIGLU_LIMIT)
        act = hg * jax.nn.sigmoid(SWIGLU_ALPHA * hg) * (hl + 1.0)
        o_ref[...] = _dot(act.astype(BF16), w2_ref[0].astype(BF16)) + b2_ref[0]

    @pl.when(i >= n_used_ref[0])
    def _():
        o_ref[...] = jnp.zeros_like(o_ref)


def moe_experts(xs, blk_e, n_used, w1, b1, w2, b2):
    r, d = xs.shape
    f = w1.shape[3]
    m = MOE_ROW_BLOCK
    ex = lambda *s: pl.BlockSpec((1,) + s, lambda i, be, nu: (be[i],) + (0,) * len(s))
    grid_spec = pltpu.PrefetchScalarGridSpec(
        num_scalar_prefetch=2,
        grid=(r // m,),
        in_specs=[pl.BlockSpec((m, d), lambda i, be, nu: (i, 0)),
                  ex(2, d, f), ex(2, 1, f), ex(f, d), ex(1, d)],
        out_specs=pl.BlockSpec((m, d), lambda i, be, nu: (i, 0)),
    )
    return pl.pallas_call(
        _expert_kernel,
        grid_spec=grid_spec,
        out_shape=jax.ShapeDtypeStruct((r, d), F32),
        compiler_params=_cparams("arbitrary"),
        name="moe_experts",
    )(blk_e, n_used, xs, w1, b1, w2, b2)


def _combine_kernel(h_ref, y_ref, gate_ref, o_ref):
    d = h_ref.shape[1]
    acc = h_ref[...]
    for k in range(TOP_K):
        acc = acc + gate_ref[:, k:k + 1] * y_ref[:, k * d:(k + 1) * d]
    o_ref[...] = acc


def moe_combine(h, y4, gate, tm):
    t, d = h.shape
    row = lambda w: pl.BlockSpec((tm, w), lambda i: (i, 0))
    return pl.pallas_call(
        _combine_kernel,
        grid=(t // tm,),
        in_specs=[row(d), row(TOP_K * d), row(LANES)],
        out_specs=row(d),
        out_shape=jax.ShapeDtypeStruct((t, d), F32),
        compiler_params=_cparams("parallel"),
        name="moe_combine",
    )(h, y4, gate)


def _layer(x, mem, norm_mix_g, w_in, q_norm_g, k_cmp_norm_g, k_slc_norm_g, k_win_norm_g,
           cmp_pe_k, cmp_pe_v, cmp_k_w1, cmp_k_w2, cmp_v_w1, cmp_v_w2,
           rwkv_mu, rwkv_w0, rwkv_w_up, rwkv_a0, rwkv_a_up, rwkv_g_up, rwkv_k_k, rwkv_k_a,
           rwkv_r_k, rwkv_ln_w, rwkv_ln_b, w_out,
           norm_x_g, norm_mem_g, xq_w, xk_w, xv_w, xq_norm_g, xk_norm_g, xo_w,
           norm_ffn_g, router_w, router_b, mlp1_w, mlp1_b, mlp2_w, mlp2_b):
    b, s, d = x.shape
    t = b * s
    tm = 512
    xt = x.reshape(t, d)

    w_nsa = jnp.pad(w_in[:, :NSA_PROJ], ((0, 0), (0, NSA_PROJ_PAD - NSA_PROJ)))
    p_nsa = norm_matmul(xt, norm_mix_g, w_nsa, tm)
    p_rwkv = norm_matmul(xt, norm_mix_g, w_in[:, NSA_PROJ:], tm)

    nq = NSA_HEADS * HEAD_DIM
    gw = NSA_GROUPS * HEAD_DIM
    qn, ks, vs, kw, vw, gates = nsa_prep(p_nsa, q_norm_g, k_slc_norm_g, k_win_norm_g, b, s, tm)
    kcv = nsa_compress(p_nsa[:, nq:nq + gw], p_nsa[:, nq + gw:nq + 2 * gw], cmp_pe_k, cmp_pe_v,
                       cmp_k_w1, cmp_k_w2, cmp_v_w1, cmp_v_w2, k_cmp_norm_g, b, s)
    o_nsa = nsa_attention(qn, kcv, ks, vs, kw, vw, gates, b, s)

    at, bt, kt, rt, v, bw, kwd, wc, g_gate, bonus = rwkv_prep(
        p_rwkv, rwkv_mu, rwkv_w0, rwkv_w_up, rwkv_a0, rwkv_a_up, rwkv_g_up, rwkv_k_k, rwkv_k_a, rwkv_r_k, b, s, tm)
    ta, tr, arb, yv = rwkv_intra(at, bt, kt, rt, v, ts=256)
    y = rwkv_scan(ta, tr, arb, yv, rt, v, bw, kwd, wc, ts=256)

    h1 = out_proj(xt, o_nsa, y, bonus, g_gate, rwkv_ln_w, rwkv_ln_b, w_out, s, tm)
    m = mem.shape[1]
    kv = norm_matmul(mem.reshape(b * m, d), norm_mem_g, jnp.concatenate([xk_w, xv_w], axis=1), m)
    h2 = cross_attention(h1, kv.reshape(b, m, 2 * d), norm_x_g, xq_w, xq_norm_g, xk_norm_g, xo_w, b, s, tm)

    xn, top_i, gate, rank, seen = moe_router(h2, norm_ffn_g, router_w, router_b, tm)
    top_i = top_i[:, :TOP_K]
    a = t * TOP_K
    mb = MOE_ROW_BLOCK
    order = jnp.argsort(top_i.reshape(a), stable=True)
    counts = seen[0, :N_EXPERTS].astype(jnp.int32)
    starts = jnp.cumsum(counts) - counts
    padded = (counts + mb - 1) // mb * mb
    pends = jnp.cumsum(padded)
    pstarts = pends - padded
    pos = (pstarts[top_i] + rank[:, :TOP_K]).reshape(a)
    n_blocks = -(-a // mb) + N_EXPERTS
    r = n_blocks * mb
    blk_start = jnp.arange(n_blocks, dtype=jnp.int32) * mb
    blk_e = jnp.minimum(jnp.sum(pends[None, :] <= blk_start[:, None], axis=1), N_EXPERTS - 1).astype(jnp.int32)
    src_i = blk_start[:, None] + jnp.arange(mb, dtype=jnp.int32)[None, :] - (pstarts - starts)[blk_e][:, None]
    valid = src_i < (starts + counts)[blk_e][:, None]
    row_src = jnp.where(valid, order[jnp.minimum(src_i, a - 1)] // TOP_K, 0).astype(jnp.int32).reshape(r)
    n_used = (pends[-1] // mb).astype(jnp.int32).reshape(1)
    xs = xn.at[row_src].get(mode="promise_in_bounds")
    f = mlp1_w.shape[2] // 2
    w1 = mlp1_w.reshape(N_EXPERTS, d, f, 2).transpose(0, 3, 1, 2).astype(BF16)
    b1 = mlp1_b.reshape(N_EXPERTS, 1, f, 2).transpose(0, 3, 1, 2)
    ys = moe_experts(xs, blk_e, n_used, w1, b1, mlp2_w, mlp2_b.reshape(N_EXPERTS, 1, d))
    y4 = ys.at[pos].get(mode="promise_in_bounds").reshape(t, TOP_K * d)
    out = moe_combine(h2, y4, gate, 256)
    return out.reshape(b, s, d)


def kernel(x, mem, norm_mix_g, w_in, q_norm_g, k_cmp_norm_g, k_slc_norm_g, k_win_norm_g, cmp_pe_k, cmp_pe_v, cmp_k_w1, cmp_k_w2, cmp_v_w1, cmp_v_w2, rwkv_mu, rwkv_w0, rwkv_w_up, rwkv_a0, rwkv_a_up, rwkv_g_up, rwkv_k_k, rwkv_k_a, rwkv_r_k, rwkv_ln_w, rwkv_ln_b, w_out, norm_x_g, norm_mem_g, xq_w, xk_w, xv_w, xq_norm_g, xk_norm_g, xo_w, norm_ffn_g, router_w, router_b, mlp1_w, mlp1_b, mlp2_w, mlp2_b):
    params = (norm_mix_g, w_in, q_norm_g, k_cmp_norm_g, k_slc_norm_g, k_win_norm_g, cmp_pe_k, cmp_pe_v,
              cmp_k_w1, cmp_k_w2, cmp_v_w1, cmp_v_w2, rwkv_mu, rwkv_w0, rwkv_w_up, rwkv_a0, rwkv_a_up,
              rwkv_g_up, rwkv_k_k, rwkv_k_a, rwkv_r_k, rwkv_ln_w, rwkv_ln_b, w_out, norm_x_g, norm_mem_g,
              xq_w, xk_w, xv_w, xq_norm_g, xk_norm_g, xo_w, norm_ffn_g, router_w, router_b,
              mlp1_w, mlp1_b, mlp2_w, mlp2_b)
    h = x
    for layer in range(norm_mix_g.shape[0]):
        h = _layer(h, mem, *[prm[layer] for prm in params])
    return h
```

```python
import functools

import numpy as np
import jax
import jax.numpy as jnp
from jax import lax
from jax.experimental import pallas as pl
from jax.experimental.pallas import tpu as pltpu

F32 = jnp.float32
BF16 = jnp.bfloat16
HIGHEST = lax.Precision.HIGHEST

V7X_VMEM_BYTES = 64 * 1024 * 1024
VMEM_LIMIT = V7X_VMEM_BYTES * 3 // 4

HEAD_DIM = 64
NSA_HEADS = 8
NSA_GROUPS = 2
NSA_HPG = NSA_HEADS // NSA_GROUPS
GROUP_W = NSA_HPG * HEAD_DIM
CMP_BLOCK = 32
CMP_STRIDE = 16
SLC_BLOCK = 64
SLC_TOPK = 16
WINDOW = 512
Q_BLOCK = 128
SEL_CHUNK = 512
RWKV_HEADS = 8
RWKV_DIM = RWKV_HEADS * HEAD_DIM
RWKV_CHUNK = 64
GN_EPS = HEAD_DIM * 1e-5
X_HEADS = 4
N_EXPERTS = 32
TOP_K = 4
SWIGLU_LIMIT = 7.0
SWIGLU_ALPHA = 1.702
MOE_ROW_BLOCK = 256
RMS_EPS = 1e-6
NEG_INF = -1e30
BIG = 1e9
REMOVED = -3e38
LANES = 128

NSA_PROJ = NSA_HEADS * HEAD_DIM + 6 * NSA_GROUPS * HEAD_DIM + NSA_HEADS * 3
NSA_PROJ_PAD = -(-NSA_PROJ // LANES) * LANES
RWKV_PROJ = 3 * RWKV_DIM + 64 + 64 + 128


def _cparams(*sem):
    return pltpu.CompilerParams(dimension_semantics=sem, vmem_limit_bytes=VMEM_LIMIT)


def _dot(a, b, **kw):
    return jnp.dot(a, b, preferred_element_type=F32, **kw)


def _dot_nt(a, b, **kw):
    return lax.dot_general(a, b, (((1,), (1,)), ((), ())), preferred_element_type=F32, **kw)


def _dot_tn(a, b, **kw):
    return lax.dot_general(a, b, (((0,), (0,)), ((), ())), preferred_element_type=F32, **kw)


def _rms(x, g):
    return x * lax.rsqrt(jnp.mean(x * x, axis=-1, keepdims=True) + RMS_EPS) * g


def _split_bf16(x, terms=3):
    parts = []
    for _ in range(terms):
        hi = x.astype(BF16)
        parts.append(hi)
        x = x - hi.astype(F32)
    return parts


def _dot_x_sel(x, sel):
    return sum(_dot(p, sel) for p in _split_bf16(x))


def _dot_sel_x(sel, x):
    return sum(_dot(sel, p) for p in _split_bf16(x))


def _block_diag_ones(n, blk, scale=1.0):
    i = np.arange(n)
    return jnp.asarray(((i[:, None] // blk) == (i[None, :] // blk)).astype(np.float32) * scale, BF16)


def _norm_matmul_kernel(x_ref, g_ref, w_ref, o_ref):
    xn = _rms(x_ref[...], g_ref[...]).astype(BF16)
    o_ref[...] = _dot(xn, w_ref[...])


def norm_matmul(x, g, w, tm):
    m, d = x.shape
    n = w.shape[1]
    return pl.pallas_call(
        _norm_matmul_kernel,
        grid=(m // tm,),
        in_specs=[pl.BlockSpec((tm, d), lambda i: (i, 0)),
                  pl.BlockSpec((1, d), lambda i: (0, 0)),
                  pl.BlockSpec((d, n), lambda i: (0, 0))],
        out_specs=pl.BlockSpec((tm, n), lambda i: (i, 0)),
        out_shape=jax.ShapeDtypeStruct((m, n), F32),
        compiler_params=_cparams("parallel"),
        name="norm_matmul",
    )(x, g.reshape(1, d), w.astype(BF16))


def _nsa_prep_kernel(p_ref, qg_ref, ksg_ref, kwg_ref, bdq_ref, bdk_ref, place_ref, feat_ref,
                     qt_o, ks_o, vst_o, kw_o, vwt_o, gate_o):
    p = p_ref[...]
    tm = p.shape[0]
    nq = NSA_HEADS * HEAD_DIM
    gw = NSA_GROUPS * HEAD_DIM
    q = p[:, :nq]
    msq = _dot_x_sel(q * q, bdq_ref[...])
    qn = q * lax.rsqrt(msq + RMS_EPS) * qg_ref[...] * (HEAD_DIM ** -0.5)
    qt_o[0] = qn.T.reshape(NSA_HEADS, HEAD_DIM, tm).astype(BF16)

    def seg(k):
        return p[:, nq + k * gw: nq + (k + 1) * gw]

    def head_norm(t, g):
        ms = _dot_x_sel(t * t, bdk_ref[...])
        return t * lax.rsqrt(ms + RMS_EPS) * g

    feat = feat_ref[...].astype(F32)
    ks = head_norm(seg(2), ksg_ref[...]).astype(BF16)
    kw = head_norm(seg(4), kwg_ref[...]).astype(BF16)
    vs_t = seg(3).T
    vw_t = seg(5).T
    for g in range(NSA_GROUPS):
        ks_o[0, g] = (_dot(ks, place_ref[g]) + feat).astype(BF16)
        kw_o[0, g] = (_dot(kw, place_ref[g])[:, :2 * HEAD_DIM] + feat[:, :2 * HEAD_DIM]).astype(BF16)
        for j in range(tm // Q_BLOCK):
            tile = (slice(g * HEAD_DIM, (g + 1) * HEAD_DIM), slice(j * Q_BLOCK, (j + 1) * Q_BLOCK))
            vst_o[0, g, j] = vs_t[tile].astype(BF16)
            vwt_o[0, g, j] = vw_t[tile].astype(BF16)
    gate_o[...] = jax.nn.sigmoid(p[:, nq + 6 * gw: nq + 6 * gw + LANES])


def nsa_prep(p_nsa, q_g, ks_g, kw_g, b, s, tm):
    t = p_nsa.shape[0]
    nq = NSA_HEADS * HEAD_DIM
    gw = NSA_GROUPS * HEAD_DIM
    grp, dh = NSA_GROUPS, HEAD_DIM
    tps = s // tm
    kx = 2 * dh + LANES
    assert s // SLC_BLOCK <= LANES
    tok = np.arange(s)
    feat = np.zeros((s, kx), np.float32)
    feat[:, dh] = tok // SLC_BLOCK * SLC_BLOCK
    feat[:, dh + 1] = tok % SLC_BLOCK
    feat[:, dh + 2:dh + 4] = 1.0
    feat[tok, 2 * dh + tok // SLC_BLOCK] = 1.0
    place = np.zeros((grp, gw, kx), np.float32)
    for g in range(grp):
        place[g, g * dh + np.arange(dh), np.arange(dh)] = 1.0
    tile = lambda v, n: jnp.tile(v.reshape(1, HEAD_DIM), (1, n))
    row = lambda w: pl.BlockSpec((tm, w), lambda i: (i, 0))
    full = lambda *sh: pl.BlockSpec(sh, lambda i: (0,) * len(sh))
    seq = lambda *sh: pl.BlockSpec((1, grp) + sh, lambda i: (i // tps, 0, i % tps) + (0,) * (len(sh) - 1))
    return pl.pallas_call(
        _nsa_prep_kernel,
        grid=(t // tm,),
        in_specs=[row(NSA_PROJ_PAD), full(1, nq), full(1, gw), full(1, gw), full(nq, nq), full(gw, gw),
                  full(grp, gw, kx), pl.BlockSpec((tm, kx), lambda i: (i % tps, 0))],
        out_specs=[pl.BlockSpec((1, NSA_HEADS, dh, tm), lambda i: (i // tps, 0, 0, i % tps)),
                   seq(tm, kx), seq(tm // Q_BLOCK, dh, Q_BLOCK), seq(tm, 2 * dh), seq(tm // Q_BLOCK, dh, Q_BLOCK),
                   row(LANES)],
        out_shape=[jax.ShapeDtypeStruct((b, NSA_HEADS, dh, s), BF16),
                   jax.ShapeDtypeStruct((b, grp, s, kx), BF16),
                   jax.ShapeDtypeStruct((b, grp, s // Q_BLOCK, dh, Q_BLOCK), BF16),
                   jax.ShapeDtypeStruct((b, grp, s, 2 * dh), BF16),
                   jax.ShapeDtypeStruct((b, grp, s // Q_BLOCK, dh, Q_BLOCK), BF16),
                   jax.ShapeDtypeStruct((t, LANES), F32)],
        compiler_params=_cparams("parallel"),
        name="nsa_prep",
    )(p_nsa, tile(q_g, NSA_HEADS), tile(ks_g, NSA_GROUPS), tile(kw_g, NSA_GROUPS),
      _block_diag_ones(nq, HEAD_DIM, 1.0 / HEAD_DIM), _block_diag_ones(gw, HEAD_DIM, 1.0 / HEAD_DIM),
      jnp.asarray(place, BF16), jnp.asarray(feat, BF16))


def _compress_kernel(ch_ref, pe_ref, w1_ref, w2_ref, g_ref, o_ref):
    ch = ch_ref[0, 0, 0]
    nc = ch.shape[0]
    half = CMP_STRIDE * HEAD_DIM
    nxt = pltpu.roll(ch, nc - 1, 0)
    w1 = w1_ref[0]
    h1 = (_dot(ch, w1[:half], precision=HIGHEST) + _dot(nxt, w1[half:], precision=HIGHEST)
          + _dot(pe_ref[0], w1, precision=HIGHEST))
    out = _dot(jax.nn.silu(h1), w2_ref[0], precision=HIGHEST)
    out = jnp.where(pl.program_id(0) == 0, _rms(out, g_ref[...]), out)
    o_ref[0, 0, 0] = out.astype(BF16)


def nsa_compress(kc, vc, pe_k, pe_v, kw1, kw2, vw1, vw2, kc_g, b, s):
    nc = s // CMP_STRIDE
    half = CMP_STRIDE * HEAD_DIM

    def chunks(t):
        return t.reshape(b, nc, CMP_STRIDE, NSA_GROUPS, HEAD_DIM).transpose(0, 3, 1, 2, 4).reshape(
            b, NSA_GROUPS, nc, half)

    ch = jnp.stack([chunks(kc), chunks(vc)])
    pe = jnp.stack([pe_k.reshape(1, 2 * half), pe_v.reshape(1, 2 * half)])
    return pl.pallas_call(
        _compress_kernel,
        grid=(2, b, NSA_GROUPS),
        in_specs=[pl.BlockSpec((1, 1, 1, nc, half), lambda kv, bi, g: (kv, bi, g, 0, 0)),
                  pl.BlockSpec((1, 1, 2 * half), lambda kv, bi, g: (kv, 0, 0)),
                  pl.BlockSpec((1, 2 * half, HEAD_DIM), lambda kv, bi, g: (kv, 0, 0)),
                  pl.BlockSpec((1, HEAD_DIM, HEAD_DIM), lambda kv, bi, g: (kv, 0, 0)),
                  pl.BlockSpec((1, HEAD_DIM), lambda kv, bi, g: (0, 0))],
        out_specs=pl.BlockSpec((1, 1, 1, nc, HEAD_DIM), lambda kv, bi, g: (kv, bi, g, 0, 0)),
        out_shape=jax.ShapeDtypeStruct((2, b, NSA_GROUPS, nc, HEAD_DIM), BF16),
        compiler_params=_cparams("parallel", "parallel", "parallel"),
        name="nsa_compress",
    )(ch, pe, jnp.stack([kw1, vw1]), jnp.stack([kw2, vw2]), kc_g.reshape(1, HEAD_DIM))


def _masked_exp_cols(s, mask):
    sm = jnp.where(mask, s, NEG_INF)
    m = jnp.max(sm, axis=0, keepdims=True)
    p = jnp.exp(sm - jnp.where(m > 0.5 * NEG_INF, m, 0.0))
    l = jnp.sum(p, axis=0, keepdims=True)
    return p, 1.0 / jnp.where(l > 0.0, l, 1.0)


def _nsa_attn_kernel(qt_ref, kc_ref, vct_ref, ovt_ref, ks_ref, vst_ref, kw_ref, vwt_ref, gate_ref, o_ref,
                     *, n_sel, n_top):
    g = pl.program_id(1)
    t0 = pl.program_id(2) * Q_BLOCK
    ks_ref, vst_ref, kw_ref, vwt_ref = (r.at[0, 0] for r in (ks_ref, vst_ref, kw_ref, vwt_ref))
    cols = NSA_HPG * Q_BLOCK
    col = lax.broadcasted_iota(jnp.int32, (1, cols), 1)
    tq = t0 + col % Q_BLOCK
    head = g * NSA_HPG + col // Q_BLOCK
    slope = lax.bitcast_convert_type((127 - (head + 1)) << 23, F32)
    qt = jnp.concatenate([qt_ref[0, h] for h in range(NSA_HPG)], axis=1)
    frow = lax.broadcasted_iota(jnp.int32, (HEAD_DIM, cols), 0)
    tq_hi = (tq // SLC_BLOCK * SLC_BLOCK).astype(F32)
    tq_lo = (tq % SLC_BLOCK).astype(F32)
    qpos = jnp.where(frow < 2, slope, jnp.where(frow == 2, -slope * tq_hi, jnp.where(frow == 3, -slope * tq_lo, 0.0)))
    qc = jnp.concatenate([qt, qpos.astype(BF16)], axis=0)

    def heads_sum(x):
        acc = x[:, :Q_BLOCK]
        for h in range(1, NSA_HPG):
            acc = acc + x[:, h * Q_BLOCK:(h + 1) * Q_BLOCK]
        return acc

    kc = kc_ref[0, 0]
    nc = kc.shape[0]
    c_last = lax.broadcasted_iota(jnp.int32, (nc, 1), 0) * CMP_STRIDE + (CMP_BLOCK - 1)
    p_c, inv_c = _masked_exp_cols(_dot(kc, qc), c_last <= tq)
    p_c = p_c * inv_c
    o_c = _dot(vct_ref[0, 0], p_c.astype(BF16))
    imp = _dot_sel_x(ovt_ref[...], heads_sum(p_c))

    bid = lax.broadcasted_iota(jnp.int32, (LANES, Q_BLOCK), 0)
    bidf = bid.astype(F32)
    tq1 = t0 + lax.broadcasted_iota(jnp.int32, (1, Q_BLOCK), 1)
    cur = tq1 // SLC_BLOCK
    forced = (bid == 0) | (bid == cur) | (bid == cur - 1)
    score = jnp.where(forced, BIG, jnp.where(bid * SLC_BLOCK <= tq1, imp, -BIG))
    score = jnp.where(bid < n_sel, score, REMOVED)
    picked = jnp.zeros((LANES, Q_BLOCK), jnp.bool_)
    for _ in range(n_top):
        m = jnp.max(score, axis=0, keepdims=True)
        pick = bidf == jnp.min(jnp.where(score == m, bidf, float(LANES)), axis=0, keepdims=True)
        picked = picked | pick
        score = jnp.where(pick, REMOVED, score)
    sel_neg = jnp.where(picked, 0.0, NEG_INF).astype(BF16)
    qx = jnp.concatenate([qc, jnp.concatenate([sel_neg] * NSA_HPG, axis=1)], axis=0)

    def values_t(ref, first_tile, n_tiles):
        return jnp.concatenate([ref[first_tile + i] for i in range(n_tiles)], axis=1)

    span = WINDOW + Q_BLOCK
    ws = pl.multiple_of(jnp.maximum(t0 - WINDOW, 0), Q_BLOCK)
    dw = tq - (ws + lax.broadcasted_iota(jnp.int32, (span, 1), 0))
    in_window = dw.astype(jnp.uint32) < WINDOW
    p_w, inv_w = _masked_exp_cols(_dot(kw_ref[pl.ds(ws, span), :], qc), in_window)
    o_w = _dot(values_t(vwt_ref, ws // Q_BLOCK, span // Q_BLOCK), p_w.astype(BF16)) * inv_w

    krow = lax.broadcasted_iota(jnp.int32, (SEL_CHUNK, 1), 0)

    def flash(s, start, carry):
        m, l, acc = carry
        m_new = jnp.maximum(m, jnp.max(s, axis=0, keepdims=True))
        alpha = jnp.exp(m - m_new)
        p = jnp.exp(s - m_new)
        l = alpha * l + jnp.sum(p, axis=0, keepdims=True)
        v = values_t(vst_ref, start // Q_BLOCK, SEL_CHUNK // Q_BLOCK)
        return m_new, l, alpha * acc + _dot(v, p.astype(BF16))

    bpc = SEL_CHUNK // SLC_BLOCK
    blk_any = jnp.max(jnp.where(picked, 1.0, 0.0), axis=1, keepdims=True)
    chunk_bit = lax.bitcast_convert_type((bid[:, :1] // bpc + 127) << 23, F32)
    bits = jnp.max((blk_any * chunk_bit).reshape(LANES // bpc, bpc, 1), axis=1)
    active = jnp.sum(bits, axis=0, keepdims=True)[0, 0].astype(jnp.int32)

    def full_step(j, carry):
        start = pl.multiple_of(j * SEL_CHUNK, SEL_CHUNK)
        return lax.cond((active >> j) & 1 == 1,
                        lambda c: flash(_dot(ks_ref[pl.ds(start, SEL_CHUNK), :], qx), start, c),
                        lambda c: c, carry)

    n_full = t0 // SEL_CHUNK
    init = (jnp.full((1, cols), NEG_INF, F32), jnp.zeros((1, cols), F32), jnp.zeros((HEAD_DIM, cols), F32))
    carry = lax.fori_loop(0, n_full, full_step, init)
    start = pl.multiple_of(n_full * SEL_CHUNK, SEL_CHUNK)
    s_diag = jnp.where(start + krow <= tq, _dot(ks_ref[pl.ds(start, SEL_CHUNK), :], qx), NEG_INF)
    _, l_s, acc_s = flash(s_diag, start, carry)
    o_s = acc_s / l_s

    for h in range(NSA_HPG):
        hs = slice(h * Q_BLOCK, (h + 1) * Q_BLOCK)
        gate = lambda br: gate_ref[0, 0, br, h:h + 1, :]
        o_ref[0, h] = gate(0) * o_c[:, hs] + gate(1) * o_s[:, hs] + gate(2) * o_w[:, hs]


def nsa_attention(qt, kcv, ks_x, vs_t, kw_x, vw_t, gates, b, s):
    nq_blocks = s // Q_BLOCK
    nc = s // CMP_STRIDE
    n_cmp = nc - CMP_BLOCK // CMP_STRIDE + 1
    n_sel = s // SLC_BLOCK
    n_top = min(SLC_TOPK, n_sel)
    grp, dh = NSA_GROUPS, HEAD_DIM
    assert n_sel <= LANES and s % SEL_CHUNK == 0 and s >= WINDOW + Q_BLOCK
    c_start = np.arange(n_cmp) * CMP_STRIDE
    s_start = np.arange(n_sel) * SLC_BLOCK
    ovt = np.zeros((LANES, nc), np.float32)
    ovt[:n_sel, :n_cmp] = (np.clip(np.minimum((c_start + CMP_BLOCK)[:, None], s_start[None] + SLC_BLOCK)
                                   - np.maximum(c_start[:, None], s_start[None]), 0, None) / CMP_BLOCK).T

    c_pos = np.arange(nc) * CMP_STRIDE + (CMP_BLOCK - 1)
    c_feat = np.zeros((nc, dh), np.float32)
    c_feat[:, 0] = c_pos // SLC_BLOCK * SLC_BLOCK
    c_feat[:, 1] = c_pos % SLC_BLOCK
    c_feat[:, 2:4] = 1.0
    kc_x = jnp.concatenate([kcv[0], jnp.broadcast_to(jnp.asarray(c_feat, BF16), kcv[0].shape)], axis=-1)
    vct = kcv[1].transpose(0, 1, 3, 2)
    gates_t = gates[:, :NSA_HEADS * 3].reshape(b, s, grp, NSA_HPG, 3).transpose(0, 2, 4, 3, 1)

    grp_spec = lambda *shape: pl.BlockSpec((1, 1) + shape, lambda bi, g, i: (bi, g) + (0,) * len(shape),
                                           pipeline_mode=pl.Buffered(1))
    return pl.pallas_call(
        functools.partial(_nsa_attn_kernel, n_sel=n_sel, n_top=n_top),
        grid=(b, grp, nq_blocks),
        in_specs=[pl.BlockSpec((1, NSA_HPG, dh, Q_BLOCK), lambda bi, g, i: (bi, g, 0, i)),
                  grp_spec(nc, 2 * dh), grp_spec(dh, nc),
                  pl.BlockSpec((LANES, nc), lambda bi, g, i: (0, 0)),
                  grp_spec(s, 2 * dh + LANES), grp_spec(s // Q_BLOCK, dh, Q_BLOCK),
                  grp_spec(s, 2 * dh), grp_spec(s // Q_BLOCK, dh, Q_BLOCK),
                  pl.BlockSpec((1, 1, 3, NSA_HPG, Q_BLOCK), lambda bi, g, i: (bi, g, 0, 0, i))],
        out_specs=pl.BlockSpec((1, NSA_HPG, dh, Q_BLOCK), lambda bi, g, i: (bi, g, 0, i)),
        out_shape=jax.ShapeDtypeStruct((b, NSA_HEADS, dh, s), F32),
        compiler_params=_cparams("parallel", "parallel", "arbitrary"),
        name="nsa_attention",
    )(qt, kc_x, vct, jnp.asarray(ovt, BF16), ks_x, vs_t, kw_x, vw_t, gates_t)


def _rwkv_prep_kernel(p_ref, prev_ref, mu_ref, w0_ref, a0_ref, kk_ref, ka_ref, rk_ref,
                      wup_ref, aup_ref, gup_ref, bd_ref, ltri_ref, lones_ref, csum_ref,
                      at_o, bt_o, kt_o, rt_o, v_o, bw_o, kw_o, wc_o, g_o, bonus_o, *, tiles_per_seq):
    p = p_ref[...]
    tm = p.shape[0]
    first = pl.program_id(0) % tiles_per_seq == 0
    last_prev = jnp.where(first, 0.0, prev_ref[7:8, :])
    prev = pltpu.roll(p, 1, 0)
    prev = jnp.where(lax.broadcasted_iota(jnp.int32, (tm, 1), 0) == 0, last_prev, prev)
    pm = p + (prev - p) * mu_ref[...]
    d = RWKV_DIM
    r, k, v = pm[:, :d], pm[:, d:2 * d], pm[:, 2 * d:3 * d]
    lora = pm[:, 3 * d:3 * d + LANES]
    gd = pm[:, 3 * d + LANES:3 * d + 2 * LANES]
    z = -(w0_ref[...] + _dot(jnp.tanh(lora).astype(BF16), wup_ref[...]))
    softplus = jnp.maximum(z, 0.0) + jnp.log(1.0 + jnp.exp(-jnp.abs(z)))
    w = -softplus - 0.5
    a = jax.nn.sigmoid(a0_ref[...] + _dot(lora.astype(BF16), aup_ref[...]))
    g_o[...] = _dot(jax.nn.sigmoid(gd).astype(BF16), gup_ref[...])
    bd = bd_ref[...]
    kkr = k * kk_ref[...]
    kk = kkr / jnp.maximum(jnp.sqrt(_dot_x_sel(kkr * kkr, bd)), 1e-12)
    k2 = k * (1.0 + (a - 1.0) * ka_ref[...])
    bonus_o[...] = _dot_x_sel(r * k2 * rk_ref[...], bd) * v
    lw = -jnp.exp(w)
    lw_parts = _split_bf16(lw)
    cum = sum(_dot(ltri_ref[...], p) for p in lw_parts)
    tot = sum(_dot(lones_ref[...], p) for p in lw_parts)
    e_in = jnp.exp(cum)
    e_out = jnp.exp(-cum)
    e_end = jnp.exp(tot - cum)

    def put_heads(o, val):
        for h in range(RWKV_HEADS):
            o[0, h] = val[:, h * HEAD_DIM:(h + 1) * HEAD_DIM].astype(o.dtype)

    put_heads(at_o, -kk * jnp.exp(cum - lw))
    put_heads(bt_o, kk * a * e_out)
    put_heads(kt_o, k2 * e_out)
    put_heads(rt_o, r * e_in)
    put_heads(v_o, v)
    put_heads(bw_o, kk * a * e_end)
    put_heads(kw_o, k2 * e_end)
    put_heads(wc_o, jnp.exp(sum(_dot(csum_ref[...], p) for p in lw_parts)))


def rwkv_prep(p_rwkv, mu, w0, w_up, a0, a_up, g_up, k_k, k_a, r_k, b, s, tm):
    t = p_rwkv.shape[0]
    d = RWKV_DIM
    c = RWKV_CHUNK
    tps = s // tm
    cpt = tm // c
    wup = jnp.concatenate([w_up, jnp.zeros_like(a_up)], axis=0).astype(BF16)
    aup = jnp.concatenate([jnp.zeros_like(w_up), a_up], axis=0).astype(BF16)
    i = np.arange(tm)
    same = (i[:, None] // c) == (i[None, :] // c)
    ltri = jnp.asarray(same & (i[:, None] >= i[None, :]), BF16)
    lones = jnp.asarray(same, BF16)
    csum = jnp.asarray(np.arange(cpt)[:, None] == (i[None, :] // c), BF16)
    row = lambda w: pl.BlockSpec((tm, w), lambda i: (i, 0))
    full = lambda *sh: pl.BlockSpec(sh, lambda i: (0,) * len(sh))
    heads = lambda n: pl.BlockSpec((1, RWKV_HEADS, n, HEAD_DIM), lambda i: (i // tps, 0, i % tps, 0))
    hshape = lambda n, dt: jax.ShapeDtypeStruct((b, RWKV_HEADS, n, HEAD_DIM), dt)
    vec = lambda x: x.reshape(1, -1)
    return pl.pallas_call(
        functools.partial(_rwkv_prep_kernel, tiles_per_seq=tps),
        grid=(t // tm,),
        in_specs=[row(RWKV_PROJ),
                  pl.BlockSpec((8, RWKV_PROJ), lambda i: (jnp.maximum(i * (tm // 8) - 1, 0), 0)),
                  full(1, RWKV_PROJ), full(1, d), full(1, d), full(1, d), full(1, d), full(1, d),
                  full(LANES, d), full(LANES, d), full(LANES, d), full(d, d), full(tm, tm), full(tm, tm),
                  full(cpt, tm)],
        out_specs=[heads(tm)] * 7 + [heads(cpt), row(d), row(d)],
        out_shape=[hshape(s, BF16)] * 7 + [hshape(s // c, F32)] + [jax.ShapeDtypeStruct((t, d), F32)] * 2,
        compiler_params=_cparams("parallel"),
        name="rwkv_prep",
    )(p_rwkv, p_rwkv, vec(mu), vec(w0), vec(a0), vec(k_k), vec(k_a), vec(r_k), wup, aup, g_up.astype(BF16),
      _block_diag_ones(d, HEAD_DIM), ltri, lones, csum)


def _bdot(a, b):
    return lax.dot_general(a, b, (((2,), (1,)), ((0,), (0,))), preferred_element_type=F32)


def _bdot_nt(a, b):
    return lax.dot_general(a, b, (((2,), (2,)), ((0,), (0,))), preferred_element_type=F32)


def _bdot_tn(a, b):
    return lax.dot_general(a, b, (((1,), (1,)), ((0,), (0,))), preferred_element_type=F32)


def _rwkv_intra_kernel(at_ref, bt_ref, kt_ref, rt_ref, v_ref, ta_o, tr_o, arb_o, yv_o):
    c = RWKV_CHUNK
    _, nh, ts, dh = at_ref.shape
    n = nh * (ts // c)
    chunked = lambda ref: ref[0].reshape(n, c, dh)
    at, bt, kt, rt, v = (chunked(r) for r in (at_ref, bt_ref, kt_ref, rt_ref, v_ref))
    ri = lax.broadcasted_iota(jnp.int32, (1, c, c), 1)
    ci = lax.broadcasted_iota(jnp.int32, (1, c, c), 2)
    strict = ri > ci
    incl = ri >= ci
    ar = jnp.concatenate([at, rt], axis=1)
    xb = _bdot_nt(ar, bt)
    xk = _bdot_nt(ar, kt)
    l_ab = jnp.where(strict, xb[:, :c], 0.0)
    a_ak = jnp.where(strict, xk[:, :c], 0.0)
    a_rb = jnp.where(incl, xb[:, c:], 0.0)
    a_rk = jnp.where(incl, xk[:, c:], 0.0)
    pw = l_ab
    tinv = jnp.where(ri == ci, 1.0, 0.0) + l_ab
    for _ in range(int(np.log2(c)) - 1):
        pw_b = pw.astype(BF16)
        pw = _bdot(pw_b, pw_b)
        tinv = tinv + _bdot(tinv.astype(BF16), pw.astype(BF16))
    tinv_b = tinv.astype(BF16)

    def put(o, val):
        o[0] = val.reshape(nh, ts, val.shape[-1]).astype(o.dtype)

    put(ta_o, _bdot(tinv_b, at))
    put(tr_o, _bdot(tinv_b, _bdot(a_ak.astype(BF16), v).astype(BF16)))
    put(arb_o, a_rb)
    put(yv_o, _bdot(a_rk.astype(BF16), v))


def rwkv_intra(at, bt, kt, rt, v, ts):
    b, h, s, dh = at.shape
    seq = lambda: pl.BlockSpec((1, h, ts, dh), lambda bi, i: (bi, 0, i, 0))
    shp = lambda dt: jax.ShapeDtypeStruct((b, h, s, dh), dt)
    return pl.pallas_call(
        _rwkv_intra_kernel,
        grid=(b, s // ts),
        in_specs=[seq()] * 5,
        out_specs=[seq()] * 4,
        out_shape=[shp(BF16), shp(F32), shp(BF16), shp(F32)],
        compiler_params=_cparams("parallel", "parallel"),
        name="rwkv_intra",
    )(at, bt, kt, rt, v)


def _rwkv_scan_kernel(ta_ref, tr_ref, arb_ref, yv_ref, rt_ref, v_ref, bw_ref, kw_ref, wc_ref, y_ref, st_ref):
    c = RWKV_CHUNK
    nb, nh, ts, dh = ta_ref.shape
    n = nb * nh

    @pl.when(pl.program_id(0) == 0)
    def _():
        st_ref[...] = jnp.zeros_like(st_ref)

    def chunk_step(j, _):
        sl = (slice(None), slice(None), pl.ds(pl.multiple_of(j * c, c), c), slice(None))
        get = lambda ref: ref[sl].reshape(n, c, dh)
        st = st_ref[...]
        st_b = st.astype(BF16)
        u = _bdot_nt(get(ta_ref), st_b) + get(tr_ref)
        u_b = u.astype(BF16)
        y = _bdot_nt(get(rt_ref), st_b) + _bdot(get(arb_ref), u_b) + get(yv_ref)
        wc = wc_ref[:, :, pl.ds(pl.program_id(0) * (ts // c) + j, 1), :].reshape(n, 1, dh)
        st_ref[...] = st * wc + _bdot_tn(jnp.concatenate([u_b, get(v_ref)], axis=1),
                                         jnp.concatenate([get(bw_ref), get(kw_ref)], axis=1))
        y_ref[sl] = y.reshape(nb, nh, c, dh)
        return 0

    lax.fori_loop(0, ts // c, chunk_step, 0)


def rwkv_scan(ta, tr, arb, yv, rt, v, bw, kw, wc, ts):
    b, h, s, dh = ta.shape
    seq = lambda n: pl.BlockSpec((b, h, n, dh), lambda i: (0, 0, i, 0))
    return pl.pallas_call(
        _rwkv_scan_kernel,
        grid=(s // ts,),
        in_specs=[seq(ts)] * 8 + [pl.BlockSpec(wc.shape, lambda i: (0, 0, 0, 0))],
        out_specs=seq(ts),
        out_shape=jax.ShapeDtypeStruct((b, h, s, dh), F32),
        scratch_shapes=[pltpu.VMEM((b * h, dh, dh), F32)],
        compiler_params=_cparams("arbitrary"),
        name="rwkv_scan",
    )(ta, tr, arb, yv, rt, v, bw, kw, wc)


def _out_proj_kernel(x_ref, on_ref, y_ref, bonus_ref, g_ref, lnw_ref, lnb_ref, bd_ref, wn_ref, wr_ref, o_ref):
    y = jnp.concatenate([y_ref[0, h] for h in range(RWKV_HEADS)], axis=-1)
    bd = bd_ref[...]
    yc = y - _dot_x_sel(y, bd)
    yn = yc * lax.rsqrt(_dot_x_sel(yc * yc, bd) + GN_EPS)
    o_rwkv = (yn * lnw_ref[...] + lnb_ref[...] + bonus_ref[...]) * g_ref[...]
    tm = y.shape[0]
    o_nsa_t = on_ref[0].reshape(NSA_HEADS * HEAD_DIM, tm)
    o_ref[...] = (x_ref[...] + _dot_tn(o_nsa_t.astype(BF16), wn_ref[...])
                  + _dot(o_rwkv.astype(BF16), wr_ref[...]))


def out_proj(x, o_nsa_t, y, bonus, g, ln_w, ln_b, w_out, s, tm):
    t, d = x.shape
    dn = o_nsa_t.shape[1] * o_nsa_t.shape[2]
    dr = bonus.shape[1]
    tps = s // tm
    row = lambda w: pl.BlockSpec((tm, w), lambda i: (i, 0))
    full = lambda *sh: pl.BlockSpec(sh, lambda i: (0,) * len(sh))
    return pl.pallas_call(
        _out_proj_kernel,
        grid=(t // tm,),
        in_specs=[row(d), pl.BlockSpec((1, NSA_HEADS, HEAD_DIM, tm), lambda i: (i // tps, 0, 0, i % tps)),
                  pl.BlockSpec((1, RWKV_HEADS, tm, HEAD_DIM), lambda i: (i // tps, 0, i % tps, 0)),
                  row(dr), row(dr), full(1, dr), full(1, dr), full(dr, dr), full(dn, d), full(dr, d)],
        out_specs=row(d),
        out_shape=jax.ShapeDtypeStruct((t, d), F32),
        compiler_params=_cparams("parallel"),
        name="out_proj",
    )(x, o_nsa_t, y, bonus, g, ln_w.reshape(1, dr), ln_b.reshape(1, dr),
      _block_diag_ones(dr, HEAD_DIM, 1.0 / HEAD_DIM), w_out[:dn].astype(BF16), w_out[dn:].astype(BF16))


def _cross_attn_kernel(h_ref, g_ref, wq_ref, qg_ref, kv_ref, kg_ref, wo_ref, o_ref):
    h = h_ref[...]
    d = h.shape[1]
    xd = d // X_HEADS
    q = _dot(_rms(h, g_ref[...]).astype(BF16), wq_ref[...])
    kv = kv_ref[0]
    outs = []
    for hd in range(X_HEADS):
        qh = _rms(q[:, hd * xd:(hd + 1) * xd], qg_ref[...]) * (xd ** -0.5)
        kh = _rms(kv[:, hd * xd:(hd + 1) * xd], kg_ref[...])
        vh = kv[:, d + hd * xd:d + (hd + 1) * xd]
        s = _dot_nt(qh.astype(BF16), kh.astype(BF16))
        p = jnp.exp(s - jnp.max(s, axis=-1, keepdims=True))
        p = p / jnp.sum(p, axis=-1, keepdims=True)
        outs.append(_dot(p.astype(BF16), vh.astype(BF16)))
    o = jnp.concatenate(outs, axis=-1)
    o_ref[...] = h + _dot(o.astype(BF16), wo_ref[...])


def cross_attention(h, kv, norm_g, xq_w, xq_g, xk_g, xo_w, b, s, tm):
    t, d = h.shape
    m = kv.shape[1]
    xd = d // X_HEADS
    tiles = s // tm
    full = lambda *sh: pl.BlockSpec(sh, lambda i: (0,) * len(sh))
    return pl.pallas_call(
        _cross_attn_kernel,
        grid=(t // tm,),
        in_specs=[pl.BlockSpec((tm, d), lambda i: (i, 0)), full(1, d), full(d, d), full(1, xd),
                  pl.BlockSpec((1, m, 2 * d), lambda i: (i // tiles, 0, 0)), full(1, xd), full(d, d)],
        out_specs=pl.BlockSpec((tm, d), lambda i: (i, 0)),
        out_shape=jax.ShapeDtypeStruct((t, d), F32),
        compiler_params=_cparams("parallel"),
        name="cross_attention",
    )(h, norm_g.reshape(1, d), xq_w.astype(BF16), xq_g.reshape(1, xd), kv, xk_g.reshape(1, xd),
      xo_w.astype(BF16))


def _router_kernel(h_ref, g_ref, rw_ref, rb_ref, ltri_ref, xn_o, idx_o, gate_o, rank_o, count_o, seen_ref):
    @pl.when(pl.program_id(0) == 0)
    def _():
        seen_ref[...] = jnp.zeros_like(seen_ref)

    xn = _rms(h_ref[...], g_ref[...])
    xn_o[...] = xn
    logits = _dot(xn, rw_ref[...], precision=HIGHEST) + rb_ref[...]
    tm = logits.shape[0]
    lane = lax.broadcasted_iota(jnp.int32, (tm, LANES), 1)
    lanef = lane.astype(F32)
    logits = jnp.where(lane < N_EXPERTS, logits, REMOVED)
    idx_acc = jnp.zeros((tm, LANES), F32)
    val_acc = jnp.zeros((tm, LANES), F32)
    chosen = jnp.zeros((tm, LANES), F32)
    picks = []
    top = None
    for k in range(TOP_K):
        m = jnp.max(logits, axis=-1, keepdims=True)
        idx = jnp.min(jnp.where(logits == m, lanef, float(LANES)), axis=-1, keepdims=True)
        pick = lanef == idx
        picks.append(pick)
        chosen = jnp.where(pick, 1.0, chosen)
        logits = jnp.where(pick, REMOVED, logits)
        top = m if top is None else top
        idx_acc = jnp.where(lane == k, idx, idx_acc)
        val_acc = jnp.where(lane == k, jnp.exp(m - top), val_acc)
    idx_o[...] = idx_acc.astype(jnp.int32)
    gate_o[...] = val_acc / jnp.sum(val_acc, axis=-1, keepdims=True)
    before = seen_ref[0:1, :] + _dot(ltri_ref[...], chosen.astype(BF16))
    rank_acc = jnp.zeros((tm, LANES), F32)
    for k, pick in enumerate(picks):
        rank_acc = jnp.where(lane == k, jnp.sum(jnp.where(pick, before, 0.0), axis=-1, keepdims=True), rank_acc)
    rank_o[...] = rank_acc.astype(jnp.int32)
    seen_ref[...] = seen_ref[...] + jnp.sum(chosen, axis=0, keepdims=True)
    count_o[...] = seen_ref[...]


def moe_router(h, norm_g, router_w, router_b, tm):
    t, d = h.shape
    rw = jnp.zeros((d, LANES), F32).at[:, :N_EXPERTS].set(router_w)
    rb = jnp.zeros((1, LANES), F32).at[0, :N_EXPERTS].set(router_b)
    i = np.arange(tm)
    ltri = jnp.asarray(i[:, None] > i[None, :], BF16)
    row = lambda w: pl.BlockSpec((tm, w), lambda i: (i, 0))
    full = lambda *s: pl.BlockSpec(s, lambda i: (0,) * len(s))
    return pl.pallas_call(
        _router_kernel,
        grid=(t // tm,),
        in_specs=[row(d), full(1, d), full(d, LANES), full(1, LANES), full(tm, tm)],
        out_specs=[row(d), row(LANES), row(LANES), row(LANES), full(8, LANES)],
        out_shape=[jax.ShapeDtypeStruct((t, d), F32), jax.ShapeDtypeStruct((t, LANES), jnp.int32),
                   jax.ShapeDtypeStruct((t, LANES), F32), jax.ShapeDtypeStruct((t, LANES), jnp.int32),
                   jax.ShapeDtypeStruct((8, LANES), F32)],
        scratch_shapes=[pltpu.VMEM((8, LANES), F32)],
        compiler_params=_cparams("arbitrary"),
        name="moe_router",
    )(h, norm_g.reshape(1, d), rw, rb, ltri)


def _expert_kernel(blk_e_ref, n_used_ref, x_ref, w1_ref, b1_ref, w2_ref, b2_ref, o_ref):
    i = pl.program_id(0)

    @pl.when(i < n_used_ref[0])
    def _():
        x = x_ref[...].astype(BF16)
        h = _dot(x, w1_ref[0].astype(BF16)) + b1_ref[0]
        hg = jnp.minimum(h, SWIGLU_LIMIT)
        gate = hg * jax.nn.sigmoid(SWIGLU_ALPHA * hg)
        lin = jnp.clip(h, -SWIGLU_LIMIT, SWIGLU_LIMIT) + 1.0
        even = lax.broadcasted_iota(jnp.int32, (1, LANES), 1) % 2 == 0
        parts = []
        for c in range(h.shape[1] // LANES):
            cols = slice(c * LANES, (c + 1) * LANES)
            nxt = pltpu.roll(lin[:, cols], LANES - 1, 1)
            parts.append(jnp.where(even, gate[:, cols] * nxt, 0.0).astype(BF16))
        act = jnp.concatenate(parts, axis=1)
        o_ref[...] = _dot(act, w2_ref[0]) + b2_ref[0]

    @pl.when(i >= n_used_ref[0])
    def _():
        o_ref[...] = jnp.zeros_like(o_ref)


def moe_experts(xs, blk_e, n_used, w1, b1, w2x, b2):
    r, d = xs.shape
    f2 = w1.shape[2]
    m = MOE_ROW_BLOCK
    ex = lambda *s: pl.BlockSpec((1,) + s, lambda i, be, nu: (be[i],) + (0,) * len(s))
    grid_spec = pltpu.PrefetchScalarGridSpec(
        num_scalar_prefetch=2,
        grid=(r // m,),
        in_specs=[pl.BlockSpec((m, d), lambda i, be, nu: (i, 0)),
                  ex(d, f2), ex(1, f2), ex(f2, d), ex(1, d)],
        out_specs=pl.BlockSpec((m, d), lambda i, be, nu: (i, 0)),
    )
    return pl.pallas_call(
        _expert_kernel,
        grid_spec=grid_spec,
        out_shape=jax.ShapeDtypeStruct((r, d), F32),
        compiler_params=_cparams("arbitrary"),
        name="moe_experts",
    )(blk_e, n_used, xs, w1, b1, w2x, b2)


def _combine_kernel(h_ref, y_ref, gate_ref, o_ref):
    d = h_ref.shape[1]
    acc = h_ref[...]
    for k in range(TOP_K):
        acc = acc + gate_ref[:, k:k + 1] * y_ref[:, k * d:(k + 1) * d]
    o_ref[...] = acc


def moe_combine(h, y4, gate, tm):
    t, d = h.shape
    row = lambda w: pl.BlockSpec((tm, w), lambda i: (i, 0))
    return pl.pallas_call(
        _combine_kernel,
        grid=(t // tm,),
        in_specs=[row(d), row(TOP_K * d), row(LANES)],
        out_specs=row(d),
        out_shape=jax.ShapeDtypeStruct((t, d), F32),
        compiler_params=_cparams("parallel"),
        name="moe_combine",
    )(h, y4, gate)


def _layer(x, mem, norm_mix_g, w_in, q_norm_g, k_cmp_norm_g, k_slc_norm_g, k_win_norm_g,
           cmp_pe_k, cmp_pe_v, cmp_k_w1, cmp_k_w2, cmp_v_w1, cmp_v_w2,
           rwkv_mu, rwkv_w0, rwkv_w_up, rwkv_a0, rwkv_a_up, rwkv_g_up, rwkv_k_k, rwkv_k_a,
           rwkv_r_k, rwkv_ln_w, rwkv_ln_b, w_out,
           norm_x_g, norm_mem_g, xq_w, xk_w, xv_w, xq_norm_g, xk_norm_g, xo_w,
           norm_ffn_g, router_w, router_b, mlp1_w, mlp1_b, mlp2_w, mlp2_b):
    b, s, d = x.shape
    t = b * s
    tm = 512
    xt = x.reshape(t, d)

    w_nsa = jnp.pad(w_in[:, :NSA_PROJ], ((0, 0), (0, NSA_PROJ_PAD - NSA_PROJ)))
    p_nsa = norm_matmul(xt, norm_mix_g, w_nsa, tm)
    p_rwkv = norm_matmul(xt, norm_mix_g, w_in[:, NSA_PROJ:], tm)

    nq = NSA_HEADS * HEAD_DIM
    gw = NSA_GROUPS * HEAD_DIM
    qn, ks, vs, kw, vw, gates = nsa_prep(p_nsa, q_norm_g, k_slc_norm_g, k_win_norm_g, b, s, tm)
    kcv = nsa_compress(p_nsa[:, nq:nq + gw], p_nsa[:, nq + gw:nq + 2 * gw], cmp_pe_k, cmp_pe_v,
                       cmp_k_w1, cmp_k_w2, cmp_v_w1, cmp_v_w2, k_cmp_norm_g, b, s)
    o_nsa = nsa_attention(qn, kcv, ks, vs, kw, vw, gates, b, s)

    at, bt, kt, rt, v, bw, kwd, wc, g_gate, bonus = rwkv_prep(
        p_rwkv, rwkv_mu, rwkv_w0, rwkv_w_up, rwkv_a0, rwkv_a_up, rwkv_g_up, rwkv_k_k, rwkv_k_a, rwkv_r_k, b, s, tm)
    ta, tr, arb, yv = rwkv_intra(at, bt, kt, rt, v, ts=256)
    y = rwkv_scan(ta, tr, arb, yv, rt, v, bw, kwd, wc, ts=256)

    h1 = out_proj(xt, o_nsa, y, bonus, g_gate, rwkv_ln_w, rwkv_ln_b, w_out, s, tm)
    m = mem.shape[1]
    kv = norm_matmul(mem.reshape(b * m, d), norm_mem_g, jnp.concatenate([xk_w, xv_w], axis=1), m)
    h2 = cross_attention(h1, kv.reshape(b, m, 2 * d), norm_x_g, xq_w, xq_norm_g, xk_norm_g, xo_w, b, s, tm)

    xn, top_i, gate, rank, seen = moe_router(h2, norm_ffn_g, router_w, router_b, tm)
    top_i = top_i[:, :TOP_K]
    a = t * TOP_K
    mb = MOE_ROW_BLOCK
    order = jnp.argsort(top_i.reshape(a), stable=True)
    counts = seen[0, :N_EXPERTS].astype(jnp.int32)
    starts = jnp.cumsum(counts) - counts
    padded = (counts + mb - 1) // mb * mb
    pends = jnp.cumsum(padded)
    pstarts = pends - padded
    pos = (pstarts[top_i] + rank[:, :TOP_K]).reshape(a)
    n_blocks = -(-a // mb) + N_EXPERTS
    r = n_blocks * mb
    blk_start = jnp.arange(n_blocks, dtype=jnp.int32) * mb
    blk_e = jnp.minimum(jnp.sum(pends[None, :] <= blk_start[:, None], axis=1), N_EXPERTS - 1).astype(jnp.int32)
    src_i = blk_start[:, None] + jnp.arange(mb, dtype=jnp.int32)[None, :] - (pstarts - starts)[blk_e][:, None]
    valid = src_i < (starts + counts)[blk_e][:, None]
    row_src = jnp.where(valid, order[jnp.minimum(src_i, a - 1)] // TOP_K, 0).astype(jnp.int32).reshape(r)
    n_used = (pends[-1] // mb).astype(jnp.int32).reshape(1)
    xs = xn.at[row_src].get(mode="promise_in_bounds")
    f2 = mlp1_w.shape[2]
    w2x = jnp.stack([mlp2_w.astype(BF16), jnp.zeros(mlp2_w.shape, BF16)], axis=2).reshape(N_EXPERTS, f2, d)
    ys = moe_experts(xs, blk_e, n_used, mlp1_w, mlp1_b.reshape(N_EXPERTS, 1, f2), w2x,
                     mlp2_b.reshape(N_EXPERTS, 1, d))
    y4 = ys.at[pos].get(mode="promise_in_bounds").reshape(t, TOP_K * d)
    out = moe_combine(h2, y4, gate, 256)
    return out.reshape(b, s, d)


def kernel(x, mem, norm_mix_g, w_in, q_norm_g, k_cmp_norm_g, k_slc_norm_g, k_win_norm_g, cmp_pe_k, cmp_pe_v, cmp_k_w1, cmp_k_w2, cmp_v_w1, cmp_v_w2, rwkv_mu, rwkv_w0, rwkv_w_up, rwkv_a0, rwkv_a_up, rwkv_g_up, rwkv_k_k, rwkv_k_a, rwkv_r_k, rwkv_ln_w, rwkv_ln_b, w_out, norm_x_g, norm_mem_g, xq_w, xk_w, xv_w, xq_norm_g, xk_norm_g, xo_w, norm_ffn_g, router_w, router_b, mlp1_w, mlp1_b, mlp2_w, mlp2_b):
    params = (norm_mix_g, w_in, q_norm_g, k_cmp_norm_g, k_slc_norm_g, k_win_norm_g, cmp_pe_k, cmp_pe_v,
              cmp_k_w1, cmp_k_w2, cmp_v_w1, cmp_v_w2, rwkv_mu, rwkv_w0, rwkv_w_up, rwkv_a0, rwkv_a_up,
              rwkv_g_up, rwkv_k_k, rwkv_k_a, rwkv_r_k, rwkv_ln_w, rwkv_ln_b, w_out, norm_x_g, norm_mem_g,
              xq_w, xk_w, xv_w, xq_norm_g, xk_norm_g, xo_w, norm_ffn_g, router_w, router_b,
              mlp1_w, mlp1_b, mlp2_w, mlp2_b)
    h = x
    for layer in range(norm_mix_g.shape[0]):
        h = _layer(h, mem, *[prm[layer] for prm in params])
    return h
```

```python
import functools

import numpy as np
import jax
import jax.numpy as jnp
from jax import lax
from jax.experimental import pallas as pl
from jax.experimental.pallas import tpu as pltpu

F32 = jnp.float32
BF16 = jnp.bfloat16
HIGHEST = lax.Precision.HIGHEST

V7X_VMEM_BYTES = 64 * 1024 * 1024
VMEM_LIMIT = V7X_VMEM_BYTES * 3 // 4

HEAD_DIM = 64
NSA_HEADS = 8
NSA_GROUPS = 2
NSA_HPG = NSA_HEADS // NSA_GROUPS
GROUP_W = NSA_HPG * HEAD_DIM
CMP_BLOCK = 32
CMP_STRIDE = 16
SLC_BLOCK = 64
SLC_TOPK = 16
WINDOW = 512
Q_BLOCK = 128
SEL_CHUNK = 512
RWKV_HEADS = 8
RWKV_DIM = RWKV_HEADS * HEAD_DIM
RWKV_CHUNK = 64
GN_EPS = HEAD_DIM * 1e-5
X_HEADS = 4
N_EXPERTS = 32
TOP_K = 4
SWIGLU_LIMIT = 7.0
SWIGLU_ALPHA = 1.702
MOE_ROW_BLOCK = 256
RMS_EPS = 1e-6
NEG_INF = -1e30
BIG = 1e9
REMOVED = -3e38
LANES = 128

NSA_PROJ = NSA_HEADS * HEAD_DIM + 6 * NSA_GROUPS * HEAD_DIM + NSA_HEADS * 3
NSA_PROJ_PAD = -(-NSA_PROJ // LANES) * LANES
RWKV_PROJ = 3 * RWKV_DIM + 64 + 64 + 128


def _cparams(*sem):
    return pltpu.CompilerParams(dimension_semantics=sem, vmem_limit_bytes=VMEM_LIMIT)


def _dot(a, b, **kw):
    return jnp.dot(a, b, preferred_element_type=F32, **kw)


def _dot_nt(a, b, **kw):
    return lax.dot_general(a, b, (((1,), (1,)), ((), ())), preferred_element_type=F32, **kw)


def _dot_tn(a, b, **kw):
    return lax.dot_general(a, b, (((0,), (0,)), ((), ())), preferred_element_type=F32, **kw)


def _rms(x, g):
    return x * lax.rsqrt(jnp.mean(x * x, axis=-1, keepdims=True) + RMS_EPS) * g


def _split_bf16(x, terms=3):
    parts = []
    for _ in range(terms):
        hi = x.astype(BF16)
        parts.append(hi)
        x = x - hi.astype(F32)
    return parts


def _dot_x_sel(x, sel):
    return sum(_dot(p, sel) for p in _split_bf16(x))


def _dot_sel_x(sel, x):
    return sum(_dot(sel, p) for p in _split_bf16(x))


def _block_diag_ones(n, blk, scale=1.0):
    i = np.arange(n)
    return jnp.asarray(((i[:, None] // blk) == (i[None, :] // blk)).astype(np.float32) * scale, BF16)


def _norm_matmul_kernel(x_ref, g_ref, w_ref, o_ref):
    xn = _rms(x_ref[...], g_ref[...]).astype(BF16)
    o_ref[...] = _dot(xn, w_ref[...])


def norm_matmul(x, g, w, tm):
    m, d = x.shape
    n = w.shape[1]
    return pl.pallas_call(
        _norm_matmul_kernel,
        grid=(m // tm,),
        in_specs=[pl.BlockSpec((tm, d), lambda i: (i, 0)),
                  pl.BlockSpec((1, d), lambda i: (0, 0)),
                  pl.BlockSpec((d, n), lambda i: (0, 0))],
        out_specs=pl.BlockSpec((tm, n), lambda i: (i, 0)),
        out_shape=jax.ShapeDtypeStruct((m, n), F32),
        compiler_params=_cparams("parallel"),
        name="norm_matmul",
    )(x, g.reshape(1, d), w.astype(BF16))


def _nsa_prep_kernel(p_ref, qg_ref, ksg_ref, kwg_ref, bdq_ref, bdk_ref, place_ref, feat_ref,
                     qt_o, ks_o, vst_o, kw_o, vwt_o, gate_o):
    p = p_ref[...]
    tm = p.shape[0]
    nq = NSA_HEADS * HEAD_DIM
    gw = NSA_GROUPS * HEAD_DIM
    q = p[:, :nq]
    msq = _dot_x_sel(q * q, bdq_ref[...])
    qn = q * lax.rsqrt(msq + RMS_EPS) * qg_ref[...] * (HEAD_DIM ** -0.5)
    qt_o[0] = qn.T.reshape(NSA_HEADS, HEAD_DIM, tm).astype(BF16)

    def seg(k):
        return p[:, nq + k * gw: nq + (k + 1) * gw]

    def head_norm(t, g):
        ms = _dot_x_sel(t * t, bdk_ref[...])
        return t * lax.rsqrt(ms + RMS_EPS) * g

    feat = feat_ref[...].astype(F32)
    ks = head_norm(seg(2), ksg_ref[...]).astype(BF16)
    kw = head_norm(seg(4), kwg_ref[...]).astype(BF16)
    vs_t = seg(3).T
    vw_t = seg(5).T
    for g in range(NSA_GROUPS):
        ks_o[0, g] = (_dot(ks, place_ref[g]) + feat).astype(BF16)
        kw_o[0, g] = (_dot(kw, place_ref[g])[:, :2 * HEAD_DIM] + feat[:, :2 * HEAD_DIM]).astype(BF16)
        for j in range(tm // Q_BLOCK):
            tile = (slice(g * HEAD_DIM, (g + 1) * HEAD_DIM), slice(j * Q_BLOCK, (j + 1) * Q_BLOCK))
            vst_o[0, g, j] = vs_t[tile].astype(BF16)
            vwt_o[0, g, j] = vw_t[tile].astype(BF16)
    gate_o[...] = jax.nn.sigmoid(p[:, nq + 6 * gw: nq + 6 * gw + LANES])


def nsa_prep(p_nsa, q_g, ks_g, kw_g, b, s, tm):
    t = p_nsa.shape[0]
    nq = NSA_HEADS * HEAD_DIM
    gw = NSA_GROUPS * HEAD_DIM
    grp, dh = NSA_GROUPS, HEAD_DIM
    tps = s // tm
    kx = 2 * dh + LANES
    assert s // SLC_BLOCK <= LANES
    tok = np.arange(s)
    feat = np.zeros((s, kx), np.float32)
    feat[:, dh] = tok // SLC_BLOCK * SLC_BLOCK
    feat[:, dh + 1] = tok % SLC_BLOCK
    feat[:, dh + 2:dh + 4] = 1.0
    feat[tok, 2 * dh + tok // SLC_BLOCK] = 1.0
    place = np.zeros((grp, gw, kx), np.float32)
    for g in range(grp):
        place[g, g * dh + np.arange(dh), np.arange(dh)] = 1.0
    tile = lambda v, n: jnp.tile(v.reshape(1, HEAD_DIM), (1, n))
    row = lambda w: pl.BlockSpec((tm, w), lambda i: (i, 0))
    full = lambda *sh: pl.BlockSpec(sh, lambda i: (0,) * len(sh))
    seq = lambda *sh: pl.BlockSpec((1, grp) + sh, lambda i: (i // tps, 0, i % tps) + (0,) * (len(sh) - 1))
    return pl.pallas_call(
        _nsa_prep_kernel,
        grid=(t // tm,),
        in_specs=[row(NSA_PROJ_PAD), full(1, nq), full(1, gw), full(1, gw), full(nq, nq), full(gw, gw),
                  full(grp, gw, kx), pl.BlockSpec((tm, kx), lambda i: (i % tps, 0))],
        out_specs=[pl.BlockSpec((1, NSA_HEADS, dh, tm), lambda i: (i // tps, 0, 0, i % tps)),
                   seq(tm, kx), seq(tm // Q_BLOCK, dh, Q_BLOCK), seq(tm, 2 * dh), seq(tm // Q_BLOCK, dh, Q_BLOCK),
                   row(LANES)],
        out_shape=[jax.ShapeDtypeStruct((b, NSA_HEADS, dh, s), BF16),
                   jax.ShapeDtypeStruct((b, grp, s, kx), BF16),
                   jax.ShapeDtypeStruct((b, grp, s // Q_BLOCK, dh, Q_BLOCK), BF16),
                   jax.ShapeDtypeStruct((b, grp, s, 2 * dh), BF16),
                   jax.ShapeDtypeStruct((b, grp, s // Q_BLOCK, dh, Q_BLOCK), BF16),
                   jax.ShapeDtypeStruct((t, LANES), F32)],
        compiler_params=_cparams("parallel"),
        name="nsa_prep",
    )(p_nsa, tile(q_g, NSA_HEADS), tile(ks_g, NSA_GROUPS), tile(kw_g, NSA_GROUPS),
      _block_diag_ones(nq, HEAD_DIM, 1.0 / HEAD_DIM), _block_diag_ones(gw, HEAD_DIM, 1.0 / HEAD_DIM),
      jnp.asarray(place, BF16), jnp.asarray(feat, BF16))


def _compress_kernel(ch_ref, pe_ref, w1_ref, w2_ref, g_ref, o_ref):
    ch = ch_ref[0, 0, 0]
    nc = ch.shape[0]
    half = CMP_STRIDE * HEAD_DIM
    nxt = pltpu.roll(ch, nc - 1, 0)
    w1 = w1_ref[0]
    h1 = (_dot(ch, w1[:half], precision=HIGHEST) + _dot(nxt, w1[half:], precision=HIGHEST)
          + _dot(pe_ref[0], w1, precision=HIGHEST))
    out = _dot(jax.nn.silu(h1), w2_ref[0], precision=HIGHEST)
    out = jnp.where(pl.program_id(0) == 0, _rms(out, g_ref[...]), out)
    o_ref[0, 0, 0] = out.astype(BF16)


def nsa_compress(kc, vc, pe_k, pe_v, kw1, kw2, vw1, vw2, kc_g, b, s):
    nc = s // CMP_STRIDE
    half = CMP_STRIDE * HEAD_DIM

    def chunks(t):
        return t.reshape(b, nc, CMP_STRIDE, NSA_GROUPS, HEAD_DIM).transpose(0, 3, 1, 2, 4).reshape(
            b, NSA_GROUPS, nc, half)

    ch = jnp.stack([chunks(kc), chunks(vc)])
    pe = jnp.stack([pe_k.reshape(1, 2 * half), pe_v.reshape(1, 2 * half)])
    return pl.pallas_call(
        _compress_kernel,
        grid=(2, b, NSA_GROUPS),
        in_specs=[pl.BlockSpec((1, 1, 1, nc, half), lambda kv, bi, g: (kv, bi, g, 0, 0)),
                  pl.BlockSpec((1, 1, 2 * half), lambda kv, bi, g: (kv, 0, 0)),
                  pl.BlockSpec((1, 2 * half, HEAD_DIM), lambda kv, bi, g: (kv, 0, 0)),
                  pl.BlockSpec((1, HEAD_DIM, HEAD_DIM), lambda kv, bi, g: (kv, 0, 0)),
                  pl.BlockSpec((1, HEAD_DIM), lambda kv, bi, g: (0, 0))],
        out_specs=pl.BlockSpec((1, 1, 1, nc, HEAD_DIM), lambda kv, bi, g: (kv, bi, g, 0, 0)),
        out_shape=jax.ShapeDtypeStruct((2, b, NSA_GROUPS, nc, HEAD_DIM), BF16),
        compiler_params=_cparams("parallel", "parallel", "parallel"),
        name="nsa_compress",
    )(ch, pe, jnp.stack([kw1, vw1]), jnp.stack([kw2, vw2]), kc_g.reshape(1, HEAD_DIM))


def _masked_exp_cols(s, mask):
    sm = jnp.where(mask, s, NEG_INF)
    m = jnp.max(sm, axis=0, keepdims=True)
    p = jnp.exp(sm - jnp.where(m > 0.5 * NEG_INF, m, 0.0))
    l = jnp.sum(p, axis=0, keepdims=True)
    return p, 1.0 / jnp.where(l > 0.0, l, 1.0)


def _nsa_attn_kernel(qt_ref, kc_ref, vct_ref, ovt_ref, ks_ref, vst_ref, kw_ref, vwt_ref, gate_ref, o_ref,
                     *, n_sel, n_top):
    g = pl.program_id(1)
    t0 = pl.program_id(2) * Q_BLOCK
    ks_ref, vst_ref, kw_ref, vwt_ref = (r.at[0, 0] for r in (ks_ref, vst_ref, kw_ref, vwt_ref))
    cols = NSA_HPG * Q_BLOCK
    col = lax.broadcasted_iota(jnp.int32, (1, cols), 1)
    tq = t0 + col % Q_BLOCK
    head = g * NSA_HPG + col // Q_BLOCK
    slope = lax.bitcast_convert_type((127 - (head + 1)) << 23, F32)
    qt = jnp.concatenate([qt_ref[0, h] for h in range(NSA_HPG)], axis=1)
    frow = lax.broadcasted_iota(jnp.int32, (HEAD_DIM, cols), 0)
    tq_hi = (tq // SLC_BLOCK * SLC_BLOCK).astype(F32)
    tq_lo = (tq % SLC_BLOCK).astype(F32)
    qpos = jnp.where(frow < 2, slope, jnp.where(frow == 2, -slope * tq_hi, jnp.where(frow == 3, -slope * tq_lo, 0.0)))
    qc = jnp.concatenate([qt, qpos.astype(BF16)], axis=0)

    def heads_sum(x):
        acc = x[:, :Q_BLOCK]
        for h in range(1, NSA_HPG):
            acc = acc + x[:, h * Q_BLOCK:(h + 1) * Q_BLOCK]
        return acc

    kc = kc_ref[0, 0]
    nc = kc.shape[0]
    c_last = lax.broadcasted_iota(jnp.int32, (nc, 1), 0) * CMP_STRIDE + (CMP_BLOCK - 1)
    p_c, inv_c = _masked_exp_cols(_dot(kc, qc), c_last <= tq)
    p_c = p_c * inv_c
    o_c = _dot(vct_ref[0, 0], p_c.astype(BF16))
    imp = _dot_sel_x(ovt_ref[...], heads_sum(p_c))

    bid = lax.broadcasted_iota(jnp.int32, (LANES, Q_BLOCK), 0)
    bidf = bid.astype(F32)
    tq1 = t0 + lax.broadcasted_iota(jnp.int32, (1, Q_BLOCK), 1)
    cur = tq1 // SLC_BLOCK
    forced = (bid == 0) | (bid == cur) | (bid == cur - 1)
    score = jnp.where(forced, BIG, jnp.where(bid * SLC_BLOCK <= tq1, imp, -BIG))
    score = jnp.where(bid < n_sel, score, REMOVED)
    picked = jnp.zeros((LANES, Q_BLOCK), jnp.bool_)
    for _ in range(n_top):
        m = jnp.max(score, axis=0, keepdims=True)
        pick = bidf == jnp.min(jnp.where(score == m, bidf, float(LANES)), axis=0, keepdims=True)
        picked = picked | pick
        score = jnp.where(pick, REMOVED, score)
    sel_neg = jnp.where(picked, 0.0, NEG_INF).astype(BF16)
    qx = jnp.concatenate([qc, jnp.concatenate([sel_neg] * NSA_HPG, axis=1)], axis=0)

    def values_t(ref, first_tile, n_tiles):
        return jnp.concatenate([ref[first_tile + i] for i in range(n_tiles)], axis=1)

    span = WINDOW + Q_BLOCK
    ws = pl.multiple_of(jnp.maximum(t0 - WINDOW, 0), Q_BLOCK)
    dw = tq - (ws + lax.broadcasted_iota(jnp.int32, (span, 1), 0))
    in_window = dw.astype(jnp.uint32) < WINDOW
    p_w, inv_w = _masked_exp_cols(_dot(kw_ref[pl.ds(ws, span), :], qc), in_window)
    o_w = _dot(values_t(vwt_ref, ws // Q_BLOCK, span // Q_BLOCK), p_w.astype(BF16)) * inv_w

    krow = lax.broadcasted_iota(jnp.int32, (SEL_CHUNK, 1), 0)

    def flash(s, start, carry):
        m, l, acc = carry
        m_new = jnp.maximum(m, jnp.max(s, axis=0, keepdims=True))
        alpha = jnp.exp(m - m_new)
        p = jnp.exp(s - m_new)
        l = alpha * l + jnp.sum(p, axis=0, keepdims=True)
        v = values_t(vst_ref, start // Q_BLOCK, SEL_CHUNK // Q_BLOCK)
        return m_new, l, alpha * acc + _dot(v, p.astype(BF16))

    bpc = SEL_CHUNK // SLC_BLOCK
    blk_any = jnp.max(jnp.where(picked, 1.0, 0.0), axis=1, keepdims=True)
    chunk_bit = lax.bitcast_convert_type((bid[:, :1] // bpc + 127) << 23, F32)
    bits = jnp.max((blk_any * chunk_bit).reshape(LANES // bpc, bpc, 1), axis=1)
    active = jnp.sum(bits, axis=0, keepdims=True)[0, 0].astype(jnp.int32)

    def full_step(j, carry):
        start = pl.multiple_of(j * SEL_CHUNK, SEL_CHUNK)
        return lax.cond((active >> j) & 1 == 1,
                        lambda c: flash(_dot(ks_ref[pl.ds(start, SEL_CHUNK), :], qx), start, c),
                        lambda c: c, carry)

    n_full = t0 // SEL_CHUNK
    init = (jnp.full((1, cols), NEG_INF, F32), jnp.zeros((1, cols), F32), jnp.zeros((HEAD_DIM, cols), F32))
    carry = lax.fori_loop(0, n_full, full_step, init)
    start = pl.multiple_of(n_full * SEL_CHUNK, SEL_CHUNK)
    s_diag = jnp.where(start + krow <= tq, _dot(ks_ref[pl.ds(start, SEL_CHUNK), :], qx), NEG_INF)
    _, l_s, acc_s = flash(s_diag, start, carry)
    o_s = acc_s / l_s

    for h in range(NSA_HPG):
        hs = slice(h * Q_BLOCK, (h + 1) * Q_BLOCK)
        gate = lambda br: gate_ref[0, 0, br, h:h + 1, :]
        o_ref[0, h] = gate(0) * o_c[:, hs] + gate(1) * o_s[:, hs] + gate(2) * o_w[:, hs]


def nsa_attention(qt, kcv, ks_x, vs_t, kw_x, vw_t, gates, b, s):
    nq_blocks = s // Q_BLOCK
    nc = s // CMP_STRIDE
    n_cmp = nc - CMP_BLOCK // CMP_STRIDE + 1
    n_sel = s // SLC_BLOCK
    n_top = min(SLC_TOPK, n_sel)
    grp, dh = NSA_GROUPS, HEAD_DIM
    assert n_sel <= LANES and s % SEL_CHUNK == 0 and s >= WINDOW + Q_BLOCK
    c_start = np.arange(n_cmp) * CMP_STRIDE
    s_start = np.arange(n_sel) * SLC_BLOCK
    ovt = np.zeros((LANES, nc), np.float32)
    ovt[:n_sel, :n_cmp] = (np.clip(np.minimum((c_start + CMP_BLOCK)[:, None], s_start[None] + SLC_BLOCK)
                                   - np.maximum(c_start[:, None], s_start[None]), 0, None) / CMP_BLOCK).T

    c_pos = np.arange(nc) * CMP_STRIDE + (CMP_BLOCK - 1)
    c_feat = np.zeros((nc, dh), np.float32)
    c_feat[:, 0] = c_pos // SLC_BLOCK * SLC_BLOCK
    c_feat[:, 1] = c_pos % SLC_BLOCK
    c_feat[:, 2:4] = 1.0
    kc_x = jnp.concatenate([kcv[0], jnp.broadcast_to(jnp.asarray(c_feat, BF16), kcv[0].shape)], axis=-1)
    vct = kcv[1].transpose(0, 1, 3, 2)
    gates_t = gates[:, :NSA_HEADS * 3].reshape(b, s, grp, NSA_HPG, 3).transpose(0, 2, 4, 3, 1)

    grp_spec = lambda *shape: pl.BlockSpec((1, 1) + shape, lambda bi, g, i: (bi, g) + (0,) * len(shape),
                                           pipeline_mode=pl.Buffered(1))
    return pl.pallas_call(
        functools.partial(_nsa_attn_kernel, n_sel=n_sel, n_top=n_top),
        grid=(b, grp, nq_blocks),
        in_specs=[pl.BlockSpec((1, NSA_HPG, dh, Q_BLOCK), lambda bi, g, i: (bi, g, 0, i)),
                  grp_spec(nc, 2 * dh), grp_spec(dh, nc),
                  pl.BlockSpec((LANES, nc), lambda bi, g, i: (0, 0)),
                  grp_spec(s, 2 * dh + LANES), grp_spec(s // Q_BLOCK, dh, Q_BLOCK),
                  grp_spec(s, 2 * dh), grp_spec(s // Q_BLOCK, dh, Q_BLOCK),
                  pl.BlockSpec((1, 1, 3, NSA_HPG, Q_BLOCK), lambda bi, g, i: (bi, g, 0, 0, i))],
        out_specs=pl.BlockSpec((1, NSA_HPG, dh, Q_BLOCK), lambda bi, g, i: (bi, g, 0, i)),
        out_shape=jax.ShapeDtypeStruct((b, NSA_HEADS, dh, s), F32),
        compiler_params=_cparams("parallel", "parallel", "arbitrary"),
        name="nsa_attention",
    )(qt, kc_x, vct, jnp.asarray(ovt, BF16), ks_x, vs_t, kw_x, vw_t, gates_t)


def _rwkv_prep_kernel(p_ref, prev_ref, mu_ref, w0_ref, a0_ref, kk_ref, ka_ref, rk_ref,
                      wup_ref, aup_ref, gup_ref, bd_ref, ltri_ref, lones_ref, csum_ref,
                      at_o, bt_o, kt_o, rt_o, v_o, bw_o, kw_o, wc_o, g_o, bonus_o, *, tiles_per_seq):
    p = p_ref[...]
    tm = p.shape[0]
    first = pl.program_id(0) % tiles_per_seq == 0
    last_prev = jnp.where(first, 0.0, prev_ref[7:8, :])
    prev = pltpu.roll(p, 1, 0)
    prev = jnp.where(lax.broadcasted_iota(jnp.int32, (tm, 1), 0) == 0, last_prev, prev)
    pm = p + (prev - p) * mu_ref[...]
    d = RWKV_DIM
    r, k, v = pm[:, :d], pm[:, d:2 * d], pm[:, 2 * d:3 * d]
    lora = pm[:, 3 * d:3 * d + LANES]
    gd = pm[:, 3 * d + LANES:3 * d + 2 * LANES]
    z = -(w0_ref[...] + _dot(jnp.tanh(lora).astype(BF16), wup_ref[...]))
    softplus = jnp.maximum(z, 0.0) + jnp.log(1.0 + jnp.exp(-jnp.abs(z)))
    w = -softplus - 0.5
    a = jax.nn.sigmoid(a0_ref[...] + _dot(lora.astype(BF16), aup_ref[...]))
    g_o[...] = _dot(jax.nn.sigmoid(gd).astype(BF16), gup_ref[...])
    bd = bd_ref[...]
    kkr = k * kk_ref[...]
    kk = kkr / jnp.maximum(jnp.sqrt(_dot_x_sel(kkr * kkr, bd)), 1e-12)
    k2 = k * (1.0 + (a - 1.0) * ka_ref[...])
    bonus_o[...] = _dot_x_sel(r * k2 * rk_ref[...], bd) * v
    lw = -jnp.exp(w)
    lw_parts = _split_bf16(lw)
    cum = sum(_dot(ltri_ref[...], p) for p in lw_parts)
    tot = sum(_dot(lones_ref[...], p) for p in lw_parts)
    e_in = jnp.exp(cum)
    e_out = jnp.exp(-cum)
    e_end = jnp.exp(tot - cum)

    def put_heads(o, val):
        for h in range(RWKV_HEADS):
            o[0, h] = val[:, h * HEAD_DIM:(h + 1) * HEAD_DIM].astype(o.dtype)

    put_heads(at_o, -kk * jnp.exp(cum - lw))
    put_heads(bt_o, kk * a * e_out)
    put_heads(kt_o, k2 * e_out)
    put_heads(rt_o, r * e_in)
    put_heads(v_o, v)
    put_heads(bw_o, kk * a * e_end)
    put_heads(kw_o, k2 * e_end)
    put_heads(wc_o, jnp.exp(sum(_dot(csum_ref[...], p) for p in lw_parts)))


def rwkv_prep(p_rwkv, mu, w0, w_up, a0, a_up, g_up, k_k, k_a, r_k, b, s, tm):
    t = p_rwkv.shape[0]
    d = RWKV_DIM
    c = RWKV_CHUNK
    tps = s // tm
    cpt = tm // c
    wup = jnp.concatenate([w_up, jnp.zeros_like(a_up)], axis=0).astype(BF16)
    aup = jnp.concatenate([jnp.zeros_like(w_up), a_up], axis=0).astype(BF16)
    i = np.arange(tm)
    same = (i[:, None] // c) == (i[None, :] // c)
    ltri = jnp.asarray(same & (i[:, None] >= i[None, :]), BF16)
    lones = jnp.asarray(same, BF16)
    csum = jnp.asarray(np.arange(cpt)[:, None] == (i[None, :] // c), BF16)
    row = lambda w: pl.BlockSpec((tm, w), lambda i: (i, 0))
    full = lambda *sh: pl.BlockSpec(sh, lambda i: (0,) * len(sh))
    heads = lambda n: pl.BlockSpec((1, RWKV_HEADS, n, HEAD_DIM), lambda i: (i // tps, 0, i % tps, 0))
    hshape = lambda n, dt: jax.ShapeDtypeStruct((b, RWKV_HEADS, n, HEAD_DIM), dt)
    vec = lambda x: x.reshape(1, -1)
    return pl.pallas_call(
        functools.partial(_rwkv_prep_kernel, tiles_per_seq=tps),
        grid=(t // tm,),
        in_specs=[row(RWKV_PROJ),
                  pl.BlockSpec((8, RWKV_PROJ), lambda i: (jnp.maximum(i * (tm // 8) - 1, 0), 0)),
                  full(1, RWKV_PROJ), full(1, d), full(1, d), full(1, d), full(1, d), full(1, d),
                  full(LANES, d), full(LANES, d), full(LANES, d), full(d, d), full(tm, tm), full(tm, tm),
                  full(cpt, tm)],
        out_specs=[heads(tm)] * 7 + [heads(cpt), row(d), row(d)],
        out_shape=[hshape(s, BF16)] * 7 + [hshape(s // c, F32)] + [jax.ShapeDtypeStruct((t, d), F32)] * 2,
        compiler_params=_cparams("parallel"),
        name="rwkv_prep",
    )(p_rwkv, p_rwkv, vec(mu), vec(w0), vec(a0), vec(k_k), vec(k_a), vec(r_k), wup, aup, g_up.astype(BF16),
      _block_diag_ones(d, HEAD_DIM), ltri, lones, csum)


def _bdot(a, b):
    return lax.dot_general(a, b, (((2,), (1,)), ((0,), (0,))), preferred_element_type=F32)


def _bdot_nt(a, b):
    return lax.dot_general(a, b, (((2,), (2,)), ((0,), (0,))), preferred_element_type=F32)


def _bdot_tn(a, b):
    return lax.dot_general(a, b, (((1,), (1,)), ((0,), (0,))), preferred_element_type=F32)


def _rwkv_intra_kernel(at_ref, bt_ref, kt_ref, rt_ref, v_ref, ta_o, tr_o, arb_o, yv_o):
    c = RWKV_CHUNK
    _, nh, ts, dh = at_ref.shape
    n = nh * (ts // c)
    chunked = lambda ref: ref[0].reshape(n, c, dh)
    at, bt, kt, rt, v = (chunked(r) for r in (at_ref, bt_ref, kt_ref, rt_ref, v_ref))
    ri = lax.broadcasted_iota(jnp.int32, (1, c, c), 1)
    ci = lax.broadcasted_iota(jnp.int32, (1, c, c), 2)
    strict = ri > ci
    incl = ri >= ci
    ar = jnp.concatenate([at, rt], axis=1)
    xb = _bdot_nt(ar, bt)
    xk = _bdot_nt(ar, kt)
    l_ab = jnp.where(strict, xb[:, :c], 0.0)
    a_ak = jnp.where(strict, xk[:, :c], 0.0)
    a_rb = jnp.where(incl, xb[:, c:], 0.0)
    a_rk = jnp.where(incl, xk[:, c:], 0.0)
    pw = l_ab
    tinv = jnp.where(ri == ci, 1.0, 0.0) + l_ab
    for _ in range(int(np.log2(c)) - 1):
        pw_b = pw.astype(BF16)
        pw = _bdot(pw_b, pw_b)
        tinv = tinv + _bdot(tinv.astype(BF16), pw.astype(BF16))
    tinv_b = tinv.astype(BF16)

    def put(o, val):
        o[0] = val.reshape(nh, ts, val.shape[-1]).astype(o.dtype)

    put(ta_o, _bdot(tinv_b, at))
    put(tr_o, _bdot(tinv_b, _bdot(a_ak.astype(BF16), v).astype(BF16)))
    put(arb_o, a_rb)
    put(yv_o, _bdot(a_rk.astype(BF16), v))


def rwkv_intra(at, bt, kt, rt, v, ts):
    b, h, s, dh = at.shape
    seq = lambda: pl.BlockSpec((1, h, ts, dh), lambda bi, i: (bi, 0, i, 0))
    shp = lambda dt: jax.ShapeDtypeStruct((b, h, s, dh), dt)
    return pl.pallas_call(
        _rwkv_intra_kernel,
        grid=(b, s // ts),
        in_specs=[seq()] * 5,
        out_specs=[seq()] * 4,
        out_shape=[shp(BF16), shp(F32), shp(BF16), shp(F32)],
        compiler_params=_cparams("parallel", "parallel"),
        name="rwkv_intra",
    )(at, bt, kt, rt, v)


def _rwkv_scan_kernel(ta_ref, tr_ref, arb_ref, yv_ref, rt_ref, v_ref, bw_ref, kw_ref, wc_ref, y_ref, st_ref):
    c = RWKV_CHUNK
    nb, nh, ts, dh = ta_ref.shape
    n = nb * nh

    @pl.when(pl.program_id(0) == 0)
    def _():
        st_ref[...] = jnp.zeros_like(st_ref)

    def chunk_step(j, _):
        sl = (slice(None), slice(None), pl.ds(pl.multiple_of(j * c, c), c), slice(None))
        get = lambda ref: ref[sl].reshape(n, c, dh)
        st = st_ref[...]
        st_b = st.astype(BF16)
        u = _bdot_nt(get(ta_ref), st_b) + get(tr_ref)
        u_b = u.astype(BF16)
        y = _bdot_nt(get(rt_ref), st_b) + _bdot(get(arb_ref), u_b) + get(yv_ref)
        wc = wc_ref[:, :, pl.ds(pl.program_id(0) * (ts // c) + j, 1), :].reshape(n, 1, dh)
        st_ref[...] = st * wc + _bdot_tn(jnp.concatenate([u_b, get(v_ref)], axis=1),
                                         jnp.concatenate([get(bw_ref), get(kw_ref)], axis=1))
        y_ref[sl] = y.reshape(nb, nh, c, dh)
        return 0

    lax.fori_loop(0, ts // c, chunk_step, 0)


def rwkv_scan(ta, tr, arb, yv, rt, v, bw, kw, wc, ts):
    b, h, s, dh = ta.shape
    seq = lambda n: pl.BlockSpec((b, h, n, dh), lambda i: (0, 0, i, 0))
    return pl.pallas_call(
        _rwkv_scan_kernel,
        grid=(s // ts,),
        in_specs=[seq(ts)] * 8 + [pl.BlockSpec(wc.shape, lambda i: (0, 0, 0, 0))],
        out_specs=seq(ts),
        out_shape=jax.ShapeDtypeStruct((b, h, s, dh), F32),
        scratch_shapes=[pltpu.VMEM((b * h, dh, dh), F32)],
        compiler_params=_cparams("arbitrary"),
        name="rwkv_scan",
    )(ta, tr, arb, yv, rt, v, bw, kw, wc)


def _out_proj_kernel(x_ref, on_ref, y_ref, bonus_ref, g_ref, lnw_ref, lnb_ref, bd_ref, wn_ref, wr_ref, o_ref):
    y = jnp.concatenate([y_ref[0, h] for h in range(RWKV_HEADS)], axis=-1)
    bd = bd_ref[...]
    yc = y - _dot_x_sel(y, bd)
    yn = yc * lax.rsqrt(_dot_x_sel(yc * yc, bd) + GN_EPS)
    o_rwkv = (yn * lnw_ref[...] + lnb_ref[...] + bonus_ref[...]) * g_ref[...]
    tm = y.shape[0]
    o_nsa_t = on_ref[0].reshape(NSA_HEADS * HEAD_DIM, tm)
    o_ref[...] = (x_ref[...] + _dot_tn(o_nsa_t.astype(BF16), wn_ref[...])
                  + _dot(o_rwkv.astype(BF16), wr_ref[...]))


def out_proj(x, o_nsa_t, y, bonus, g, ln_w, ln_b, w_out, s, tm):
    t, d = x.shape
    dn = o_nsa_t.shape[1] * o_nsa_t.shape[2]
    dr = bonus.shape[1]
    tps = s // tm
    row = lambda w: pl.BlockSpec((tm, w), lambda i: (i, 0))
    full = lambda *sh: pl.BlockSpec(sh, lambda i: (0,) * len(sh))
    return pl.pallas_call(
        _out_proj_kernel,
        grid=(t // tm,),
        in_specs=[row(d), pl.BlockSpec((1, NSA_HEADS, HEAD_DIM, tm), lambda i: (i // tps, 0, 0, i % tps)),
                  pl.BlockSpec((1, RWKV_HEADS, tm, HEAD_DIM), lambda i: (i // tps, 0, i % tps, 0)),
                  row(dr), row(dr), full(1, dr), full(1, dr), full(dr, dr), full(dn, d), full(dr, d)],
        out_specs=row(d),
        out_shape=jax.ShapeDtypeStruct((t, d), F32),
        compiler_params=_cparams("parallel"),
        name="out_proj",
    )(x, o_nsa_t, y, bonus, g, ln_w.reshape(1, dr), ln_b.reshape(1, dr),
      _block_diag_ones(dr, HEAD_DIM, 1.0 / HEAD_DIM), w_out[:dn].astype(BF16), w_out[dn:].astype(BF16))


def _cross_attn_kernel(h_ref, g_ref, wq_ref, qg_ref, kv_ref, kg_ref, wo_ref, o_ref):
    h = h_ref[...]
    d = h.shape[1]
    xd = d // X_HEADS
    q = _dot(_rms(h, g_ref[...]).astype(BF16), wq_ref[...])
    kv = kv_ref[0]
    outs = []
    for hd in range(X_HEADS):
        qh = _rms(q[:, hd * xd:(hd + 1) * xd], qg_ref[...]) * (xd ** -0.5)
        kh = _rms(kv[:, hd * xd:(hd + 1) * xd], kg_ref[...])
        vh = kv[:, d + hd * xd:d + (hd + 1) * xd]
        s = _dot_nt(qh.astype(BF16), kh.astype(BF16))
        p = jnp.exp(s - jnp.max(s, axis=-1, keepdims=True))
        p = p / jnp.sum(p, axis=-1, keepdims=True)
        outs.append(_dot(p.astype(BF16), vh.astype(BF16)))
    o = jnp.concatenate(outs, axis=-1)
    o_ref[...] = h + _dot(o.astype(BF16), wo_ref[...])


def cross_attention(h, kv, norm_g, xq_w, xq_g, xk_g, xo_w, b, s, tm):
    t, d = h.shape
    m = kv.shape[1]
    xd = d // X_HEADS
    tiles = s // tm
    full = lambda *sh: pl.BlockSpec(sh, lambda i: (0,) * len(sh))
    return pl.pallas_call(
        _cross_attn_kernel,
        grid=(t // tm,),
        in_specs=[pl.BlockSpec((tm, d), lambda i: (i, 0)), full(1, d), full(d, d), full(1, xd),
                  pl.BlockSpec((1, m, 2 * d), lambda i: (i // tiles, 0, 0)), full(1, xd), full(d, d)],
        out_specs=pl.BlockSpec((tm, d), lambda i: (i, 0)),
        out_shape=jax.ShapeDtypeStruct((t, d), F32),
        compiler_params=_cparams("parallel"),
        name="cross_attention",
    )(h, norm_g.reshape(1, d), xq_w.astype(BF16), xq_g.reshape(1, xd), kv, xk_g.reshape(1, xd),
      xo_w.astype(BF16))


def _router_kernel(h_ref, g_ref, rw_ref, rb_ref, ltri_ref, xn_o, idx_o, gate_o, rank_o, count_o, seen_ref):
    @pl.when(pl.program_id(0) == 0)
    def _():
        seen_ref[...] = jnp.zeros_like(seen_ref)

    xn = _rms(h_ref[...], g_ref[...])
    xn_o[...] = xn
    logits = _dot(xn, rw_ref[...], precision=HIGHEST) + rb_ref[...]
    tm = logits.shape[0]
    lane = lax.broadcasted_iota(jnp.int32, (tm, LANES), 1)
    lanef = lane.astype(F32)
    logits = jnp.where(lane < N_EXPERTS, logits, REMOVED)
    idx_acc = jnp.zeros((tm, LANES), F32)
    val_acc = jnp.zeros((tm, LANES), F32)
    chosen = jnp.zeros((tm, LANES), F32)
    picks = []
    top = None
    for k in range(TOP_K):
        m = jnp.max(logits, axis=-1, keepdims=True)
        idx = jnp.min(jnp.where(logits == m, lanef, float(LANES)), axis=-1, keepdims=True)
        pick = lanef == idx
        picks.append(pick)
        chosen = jnp.where(pick, 1.0, chosen)
        logits = jnp.where(pick, REMOVED, logits)
        top = m if top is None else top
        idx_acc = jnp.where(lane == k, idx, idx_acc)
        val_acc = jnp.where(lane == k, jnp.exp(m - top), val_acc)
    idx_o[...] = idx_acc.astype(jnp.int32)
    gate_o[...] = val_acc / jnp.sum(val_acc, axis=-1, keepdims=True)
    before = seen_ref[0:1, :] + _dot(ltri_ref[...], chosen.astype(BF16))
    rank_acc = jnp.zeros((tm, LANES), F32)
    for k, pick in enumerate(picks):
        rank_acc = jnp.where(lane == k, jnp.sum(jnp.where(pick, before, 0.0), axis=-1, keepdims=True), rank_acc)
    rank_o[...] = rank_acc.astype(jnp.int32)
    seen_ref[...] = seen_ref[...] + jnp.sum(chosen, axis=0, keepdims=True)
    count_o[...] = seen_ref[...]


def moe_router(h, norm_g, router_w, router_b, tm):
    t, d = h.shape
    rw = jnp.zeros((d, LANES), F32).at[:, :N_EXPERTS].set(router_w)
    rb = jnp.zeros((1, LANES), F32).at[0, :N_EXPERTS].set(router_b)
    i = np.arange(tm)
    ltri = jnp.asarray(i[:, None] > i[None, :], BF16)
    row = lambda w: pl.BlockSpec((tm, w), lambda i: (i, 0))
    full = lambda *s: pl.BlockSpec(s, lambda i: (0,) * len(s))
    return pl.pallas_call(
        _router_kernel,
        grid=(t // tm,),
        in_specs=[row(d), full(1, d), full(d, LANES), full(1, LANES), full(tm, tm)],
        out_specs=[row(d), row(LANES), row(LANES), row(LANES), full(8, LANES)],
        out_shape=[jax.ShapeDtypeStruct((t, d), F32), jax.ShapeDtypeStruct((t, LANES), jnp.int32),
                   jax.ShapeDtypeStruct((t, LANES), F32), jax.ShapeDtypeStruct((t, LANES), jnp.int32),
                   jax.ShapeDtypeStruct((8, LANES), F32)],
        scratch_shapes=[pltpu.VMEM((8, LANES), F32)],
        compiler_params=_cparams("arbitrary"),
        name="moe_router",
    )(h, norm_g.reshape(1, d), rw, rb, ltri)


def _expert_kernel(blk_e_ref, n_used_ref, x_ref, w1_ref, b1_ref, w2_ref, b2_ref, o_ref, w2x_ref):
    i = pl.program_id(0)
    f = w2_ref.shape[1]

    @pl.when(i == 0)
    def _():
        w2x_ref[...] = jnp.zeros_like(w2x_ref)

    @pl.when((i == 0) | (blk_e_ref[i] != blk_e_ref[jnp.maximum(i - 1, 0)]))
    def _():
        for c in range(w2x_ref.shape[0]):
            w2x_ref[c, pl.ds(0, f, stride=2), :] = w2_ref[0, :, c * LANES:(c + 1) * LANES]

    @pl.when(i < n_used_ref[0])
    def _():
        x = x_ref[...].astype(BF16)
        h = _dot(x, w1_ref[0].astype(BF16)) + b1_ref[0]
        hg = jnp.minimum(h, SWIGLU_LIMIT)
        gate = hg * jax.nn.sigmoid(SWIGLU_ALPHA * hg)
        lin = jnp.clip(h, -SWIGLU_LIMIT, SWIGLU_LIMIT) + 1.0
        even = lax.broadcasted_iota(jnp.int32, (1, LANES), 1) % 2 == 0
        parts = []
        for c in range(h.shape[1] // LANES):
            cols = slice(c * LANES, (c + 1) * LANES)
            nxt = pltpu.roll(lin[:, cols], LANES - 1, 1)
            parts.append(jnp.where(even, gate[:, cols] * nxt, 0.0).astype(BF16))
        act = jnp.concatenate(parts, axis=1)
        w2x = jnp.concatenate([w2x_ref[c] for c in range(w2x_ref.shape[0])], axis=1)
        o_ref[...] = _dot(act, w2x.astype(BF16)) + b2_ref[0]

    @pl.when(i >= n_used_ref[0])
    def _():
        o_ref[...] = jnp.zeros_like(o_ref)


def moe_experts(xs, blk_e, n_used, w1, b1, w2, b2):
    r, d = xs.shape
    f2 = w1.shape[2]
    m = MOE_ROW_BLOCK
    ex = lambda *s: pl.BlockSpec((1,) + s, lambda i, be, nu: (be[i],) + (0,) * len(s))
    grid_spec = pltpu.PrefetchScalarGridSpec(
        num_scalar_prefetch=2,
        grid=(r // m,),
        in_specs=[pl.BlockSpec((m, d), lambda i, be, nu: (i, 0)),
                  ex(d, f2), ex(1, f2), ex(f2 // 2, d), ex(1, d)],
        out_specs=pl.BlockSpec((m, d), lambda i, be, nu: (i, 0)),
        scratch_shapes=[pltpu.VMEM((d // LANES, f2, LANES), F32)],
    )
    return pl.pallas_call(
        _expert_kernel,
        grid_spec=grid_spec,
        out_shape=jax.ShapeDtypeStruct((r, d), F32),
        compiler_params=_cparams("arbitrary"),
        name="moe_experts",
    )(blk_e, n_used, xs, w1, b1, w2, b2)


def _combine_kernel(h_ref, y_ref, gate_ref, o_ref):
    d = h_ref.shape[1]
    acc = h_ref[...]
    for k in range(TOP_K):
        acc = acc + gate_ref[:, k:k + 1] * y_ref[:, k * d:(k + 1) * d]
    o_ref[...] = acc


def moe_combine(h, y4, gate, tm):
    t, d = h.shape
    row = lambda w: pl.BlockSpec((tm, w), lambda i: (i, 0))
    return pl.pallas_call(
        _combine_kernel,
        grid=(t // tm,),
        in_specs=[row(d), row(TOP_K * d), row(LANES)],
        out_specs=row(d),
        out_shape=jax.ShapeDtypeStruct((t, d), F32),
        compiler_params=_cparams("parallel"),
        name="moe_combine",
    )(h, y4, gate)


def _layer(x, mem, norm_mix_g, w_in, q_norm_g, k_cmp_norm_g, k_slc_norm_g, k_win_norm_g,
           cmp_pe_k, cmp_pe_v, cmp_k_w1, cmp_k_w2, cmp_v_w1, cmp_v_w2,
           rwkv_mu, rwkv_w0, rwkv_w_up, rwkv_a0, rwkv_a_up, rwkv_g_up, rwkv_k_k, rwkv_k_a,
           rwkv_r_k, rwkv_ln_w, rwkv_ln_b, w_out,
           norm_x_g, norm_mem_g, xq_w, xk_w, xv_w, xq_norm_g, xk_norm_g, xo_w,
           norm_ffn_g, router_w, router_b, mlp1_w, mlp1_b, mlp2_w, mlp2_b):
    b, s, d = x.shape
    t = b * s
    tm = 512
    xt = x.reshape(t, d)

    w_nsa = jnp.pad(w_in[:, :NSA_PROJ], ((0, 0), (0, NSA_PROJ_PAD - NSA_PROJ)))
    p_nsa = norm_matmul(xt, norm_mix_g, w_nsa, tm)
    p_rwkv = norm_matmul(xt, norm_mix_g, w_in[:, NSA_PROJ:], tm)

    nq = NSA_HEADS * HEAD_DIM
    gw = NSA_GROUPS * HEAD_DIM
    qn, ks, vs, kw, vw, gates = nsa_prep(p_nsa, q_norm_g, k_slc_norm_g, k_win_norm_g, b, s, tm)
    kcv = nsa_compress(p_nsa[:, nq:nq + gw], p_nsa[:, nq + gw:nq + 2 * gw], cmp_pe_k, cmp_pe_v,
                       cmp_k_w1, cmp_k_w2, cmp_v_w1, cmp_v_w2, k_cmp_norm_g, b, s)
    o_nsa = nsa_attention(qn, kcv, ks, vs, kw, vw, gates, b, s)

    at, bt, kt, rt, v, bw, kwd, wc, g_gate, bonus = rwkv_prep(
        p_rwkv, rwkv_mu, rwkv_w0, rwkv_w_up, rwkv_a0, rwkv_a_up, rwkv_g_up, rwkv_k_k, rwkv_k_a, rwkv_r_k, b, s, tm)
    ta, tr, arb, yv = rwkv_intra(at, bt, kt, rt, v, ts=256)
    y = rwkv_scan(ta, tr, arb, yv, rt, v, bw, kwd, wc, ts=256)

    h1 = out_proj(xt, o_nsa, y, bonus, g_gate, rwkv_ln_w, rwkv_ln_b, w_out, s, tm)
    m = mem.shape[1]
    kv = norm_matmul(mem.reshape(b * m, d), norm_mem_g, jnp.concatenate([xk_w, xv_w], axis=1), m)
    h2 = cross_attention(h1, kv.reshape(b, m, 2 * d), norm_x_g, xq_w, xq_norm_g, xk_norm_g, xo_w, b, s, tm)

    xn, top_i, gate, rank, seen = moe_router(h2, norm_ffn_g, router_w, router_b, tm)
    top_i = top_i[:, :TOP_K]
    a = t * TOP_K
    mb = MOE_ROW_BLOCK
    order = jnp.argsort(top_i.reshape(a), stable=True)
    counts = seen[0, :N_EXPERTS].astype(jnp.int32)
    starts = jnp.cumsum(counts) - counts
    padded = (counts + mb - 1) // mb * mb
    pends = jnp.cumsum(padded)
    pstarts = pends - padded
    pos = (pstarts[top_i] + rank[:, :TOP_K]).reshape(a)
    n_blocks = -(-a // mb) + N_EXPERTS
    r = n_blocks * mb
    blk_start = jnp.arange(n_blocks, dtype=jnp.int32) * mb
    blk_e = jnp.minimum(jnp.sum(pends[None, :] <= blk_start[:, None], axis=1), N_EXPERTS - 1).astype(jnp.int32)
    src_i = blk_start[:, None] + jnp.arange(mb, dtype=jnp.int32)[None, :] - (pstarts - starts)[blk_e][:, None]
    valid = src_i < (starts + counts)[blk_e][:, None]
    row_src = jnp.where(valid, order[jnp.minimum(src_i, a - 1)] // TOP_K, 0).astype(jnp.int32).reshape(r)
    n_used = (pends[-1] // mb).astype(jnp.int32).reshape(1)
    xs = xn.at[row_src].get(mode="promise_in_bounds")
    f2 = mlp1_w.shape[2]
    ys = moe_experts(xs, blk_e, n_used, mlp1_w, mlp1_b.reshape(N_EXPERTS, 1, f2), mlp2_w,
                     mlp2_b.reshape(N_EXPERTS, 1, d))
    y4 = ys.at[pos].get(mode="promise_in_bounds").reshape(t, TOP_K * d)
    out = moe_combine(h2, y4, gate, 256)
    return out.reshape(b, s, d)


def kernel(x, mem, norm_mix_g, w_in, q_norm_g, k_cmp_norm_g, k_slc_norm_g, k_win_norm_g, cmp_pe_k, cmp_pe_v, cmp_k_w1, cmp_k_w2, cmp_v_w1, cmp_v_w2, rwkv_mu, rwkv_w0, rwkv_w_up, rwkv_a0, rwkv_a_up, rwkv_g_up, rwkv_k_k, rwkv_k_a, rwkv_r_k, rwkv_ln_w, rwkv_ln_b, w_out, norm_x_g, norm_mem_g, xq_w, xk_w, xv_w, xq_norm_g, xk_norm_g, xo_w, norm_ffn_g, router_w, router_b, mlp1_w, mlp1_b, mlp2_w, mlp2_b):
    params = (norm_mix_g, w_in, q_norm_g, k_cmp_norm_g, k_slc_norm_g, k_win_norm_g, cmp_pe_k, cmp_pe_v,
              cmp_k_w1, cmp_k_w2, cmp_v_w1, cmp_v_w2, rwkv_mu, rwkv_w0, rwkv_w_up, rwkv_a0, rwkv_a_up,
              rwkv_g_up, rwkv_k_k, rwkv_k_a, rwkv_r_k, rwkv_ln_w, rwkv_ln_b, w_out, norm_x_g, norm_mem_g,
              xq_w, xk_w, xv_w, xq_norm_g, xk_norm_g, xo_w, norm_ffn_g, router_w, router_b,
              mlp1_w, mlp1_b, mlp2_w, mlp2_b)
    h = x
    for layer in range(norm_mix_g.shape[0]):
        h = _layer(h, mem, *[prm[layer] for prm in params])
    return h
```

```python
import functools

import numpy as np
import jax
import jax.numpy as jnp
from jax import lax
from jax.experimental import pallas as pl
from jax.experimental.pallas import tpu as pltpu

F32 = jnp.float32
BF16 = jnp.bfloat16
HIGHEST = lax.Precision.HIGHEST

V7X_VMEM_BYTES = 64 * 1024 * 1024
VMEM_LIMIT = V7X_VMEM_BYTES * 3 // 4

HEAD_DIM = 64
NSA_HEADS = 8
NSA_GROUPS = 2
NSA_HPG = NSA_HEADS // NSA_GROUPS
GROUP_W = NSA_HPG * HEAD_DIM
CMP_BLOCK = 32
CMP_STRIDE = 16
SLC_BLOCK = 64
SLC_TOPK = 16
WINDOW = 512
Q_BLOCK = 128
SEL_CHUNK = 512
RWKV_HEADS = 8
RWKV_DIM = RWKV_HEADS * HEAD_DIM
RWKV_CHUNK = 64
GN_EPS = HEAD_DIM * 1e-5
X_HEADS = 4
N_EXPERTS = 32
TOP_K = 4
SWIGLU_LIMIT = 7.0
SWIGLU_ALPHA = 1.702
MOE_ROW_BLOCK = 256
RMS_EPS = 1e-6
NEG_INF = -1e30
BIG = 1e9
REMOVED = -3e38
LANES = 128

NSA_PROJ = NSA_HEADS * HEAD_DIM + 6 * NSA_GROUPS * HEAD_DIM + NSA_HEADS * 3
NSA_PROJ_PAD = -(-NSA_PROJ // LANES) * LANES
RWKV_PROJ = 3 * RWKV_DIM + 64 + 64 + 128


def _cparams(*sem):
    return pltpu.CompilerParams(dimension_semantics=sem, vmem_limit_bytes=VMEM_LIMIT)


def _dot(a, b, **kw):
    return jnp.dot(a, b, preferred_element_type=F32, **kw)


def _dot_nt(a, b, **kw):
    return lax.dot_general(a, b, (((1,), (1,)), ((), ())), preferred_element_type=F32, **kw)


def _dot_tn(a, b, **kw):
    return lax.dot_general(a, b, (((0,), (0,)), ((), ())), preferred_element_type=F32, **kw)


def _rms(x, g):
    return x * lax.rsqrt(jnp.mean(x * x, axis=-1, keepdims=True) + RMS_EPS) * g


def _split_bf16(x, terms=3):
    parts = []
    for _ in range(terms):
        hi = x.astype(BF16)
        parts.append(hi)
        x = x - hi.astype(F32)
    return parts


def _dot_x_sel(x, sel):
    return sum(_dot(p, sel) for p in _split_bf16(x))


def _dot_sel_x(sel, x):
    return sum(_dot(sel, p) for p in _split_bf16(x))


def _block_diag_ones(n, blk, scale=1.0):
    i = np.arange(n)
    return jnp.asarray(((i[:, None] // blk) == (i[None, :] // blk)).astype(np.float32) * scale, BF16)


def _norm_matmul_kernel(x_ref, g_ref, w_ref, o_ref):
    xn = _rms(x_ref[...], g_ref[...]).astype(BF16)
    o_ref[...] = _dot(xn, w_ref[...])


def norm_matmul(x, g, w, tm):
    m, d = x.shape
    n = w.shape[1]
    return pl.pallas_call(
        _norm_matmul_kernel,
        grid=(m // tm,),
        in_specs=[pl.BlockSpec((tm, d), lambda i: (i, 0)),
                  pl.BlockSpec((1, d), lambda i: (0, 0)),
                  pl.BlockSpec((d, n), lambda i: (0, 0))],
        out_specs=pl.BlockSpec((tm, n), lambda i: (i, 0)),
        out_shape=jax.ShapeDtypeStruct((m, n), F32),
        compiler_params=_cparams("parallel"),
        name="norm_matmul",
    )(x, g.reshape(1, d), w.astype(BF16))


def _nsa_prep_kernel(p_ref, qg_ref, ksg_ref, kwg_ref, bdq_ref, bdk_ref, place_ref, feat_ref,
                     qt_o, ks_o, vst_o, kw_o, vwt_o, gate_o):
    p = p_ref[...]
    tm = p.shape[0]
    nq = NSA_HEADS * HEAD_DIM
    gw = NSA_GROUPS * HEAD_DIM
    q = p[:, :nq]
    msq = _dot_x_sel(q * q, bdq_ref[...])
    qn = q * lax.rsqrt(msq + RMS_EPS) * qg_ref[...] * (HEAD_DIM ** -0.5)
    qt_o[0] = qn.T.reshape(NSA_HEADS, HEAD_DIM, tm).astype(BF16)

    def seg(k):
        return p[:, nq + k * gw: nq + (k + 1) * gw]

    def head_norm(t, g):
        ms = _dot_x_sel(t * t, bdk_ref[...])
        return t * lax.rsqrt(ms + RMS_EPS) * g

    feat = feat_ref[...].astype(F32)
    ks = head_norm(seg(2), ksg_ref[...]).astype(BF16)
    kw = head_norm(seg(4), kwg_ref[...]).astype(BF16)
    vs_t = seg(3).T
    vw_t = seg(5).T
    for g in range(NSA_GROUPS):
        ks_o[0, g] = (_dot(ks, place_ref[g]) + feat).astype(BF16)
        kw_o[0, g] = (_dot(kw, place_ref[g])[:, :2 * HEAD_DIM] + feat[:, :2 * HEAD_DIM]).astype(BF16)
        for j in range(tm // Q_BLOCK):
            tile = (slice(g * HEAD_DIM, (g + 1) * HEAD_DIM), slice(j * Q_BLOCK, (j + 1) * Q_BLOCK))
            vst_o[0, g, j] = vs_t[tile].astype(BF16)
            vwt_o[0, g, j] = vw_t[tile].astype(BF16)
    gate_o[...] = jax.nn.sigmoid(p[:, nq + 6 * gw: nq + 6 * gw + LANES])


def nsa_prep(p_nsa, q_g, ks_g, kw_g, b, s, tm):
    t = p_nsa.shape[0]
    nq = NSA_HEADS * HEAD_DIM
    gw = NSA_GROUPS * HEAD_DIM
    grp, dh = NSA_GROUPS, HEAD_DIM
    tps = s // tm
    kx = 2 * dh + LANES
    assert s // SLC_BLOCK <= LANES
    tok = np.arange(s)
    feat = np.zeros((s, kx), np.float32)
    feat[:, dh] = tok // SLC_BLOCK * SLC_BLOCK
    feat[:, dh + 1] = tok % SLC_BLOCK
    feat[:, dh + 2:dh + 4] = 1.0
    feat[tok, 2 * dh + tok // SLC_BLOCK] = 1.0
    place = np.zeros((grp, gw, kx), np.float32)
    for g in range(grp):
        place[g, g * dh + np.arange(dh), np.arange(dh)] = 1.0
    tile = lambda v, n: jnp.tile(v.reshape(1, HEAD_DIM), (1, n))
    row = lambda w: pl.BlockSpec((tm, w), lambda i: (i, 0))
    full = lambda *sh: pl.BlockSpec(sh, lambda i: (0,) * len(sh))
    seq = lambda *sh: pl.BlockSpec((1, grp) + sh, lambda i: (i // tps, 0, i % tps) + (0,) * (len(sh) - 1))
    return pl.pallas_call(
        _nsa_prep_kernel,
        grid=(t // tm,),
        in_specs=[row(NSA_PROJ_PAD), full(1, nq), full(1, gw), full(1, gw), full(nq, nq), full(gw, gw),
                  full(grp, gw, kx), pl.BlockSpec((tm, kx), lambda i: (i % tps, 0))],
        out_specs=[pl.BlockSpec((1, NSA_HEADS, dh, tm), lambda i: (i // tps, 0, 0, i % tps)),
                   seq(tm, kx), seq(tm // Q_BLOCK, dh, Q_BLOCK), seq(tm, 2 * dh), seq(tm // Q_BLOCK, dh, Q_BLOCK),
                   row(LANES)],
        out_shape=[jax.ShapeDtypeStruct((b, NSA_HEADS, dh, s), BF16),
                   jax.ShapeDtypeStruct((b, grp, s, kx), BF16),
                   jax.ShapeDtypeStruct((b, grp, s // Q_BLOCK, dh, Q_BLOCK), BF16),
                   jax.ShapeDtypeStruct((b, grp, s, 2 * dh), BF16),
                   jax.ShapeDtypeStruct((b, grp, s // Q_BLOCK, dh, Q_BLOCK), BF16),
                   jax.ShapeDtypeStruct((t, LANES), F32)],
        compiler_params=_cparams("parallel"),
        name="nsa_prep",
    )(p_nsa, tile(q_g, NSA_HEADS), tile(ks_g, NSA_GROUPS), tile(kw_g, NSA_GROUPS),
      _block_diag_ones(nq, HEAD_DIM, 1.0 / HEAD_DIM), _block_diag_ones(gw, HEAD_DIM, 1.0 / HEAD_DIM),
      jnp.asarray(place, BF16), jnp.asarray(feat, BF16))


def _compress_kernel(ch_ref, pe_ref, w1_ref, w2_ref, g_ref, o_ref):
    ch = ch_ref[0, 0, 0]
    nc = ch.shape[0]
    half = CMP_STRIDE * HEAD_DIM
    nxt = pltpu.roll(ch, nc - 1, 0)
    w1 = w1_ref[0]
    h1 = (_dot(ch, w1[:half], precision=HIGHEST) + _dot(nxt, w1[half:], precision=HIGHEST)
          + _dot(pe_ref[0], w1, precision=HIGHEST))
    out = _dot(jax.nn.silu(h1), w2_ref[0], precision=HIGHEST)
    out = jnp.where(pl.program_id(0) == 0, _rms(out, g_ref[...]), out)
    o_ref[0, 0, 0] = out.astype(BF16)


def nsa_compress(kc, vc, pe_k, pe_v, kw1, kw2, vw1, vw2, kc_g, b, s):
    nc = s // CMP_STRIDE
    half = CMP_STRIDE * HEAD_DIM

    def chunks(t):
        return t.reshape(b, nc, CMP_STRIDE, NSA_GROUPS, HEAD_DIM).transpose(0, 3, 1, 2, 4).reshape(
            b, NSA_GROUPS, nc, half)

    ch = jnp.stack([chunks(kc), chunks(vc)])
    pe = jnp.stack([pe_k.reshape(1, 2 * half), pe_v.reshape(1, 2 * half)])
    return pl.pallas_call(
        _compress_kernel,
        grid=(2, b, NSA_GROUPS),
        in_specs=[pl.BlockSpec((1, 1, 1, nc, half), lambda kv, bi, g: (kv, bi, g, 0, 0)),
                  pl.BlockSpec((1, 1, 2 * half), lambda kv, bi, g: (kv, 0, 0)),
                  pl.BlockSpec((1, 2 * half, HEAD_DIM), lambda kv, bi, g: (kv, 0, 0)),
                  pl.BlockSpec((1, HEAD_DIM, HEAD_DIM), lambda kv, bi, g: (kv, 0, 0)),
                  pl.BlockSpec((1, HEAD_DIM), lambda kv, bi, g: (0, 0))],
        out_specs=pl.BlockSpec((1, 1, 1, nc, HEAD_DIM), lambda kv, bi, g: (kv, bi, g, 0, 0)),
        out_shape=jax.ShapeDtypeStruct((2, b, NSA_GROUPS, nc, HEAD_DIM), BF16),
        compiler_params=_cparams("parallel", "parallel", "parallel"),
        name="nsa_compress",
    )(ch, pe, jnp.stack([kw1, vw1]), jnp.stack([kw2, vw2]), kc_g.reshape(1, HEAD_DIM))


def _masked_exp_cols(s, mask):
    sm = jnp.where(mask, s, NEG_INF)
    m = jnp.max(sm, axis=0, keepdims=True)
    p = jnp.exp(sm - jnp.where(m > 0.5 * NEG_INF, m, 0.0))
    l = jnp.sum(p, axis=0, keepdims=True)
    return p, 1.0 / jnp.where(l > 0.0, l, 1.0)


def _nsa_attn_kernel(qt_ref, kc_ref, vct_ref, ovt_ref, ks_ref, vst_ref, kw_ref, vwt_ref, gate_ref, o_ref,
                     *, n_sel, n_top):
    g = pl.program_id(1)
    t0 = pl.program_id(2) * Q_BLOCK
    ks_ref, vst_ref, kw_ref, vwt_ref = (r.at[0, 0] for r in (ks_ref, vst_ref, kw_ref, vwt_ref))
    cols = NSA_HPG * Q_BLOCK
    col = lax.broadcasted_iota(jnp.int32, (1, cols), 1)
    tq = t0 + col % Q_BLOCK
    head = g * NSA_HPG + col // Q_BLOCK
    slope = lax.bitcast_convert_type((127 - (head + 1)) << 23, F32)
    qt = jnp.concatenate([qt_ref[0, h] for h in range(NSA_HPG)], axis=1)
    frow = lax.broadcasted_iota(jnp.int32, (HEAD_DIM, cols), 0)
    tq_hi = (tq // SLC_BLOCK * SLC_BLOCK).astype(F32)
    tq_lo = (tq % SLC_BLOCK).astype(F32)
    qpos = jnp.where(frow < 2, slope, jnp.where(frow == 2, -slope * tq_hi, jnp.where(frow == 3, -slope * tq_lo, 0.0)))
    qc = jnp.concatenate([qt, qpos.astype(BF16)], axis=0)

    def heads_sum(x):
        acc = x[:, :Q_BLOCK]
        for h in range(1, NSA_HPG):
            acc = acc + x[:, h * Q_BLOCK:(h + 1) * Q_BLOCK]
        return acc

    kc = kc_ref[0, 0]
    nc = kc.shape[0]
    c_last = lax.broadcasted_iota(jnp.int32, (nc, 1), 0) * CMP_STRIDE + (CMP_BLOCK - 1)
    p_c, inv_c = _masked_exp_cols(_dot(kc, qc), c_last <= tq)
    p_c = p_c * inv_c
    o_c = _dot(vct_ref[0, 0], p_c.astype(BF16))
    imp = _dot_sel_x(ovt_ref[...], heads_sum(p_c))

    bid = lax.broadcasted_iota(jnp.int32, (LANES, Q_BLOCK), 0)
    bidf = bid.astype(F32)
    tq1 = t0 + lax.broadcasted_iota(jnp.int32, (1, Q_BLOCK), 1)
    cur = tq1 // SLC_BLOCK
    forced = (bid == 0) | (bid == cur) | (bid == cur - 1)
    score = jnp.where(forced, BIG, jnp.where(bid * SLC_BLOCK <= tq1, imp, -BIG))
    score = jnp.where(bid < n_sel, score, REMOVED)
    picked = jnp.zeros((LANES, Q_BLOCK), jnp.bool_)
    for _ in range(n_top):
        m = jnp.max(score, axis=0, keepdims=True)
        pick = bidf == jnp.min(jnp.where(score == m, bidf, float(LANES)), axis=0, keepdims=True)
        picked = picked | pick
        score = jnp.where(pick, REMOVED, score)
    sel_neg = jnp.where(picked, 0.0, NEG_INF).astype(BF16)
    qx = jnp.concatenate([qc, jnp.concatenate([sel_neg] * NSA_HPG, axis=1)], axis=0)

    def values_t(ref, first_tile, n_tiles):
        return jnp.concatenate([ref[first_tile + i] for i in range(n_tiles)], axis=1)

    span = WINDOW + Q_BLOCK
    ws = pl.multiple_of(jnp.maximum(t0 - WINDOW, 0), Q_BLOCK)
    dw = tq - (ws + lax.broadcasted_iota(jnp.int32, (span, 1), 0))
    in_window = dw.astype(jnp.uint32) < WINDOW
    p_w, inv_w = _masked_exp_cols(_dot(kw_ref[pl.ds(ws, span), :], qc), in_window)
    o_w = _dot(values_t(vwt_ref, ws // Q_BLOCK, span // Q_BLOCK), p_w.astype(BF16)) * inv_w

    krow = lax.broadcasted_iota(jnp.int32, (SEL_CHUNK, 1), 0)

    def flash(s, start, carry):
        m, l, acc = carry
        m_new = jnp.maximum(m, jnp.max(s, axis=0, keepdims=True))
        alpha = jnp.exp(m - m_new)
        p = jnp.exp(s - m_new)
        l = alpha * l + jnp.sum(p, axis=0, keepdims=True)
        v = values_t(vst_ref, start // Q_BLOCK, SEL_CHUNK // Q_BLOCK)
        return m_new, l, alpha * acc + _dot(v, p.astype(BF16))

    bpc = SEL_CHUNK // SLC_BLOCK
    blk_any = jnp.max(jnp.where(picked, 1.0, 0.0), axis=1, keepdims=True)
    chunk_bit = lax.bitcast_convert_type((bid[:, :1] // bpc + 127) << 23, F32)
    bits = jnp.max((blk_any * chunk_bit).reshape(LANES // bpc, bpc, 1), axis=1)
    active = jnp.sum(bits, axis=0, keepdims=True)[0, 0].astype(jnp.int32)

    def full_step(j, carry):
        start = pl.multiple_of(j * SEL_CHUNK, SEL_CHUNK)
        return lax.cond((active >> j) & 1 == 1,
                        lambda c: flash(_dot(ks_ref[pl.ds(start, SEL_CHUNK), :], qx), start, c),
                        lambda c: c, carry)

    n_full = t0 // SEL_CHUNK
    init = (jnp.full((1, cols), NEG_INF, F32), jnp.zeros((1, cols), F32), jnp.zeros((HEAD_DIM, cols), F32))
    carry = lax.fori_loop(0, n_full, full_step, init)
    start = pl.multiple_of(n_full * SEL_CHUNK, SEL_CHUNK)
    s_diag = jnp.where(start + krow <= tq, _dot(ks_ref[pl.ds(start, SEL_CHUNK), :], qx), NEG_INF)
    _, l_s, acc_s = flash(s_diag, start, carry)
    o_s = acc_s / l_s

    for h in range(NSA_HPG):
        hs = slice(h * Q_BLOCK, (h + 1) * Q_BLOCK)
        gate = lambda br: gate_ref[0, 0, br, h:h + 1, :]
        o_ref[0, h] = gate(0) * o_c[:, hs] + gate(1) * o_s[:, hs] + gate(2) * o_w[:, hs]


def nsa_attention(qt, kcv, ks_x, vs_t, kw_x, vw_t, gates, b, s):
    nq_blocks = s // Q_BLOCK
    nc = s // CMP_STRIDE
    n_cmp = nc - CMP_BLOCK // CMP_STRIDE + 1
    n_sel = s // SLC_BLOCK
    n_top = min(SLC_TOPK, n_sel)
    grp, dh = NSA_GROUPS, HEAD_DIM
    assert n_sel <= LANES and s % SEL_CHUNK == 0 and s >= WINDOW + Q_BLOCK
    c_start = np.arange(n_cmp) * CMP_STRIDE
    s_start = np.arange(n_sel) * SLC_BLOCK
    ovt = np.zeros((LANES, nc), np.float32)
    ovt[:n_sel, :n_cmp] = (np.clip(np.minimum((c_start + CMP_BLOCK)[:, None], s_start[None] + SLC_BLOCK)
                                   - np.maximum(c_start[:, None], s_start[None]), 0, None) / CMP_BLOCK).T

    c_pos = np.arange(nc) * CMP_STRIDE + (CMP_BLOCK - 1)
    c_feat = np.zeros((nc, dh), np.float32)
    c_feat[:, 0] = c_pos // SLC_BLOCK * SLC_BLOCK
    c_feat[:, 1] = c_pos % SLC_BLOCK
    c_feat[:, 2:4] = 1.0
    kc_x = jnp.concatenate([kcv[0], jnp.broadcast_to(jnp.asarray(c_feat, BF16), kcv[0].shape)], axis=-1)
    vct = kcv[1].transpose(0, 1, 3, 2)
    gates_t = gates[:, :NSA_HEADS * 3].reshape(b, s, grp, NSA_HPG, 3).transpose(0, 2, 4, 3, 1)

    grp_spec = lambda *shape: pl.BlockSpec((1, 1) + shape, lambda bi, g, i: (bi, g) + (0,) * len(shape),
                                           pipeline_mode=pl.Buffered(1))
    return pl.pallas_call(
        functools.partial(_nsa_attn_kernel, n_sel=n_sel, n_top=n_top),
        grid=(b, grp, nq_blocks),
        in_specs=[pl.BlockSpec((1, NSA_HPG, dh, Q_BLOCK), lambda bi, g, i: (bi, g, 0, i)),
                  grp_spec(nc, 2 * dh), grp_spec(dh, nc),
                  pl.BlockSpec((LANES, nc), lambda bi, g, i: (0, 0)),
                  grp_spec(s, 2 * dh + LANES), grp_spec(s // Q_BLOCK, dh, Q_BLOCK),
                  grp_spec(s, 2 * dh), grp_spec(s // Q_BLOCK, dh, Q_BLOCK),
                  pl.BlockSpec((1, 1, 3, NSA_HPG, Q_BLOCK), lambda bi, g, i: (bi, g, 0, 0, i))],
        out_specs=pl.BlockSpec((1, NSA_HPG, dh, Q_BLOCK), lambda bi, g, i: (bi, g, 0, i)),
        out_shape=jax.ShapeDtypeStruct((b, NSA_HEADS, dh, s), F32),
        compiler_params=_cparams("parallel", "parallel", "arbitrary"),
        name="nsa_attention",
    )(qt, kc_x, vct, jnp.asarray(ovt, BF16), ks_x, vs_t, kw_x, vw_t, gates_t)


def _rwkv_prep_kernel(p_ref, prev_ref, mu_ref, w0_ref, a0_ref, kk_ref, ka_ref, rk_ref,
                      wup_ref, aup_ref, gup_ref, bd_ref, ltri_ref, lones_ref, csum_ref,
                      at_o, bt_o, kt_o, rt_o, v_o, bw_o, kw_o, wc_o, g_o, bonus_o, *, tiles_per_seq):
    p = p_ref[...]
    tm = p.shape[0]
    first = pl.program_id(0) % tiles_per_seq == 0
    last_prev = jnp.where(first, 0.0, prev_ref[7:8, :])
    prev = pltpu.roll(p, 1, 0)
    prev = jnp.where(lax.broadcasted_iota(jnp.int32, (tm, 1), 0) == 0, last_prev, prev)
    pm = p + (prev - p) * mu_ref[...]
    d = RWKV_DIM
    r, k, v = pm[:, :d], pm[:, d:2 * d], pm[:, 2 * d:3 * d]
    lora = pm[:, 3 * d:3 * d + LANES]
    gd = pm[:, 3 * d + LANES:3 * d + 2 * LANES]
    z = -(w0_ref[...] + _dot(jnp.tanh(lora).astype(BF16), wup_ref[...]))
    softplus = jnp.maximum(z, 0.0) + jnp.log(1.0 + jnp.exp(-jnp.abs(z)))
    w = -softplus - 0.5
    a = jax.nn.sigmoid(a0_ref[...] + _dot(lora.astype(BF16), aup_ref[...]))
    g_o[...] = _dot(jax.nn.sigmoid(gd).astype(BF16), gup_ref[...])
    bd = bd_ref[...]
    kkr = k * kk_ref[...]
    kk = kkr / jnp.maximum(jnp.sqrt(_dot_x_sel(kkr * kkr, bd)), 1e-12)
    k2 = k * (1.0 + (a - 1.0) * ka_ref[...])
    bonus_o[...] = _dot_x_sel(r * k2 * rk_ref[...], bd) * v
    lw = -jnp.exp(w)
    lw_parts = _split_bf16(lw)
    cum = sum(_dot(ltri_ref[...], p) for p in lw_parts)
    tot = sum(_dot(lones_ref[...], p) for p in lw_parts)
    e_in = jnp.exp(cum)
    e_out = jnp.exp(-cum)
    e_end = jnp.exp(tot - cum)

    def put_heads(o, val):
        for h in range(RWKV_HEADS):
            o[0, h] = val[:, h * HEAD_DIM:(h + 1) * HEAD_DIM].astype(o.dtype)

    put_heads(at_o, -kk * jnp.exp(cum - lw))
    put_heads(bt_o, kk * a * e_out)
    put_heads(kt_o, k2 * e_out)
    put_heads(rt_o, r * e_in)
    put_heads(v_o, v)
    put_heads(bw_o, kk * a * e_end)
    put_heads(kw_o, k2 * e_end)
    put_heads(wc_o, jnp.exp(sum(_dot(csum_ref[...], p) for p in lw_parts)))


def rwkv_prep(p_rwkv, mu, w0, w_up, a0, a_up, g_up, k_k, k_a, r_k, b, s, tm):
    t = p_rwkv.shape[0]
    d = RWKV_DIM
    c = RWKV_CHUNK
    tps = s // tm
    cpt = tm // c
    wup = jnp.concatenate([w_up, jnp.zeros_like(a_up)], axis=0).astype(BF16)
    aup = jnp.concatenate([jnp.zeros_like(w_up), a_up], axis=0).astype(BF16)
    i = np.arange(tm)
    same = (i[:, None] // c) == (i[None, :] // c)
    ltri = jnp.asarray(same & (i[:, None] >= i[None, :]), BF16)
    lones = jnp.asarray(same, BF16)
    csum = jnp.asarray(np.arange(cpt)[:, None] == (i[None, :] // c), BF16)
    row = lambda w: pl.BlockSpec((tm, w), lambda i: (i, 0))
    full = lambda *sh: pl.BlockSpec(sh, lambda i: (0,) * len(sh))
    heads = lambda n: pl.BlockSpec((1, RWKV_HEADS, n, HEAD_DIM), lambda i: (i // tps, 0, i % tps, 0))
    hshape = lambda n, dt: jax.ShapeDtypeStruct((b, RWKV_HEADS, n, HEAD_DIM), dt)
    vec = lambda x: x.reshape(1, -1)
    return pl.pallas_call(
        functools.partial(_rwkv_prep_kernel, tiles_per_seq=tps),
        grid=(t // tm,),
        in_specs=[row(RWKV_PROJ),
                  pl.BlockSpec((8, RWKV_PROJ), lambda i: (jnp.maximum(i * (tm // 8) - 1, 0), 0)),
                  full(1, RWKV_PROJ), full(1, d), full(1, d), full(1, d), full(1, d), full(1, d),
                  full(LANES, d), full(LANES, d), full(LANES, d), full(d, d), full(tm, tm), full(tm, tm),
                  full(cpt, tm)],
        out_specs=[heads(tm)] * 7 + [heads(cpt), row(d), row(d)],
        out_shape=[hshape(s, BF16)] * 7 + [hshape(s // c, F32)] + [jax.ShapeDtypeStruct((t, d), F32)] * 2,
        compiler_params=_cparams("parallel"),
        name="rwkv_prep",
    )(p_rwkv, p_rwkv, vec(mu), vec(w0), vec(a0), vec(k_k), vec(k_a), vec(r_k), wup, aup, g_up.astype(BF16),
      _block_diag_ones(d, HEAD_DIM), ltri, lones, csum)


def _bdot(a, b):
    return lax.dot_general(a, b, (((2,), (1,)), ((0,), (0,))), preferred_element_type=F32)


def _bdot_nt(a, b):
    return lax.dot_general(a, b, (((2,), (2,)), ((0,), (0,))), preferred_element_type=F32)


def _bdot_tn(a, b):
    return lax.dot_general(a, b, (((1,), (1,)), ((0,), (0,))), preferred_element_type=F32)


def _rwkv_intra_kernel(at_ref, bt_ref, kt_ref, rt_ref, v_ref, ta_o, tr_o, arb_o, yv_o):
    c = RWKV_CHUNK
    _, nh, ts, dh = at_ref.shape
    n = nh * (ts // c)
    chunked = lambda ref: ref[0].reshape(n, c, dh)
    at, bt, kt, rt, v = (chunked(r) for r in (at_ref, bt_ref, kt_ref, rt_ref, v_ref))
    ri = lax.broadcasted_iota(jnp.int32, (1, c, c), 1)
    ci = lax.broadcasted_iota(jnp.int32, (1, c, c), 2)
    strict = ri > ci
    incl = ri >= ci
    ar = jnp.concatenate([at, rt], axis=1)
    xb = _bdot_nt(ar, bt)
    xk = _bdot_nt(ar, kt)
    l_ab = jnp.where(strict, xb[:, :c], 0.0)
    a_ak = jnp.where(strict, xk[:, :c], 0.0)
    a_rb = jnp.where(incl, xb[:, c:], 0.0)
    a_rk = jnp.where(incl, xk[:, c:], 0.0)
    pw = l_ab
    tinv = jnp.where(ri == ci, 1.0, 0.0) + l_ab
    for _ in range(int(np.log2(c)) - 1):
        pw_b = pw.astype(BF16)
        pw = _bdot(pw_b, pw_b)
        tinv = tinv + _bdot(tinv.astype(BF16), pw.astype(BF16))
    tinv_b = tinv.astype(BF16)

    def put(o, val):
        o[0] = val.reshape(nh, ts, val.shape[-1]).astype(o.dtype)

    put(ta_o, _bdot(tinv_b, at))
    put(tr_o, _bdot(tinv_b, _bdot(a_ak.astype(BF16), v).astype(BF16)))
    put(arb_o, a_rb)
    put(yv_o, _bdot(a_rk.astype(BF16), v))


def rwkv_intra(at, bt, kt, rt, v, ts):
    b, h, s, dh = at.shape
    seq = lambda: pl.BlockSpec((1, h, ts, dh), lambda bi, i: (bi, 0, i, 0))
    shp = lambda dt: jax.ShapeDtypeStruct((b, h, s, dh), dt)
    return pl.pallas_call(
        _rwkv_intra_kernel,
        grid=(b, s // ts),
        in_specs=[seq()] * 5,
        out_specs=[seq()] * 4,
        out_shape=[shp(BF16), shp(F32), shp(BF16), shp(F32)],
        compiler_params=_cparams("parallel", "parallel"),
        name="rwkv_intra",
    )(at, bt, kt, rt, v)


def _rwkv_scan_kernel(ta_ref, tr_ref, arb_ref, yv_ref, rt_ref, v_ref, bw_ref, kw_ref, wc_ref, y_ref, st_ref):
    c = RWKV_CHUNK
    nb, nh, ts, dh = ta_ref.shape
    n = nb * nh

    @pl.when(pl.program_id(0) == 0)
    def _():
        st_ref[...] = jnp.zeros_like(st_ref)

    def chunk_step(j, _):
        sl = (slice(None), slice(None), pl.ds(pl.multiple_of(j * c, c), c), slice(None))
        get = lambda ref: ref[sl].reshape(n, c, dh)
        st = st_ref[...]
        st_b = st.astype(BF16)
        u = _bdot_nt(get(ta_ref), st_b) + get(tr_ref)
        u_b = u.astype(BF16)
        y = _bdot_nt(get(rt_ref), st_b) + _bdot(get(arb_ref), u_b) + get(yv_ref)
        wc = wc_ref[:, :, pl.ds(pl.program_id(0) * (ts // c) + j, 1), :].reshape(n, 1, dh)
        st_ref[...] = st * wc + _bdot_tn(jnp.concatenate([u_b, get(v_ref)], axis=1),
                                         jnp.concatenate([get(bw_ref), get(kw_ref)], axis=1))
        y_ref[sl] = y.reshape(nb, nh, c, dh)
        return 0

    lax.fori_loop(0, ts // c, chunk_step, 0)


def rwkv_scan(ta, tr, arb, yv, rt, v, bw, kw, wc, ts):
    b, h, s, dh = ta.shape
    seq = lambda n: pl.BlockSpec((b, h, n, dh), lambda i: (0, 0, i, 0))
    return pl.pallas_call(
        _rwkv_scan_kernel,
        grid=(s // ts,),
        in_specs=[seq(ts)] * 8 + [pl.BlockSpec(wc.shape, lambda i: (0, 0, 0, 0))],
        out_specs=seq(ts),
        out_shape=jax.ShapeDtypeStruct((b, h, s, dh), F32),
        scratch_shapes=[pltpu.VMEM((b * h, dh, dh), F32)],
        compiler_params=_cparams("arbitrary"),
        name="rwkv_scan",
    )(ta, tr, arb, yv, rt, v, bw, kw, wc)


def _out_proj_kernel(x_ref, on_ref, y_ref, bonus_ref, g_ref, lnw_ref, lnb_ref, bd_ref, wn_ref, wr_ref, o_ref):
    y = jnp.concatenate([y_ref[0, h] for h in range(RWKV_HEADS)], axis=-1)
    bd = bd_ref[...]
    yc = y - _dot_x_sel(y, bd)
    yn = yc * lax.rsqrt(_dot_x_sel(yc * yc, bd) + GN_EPS)
    o_rwkv = (yn * lnw_ref[...] + lnb_ref[...] + bonus_ref[...]) * g_ref[...]
    tm = y.shape[0]
    o_nsa_t = on_ref[0].reshape(NSA_HEADS * HEAD_DIM, tm)
    o_ref[...] = (x_ref[...] + _dot_tn(o_nsa_t.astype(BF16), wn_ref[...])
                  + _dot(o_rwkv.astype(BF16), wr_ref[...]))


def out_proj(x, o_nsa_t, y, bonus, g, ln_w, ln_b, w_out, s, tm):
    t, d = x.shape
    dn = o_nsa_t.shape[1] * o_nsa_t.shape[2]
    dr = bonus.shape[1]
    tps = s // tm
    row = lambda w: pl.BlockSpec((tm, w), lambda i: (i, 0))
    full = lambda *sh: pl.BlockSpec(sh, lambda i: (0,) * len(sh))
    return pl.pallas_call(
        _out_proj_kernel,
        grid=(t // tm,),
        in_specs=[row(d), pl.BlockSpec((1, NSA_HEADS, HEAD_DIM, tm), lambda i: (i // tps, 0, 0, i % tps)),
                  pl.BlockSpec((1, RWKV_HEADS, tm, HEAD_DIM), lambda i: (i // tps, 0, i % tps, 0)),
                  row(dr), row(dr), full(1, dr), full(1, dr), full(dr, dr), full(dn, d), full(dr, d)],
        out_specs=row(d),
        out_shape=jax.ShapeDtypeStruct((t, d), F32),
        compiler_params=_cparams("parallel"),
        name="out_proj",
    )(x, o_nsa_t, y, bonus, g, ln_w.reshape(1, dr), ln_b.reshape(1, dr),
      _block_diag_ones(dr, HEAD_DIM, 1.0 / HEAD_DIM), w_out[:dn].astype(BF16), w_out[dn:].astype(BF16))


def _cross_attn_kernel(h_ref, g_ref, wq_ref, qg_ref, kv_ref, kg_ref, wo_ref, o_ref):
    h = h_ref[...]
    d = h.shape[1]
    xd = d // X_HEADS
    q = _dot(_rms(h, g_ref[...]).astype(BF16), wq_ref[...])
    kv = kv_ref[0]
    outs = []
    for hd in range(X_HEADS):
        qh = _rms(q[:, hd * xd:(hd + 1) * xd], qg_ref[...]) * (xd ** -0.5)
        kh = _rms(kv[:, hd * xd:(hd + 1) * xd], kg_ref[...])
        vh = kv[:, d + hd * xd:d + (hd + 1) * xd]
        s = _dot_nt(qh.astype(BF16), kh.astype(BF16))
        p = jnp.exp(s - jnp.max(s, axis=-1, keepdims=True))
        p = p / jnp.sum(p, axis=-1, keepdims=True)
        outs.append(_dot(p.astype(BF16), vh.astype(BF16)))
    o = jnp.concatenate(outs, axis=-1)
    o_ref[...] = h + _dot(o.astype(BF16), wo_ref[...])


def cross_attention(h, kv, norm_g, xq_w, xq_g, xk_g, xo_w, b, s, tm):
    t, d = h.shape
    m = kv.shape[1]
    xd = d // X_HEADS
    tiles = s // tm
    full = lambda *sh: pl.BlockSpec(sh, lambda i: (0,) * len(sh))
    return pl.pallas_call(
        _cross_attn_kernel,
        grid=(t // tm,),
        in_specs=[pl.BlockSpec((tm, d), lambda i: (i, 0)), full(1, d), full(d, d), full(1, xd),
                  pl.BlockSpec((1, m, 2 * d), lambda i: (i // tiles, 0, 0)), full(1, xd), full(d, d)],
        out_specs=pl.BlockSpec((tm, d), lambda i: (i, 0)),
        out_shape=jax.ShapeDtypeStruct((t, d), F32),
        compiler_params=_cparams("parallel"),
        name="cross_attention",
    )(h, norm_g.reshape(1, d), xq_w.astype(BF16), xq_g.reshape(1, xd), kv, xk_g.reshape(1, xd),
      xo_w.astype(BF16))


def _router_kernel(h_ref, g_ref, rw_ref, rb_ref, ltri_ref, xn_o, idx_o, gate_o, rank_o, count_o, seen_ref):
    @pl.when(pl.program_id(0) == 0)
    def _():
        seen_ref[...] = jnp.zeros_like(seen_ref)

    xn = _rms(h_ref[...], g_ref[...])
    xn_o[...] = xn
    logits = _dot(xn, rw_ref[...], precision=HIGHEST) + rb_ref[...]
    tm = logits.shape[0]
    lane = lax.broadcasted_iota(jnp.int32, (tm, LANES), 1)
    lanef = lane.astype(F32)
    logits = jnp.where(lane < N_EXPERTS, logits, REMOVED)
    idx_acc = jnp.zeros((tm, LANES), F32)
    val_acc = jnp.zeros((tm, LANES), F32)
    chosen = jnp.zeros((tm, LANES), F32)
    picks = []
    top = None
    for k in range(TOP_K):
        m = jnp.max(logits, axis=-1, keepdims=True)
        idx = jnp.min(jnp.where(logits == m, lanef, float(LANES)), axis=-1, keepdims=True)
        pick = lanef == idx
        picks.append(pick)
        chosen = jnp.where(pick, 1.0, chosen)
        logits = jnp.where(pick, REMOVED, logits)
        top = m if top is None else top
        idx_acc = jnp.where(lane == k, idx, idx_acc)
        val_acc = jnp.where(lane == k, jnp.exp(m - top), val_acc)
    idx_o[...] = idx_acc.astype(jnp.int32)
    gate_o[...] = val_acc / jnp.sum(val_acc, axis=-1, keepdims=True)
    before = seen_ref[0:1, :] + _dot(ltri_ref[...], chosen.astype(BF16))
    rank_acc = jnp.zeros((tm, LANES), F32)
    for k, pick in enumerate(picks):
        rank_acc = jnp.where(lane == k, jnp.sum(jnp.where(pick, before, 0.0), axis=-1, keepdims=True), rank_acc)
    rank_o[...] = rank_acc.astype(jnp.int32)
    seen_ref[...] = seen_ref[...] + jnp.sum(chosen, axis=0, keepdims=True)
    count_o[...] = seen_ref[...]


def moe_router(h, norm_g, router_w, router_b, tm):
    t, d = h.shape
    rw = jnp.zeros((d, LANES), F32).at[:, :N_EXPERTS].set(router_w)
    rb = jnp.zeros((1, LANES), F32).at[0, :N_EXPERTS].set(router_b)
    i = np.arange(tm)
    ltri = jnp.asarray(i[:, None] > i[None, :], BF16)
    row = lambda w: pl.BlockSpec((tm, w), lambda i: (i, 0))
    full = lambda *s: pl.BlockSpec(s, lambda i: (0,) * len(s))
    return pl.pallas_call(
        _router_kernel,
        grid=(t // tm,),
        in_specs=[row(d), full(1, d), full(d, LANES), full(1, LANES), full(tm, tm)],
        out_specs=[row(d), row(LANES), row(LANES), row(LANES), full(8, LANES)],
        out_shape=[jax.ShapeDtypeStruct((t, d), F32), jax.ShapeDtypeStruct((t, LANES), jnp.int32),
                   jax.ShapeDtypeStruct((t, LANES), F32), jax.ShapeDtypeStruct((t, LANES), jnp.int32),
                   jax.ShapeDtypeStruct((8, LANES), F32)],
        scratch_shapes=[pltpu.VMEM((8, LANES), F32)],
        compiler_params=_cparams("arbitrary"),
        name="moe_router",
    )(h, norm_g.reshape(1, d), rw, rb, ltri)


def _expert_kernel(blk_e_ref, n_used_ref, x_ref, w1_ref, b1_ref, w2_ref, b2_ref, o_ref, w2x_ref):
    i = pl.program_id(0)
    f = w2_ref.shape[1]

    @pl.when(i == 0)
    def _():
        w2x_ref[...] = jnp.zeros_like(w2x_ref)

    @pl.when((i == 0) | (blk_e_ref[i] != blk_e_ref[jnp.maximum(i - 1, 0)]))
    def _():
        for c in range(w2x_ref.shape[0]):
            w2x_ref[c, pl.ds(0, f, stride=2), :] = w2_ref[0, :, c * LANES:(c + 1) * LANES]

    @pl.when(i < n_used_ref[0])
    def _():
        x = x_ref[...].astype(BF16)
        h = _dot(x, w1_ref[0].astype(BF16)) + b1_ref[0]
        hg = jnp.minimum(h, SWIGLU_LIMIT)
        gate = hg * jax.nn.sigmoid(SWIGLU_ALPHA * hg)
        lin = jnp.clip(h, -SWIGLU_LIMIT, SWIGLU_LIMIT) + 1.0
        even = lax.broadcasted_iota(jnp.int32, (1, LANES), 1) % 2 == 0
        parts = []
        for c in range(h.shape[1] // LANES):
            cols = slice(c * LANES, (c + 1) * LANES)
            nxt = pltpu.roll(lin[:, cols], LANES - 1, 1)
            parts.append(jnp.where(even, gate[:, cols] * nxt, 0.0).astype(BF16))
        act = jnp.concatenate(parts, axis=1)
        w2x = jnp.concatenate([w2x_ref[c] for c in range(w2x_ref.shape[0])], axis=1)
        o_ref[...] = _dot(act, w2x.astype(BF16)) + b2_ref[0]

    @pl.when(i >= n_used_ref[0])
    def _():
        o_ref[...] = jnp.zeros_like(o_ref)


def moe_experts(xs, blk_e, n_used, w1, b1, w2, b2):
    r, d = xs.shape
    f2 = w1.shape[2]
    m = MOE_ROW_BLOCK
    ex = lambda *s: pl.BlockSpec((1,) + s, lambda i, be, nu: (be[i],) + (0,) * len(s))
    grid_spec = pltpu.PrefetchScalarGridSpec(
        num_scalar_prefetch=2,
        grid=(r // m,),
        in_specs=[pl.BlockSpec((m, d), lambda i, be, nu: (i, 0)),
                  ex(d, f2), ex(1, f2), ex(f2 // 2, d), ex(1, d)],
        out_specs=pl.BlockSpec((m, d), lambda i, be, nu: (i, 0)),
        scratch_shapes=[pltpu.VMEM((d // LANES, f2, LANES), F32)],
    )
    return pl.pallas_call(
        _expert_kernel,
        grid_spec=grid_spec,
        out_shape=jax.ShapeDtypeStruct((r, d), F32),
        compiler_params=_cparams("arbitrary"),
        name="moe_experts",
    )(blk_e, n_used, xs, w1, b1, w2, b2)


def _combine_kernel(h_ref, gate_ref, *refs):
    *y_refs, o_ref = refs
    acc = h_ref[...]
    for k, y_ref in enumerate(y_refs):
        acc = acc + gate_ref[:, k:k + 1] * y_ref[...]
    o_ref[...] = acc


def moe_combine(h, gate, ys_k, tm):
    t, d = h.shape
    row = lambda w: pl.BlockSpec((tm, w), lambda i: (i, 0))
    return pl.pallas_call(
        _combine_kernel,
        grid=(t // tm,),
        in_specs=[row(d), row(LANES)] + [row(d)] * len(ys_k),
        out_specs=row(d),
        out_shape=jax.ShapeDtypeStruct((t, d), F32),
        compiler_params=_cparams("parallel"),
        name="moe_combine",
    )(h, gate, *ys_k)


def _layer(x, mem, norm_mix_g, w_in, q_norm_g, k_cmp_norm_g, k_slc_norm_g, k_win_norm_g,
           cmp_pe_k, cmp_pe_v, cmp_k_w1, cmp_k_w2, cmp_v_w1, cmp_v_w2,
           rwkv_mu, rwkv_w0, rwkv_w_up, rwkv_a0, rwkv_a_up, rwkv_g_up, rwkv_k_k, rwkv_k_a,
           rwkv_r_k, rwkv_ln_w, rwkv_ln_b, w_out,
           norm_x_g, norm_mem_g, xq_w, xk_w, xv_w, xq_norm_g, xk_norm_g, xo_w,
           norm_ffn_g, router_w, router_b, mlp1_w, mlp1_b, mlp2_w, mlp2_b):
    b, s, d = x.shape
    t = b * s
    tm = 512
    xt = x.reshape(t, d)

    w_nsa = jnp.pad(w_in[:, :NSA_PROJ], ((0, 0), (0, NSA_PROJ_PAD - NSA_PROJ)))
    p_nsa = norm_matmul(xt, norm_mix_g, w_nsa, tm)
    p_rwkv = norm_matmul(xt, norm_mix_g, w_in[:, NSA_PROJ:], tm)

    nq = NSA_HEADS * HEAD_DIM
    gw = NSA_GROUPS * HEAD_DIM
    qn, ks, vs, kw, vw, gates = nsa_prep(p_nsa, q_norm_g, k_slc_norm_g, k_win_norm_g, b, s, tm)
    kcv = nsa_compress(p_nsa[:, nq:nq + gw], p_nsa[:, nq + gw:nq + 2 * gw], cmp_pe_k, cmp_pe_v,
                       cmp_k_w1, cmp_k_w2, cmp_v_w1, cmp_v_w2, k_cmp_norm_g, b, s)
    o_nsa = nsa_attention(qn, kcv, ks, vs, kw, vw, gates, b, s)

    at, bt, kt, rt, v, bw, kwd, wc, g_gate, bonus = rwkv_prep(
        p_rwkv, rwkv_mu, rwkv_w0, rwkv_w_up, rwkv_a0, rwkv_a_up, rwkv_g_up, rwkv_k_k, rwkv_k_a, rwkv_r_k, b, s, tm)
    ta, tr, arb, yv = rwkv_intra(at, bt, kt, rt, v, ts=256)
    y = rwkv_scan(ta, tr, arb, yv, rt, v, bw, kwd, wc, ts=256)

    h1 = out_proj(xt, o_nsa, y, bonus, g_gate, rwkv_ln_w, rwkv_ln_b, w_out, s, tm)
    m = mem.shape[1]
    kv = norm_matmul(mem.reshape(b * m, d), norm_mem_g, jnp.concatenate([xk_w, xv_w], axis=1), m)
    h2 = cross_attention(h1, kv.reshape(b, m, 2 * d), norm_x_g, xq_w, xq_norm_g, xk_norm_g, xo_w, b, s, tm)

    xn, top_i, gate, rank, seen = moe_router(h2, norm_ffn_g, router_w, router_b, tm)
    top_i = top_i[:, :TOP_K]
    a = t * TOP_K
    mb = MOE_ROW_BLOCK
    order = jnp.argsort(top_i.reshape(a), stable=True)
    counts = seen[0, :N_EXPERTS].astype(jnp.int32)
    starts = jnp.cumsum(counts) - counts
    padded = (counts + mb - 1) // mb * mb
    pends = jnp.cumsum(padded)
    pstarts = pends - padded
    pos = pstarts[top_i] + rank[:, :TOP_K]
    n_blocks = -(-a // mb) + N_EXPERTS
    r = n_blocks * mb
    blk_start = jnp.arange(n_blocks, dtype=jnp.int32) * mb
    blk_e = jnp.minimum(jnp.sum(pends[None, :] <= blk_start[:, None], axis=1), N_EXPERTS - 1).astype(jnp.int32)
    src_i = blk_start[:, None] + jnp.arange(mb, dtype=jnp.int32)[None, :] - (pstarts - starts)[blk_e][:, None]
    valid = src_i < (starts + counts)[blk_e][:, None]
    row_src = jnp.where(valid, order[jnp.minimum(src_i, a - 1)] // TOP_K, 0).astype(jnp.int32).reshape(r)
    n_used = (pends[-1] // mb).astype(jnp.int32).reshape(1)
    xs = xn.at[row_src].get(mode="promise_in_bounds")
    f2 = mlp1_w.shape[2]
    ys = moe_experts(xs, blk_e, n_used, mlp1_w, mlp1_b.reshape(N_EXPERTS, 1, f2), mlp2_w,
                     mlp2_b.reshape(N_EXPERTS, 1, d))
    ys_k = [ys.at[pos[:, k]].get(mode="promise_in_bounds") for k in range(TOP_K)]
    out = moe_combine(h2, gate, ys_k, 256)
    return out.reshape(b, s, d)


def kernel(x, mem, norm_mix_g, w_in, q_norm_g, k_cmp_norm_g, k_slc_norm_g, k_win_norm_g, cmp_pe_k, cmp_pe_v, cmp_k_w1, cmp_k_w2, cmp_v_w1, cmp_v_w2, rwkv_mu, rwkv_w0, rwkv_w_up, rwkv_a0, rwkv_a_up, rwkv_g_up, rwkv_k_k, rwkv_k_a, rwkv_r_k, rwkv_ln_w, rwkv_ln_b, w_out, norm_x_g, norm_mem_g, xq_w, xk_w, xv_w, xq_norm_g, xk_norm_g, xo_w, norm_ffn_g, router_w, router_b, mlp1_w, mlp1_b, mlp2_w, mlp2_b):
    params = (norm_mix_g, w_in, q_norm_g, k_cmp_norm_g, k_slc_norm_g, k_win_norm_g, cmp_pe_k, cmp_pe_v,
              cmp_k_w1, cmp_k_w2, cmp_v_w1, cmp_v_w2, rwkv_mu, rwkv_w0, rwkv_w_up, rwkv_a0, rwkv_a_up,
              rwkv_g_up, rwkv_k_k, rwkv_k_a, rwkv_r_k, rwkv_ln_w, rwkv_ln_b, w_out, norm_x_g, norm_mem_g,
              xq_w, xk_w, xv_w, xq_norm_g, xk_norm_g, xo_w, norm_ffn_g, router_w, router_b,
              mlp1_w, mlp1_b, mlp2_w, mlp2_b)
    h = x
    for layer in range(norm_mix_g.shape[0]):
        h = _layer(h, mem, *[prm[layer] for prm in params])
    return h
```

```python
import functools

import numpy as np
import jax
import jax.numpy as jnp
from jax import lax
from jax.experimental import pallas as pl
from jax.experimental.pallas import tpu as pltpu

F32 = jnp.float32
BF16 = jnp.bfloat16
HIGHEST = lax.Precision.HIGHEST

V7X_VMEM_BYTES = 64 * 1024 * 1024
VMEM_LIMIT = V7X_VMEM_BYTES * 3 // 4

HEAD_DIM = 64
NSA_HEADS = 8
NSA_GROUPS = 2
NSA_HPG = NSA_HEADS // NSA_GROUPS
GROUP_W = NSA_HPG * HEAD_DIM
CMP_BLOCK = 32
CMP_STRIDE = 16
SLC_BLOCK = 64
SLC_TOPK = 16
WINDOW = 512
Q_BLOCK = 128
SEL_CHUNK = 512
RWKV_HEADS = 8
RWKV_DIM = RWKV_HEADS * HEAD_DIM
RWKV_CHUNK = 64
GN_EPS = HEAD_DIM * 1e-5
X_HEADS = 4
N_EXPERTS = 32
TOP_K = 4
SWIGLU_LIMIT = 7.0
SWIGLU_ALPHA = 1.702
MOE_ROW_BLOCK = 256
RMS_EPS = 1e-6
NEG_INF = -1e30
BIG = 1e9
REMOVED = -3e38
LANES = 128

NSA_PROJ = NSA_HEADS * HEAD_DIM + 6 * NSA_GROUPS * HEAD_DIM + NSA_HEADS * 3
NSA_PROJ_PAD = -(-NSA_PROJ // LANES) * LANES
RWKV_PROJ = 3 * RWKV_DIM + 64 + 64 + 128


def _cparams(*sem):
    return pltpu.CompilerParams(dimension_semantics=sem, vmem_limit_bytes=VMEM_LIMIT)


def _dot(a, b, **kw):
    return jnp.dot(a, b, preferred_element_type=F32, **kw)


def _dot_nt(a, b, **kw):
    return lax.dot_general(a, b, (((1,), (1,)), ((), ())), preferred_element_type=F32, **kw)


def _dot_tn(a, b, **kw):
    return lax.dot_general(a, b, (((0,), (0,)), ((), ())), preferred_element_type=F32, **kw)


def _rms(x, g):
    return x * lax.rsqrt(jnp.mean(x * x, axis=-1, keepdims=True) + RMS_EPS) * g


def _split_bf16(x, terms=3):
    parts = []
    for _ in range(terms):
        hi = x.astype(BF16)
        parts.append(hi)
        x = x - hi.astype(F32)
    return parts


def _dot_x_sel(x, sel):
    return sum(_dot(p, sel) for p in _split_bf16(x))


def _dot_sel_x(sel, x):
    return sum(_dot(sel, p) for p in _split_bf16(x))


def _block_diag_ones(n, blk, scale=1.0):
    i = np.arange(n)
    return jnp.asarray(((i[:, None] // blk) == (i[None, :] // blk)).astype(np.float32) * scale, BF16)


def _norm_matmul_kernel(x_ref, g_ref, *refs):
    n = len(refs) // 2
    xn = _rms(x_ref[...], g_ref[...]).astype(BF16)
    for w_ref, o_ref in zip(refs[:n], refs[n:]):
        o_ref[...] = _dot(xn, w_ref[...])


def norm_matmul(x, g, ws, tm):
    m, d = x.shape
    return pl.pallas_call(
        _norm_matmul_kernel,
        grid=(m // tm,),
        in_specs=[pl.BlockSpec((tm, d), lambda i: (i, 0)), pl.BlockSpec((1, d), lambda i: (0, 0))]
                 + [pl.BlockSpec((d, w.shape[1]), lambda i: (0, 0)) for w in ws],
        out_specs=[pl.BlockSpec((tm, w.shape[1]), lambda i: (i, 0)) for w in ws],
        out_shape=[jax.ShapeDtypeStruct((m, w.shape[1]), F32) for w in ws],
        compiler_params=_cparams("parallel"),
        name="norm_matmul",
    )(x, g.reshape(1, d), *[w.astype(BF16) for w in ws])


def _nsa_prep_kernel(p_ref, qg_ref, ksg_ref, kwg_ref, bdq_ref, bdk_ref, place_ref, feat_ref,
                     qt_o, ks_o, vst_o, kw_o, vwt_o, gate_o):
    p = p_ref[...]
    tm = p.shape[0]
    nq = NSA_HEADS * HEAD_DIM
    gw = NSA_GROUPS * HEAD_DIM
    q = p[:, :nq]
    msq = _dot_x_sel(q * q, bdq_ref[...])
    qn = q * lax.rsqrt(msq + RMS_EPS) * qg_ref[...] * (HEAD_DIM ** -0.5)
    qt_o[0] = qn.T.reshape(NSA_HEADS, HEAD_DIM, tm).astype(BF16)

    def seg(k):
        return p[:, nq + k * gw: nq + (k + 1) * gw]

    def head_norm(t, g):
        ms = _dot_x_sel(t * t, bdk_ref[...])
        return t * lax.rsqrt(ms + RMS_EPS) * g

    feat = feat_ref[...].astype(F32)
    ks = head_norm(seg(2), ksg_ref[...]).astype(BF16)
    kw = head_norm(seg(4), kwg_ref[...]).astype(BF16)
    vs_t = seg(3).T
    vw_t = seg(5).T
    for g in range(NSA_GROUPS):
        ks_o[0, g] = (_dot(ks, place_ref[g]) + feat).astype(BF16)
        kw_o[0, g] = (_dot(kw, place_ref[g])[:, :2 * HEAD_DIM] + feat[:, :2 * HEAD_DIM]).astype(BF16)
        for j in range(tm // Q_BLOCK):
            tile = (slice(g * HEAD_DIM, (g + 1) * HEAD_DIM), slice(j * Q_BLOCK, (j + 1) * Q_BLOCK))
            vst_o[0, g, j] = vs_t[tile].astype(BF16)
            vwt_o[0, g, j] = vw_t[tile].astype(BF16)
    gate_o[...] = jax.nn.sigmoid(p[:, nq + 6 * gw: nq + 6 * gw + LANES])


def nsa_prep(p_nsa, q_g, ks_g, kw_g, b, s, tm):
    t = p_nsa.shape[0]
    nq = NSA_HEADS * HEAD_DIM
    gw = NSA_GROUPS * HEAD_DIM
    grp, dh = NSA_GROUPS, HEAD_DIM
    tps = s // tm
    kx = 2 * dh + LANES
    assert s // SLC_BLOCK <= LANES
    tok = np.arange(s)
    feat = np.zeros((s, kx), np.float32)
    feat[:, dh] = tok // SLC_BLOCK * SLC_BLOCK
    feat[:, dh + 1] = tok % SLC_BLOCK
    feat[:, dh + 2:dh + 4] = 1.0
    feat[tok, 2 * dh + tok // SLC_BLOCK] = 1.0
    place = np.zeros((grp, gw, kx), np.float32)
    for g in range(grp):
        place[g, g * dh + np.arange(dh), np.arange(dh)] = 1.0
    tile = lambda v, n: jnp.tile(v.reshape(1, HEAD_DIM), (1, n))
    row = lambda w: pl.BlockSpec((tm, w), lambda i: (i, 0))
    full = lambda *sh: pl.BlockSpec(sh, lambda i: (0,) * len(sh))
    seq = lambda *sh: pl.BlockSpec((1, grp) + sh, lambda i: (i // tps, 0, i % tps) + (0,) * (len(sh) - 1))
    return pl.pallas_call(
        _nsa_prep_kernel,
        grid=(t // tm,),
        in_specs=[row(NSA_PROJ_PAD), full(1, nq), full(1, gw), full(1, gw), full(nq, nq), full(gw, gw),
                  full(grp, gw, kx), pl.BlockSpec((tm, kx), lambda i: (i % tps, 0))],
        out_specs=[pl.BlockSpec((1, NSA_HEADS, dh, tm), lambda i: (i // tps, 0, 0, i % tps)),
                   seq(tm, kx), seq(tm // Q_BLOCK, dh, Q_BLOCK), seq(tm, 2 * dh), seq(tm // Q_BLOCK, dh, Q_BLOCK),
                   row(LANES)],
        out_shape=[jax.ShapeDtypeStruct((b, NSA_HEADS, dh, s), BF16),
                   jax.ShapeDtypeStruct((b, grp, s, kx), BF16),
                   jax.ShapeDtypeStruct((b, grp, s // Q_BLOCK, dh, Q_BLOCK), BF16),
                   jax.ShapeDtypeStruct((b, grp, s, 2 * dh), BF16),
                   jax.ShapeDtypeStruct((b, grp, s // Q_BLOCK, dh, Q_BLOCK), BF16),
                   jax.ShapeDtypeStruct((t, LANES), F32)],
        compiler_params=_cparams("parallel"),
        name="nsa_prep",
    )(p_nsa, tile(q_g, NSA_HEADS), tile(ks_g, NSA_GROUPS), tile(kw_g, NSA_GROUPS),
      _block_diag_ones(nq, HEAD_DIM, 1.0 / HEAD_DIM), _block_diag_ones(gw, HEAD_DIM, 1.0 / HEAD_DIM),
      jnp.asarray(place, BF16), jnp.asarray(feat, BF16))


def _compress_kernel(ch_ref, pe_ref, w1_ref, w2_ref, g_ref, o_ref):
    ch = ch_ref[0, 0, 0]
    nc = ch.shape[0]
    half = CMP_STRIDE * HEAD_DIM
    nxt = pltpu.roll(ch, nc - 1, 0)
    w1 = w1_ref[0]
    h1 = (_dot(ch, w1[:half], precision=HIGHEST) + _dot(nxt, w1[half:], precision=HIGHEST)
          + _dot(pe_ref[0], w1, precision=HIGHEST))
    out = _dot(jax.nn.silu(h1), w2_ref[0], precision=HIGHEST)
    out = jnp.where(pl.program_id(0) == 0, _rms(out, g_ref[...]), out)
    o_ref[0, 0, 0] = out.astype(BF16)


def nsa_compress(kc, vc, pe_k, pe_v, kw1, kw2, vw1, vw2, kc_g, b, s):
    nc = s // CMP_STRIDE
    half = CMP_STRIDE * HEAD_DIM

    def chunks(t):
        return t.reshape(b, nc, CMP_STRIDE, NSA_GROUPS, HEAD_DIM).transpose(0, 3, 1, 2, 4).reshape(
            b, NSA_GROUPS, nc, half)

    ch = jnp.stack([chunks(kc), chunks(vc)])
    pe = jnp.stack([pe_k.reshape(1, 2 * half), pe_v.reshape(1, 2 * half)])
    return pl.pallas_call(
        _compress_kernel,
        grid=(2, b, NSA_GROUPS),
        in_specs=[pl.BlockSpec((1, 1, 1, nc, half), lambda kv, bi, g: (kv, bi, g, 0, 0)),
                  pl.BlockSpec((1, 1, 2 * half), lambda kv, bi, g: (kv, 0, 0)),
                  pl.BlockSpec((1, 2 * half, HEAD_DIM), lambda kv, bi, g: (kv, 0, 0)),
                  pl.BlockSpec((1, HEAD_DIM, HEAD_DIM), lambda kv, bi, g: (kv, 0, 0)),
                  pl.BlockSpec((1, HEAD_DIM), lambda kv, bi, g: (0, 0))],
        out_specs=pl.BlockSpec((1, 1, 1, nc, HEAD_DIM), lambda kv, bi, g: (kv, bi, g, 0, 0)),
        out_shape=jax.ShapeDtypeStruct((2, b, NSA_GROUPS, nc, HEAD_DIM), BF16),
        compiler_params=_cparams("parallel", "parallel", "parallel"),
        name="nsa_compress",
    )(ch, pe, jnp.stack([kw1, vw1]), jnp.stack([kw2, vw2]), kc_g.reshape(1, HEAD_DIM))


def _masked_exp_cols(s, mask):
    sm = jnp.where(mask, s, NEG_INF)
    m = jnp.max(sm, axis=0, keepdims=True)
    p = jnp.exp(sm - jnp.where(m > 0.5 * NEG_INF, m, 0.0))
    l = jnp.sum(p, axis=0, keepdims=True)
    return p, 1.0 / jnp.where(l > 0.0, l, 1.0)


def _nsa_attn_kernel(qt_ref, kc_ref, vct_ref, ovt_ref, ks_ref, vst_ref, kw_ref, vwt_ref, gate_ref, o_ref,
                     *, n_sel, n_top):
    g = pl.program_id(1)
    t0 = pl.program_id(2) * Q_BLOCK
    ks_ref, vst_ref, kw_ref, vwt_ref = (r.at[0, 0] for r in (ks_ref, vst_ref, kw_ref, vwt_ref))
    cols = NSA_HPG * Q_BLOCK
    col = lax.broadcasted_iota(jnp.int32, (1, cols), 1)
    tq = t0 + col % Q_BLOCK
    head = g * NSA_HPG + col // Q_BLOCK
    slope = lax.bitcast_convert_type((127 - (head + 1)) << 23, F32)
    qt = jnp.concatenate([qt_ref[0, h] for h in range(NSA_HPG)], axis=1)
    frow = lax.broadcasted_iota(jnp.int32, (HEAD_DIM, cols), 0)
    tq_hi = (tq // SLC_BLOCK * SLC_BLOCK).astype(F32)
    tq_lo = (tq % SLC_BLOCK).astype(F32)
    qpos = jnp.where(frow < 2, slope, jnp.where(frow == 2, -slope * tq_hi, jnp.where(frow == 3, -slope * tq_lo, 0.0)))
    qc = jnp.concatenate([qt, qpos.astype(BF16)], axis=0)

    def heads_sum(x):
        acc = x[:, :Q_BLOCK]
        for h in range(1, NSA_HPG):
            acc = acc + x[:, h * Q_BLOCK:(h + 1) * Q_BLOCK]
        return acc

    kc = kc_ref[0, 0]
    nc = kc.shape[0]
    c_last = lax.broadcasted_iota(jnp.int32, (nc, 1), 0) * CMP_STRIDE + (CMP_BLOCK - 1)
    p_c, inv_c = _masked_exp_cols(_dot(kc, qc), c_last <= tq)
    p_c = p_c * inv_c
    o_c = _dot(vct_ref[0, 0], p_c.astype(BF16))
    imp = _dot_sel_x(ovt_ref[...], heads_sum(p_c))

    bid = lax.broadcasted_iota(jnp.int32, (LANES, Q_BLOCK), 0)
    bidf = bid.astype(F32)
    tq1 = t0 + lax.broadcasted_iota(jnp.int32, (1, Q_BLOCK), 1)
    cur = tq1 // SLC_BLOCK
    forced = (bid == 0) | (bid == cur) | (bid == cur - 1)
    score = jnp.where(forced, BIG, jnp.where(bid * SLC_BLOCK <= tq1, imp, -BIG))
    score = jnp.where(bid < n_sel, score, REMOVED)
    picked = jnp.zeros((LANES, Q_BLOCK), jnp.bool_)
    for _ in range(n_top):
        m = jnp.max(score, axis=0, keepdims=True)
        pick = bidf == jnp.min(jnp.where(score == m, bidf, float(LANES)), axis=0, keepdims=True)
        picked = picked | pick
        score = jnp.where(pick, REMOVED, score)
    sel_neg = jnp.where(picked, 0.0, NEG_INF).astype(BF16)
    qx = jnp.concatenate([qc, jnp.concatenate([sel_neg] * NSA_HPG, axis=1)], axis=0)

    def values_t(ref, first_tile, n_tiles):
        return jnp.concatenate([ref[first_tile + i] for i in range(n_tiles)], axis=1)

    span = WINDOW + Q_BLOCK
    ws = pl.multiple_of(jnp.maximum(t0 - WINDOW, 0), Q_BLOCK)
    dw = tq - (ws + lax.broadcasted_iota(jnp.int32, (span, 1), 0))
    in_window = dw.astype(jnp.uint32) < WINDOW
    p_w, inv_w = _masked_exp_cols(_dot(kw_ref[pl.ds(ws, span), :], qc), in_window)
    o_w = _dot(values_t(vwt_ref, ws // Q_BLOCK, span // Q_BLOCK), p_w.astype(BF16)) * inv_w

    def scores(chunk):
        return _dot(ks_ref[pl.ds(pl.multiple_of(chunk * SEL_CHUNK, SEL_CHUNK), SEL_CHUNK), :], qx)

    def flash(s, chunk, carry):
        m, l, acc = carry
        m_new = jnp.maximum(m, jnp.max(s, axis=0, keepdims=True))
        alpha = jnp.exp(m - m_new)
        p = jnp.exp(s - m_new)
        l = alpha * l + jnp.sum(p, axis=0, keepdims=True)
        v = values_t(vst_ref, chunk * (SEL_CHUNK // Q_BLOCK), SEL_CHUNK // Q_BLOCK)
        return m_new, l, alpha * acc + _dot(v, p.astype(BF16))

    def flash_pair(chunk_a, chunk_b, carry, keep_b=None):
        s_a, s_b = scores(chunk_a), scores(chunk_b)
        if keep_b is not None:
            s_b = jnp.where(keep_b, s_b, NEG_INF)
        return flash(s_b, chunk_b, flash(s_a, chunk_a, carry))

    bpc = SEL_CHUNK // SLC_BLOCK
    blk_any = jnp.max(jnp.where(picked, 1.0, 0.0), axis=1, keepdims=True)
    chunk_bit = lax.bitcast_convert_type((bid[:, :1] // bpc + 127) << 23, F32)
    bits = jnp.max((blk_any * chunk_bit).reshape(LANES // bpc, bpc, 1), axis=1)
    active = jnp.sum(bits, axis=0, keepdims=True)[0, 0].astype(jnp.int32)

    def full_step(j, state):
        def on_active(st):
            return lax.cond(st[0] >= 0,
                            lambda s2: (jnp.int32(-1), flash_pair(s2[0], j, s2[1])),
                            lambda s2: (j, s2[1]), st)
        return lax.cond((active >> j) & 1 == 1, on_active, lambda st: st, state)

    n_full = t0 // SEL_CHUNK
    init = (jnp.full((1, cols), NEG_INF, F32), jnp.zeros((1, cols), F32), jnp.zeros((HEAD_DIM, cols), F32))
    waiting, carry = lax.fori_loop(0, n_full, full_step, (jnp.int32(-1), init))
    keep = n_full * SEL_CHUNK + lax.broadcasted_iota(jnp.int32, (SEL_CHUNK, 1), 0) <= tq
    _, l_s, acc_s = lax.cond(
        waiting >= 0,
        lambda c: flash_pair(waiting, n_full, c, keep_b=keep),
        lambda c: flash(jnp.where(keep, scores(n_full), NEG_INF), n_full, c), carry)
    o_s = acc_s / l_s

    for h in range(NSA_HPG):
        hs = slice(h * Q_BLOCK, (h + 1) * Q_BLOCK)
        gate = lambda br: gate_ref[0, 0, br, h:h + 1, :]
        o_ref[0, h] = gate(0) * o_c[:, hs] + gate(1) * o_s[:, hs] + gate(2) * o_w[:, hs]


def nsa_attention(qt, kcv, ks_x, vs_t, kw_x, vw_t, gates, b, s):
    nq_blocks = s // Q_BLOCK
    nc = s // CMP_STRIDE
    n_cmp = nc - CMP_BLOCK // CMP_STRIDE + 1
    n_sel = s // SLC_BLOCK
    n_top = min(SLC_TOPK, n_sel)
    grp, dh = NSA_GROUPS, HEAD_DIM
    assert n_sel <= LANES and s % SEL_CHUNK == 0 and s >= WINDOW + Q_BLOCK
    c_start = np.arange(n_cmp) * CMP_STRIDE
    s_start = np.arange(n_sel) * SLC_BLOCK
    ovt = np.zeros((LANES, nc), np.float32)
    ovt[:n_sel, :n_cmp] = (np.clip(np.minimum((c_start + CMP_BLOCK)[:, None], s_start[None] + SLC_BLOCK)
                                   - np.maximum(c_start[:, None], s_start[None]), 0, None) / CMP_BLOCK).T

    c_pos = np.arange(nc) * CMP_STRIDE + (CMP_BLOCK - 1)
    c_feat = np.zeros((nc, dh), np.float32)
    c_feat[:, 0] = c_pos // SLC_BLOCK * SLC_BLOCK
    c_feat[:, 1] = c_pos % SLC_BLOCK
    c_feat[:, 2:4] = 1.0
    kc_x = jnp.concatenate([kcv[0], jnp.broadcast_to(jnp.asarray(c_feat, BF16), kcv[0].shape)], axis=-1)
    vct = kcv[1].transpose(0, 1, 3, 2)
    gates_t = gates[:, :NSA_HEADS * 3].reshape(b, s, grp, NSA_HPG, 3).transpose(0, 2, 4, 3, 1)

    grp_spec = lambda *shape: pl.BlockSpec((1, 1) + shape, lambda bi, g, i: (bi, g) + (0,) * len(shape),
                                           pipeline_mode=pl.Buffered(1))
    return pl.pallas_call(
        functools.partial(_nsa_attn_kernel, n_sel=n_sel, n_top=n_top),
        grid=(b, grp, nq_blocks),
        in_specs=[pl.BlockSpec((1, NSA_HPG, dh, Q_BLOCK), lambda bi, g, i: (bi, g, 0, i)),
                  grp_spec(nc, 2 * dh), grp_spec(dh, nc),
                  pl.BlockSpec((LANES, nc), lambda bi, g, i: (0, 0)),
                  grp_spec(s, 2 * dh + LANES), grp_spec(s // Q_BLOCK, dh, Q_BLOCK),
                  grp_spec(s, 2 * dh), grp_spec(s // Q_BLOCK, dh, Q_BLOCK),
                  pl.BlockSpec((1, 1, 3, NSA_HPG, Q_BLOCK), lambda bi, g, i: (bi, g, 0, 0, i))],
        out_specs=pl.BlockSpec((1, NSA_HPG, dh, Q_BLOCK), lambda bi, g, i: (bi, g, 0, i)),
        out_shape=jax.ShapeDtypeStruct((b, NSA_HEADS, dh, s), F32),
        compiler_params=_cparams("parallel", "parallel", "arbitrary"),
        name="nsa_attention",
    )(qt, kc_x, vct, jnp.asarray(ovt, BF16), ks_x, vs_t, kw_x, vw_t, gates_t)


def _rwkv_prep_kernel(p_ref, prev_ref, mu_ref, w0_ref, a0_ref, kk_ref, ka_ref, rk_ref,
                      wup_ref, aup_ref, gup_ref, bd_ref, ltri_ref, lones_ref, csum_ref,
                      at_o, bt_o, kt_o, rt_o, v_o, bw_o, kw_o, wc_o, g_o, bonus_o, *, tiles_per_seq):
    p = p_ref[...]
    tm = p.shape[0]
    first = pl.program_id(0) % tiles_per_seq == 0
    last_prev = jnp.where(first, 0.0, prev_ref[7:8, :])
    prev = pltpu.roll(p, 1, 0)
    prev = jnp.where(lax.broadcasted_iota(jnp.int32, (tm, 1), 0) == 0, last_prev, prev)
    pm = p + (prev - p) * mu_ref[...]
    d = RWKV_DIM
    r, k, v = pm[:, :d], pm[:, d:2 * d], pm[:, 2 * d:3 * d]
    lora = pm[:, 3 * d:3 * d + LANES]
    gd = pm[:, 3 * d + LANES:3 * d + 2 * LANES]
    z = -(w0_ref[...] + _dot(jnp.tanh(lora).astype(BF16), wup_ref[...]))
    softplus = jnp.maximum(z, 0.0) + jnp.log(1.0 + jnp.exp(-jnp.abs(z)))
    w = -softplus - 0.5
    a = jax.nn.sigmoid(a0_ref[...] + _dot(lora.astype(BF16), aup_ref[...]))
    g_o[...] = _dot(jax.nn.sigmoid(gd).astype(BF16), gup_ref[...])
    bd = bd_ref[...]
    kkr = k * kk_ref[...]
    kk = kkr / jnp.maximum(jnp.sqrt(_dot_x_sel(kkr * kkr, bd)), 1e-12)
    k2 = k * (1.0 + (a - 1.0) * ka_ref[...])
    bonus_o[...] = _dot_x_sel(r * k2 * rk_ref[...], bd) * v
    lw = -jnp.exp(w)
    lw_parts = _split_bf16(lw)
    cum = sum(_dot(ltri_ref[...], p) for p in lw_parts)
    tot = sum(_dot(lones_ref[...], p) for p in lw_parts)
    e_in = jnp.exp(cum)
    e_out = jnp.exp(-cum)
    e_end = jnp.exp(tot - cum)

    def put_heads(o, val):
        for h in range(RWKV_HEADS):
            o[0, h] = val[:, h * HEAD_DIM:(h + 1) * HEAD_DIM].astype(o.dtype)

    put_heads(at_o, -kk * jnp.exp(cum - lw))
    put_heads(bt_o, kk * a * e_out)
    put_heads(kt_o, k2 * e_out)
    put_heads(rt_o, r * e_in)
    put_heads(v_o, v)
    put_heads(bw_o, kk * a * e_end)
    put_heads(kw_o, k2 * e_end)
    put_heads(wc_o, jnp.exp(sum(_dot(csum_ref[...], p) for p in lw_parts)))


def rwkv_prep(p_rwkv, mu, w0, w_up, a0, a_up, g_up, k_k, k_a, r_k, b, s, tm):
    t = p_rwkv.shape[0]
    d = RWKV_DIM
    c = RWKV_CHUNK
    tps = s // tm
    cpt = tm // c
    wup = jnp.concatenate([w_up, jnp.zeros_like(a_up)], axis=0).astype(BF16)
    aup = jnp.concatenate([jnp.zeros_like(w_up), a_up], axis=0).astype(BF16)
    i = np.arange(tm)
    same = (i[:, None] // c) == (i[None, :] // c)
    ltri = jnp.asarray(same & (i[:, None] >= i[None, :]), BF16)
    lones = jnp.asarray(same, BF16)
    csum = jnp.asarray(np.arange(cpt)[:, None] == (i[None, :] // c), BF16)
    row = lambda w: pl.BlockSpec((tm, w), lambda i: (i, 0))
    full = lambda *sh: pl.BlockSpec(sh, lambda i: (0,) * len(sh))
    heads = lambda n: pl.BlockSpec((1, RWKV_HEADS, n, HEAD_DIM), lambda i: (i // tps, 0, i % tps, 0))
    hshape = lambda n, dt: jax.ShapeDtypeStruct((b, RWKV_HEADS, n, HEAD_DIM), dt)
    vec = lambda x: x.reshape(1, -1)
    return pl.pallas_call(
        functools.partial(_rwkv_prep_kernel, tiles_per_seq=tps),
        grid=(t // tm,),
        in_specs=[row(RWKV_PROJ),
                  pl.BlockSpec((8, RWKV_PROJ), lambda i: (jnp.maximum(i * (tm // 8) - 1, 0), 0)),
                  full(1, RWKV_PROJ), full(1, d), full(1, d), full(1, d), full(1, d), full(1, d),
                  full(LANES, d), full(LANES, d), full(LANES, d), full(d, d), full(tm, tm), full(tm, tm),
                  full(cpt, tm)],
        out_specs=[heads(tm)] * 7 + [heads(cpt), row(d), row(d)],
        out_shape=[hshape(s, BF16)] * 7 + [hshape(s // c, F32)] + [jax.ShapeDtypeStruct((t, d), F32)] * 2,
        compiler_params=_cparams("parallel"),
        name="rwkv_prep",
    )(p_rwkv, p_rwkv, vec(mu), vec(w0), vec(a0), vec(k_k), vec(k_a), vec(r_k), wup, aup, g_up.astype(BF16),
      _block_diag_ones(d, HEAD_DIM), ltri, lones, csum)


def _bdot(a, b):
    return lax.dot_general(a, b, (((2,), (1,)), ((0,), (0,))), preferred_element_type=F32)


def _bdot_nt(a, b):
    return lax.dot_general(a, b, (((2,), (2,)), ((0,), (0,))), preferred_element_type=F32)


def _bdot_tn(a, b):
    return lax.dot_general(a, b, (((1,), (1,)), ((0,), (0,))), preferred_element_type=F32)


def _rwkv_intra_kernel(at_ref, bt_ref, kt_ref, rt_ref, v_ref, ta_o, tr_o, arb_o, yv_o):
    c = RWKV_CHUNK
    _, nh, ts, dh = at_ref.shape
    n = nh * (ts // c)
    chunked = lambda ref: ref[0].reshape(n, c, dh)
    at, bt, kt, rt, v = (chunked(r) for r in (at_ref, bt_ref, kt_ref, rt_ref, v_ref))
    ri = lax.broadcasted_iota(jnp.int32, (1, c, c), 1)
    ci = lax.broadcasted_iota(jnp.int32, (1, c, c), 2)
    strict = ri > ci
    incl = ri >= ci
    ar = jnp.concatenate([at, rt], axis=1)
    xb = _bdot_nt(ar, bt)
    xk = _bdot_nt(ar, kt)
    l_ab = jnp.where(strict, xb[:, :c], 0.0)
    a_ak = jnp.where(strict, xk[:, :c], 0.0)
    a_rb = jnp.where(incl, xb[:, c:], 0.0)
    a_rk = jnp.where(incl, xk[:, c:], 0.0)
    pw = l_ab
    tinv = jnp.where(ri == ci, 1.0, 0.0) + l_ab
    for _ in range(int(np.log2(c)) - 1):
        pw_b = pw.astype(BF16)
        pw = _bdot(pw_b, pw_b)
        tinv = tinv + _bdot(tinv.astype(BF16), pw.astype(BF16))
    tinv_b = tinv.astype(BF16)

    def put(o, val):
        o[0] = val.reshape(nh, ts, val.shape[-1]).astype(o.dtype)

    put(ta_o, _bdot(tinv_b, at))
    put(tr_o, _bdot(tinv_b, _bdot(a_ak.astype(BF16), v).astype(BF16)))
    put(arb_o, a_rb)
    put(yv_o, _bdot(a_rk.astype(BF16), v))


def rwkv_intra(at, bt, kt, rt, v, ts):
    b, h, s, dh = at.shape
    seq = lambda: pl.BlockSpec((1, h, ts, dh), lambda bi, i: (bi, 0, i, 0))
    shp = lambda dt: jax.ShapeDtypeStruct((b, h, s, dh), dt)
    return pl.pallas_call(
        _rwkv_intra_kernel,
        grid=(b, s // ts),
        in_specs=[seq()] * 5,
        out_specs=[seq()] * 4,
        out_shape=[shp(BF16), shp(F32), shp(BF16), shp(F32)],
        compiler_params=_cparams("parallel", "parallel"),
        name="rwkv_intra",
    )(at, bt, kt, rt, v)


def _rwkv_scan_kernel(ta_ref, tr_ref, arb_ref, yv_ref, rt_ref, v_ref, bw_ref, kw_ref, wc_ref, y_ref, st_ref):
    c = RWKV_CHUNK
    nb, nh, ts, dh = ta_ref.shape
    n = nb * nh

    @pl.when(pl.program_id(0) == 0)
    def _():
        st_ref[...] = jnp.zeros_like(st_ref)

    def chunk_step(j, _):
        sl = (slice(None), slice(None), pl.ds(pl.multiple_of(j * c, c), c), slice(None))
        get = lambda ref: ref[sl].reshape(n, c, dh)
        st = st_ref[...]
        st_b = st.astype(BF16)
        u = _bdot_nt(get(ta_ref), st_b) + get(tr_ref)
        u_b = u.astype(BF16)
        y = _bdot_nt(get(rt_ref), st_b) + _bdot(get(arb_ref), u_b) + get(yv_ref)
        wc = wc_ref[:, :, pl.ds(pl.program_id(0) * (ts // c) + j, 1), :].reshape(n, 1, dh)
        st_ref[...] = st * wc + _bdot_tn(jnp.concatenate([u_b, get(v_ref)], axis=1),
                                         jnp.concatenate([get(bw_ref), get(kw_ref)], axis=1))
        y_ref[sl] = y.reshape(nb, nh, c, dh)
        return 0

    lax.fori_loop(0, ts // c, chunk_step, 0)


def rwkv_scan(ta, tr, arb, yv, rt, v, bw, kw, wc, ts):
    b, h, s, dh = ta.shape
    seq = lambda n: pl.BlockSpec((b, h, n, dh), lambda i: (0, 0, i, 0))
    return pl.pallas_call(
        _rwkv_scan_kernel,
        grid=(s // ts,),
        in_specs=[seq(ts)] * 8 + [pl.BlockSpec(wc.shape, lambda i: (0, 0, 0, 0))],
        out_specs=seq(ts),
        out_shape=jax.ShapeDtypeStruct((b, h, s, dh), F32),
        scratch_shapes=[pltpu.VMEM((b * h, dh, dh), F32)],
        compiler_params=_cparams("arbitrary"),
        name="rwkv_scan",
    )(ta, tr, arb, yv, rt, v, bw, kw, wc)


def _out_proj_kernel(x_ref, on_ref, y_ref, bonus_ref, g_ref, lnw_ref, lnb_ref, bd_ref, wn_ref, wr_ref, o_ref):
    y = jnp.concatenate([y_ref[0, h] for h in range(RWKV_HEADS)], axis=-1)
    bd = bd_ref[...]
    yc = y - _dot_x_sel(y, bd)
    yn = yc * lax.rsqrt(_dot_x_sel(yc * yc, bd) + GN_EPS)
    o_rwkv = (yn * lnw_ref[...] + lnb_ref[...] + bonus_ref[...]) * g_ref[...]
    tm = y.shape[0]
    o_nsa_t = on_ref[0].reshape(NSA_HEADS * HEAD_DIM, tm)
    o_ref[...] = (x_ref[...] + _dot_tn(o_nsa_t.astype(BF16), wn_ref[...])
                  + _dot(o_rwkv.astype(BF16), wr_ref[...]))


def out_proj(x, o_nsa_t, y, bonus, g, ln_w, ln_b, w_out, s, tm):
    t, d = x.shape
    dn = o_nsa_t.shape[1] * o_nsa_t.shape[2]
    dr = bonus.shape[1]
    tps = s // tm
    row = lambda w: pl.BlockSpec((tm, w), lambda i: (i, 0))
    full = lambda *sh: pl.BlockSpec(sh, lambda i: (0,) * len(sh))
    return pl.pallas_call(
        _out_proj_kernel,
        grid=(t // tm,),
        in_specs=[row(d), pl.BlockSpec((1, NSA_HEADS, HEAD_DIM, tm), lambda i: (i // tps, 0, 0, i % tps)),
                  pl.BlockSpec((1, RWKV_HEADS, tm, HEAD_DIM), lambda i: (i // tps, 0, i % tps, 0)),
                  row(dr), row(dr), full(1, dr), full(1, dr), full(dr, dr), full(dn, d), full(dr, d)],
        out_specs=row(d),
        out_shape=jax.ShapeDtypeStruct((t, d), F32),
        compiler_params=_cparams("parallel"),
        name="out_proj",
    )(x, o_nsa_t, y, bonus, g, ln_w.reshape(1, dr), ln_b.reshape(1, dr),
      _block_diag_ones(dr, HEAD_DIM, 1.0 / HEAD_DIM), w_out[:dn].astype(BF16), w_out[dn:].astype(BF16))


def _cross_attn_kernel(h_ref, g_ref, wq_ref, qg_ref, kv_ref, kg_ref, wo_ref, o_ref):
    h = h_ref[...]
    d = h.shape[1]
    xd = d // X_HEADS
    q = _dot(_rms(h, g_ref[...]).astype(BF16), wq_ref[...])
    kv = kv_ref[0]
    outs = []
    for hd in range(X_HEADS):
        qh = _rms(q[:, hd * xd:(hd + 1) * xd], qg_ref[...]) * (xd ** -0.5)
        kh = _rms(kv[:, hd * xd:(hd + 1) * xd], kg_ref[...])
        vh = kv[:, d + hd * xd:d + (hd + 1) * xd]
        s = _dot_nt(qh.astype(BF16), kh.astype(BF16))
        p = jnp.exp(s - jnp.max(s, axis=-1, keepdims=True))
        p = p / jnp.sum(p, axis=-1, keepdims=True)
        outs.append(_dot(p.astype(BF16), vh.astype(BF16)))
    o = jnp.concatenate(outs, axis=-1)
    o_ref[...] = h + _dot(o.astype(BF16), wo_ref[...])


def cross_attention(h, kv, norm_g, xq_w, xq_g, xk_g, xo_w, b, s, tm):
    t, d = h.shape
    m = kv.shape[1]
    xd = d // X_HEADS
    tiles = s // tm
    full = lambda *sh: pl.BlockSpec(sh, lambda i: (0,) * len(sh))
    return pl.pallas_call(
        _cross_attn_kernel,
        grid=(t // tm,),
        in_specs=[pl.BlockSpec((tm, d), lambda i: (i, 0)), full(1, d), full(d, d), full(1, xd),
                  pl.BlockSpec((1, m, 2 * d), lambda i: (i // tiles, 0, 0)), full(1, xd), full(d, d)],
        out_specs=pl.BlockSpec((tm, d), lambda i: (i, 0)),
        out_shape=jax.ShapeDtypeStruct((t, d), F32),
        compiler_params=_cparams("parallel"),
        name="cross_attention",
    )(h, norm_g.reshape(1, d), xq_w.astype(BF16), xq_g.reshape(1, xd), kv, xk_g.reshape(1, xd),
      xo_w.astype(BF16))


def _router_kernel(h_ref, g_ref, rw_ref, rb_ref, ltri_ref, xn_o, idx_o, gate_o, rank_o, count_o, seen_ref):
    @pl.when(pl.program_id(0) == 0)
    def _():
        seen_ref[...] = jnp.zeros_like(seen_ref)

    xn = _rms(h_ref[...], g_ref[...])
    xn_o[...] = xn
    logits = _dot(xn, rw_ref[...], precision=HIGHEST) + rb_ref[...]
    tm = logits.shape[0]
    lane = lax.broadcasted_iota(jnp.int32, (tm, LANES), 1)
    lanef = lane.astype(F32)
    logits = jnp.where(lane < N_EXPERTS, logits, REMOVED)
    idx_acc = jnp.zeros((tm, LANES), F32)
    val_acc = jnp.zeros((tm, LANES), F32)
    chosen = jnp.zeros((tm, LANES), F32)
    picks = []
    top = None
    for k in range(TOP_K):
        m = jnp.max(logits, axis=-1, keepdims=True)
        idx = jnp.min(jnp.where(logits == m, lanef, float(LANES)), axis=-1, keepdims=True)
        pick = lanef == idx
        picks.append(pick)
        chosen = jnp.where(pick, 1.0, chosen)
        logits = jnp.where(pick, REMOVED, logits)
        top = m if top is None else top
        idx_acc = jnp.where(lane == k, idx, idx_acc)
        val_acc = jnp.where(lane == k, jnp.exp(m - top), val_acc)
    idx_o[...] = idx_acc.astype(jnp.int32)
    gate_o[...] = val_acc / jnp.sum(val_acc, axis=-1, keepdims=True)
    before = seen_ref[0:1, :] + _dot(ltri_ref[...], chosen.astype(BF16))
    rank_acc = jnp.zeros((tm, LANES), F32)
    for k, pick in enumerate(picks):
        rank_acc = jnp.where(lane == k, jnp.sum(jnp.where(pick, before, 0.0), axis=-1, keepdims=True), rank_acc)
    rank_o[...] = rank_acc.astype(jnp.int32)
    seen_ref[...] = seen_ref[...] + jnp.sum(chosen, axis=0, keepdims=True)
    count_o[...] = seen_ref[...]


def moe_router(h, norm_g, router_w, router_b, tm):
    t, d = h.shape
    rw = jnp.zeros((d, LANES), F32).at[:, :N_EXPERTS].set(router_w)
    rb = jnp.zeros((1, LANES), F32).at[0, :N_EXPERTS].set(router_b)
    i = np.arange(tm)
    ltri = jnp.asarray(i[:, None] > i[None, :], BF16)
    row = lambda w: pl.BlockSpec((tm, w), lambda i: (i, 0))
    full = lambda *s: pl.BlockSpec(s, lambda i: (0,) * len(s))
    return pl.pallas_call(
        _router_kernel,
        grid=(t // tm,),
        in_specs=[row(d), full(1, d), full(d, LANES), full(1, LANES), full(tm, tm)],
        out_specs=[row(d), row(LANES), row(LANES), row(LANES), full(8, LANES)],
        out_shape=[jax.ShapeDtypeStruct((t, d), F32), jax.ShapeDtypeStruct((t, LANES), jnp.int32),
                   jax.ShapeDtypeStruct((t, LANES), F32), jax.ShapeDtypeStruct((t, LANES), jnp.int32),
                   jax.ShapeDtypeStruct((8, LANES), F32)],
        scratch_shapes=[pltpu.VMEM((8, LANES), F32)],
        compiler_params=_cparams("arbitrary"),
        name="moe_router",
    )(h, norm_g.reshape(1, d), rw, rb, ltri)


def _expert_kernel(blk_e_ref, n_used_ref, x_ref, w1_ref, b1_ref, w2_ref, b2_ref, o_ref, w2x_ref):
    i = pl.program_id(0)
    f = w2_ref.shape[1]

    @pl.when(i == 0)
    def _():
        w2x_ref[...] = jnp.zeros_like(w2x_ref)

    @pl.when((i == 0) | (blk_e_ref[i] != blk_e_ref[jnp.maximum(i - 1, 0)]))
    def _():
        for c in range(w2x_ref.shape[0]):
            w2x_ref[c, pl.ds(0, f, stride=2), :] = w2_ref[0, :, c * LANES:(c + 1) * LANES]

    @pl.when(i < n_used_ref[0])
    def _():
        x = x_ref[...].astype(BF16)
        h = _dot(x, w1_ref[0].astype(BF16)) + b1_ref[0]
        hg = jnp.minimum(h, SWIGLU_LIMIT)
        gate = hg * jax.nn.sigmoid(SWIGLU_ALPHA * hg)
        lin = jnp.clip(h, -SWIGLU_LIMIT, SWIGLU_LIMIT) + 1.0
        even = lax.broadcasted_iota(jnp.int32, (1, LANES), 1) % 2 == 0
        parts = []
        for c in range(h.shape[1] // LANES):
            cols = slice(c * LANES, (c + 1) * LANES)
            nxt = pltpu.roll(lin[:, cols], LANES - 1, 1)
            parts.append(jnp.where(even, gate[:, cols] * nxt, 0.0).astype(BF16))
        act = jnp.concatenate(parts, axis=1)
        w2x = jnp.concatenate([w2x_ref[c] for c in range(w2x_ref.shape[0])], axis=1)
        o_ref[...] = _dot(act, w2x.astype(BF16)) + b2_ref[0]

    @pl.when(i >= n_used_ref[0])
    def _():
        o_ref[...] = jnp.zeros_like(o_ref)


def moe_experts(xs, blk_e, n_used, w1, b1, w2, b2):
    r, d = xs.shape
    f2 = w1.shape[2]
    m = MOE_ROW_BLOCK
    ex = lambda *s: pl.BlockSpec((1,) + s, lambda i, be, nu: (be[i],) + (0,) * len(s))
    grid_spec = pltpu.PrefetchScalarGridSpec(
        num_scalar_prefetch=2,
        grid=(r // m,),
        in_specs=[pl.BlockSpec((m, d), lambda i, be, nu: (i, 0)),
                  ex(d, f2), ex(1, f2), ex(f2 // 2, d), ex(1, d)],
        out_specs=pl.BlockSpec((m, d), lambda i, be, nu: (i, 0)),
        scratch_shapes=[pltpu.VMEM((d // LANES, f2, LANES), F32)],
    )
    return pl.pallas_call(
        _expert_kernel,
        grid_spec=grid_spec,
        out_shape=jax.ShapeDtypeStruct((r, d), F32),
        compiler_params=_cparams("arbitrary"),
        name="moe_experts",
    )(blk_e, n_used, xs, w1, b1, w2, b2)


def _combine_kernel(h_ref, gate_ref, *refs):
    *y_refs, o_ref = refs
    acc = h_ref[...]
    for k, y_ref in enumerate(y_refs):
        acc = acc + gate_ref[:, k:k + 1] * y_ref[...]
    o_ref[...] = acc


def moe_combine(h, gate, ys_k, tm):
    t, d = h.shape
    row = lambda w: pl.BlockSpec((tm, w), lambda i: (i, 0))
    return pl.pallas_call(
        _combine_kernel,
        grid=(t // tm,),
        in_specs=[row(d), row(LANES)] + [row(d)] * len(ys_k),
        out_specs=row(d),
        out_shape=jax.ShapeDtypeStruct((t, d), F32),
        compiler_params=_cparams("parallel"),
        name="moe_combine",
    )(h, gate, *ys_k)


def _layer(x, mem, norm_mix_g, w_in, q_norm_g, k_cmp_norm_g, k_slc_norm_g, k_win_norm_g,
           cmp_pe_k, cmp_pe_v, cmp_k_w1, cmp_k_w2, cmp_v_w1, cmp_v_w2,
           rwkv_mu, rwkv_w0, rwkv_w_up, rwkv_a0, rwkv_a_up, rwkv_g_up, rwkv_k_k, rwkv_k_a,
           rwkv_r_k, rwkv_ln_w, rwkv_ln_b, w_out,
           norm_x_g, norm_mem_g, xq_w, xk_w, xv_w, xq_norm_g, xk_norm_g, xo_w,
           norm_ffn_g, router_w, router_b, mlp1_w, mlp1_b, mlp2_w, mlp2_b):
    b, s, d = x.shape
    t = b * s
    tm = 512
    xt = x.reshape(t, d)

    w_nsa = jnp.pad(w_in[:, :NSA_PROJ], ((0, 0), (0, NSA_PROJ_PAD - NSA_PROJ)))
    p_nsa, p_rwkv = norm_matmul(xt, norm_mix_g, [w_nsa, w_in[:, NSA_PROJ:]], tm)

    nq = NSA_HEADS * HEAD_DIM
    gw = NSA_GROUPS * HEAD_DIM
    qn, ks, vs, kw, vw, gates = nsa_prep(p_nsa, q_norm_g, k_slc_norm_g, k_win_norm_g, b, s, tm)
    kcv = nsa_compress(p_nsa[:, nq:nq + gw], p_nsa[:, nq + gw:nq + 2 * gw], cmp_pe_k, cmp_pe_v,
                       cmp_k_w1, cmp_k_w2, cmp_v_w1, cmp_v_w2, k_cmp_norm_g, b, s)
    o_nsa = nsa_attention(qn, kcv, ks, vs, kw, vw, gates, b, s)

    at, bt, kt, rt, v, bw, kwd, wc, g_gate, bonus = rwkv_prep(
        p_rwkv, rwkv_mu, rwkv_w0, rwkv_w_up, rwkv_a0, rwkv_a_up, rwkv_g_up, rwkv_k_k, rwkv_k_a, rwkv_r_k, b, s, tm)
    ta, tr, arb, yv = rwkv_intra(at, bt, kt, rt, v, ts=256)
    y = rwkv_scan(ta, tr, arb, yv, rt, v, bw, kwd, wc, ts=256)

    h1 = out_proj(xt, o_nsa, y, bonus, g_gate, rwkv_ln_w, rwkv_ln_b, w_out, s, tm)
    m = mem.shape[1]
    kv, = norm_matmul(mem.reshape(b * m, d), norm_mem_g, [jnp.concatenate([xk_w, xv_w], axis=1)], m)
    h2 = cross_attention(h1, kv.reshape(b, m, 2 * d), norm_x_g, xq_w, xq_norm_g, xk_norm_g, xo_w, b, s, tm)

    xn, top_i, gate, rank, seen = moe_router(h2, norm_ffn_g, router_w, router_b, tm)
    top_i = top_i[:, :TOP_K]
    a = t * TOP_K
    mb = MOE_ROW_BLOCK
    idx_bits = max(a - 1, 1).bit_length()
    assert N_EXPERTS << idx_bits < 2 ** 31
    packed = (top_i.reshape(a) << idx_bits) | jnp.arange(a, dtype=jnp.int32)
    order = jnp.sort(packed) & ((1 << idx_bits) - 1)
    counts = seen[0, :N_EXPERTS].astype(jnp.int32)
    starts = jnp.cumsum(counts) - counts
    padded = (counts + mb - 1) // mb * mb
    pends = jnp.cumsum(padded)
    pstarts = pends - padded
    pos = pstarts[top_i] + rank[:, :TOP_K]
    n_blocks = -(-a // mb) + N_EXPERTS
    r = n_blocks * mb
    blk_start = jnp.arange(n_blocks, dtype=jnp.int32) * mb
    blk_e = jnp.minimum(jnp.sum(pends[None, :] <= blk_start[:, None], axis=1), N_EXPERTS - 1).astype(jnp.int32)
    src_i = blk_start[:, None] + jnp.arange(mb, dtype=jnp.int32)[None, :] - (pstarts - starts)[blk_e][:, None]
    valid = src_i < (starts + counts)[blk_e][:, None]
    row_src = jnp.where(valid, order[jnp.minimum(src_i, a - 1)] // TOP_K, 0).astype(jnp.int32).reshape(r)
    n_used = (pends[-1] // mb).astype(jnp.int32).reshape(1)
    xs = xn.at[row_src].get(mode="promise_in_bounds")
    f2 = mlp1_w.shape[2]
    ys = moe_experts(xs, blk_e, n_used, mlp1_w, mlp1_b.reshape(N_EXPERTS, 1, f2), mlp2_w,
                     mlp2_b.reshape(N_EXPERTS, 1, d))
    ys_k = [ys.at[pos[:, k]].get(mode="promise_in_bounds") for k in range(TOP_K)]
    out = moe_combine(h2, gate, ys_k, 256)
    return out.reshape(b, s, d)


def kernel(x, mem, norm_mix_g, w_in, q_norm_g, k_cmp_norm_g, k_slc_norm_g, k_win_norm_g, cmp_pe_k, cmp_pe_v, cmp_k_w1, cmp_k_w2, cmp_v_w1, cmp_v_w2, rwkv_mu, rwkv_w0, rwkv_w_up, rwkv_a0, rwkv_a_up, rwkv_g_up, rwkv_k_k, rwkv_k_a, rwkv_r_k, rwkv_ln_w, rwkv_ln_b, w_out, norm_x_g, norm_mem_g, xq_w, xk_w, xv_w, xq_norm_g, xk_norm_g, xo_w, norm_ffn_g, router_w, router_b, mlp1_w, mlp1_b, mlp2_w, mlp2_b):
    params = (norm_mix_g, w_in, q_norm_g, k_cmp_norm_g, k_slc_norm_g, k_win_norm_g, cmp_pe_k, cmp_pe_v,
              cmp_k_w1, cmp_k_w2, cmp_v_w1, cmp_v_w2, rwkv_mu, rwkv_w0, rwkv_w_up, rwkv_a0, rwkv_a_up,
              rwkv_g_up, rwkv_k_k, rwkv_k_a, rwkv_r_k, rwkv_ln_w, rwkv_ln_b, w_out, norm_x_g, norm_mem_g,
              xq_w, xk_w, xv_w, xq_norm_g, xk_norm_g, xo_w, norm_ffn_g, router_w, router_b,
              mlp1_w, mlp1_b, mlp2_w, mlp2_b)
    h = x
    for layer in range(norm_mix_g.shape[0]):
        h = _layer(h, mem, *[prm[layer] for prm in params])
    return h
```

```python
import functools

import numpy as np
import jax
import jax.numpy as jnp
from jax import lax
from jax.experimental import pallas as pl
from jax.experimental.pallas import tpu as pltpu

F32 = jnp.float32
BF16 = jnp.bfloat16
HIGHEST = lax.Precision.HIGHEST

V7X_VMEM_BYTES = 64 * 1024 * 1024
VMEM_LIMIT = V7X_VMEM_BYTES * 3 // 4

HEAD_DIM = 64
NSA_HEADS = 8
NSA_GROUPS = 2
NSA_HPG = NSA_HEADS // NSA_GROUPS
GROUP_W = NSA_HPG * HEAD_DIM
CMP_BLOCK = 32
CMP_STRIDE = 16
SLC_BLOCK = 64
SLC_TOPK = 16
WINDOW = 512
Q_BLOCK = 128
SEL_CHUNK = 512
RWKV_HEADS = 8
RWKV_DIM = RWKV_HEADS * HEAD_DIM
RWKV_CHUNK = 64
GN_EPS = HEAD_DIM * 1e-5
X_HEADS = 4
N_EXPERTS = 32
TOP_K = 4
SWIGLU_LIMIT = 7.0
SWIGLU_ALPHA = 1.702
MOE_ROW_BLOCK = 256
RMS_EPS = 1e-6
NEG_INF = -1e30
BIG = 1e9
REMOVED = -3e38
LANES = 128

NSA_PROJ = NSA_HEADS * HEAD_DIM + 6 * NSA_GROUPS * HEAD_DIM + NSA_HEADS * 3
NSA_PROJ_PAD = -(-NSA_PROJ // LANES) * LANES
RWKV_PROJ = 3 * RWKV_DIM + 64 + 64 + 128


def _cparams(*sem):
    return pltpu.CompilerParams(dimension_semantics=sem, vmem_limit_bytes=VMEM_LIMIT)


def _dot(a, b, **kw):
    return jnp.dot(a, b, preferred_element_type=F32, **kw)


def _dot_nt(a, b, **kw):
    return lax.dot_general(a, b, (((1,), (1,)), ((), ())), preferred_element_type=F32, **kw)


def _dot_tn(a, b, **kw):
    return lax.dot_general(a, b, (((0,), (0,)), ((), ())), preferred_element_type=F32, **kw)


def _rms(x, g):
    return x * lax.rsqrt(jnp.mean(x * x, axis=-1, keepdims=True) + RMS_EPS) * g


def _split_bf16(x, terms=3):
    parts = []
    for _ in range(terms):
        hi = x.astype(BF16)
        parts.append(hi)
        x = x - hi.astype(F32)
    return parts


def _dot_x_sel(x, sel):
    return sum(_dot(p, sel) for p in _split_bf16(x))


def _dot_sel_x(sel, x):
    return sum(_dot(sel, p) for p in _split_bf16(x))


def _block_diag_ones(n, blk, scale=1.0):
    i = np.arange(n)
    return jnp.asarray(((i[:, None] // blk) == (i[None, :] // blk)).astype(np.float32) * scale, BF16)


def _norm_matmul_kernel(x_ref, g_ref, *refs):
    n = len(refs) // 2
    xn = _rms(x_ref[...], g_ref[...]).astype(BF16)
    for w_ref, o_ref in zip(refs[:n], refs[n:]):
        o_ref[...] = _dot(xn, w_ref[...])


def norm_matmul(x, g, ws, tm):
    m, d = x.shape
    return pl.pallas_call(
        _norm_matmul_kernel,
        grid=(m // tm,),
        in_specs=[pl.BlockSpec((tm, d), lambda i: (i, 0)), pl.BlockSpec((1, d), lambda i: (0, 0))]
                 + [pl.BlockSpec((d, w.shape[1]), lambda i: (0, 0)) for w in ws],
        out_specs=[pl.BlockSpec((tm, w.shape[1]), lambda i: (i, 0)) for w in ws],
        out_shape=[jax.ShapeDtypeStruct((m, w.shape[1]), F32) for w in ws],
        compiler_params=_cparams("parallel"),
        name="norm_matmul",
    )(x, g.reshape(1, d), *[w.astype(BF16) for w in ws])


def _nsa_prep_kernel(p_ref, qg_ref, ksg_ref, kwg_ref, bdq_ref, bdk_ref, place_ref, feat_ref,
                     qt_o, ks_o, vst_o, kw_o, vwt_o, gate_o):
    p = p_ref[...]
    tm = p.shape[0]
    nq = NSA_HEADS * HEAD_DIM
    gw = NSA_GROUPS * HEAD_DIM
    q = p[:, :nq]
    msq = _dot_x_sel(q * q, bdq_ref[...])
    qn = q * lax.rsqrt(msq + RMS_EPS) * qg_ref[...] * (HEAD_DIM ** -0.5)
    qt_o[0] = qn.T.reshape(NSA_HEADS, HEAD_DIM, tm).astype(BF16)

    def seg(k):
        return p[:, nq + k * gw: nq + (k + 1) * gw]

    def head_norm(t, g):
        ms = _dot_x_sel(t * t, bdk_ref[...])
        return t * lax.rsqrt(ms + RMS_EPS) * g

    feat = feat_ref[...].astype(F32)
    ks = head_norm(seg(2), ksg_ref[...]).astype(BF16)
    kw = head_norm(seg(4), kwg_ref[...]).astype(BF16)
    vs_t = seg(3).T
    vw_t = seg(5).T
    for g in range(NSA_GROUPS):
        ks_o[0, g] = (_dot(ks, place_ref[g]) + feat).astype(BF16)
        kw_o[0, g] = (_dot(kw, place_ref[g])[:, :2 * HEAD_DIM] + feat[:, :2 * HEAD_DIM]).astype(BF16)
        for j in range(tm // Q_BLOCK):
            tile = (slice(g * HEAD_DIM, (g + 1) * HEAD_DIM), slice(j * Q_BLOCK, (j + 1) * Q_BLOCK))
            vst_o[0, g, j] = vs_t[tile].astype(BF16)
            vwt_o[0, g, j] = vw_t[tile].astype(BF16)
    gate_o[...] = jax.nn.sigmoid(p[:, nq + 6 * gw: nq + 6 * gw + LANES])


def nsa_prep(p_nsa, q_g, ks_g, kw_g, b, s, tm):
    t = p_nsa.shape[0]
    nq = NSA_HEADS * HEAD_DIM
    gw = NSA_GROUPS * HEAD_DIM
    grp, dh = NSA_GROUPS, HEAD_DIM
    tps = s // tm
    kx = 2 * dh + LANES
    assert s // SLC_BLOCK <= LANES
    tok = np.arange(s)
    feat = np.zeros((s, kx), np.float32)
    feat[:, dh] = tok // SLC_BLOCK * SLC_BLOCK
    feat[:, dh + 1] = tok % SLC_BLOCK
    feat[:, dh + 2:dh + 4] = 1.0
    feat[tok, 2 * dh + tok // SLC_BLOCK] = 1.0
    place = np.zeros((grp, gw, kx), np.float32)
    for g in range(grp):
        place[g, g * dh + np.arange(dh), np.arange(dh)] = 1.0
    tile = lambda v, n: jnp.tile(v.reshape(1, HEAD_DIM), (1, n))
    row = lambda w: pl.BlockSpec((tm, w), lambda i: (i, 0))
    full = lambda *sh: pl.BlockSpec(sh, lambda i: (0,) * len(sh))
    seq = lambda *sh: pl.BlockSpec((1, grp) + sh, lambda i: (i // tps, 0, i % tps) + (0,) * (len(sh) - 1))
    return pl.pallas_call(
        _nsa_prep_kernel,
        grid=(t // tm,),
        in_specs=[row(NSA_PROJ_PAD), full(1, nq), full(1, gw), full(1, gw), full(nq, nq), full(gw, gw),
                  full(grp, gw, kx), pl.BlockSpec((tm, kx), lambda i: (i % tps, 0))],
        out_specs=[pl.BlockSpec((1, NSA_HEADS, dh, tm), lambda i: (i // tps, 0, 0, i % tps)),
                   seq(tm, kx), seq(tm // Q_BLOCK, dh, Q_BLOCK), seq(tm, 2 * dh), seq(tm // Q_BLOCK, dh, Q_BLOCK),
                   row(LANES)],
        out_shape=[jax.ShapeDtypeStruct((b, NSA_HEADS, dh, s), BF16),
                   jax.ShapeDtypeStruct((b, grp, s, kx), BF16),
                   jax.ShapeDtypeStruct((b, grp, s // Q_BLOCK, dh, Q_BLOCK), BF16),
                   jax.ShapeDtypeStruct((b, grp, s, 2 * dh), BF16),
                   jax.ShapeDtypeStruct((b, grp, s // Q_BLOCK, dh, Q_BLOCK), BF16),
                   jax.ShapeDtypeStruct((t, LANES), F32)],
        compiler_params=_cparams("parallel"),
        name="nsa_prep",
    )(p_nsa, tile(q_g, NSA_HEADS), tile(ks_g, NSA_GROUPS), tile(kw_g, NSA_GROUPS),
      _block_diag_ones(nq, HEAD_DIM, 1.0 / HEAD_DIM), _block_diag_ones(gw, HEAD_DIM, 1.0 / HEAD_DIM),
      jnp.asarray(place, BF16), jnp.asarray(feat, BF16))


def _compress_kernel(ch_ref, pe_ref, w1_ref, w2_ref, g_ref, o_ref):
    ch = ch_ref[0, 0, 0]
    nc = ch.shape[0]
    half = CMP_STRIDE * HEAD_DIM
    nxt = pltpu.roll(ch, nc - 1, 0)
    w1 = w1_ref[0]
    h1 = (_dot(ch, w1[:half], precision=HIGHEST) + _dot(nxt, w1[half:], precision=HIGHEST)
          + _dot(pe_ref[0], w1, precision=HIGHEST))
    out = _dot(jax.nn.silu(h1), w2_ref[0], precision=HIGHEST)
    out = jnp.where(pl.program_id(0) == 0, _rms(out, g_ref[...]), out)
    o_ref[0, 0, 0] = out.astype(BF16)


def nsa_compress(kc, vc, pe_k, pe_v, kw1, kw2, vw1, vw2, kc_g, b, s):
    nc = s // CMP_STRIDE
    half = CMP_STRIDE * HEAD_DIM

    def chunks(t):
        return t.reshape(b, nc, CMP_STRIDE, NSA_GROUPS, HEAD_DIM).transpose(0, 3, 1, 2, 4).reshape(
            b, NSA_GROUPS, nc, half)

    ch = jnp.stack([chunks(kc), chunks(vc)])
    pe = jnp.stack([pe_k.reshape(1, 2 * half), pe_v.reshape(1, 2 * half)])
    return pl.pallas_call(
        _compress_kernel,
        grid=(2, b, NSA_GROUPS),
        in_specs=[pl.BlockSpec((1, 1, 1, nc, half), lambda kv, bi, g: (kv, bi, g, 0, 0)),
                  pl.BlockSpec((1, 1, 2 * half), lambda kv, bi, g: (kv, 0, 0)),
                  pl.BlockSpec((1, 2 * half, HEAD_DIM), lambda kv, bi, g: (kv, 0, 0)),
                  pl.BlockSpec((1, HEAD_DIM, HEAD_DIM), lambda kv, bi, g: (kv, 0, 0)),
                  pl.BlockSpec((1, HEAD_DIM), lambda kv, bi, g: (0, 0))],
        out_specs=pl.BlockSpec((1, 1, 1, nc, HEAD_DIM), lambda kv, bi, g: (kv, bi, g, 0, 0)),
        out_shape=jax.ShapeDtypeStruct((2, b, NSA_GROUPS, nc, HEAD_DIM), BF16),
        compiler_params=_cparams("parallel", "parallel", "parallel"),
        name="nsa_compress",
    )(ch, pe, jnp.stack([kw1, vw1]), jnp.stack([kw2, vw2]), kc_g.reshape(1, HEAD_DIM))


def _masked_exp_cols(s, mask):
    sm = jnp.where(mask, s, NEG_INF)
    m = jnp.max(sm, axis=0, keepdims=True)
    p = jnp.exp(sm - jnp.where(m > 0.5 * NEG_INF, m, 0.0))
    l = jnp.sum(p, axis=0, keepdims=True)
    return p, 1.0 / jnp.where(l > 0.0, l, 1.0)


def _nsa_attn_kernel(qt_ref, kc_ref, vct_ref, ovt_ref, ks_ref, vst_ref, kw_ref, vwt_ref, gate_ref, o_ref,
                     *, n_sel, n_top):
    g = pl.program_id(1)
    t0 = pl.program_id(2) * Q_BLOCK
    ks_ref, vst_ref, kw_ref, vwt_ref = (r.at[0, 0] for r in (ks_ref, vst_ref, kw_ref, vwt_ref))
    cols = NSA_HPG * Q_BLOCK
    col = lax.broadcasted_iota(jnp.int32, (1, cols), 1)
    tq = t0 + col % Q_BLOCK
    head = g * NSA_HPG + col // Q_BLOCK
    slope = lax.bitcast_convert_type((127 - (head + 1)) << 23, F32)
    qt = jnp.concatenate([qt_ref[0, h] for h in range(NSA_HPG)], axis=1)
    frow = lax.broadcasted_iota(jnp.int32, (HEAD_DIM, cols), 0)
    tq_hi = (tq // SLC_BLOCK * SLC_BLOCK).astype(F32)
    tq_lo = (tq % SLC_BLOCK).astype(F32)
    qpos = jnp.where(frow < 2, slope, jnp.where(frow == 2, -slope * tq_hi, jnp.where(frow == 3, -slope * tq_lo, 0.0)))
    qc = jnp.concatenate([qt, qpos.astype(BF16)], axis=0)

    def heads_sum(x):
        acc = x[:, :Q_BLOCK]
        for h in range(1, NSA_HPG):
            acc = acc + x[:, h * Q_BLOCK:(h + 1) * Q_BLOCK]
        return acc

    kc = kc_ref[0, 0]
    nc = kc.shape[0]
    c_last = lax.broadcasted_iota(jnp.int32, (nc, 1), 0) * CMP_STRIDE + (CMP_BLOCK - 1)
    p_c, inv_c = _masked_exp_cols(_dot(kc, qc), c_last <= tq)
    p_c = p_c * inv_c
    o_c = _dot(vct_ref[0, 0], p_c.astype(BF16))
    imp = _dot_sel_x(ovt_ref[...], heads_sum(p_c))

    bid = lax.broadcasted_iota(jnp.int32, (LANES, Q_BLOCK), 0)
    bidf = bid.astype(F32)
    tq1 = t0 + lax.broadcasted_iota(jnp.int32, (1, Q_BLOCK), 1)
    cur = tq1 // SLC_BLOCK
    forced = (bid == 0) | (bid == cur) | (bid == cur - 1)
    score = jnp.where(forced, BIG, jnp.where(bid * SLC_BLOCK <= tq1, imp, -BIG))
    score = jnp.where(bid < n_sel, score, REMOVED)
    picked = jnp.zeros((LANES, Q_BLOCK), jnp.bool_)
    for _ in range(n_top):
        m = jnp.max(score, axis=0, keepdims=True)
        pick = bidf == jnp.min(jnp.where(score == m, bidf, float(LANES)), axis=0, keepdims=True)
        picked = picked | pick
        score = jnp.where(pick, REMOVED, score)
    sel_neg = jnp.where(picked, 0.0, NEG_INF).astype(BF16)
    qx = jnp.concatenate([qc, jnp.concatenate([sel_neg] * NSA_HPG, axis=1)], axis=0)

    def values_t(ref, first_tile, n_tiles):
        return jnp.concatenate([ref[first_tile + i] for i in range(n_tiles)], axis=1)

    span = WINDOW + Q_BLOCK
    ws = pl.multiple_of(jnp.maximum(t0 - WINDOW, 0), Q_BLOCK)
    dw = tq - (ws + lax.broadcasted_iota(jnp.int32, (span, 1), 0))
    in_window = dw.astype(jnp.uint32) < WINDOW
    p_w, inv_w = _masked_exp_cols(_dot(kw_ref[pl.ds(ws, span), :], qc), in_window)
    o_w = _dot(values_t(vwt_ref, ws // Q_BLOCK, span // Q_BLOCK), p_w.astype(BF16)) * inv_w

    def scores(chunk):
        return _dot(ks_ref[pl.ds(pl.multiple_of(chunk * SEL_CHUNK, SEL_CHUNK), SEL_CHUNK), :], qx)

    def flash(s, chunk, carry):
        m, l, acc = carry
        m_new = jnp.maximum(m, jnp.max(s, axis=0, keepdims=True))
        alpha = jnp.exp(m - m_new)
        p = jnp.exp(s - m_new)
        l = alpha * l + jnp.sum(p, axis=0, keepdims=True)
        v = values_t(vst_ref, chunk * (SEL_CHUNK // Q_BLOCK), SEL_CHUNK // Q_BLOCK)
        return m_new, l, alpha * acc + _dot(v, p.astype(BF16))

    def flash_pair(chunk_a, chunk_b, carry, keep_b=None):
        s_a, s_b = scores(chunk_a), scores(chunk_b)
        if keep_b is not None:
            s_b = jnp.where(keep_b, s_b, NEG_INF)
        return flash(s_b, chunk_b, flash(s_a, chunk_a, carry))

    bpc = SEL_CHUNK // SLC_BLOCK
    blk_any = jnp.max(jnp.where(picked, 1.0, 0.0), axis=1, keepdims=True)
    chunk_bit = lax.bitcast_convert_type((bid[:, :1] // bpc + 127) << 23, F32)
    bits = jnp.max((blk_any * chunk_bit).reshape(LANES // bpc, bpc, 1), axis=1)
    active = jnp.sum(bits, axis=0, keepdims=True)[0, 0].astype(jnp.int32)

    def full_step(j, state):
        def on_active(st):
            return lax.cond(st[0] >= 0,
                            lambda s2: (jnp.int32(-1), flash_pair(s2[0], j, s2[1])),
                            lambda s2: (j, s2[1]), st)
        return lax.cond((active >> j) & 1 == 1, on_active, lambda st: st, state)

    n_full = t0 // SEL_CHUNK
    init = (jnp.full((1, cols), NEG_INF, F32), jnp.zeros((1, cols), F32), jnp.zeros((HEAD_DIM, cols), F32))
    waiting, carry = lax.fori_loop(0, n_full, full_step, (jnp.int32(-1), init))
    keep = n_full * SEL_CHUNK + lax.broadcasted_iota(jnp.int32, (SEL_CHUNK, 1), 0) <= tq
    _, l_s, acc_s = lax.cond(
        waiting >= 0,
        lambda c: flash_pair(waiting, n_full, c, keep_b=keep),
        lambda c: flash(jnp.where(keep, scores(n_full), NEG_INF), n_full, c), carry)
    o_s = acc_s / l_s

    for h in range(NSA_HPG):
        hs = slice(h * Q_BLOCK, (h + 1) * Q_BLOCK)
        gate = lambda br: gate_ref[0, 0, br, h:h + 1, :]
        o_ref[0, h] = gate(0) * o_c[:, hs] + gate(1) * o_s[:, hs] + gate(2) * o_w[:, hs]


def nsa_attention(qt, kcv, ks_x, vs_t, kw_x, vw_t, gates, b, s):
    nq_blocks = s // Q_BLOCK
    nc = s // CMP_STRIDE
    n_cmp = nc - CMP_BLOCK // CMP_STRIDE + 1
    n_sel = s // SLC_BLOCK
    n_top = min(SLC_TOPK, n_sel)
    grp, dh = NSA_GROUPS, HEAD_DIM
    assert n_sel <= LANES and s % SEL_CHUNK == 0 and s >= WINDOW + Q_BLOCK
    c_start = np.arange(n_cmp) * CMP_STRIDE
    s_start = np.arange(n_sel) * SLC_BLOCK
    ovt = np.zeros((LANES, nc), np.float32)
    ovt[:n_sel, :n_cmp] = (np.clip(np.minimum((c_start + CMP_BLOCK)[:, None], s_start[None] + SLC_BLOCK)
                                   - np.maximum(c_start[:, None], s_start[None]), 0, None) / CMP_BLOCK).T

    c_pos = np.arange(nc) * CMP_STRIDE + (CMP_BLOCK - 1)
    c_feat = np.zeros((nc, dh), np.float32)
    c_feat[:, 0] = c_pos // SLC_BLOCK * SLC_BLOCK
    c_feat[:, 1] = c_pos % SLC_BLOCK
    c_feat[:, 2:4] = 1.0
    kc_x = jnp.concatenate([kcv[0], jnp.broadcast_to(jnp.asarray(c_feat, BF16), kcv[0].shape)], axis=-1)
    vct = kcv[1].transpose(0, 1, 3, 2)
    gates_t = gates[:, :NSA_HEADS * 3].reshape(b, s, grp, NSA_HPG, 3).transpose(0, 2, 4, 3, 1)

    grp_spec = lambda *shape: pl.BlockSpec((1, 1) + shape, lambda bi, g, i: (bi, g) + (0,) * len(shape),
                                           pipeline_mode=pl.Buffered(1))
    return pl.pallas_call(
        functools.partial(_nsa_attn_kernel, n_sel=n_sel, n_top=n_top),
        grid=(b, grp, nq_blocks),
        in_specs=[pl.BlockSpec((1, NSA_HPG, dh, Q_BLOCK), lambda bi, g, i: (bi, g, 0, i)),
                  grp_spec(nc, 2 * dh), grp_spec(dh, nc),
                  pl.BlockSpec((LANES, nc), lambda bi, g, i: (0, 0)),
                  grp_spec(s, 2 * dh + LANES), grp_spec(s // Q_BLOCK, dh, Q_BLOCK),
                  grp_spec(s, 2 * dh), grp_spec(s // Q_BLOCK, dh, Q_BLOCK),
                  pl.BlockSpec((1, 1, 3, NSA_HPG, Q_BLOCK), lambda bi, g, i: (bi, g, 0, 0, i))],
        out_specs=pl.BlockSpec((1, NSA_HPG, dh, Q_BLOCK), lambda bi, g, i: (bi, g, 0, i)),
        out_shape=jax.ShapeDtypeStruct((b, NSA_HEADS, dh, s), F32),
        compiler_params=_cparams("parallel", "parallel", "arbitrary"),
        name="nsa_attention",
    )(qt, kc_x, vct, jnp.asarray(ovt, BF16), ks_x, vs_t, kw_x, vw_t, gates_t)


def _rwkv_prep_kernel(p_ref, prev_ref, mu_ref, w0_ref, a0_ref, kk_ref, ka_ref, rk_ref,
                      wup_ref, aup_ref, gup_ref, bd_ref, ltri_ref, lones_ref, csum_ref,
                      at_o, bt_o, kt_o, rt_o, v_o, bw_o, kw_o, wc_o, g_o, bonus_o, *, tiles_per_seq):
    p = p_ref[...]
    tm = p.shape[0]
    first = pl.program_id(0) % tiles_per_seq == 0
    last_prev = jnp.where(first, 0.0, prev_ref[7:8, :])
    prev = pltpu.roll(p, 1, 0)
    prev = jnp.where(lax.broadcasted_iota(jnp.int32, (tm, 1), 0) == 0, last_prev, prev)
    pm = p + (prev - p) * mu_ref[...]
    d = RWKV_DIM
    r, k, v = pm[:, :d], pm[:, d:2 * d], pm[:, 2 * d:3 * d]
    lora = pm[:, 3 * d:3 * d + LANES]
    gd = pm[:, 3 * d + LANES:3 * d + 2 * LANES]
    z = -(w0_ref[...] + _dot(jnp.tanh(lora).astype(BF16), wup_ref[...]))
    softplus = jnp.maximum(z, 0.0) + jnp.log(1.0 + jnp.exp(-jnp.abs(z)))
    w = -softplus - 0.5
    a = jax.nn.sigmoid(a0_ref[...] + _dot(lora.astype(BF16), aup_ref[...]))
    g_o[...] = _dot(jax.nn.sigmoid(gd).astype(BF16), gup_ref[...])
    bd = bd_ref[...]
    kkr = k * kk_ref[...]
    kk = kkr / jnp.maximum(jnp.sqrt(_dot_x_sel(kkr * kkr, bd)), 1e-12)
    k2 = k * (1.0 + (a - 1.0) * ka_ref[...])
    bonus_o[...] = _dot_x_sel(r * k2 * rk_ref[...], bd) * v
    lw = -jnp.exp(w)
    lw_parts = _split_bf16(lw)
    cum = sum(_dot(ltri_ref[...], p) for p in lw_parts)
    tot = sum(_dot(lones_ref[...], p) for p in lw_parts)
    e_in = jnp.exp(cum)
    e_out = jnp.exp(-cum)
    e_end = jnp.exp(tot - cum)

    def put_heads(o, val):
        for h in range(RWKV_HEADS):
            o[0, h] = val[:, h * HEAD_DIM:(h + 1) * HEAD_DIM].astype(o.dtype)

    put_heads(at_o, -kk * jnp.exp(cum - lw))
    put_heads(bt_o, kk * a * e_out)
    put_heads(kt_o, k2 * e_out)
    put_heads(rt_o, r * e_in)
    put_heads(v_o, v)
    put_heads(bw_o, kk * a * e_end)
    put_heads(kw_o, k2 * e_end)
    put_heads(wc_o, jnp.exp(sum(_dot(csum_ref[...], p) for p in lw_parts)))


def rwkv_prep(p_rwkv, mu, w0, w_up, a0, a_up, g_up, k_k, k_a, r_k, b, s, tm):
    t = p_rwkv.shape[0]
    d = RWKV_DIM
    c = RWKV_CHUNK
    tps = s // tm
    cpt = tm // c
    wup = jnp.concatenate([w_up, jnp.zeros_like(a_up)], axis=0).astype(BF16)
    aup = jnp.concatenate([jnp.zeros_like(w_up), a_up], axis=0).astype(BF16)
    i = np.arange(tm)
    same = (i[:, None] // c) == (i[None, :] // c)
    ltri = jnp.asarray(same & (i[:, None] >= i[None, :]), BF16)
    lones = jnp.asarray(same, BF16)
    csum = jnp.asarray(np.arange(cpt)[:, None] == (i[None, :] // c), BF16)
    row = lambda w: pl.BlockSpec((tm, w), lambda i: (i, 0))
    full = lambda *sh: pl.BlockSpec(sh, lambda i: (0,) * len(sh))
    heads = lambda n: pl.BlockSpec((1, RWKV_HEADS, n, HEAD_DIM), lambda i: (i // tps, 0, i % tps, 0))
    hshape = lambda n, dt: jax.ShapeDtypeStruct((b, RWKV_HEADS, n, HEAD_DIM), dt)
    vec = lambda x: x.reshape(1, -1)
    return pl.pallas_call(
        functools.partial(_rwkv_prep_kernel, tiles_per_seq=tps),
        grid=(t // tm,),
        in_specs=[row(RWKV_PROJ),
                  pl.BlockSpec((8, RWKV_PROJ), lambda i: (jnp.maximum(i * (tm // 8) - 1, 0), 0)),
                  full(1, RWKV_PROJ), full(1, d), full(1, d), full(1, d), full(1, d), full(1, d),
                  full(LANES, d), full(LANES, d), full(LANES, d), full(d, d), full(tm, tm), full(tm, tm),
                  full(cpt, tm)],
        out_specs=[heads(tm)] * 7 + [heads(cpt), row(d), row(d)],
        out_shape=[hshape(s, BF16)] * 7 + [hshape(s // c, F32)] + [jax.ShapeDtypeStruct((t, d), F32)] * 2,
        compiler_params=_cparams("parallel"),
        name="rwkv_prep",
    )(p_rwkv, p_rwkv, vec(mu), vec(w0), vec(a0), vec(k_k), vec(k_a), vec(r_k), wup, aup, g_up.astype(BF16),
      _block_diag_ones(d, HEAD_DIM), ltri, lones, csum)


def _bdot(a, b):
    return lax.dot_general(a, b, (((2,), (1,)), ((0,), (0,))), preferred_element_type=F32)


def _bdot_nt(a, b):
    return lax.dot_general(a, b, (((2,), (2,)), ((0,), (0,))), preferred_element_type=F32)


def _bdot_tn(a, b):
    return lax.dot_general(a, b, (((1,), (1,)), ((0,), (0,))), preferred_element_type=F32)


def _rwkv_intra_kernel(at_ref, bt_ref, kt_ref, rt_ref, v_ref, ta_o, tr_o, arb_o, yv_o):
    c = RWKV_CHUNK
    _, nh, ts, dh = at_ref.shape
    n = nh * (ts // c)
    chunked = lambda ref: ref[0].reshape(n, c, dh)
    at, bt, kt, rt, v = (chunked(r) for r in (at_ref, bt_ref, kt_ref, rt_ref, v_ref))
    ri = lax.broadcasted_iota(jnp.int32, (1, c, c), 1)
    ci = lax.broadcasted_iota(jnp.int32, (1, c, c), 2)
    strict = ri > ci
    incl = ri >= ci
    ar = jnp.concatenate([at, rt], axis=1)
    xb = _bdot_nt(ar, bt)
    xk = _bdot_nt(ar, kt)
    l_ab = jnp.where(strict, xb[:, :c], 0.0)
    a_ak = jnp.where(strict, xk[:, :c], 0.0)
    a_rb = jnp.where(incl, xb[:, c:], 0.0)
    a_rk = jnp.where(incl, xk[:, c:], 0.0)
    pw = l_ab
    tinv = jnp.where(ri == ci, 1.0, 0.0) + l_ab
    for _ in range(int(np.log2(c)) - 1):
        pw_b = pw.astype(BF16)
        pw = _bdot(pw_b, pw_b)
        tinv = tinv + _bdot(tinv.astype(BF16), pw.astype(BF16))
    tinv_b = tinv.astype(BF16)

    def put(o, val):
        o[0] = val.reshape(nh, ts, val.shape[-1]).astype(o.dtype)

    put(ta_o, _bdot(tinv_b, at))
    put(tr_o, _bdot(tinv_b, _bdot(a_ak.astype(BF16), v).astype(BF16)))
    put(arb_o, a_rb)
    put(yv_o, _bdot(a_rk.astype(BF16), v))


def rwkv_intra(at, bt, kt, rt, v, ts):
    b, h, s, dh = at.shape
    seq = lambda: pl.BlockSpec((1, h, ts, dh), lambda bi, i: (bi, 0, i, 0))
    shp = lambda dt: jax.ShapeDtypeStruct((b, h, s, dh), dt)
    return pl.pallas_call(
        _rwkv_intra_kernel,
        grid=(b, s // ts),
        in_specs=[seq()] * 5,
        out_specs=[seq()] * 4,
        out_shape=[shp(BF16), shp(F32), shp(BF16), shp(F32)],
        compiler_params=_cparams("parallel", "parallel"),
        name="rwkv_intra",
    )(at, bt, kt, rt, v)


def _rwkv_scan_kernel(ta_ref, tr_ref, arb_ref, yv_ref, rt_ref, v_ref, bw_ref, kw_ref, wc_ref, y_ref, st_ref):
    c = RWKV_CHUNK
    nb, nh, ts, dh = ta_ref.shape
    n = nb * nh

    @pl.when(pl.program_id(0) == 0)
    def _():
        st_ref[...] = jnp.zeros_like(st_ref)

    def chunk_step(j, _):
        sl = (slice(None), slice(None), pl.ds(pl.multiple_of(j * c, c), c), slice(None))
        get = lambda ref: ref[sl].reshape(n, c, dh)
        st = st_ref[...]
        st_b = st.astype(BF16)
        u = _bdot_nt(get(ta_ref), st_b) + get(tr_ref)
        u_b = u.astype(BF16)
        y = _bdot_nt(get(rt_ref), st_b) + _bdot(get(arb_ref), u_b) + get(yv_ref)
        wc = wc_ref[:, :, pl.ds(pl.program_id(0) * (ts // c) + j, 1), :].reshape(n, 1, dh)
        st_ref[...] = st * wc + _bdot_tn(jnp.concatenate([u_b, get(v_ref)], axis=1),
                                         jnp.concatenate([get(bw_ref), get(kw_ref)], axis=1))
        y_ref[sl] = y.reshape(nb, nh, c, dh)
        return 0

    lax.fori_loop(0, ts // c, chunk_step, 0)


def rwkv_scan(ta, tr, arb, yv, rt, v, bw, kw, wc, ts):
    b, h, s, dh = ta.shape
    seq = lambda n: pl.BlockSpec((b, h, n, dh), lambda i: (0, 0, i, 0))
    return pl.pallas_call(
        _rwkv_scan_kernel,
        grid=(s // ts,),
        in_specs=[seq(ts)] * 8 + [pl.BlockSpec(wc.shape, lambda i: (0, 0, 0, 0))],
        out_specs=seq(ts),
        out_shape=jax.ShapeDtypeStruct((b, h, s, dh), F32),
        scratch_shapes=[pltpu.VMEM((b * h, dh, dh), F32)],
        compiler_params=_cparams("arbitrary"),
        name="rwkv_scan",
    )(ta, tr, arb, yv, rt, v, bw, kw, wc)


def _out_proj_kernel(x_ref, on_ref, y_ref, bonus_ref, g_ref, lnw_ref, lnb_ref, bd_ref, wn_ref, wr_ref, o_ref):
    y = jnp.concatenate([y_ref[0, h] for h in range(RWKV_HEADS)], axis=-1)
    bd = bd_ref[...]
    yc = y - _dot_x_sel(y, bd)
    yn = yc * lax.rsqrt(_dot_x_sel(yc * yc, bd) + GN_EPS)
    o_rwkv = (yn * lnw_ref[...] + lnb_ref[...] + bonus_ref[...]) * g_ref[...]
    tm = y.shape[0]
    o_nsa_t = on_ref[0].reshape(NSA_HEADS * HEAD_DIM, tm)
    o_ref[...] = (x_ref[...] + _dot_tn(o_nsa_t.astype(BF16), wn_ref[...])
                  + _dot(o_rwkv.astype(BF16), wr_ref[...]))


def out_proj(x, o_nsa_t, y, bonus, g, ln_w, ln_b, w_out, s, tm):
    t, d = x.shape
    dn = o_nsa_t.shape[1] * o_nsa_t.shape[2]
    dr = bonus.shape[1]
    tps = s // tm
    row = lambda w: pl.BlockSpec((tm, w), lambda i: (i, 0))
    full = lambda *sh: pl.BlockSpec(sh, lambda i: (0,) * len(sh))
    return pl.pallas_call(
        _out_proj_kernel,
        grid=(t // tm,),
        in_specs=[row(d), pl.BlockSpec((1, NSA_HEADS, HEAD_DIM, tm), lambda i: (i // tps, 0, 0, i % tps)),
                  pl.BlockSpec((1, RWKV_HEADS, tm, HEAD_DIM), lambda i: (i // tps, 0, i % tps, 0)),
                  row(dr), row(dr), full(1, dr), full(1, dr), full(dr, dr), full(dn, d), full(dr, d)],
        out_specs=row(d),
        out_shape=jax.ShapeDtypeStruct((t, d), F32),
        compiler_params=_cparams("parallel"),
        name="out_proj",
    )(x, o_nsa_t, y, bonus, g, ln_w.reshape(1, dr), ln_b.reshape(1, dr),
      _block_diag_ones(dr, HEAD_DIM, 1.0 / HEAD_DIM), w_out[:dn].astype(BF16), w_out[dn:].astype(BF16))


def _cross_attn_kernel(h_ref, g_ref, wq_ref, qg_ref, kv_ref, kg_ref, wo_ref, o_ref):
    h = h_ref[...]
    d = h.shape[1]
    xd = d // X_HEADS
    q = _dot(_rms(h, g_ref[...]).astype(BF16), wq_ref[...])
    kv = kv_ref[0]
    outs = []
    for hd in range(X_HEADS):
        qh = _rms(q[:, hd * xd:(hd + 1) * xd], qg_ref[...]) * (xd ** -0.5)
        kh = _rms(kv[:, hd * xd:(hd + 1) * xd], kg_ref[...])
        vh = kv[:, d + hd * xd:d + (hd + 1) * xd]
        s = _dot_nt(qh.astype(BF16), kh.astype(BF16))
        p = jnp.exp(s - jnp.max(s, axis=-1, keepdims=True))
        p = p / jnp.sum(p, axis=-1, keepdims=True)
        outs.append(_dot(p.astype(BF16), vh.astype(BF16)))
    o = jnp.concatenate(outs, axis=-1)
    o_ref[...] = h + _dot(o.astype(BF16), wo_ref[...])


def cross_attention(h, kv, norm_g, xq_w, xq_g, xk_g, xo_w, b, s, tm):
    t, d = h.shape
    m = kv.shape[1]
    xd = d // X_HEADS
    tiles = s // tm
    full = lambda *sh: pl.BlockSpec(sh, lambda i: (0,) * len(sh))
    return pl.pallas_call(
        _cross_attn_kernel,
        grid=(t // tm,),
        in_specs=[pl.BlockSpec((tm, d), lambda i: (i, 0)), full(1, d), full(d, d), full(1, xd),
                  pl.BlockSpec((1, m, 2 * d), lambda i: (i // tiles, 0, 0)), full(1, xd), full(d, d)],
        out_specs=pl.BlockSpec((tm, d), lambda i: (i, 0)),
        out_shape=jax.ShapeDtypeStruct((t, d), F32),
        compiler_params=_cparams("parallel"),
        name="cross_attention",
    )(h, norm_g.reshape(1, d), xq_w.astype(BF16), xq_g.reshape(1, xd), kv, xk_g.reshape(1, xd),
      xo_w.astype(BF16))


def _router_kernel(h_ref, g_ref, rw_ref, rb_ref, ltri_ref, xn_o, idx_o, gate_o, rank_o, count_o, seen_ref):
    @pl.when(pl.program_id(0) == 0)
    def _():
        seen_ref[...] = jnp.zeros_like(seen_ref)

    xn = _rms(h_ref[...], g_ref[...])
    xn_o[...] = xn
    logits = _dot(xn, rw_ref[...], precision=HIGHEST) + rb_ref[...]
    tm = logits.shape[0]
    lane = lax.broadcasted_iota(jnp.int32, (tm, LANES), 1)
    lanef = lane.astype(F32)
    logits = jnp.where(lane < N_EXPERTS, logits, REMOVED)
    idx_acc = jnp.zeros((tm, LANES), F32)
    val_acc = jnp.zeros((tm, LANES), F32)
    chosen = jnp.zeros((tm, LANES), F32)
    picks = []
    top = None
    for k in range(TOP_K):
        m = jnp.max(logits, axis=-1, keepdims=True)
        idx = jnp.min(jnp.where(logits == m, lanef, float(LANES)), axis=-1, keepdims=True)
        pick = lanef == idx
        picks.append(pick)
        chosen = jnp.where(pick, 1.0, chosen)
        logits = jnp.where(pick, REMOVED, logits)
        top = m if top is None else top
        idx_acc = jnp.where(lane == k, idx, idx_acc)
        val_acc = jnp.where(lane == k, jnp.exp(m - top), val_acc)
    idx_o[...] = idx_acc.astype(jnp.int32)
    gate_o[...] = val_acc / jnp.sum(val_acc, axis=-1, keepdims=True)
    before = seen_ref[0:1, :] + _dot(ltri_ref[...], chosen.astype(BF16))
    rank_acc = jnp.zeros((tm, LANES), F32)
    for k, pick in enumerate(picks):
        rank_acc = jnp.where(lane == k, jnp.sum(jnp.where(pick, before, 0.0), axis=-1, keepdims=True), rank_acc)
    rank_o[...] = rank_acc.astype(jnp.int32)
    seen_ref[...] = seen_ref[...] + jnp.sum(chosen, axis=0, keepdims=True)
    count_o[...] = seen_ref[...]


def moe_router(h, norm_g, router_w, router_b, tm):
    t, d = h.shape
    rw = jnp.zeros((d, LANES), F32).at[:, :N_EXPERTS].set(router_w)
    rb = jnp.zeros((1, LANES), F32).at[0, :N_EXPERTS].set(router_b)
    i = np.arange(tm)
    ltri = jnp.asarray(i[:, None] > i[None, :], BF16)
    row = lambda w: pl.BlockSpec((tm, w), lambda i: (i, 0))
    full = lambda *s: pl.BlockSpec(s, lambda i: (0,) * len(s))
    return pl.pallas_call(
        _router_kernel,
        grid=(t // tm,),
        in_specs=[row(d), full(1, d), full(d, LANES), full(1, LANES), full(tm, tm)],
        out_specs=[row(d), row(LANES), row(LANES), row(LANES), full(8, LANES)],
        out_shape=[jax.ShapeDtypeStruct((t, d), F32), jax.ShapeDtypeStruct((t, LANES), jnp.int32),
                   jax.ShapeDtypeStruct((t, LANES), F32), jax.ShapeDtypeStruct((t, LANES), jnp.int32),
                   jax.ShapeDtypeStruct((8, LANES), F32)],
        scratch_shapes=[pltpu.VMEM((8, LANES), F32)],
        compiler_params=_cparams("arbitrary"),
        name="moe_router",
    )(h, norm_g.reshape(1, d), rw, rb, ltri)


def _expert_kernel(blk_e_ref, n_used_ref, x_ref, w1_ref, b1_ref, w2_ref, b2_ref, o_ref, w2x_ref):
    i = pl.program_id(0)
    f = w2_ref.shape[1]
    half_lanes = LANES // 2
    n_merged = f // LANES

    @pl.when((i == 0) | (blk_e_ref[i] != blk_e_ref[jnp.maximum(i - 1, 0)]))
    def _():
        for s in range(w2x_ref.shape[0]):
            lanes = slice(s * LANES, (s + 1) * LANES)
            for c in range(n_merged):
                lo = w2_ref[0, c * half_lanes:(c + 1) * half_lanes, lanes]
                hi = w2_ref[0, (c + n_merged) * half_lanes:(c + n_merged + 1) * half_lanes, lanes]
                w2x_ref[s, pl.ds(c * LANES, half_lanes, stride=2), :] = lo
                w2x_ref[s, pl.ds(c * LANES + 1, half_lanes, stride=2), :] = hi

    @pl.when(i < n_used_ref[0])
    def _():
        x = x_ref[...].astype(BF16)
        h = _dot(x, w1_ref[0].astype(BF16)) + b1_ref[0]
        hg = jnp.minimum(h, SWIGLU_LIMIT)
        gate = hg * jax.nn.sigmoid(SWIGLU_ALPHA * hg)
        lin = jnp.clip(h, -SWIGLU_LIMIT, SWIGLU_LIMIT) + 1.0
        even = lax.broadcasted_iota(jnp.int32, (1, LANES), 1) % 2 == 0
        def swiglu(c):
            cols = slice(c * LANES, (c + 1) * LANES)
            nxt = pltpu.roll(lin[:, cols], LANES - 1, 1)
            return jnp.where(even, gate[:, cols] * nxt, 0.0)

        act = jnp.concatenate([(swiglu(c) + pltpu.roll(swiglu(c + n_merged), 1, 1)).astype(BF16)
                               for c in range(n_merged)], axis=1)
        w2x = jnp.concatenate([w2x_ref[s] for s in range(w2x_ref.shape[0])], axis=1)
        o_ref[...] = _dot(act, w2x.astype(BF16)) + b2_ref[0]

    @pl.when(i >= n_used_ref[0])
    def _():
        o_ref[...] = jnp.zeros_like(o_ref)


def moe_experts(xs, blk_e, n_used, w1, b1, w2, b2):
    r, d = xs.shape
    f2 = w1.shape[2]
    m = MOE_ROW_BLOCK
    ex = lambda *s: pl.BlockSpec((1,) + s, lambda i, be, nu: (be[i],) + (0,) * len(s))
    grid_spec = pltpu.PrefetchScalarGridSpec(
        num_scalar_prefetch=2,
        grid=(r // m,),
        in_specs=[pl.BlockSpec((m, d), lambda i, be, nu: (i, 0)),
                  ex(d, f2), ex(1, f2), ex(f2 // 2, d), ex(1, d)],
        out_specs=pl.BlockSpec((m, d), lambda i, be, nu: (i, 0)),
        scratch_shapes=[pltpu.VMEM((d // LANES, f2 // 2, LANES), F32)],
    )
    return pl.pallas_call(
        _expert_kernel,
        grid_spec=grid_spec,
        out_shape=jax.ShapeDtypeStruct((r, d), F32),
        compiler_params=_cparams("arbitrary"),
        name="moe_experts",
    )(blk_e, n_used, xs, w1, b1, w2, b2)


def _combine_kernel(h_ref, gate_ref, *refs):
    *y_refs, o_ref = refs
    acc = h_ref[...]
    for k, y_ref in enumerate(y_refs):
        acc = acc + gate_ref[:, k:k + 1] * y_ref[...]
    o_ref[...] = acc


def moe_combine(h, gate, ys_k, tm):
    t, d = h.shape
    row = lambda w: pl.BlockSpec((tm, w), lambda i: (i, 0))
    return pl.pallas_call(
        _combine_kernel,
        grid=(t // tm,),
        in_specs=[row(d), row(LANES)] + [row(d)] * len(ys_k),
        out_specs=row(d),
        out_shape=jax.ShapeDtypeStruct((t, d), F32),
        compiler_params=_cparams("parallel"),
        name="moe_combine",
    )(h, gate, *ys_k)


def _layer(x, mem, norm_mix_g, w_in, q_norm_g, k_cmp_norm_g, k_slc_norm_g, k_win_norm_g,
           cmp_pe_k, cmp_pe_v, cmp_k_w1, cmp_k_w2, cmp_v_w1, cmp_v_w2,
           rwkv_mu, rwkv_w0, rwkv_w_up, rwkv_a0, rwkv_a_up, rwkv_g_up, rwkv_k_k, rwkv_k_a,
           rwkv_r_k, rwkv_ln_w, rwkv_ln_b, w_out,
           norm_x_g, norm_mem_g, xq_w, xk_w, xv_w, xq_norm_g, xk_norm_g, xo_w,
           norm_ffn_g, router_w, router_b, mlp1_w, mlp1_b, mlp2_w, mlp2_b):
    b, s, d = x.shape
    t = b * s
    tm = 512
    xt = x.reshape(t, d)

    w_nsa = jnp.pad(w_in[:, :NSA_PROJ], ((0, 0), (0, NSA_PROJ_PAD - NSA_PROJ)))
    p_nsa, p_rwkv = norm_matmul(xt, norm_mix_g, [w_nsa, w_in[:, NSA_PROJ:]], tm)

    nq = NSA_HEADS * HEAD_DIM
    gw = NSA_GROUPS * HEAD_DIM
    qn, ks, vs, kw, vw, gates = nsa_prep(p_nsa, q_norm_g, k_slc_norm_g, k_win_norm_g, b, s, tm)
    kcv = nsa_compress(p_nsa[:, nq:nq + gw], p_nsa[:, nq + gw:nq + 2 * gw], cmp_pe_k, cmp_pe_v,
                       cmp_k_w1, cmp_k_w2, cmp_v_w1, cmp_v_w2, k_cmp_norm_g, b, s)
    o_nsa = nsa_attention(qn, kcv, ks, vs, kw, vw, gates, b, s)

    at, bt, kt, rt, v, bw, kwd, wc, g_gate, bonus = rwkv_prep(
        p_rwkv, rwkv_mu, rwkv_w0, rwkv_w_up, rwkv_a0, rwkv_a_up, rwkv_g_up, rwkv_k_k, rwkv_k_a, rwkv_r_k, b, s, tm)
    ta, tr, arb, yv = rwkv_intra(at, bt, kt, rt, v, ts=256)
    y = rwkv_scan(ta, tr, arb, yv, rt, v, bw, kwd, wc, ts=256)

    h1 = out_proj(xt, o_nsa, y, bonus, g_gate, rwkv_ln_w, rwkv_ln_b, w_out, s, tm)
    m = mem.shape[1]
    kv, = norm_matmul(mem.reshape(b * m, d), norm_mem_g, [jnp.concatenate([xk_w, xv_w], axis=1)], m)
    h2 = cross_attention(h1, kv.reshape(b, m, 2 * d), norm_x_g, xq_w, xq_norm_g, xk_norm_g, xo_w, b, s, tm)

    xn, top_i, gate, rank, seen = moe_router(h2, norm_ffn_g, router_w, router_b, tm)
    top_i = top_i[:, :TOP_K]
    a = t * TOP_K
    mb = MOE_ROW_BLOCK
    idx_bits = max(a - 1, 1).bit_length()
    assert N_EXPERTS << idx_bits < 2 ** 31
    packed = (top_i.reshape(a) << idx_bits) | jnp.arange(a, dtype=jnp.int32)
    order = jnp.sort(packed) & ((1 << idx_bits) - 1)
    counts = seen[0, :N_EXPERTS].astype(jnp.int32)
    starts = jnp.cumsum(counts) - counts
    padded = (counts + mb - 1) // mb * mb
    pends = jnp.cumsum(padded)
    pstarts = pends - padded
    pos = pstarts[top_i] + rank[:, :TOP_K]
    n_blocks = -(-a // mb) + N_EXPERTS
    r = n_blocks * mb
    blk_start = jnp.arange(n_blocks, dtype=jnp.int32) * mb
    blk_e = jnp.minimum(jnp.sum(pends[None, :] <= blk_start[:, None], axis=1), N_EXPERTS - 1).astype(jnp.int32)
    src_i = blk_start[:, None] + jnp.arange(mb, dtype=jnp.int32)[None, :] - (pstarts - starts)[blk_e][:, None]
    valid = src_i < (starts + counts)[blk_e][:, None]
    row_src = jnp.where(valid, order[jnp.minimum(src_i, a - 1)] // TOP_K, 0).astype(jnp.int32).reshape(r)
    n_used = (pends[-1] // mb).astype(jnp.int32).reshape(1)
    xs = xn.at[row_src].get(mode="promise_in_bounds")
    f2 = mlp1_w.shape[2]
    ys = moe_experts(xs, blk_e, n_used, mlp1_w, mlp1_b.reshape(N_EXPERTS, 1, f2), mlp2_w,
                     mlp2_b.reshape(N_EXPERTS, 1, d))
    ys_k = [ys.at[pos[:, k]].get(mode="promise_in_bounds") for k in range(TOP_K)]
    out = moe_combine(h2, gate, ys_k, 256)
    return out.reshape(b, s, d)


def kernel(x, mem, norm_mix_g, w_in, q_norm_g, k_cmp_norm_g, k_slc_norm_g, k_win_norm_g, cmp_pe_k, cmp_pe_v, cmp_k_w1, cmp_k_w2, cmp_v_w1, cmp_v_w2, rwkv_mu, rwkv_w0, rwkv_w_up, rwkv_a0, rwkv_a_up, rwkv_g_up, rwkv_k_k, rwkv_k_a, rwkv_r_k, rwkv_ln_w, rwkv_ln_b, w_out, norm_x_g, norm_mem_g, xq_w, xk_w, xv_w, xq_norm_g, xk_norm_g, xo_w, norm_ffn_g, router_w, router_b, mlp1_w, mlp1_b, mlp2_w, mlp2_b):
    params = (norm_mix_g, w_in, q_norm_g, k_cmp_norm_g, k_slc_norm_g, k_win_norm_g, cmp_pe_k, cmp_pe_v,
              cmp_k_w1, cmp_k_w2, cmp_v_w1, cmp_v_w2, rwkv_mu, rwkv_w0, rwkv_w_up, rwkv_a0, rwkv_a_up,
              rwkv_g_up, rwkv_k_k, rwkv_k_a, rwkv_r_k, rwkv_ln_w, rwkv_ln_b, w_out, norm_x_g, norm_mem_g,
              xq_w, xk_w, xv_w, xq_norm_g, xk_norm_g, xo_w, norm_ffn_g, router_w, router_b,
              mlp1_w, mlp1_b, mlp2_w, mlp2_b)
    h = x
    for layer in range(norm_mix_g.shape[0]):
        h = _layer(h, mem, *[prm[layer] for prm in params])
    return h
```

```python
import functools

import numpy as np
import jax
import jax.numpy as jnp
from jax import lax
from jax.experimental import pallas as pl
from jax.experimental.pallas import tpu as pltpu

F32 = jnp.float32
BF16 = jnp.bfloat16
HIGHEST = lax.Precision.HIGHEST

V7X_VMEM_BYTES = 64 * 1024 * 1024
VMEM_LIMIT = V7X_VMEM_BYTES * 3 // 4

HEAD_DIM = 64
NSA_HEADS = 8
NSA_GROUPS = 2
NSA_HPG = NSA_HEADS // NSA_GROUPS
GROUP_W = NSA_HPG * HEAD_DIM
CMP_BLOCK = 32
CMP_STRIDE = 16
SLC_BLOCK = 64
SLC_TOPK = 16
WINDOW = 512
Q_BLOCK = 128
SEL_CHUNK = 512
RWKV_HEADS = 8
RWKV_DIM = RWKV_HEADS * HEAD_DIM
RWKV_CHUNK = 64
GN_EPS = HEAD_DIM * 1e-5
X_HEADS = 4
N_EXPERTS = 32
TOP_K = 4
SWIGLU_LIMIT = 7.0
SWIGLU_ALPHA = 1.702
MOE_ROW_BLOCK = 256
RMS_EPS = 1e-6
NEG_INF = -1e30
BIG = 1e9
REMOVED = -3e38
LANES = 128
MXU_DIM = 256

NSA_PROJ = NSA_HEADS * HEAD_DIM + 6 * NSA_GROUPS * HEAD_DIM + NSA_HEADS * 3
NSA_PROJ_PAD = -(-NSA_PROJ // LANES) * LANES
RWKV_PROJ = 3 * RWKV_DIM + 64 + 64 + 128


def _cparams(*sem):
    return pltpu.CompilerParams(dimension_semantics=sem, vmem_limit_bytes=VMEM_LIMIT)


def _dot(a, b, **kw):
    return jnp.dot(a, b, preferred_element_type=F32, **kw)


def _dot_nt(a, b, **kw):
    return lax.dot_general(a, b, (((1,), (1,)), ((), ())), preferred_element_type=F32, **kw)


def _dot_tn(a, b, **kw):
    return lax.dot_general(a, b, (((0,), (0,)), ((), ())), preferred_element_type=F32, **kw)


def _rms(x, g):
    return x * lax.rsqrt(jnp.mean(x * x, axis=-1, keepdims=True) + RMS_EPS) * g


def _split_bf16(x, terms=3):
    parts = []
    for _ in range(terms):
        hi = x.astype(BF16)
        parts.append(hi)
        x = x - hi.astype(F32)
    return parts


def _dot_x_sel(x, sel):
    w = sel.shape[0]
    parts = _split_bf16(x)
    slabs = [sum(_dot(p[:, c:c + w], sel) for p in parts) for c in range(0, x.shape[1], w)]
    return slabs[0] if len(slabs) == 1 else jnp.concatenate(slabs, axis=1)


def _dot_sel_x(sel, x):
    w = sel.shape[1]
    parts = _split_bf16(x)
    slabs = [sum(_dot(sel, p[r:r + w]) for p in parts) for r in range(0, x.shape[0], w)]
    return slabs[0] if len(slabs) == 1 else jnp.concatenate(slabs, axis=0)


def _block_diag_ones(n, blk, scale=1.0):
    i = np.arange(n)
    return jnp.asarray(((i[:, None] // blk) == (i[None, :] // blk)).astype(np.float32) * scale, BF16)


def _norm_matmul_kernel(x_ref, g_ref, *refs):
    n = len(refs) // 2
    xn = _rms(x_ref[...], g_ref[...]).astype(BF16)
    for w_ref, o_ref in zip(refs[:n], refs[n:]):
        o_ref[...] = _dot(xn, w_ref[...])


def norm_matmul(x, g, ws, tm):
    m, d = x.shape
    return pl.pallas_call(
        _norm_matmul_kernel,
        grid=(m // tm,),
        in_specs=[pl.BlockSpec((tm, d), lambda i: (i, 0)), pl.BlockSpec((1, d), lambda i: (0, 0))]
                 + [pl.BlockSpec((d, w.shape[1]), lambda i: (0, 0)) for w in ws],
        out_specs=[pl.BlockSpec((tm, w.shape[1]), lambda i: (i, 0)) for w in ws],
        out_shape=[jax.ShapeDtypeStruct((m, w.shape[1]), F32) for w in ws],
        compiler_params=_cparams("parallel"),
        name="norm_matmul",
    )(x, g.reshape(1, d), *[w.astype(BF16) for w in ws])


def _nsa_prep_kernel(p_ref, qg_ref, ksg_ref, kwg_ref, bdq_ref, bdk_ref, place_ref, feat_ref,
                     qt_o, ks_o, vst_o, kw_o, vwt_o, gate_o):
    p = p_ref[...]
    tm = p.shape[0]
    nq = NSA_HEADS * HEAD_DIM
    gw = NSA_GROUPS * HEAD_DIM
    q = p[:, :nq]
    msq = _dot_x_sel(q * q, bdq_ref[...])
    qn = q * lax.rsqrt(msq + RMS_EPS) * qg_ref[...] * (HEAD_DIM ** -0.5)
    qt_o[0] = qn.T.reshape(NSA_HEADS, HEAD_DIM, tm).astype(BF16)

    def seg(k):
        return p[:, nq + k * gw: nq + (k + 1) * gw]

    def head_norm(t, g):
        ms = _dot_x_sel(t * t, bdk_ref[...])
        return t * lax.rsqrt(ms + RMS_EPS) * g

    feat = feat_ref[...].astype(F32)
    ks = head_norm(seg(2), ksg_ref[...]).astype(BF16)
    kw = head_norm(seg(4), kwg_ref[...]).astype(BF16)
    vs_t = seg(3).T
    vw_t = seg(5).T
    for g in range(NSA_GROUPS):
        ks_o[0, g] = (_dot(ks, place_ref[g]) + feat).astype(BF16)
        kw_o[0, g] = (_dot(kw, place_ref[g])[:, :2 * HEAD_DIM] + feat[:, :2 * HEAD_DIM]).astype(BF16)
        for j in range(tm // Q_BLOCK):
            tile = (slice(g * HEAD_DIM, (g + 1) * HEAD_DIM), slice(j * Q_BLOCK, (j + 1) * Q_BLOCK))
            vst_o[0, g, j] = vs_t[tile].astype(BF16)
            vwt_o[0, g, j] = vw_t[tile].astype(BF16)
    gate_o[...] = jax.nn.sigmoid(p[:, nq + 6 * gw: nq + 6 * gw + LANES])


def nsa_prep(p_nsa, q_g, ks_g, kw_g, b, s, tm):
    t = p_nsa.shape[0]
    nq = NSA_HEADS * HEAD_DIM
    gw = NSA_GROUPS * HEAD_DIM
    grp, dh = NSA_GROUPS, HEAD_DIM
    tps = s // tm
    kx = 2 * dh + LANES
    assert s // SLC_BLOCK <= LANES
    tok = np.arange(s)
    feat = np.zeros((s, kx), np.float32)
    feat[:, dh] = tok // SLC_BLOCK * SLC_BLOCK
    feat[:, dh + 1] = tok % SLC_BLOCK
    feat[:, dh + 2:dh + 4] = 1.0
    feat[tok, 2 * dh + tok // SLC_BLOCK] = 1.0
    place = np.zeros((grp, gw, kx), np.float32)
    for g in range(grp):
        place[g, g * dh + np.arange(dh), np.arange(dh)] = 1.0
    tile = lambda v, n: jnp.tile(v.reshape(1, HEAD_DIM), (1, n))
    row = lambda w: pl.BlockSpec((tm, w), lambda i: (i, 0))
    full = lambda *sh: pl.BlockSpec(sh, lambda i: (0,) * len(sh))
    seq = lambda *sh: pl.BlockSpec((1, grp) + sh, lambda i: (i // tps, 0, i % tps) + (0,) * (len(sh) - 1))
    return pl.pallas_call(
        _nsa_prep_kernel,
        grid=(t // tm,),
        in_specs=[row(NSA_PROJ_PAD), full(1, nq), full(1, gw), full(1, gw), full(MXU_DIM, MXU_DIM), full(gw, gw),
                  full(grp, gw, kx), pl.BlockSpec((tm, kx), lambda i: (i % tps, 0))],
        out_specs=[pl.BlockSpec((1, NSA_HEADS, dh, tm), lambda i: (i // tps, 0, 0, i % tps)),
                   seq(tm, kx), seq(tm // Q_BLOCK, dh, Q_BLOCK), seq(tm, 2 * dh), seq(tm // Q_BLOCK, dh, Q_BLOCK),
                   row(LANES)],
        out_shape=[jax.ShapeDtypeStruct((b, NSA_HEADS, dh, s), BF16),
                   jax.ShapeDtypeStruct((b, grp, s, kx), BF16),
                   jax.ShapeDtypeStruct((b, grp, s // Q_BLOCK, dh, Q_BLOCK), BF16),
                   jax.ShapeDtypeStruct((b, grp, s, 2 * dh), BF16),
                   jax.ShapeDtypeStruct((b, grp, s // Q_BLOCK, dh, Q_BLOCK), BF16),
                   jax.ShapeDtypeStruct((t, LANES), F32)],
        compiler_params=_cparams("parallel"),
        name="nsa_prep",
    )(p_nsa, tile(q_g, NSA_HEADS), tile(ks_g, NSA_GROUPS), tile(kw_g, NSA_GROUPS),
      _block_diag_ones(MXU_DIM, HEAD_DIM, 1.0 / HEAD_DIM), _block_diag_ones(gw, HEAD_DIM, 1.0 / HEAD_DIM),
      jnp.asarray(place, BF16), jnp.asarray(feat, BF16))


def _compress_kernel(x_ref, pe_ref, w1_ref, w2_ref, g_ref, bd_ref, o_ref):
    nc = x_ref.shape[0] // CMP_STRIDE
    first = jnp.zeros((nc, x_ref.shape[1]), F32)
    second = jnp.zeros_like(first)
    for l in range(CMP_STRIDE):
        x_l = x_ref[pl.ds(l, nc, stride=CMP_STRIDE), :]
        first = first + _dot(x_l + pe_ref[0, l:l + 1], w1_ref[0, l], precision=HIGHEST)
        second = second + _dot(x_l + pe_ref[0, CMP_STRIDE + l:CMP_STRIDE + l + 1], w1_ref[0, CMP_STRIDE + l],
                               precision=HIGHEST)
    h1 = first + pltpu.roll(second, nc - 1, 0)
    out = _dot(jax.nn.silu(h1), w2_ref[0], precision=HIGHEST)
    normed = out * lax.rsqrt(_dot_x_sel(out * out, bd_ref[...]) + RMS_EPS) * g_ref[...]
    o_ref[0, 0] = jnp.where(pl.program_id(0) == 0, normed, out).astype(BF16)


def nsa_compress(p_nsa, pe_k, pe_v, kw1, kw2, vw1, vw2, kc_g, b, s):
    nc = s // CMP_STRIDE
    gw = NSA_GROUPS * HEAD_DIM
    first_col_block = NSA_HEADS * HEAD_DIM // gw
    both = lambda w: jnp.kron(jnp.eye(NSA_GROUPS, dtype=w.dtype), w)
    w1 = jnp.stack([jnp.stack([both(w[p * HEAD_DIM:(p + 1) * HEAD_DIM]) for p in range(CMP_BLOCK)])
                    for w in (kw1, vw1)])
    w2 = jnp.stack([both(kw2), both(vw2)])
    pe = jnp.stack([jnp.tile(pe_k, (1, NSA_GROUPS)), jnp.tile(pe_v, (1, NSA_GROUPS))])
    return pl.pallas_call(
        _compress_kernel,
        grid=(2, b),
        in_specs=[pl.BlockSpec((s, gw), lambda kv, bi: (bi, first_col_block + kv)),
                  pl.BlockSpec((1, CMP_BLOCK, gw), lambda kv, bi: (kv, 0, 0)),
                  pl.BlockSpec((1, CMP_BLOCK, gw, gw), lambda kv, bi: (kv, 0, 0, 0)),
                  pl.BlockSpec((1, gw, gw), lambda kv, bi: (kv, 0, 0)),
                  pl.BlockSpec((1, gw), lambda kv, bi: (0, 0)),
                  pl.BlockSpec((gw, gw), lambda kv, bi: (0, 0))],
        out_specs=pl.BlockSpec((1, 1, nc, gw), lambda kv, bi: (kv, bi, 0, 0)),
        out_shape=jax.ShapeDtypeStruct((2, b, nc, gw), BF16),
        compiler_params=_cparams("parallel", "parallel"),
        name="nsa_compress",
    )(p_nsa, pe, w1, w2, jnp.tile(kc_g.reshape(1, HEAD_DIM), (1, NSA_GROUPS)),
      _block_diag_ones(gw, HEAD_DIM, 1.0 / HEAD_DIM))


def _masked_exp_cols(s, mask):
    sm = jnp.where(mask, s, NEG_INF)
    m = jnp.max(sm, axis=0, keepdims=True)
    p = jnp.exp(sm - jnp.where(m > 0.5 * NEG_INF, m, 0.0))
    l = jnp.sum(p, axis=0, keepdims=True)
    return p, 1.0 / jnp.where(l > 0.0, l, 1.0)


def _nsa_attn_kernel(qt_ref, kc_ref, vct_ref, ovt_ref, ks_ref, vst_ref, kw_ref, vwt_ref, gate_ref, o_ref,
                     *, n_sel, n_top):
    g = pl.program_id(1)
    t0 = pl.program_id(2) * Q_BLOCK
    ks_ref, vst_ref, kw_ref, vwt_ref = (r.at[0, 0] for r in (ks_ref, vst_ref, kw_ref, vwt_ref))
    cols = NSA_HPG * Q_BLOCK
    col = lax.broadcasted_iota(jnp.int32, (1, cols), 1)
    tq = t0 + col % Q_BLOCK
    head = g * NSA_HPG + col // Q_BLOCK
    slope = lax.bitcast_convert_type((127 - (head + 1)) << 23, F32)
    qt = jnp.concatenate([qt_ref[0, h] for h in range(NSA_HPG)], axis=1)
    frow = lax.broadcasted_iota(jnp.int32, (HEAD_DIM, cols), 0)
    tq_hi = (tq // SLC_BLOCK * SLC_BLOCK).astype(F32)
    tq_lo = (tq % SLC_BLOCK).astype(F32)
    qpos = jnp.where(frow < 2, slope, jnp.where(frow == 2, -slope * tq_hi, jnp.where(frow == 3, -slope * tq_lo, 0.0)))
    qc = jnp.concatenate([qt, qpos.astype(BF16)], axis=0)

    def heads_sum(x):
        acc = x[:, :Q_BLOCK]
        for h in range(1, NSA_HPG):
            acc = acc + x[:, h * Q_BLOCK:(h + 1) * Q_BLOCK]
        return acc

    kc = kc_ref[0, 0]
    nc = kc.shape[0]
    c_last = lax.broadcasted_iota(jnp.int32, (nc, 1), 0) * CMP_STRIDE + (CMP_BLOCK - 1)
    p_c, inv_c = _masked_exp_cols(_dot(kc, qc), c_last <= tq)
    p_c = p_c * inv_c
    o_c = _dot(vct_ref[0, 0], p_c.astype(BF16))
    imp = _dot_sel_x(ovt_ref[...], heads_sum(p_c))

    bid = lax.broadcasted_iota(jnp.int32, (LANES, Q_BLOCK), 0)
    bidf = bid.astype(F32)
    tq1 = t0 + lax.broadcasted_iota(jnp.int32, (1, Q_BLOCK), 1)
    cur = tq1 // SLC_BLOCK
    forced = (bid == 0) | (bid == cur) | (bid == cur - 1)
    score = jnp.where(forced, BIG, jnp.where(bid * SLC_BLOCK <= tq1, imp, -BIG))
    score = jnp.where(bid < n_sel, score, REMOVED)
    picked = jnp.zeros((LANES, Q_BLOCK), jnp.bool_)
    for _ in range(n_top):
        m = jnp.max(score, axis=0, keepdims=True)
        pick = bidf == jnp.min(jnp.where(score == m, bidf, float(LANES)), axis=0, keepdims=True)
        picked = picked | pick
        score = jnp.where(pick, REMOVED, score)
    sel_neg = jnp.where(picked, 0.0, NEG_INF).astype(BF16)
    qx = jnp.concatenate([qc, jnp.concatenate([sel_neg] * NSA_HPG, axis=1)], axis=0)

    def values_t(ref, first_tile, n_tiles):
        return jnp.concatenate([ref[first_tile + i] for i in range(n_tiles)], axis=1)

    span = WINDOW + Q_BLOCK
    ws = pl.multiple_of(jnp.maximum(t0 - WINDOW, 0), Q_BLOCK)
    dw = tq - (ws + lax.broadcasted_iota(jnp.int32, (span, 1), 0))
    in_window = dw.astype(jnp.uint32) < WINDOW
    p_w, inv_w = _masked_exp_cols(_dot(kw_ref[pl.ds(ws, span), :], qc), in_window)
    o_w = _dot(values_t(vwt_ref, ws // Q_BLOCK, span // Q_BLOCK), p_w.astype(BF16)) * inv_w

    def scores(chunk):
        return _dot(ks_ref[pl.ds(pl.multiple_of(chunk * SEL_CHUNK, SEL_CHUNK), SEL_CHUNK), :], qx)

    def flash(s, chunk, carry):
        m, l, acc = carry
        m_new = jnp.maximum(m, jnp.max(s, axis=0, keepdims=True))
        alpha = jnp.exp(m - m_new)
        p = jnp.exp(s - m_new)
        l = alpha * l + jnp.sum(p, axis=0, keepdims=True)
        v = values_t(vst_ref, chunk * (SEL_CHUNK // Q_BLOCK), SEL_CHUNK // Q_BLOCK)
        return m_new, l, alpha * acc + _dot(v, p.astype(BF16))

    def flash_pair(chunk_a, chunk_b, carry, keep_b=None):
        s_a, s_b = scores(chunk_a), scores(chunk_b)
        if keep_b is not None:
            s_b = jnp.where(keep_b, s_b, NEG_INF)
        return flash(s_b, chunk_b, flash(s_a, chunk_a, carry))

    bpc = SEL_CHUNK // SLC_BLOCK
    blk_any = jnp.max(jnp.where(picked, 1.0, 0.0), axis=1, keepdims=True)
    chunk_bit = lax.bitcast_convert_type((bid[:, :1] // bpc + 127) << 23, F32)
    bits = jnp.max((blk_any * chunk_bit).reshape(LANES // bpc, bpc, 1), axis=1)
    active = jnp.sum(bits, axis=0, keepdims=True)[0, 0].astype(jnp.int32)

    def full_step(j, state):
        def on_active(st):
            return lax.cond(st[0] >= 0,
                            lambda s2: (jnp.int32(-1), flash_pair(s2[0], j, s2[1])),
                            lambda s2: (j, s2[1]), st)
        return lax.cond((active >> j) & 1 == 1, on_active, lambda st: st, state)

    n_full = t0 // SEL_CHUNK
    init = (jnp.full((1, cols), NEG_INF, F32), jnp.zeros((1, cols), F32), jnp.zeros((HEAD_DIM, cols), F32))
    waiting, carry = lax.fori_loop(0, n_full, full_step, (jnp.int32(-1), init))
    keep = n_full * SEL_CHUNK + lax.broadcasted_iota(jnp.int32, (SEL_CHUNK, 1), 0) <= tq
    _, l_s, acc_s = lax.cond(
        waiting >= 0,
        lambda c: flash_pair(waiting, n_full, c, keep_b=keep),
        lambda c: flash(jnp.where(keep, scores(n_full), NEG_INF), n_full, c), carry)
    o_s = acc_s / l_s

    for h in range(NSA_HPG):
        hs = slice(h * Q_BLOCK, (h + 1) * Q_BLOCK)
        gate = lambda br: gate_ref[0, 0, br, h:h + 1, :]
        o_ref[0, h] = gate(0) * o_c[:, hs] + gate(1) * o_s[:, hs] + gate(2) * o_w[:, hs]


def nsa_attention(qt, kcv, ks_x, vs_t, kw_x, vw_t, gates, b, s):
    nq_blocks = s // Q_BLOCK
    nc = s // CMP_STRIDE
    n_cmp = nc - CMP_BLOCK // CMP_STRIDE + 1
    n_sel = s // SLC_BLOCK
    n_top = min(SLC_TOPK, n_sel)
    grp, dh = NSA_GROUPS, HEAD_DIM
    assert n_sel <= LANES and s % SEL_CHUNK == 0 and s >= WINDOW + Q_BLOCK
    c_start = np.arange(n_cmp) * CMP_STRIDE
    s_start = np.arange(n_sel) * SLC_BLOCK
    ovt = np.zeros((LANES, nc), np.float32)
    ovt[:n_sel, :n_cmp] = (np.clip(np.minimum((c_start + CMP_BLOCK)[:, None], s_start[None] + SLC_BLOCK)
                                   - np.maximum(c_start[:, None], s_start[None]), 0, None) / CMP_BLOCK).T

    c_pos = np.arange(nc) * CMP_STRIDE + (CMP_BLOCK - 1)
    c_feat = np.zeros((nc, dh), np.float32)
    c_feat[:, 0] = c_pos // SLC_BLOCK * SLC_BLOCK
    c_feat[:, 1] = c_pos % SLC_BLOCK
    c_feat[:, 2:4] = 1.0
    kc, vc = (a.reshape(b, nc, grp, dh) for a in kcv)
    kc = kc.transpose(0, 2, 1, 3)
    kc_x = jnp.concatenate([kc, jnp.broadcast_to(jnp.asarray(c_feat, BF16), kc.shape)], axis=-1)
    vct = vc.transpose(0, 2, 3, 1)
    gates_t = gates[:, :NSA_HEADS * 3].reshape(b, s, grp, NSA_HPG, 3).transpose(0, 2, 4, 3, 1)

    grp_spec = lambda *shape: pl.BlockSpec((1, 1) + shape, lambda bi, g, i: (bi, g) + (0,) * len(shape),
                                           pipeline_mode=pl.Buffered(1))
    return pl.pallas_call(
        functools.partial(_nsa_attn_kernel, n_sel=n_sel, n_top=n_top),
        grid=(b, grp, nq_blocks),
        in_specs=[pl.BlockSpec((1, NSA_HPG, dh, Q_BLOCK), lambda bi, g, i: (bi, g, 0, i)),
                  grp_spec(nc, 2 * dh), grp_spec(dh, nc),
                  pl.BlockSpec((LANES, nc), lambda bi, g, i: (0, 0)),
                  grp_spec(s, 2 * dh + LANES), grp_spec(s // Q_BLOCK, dh, Q_BLOCK),
                  grp_spec(s, 2 * dh), grp_spec(s // Q_BLOCK, dh, Q_BLOCK),
                  pl.BlockSpec((1, 1, 3, NSA_HPG, Q_BLOCK), lambda bi, g, i: (bi, g, 0, 0, i))],
        out_specs=pl.BlockSpec((1, NSA_HPG, dh, Q_BLOCK), lambda bi, g, i: (bi, g, 0, i)),
        out_shape=jax.ShapeDtypeStruct((b, NSA_HEADS, dh, s), F32),
        compiler_params=_cparams("parallel", "parallel", "arbitrary"),
        name="nsa_attention",
    )(qt, kc_x, vct, jnp.asarray(ovt, BF16), ks_x, vs_t, kw_x, vw_t, gates_t)


def _rwkv_prep_kernel(p_ref, prev_ref, mu_ref, w0_ref, a0_ref, kk_ref, ka_ref, rk_ref,
                      wup_ref, aup_ref, gup_ref, bd_ref, ltri_ref, lones_ref, csum_ref,
                      at_o, bt_o, kt_o, rt_o, v_o, bw_o, kw_o, wc_o, g_o, bonus_o, *, tiles_per_seq):
    p = p_ref[...]
    tm = p.shape[0]
    first = pl.program_id(0) % tiles_per_seq == 0
    last_prev = jnp.where(first, 0.0, prev_ref[7:8, :])
    prev = pltpu.roll(p, 1, 0)
    prev = jnp.where(lax.broadcasted_iota(jnp.int32, (tm, 1), 0) == 0, last_prev, prev)
    pm = p + (prev - p) * mu_ref[...]
    d = RWKV_DIM
    r, k, v = pm[:, :d], pm[:, d:2 * d], pm[:, 2 * d:3 * d]
    lora = pm[:, 3 * d:3 * d + LANES]
    gd = pm[:, 3 * d + LANES:3 * d + 2 * LANES]
    z = -(w0_ref[...] + _dot(jnp.tanh(lora).astype(BF16), wup_ref[...]))
    softplus = jnp.maximum(z, 0.0) + jnp.log(1.0 + jnp.exp(-jnp.abs(z)))
    w = -softplus - 0.5
    a = jax.nn.sigmoid(a0_ref[...] + _dot(lora.astype(BF16), aup_ref[...]))
    g_o[...] = _dot(jax.nn.sigmoid(gd).astype(BF16), gup_ref[...])
    bd = bd_ref[...]
    kkr = k * kk_ref[...]
    kk = kkr / jnp.maximum(jnp.sqrt(_dot_x_sel(kkr * kkr, bd)), 1e-12)
    k2 = k * (1.0 + (a - 1.0) * ka_ref[...])
    bonus_o[...] = _dot_x_sel(r * k2 * rk_ref[...], bd) * v
    lw = -jnp.exp(w)
    lw_parts = _split_bf16(lw)
    cum = _dot_sel_x(ltri_ref[...], lw)
    tot = _dot_sel_x(lones_ref[...], lw)
    e_in = jnp.exp(cum)
    e_out = jnp.exp(-cum)
    e_end = jnp.exp(tot - cum)

    def put_heads(o, val):
        for h in range(RWKV_HEADS):
            o[0, h] = val[:, h * HEAD_DIM:(h + 1) * HEAD_DIM].astype(o.dtype)

    put_heads(at_o, -kk * jnp.exp(cum - lw))
    put_heads(bt_o, kk * a * e_out)
    put_heads(kt_o, k2 * e_out)
    put_heads(rt_o, r * e_in)
    put_heads(v_o, v)
    put_heads(bw_o, kk * a * e_end)
    put_heads(kw_o, k2 * e_end)
    put_heads(wc_o, jnp.exp(sum(_dot(csum_ref[...], p) for p in lw_parts)))


def rwkv_prep(p_rwkv, mu, w0, w_up, a0, a_up, g_up, k_k, k_a, r_k, b, s, tm):
    t = p_rwkv.shape[0]
    d = RWKV_DIM
    c = RWKV_CHUNK
    tps = s // tm
    cpt = tm // c
    wup = jnp.concatenate([w_up, jnp.zeros_like(a_up)], axis=0).astype(BF16)
    aup = jnp.concatenate([jnp.zeros_like(w_up), a_up], axis=0).astype(BF16)
    i = np.arange(tm)
    j = np.arange(MXU_DIM)
    same = (j[:, None] // c) == (j[None, :] // c)
    ltri = jnp.asarray(same & (j[:, None] >= j[None, :]), BF16)
    lones = jnp.asarray(same, BF16)
    csum = jnp.asarray(np.arange(cpt)[:, None] == (i[None, :] // c), BF16)
    row = lambda w: pl.BlockSpec((tm, w), lambda i: (i, 0))
    full = lambda *sh: pl.BlockSpec(sh, lambda i: (0,) * len(sh))
    heads = lambda n: pl.BlockSpec((1, RWKV_HEADS, n, HEAD_DIM), lambda i: (i // tps, 0, i % tps, 0))
    hshape = lambda n, dt: jax.ShapeDtypeStruct((b, RWKV_HEADS, n, HEAD_DIM), dt)
    vec = lambda x: x.reshape(1, -1)
    return pl.pallas_call(
        functools.partial(_rwkv_prep_kernel, tiles_per_seq=tps),
        grid=(t // tm,),
        in_specs=[row(RWKV_PROJ),
                  pl.BlockSpec((8, RWKV_PROJ), lambda i: (jnp.maximum(i * (tm // 8) - 1, 0), 0)),
                  full(1, RWKV_PROJ), full(1, d), full(1, d), full(1, d), full(1, d), full(1, d),
                  full(LANES, d), full(LANES, d), full(LANES, d), full(MXU_DIM, MXU_DIM),
                  full(MXU_DIM, MXU_DIM), full(MXU_DIM, MXU_DIM), full(cpt, tm)],
        out_specs=[heads(tm)] * 7 + [heads(cpt), row(d), row(d)],
        out_shape=[hshape(s, BF16)] * 7 + [hshape(s // c, F32)] + [jax.ShapeDtypeStruct((t, d), F32)] * 2,
        compiler_params=_cparams("parallel"),
        name="rwkv_prep",
    )(p_rwkv, p_rwkv, vec(mu), vec(w0), vec(a0), vec(k_k), vec(k_a), vec(r_k), wup, aup, g_up.astype(BF16),
      _block_diag_ones(MXU_DIM, HEAD_DIM), ltri, lones, csum)


def _bdot(a, b):
    return lax.dot_general(a, b, (((2,), (1,)), ((0,), (0,))), preferred_element_type=F32)


def _bdot_nt(a, b):
    return lax.dot_general(a, b, (((2,), (2,)), ((0,), (0,))), preferred_element_type=F32)


def _bdot_tn(a, b):
    return lax.dot_general(a, b, (((1,), (1,)), ((0,), (0,))), preferred_element_type=F32)


def _rwkv_intra_kernel(at_ref, bt_ref, kt_ref, rt_ref, v_ref, ta_o, tr_o, arb_o, yv_o):
    c = RWKV_CHUNK
    _, nh, ts, dh = at_ref.shape
    n = nh * (ts // c)
    chunked = lambda ref: ref[0].reshape(n, c, dh)
    at, bt, kt, rt, v = (chunked(r) for r in (at_ref, bt_ref, kt_ref, rt_ref, v_ref))
    ri = lax.broadcasted_iota(jnp.int32, (1, c, c), 1)
    ci = lax.broadcasted_iota(jnp.int32, (1, c, c), 2)
    strict = ri > ci
    incl = ri >= ci
    ar = jnp.concatenate([at, rt], axis=1)
    xb = _bdot_nt(ar, bt)
    xk = _bdot_nt(ar, kt)
    l_ab = jnp.where(strict, xb[:, :c], 0.0)
    a_ak = jnp.where(strict, xk[:, :c], 0.0)
    a_rb = jnp.where(incl, xb[:, c:], 0.0)
    a_rk = jnp.where(incl, xk[:, c:], 0.0)
    pw = l_ab
    tinv = jnp.where(ri == ci, 1.0, 0.0) + l_ab
    for _ in range(int(np.log2(c)) - 1):
        pw_b = pw.astype(BF16)
        pw = _bdot(pw_b, pw_b)
        tinv = tinv + _bdot(tinv.astype(BF16), pw.astype(BF16))
    tinv_b = tinv.astype(BF16)

    def put(o, val):
        o[0] = val.reshape(nh, ts, val.shape[-1]).astype(o.dtype)

    put(ta_o, _bdot(tinv_b, at))
    put(tr_o, _bdot(tinv_b, _bdot(a_ak.astype(BF16), v).astype(BF16)))
    put(arb_o, a_rb)
    put(yv_o, _bdot(a_rk.astype(BF16), v))


def rwkv_intra(at, bt, kt, rt, v, ts):
    b, h, s, dh = at.shape
    seq = lambda: pl.BlockSpec((1, h, ts, dh), lambda bi, i: (bi, 0, i, 0))
    shp = lambda dt: jax.ShapeDtypeStruct((b, h, s, dh), dt)
    return pl.pallas_call(
        _rwkv_intra_kernel,
        grid=(b, s // ts),
        in_specs=[seq()] * 5,
        out_specs=[seq()] * 4,
        out_shape=[shp(BF16), shp(F32), shp(BF16), shp(F32)],
        compiler_params=_cparams("parallel", "parallel"),
        name="rwkv_intra",
    )(at, bt, kt, rt, v)


def _rwkv_scan_kernel(ta_ref, tr_ref, arb_ref, yv_ref, rt_ref, v_ref, bw_ref, kw_ref, wc_ref, y_ref, st_ref):
    c = RWKV_CHUNK
    nb, nh, ts, dh = ta_ref.shape
    n = nb * nh

    @pl.when(pl.program_id(0) == 0)
    def _():
        st_ref[...] = jnp.zeros_like(st_ref)

    def chunk_step(j, _):
        sl = (slice(None), slice(None), pl.ds(pl.multiple_of(j * c, c), c), slice(None))
        get = lambda ref: ref[sl].reshape(n, c, dh)
        st = st_ref[...]
        st_b = st.astype(BF16)
        u = _bdot_nt(get(ta_ref), st_b) + get(tr_ref)
        u_b = u.astype(BF16)
        y = _bdot_nt(get(rt_ref), st_b) + _bdot(get(arb_ref), u_b) + get(yv_ref)
        wc = wc_ref[:, :, pl.ds(pl.program_id(0) * (ts // c) + j, 1), :].reshape(n, 1, dh)
        st_ref[...] = st * wc + _bdot_tn(jnp.concatenate([u_b, get(v_ref)], axis=1),
                                         jnp.concatenate([get(bw_ref), get(kw_ref)], axis=1))
        y_ref[sl] = y.reshape(nb, nh, c, dh)
        return 0

    lax.fori_loop(0, ts // c, chunk_step, 0)


def rwkv_scan(ta, tr, arb, yv, rt, v, bw, kw, wc, ts):
    b, h, s, dh = ta.shape
    seq = lambda n: pl.BlockSpec((b, h, n, dh), lambda i: (0, 0, i, 0))
    return pl.pallas_call(
        _rwkv_scan_kernel,
        grid=(s // ts,),
        in_specs=[seq(ts)] * 8 + [pl.BlockSpec(wc.shape, lambda i: (0, 0, 0, 0))],
        out_specs=seq(ts),
        out_shape=jax.ShapeDtypeStruct((b, h, s, dh), F32),
        scratch_shapes=[pltpu.VMEM((b * h, dh, dh), F32)],
        compiler_params=_cparams("arbitrary"),
        name="rwkv_scan",
    )(ta, tr, arb, yv, rt, v, bw, kw, wc)


def _out_proj_kernel(x_ref, on_ref, y_ref, bonus_ref, g_ref, lnw_ref, lnb_ref, bd_ref, wn_ref, wr_ref, o_ref):
    y = jnp.concatenate([y_ref[0, h] for h in range(RWKV_HEADS)], axis=-1)
    bd = bd_ref[...]
    yc = y - _dot_x_sel(y, bd)
    yn = yc * lax.rsqrt(_dot_x_sel(yc * yc, bd) + GN_EPS)
    o_rwkv = (yn * lnw_ref[...] + lnb_ref[...] + bonus_ref[...]) * g_ref[...]
    tm = y.shape[0]
    o_nsa_t = on_ref[0].reshape(NSA_HEADS * HEAD_DIM, tm)
    o_ref[...] = (x_ref[...] + _dot_tn(o_nsa_t.astype(BF16), wn_ref[...])
                  + _dot(o_rwkv.astype(BF16), wr_ref[...]))


def out_proj(x, o_nsa_t, y, bonus, g, ln_w, ln_b, w_out, s, tm):
    t, d = x.shape
    dn = o_nsa_t.shape[1] * o_nsa_t.shape[2]
    dr = bonus.shape[1]
    tps = s // tm
    row = lambda w: pl.BlockSpec((tm, w), lambda i: (i, 0))
    full = lambda *sh: pl.BlockSpec(sh, lambda i: (0,) * len(sh))
    return pl.pallas_call(
        _out_proj_kernel,
        grid=(t // tm,),
        in_specs=[row(d), pl.BlockSpec((1, NSA_HEADS, HEAD_DIM, tm), lambda i: (i // tps, 0, 0, i % tps)),
                  pl.BlockSpec((1, RWKV_HEADS, tm, HEAD_DIM), lambda i: (i // tps, 0, i % tps, 0)),
                  row(dr), row(dr), full(1, dr), full(1, dr), full(MXU_DIM, MXU_DIM), full(dn, d), full(dr, d)],
        out_specs=row(d),
        out_shape=jax.ShapeDtypeStruct((t, d), F32),
        compiler_params=_cparams("parallel"),
        name="out_proj",
    )(x, o_nsa_t, y, bonus, g, ln_w.reshape(1, dr), ln_b.reshape(1, dr),
      _block_diag_ones(MXU_DIM, HEAD_DIM, 1.0 / HEAD_DIM), w_out[:dn].astype(BF16), w_out[dn:].astype(BF16))


def _cross_attn_kernel(h_ref, g_ref, wq_ref, qg_ref, kv_ref, kg_ref, wo_ref, o_ref):
    h = h_ref[...]
    d = h.shape[1]
    xd = d // X_HEADS
    q = _dot(_rms(h, g_ref[...]).astype(BF16), wq_ref[...])
    kv = kv_ref[0]
    outs = []
    for hd in range(X_HEADS):
        qh = _rms(q[:, hd * xd:(hd + 1) * xd], qg_ref[...]) * (xd ** -0.5)
        kh = _rms(kv[:, hd * xd:(hd + 1) * xd], kg_ref[...])
        vh = kv[:, d + hd * xd:d + (hd + 1) * xd]
        s = _dot_nt(qh.astype(BF16), kh.astype(BF16))
        p = jnp.exp(s - jnp.max(s, axis=-1, keepdims=True))
        p = p / jnp.sum(p, axis=-1, keepdims=True)
        outs.append(_dot(p.astype(BF16), vh.astype(BF16)))
    o = jnp.concatenate(outs, axis=-1)
    o_ref[...] = h + _dot(o.astype(BF16), wo_ref[...])


def cross_attention(h, kv, norm_g, xq_w, xq_g, xk_g, xo_w, b, s, tm):
    t, d = h.shape
    m = kv.shape[1]
    xd = d // X_HEADS
    tiles = s // tm
    full = lambda *sh: pl.BlockSpec(sh, lambda i: (0,) * len(sh))
    return pl.pallas_call(
        _cross_attn_kernel,
        grid=(t // tm,),
        in_specs=[pl.BlockSpec((tm, d), lambda i: (i, 0)), full(1, d), full(d, d), full(1, xd),
                  pl.BlockSpec((1, m, 2 * d), lambda i: (i // tiles, 0, 0)), full(1, xd), full(d, d)],
        out_specs=pl.BlockSpec((tm, d), lambda i: (i, 0)),
        out_shape=jax.ShapeDtypeStruct((t, d), F32),
        compiler_params=_cparams("parallel"),
        name="cross_attention",
    )(h, norm_g.reshape(1, d), xq_w.astype(BF16), xq_g.reshape(1, xd), kv, xk_g.reshape(1, xd),
      xo_w.astype(BF16))


def _router_kernel(h_ref, g_ref, rw_ref, rb_ref, ltri_ref, xn_o, idx_o, gate_o, rank_o, count_o, seen_ref):
    @pl.when(pl.program_id(0) == 0)
    def _():
        seen_ref[...] = jnp.zeros_like(seen_ref)

    xn = _rms(h_ref[...], g_ref[...])
    xn_o[...] = xn
    logits = _dot(xn, rw_ref[...], precision=HIGHEST) + rb_ref[...]
    tm = logits.shape[0]
    lane = lax.broadcasted_iota(jnp.int32, (tm, LANES), 1)
    lanef = lane.astype(F32)
    logits = jnp.where(lane < N_EXPERTS, logits, REMOVED)
    idx_acc = jnp.zeros((tm, LANES), F32)
    val_acc = jnp.zeros((tm, LANES), F32)
    chosen = jnp.zeros((tm, LANES), F32)
    picks = []
    top = None
    for k in range(TOP_K):
        m = jnp.max(logits, axis=-1, keepdims=True)
        idx = jnp.min(jnp.where(logits == m, lanef, float(LANES)), axis=-1, keepdims=True)
        pick = lanef == idx
        picks.append(pick)
        chosen = jnp.where(pick, 1.0, chosen)
        logits = jnp.where(pick, REMOVED, logits)
        top = m if top is None else top
        idx_acc = jnp.where(lane == k, idx, idx_acc)
        val_acc = jnp.where(lane == k, jnp.exp(m - top), val_acc)
    idx_o[...] = idx_acc.astype(jnp.int32)
    gate_o[...] = val_acc / jnp.sum(val_acc, axis=-1, keepdims=True)
    before = seen_ref[0:1, :] + _dot(ltri_ref[...], chosen.astype(BF16))
    rank_acc = jnp.zeros((tm, LANES), F32)
    for k, pick in enumerate(picks):
        rank_acc = jnp.where(lane == k, jnp.sum(jnp.where(pick, before, 0.0), axis=-1, keepdims=True), rank_acc)
    rank_o[...] = rank_acc.astype(jnp.int32)
    seen_ref[...] = seen_ref[...] + jnp.sum(chosen, axis=0, keepdims=True)
    count_o[...] = seen_ref[...]


def moe_router(h, norm_g, router_w, router_b, tm):
    t, d = h.shape
    rw = jnp.zeros((d, LANES), F32).at[:, :N_EXPERTS].set(router_w)
    rb = jnp.zeros((1, LANES), F32).at[0, :N_EXPERTS].set(router_b)
    i = np.arange(tm)
    ltri = jnp.asarray(i[:, None] > i[None, :], BF16)
    row = lambda w: pl.BlockSpec((tm, w), lambda i: (i, 0))
    full = lambda *s: pl.BlockSpec(s, lambda i: (0,) * len(s))
    return pl.pallas_call(
        _router_kernel,
        grid=(t // tm,),
        in_specs=[row(d), full(1, d), full(d, LANES), full(1, LANES), full(tm, tm)],
        out_specs=[row(d), row(LANES), row(LANES), row(LANES), full(8, LANES)],
        out_shape=[jax.ShapeDtypeStruct((t, d), F32), jax.ShapeDtypeStruct((t, LANES), jnp.int32),
                   jax.ShapeDtypeStruct((t, LANES), F32), jax.ShapeDtypeStruct((t, LANES), jnp.int32),
                   jax.ShapeDtypeStruct((8, LANES), F32)],
        scratch_shapes=[pltpu.VMEM((8, LANES), F32)],
        compiler_params=_cparams("arbitrary"),
        name="moe_router",
    )(h, norm_g.reshape(1, d), rw, rb, ltri)


def _expert_kernel(blk_e_ref, n_used_ref, x_ref, w1_ref, b1_ref, w2_ref, b2_ref, o_ref, w2x_ref):
    i = pl.program_id(0)
    f = w2_ref.shape[1]
    half_lanes = LANES // 2
    n_merged = f // LANES

    @pl.when((i == 0) | (blk_e_ref[i] != blk_e_ref[jnp.maximum(i - 1, 0)]))
    def _():
        for s in range(w2x_ref.shape[0]):
            lanes = slice(s * LANES, (s + 1) * LANES)
            for c in range(n_merged):
                lo = w2_ref[0, c * half_lanes:(c + 1) * half_lanes, lanes]
                hi = w2_ref[0, (c + n_merged) * half_lanes:(c + n_merged + 1) * half_lanes, lanes]
                w2x_ref[s, pl.ds(c * LANES, half_lanes, stride=2), :] = lo
                w2x_ref[s, pl.ds(c * LANES + 1, half_lanes, stride=2), :] = hi

    @pl.when(i < n_used_ref[0])
    def _():
        x = x_ref[...].astype(BF16)
        h = _dot(x, w1_ref[0].astype(BF16)) + b1_ref[0]
        hg = jnp.minimum(h, SWIGLU_LIMIT)
        gate = hg * jax.nn.sigmoid(SWIGLU_ALPHA * hg)
        lin = jnp.clip(h, -SWIGLU_LIMIT, SWIGLU_LIMIT) + 1.0
        even = lax.broadcasted_iota(jnp.int32, (1, LANES), 1) % 2 == 0
        def swiglu(c):
            cols = slice(c * LANES, (c + 1) * LANES)
            nxt = pltpu.roll(lin[:, cols], LANES - 1, 1)
            return jnp.where(even, gate[:, cols] * nxt, 0.0)

        act = jnp.concatenate([(swiglu(c) + pltpu.roll(swiglu(c + n_merged), 1, 1)).astype(BF16)
                               for c in range(n_merged)], axis=1)
        w2x = jnp.concatenate([w2x_ref[s] for s in range(w2x_ref.shape[0])], axis=1)
        o_ref[...] = _dot(act, w2x.astype(BF16)) + b2_ref[0]

    @pl.when(i >= n_used_ref[0])
    def _():
        o_ref[...] = jnp.zeros_like(o_ref)


def moe_experts(xs, blk_e, n_used, w1, b1, w2, b2):
    r, d = xs.shape
    f2 = w1.shape[2]
    m = MOE_ROW_BLOCK
    ex = lambda *s: pl.BlockSpec((1,) + s, lambda i, be, nu: (be[i],) + (0,) * len(s))
    grid_spec = pltpu.PrefetchScalarGridSpec(
        num_scalar_prefetch=2,
        grid=(r // m,),
        in_specs=[pl.BlockSpec((m, d), lambda i, be, nu: (i, 0)),
                  ex(d, f2), ex(1, f2), ex(f2 // 2, d), ex(1, d)],
        out_specs=pl.BlockSpec((m, d), lambda i, be, nu: (i, 0)),
        scratch_shapes=[pltpu.VMEM((d // LANES, f2 // 2, LANES), F32)],
    )
    return pl.pallas_call(
        _expert_kernel,
        grid_spec=grid_spec,
        out_shape=jax.ShapeDtypeStruct((r, d), F32),
        compiler_params=_cparams("arbitrary"),
        name="moe_experts",
    )(blk_e, n_used, xs, w1, b1, w2, b2)


def _combine_kernel(h_ref, gate_ref, *refs):
    *y_refs, o_ref = refs
    acc = h_ref[...]
    for k, y_ref in enumerate(y_refs):
        acc = acc + gate_ref[:, k:k + 1] * y_ref[...]
    o_ref[...] = acc


def moe_combine(h, gate, ys_k, tm):
    t, d = h.shape
    row = lambda w: pl.BlockSpec((tm, w), lambda i: (i, 0))
    return pl.pallas_call(
        _combine_kernel,
        grid=(t // tm,),
        in_specs=[row(d), row(LANES)] + [row(d)] * len(ys_k),
        out_specs=row(d),
        out_shape=jax.ShapeDtypeStruct((t, d), F32),
        compiler_params=_cparams("parallel"),
        name="moe_combine",
    )(h, gate, *ys_k)


def _layer(x, mem, norm_mix_g, w_in, q_norm_g, k_cmp_norm_g, k_slc_norm_g, k_win_norm_g,
           cmp_pe_k, cmp_pe_v, cmp_k_w1, cmp_k_w2, cmp_v_w1, cmp_v_w2,
           rwkv_mu, rwkv_w0, rwkv_w_up, rwkv_a0, rwkv_a_up, rwkv_g_up, rwkv_k_k, rwkv_k_a,
           rwkv_r_k, rwkv_ln_w, rwkv_ln_b, w_out,
           norm_x_g, norm_mem_g, xq_w, xk_w, xv_w, xq_norm_g, xk_norm_g, xo_w,
           norm_ffn_g, router_w, router_b, mlp1_w, mlp1_b, mlp2_w, mlp2_b):
    b, s, d = x.shape
    t = b * s
    tm = 512
    xt = x.reshape(t, d)

    w_nsa = jnp.pad(w_in[:, :NSA_PROJ], ((0, 0), (0, NSA_PROJ_PAD - NSA_PROJ)))
    p_nsa, p_rwkv = norm_matmul(xt, norm_mix_g, [w_nsa, w_in[:, NSA_PROJ:]], tm)

    qn, ks, vs, kw, vw, gates = nsa_prep(p_nsa, q_norm_g, k_slc_norm_g, k_win_norm_g, b, s, tm)
    kcv = nsa_compress(p_nsa, cmp_pe_k, cmp_pe_v, cmp_k_w1, cmp_k_w2, cmp_v_w1, cmp_v_w2, k_cmp_norm_g, b, s)
    o_nsa = nsa_attention(qn, kcv, ks, vs, kw, vw, gates, b, s)

    at, bt, kt, rt, v, bw, kwd, wc, g_gate, bonus = rwkv_prep(
        p_rwkv, rwkv_mu, rwkv_w0, rwkv_w_up, rwkv_a0, rwkv_a_up, rwkv_g_up, rwkv_k_k, rwkv_k_a, rwkv_r_k, b, s, tm)
    ta, tr, arb, yv = rwkv_intra(at, bt, kt, rt, v, ts=256)
    y = rwkv_scan(ta, tr, arb, yv, rt, v, bw, kwd, wc, ts=256)

    h1 = out_proj(xt, o_nsa, y, bonus, g_gate, rwkv_ln_w, rwkv_ln_b, w_out, s, tm)
    m = mem.shape[1]
    kv, = norm_matmul(mem.reshape(b * m, d), norm_mem_g, [jnp.concatenate([xk_w, xv_w], axis=1)], m)
    h2 = cross_attention(h1, kv.reshape(b, m, 2 * d), norm_x_g, xq_w, xq_norm_g, xk_norm_g, xo_w, b, s, tm)

    xn, top_i, gate, rank, seen = moe_router(h2, norm_ffn_g, router_w, router_b, tm)
    top_i = top_i[:, :TOP_K]
    a = t * TOP_K
    mb = MOE_ROW_BLOCK
    idx_bits = max(a - 1, 1).bit_length()
    assert N_EXPERTS << idx_bits < 2 ** 31
    packed = (top_i.reshape(a) << idx_bits) | jnp.arange(a, dtype=jnp.int32)
    order = jnp.sort(packed) & ((1 << idx_bits) - 1)
    counts = seen[0, :N_EXPERTS].astype(jnp.int32)
    starts = jnp.cumsum(counts) - counts
    padded = (counts + mb - 1) // mb * mb
    pends = jnp.cumsum(padded)
    pstarts = pends - padded
    pos = pstarts[top_i] + rank[:, :TOP_K]
    n_blocks = -(-a // mb) + N_EXPERTS
    r = n_blocks * mb
    blk_start = jnp.arange(n_blocks, dtype=jnp.int32) * mb
    blk_e = jnp.minimum(jnp.sum(pends[None, :] <= blk_start[:, None], axis=1), N_EXPERTS - 1).astype(jnp.int32)
    src_i = blk_start[:, None] + jnp.arange(mb, dtype=jnp.int32)[None, :] - (pstarts - starts)[blk_e][:, None]
    valid = src_i < (starts + counts)[blk_e][:, None]
    row_src = jnp.where(valid, order[jnp.minimum(src_i, a - 1)] // TOP_K, 0).astype(jnp.int32).reshape(r)
    n_used = (pends[-1] // mb).astype(jnp.int32).reshape(1)
    xs = xn.at[row_src].get(mode="promise_in_bounds")
    f2 = mlp1_w.shape[2]
    ys = moe_experts(xs, blk_e, n_used, mlp1_w, mlp1_b.reshape(N_EXPERTS, 1, f2), mlp2_w,
                     mlp2_b.reshape(N_EXPERTS, 1, d))
    ys_k = [ys.at[pos[:, k]].get(mode="promise_in_bounds") for k in range(TOP_K)]
    out = moe_combine(h2, gate, ys_k, 256)
    return out.reshape(b, s, d)


def kernel(x, mem, norm_mix_g, w_in, q_norm_g, k_cmp_norm_g, k_slc_norm_g, k_win_norm_g, cmp_pe_k, cmp_pe_v, cmp_k_w1, cmp_k_w2, cmp_v_w1, cmp_v_w2, rwkv_mu, rwkv_w0, rwkv_w_up, rwkv_a0, rwkv_a_up, rwkv_g_up, rwkv_k_k, rwkv_k_a, rwkv_r_k, rwkv_ln_w, rwkv_ln_b, w_out, norm_x_g, norm_mem_g, xq_w, xk_w, xv_w, xq_norm_g, xk_norm_g, xo_w, norm_ffn_g, router_w, router_b, mlp1_w, mlp1_b, mlp2_w, mlp2_b):
    params = (norm_mix_g, w_in, q_norm_g, k_cmp_norm_g, k_slc_norm_g, k_win_norm_g, cmp_pe_k, cmp_pe_v,
              cmp_k_w1, cmp_k_w2, cmp_v_w1, cmp_v_w2, rwkv_mu, rwkv_w0, rwkv_w_up, rwkv_a0, rwkv_a_up,
              rwkv_g_up, rwkv_k_k, rwkv_k_a, rwkv_r_k, rwkv_ln_w, rwkv_ln_b, w_out, norm_x_g, norm_mem_g,
              xq_w, xk_w, xv_w, xq_norm_g, xk_norm_g, xo_w, norm_ffn_g, router_w, router_b,
              mlp1_w, mlp1_b, mlp2_w, mlp2_b)
    h = x
    for layer in range(norm_mix_g.shape[0]):
        h = _layer(h, mem, *[prm[layer] for prm in params])
    return h
```

```python
import functools

import numpy as np
import jax
import jax.numpy as jnp
from jax import lax
from jax.experimental import pallas as pl
from jax.experimental.pallas import tpu as pltpu

F32 = jnp.float32
BF16 = jnp.bfloat16
HIGHEST = lax.Precision.HIGHEST

V7X_VMEM_BYTES = 64 * 1024 * 1024
VMEM_LIMIT = V7X_VMEM_BYTES * 3 // 4

HEAD_DIM = 64
NSA_HEADS = 8
NSA_GROUPS = 2
NSA_HPG = NSA_HEADS // NSA_GROUPS
GROUP_W = NSA_HPG * HEAD_DIM
CMP_BLOCK = 32
CMP_STRIDE = 16
SLC_BLOCK = 64
SLC_TOPK = 16
WINDOW = 512
Q_BLOCK = 128
SEL_CHUNK = 512
RWKV_HEADS = 8
RWKV_DIM = RWKV_HEADS * HEAD_DIM
RWKV_CHUNK = 64
GN_EPS = HEAD_DIM * 1e-5
X_HEADS = 4
N_EXPERTS = 32
TOP_K = 4
SWIGLU_LIMIT = 7.0
SWIGLU_ALPHA = 1.702
MOE_ROW_BLOCK = 256
RMS_EPS = 1e-6
NEG_INF = -1e30
BIG = 1e9
REMOVED = -3e38
LANES = 128
MXU_DIM = 256

NSA_PROJ = NSA_HEADS * HEAD_DIM + 6 * NSA_GROUPS * HEAD_DIM + NSA_HEADS * 3
NSA_PROJ_PAD = -(-NSA_PROJ // LANES) * LANES
RWKV_PROJ = 3 * RWKV_DIM + 64 + 64 + 128


def _cparams(*sem):
    return pltpu.CompilerParams(dimension_semantics=sem, vmem_limit_bytes=VMEM_LIMIT)


def _dot(a, b, **kw):
    return jnp.dot(a, b, preferred_element_type=F32, **kw)


def _dot_nt(a, b, **kw):
    return lax.dot_general(a, b, (((1,), (1,)), ((), ())), preferred_element_type=F32, **kw)


def _dot_tn(a, b, **kw):
    return lax.dot_general(a, b, (((0,), (0,)), ((), ())), preferred_element_type=F32, **kw)


def _rms(x, g):
    return x * lax.rsqrt(jnp.mean(x * x, axis=-1, keepdims=True) + RMS_EPS) * g


def _split_bf16(x, terms=3):
    parts = []
    for _ in range(terms):
        hi = x.astype(BF16)
        parts.append(hi)
        x = x - hi.astype(F32)
    return parts


def _dot_x_sel(x, sel):
    w = sel.shape[0]
    parts = _split_bf16(x)
    slabs = [sum(_dot(p[:, c:c + w], sel) for p in parts) for c in range(0, x.shape[1], w)]
    return slabs[0] if len(slabs) == 1 else jnp.concatenate(slabs, axis=1)


def _dot_sel_x(sel, x):
    w = sel.shape[1]
    parts = _split_bf16(x)
    slabs = [sum(_dot(sel, p[r:r + w]) for p in parts) for r in range(0, x.shape[0], w)]
    return slabs[0] if len(slabs) == 1 else jnp.concatenate(slabs, axis=0)


def _block_diag_ones(n, blk, scale=1.0):
    i = np.arange(n)
    return jnp.asarray(((i[:, None] // blk) == (i[None, :] // blk)).astype(np.float32) * scale, BF16)


def _norm_matmul_kernel(x_ref, g_ref, *refs):
    n = len(refs) // 2
    xn = _rms(x_ref[...], g_ref[...]).astype(BF16)
    for w_ref, o_ref in zip(refs[:n], refs[n:]):
        o_ref[...] = _dot(xn, w_ref[...])


def norm_matmul(x, g, ws, tm):
    m, d = x.shape
    return pl.pallas_call(
        _norm_matmul_kernel,
        grid=(m // tm,),
        in_specs=[pl.BlockSpec((tm, d), lambda i: (i, 0)), pl.BlockSpec((1, d), lambda i: (0, 0))]
                 + [pl.BlockSpec((d, w.shape[1]), lambda i: (0, 0)) for w in ws],
        out_specs=[pl.BlockSpec((tm, w.shape[1]), lambda i: (i, 0)) for w in ws],
        out_shape=[jax.ShapeDtypeStruct((m, w.shape[1]), F32) for w in ws],
        compiler_params=_cparams("parallel"),
        name="norm_matmul",
    )(x, g.reshape(1, d), *[w.astype(BF16) for w in ws])


def _nsa_prep_kernel(p_ref, qg_ref, ksg_ref, kwg_ref, bdq_ref, bdk_ref, place_ref, feat_ref,
                     qt_o, ks_o, vst_o, kw_o, vwt_o, gate_o):
    p = p_ref[...]
    tm = p.shape[0]
    nq = NSA_HEADS * HEAD_DIM
    gw = NSA_GROUPS * HEAD_DIM
    q = p[:, :nq]
    msq = _dot_x_sel(q * q, bdq_ref[...])
    qn = q * lax.rsqrt(msq + RMS_EPS) * qg_ref[...] * (HEAD_DIM ** -0.5)
    qt_o[0] = qn.T.reshape(NSA_HEADS, HEAD_DIM, tm).astype(BF16)

    def seg(k):
        return p[:, nq + k * gw: nq + (k + 1) * gw]

    def head_norm(t, g):
        ms = _dot_x_sel(t * t, bdk_ref[...])
        return t * lax.rsqrt(ms + RMS_EPS) * g

    feat = feat_ref[...].astype(F32)
    ks = head_norm(seg(2), ksg_ref[...]).astype(BF16)
    kw = head_norm(seg(4), kwg_ref[...]).astype(BF16)
    vs_t = seg(3).T
    vw_t = seg(5).T
    for g in range(NSA_GROUPS):
        ks_o[0, g] = (_dot(ks, place_ref[g]) + feat).astype(BF16)
        kw_o[0, g] = (_dot(kw, place_ref[g])[:, :2 * HEAD_DIM] + feat[:, :2 * HEAD_DIM]).astype(BF16)
        for j in range(tm // Q_BLOCK):
            tile = (slice(g * HEAD_DIM, (g + 1) * HEAD_DIM), slice(j * Q_BLOCK, (j + 1) * Q_BLOCK))
            vst_o[0, g, j] = vs_t[tile].astype(BF16)
            vwt_o[0, g, j] = vw_t[tile].astype(BF16)
    gate_o[...] = jax.nn.sigmoid(p[:, nq + 6 * gw: nq + 6 * gw + LANES])


def nsa_prep(p_nsa, q_g, ks_g, kw_g, b, s, tm):
    t = p_nsa.shape[0]
    nq = NSA_HEADS * HEAD_DIM
    gw = NSA_GROUPS * HEAD_DIM
    grp, dh = NSA_GROUPS, HEAD_DIM
    tps = s // tm
    kx = 2 * dh + LANES
    assert s // SLC_BLOCK <= LANES
    tok = np.arange(s)
    feat = np.zeros((s, kx), np.float32)
    feat[:, dh] = tok // SLC_BLOCK * SLC_BLOCK
    feat[:, dh + 1] = tok % SLC_BLOCK
    feat[:, dh + 2:dh + 4] = 1.0
    feat[tok, 2 * dh + tok // SLC_BLOCK] = 1.0
    place = np.zeros((grp, gw, kx), np.float32)
    for g in range(grp):
        place[g, g * dh + np.arange(dh), np.arange(dh)] = 1.0
    tile = lambda v, n: jnp.tile(v.reshape(1, HEAD_DIM), (1, n))
    row = lambda w: pl.BlockSpec((tm, w), lambda i: (i, 0))
    full = lambda *sh: pl.BlockSpec(sh, lambda i: (0,) * len(sh))
    seq = lambda *sh: pl.BlockSpec((1, grp) + sh, lambda i: (i // tps, 0, i % tps) + (0,) * (len(sh) - 1))
    return pl.pallas_call(
        _nsa_prep_kernel,
        grid=(t // tm,),
        in_specs=[row(NSA_PROJ_PAD), full(1, nq), full(1, gw), full(1, gw), full(MXU_DIM, MXU_DIM), full(gw, gw),
                  full(grp, gw, kx), pl.BlockSpec((tm, kx), lambda i: (i % tps, 0))],
        out_specs=[pl.BlockSpec((1, NSA_HEADS, dh, tm), lambda i: (i // tps, 0, 0, i % tps)),
                   seq(tm, kx), seq(tm // Q_BLOCK, dh, Q_BLOCK), seq(tm, 2 * dh), seq(tm // Q_BLOCK, dh, Q_BLOCK),
                   row(LANES)],
        out_shape=[jax.ShapeDtypeStruct((b, NSA_HEADS, dh, s), BF16),
                   jax.ShapeDtypeStruct((b, grp, s, kx), BF16),
                   jax.ShapeDtypeStruct((b, grp, s // Q_BLOCK, dh, Q_BLOCK), BF16),
                   jax.ShapeDtypeStruct((b, grp, s, 2 * dh), BF16),
                   jax.ShapeDtypeStruct((b, grp, s // Q_BLOCK, dh, Q_BLOCK), BF16),
                   jax.ShapeDtypeStruct((t, LANES), F32)],
        compiler_params=_cparams("parallel"),
        name="nsa_prep",
    )(p_nsa, tile(q_g, NSA_HEADS), tile(ks_g, NSA_GROUPS), tile(kw_g, NSA_GROUPS),
      _block_diag_ones(MXU_DIM, HEAD_DIM, 1.0 / HEAD_DIM), _block_diag_ones(gw, HEAD_DIM, 1.0 / HEAD_DIM),
      jnp.asarray(place, BF16), jnp.asarray(feat, BF16))


def _compress_kernel(x_ref, pe_ref, w1_ref, w2_ref, g_ref, bd_ref, o_ref):
    nc = x_ref.shape[0] // CMP_STRIDE
    first = jnp.zeros((nc, x_ref.shape[1]), F32)
    second = jnp.zeros_like(first)
    for l in range(CMP_STRIDE):
        x_l = x_ref[pl.ds(l, nc, stride=CMP_STRIDE), :]
        first = first + _dot(x_l + pe_ref[0, l:l + 1], w1_ref[0, l], precision=HIGHEST)
        second = second + _dot(x_l + pe_ref[0, CMP_STRIDE + l:CMP_STRIDE + l + 1], w1_ref[0, CMP_STRIDE + l],
                               precision=HIGHEST)
    h1 = first + pltpu.roll(second, nc - 1, 0)
    out = _dot(jax.nn.silu(h1), w2_ref[0], precision=HIGHEST)
    normed = out * lax.rsqrt(_dot_x_sel(out * out, bd_ref[...]) + RMS_EPS) * g_ref[...]
    o_ref[0, 0] = jnp.where(pl.program_id(0) == 0, normed, out).astype(BF16)


def nsa_compress(p_nsa, pe_k, pe_v, kw1, kw2, vw1, vw2, kc_g, b, s):
    nc = s // CMP_STRIDE
    gw = NSA_GROUPS * HEAD_DIM
    first_col_block = NSA_HEADS * HEAD_DIM // gw
    eye = jnp.eye(NSA_GROUPS, dtype=F32)

    def both(w):
        return jnp.einsum('gh,...ij->...gihj', eye, w).reshape(w.shape[:-2] + (gw, gw))

    w1 = both(jnp.stack([kw1, vw1]).reshape(2, CMP_BLOCK, HEAD_DIM, HEAD_DIM))
    w2 = both(jnp.stack([kw2, vw2]))
    pe = jnp.stack([jnp.tile(pe_k, (1, NSA_GROUPS)), jnp.tile(pe_v, (1, NSA_GROUPS))])
    return pl.pallas_call(
        _compress_kernel,
        grid=(2, b),
        in_specs=[pl.BlockSpec((s, gw), lambda kv, bi: (bi, first_col_block + kv)),
                  pl.BlockSpec((1, CMP_BLOCK, gw), lambda kv, bi: (kv, 0, 0)),
                  pl.BlockSpec((1, CMP_BLOCK, gw, gw), lambda kv, bi: (kv, 0, 0, 0)),
                  pl.BlockSpec((1, gw, gw), lambda kv, bi: (kv, 0, 0)),
                  pl.BlockSpec((1, gw), lambda kv, bi: (0, 0)),
                  pl.BlockSpec((gw, gw), lambda kv, bi: (0, 0))],
        out_specs=pl.BlockSpec((1, 1, nc, gw), lambda kv, bi: (kv, bi, 0, 0)),
        out_shape=jax.ShapeDtypeStruct((2, b, nc, gw), BF16),
        compiler_params=_cparams("parallel", "parallel"),
        name="nsa_compress",
    )(p_nsa, pe, w1, w2, jnp.tile(kc_g.reshape(1, HEAD_DIM), (1, NSA_GROUPS)),
      _block_diag_ones(gw, HEAD_DIM, 1.0 / HEAD_DIM))


def _masked_exp_cols(s, mask):
    sm = jnp.where(mask, s, NEG_INF)
    m = jnp.max(sm, axis=0, keepdims=True)
    p = jnp.exp(sm - jnp.where(m > 0.5 * NEG_INF, m, 0.0))
    l = jnp.sum(p, axis=0, keepdims=True)
    return p, 1.0 / jnp.where(l > 0.0, l, 1.0)


def _nsa_attn_kernel(qt_ref, kc_ref, vct_ref, ovt_ref, ks_ref, vst_ref, kw_ref, vwt_ref, gate_ref, o_ref,
                     *, n_sel, n_top):
    g = pl.program_id(1)
    t0 = pl.program_id(2) * Q_BLOCK
    ks_ref, vst_ref, kw_ref, vwt_ref = (r.at[0, 0] for r in (ks_ref, vst_ref, kw_ref, vwt_ref))
    cols = NSA_HPG * Q_BLOCK
    col = lax.broadcasted_iota(jnp.int32, (1, cols), 1)
    tq = t0 + col % Q_BLOCK
    head = g * NSA_HPG + col // Q_BLOCK
    slope = lax.bitcast_convert_type((127 - (head + 1)) << 23, F32)
    qt = jnp.concatenate([qt_ref[0, h] for h in range(NSA_HPG)], axis=1)
    frow = lax.broadcasted_iota(jnp.int32, (HEAD_DIM, cols), 0)
    tq_hi = (tq // SLC_BLOCK * SLC_BLOCK).astype(F32)
    tq_lo = (tq % SLC_BLOCK).astype(F32)
    qpos = jnp.where(frow < 2, slope, jnp.where(frow == 2, -slope * tq_hi, jnp.where(frow == 3, -slope * tq_lo, 0.0)))
    qc = jnp.concatenate([qt, qpos.astype(BF16)], axis=0)

    def heads_sum(x):
        acc = x[:, :Q_BLOCK]
        for h in range(1, NSA_HPG):
            acc = acc + x[:, h * Q_BLOCK:(h + 1) * Q_BLOCK]
        return acc

    kc = kc_ref[0, 0]
    nc = kc.shape[0]
    c_last = lax.broadcasted_iota(jnp.int32, (nc, 1), 0) * CMP_STRIDE + (CMP_BLOCK - 1)
    p_c, inv_c = _masked_exp_cols(_dot(kc, qc), c_last <= tq)
    p_c = p_c * inv_c
    o_c = _dot(vct_ref[0, 0], p_c.astype(BF16))
    imp = _dot_sel_x(ovt_ref[...], heads_sum(p_c))

    bid = lax.broadcasted_iota(jnp.int32, (LANES, Q_BLOCK), 0)
    bidf = bid.astype(F32)
    tq1 = t0 + lax.broadcasted_iota(jnp.int32, (1, Q_BLOCK), 1)
    cur = tq1 // SLC_BLOCK
    forced = (bid == 0) | (bid == cur) | (bid == cur - 1)
    score = jnp.where(forced, BIG, jnp.where(bid * SLC_BLOCK <= tq1, imp, -BIG))
    score = jnp.where(bid < n_sel, score, REMOVED)
    picked = jnp.zeros((LANES, Q_BLOCK), jnp.bool_)
    for _ in range(n_top):
        m = jnp.max(score, axis=0, keepdims=True)
        pick = bidf == jnp.min(jnp.where(score == m, bidf, float(LANES)), axis=0, keepdims=True)
        picked = picked | pick
        score = jnp.where(pick, REMOVED, score)
    sel_neg = jnp.where(picked, 0.0, NEG_INF).astype(BF16)
    qx = jnp.concatenate([qc, jnp.concatenate([sel_neg] * NSA_HPG, axis=1)], axis=0)

    def values_t(ref, first_tile, n_tiles):
        return jnp.concatenate([ref[first_tile + i] for i in range(n_tiles)], axis=1)

    span = WINDOW + Q_BLOCK
    ws = pl.multiple_of(jnp.maximum(t0 - WINDOW, 0), Q_BLOCK)
    dw = tq - (ws + lax.broadcasted_iota(jnp.int32, (span, 1), 0))
    in_window = dw.astype(jnp.uint32) < WINDOW
    p_w, inv_w = _masked_exp_cols(_dot(kw_ref[pl.ds(ws, span), :], qc), in_window)
    o_w = _dot(values_t(vwt_ref, ws // Q_BLOCK, span // Q_BLOCK), p_w.astype(BF16)) * inv_w

    def scores(chunk):
        return _dot(ks_ref[pl.ds(pl.multiple_of(chunk * SEL_CHUNK, SEL_CHUNK), SEL_CHUNK), :], qx)

    def flash(s, chunk, carry):
        m, l, acc = carry
        m_new = jnp.maximum(m, jnp.max(s, axis=0, keepdims=True))
        alpha = jnp.exp(m - m_new)
        p = jnp.exp(s - m_new)
        l = alpha * l + jnp.sum(p, axis=0, keepdims=True)
        v = values_t(vst_ref, chunk * (SEL_CHUNK // Q_BLOCK), SEL_CHUNK // Q_BLOCK)
        return m_new, l, alpha * acc + _dot(v, p.astype(BF16))

    def flash_pair(chunk_a, chunk_b, carry, keep_b=None):
        s_a, s_b = scores(chunk_a), scores(chunk_b)
        if keep_b is not None:
            s_b = jnp.where(keep_b, s_b, NEG_INF)
        return flash(s_b, chunk_b, flash(s_a, chunk_a, carry))

    bpc = SEL_CHUNK // SLC_BLOCK
    blk_any = jnp.max(jnp.where(picked, 1.0, 0.0), axis=1, keepdims=True)
    chunk_bit = lax.bitcast_convert_type((bid[:, :1] // bpc + 127) << 23, F32)
    bits = jnp.max((blk_any * chunk_bit).reshape(LANES // bpc, bpc, 1), axis=1)
    active = jnp.sum(bits, axis=0, keepdims=True)[0, 0].astype(jnp.int32)

    def full_step(j, state):
        def on_active(st):
            return lax.cond(st[0] >= 0,
                            lambda s2: (jnp.int32(-1), flash_pair(s2[0], j, s2[1])),
                            lambda s2: (j, s2[1]), st)
        return lax.cond((active >> j) & 1 == 1, on_active, lambda st: st, state)

    n_full = t0 // SEL_CHUNK
    init = (jnp.full((1, cols), NEG_INF, F32), jnp.zeros((1, cols), F32), jnp.zeros((HEAD_DIM, cols), F32))
    waiting, carry = lax.fori_loop(0, n_full, full_step, (jnp.int32(-1), init))
    keep = n_full * SEL_CHUNK + lax.broadcasted_iota(jnp.int32, (SEL_CHUNK, 1), 0) <= tq
    _, l_s, acc_s = lax.cond(
        waiting >= 0,
        lambda c: flash_pair(waiting, n_full, c, keep_b=keep),
        lambda c: flash(jnp.where(keep, scores(n_full), NEG_INF), n_full, c), carry)
    o_s = acc_s / l_s

    for h in range(NSA_HPG):
        hs = slice(h * Q_BLOCK, (h + 1) * Q_BLOCK)
        gate = lambda br: gate_ref[0, 0, br, h:h + 1, :]
        o_ref[0, h] = gate(0) * o_c[:, hs] + gate(1) * o_s[:, hs] + gate(2) * o_w[:, hs]


def nsa_attention(qt, kcv, ks_x, vs_t, kw_x, vw_t, gates, b, s):
    nq_blocks = s // Q_BLOCK
    nc = s // CMP_STRIDE
    n_cmp = nc - CMP_BLOCK // CMP_STRIDE + 1
    n_sel = s // SLC_BLOCK
    n_top = min(SLC_TOPK, n_sel)
    grp, dh = NSA_GROUPS, HEAD_DIM
    assert n_sel <= LANES and s % SEL_CHUNK == 0 and s >= WINDOW + Q_BLOCK
    c_start = np.arange(n_cmp) * CMP_STRIDE
    s_start = np.arange(n_sel) * SLC_BLOCK
    ovt = np.zeros((LANES, nc), np.float32)
    ovt[:n_sel, :n_cmp] = (np.clip(np.minimum((c_start + CMP_BLOCK)[:, None], s_start[None] + SLC_BLOCK)
                                   - np.maximum(c_start[:, None], s_start[None]), 0, None) / CMP_BLOCK).T

    c_pos = np.arange(nc) * CMP_STRIDE + (CMP_BLOCK - 1)
    c_feat = np.zeros((nc, dh), np.float32)
    c_feat[:, 0] = c_pos // SLC_BLOCK * SLC_BLOCK
    c_feat[:, 1] = c_pos % SLC_BLOCK
    c_feat[:, 2:4] = 1.0
    kc, vc = (a.reshape(b, nc, grp, dh) for a in kcv)
    kc = kc.transpose(0, 2, 1, 3)
    kc_x = jnp.concatenate([kc, jnp.broadcast_to(jnp.asarray(c_feat, BF16), kc.shape)], axis=-1)
    vct = vc.transpose(0, 2, 3, 1)
    gates_t = gates[:, :NSA_HEADS * 3].reshape(b, s, grp, NSA_HPG, 3).transpose(0, 2, 4, 3, 1)

    grp_spec = lambda *shape: pl.BlockSpec((1, 1) + shape, lambda bi, g, i: (bi, g) + (0,) * len(shape),
                                           pipeline_mode=pl.Buffered(1))
    return pl.pallas_call(
        functools.partial(_nsa_attn_kernel, n_sel=n_sel, n_top=n_top),
        grid=(b, grp, nq_blocks),
        in_specs=[pl.BlockSpec((1, NSA_HPG, dh, Q_BLOCK), lambda bi, g, i: (bi, g, 0, i)),
                  grp_spec(nc, 2 * dh), grp_spec(dh, nc),
                  pl.BlockSpec((LANES, nc), lambda bi, g, i: (0, 0)),
                  grp_spec(s, 2 * dh + LANES), grp_spec(s // Q_BLOCK, dh, Q_BLOCK),
                  grp_spec(s, 2 * dh), grp_spec(s // Q_BLOCK, dh, Q_BLOCK),
                  pl.BlockSpec((1, 1, 3, NSA_HPG, Q_BLOCK), lambda bi, g, i: (bi, g, 0, 0, i))],
        out_specs=pl.BlockSpec((1, NSA_HPG, dh, Q_BLOCK), lambda bi, g, i: (bi, g, 0, i)),
        out_shape=jax.ShapeDtypeStruct((b, NSA_HEADS, dh, s), F32),
        compiler_params=_cparams("parallel", "parallel", "arbitrary"),
        name="nsa_attention",
    )(qt, kc_x, vct, jnp.asarray(ovt, BF16), ks_x, vs_t, kw_x, vw_t, gates_t)


def _rwkv_prep_kernel(p_ref, prev_ref, mu_ref, w0_ref, a0_ref, kk_ref, ka_ref, rk_ref,
                      wup_ref, aup_ref, gup_ref, bd_ref, ltri_ref, lones_ref, csum_ref,
                      at_o, bt_o, kt_o, rt_o, v_o, bw_o, kw_o, wc_o, g_o, bonus_o, *, tiles_per_seq):
    p = p_ref[...]
    tm = p.shape[0]
    first = pl.program_id(0) % tiles_per_seq == 0
    last_prev = jnp.where(first, 0.0, prev_ref[7:8, :])
    prev = pltpu.roll(p, 1, 0)
    prev = jnp.where(lax.broadcasted_iota(jnp.int32, (tm, 1), 0) == 0, last_prev, prev)
    pm = p + (prev - p) * mu_ref[...]
    d = RWKV_DIM
    r, k, v = pm[:, :d], pm[:, d:2 * d], pm[:, 2 * d:3 * d]
    lora = pm[:, 3 * d:3 * d + LANES]
    gd = pm[:, 3 * d + LANES:3 * d + 2 * LANES]
    z = -(w0_ref[...] + _dot(jnp.tanh(lora).astype(BF16), wup_ref[...]))
    softplus = jnp.maximum(z, 0.0) + jnp.log(1.0 + jnp.exp(-jnp.abs(z)))
    w = -softplus - 0.5
    a = jax.nn.sigmoid(a0_ref[...] + _dot(lora.astype(BF16), aup_ref[...]))
    g_o[...] = _dot(jax.nn.sigmoid(gd).astype(BF16), gup_ref[...])
    bd = bd_ref[...]
    kkr = k * kk_ref[...]
    kk = kkr / jnp.maximum(jnp.sqrt(_dot_x_sel(kkr * kkr, bd)), 1e-12)
    k2 = k * (1.0 + (a - 1.0) * ka_ref[...])
    bonus_o[...] = _dot_x_sel(r * k2 * rk_ref[...], bd) * v
    lw = -jnp.exp(w)
    lw_parts = _split_bf16(lw)
    cum = _dot_sel_x(ltri_ref[...], lw)
    tot = _dot_sel_x(lones_ref[...], lw)
    e_in = jnp.exp(cum)
    e_out = jnp.exp(-cum)
    e_end = jnp.exp(tot - cum)

    def put_heads(o, val):
        for h in range(RWKV_HEADS):
            o[0, h] = val[:, h * HEAD_DIM:(h + 1) * HEAD_DIM].astype(o.dtype)

    put_heads(at_o, -kk * jnp.exp(cum - lw))
    put_heads(bt_o, kk * a * e_out)
    put_heads(kt_o, k2 * e_out)
    put_heads(rt_o, r * e_in)
    put_heads(v_o, v)
    put_heads(bw_o, kk * a * e_end)
    put_heads(kw_o, k2 * e_end)
    put_heads(wc_o, jnp.exp(sum(_dot(csum_ref[...], p) for p in lw_parts)))


def rwkv_prep(p_rwkv, mu, w0, w_up, a0, a_up, g_up, k_k, k_a, r_k, b, s, tm):
    t = p_rwkv.shape[0]
    d = RWKV_DIM
    c = RWKV_CHUNK
    tps = s // tm
    cpt = tm // c
    wup = jnp.concatenate([w_up, jnp.zeros_like(a_up)], axis=0).astype(BF16)
    aup = jnp.concatenate([jnp.zeros_like(w_up), a_up], axis=0).astype(BF16)
    i = np.arange(tm)
    j = np.arange(MXU_DIM)
    same = (j[:, None] // c) == (j[None, :] // c)
    ltri = jnp.asarray(same & (j[:, None] >= j[None, :]), BF16)
    lones = jnp.asarray(same, BF16)
    csum = jnp.asarray(np.arange(cpt)[:, None] == (i[None, :] // c), BF16)
    row = lambda w: pl.BlockSpec((tm, w), lambda i: (i, 0))
    full = lambda *sh: pl.BlockSpec(sh, lambda i: (0,) * len(sh))
    heads = lambda n: pl.BlockSpec((1, RWKV_HEADS, n, HEAD_DIM), lambda i: (i // tps, 0, i % tps, 0))
    hshape = lambda n, dt: jax.ShapeDtypeStruct((b, RWKV_HEADS, n, HEAD_DIM), dt)
    vec = lambda x: x.reshape(1, -1)
    return pl.pallas_call(
        functools.partial(_rwkv_prep_kernel, tiles_per_seq=tps),
        grid=(t // tm,),
        in_specs=[row(RWKV_PROJ),
                  pl.BlockSpec((8, RWKV_PROJ), lambda i: (jnp.maximum(i * (tm // 8) - 1, 0), 0)),
                  full(1, RWKV_PROJ), full(1, d), full(1, d), full(1, d), full(1, d), full(1, d),
                  full(LANES, d), full(LANES, d), full(LANES, d), full(MXU_DIM, MXU_DIM),
                  full(MXU_DIM, MXU_DIM), full(MXU_DIM, MXU_DIM), full(cpt, tm)],
        out_specs=[heads(tm)] * 7 + [heads(cpt), row(d), row(d)],
        out_shape=[hshape(s, BF16)] * 7 + [hshape(s // c, F32)] + [jax.ShapeDtypeStruct((t, d), F32)] * 2,
        compiler_params=_cparams("parallel"),
        name="rwkv_prep",
    )(p_rwkv, p_rwkv, vec(mu), vec(w0), vec(a0), vec(k_k), vec(k_a), vec(r_k), wup, aup, g_up.astype(BF16),
      _block_diag_ones(MXU_DIM, HEAD_DIM), ltri, lones, csum)


def _bdot(a, b):
    return lax.dot_general(a, b, (((2,), (1,)), ((0,), (0,))), preferred_element_type=F32)


def _bdot_nt(a, b):
    return lax.dot_general(a, b, (((2,), (2,)), ((0,), (0,))), preferred_element_type=F32)


def _bdot_tn(a, b):
    return lax.dot_general(a, b, (((1,), (1,)), ((0,), (0,))), preferred_element_type=F32)


def _rwkv_intra_kernel(at_ref, bt_ref, kt_ref, rt_ref, v_ref, ta_o, tr_o, arb_o, yv_o):
    c = RWKV_CHUNK
    _, nh, ts, dh = at_ref.shape
    n = nh * (ts // c)
    chunked = lambda ref: ref[0].reshape(n, c, dh)
    at, bt, kt, rt, v = (chunked(r) for r in (at_ref, bt_ref, kt_ref, rt_ref, v_ref))
    ri = lax.broadcasted_iota(jnp.int32, (1, c, c), 1)
    ci = lax.broadcasted_iota(jnp.int32, (1, c, c), 2)
    strict = ri > ci
    incl = ri >= ci
    ar = jnp.concatenate([at, rt], axis=1)
    xb = _bdot_nt(ar, bt)
    xk = _bdot_nt(ar, kt)
    l_ab = jnp.where(strict, xb[:, :c], 0.0)
    a_ak = jnp.where(strict, xk[:, :c], 0.0)
    a_rb = jnp.where(incl, xb[:, c:], 0.0)
    a_rk = jnp.where(incl, xk[:, c:], 0.0)
    pw = l_ab
    tinv = jnp.where(ri == ci, 1.0, 0.0) + l_ab
    for _ in range(int(np.log2(c)) - 1):
        pw_b = pw.astype(BF16)
        pw = _bdot(pw_b, pw_b)
        tinv = tinv + _bdot(tinv.astype(BF16), pw.astype(BF16))
    tinv_b = tinv.astype(BF16)

    def put(o, val):
        o[0] = val.reshape(nh, ts, val.shape[-1]).astype(o.dtype)

    put(ta_o, _bdot(tinv_b, at))
    put(tr_o, _bdot(tinv_b, _bdot(a_ak.astype(BF16), v).astype(BF16)))
    put(arb_o, a_rb)
    put(yv_o, _bdot(a_rk.astype(BF16), v))


def rwkv_intra(at, bt, kt, rt, v, ts):
    b, h, s, dh = at.shape
    seq = lambda: pl.BlockSpec((1, h, ts, dh), lambda bi, i: (bi, 0, i, 0))
    shp = lambda dt: jax.ShapeDtypeStruct((b, h, s, dh), dt)
    return pl.pallas_call(
        _rwkv_intra_kernel,
        grid=(b, s // ts),
        in_specs=[seq()] * 5,
        out_specs=[seq()] * 4,
        out_shape=[shp(BF16), shp(F32), shp(BF16), shp(F32)],
        compiler_params=_cparams("parallel", "parallel"),
        name="rwkv_intra",
    )(at, bt, kt, rt, v)


def _rwkv_scan_kernel(ta_ref, tr_ref, arb_ref, yv_ref, rt_ref, v_ref, bw_ref, kw_ref, wc_ref, y_ref, st_ref):
    c = RWKV_CHUNK
    nb, nh, ts, dh = ta_ref.shape
    n = nb * nh

    @pl.when(pl.program_id(0) == 0)
    def _():
        st_ref[...] = jnp.zeros_like(st_ref)

    def chunk_step(j, _):
        sl = (slice(None), slice(None), pl.ds(pl.multiple_of(j * c, c), c), slice(None))
        get = lambda ref: ref[sl].reshape(n, c, dh)
        st = st_ref[...]
        st_b = st.astype(BF16)
        u = _bdot_nt(get(ta_ref), st_b) + get(tr_ref)
        u_b = u.astype(BF16)
        y = _bdot_nt(get(rt_ref), st_b) + _bdot(get(arb_ref), u_b) + get(yv_ref)
        wc = wc_ref[:, :, pl.ds(pl.program_id(0) * (ts // c) + j, 1), :].reshape(n, 1, dh)
        st_ref[...] = st * wc + _bdot_tn(jnp.concatenate([u_b, get(v_ref)], axis=1),
                                         jnp.concatenate([get(bw_ref), get(kw_ref)], axis=1))
        y_ref[sl] = y.reshape(nb, nh, c, dh)
        return 0

    lax.fori_loop(0, ts // c, chunk_step, 0)


def rwkv_scan(ta, tr, arb, yv, rt, v, bw, kw, wc, ts):
    b, h, s, dh = ta.shape
    seq = lambda n: pl.BlockSpec((b, h, n, dh), lambda i: (0, 0, i, 0))
    return pl.pallas_call(
        _rwkv_scan_kernel,
        grid=(s // ts,),
        in_specs=[seq(ts)] * 8 + [pl.BlockSpec(wc.shape, lambda i: (0, 0, 0, 0))],
        out_specs=seq(ts),
        out_shape=jax.ShapeDtypeStruct((b, h, s, dh), F32),
        scratch_shapes=[pltpu.VMEM((b * h, dh, dh), F32)],
        compiler_params=_cparams("arbitrary"),
        name="rwkv_scan",
    )(ta, tr, arb, yv, rt, v, bw, kw, wc)


def _out_proj_kernel(x_ref, on_ref, y_ref, bonus_ref, g_ref, lnw_ref, lnb_ref, bd_ref, wn_ref, wr_ref, o_ref):
    y = jnp.concatenate([y_ref[0, h] for h in range(RWKV_HEADS)], axis=-1)
    bd = bd_ref[...]
    yc = y - _dot_x_sel(y, bd)
    yn = yc * lax.rsqrt(_dot_x_sel(yc * yc, bd) + GN_EPS)
    o_rwkv = (yn * lnw_ref[...] + lnb_ref[...] + bonus_ref[...]) * g_ref[...]
    tm = y.shape[0]
    o_nsa_t = on_ref[0].reshape(NSA_HEADS * HEAD_DIM, tm)
    o_ref[...] = (x_ref[...] + _dot_tn(o_nsa_t.astype(BF16), wn_ref[...])
                  + _dot(o_rwkv.astype(BF16), wr_ref[...]))


def out_proj(x, o_nsa_t, y, bonus, g, ln_w, ln_b, w_out, s, tm):
    t, d = x.shape
    dn = o_nsa_t.shape[1] * o_nsa_t.shape[2]
    dr = bonus.shape[1]
    tps = s // tm
    row = lambda w: pl.BlockSpec((tm, w), lambda i: (i, 0))
    full = lambda *sh: pl.BlockSpec(sh, lambda i: (0,) * len(sh))
    return pl.pallas_call(
        _out_proj_kernel,
        grid=(t // tm,),
        in_specs=[row(d), pl.BlockSpec((1, NSA_HEADS, HEAD_DIM, tm), lambda i: (i // tps, 0, 0, i % tps)),
                  pl.BlockSpec((1, RWKV_HEADS, tm, HEAD_DIM), lambda i: (i // tps, 0, i % tps, 0)),
                  row(dr), row(dr), full(1, dr), full(1, dr), full(MXU_DIM, MXU_DIM), full(dn, d), full(dr, d)],
        out_specs=row(d),
        out_shape=jax.ShapeDtypeStruct((t, d), F32),
        compiler_params=_cparams("parallel"),
        name="out_proj",
    )(x, o_nsa_t, y, bonus, g, ln_w.reshape(1, dr), ln_b.reshape(1, dr),
      _block_diag_ones(MXU_DIM, HEAD_DIM, 1.0 / HEAD_DIM), w_out[:dn].astype(BF16), w_out[dn:].astype(BF16))


def _cross_attn_kernel(h_ref, g_ref, wq_ref, qg_ref, kv_ref, kg_ref, wo_ref, o_ref):
    h = h_ref[...]
    d = h.shape[1]
    xd = d // X_HEADS
    q = _dot(_rms(h, g_ref[...]).astype(BF16), wq_ref[...])
    kv = kv_ref[0]
    outs = []
    for hd in range(X_HEADS):
        qh = _rms(q[:, hd * xd:(hd + 1) * xd], qg_ref[...]) * (xd ** -0.5)
        kh = _rms(kv[:, hd * xd:(hd + 1) * xd], kg_ref[...])
        vh = kv[:, d + hd * xd:d + (hd + 1) * xd]
        s = _dot_nt(qh.astype(BF16), kh.astype(BF16))
        p = jnp.exp(s - jnp.max(s, axis=-1, keepdims=True))
        p = p / jnp.sum(p, axis=-1, keepdims=True)
        outs.append(_dot(p.astype(BF16), vh.astype(BF16)))
    o = jnp.concatenate(outs, axis=-1)
    o_ref[...] = h + _dot(o.astype(BF16), wo_ref[...])


def cross_attention(h, kv, norm_g, xq_w, xq_g, xk_g, xo_w, b, s, tm):
    t, d = h.shape
    m = kv.shape[1]
    xd = d // X_HEADS
    tiles = s // tm
    full = lambda *sh: pl.BlockSpec(sh, lambda i: (0,) * len(sh))
    return pl.pallas_call(
        _cross_attn_kernel,
        grid=(t // tm,),
        in_specs=[pl.BlockSpec((tm, d), lambda i: (i, 0)), full(1, d), full(d, d), full(1, xd),
                  pl.BlockSpec((1, m, 2 * d), lambda i: (i // tiles, 0, 0)), full(1, xd), full(d, d)],
        out_specs=pl.BlockSpec((tm, d), lambda i: (i, 0)),
        out_shape=jax.ShapeDtypeStruct((t, d), F32),
        compiler_params=_cparams("parallel"),
        name="cross_attention",
    )(h, norm_g.reshape(1, d), xq_w.astype(BF16), xq_g.reshape(1, xd), kv, xk_g.reshape(1, xd),
      xo_w.astype(BF16))


def _router_kernel(h_ref, g_ref, rw_ref, rb_ref, ltri_ref, xn_o, idx_o, gate_o, rank_o, count_o, seen_ref):
    @pl.when(pl.program_id(0) == 0)
    def _():
        seen_ref[...] = jnp.zeros_like(seen_ref)

    xn = _rms(h_ref[...], g_ref[...])
    xn_o[...] = xn
    logits = _dot(xn, rw_ref[...], precision=HIGHEST) + rb_ref[...]
    tm = logits.shape[0]
    lane = lax.broadcasted_iota(jnp.int32, (tm, LANES), 1)
    lanef = lane.astype(F32)
    logits = jnp.where(lane < N_EXPERTS, logits, REMOVED)
    idx_acc = jnp.zeros((tm, LANES), F32)
    val_acc = jnp.zeros((tm, LANES), F32)
    chosen = jnp.zeros((tm, LANES), F32)
    picks = []
    top = None
    for k in range(TOP_K):
        m = jnp.max(logits, axis=-1, keepdims=True)
        idx = jnp.min(jnp.where(logits == m, lanef, float(LANES)), axis=-1, keepdims=True)
        pick = lanef == idx
        picks.append(pick)
        chosen = jnp.where(pick, 1.0, chosen)
        logits = jnp.where(pick, REMOVED, logits)
        top = m if top is None else top
        idx_acc = jnp.where(lane == k, idx, idx_acc)
        val_acc = jnp.where(lane == k, jnp.exp(m - top), val_acc)
    idx_o[...] = idx_acc.astype(jnp.int32)
    gate_o[...] = val_acc / jnp.sum(val_acc, axis=-1, keepdims=True)
    before = seen_ref[0:1, :] + _dot(ltri_ref[...], chosen.astype(BF16))
    rank_acc = jnp.zeros((tm, LANES), F32)
    for k, pick in enumerate(picks):
        rank_acc = jnp.where(lane == k, jnp.sum(jnp.where(pick, before, 0.0), axis=-1, keepdims=True), rank_acc)
    rank_o[...] = rank_acc.astype(jnp.int32)
    seen_ref[...] = seen_ref[...] + jnp.sum(chosen, axis=0, keepdims=True)
    count_o[...] = seen_ref[...]


def moe_router(h, norm_g, router_w, router_b, tm):
    t, d = h.shape
    rw = jnp.zeros((d, LANES), F32).at[:, :N_EXPERTS].set(router_w)
    rb = jnp.zeros((1, LANES), F32).at[0, :N_EXPERTS].set(router_b)
    i = np.arange(tm)
    ltri = jnp.asarray(i[:, None] > i[None, :], BF16)
    row = lambda w: pl.BlockSpec((tm, w), lambda i: (i, 0))
    full = lambda *s: pl.BlockSpec(s, lambda i: (0,) * len(s))
    return pl.pallas_call(
        _router_kernel,
        grid=(t // tm,),
        in_specs=[row(d), full(1, d), full(d, LANES), full(1, LANES), full(tm, tm)],
        out_specs=[row(d), row(LANES), row(LANES), row(LANES), full(8, LANES)],
        out_shape=[jax.ShapeDtypeStruct((t, d), F32), jax.ShapeDtypeStruct((t, LANES), jnp.int32),
                   jax.ShapeDtypeStruct((t, LANES), F32), jax.ShapeDtypeStruct((t, LANES), jnp.int32),
                   jax.ShapeDtypeStruct((8, LANES), F32)],
        scratch_shapes=[pltpu.VMEM((8, LANES), F32)],
        compiler_params=_cparams("arbitrary"),
        name="moe_router",
    )(h, norm_g.reshape(1, d), rw, rb, ltri)


def _expert_kernel(blk_e_ref, n_used_ref, x_ref, w1_ref, b1_ref, w2_ref, b2_ref, o_ref, w2x_ref):
    i = pl.program_id(0)
    f = w2_ref.shape[1]
    half_lanes = LANES // 2
    n_merged = f // LANES

    @pl.when((i == 0) | (blk_e_ref[i] != blk_e_ref[jnp.maximum(i - 1, 0)]))
    def _():
        for s in range(w2x_ref.shape[0]):
            lanes = slice(s * LANES, (s + 1) * LANES)
            for c in range(n_merged):
                lo = w2_ref[0, c * half_lanes:(c + 1) * half_lanes, lanes]
                hi = w2_ref[0, (c + n_merged) * half_lanes:(c + n_merged + 1) * half_lanes, lanes]
                w2x_ref[s, pl.ds(c * LANES, half_lanes, stride=2), :] = lo
                w2x_ref[s, pl.ds(c * LANES + 1, half_lanes, stride=2), :] = hi

    @pl.when(i < n_used_ref[0])
    def _():
        x = x_ref[...].astype(BF16)
        h = _dot(x, w1_ref[0].astype(BF16)) + b1_ref[0]
        hg = jnp.minimum(h, SWIGLU_LIMIT)
        gate = hg * jax.nn.sigmoid(SWIGLU_ALPHA * hg)
        lin = jnp.clip(h, -SWIGLU_LIMIT, SWIGLU_LIMIT) + 1.0
        even = lax.broadcasted_iota(jnp.int32, (1, LANES), 1) % 2 == 0
        def swiglu(c):
            cols = slice(c * LANES, (c + 1) * LANES)
            nxt = pltpu.roll(lin[:, cols], LANES - 1, 1)
            return jnp.where(even, gate[:, cols] * nxt, 0.0)

        act = jnp.concatenate([(swiglu(c) + pltpu.roll(swiglu(c + n_merged), 1, 1)).astype(BF16)
                               for c in range(n_merged)], axis=1)
        w2x = jnp.concatenate([w2x_ref[s] for s in range(w2x_ref.shape[0])], axis=1)
        o_ref[...] = _dot(act, w2x.astype(BF16)) + b2_ref[0]

    @pl.when(i >= n_used_ref[0])
    def _():
        o_ref[...] = jnp.zeros_like(o_ref)


def moe_experts(xs, blk_e, n_used, w1, b1, w2, b2):
    r, d = xs.shape
    f2 = w1.shape[2]
    m = MOE_ROW_BLOCK
    ex = lambda *s: pl.BlockSpec((1,) + s, lambda i, be, nu: (be[i],) + (0,) * len(s))
    grid_spec = pltpu.PrefetchScalarGridSpec(
        num_scalar_prefetch=2,
        grid=(r // m,),
        in_specs=[pl.BlockSpec((m, d), lambda i, be, nu: (i, 0)),
                  ex(d, f2), ex(1, f2), ex(f2 // 2, d), ex(1, d)],
        out_specs=pl.BlockSpec((m, d), lambda i, be, nu: (i, 0)),
        scratch_shapes=[pltpu.VMEM((d // LANES, f2 // 2, LANES), F32)],
    )
    return pl.pallas_call(
        _expert_kernel,
        grid_spec=grid_spec,
        out_shape=jax.ShapeDtypeStruct((r, d), F32),
        compiler_params=_cparams("arbitrary"),
        name="moe_experts",
    )(blk_e, n_used, xs, w1, b1, w2, b2)


def _combine_kernel(h_ref, gate_ref, *refs):
    *y_refs, o_ref = refs
    acc = h_ref[...]
    for k, y_ref in enumerate(y_refs):
        acc = acc + gate_ref[:, k:k + 1] * y_ref[...]
    o_ref[...] = acc


def moe_combine(h, gate, ys_k, tm):
    t, d = h.shape
    row = lambda w: pl.BlockSpec((tm, w), lambda i: (i, 0))
    return pl.pallas_call(
        _combine_kernel,
        grid=(t // tm,),
        in_specs=[row(d), row(LANES)] + [row(d)] * len(ys_k),
        out_specs=row(d),
        out_shape=jax.ShapeDtypeStruct((t, d), F32),
        compiler_params=_cparams("parallel"),
        name="moe_combine",
    )(h, gate, *ys_k)


def _layer(x, mem, norm_mix_g, w_in, q_norm_g, k_cmp_norm_g, k_slc_norm_g, k_win_norm_g,
           cmp_pe_k, cmp_pe_v, cmp_k_w1, cmp_k_w2, cmp_v_w1, cmp_v_w2,
           rwkv_mu, rwkv_w0, rwkv_w_up, rwkv_a0, rwkv_a_up, rwkv_g_up, rwkv_k_k, rwkv_k_a,
           rwkv_r_k, rwkv_ln_w, rwkv_ln_b, w_out,
           norm_x_g, norm_mem_g, xq_w, xk_w, xv_w, xq_norm_g, xk_norm_g, xo_w,
           norm_ffn_g, router_w, router_b, mlp1_w, mlp1_b, mlp2_w, mlp2_b):
    b, s, d = x.shape
    t = b * s
    tm = 512
    xt = x.reshape(t, d)

    w_nsa = jnp.pad(w_in[:, :NSA_PROJ], ((0, 0), (0, NSA_PROJ_PAD - NSA_PROJ)))
    p_nsa, p_rwkv = norm_matmul(xt, norm_mix_g, [w_nsa, w_in[:, NSA_PROJ:]], tm)

    qn, ks, vs, kw, vw, gates = nsa_prep(p_nsa, q_norm_g, k_slc_norm_g, k_win_norm_g, b, s, tm)
    kcv = nsa_compress(p_nsa, cmp_pe_k, cmp_pe_v, cmp_k_w1, cmp_k_w2, cmp_v_w1, cmp_v_w2, k_cmp_norm_g, b, s)
    o_nsa = nsa_attention(qn, kcv, ks, vs, kw, vw, gates, b, s)

    at, bt, kt, rt, v, bw, kwd, wc, g_gate, bonus = rwkv_prep(
        p_rwkv, rwkv_mu, rwkv_w0, rwkv_w_up, rwkv_a0, rwkv_a_up, rwkv_g_up, rwkv_k_k, rwkv_k_a, rwkv_r_k, b, s, tm)
    ta, tr, arb, yv = rwkv_intra(at, bt, kt, rt, v, ts=256)
    y = rwkv_scan(ta, tr, arb, yv, rt, v, bw, kwd, wc, ts=256)

    h1 = out_proj(xt, o_nsa, y, bonus, g_gate, rwkv_ln_w, rwkv_ln_b, w_out, s, tm)
    m = mem.shape[1]
    kv, = norm_matmul(mem.reshape(b * m, d), norm_mem_g, [jnp.concatenate([xk_w, xv_w], axis=1)], m)
    h2 = cross_attention(h1, kv.reshape(b, m, 2 * d), norm_x_g, xq_w, xq_norm_g, xk_norm_g, xo_w, b, s, tm)

    xn, top_i, gate, rank, seen = moe_router(h2, norm_ffn_g, router_w, router_b, tm)
    top_i = top_i[:, :TOP_K]
    a = t * TOP_K
    mb = MOE_ROW_BLOCK
    idx_bits = max(a - 1, 1).bit_length()
    assert N_EXPERTS << idx_bits < 2 ** 31
    packed = (top_i.reshape(a) << idx_bits) | jnp.arange(a, dtype=jnp.int32)
    order = jnp.sort(packed) & ((1 << idx_bits) - 1)
    counts = seen[0, :N_EXPERTS].astype(jnp.int32)
    starts = jnp.cumsum(counts) - counts
    padded = (counts + mb - 1) // mb * mb
    pends = jnp.cumsum(padded)
    pstarts = pends - padded
    pos = pstarts[top_i] + rank[:, :TOP_K]
    n_blocks = -(-a // mb) + N_EXPERTS
    r = n_blocks * mb
    blk_start = jnp.arange(n_blocks, dtype=jnp.int32) * mb
    blk_e = jnp.minimum(jnp.sum(pends[None, :] <= blk_start[:, None], axis=1), N_EXPERTS - 1).astype(jnp.int32)
    src_i = blk_start[:, None] + jnp.arange(mb, dtype=jnp.int32)[None, :] - (pstarts - starts)[blk_e][:, None]
    valid = src_i < (starts + counts)[blk_e][:, None]
    row_src = jnp.where(valid, order[jnp.minimum(src_i, a - 1)] // TOP_K, 0).astype(jnp.int32).reshape(r)
    n_used = (pends[-1] // mb).astype(jnp.int32).reshape(1)
    xs = xn.at[row_src].get(mode="promise_in_bounds")
    f2 = mlp1_w.shape[2]
    ys = moe_experts(xs, blk_e, n_used, mlp1_w, mlp1_b.reshape(N_EXPERTS, 1, f2), mlp2_w,
                     mlp2_b.reshape(N_EXPERTS, 1, d))
    ys_k = [ys.at[pos[:, k]].get(mode="promise_in_bounds") for k in range(TOP_K)]
    out = moe_combine(h2, gate, ys_k, 256)
    return out.reshape(b, s, d)


def kernel(x, mem, norm_mix_g, w_in, q_norm_g, k_cmp_norm_g, k_slc_norm_g, k_win_norm_g, cmp_pe_k, cmp_pe_v, cmp_k_w1, cmp_k_w2, cmp_v_w1, cmp_v_w2, rwkv_mu, rwkv_w0, rwkv_w_up, rwkv_a0, rwkv_a_up, rwkv_g_up, rwkv_k_k, rwkv_k_a, rwkv_r_k, rwkv_ln_w, rwkv_ln_b, w_out, norm_x_g, norm_mem_g, xq_w, xk_w, xv_w, xq_norm_g, xk_norm_g, xo_w, norm_ffn_g, router_w, router_b, mlp1_w, mlp1_b, mlp2_w, mlp2_b):
    params = (norm_mix_g, w_in, q_norm_g, k_cmp_norm_g, k_slc_norm_g, k_win_norm_g, cmp_pe_k, cmp_pe_v,
              cmp_k_w1, cmp_k_w2, cmp_v_w1, cmp_v_w2, rwkv_mu, rwkv_w0, rwkv_w_up, rwkv_a0, rwkv_a_up,
              rwkv_g_up, rwkv_k_k, rwkv_k_a, rwkv_r_k, rwkv_ln_w, rwkv_ln_b, w_out, norm_x_g, norm_mem_g,
              xq_w, xk_w, xv_w, xq_norm_g, xk_norm_g, xo_w, norm_ffn_g, router_w, router_b,
              mlp1_w, mlp1_b, mlp2_w, mlp2_b)
    h = x
    for layer in range(norm_mix_g.shape[0]):
        h = _layer(h, mem, *[prm[layer] for prm in params])
    return h
```

```python
import functools

import numpy as np
import jax
import jax.numpy as jnp
from jax import lax
from jax.experimental import pallas as pl
from jax.experimental.pallas import tpu as pltpu

F32 = jnp.float32
BF16 = jnp.bfloat16
HIGHEST = lax.Precision.HIGHEST

V7X_VMEM_BYTES = 64 * 1024 * 1024
VMEM_LIMIT = V7X_VMEM_BYTES * 3 // 4

HEAD_DIM = 64
NSA_HEADS = 8
NSA_GROUPS = 2
NSA_HPG = NSA_HEADS // NSA_GROUPS
GROUP_W = NSA_HPG * HEAD_DIM
CMP_BLOCK = 32
CMP_STRIDE = 16
SLC_BLOCK = 64
SLC_TOPK = 16
WINDOW = 512
Q_BLOCK = 128
SEL_CHUNK = 512
RWKV_HEADS = 8
RWKV_DIM = RWKV_HEADS * HEAD_DIM
RWKV_CHUNK = 64
GN_EPS = HEAD_DIM * 1e-5
X_HEADS = 4
N_EXPERTS = 32
TOP_K = 4
SWIGLU_LIMIT = 7.0
SWIGLU_ALPHA = 1.702
MOE_ROW_BLOCK = 256
RMS_EPS = 1e-6
NEG_INF = -1e30
BIG = 1e9
REMOVED = -3e38
LANES = 128
MXU_DIM = 256

NSA_PROJ = NSA_HEADS * HEAD_DIM + 6 * NSA_GROUPS * HEAD_DIM + NSA_HEADS * 3
NSA_PROJ_PAD = -(-NSA_PROJ // LANES) * LANES
RWKV_PROJ = 3 * RWKV_DIM + 64 + 64 + 128


def _cparams(*sem):
    return pltpu.CompilerParams(dimension_semantics=sem, vmem_limit_bytes=VMEM_LIMIT)


def _dot(a, b, **kw):
    return jnp.dot(a, b, preferred_element_type=F32, **kw)


def _dot_nt(a, b, **kw):
    return lax.dot_general(a, b, (((1,), (1,)), ((), ())), preferred_element_type=F32, **kw)


def _dot_tn(a, b, **kw):
    return lax.dot_general(a, b, (((0,), (0,)), ((), ())), preferred_element_type=F32, **kw)


def _rms(x, g):
    return x * lax.rsqrt(jnp.mean(x * x, axis=-1, keepdims=True) + RMS_EPS) * g


def _split_bf16(x, terms=3):
    parts = []
    for _ in range(terms):
        hi = x.astype(BF16)
        parts.append(hi)
        x = x - hi.astype(F32)
    return parts


def _dot_x_sel(x, sel):
    w = sel.shape[0]
    parts = _split_bf16(x)
    slabs = [sum(_dot(p[:, c:c + w], sel) for p in parts) for c in range(0, x.shape[1], w)]
    return slabs[0] if len(slabs) == 1 else jnp.concatenate(slabs, axis=1)


def _dot_sel_x(sel, x):
    w = sel.shape[1]
    parts = _split_bf16(x)
    slabs = [sum(_dot(sel, p[r:r + w]) for p in parts) for r in range(0, x.shape[0], w)]
    return slabs[0] if len(slabs) == 1 else jnp.concatenate(slabs, axis=0)


def _pack_bf16_pairs(x):
    n = x.shape[1] // 2
    bits = lambda v: lax.bitcast_convert_type(v.astype(BF16).astype(F32), jnp.uint32)
    return lax.bitcast_convert_type((bits(x[:, :n]) >> 16) | bits(x[:, n:]), F32)


def _unpack_bf16_pairs(w):
    u = lax.bitcast_convert_type(w, jnp.uint32)
    lo = lax.bitcast_convert_type(u << 16, F32)
    hi = lax.bitcast_convert_type(u & jnp.uint32(0xFFFF0000), F32)
    return jnp.concatenate([lo, hi], axis=1)


def _block_diag_ones(n, blk, scale=1.0):
    i = np.arange(n)
    return jnp.asarray(((i[:, None] // blk) == (i[None, :] // blk)).astype(np.float32) * scale, BF16)


def _norm_matmul_kernel(x_ref, g_ref, *refs):
    n = len(refs) // 2
    xn = _rms(x_ref[...], g_ref[...]).astype(BF16)
    for w_ref, o_ref in zip(refs[:n], refs[n:]):
        o_ref[...] = _dot(xn, w_ref[...])


def norm_matmul(x, g, ws, tm):
    m, d = x.shape
    return pl.pallas_call(
        _norm_matmul_kernel,
        grid=(m // tm,),
        in_specs=[pl.BlockSpec((tm, d), lambda i: (i, 0)), pl.BlockSpec((1, d), lambda i: (0, 0))]
                 + [pl.BlockSpec((d, w.shape[1]), lambda i: (0, 0)) for w in ws],
        out_specs=[pl.BlockSpec((tm, w.shape[1]), lambda i: (i, 0)) for w in ws],
        out_shape=[jax.ShapeDtypeStruct((m, w.shape[1]), F32) for w in ws],
        compiler_params=_cparams("parallel"),
        name="norm_matmul",
    )(x, g.reshape(1, d), *[w.astype(BF16) for w in ws])


def _nsa_prep_kernel(p_ref, qg_ref, ksg_ref, kwg_ref, bdq_ref, bdk_ref, place_ref, feat_ref,
                     qt_o, ks_o, vst_o, kw_o, vwt_o, gate_o):
    p = p_ref[...]
    tm = p.shape[0]
    nq = NSA_HEADS * HEAD_DIM
    gw = NSA_GROUPS * HEAD_DIM
    q = p[:, :nq]
    msq = _dot_x_sel(q * q, bdq_ref[...])
    qn = q * lax.rsqrt(msq + RMS_EPS) * qg_ref[...] * (HEAD_DIM ** -0.5)
    qt_o[0] = qn.T.reshape(NSA_HEADS, HEAD_DIM, tm).astype(BF16)

    def seg(k):
        return p[:, nq + k * gw: nq + (k + 1) * gw]

    def head_norm(t, g):
        ms = _dot_x_sel(t * t, bdk_ref[...])
        return t * lax.rsqrt(ms + RMS_EPS) * g

    feat = feat_ref[...].astype(F32)
    ks = head_norm(seg(2), ksg_ref[...]).astype(BF16)
    kw = head_norm(seg(4), kwg_ref[...]).astype(BF16)
    vs_t = seg(3).T
    vw_t = seg(5).T
    for g in range(NSA_GROUPS):
        ks_o[0, g] = (_dot(ks, place_ref[g]) + feat).astype(BF16)
        kw_o[0, g] = (_dot(kw, place_ref[g])[:, :2 * HEAD_DIM] + feat[:, :2 * HEAD_DIM]).astype(BF16)
        for j in range(tm // Q_BLOCK):
            tile = (slice(g * HEAD_DIM, (g + 1) * HEAD_DIM), slice(j * Q_BLOCK, (j + 1) * Q_BLOCK))
            vst_o[0, g, j] = vs_t[tile].astype(BF16)
            vwt_o[0, g, j] = vw_t[tile].astype(BF16)
    gate_o[...] = jax.nn.sigmoid(p[:, nq + 6 * gw: nq + 6 * gw + LANES])


def nsa_prep(p_nsa, q_g, ks_g, kw_g, b, s, tm):
    t = p_nsa.shape[0]
    nq = NSA_HEADS * HEAD_DIM
    gw = NSA_GROUPS * HEAD_DIM
    grp, dh = NSA_GROUPS, HEAD_DIM
    tps = s // tm
    kx = 2 * dh + LANES
    assert s // SLC_BLOCK <= LANES
    tok = np.arange(s)
    feat = np.zeros((s, kx), np.float32)
    feat[:, dh] = tok // SLC_BLOCK * SLC_BLOCK
    feat[:, dh + 1] = tok % SLC_BLOCK
    feat[:, dh + 2:dh + 4] = 1.0
    feat[tok, 2 * dh + tok // SLC_BLOCK] = 1.0
    place = np.zeros((grp, gw, kx), np.float32)
    for g in range(grp):
        place[g, g * dh + np.arange(dh), np.arange(dh)] = 1.0
    tile = lambda v, n: jnp.tile(v.reshape(1, HEAD_DIM), (1, n))
    row = lambda w: pl.BlockSpec((tm, w), lambda i: (i, 0))
    full = lambda *sh: pl.BlockSpec(sh, lambda i: (0,) * len(sh))
    seq = lambda *sh: pl.BlockSpec((1, grp) + sh, lambda i: (i // tps, 0, i % tps) + (0,) * (len(sh) - 1))
    return pl.pallas_call(
        _nsa_prep_kernel,
        grid=(t // tm,),
        in_specs=[row(NSA_PROJ_PAD), full(1, nq), full(1, gw), full(1, gw), full(MXU_DIM, MXU_DIM), full(gw, gw),
                  full(grp, gw, kx), pl.BlockSpec((tm, kx), lambda i: (i % tps, 0))],
        out_specs=[pl.BlockSpec((1, NSA_HEADS, dh, tm), lambda i: (i // tps, 0, 0, i % tps)),
                   seq(tm, kx), seq(tm // Q_BLOCK, dh, Q_BLOCK), seq(tm, 2 * dh), seq(tm // Q_BLOCK, dh, Q_BLOCK),
                   row(LANES)],
        out_shape=[jax.ShapeDtypeStruct((b, NSA_HEADS, dh, s), BF16),
                   jax.ShapeDtypeStruct((b, grp, s, kx), BF16),
                   jax.ShapeDtypeStruct((b, grp, s // Q_BLOCK, dh, Q_BLOCK), BF16),
                   jax.ShapeDtypeStruct((b, grp, s, 2 * dh), BF16),
                   jax.ShapeDtypeStruct((b, grp, s // Q_BLOCK, dh, Q_BLOCK), BF16),
                   jax.ShapeDtypeStruct((t, LANES), F32)],
        compiler_params=_cparams("parallel"),
        name="nsa_prep",
    )(p_nsa, tile(q_g, NSA_HEADS), tile(ks_g, NSA_GROUPS), tile(kw_g, NSA_GROUPS),
      _block_diag_ones(MXU_DIM, HEAD_DIM, 1.0 / HEAD_DIM), _block_diag_ones(gw, HEAD_DIM, 1.0 / HEAD_DIM),
      jnp.asarray(place, BF16), jnp.asarray(feat, BF16))


def _compress_kernel(x_ref, pe_ref, w1_ref, w2_ref, g_ref, bd_ref, o_ref):
    nc = x_ref.shape[0] // CMP_STRIDE
    first = jnp.zeros((nc, x_ref.shape[1]), F32)
    second = jnp.zeros_like(first)
    for l in range(CMP_STRIDE):
        x_l = x_ref[pl.ds(l, nc, stride=CMP_STRIDE), :]
        first = first + _dot(x_l + pe_ref[0, l:l + 1], w1_ref[0, l], precision=HIGHEST)
        second = second + _dot(x_l + pe_ref[0, CMP_STRIDE + l:CMP_STRIDE + l + 1], w1_ref[0, CMP_STRIDE + l],
                               precision=HIGHEST)
    h1 = first + pltpu.roll(second, nc - 1, 0)
    out = _dot(jax.nn.silu(h1), w2_ref[0], precision=HIGHEST)
    normed = out * lax.rsqrt(_dot_x_sel(out * out, bd_ref[...]) + RMS_EPS) * g_ref[...]
    o_ref[0, 0] = jnp.where(pl.program_id(0) == 0, normed, out).astype(BF16)


def nsa_compress(p_nsa, pe_k, pe_v, kw1, kw2, vw1, vw2, kc_g, b, s):
    nc = s // CMP_STRIDE
    gw = NSA_GROUPS * HEAD_DIM
    first_col_block = NSA_HEADS * HEAD_DIM // gw
    eye = jnp.eye(NSA_GROUPS, dtype=F32)

    def both(w):
        return jnp.einsum('gh,...ij->...gihj', eye, w).reshape(w.shape[:-2] + (gw, gw))

    w1 = both(jnp.stack([kw1, vw1]).reshape(2, CMP_BLOCK, HEAD_DIM, HEAD_DIM))
    w2 = both(jnp.stack([kw2, vw2]))
    pe = jnp.stack([jnp.tile(pe_k, (1, NSA_GROUPS)), jnp.tile(pe_v, (1, NSA_GROUPS))])
    return pl.pallas_call(
        _compress_kernel,
        grid=(2, b),
        in_specs=[pl.BlockSpec((s, gw), lambda kv, bi: (bi, first_col_block + kv)),
                  pl.BlockSpec((1, CMP_BLOCK, gw), lambda kv, bi: (kv, 0, 0)),
                  pl.BlockSpec((1, CMP_BLOCK, gw, gw), lambda kv, bi: (kv, 0, 0, 0)),
                  pl.BlockSpec((1, gw, gw), lambda kv, bi: (kv, 0, 0)),
                  pl.BlockSpec((1, gw), lambda kv, bi: (0, 0)),
                  pl.BlockSpec((gw, gw), lambda kv, bi: (0, 0))],
        out_specs=pl.BlockSpec((1, 1, nc, gw), lambda kv, bi: (kv, bi, 0, 0)),
        out_shape=jax.ShapeDtypeStruct((2, b, nc, gw), BF16),
        compiler_params=_cparams("parallel", "parallel"),
        name="nsa_compress",
    )(p_nsa, pe, w1, w2, jnp.tile(kc_g.reshape(1, HEAD_DIM), (1, NSA_GROUPS)),
      _block_diag_ones(gw, HEAD_DIM, 1.0 / HEAD_DIM))


def _masked_exp_cols(s, mask):
    sm = jnp.where(mask, s, NEG_INF)
    m = jnp.max(sm, axis=0, keepdims=True)
    p = jnp.exp(sm - jnp.where(m > 0.5 * NEG_INF, m, 0.0))
    l = jnp.sum(p, axis=0, keepdims=True)
    return p, 1.0 / jnp.where(l > 0.0, l, 1.0)


def _nsa_attn_kernel(qt_ref, kc_ref, vct_ref, ovt_ref, ks_ref, vst_ref, kw_ref, vwt_ref, gate_ref, o_ref,
                     *, n_sel, n_top):
    g = pl.program_id(1)
    t0 = pl.program_id(2) * Q_BLOCK
    ks_ref, vst_ref, kw_ref, vwt_ref = (r.at[0, 0] for r in (ks_ref, vst_ref, kw_ref, vwt_ref))
    cols = NSA_HPG * Q_BLOCK
    col = lax.broadcasted_iota(jnp.int32, (1, cols), 1)
    tq = t0 + col % Q_BLOCK
    head = g * NSA_HPG + col // Q_BLOCK
    slope = lax.bitcast_convert_type((127 - (head + 1)) << 23, F32)
    qt = jnp.concatenate([qt_ref[0, h] for h in range(NSA_HPG)], axis=1)
    frow = lax.broadcasted_iota(jnp.int32, (HEAD_DIM, cols), 0)
    tq_hi = (tq // SLC_BLOCK * SLC_BLOCK).astype(F32)
    tq_lo = (tq % SLC_BLOCK).astype(F32)
    qpos = jnp.where(frow < 2, slope, jnp.where(frow == 2, -slope * tq_hi, jnp.where(frow == 3, -slope * tq_lo, 0.0)))
    qc = jnp.concatenate([qt, qpos.astype(BF16)], axis=0)

    def heads_sum(x):
        acc = x[:, :Q_BLOCK]
        for h in range(1, NSA_HPG):
            acc = acc + x[:, h * Q_BLOCK:(h + 1) * Q_BLOCK]
        return acc

    kc = kc_ref[0, 0]
    nc = kc.shape[0]
    c_last = lax.broadcasted_iota(jnp.int32, (nc, 1), 0) * CMP_STRIDE + (CMP_BLOCK - 1)
    p_c, inv_c = _masked_exp_cols(_dot(kc, qc), c_last <= tq)
    p_c = p_c * inv_c
    o_c = _dot(vct_ref[0, 0], p_c.astype(BF16))
    imp = _dot_sel_x(ovt_ref[...], heads_sum(p_c))

    bid = lax.broadcasted_iota(jnp.int32, (LANES, Q_BLOCK), 0)
    bidf = bid.astype(F32)
    tq1 = t0 + lax.broadcasted_iota(jnp.int32, (1, Q_BLOCK), 1)
    cur = tq1 // SLC_BLOCK
    forced = (bid == 0) | (bid == cur) | (bid == cur - 1)
    score = jnp.where(forced, BIG, jnp.where(bid * SLC_BLOCK <= tq1, imp, -BIG))
    score = jnp.where(bid < n_sel, score, REMOVED)
    picked = jnp.zeros((LANES, Q_BLOCK), jnp.bool_)
    for _ in range(n_top):
        m = jnp.max(score, axis=0, keepdims=True)
        pick = bidf == jnp.min(jnp.where(score == m, bidf, float(LANES)), axis=0, keepdims=True)
        picked = picked | pick
        score = jnp.where(pick, REMOVED, score)
    sel_neg = jnp.where(picked, 0.0, NEG_INF).astype(BF16)
    qx = jnp.concatenate([qc, jnp.concatenate([sel_neg] * NSA_HPG, axis=1)], axis=0)

    def values_t(ref, first_tile, n_tiles):
        return jnp.concatenate([ref[first_tile + i] for i in range(n_tiles)], axis=1)

    span = WINDOW + Q_BLOCK
    ws = pl.multiple_of(jnp.maximum(t0 - WINDOW, 0), Q_BLOCK)
    dw = tq - (ws + lax.broadcasted_iota(jnp.int32, (span, 1), 0))
    in_window = dw.astype(jnp.uint32) < WINDOW
    p_w, inv_w = _masked_exp_cols(_dot(kw_ref[pl.ds(ws, span), :], qc), in_window)
    o_w = _dot(values_t(vwt_ref, ws // Q_BLOCK, span // Q_BLOCK), p_w.astype(BF16)) * inv_w

    def scores(chunk):
        return _dot(ks_ref[pl.ds(pl.multiple_of(chunk * SEL_CHUNK, SEL_CHUNK), SEL_CHUNK), :], qx)

    def flash(s, chunk, carry):
        m, l, acc = carry
        m_new = jnp.maximum(m, jnp.max(s, axis=0, keepdims=True))
        alpha = jnp.exp(m - m_new)
        p = jnp.exp(s - m_new)
        l = alpha * l + jnp.sum(p, axis=0, keepdims=True)
        v = values_t(vst_ref, chunk * (SEL_CHUNK // Q_BLOCK), SEL_CHUNK // Q_BLOCK)
        return m_new, l, alpha * acc + _dot(v, p.astype(BF16))

    def flash_pair(chunk_a, chunk_b, carry, keep_b=None):
        s_a, s_b = scores(chunk_a), scores(chunk_b)
        if keep_b is not None:
            s_b = jnp.where(keep_b, s_b, NEG_INF)
        return flash(s_b, chunk_b, flash(s_a, chunk_a, carry))

    bpc = SEL_CHUNK // SLC_BLOCK
    blk_any = jnp.max(jnp.where(picked, 1.0, 0.0), axis=1, keepdims=True)
    chunk_bit = lax.bitcast_convert_type((bid[:, :1] // bpc + 127) << 23, F32)
    bits = jnp.max((blk_any * chunk_bit).reshape(LANES // bpc, bpc, 1), axis=1)
    active = jnp.sum(bits, axis=0, keepdims=True)[0, 0].astype(jnp.int32)

    def full_step(j, state):
        def on_active(st):
            return lax.cond(st[0] >= 0,
                            lambda s2: (jnp.int32(-1), flash_pair(s2[0], j, s2[1])),
                            lambda s2: (j, s2[1]), st)
        return lax.cond((active >> j) & 1 == 1, on_active, lambda st: st, state)

    n_full = t0 // SEL_CHUNK
    init = (jnp.full((1, cols), NEG_INF, F32), jnp.zeros((1, cols), F32), jnp.zeros((HEAD_DIM, cols), F32))
    waiting, carry = lax.fori_loop(0, n_full, full_step, (jnp.int32(-1), init))
    keep = n_full * SEL_CHUNK + lax.broadcasted_iota(jnp.int32, (SEL_CHUNK, 1), 0) <= tq
    _, l_s, acc_s = lax.cond(
        waiting >= 0,
        lambda c: flash_pair(waiting, n_full, c, keep_b=keep),
        lambda c: flash(jnp.where(keep, scores(n_full), NEG_INF), n_full, c), carry)
    o_s = acc_s / l_s

    for h in range(NSA_HPG):
        hs = slice(h * Q_BLOCK, (h + 1) * Q_BLOCK)
        gate = lambda br: gate_ref[0, 0, br, h:h + 1, :]
        o_ref[0, h] = gate(0) * o_c[:, hs] + gate(1) * o_s[:, hs] + gate(2) * o_w[:, hs]


def nsa_attention(qt, kcv, ks_x, vs_t, kw_x, vw_t, gates, b, s):
    nq_blocks = s // Q_BLOCK
    nc = s // CMP_STRIDE
    n_cmp = nc - CMP_BLOCK // CMP_STRIDE + 1
    n_sel = s // SLC_BLOCK
    n_top = min(SLC_TOPK, n_sel)
    grp, dh = NSA_GROUPS, HEAD_DIM
    assert n_sel <= LANES and s % SEL_CHUNK == 0 and s >= WINDOW + Q_BLOCK
    c_start = np.arange(n_cmp) * CMP_STRIDE
    s_start = np.arange(n_sel) * SLC_BLOCK
    ovt = np.zeros((LANES, nc), np.float32)
    ovt[:n_sel, :n_cmp] = (np.clip(np.minimum((c_start + CMP_BLOCK)[:, None], s_start[None] + SLC_BLOCK)
                                   - np.maximum(c_start[:, None], s_start[None]), 0, None) / CMP_BLOCK).T

    c_pos = np.arange(nc) * CMP_STRIDE + (CMP_BLOCK - 1)
    c_feat = np.zeros((nc, dh), np.float32)
    c_feat[:, 0] = c_pos // SLC_BLOCK * SLC_BLOCK
    c_feat[:, 1] = c_pos % SLC_BLOCK
    c_feat[:, 2:4] = 1.0
    kc, vc = (a.reshape(b, nc, grp, dh) for a in kcv)
    kc = kc.transpose(0, 2, 1, 3)
    kc_x = jnp.concatenate([kc, jnp.broadcast_to(jnp.asarray(c_feat, BF16), kc.shape)], axis=-1)
    vct = vc.transpose(0, 2, 3, 1)
    gates_t = gates[:, :NSA_HEADS * 3].reshape(b, s, grp, NSA_HPG, 3).transpose(0, 2, 4, 3, 1)

    grp_spec = lambda *shape: pl.BlockSpec((1, 1) + shape, lambda bi, g, i: (bi, g) + (0,) * len(shape),
                                           pipeline_mode=pl.Buffered(1))
    return pl.pallas_call(
        functools.partial(_nsa_attn_kernel, n_sel=n_sel, n_top=n_top),
        grid=(b, grp, nq_blocks),
        in_specs=[pl.BlockSpec((1, NSA_HPG, dh, Q_BLOCK), lambda bi, g, i: (bi, g, 0, i)),
                  grp_spec(nc, 2 * dh), grp_spec(dh, nc),
                  pl.BlockSpec((LANES, nc), lambda bi, g, i: (0, 0)),
                  grp_spec(s, 2 * dh + LANES), grp_spec(s // Q_BLOCK, dh, Q_BLOCK),
                  grp_spec(s, 2 * dh), grp_spec(s // Q_BLOCK, dh, Q_BLOCK),
                  pl.BlockSpec((1, 1, 3, NSA_HPG, Q_BLOCK), lambda bi, g, i: (bi, g, 0, 0, i))],
        out_specs=pl.BlockSpec((1, NSA_HPG, dh, Q_BLOCK), lambda bi, g, i: (bi, g, 0, i)),
        out_shape=jax.ShapeDtypeStruct((b, NSA_HEADS, dh, s), F32),
        compiler_params=_cparams("parallel", "parallel", "arbitrary"),
        name="nsa_attention",
    )(qt, kc_x, vct, jnp.asarray(ovt, BF16), ks_x, vs_t, kw_x, vw_t, gates_t)


def _rwkv_prep_kernel(p_ref, prev_ref, mu_ref, w0_ref, a0_ref, kk_ref, ka_ref, rk_ref,
                      wup_ref, aup_ref, gup_ref, bd_ref, ltri_ref, lones_ref, csum_ref,
                      at_o, bt_o, kt_o, rt_o, v_o, bw_o, kw_o, wc_o, g_o, bonus_o, *, tiles_per_seq):
    p = p_ref[...]
    tm = p.shape[0]
    first = pl.program_id(0) % tiles_per_seq == 0
    last_prev = jnp.where(first, 0.0, prev_ref[7:8, :])
    prev = pltpu.roll(p, 1, 0)
    prev = jnp.where(lax.broadcasted_iota(jnp.int32, (tm, 1), 0) == 0, last_prev, prev)
    pm = p + (prev - p) * mu_ref[...]
    d = RWKV_DIM
    r, k, v = pm[:, :d], pm[:, d:2 * d], pm[:, 2 * d:3 * d]
    lora = pm[:, 3 * d:3 * d + LANES]
    gd = pm[:, 3 * d + LANES:3 * d + 2 * LANES]
    z = -(w0_ref[...] + _dot(jnp.tanh(lora).astype(BF16), wup_ref[...]))
    softplus = jnp.maximum(z, 0.0) + jnp.log(1.0 + jnp.exp(-jnp.abs(z)))
    w = -softplus - 0.5
    a = jax.nn.sigmoid(a0_ref[...] + _dot(lora.astype(BF16), aup_ref[...]))
    g_o[...] = _dot(jax.nn.sigmoid(gd).astype(BF16), gup_ref[...])
    bd = bd_ref[...]
    kkr = k * kk_ref[...]
    kk = kkr / jnp.maximum(jnp.sqrt(_dot_x_sel(kkr * kkr, bd)), 1e-12)
    k2 = k * (1.0 + (a - 1.0) * ka_ref[...])
    bonus_o[...] = _dot_x_sel(r * k2 * rk_ref[...], bd) * v
    lw = -jnp.exp(w)
    lw_parts = _split_bf16(lw)
    cum = _dot_sel_x(ltri_ref[...], lw)
    tot = _dot_sel_x(lones_ref[...], lw)
    e_in = jnp.exp(cum)
    e_out = jnp.exp(-cum)
    e_end = jnp.exp(tot - cum)

    def put_heads(o, val):
        for h in range(RWKV_HEADS):
            o[0, h] = val[:, h * HEAD_DIM:(h + 1) * HEAD_DIM].astype(o.dtype)

    put_heads(at_o, -kk * jnp.exp(cum - lw))
    put_heads(bt_o, kk * a * e_out)
    put_heads(kt_o, k2 * e_out)
    put_heads(rt_o, r * e_in)
    put_heads(v_o, v)
    put_heads(bw_o, kk * a * e_end)
    put_heads(kw_o, k2 * e_end)
    put_heads(wc_o, jnp.exp(sum(_dot(csum_ref[...], p) for p in lw_parts)))


def rwkv_prep(p_rwkv, mu, w0, w_up, a0, a_up, g_up, k_k, k_a, r_k, b, s, tm):
    t = p_rwkv.shape[0]
    d = RWKV_DIM
    c = RWKV_CHUNK
    tps = s // tm
    cpt = tm // c
    wup = jnp.concatenate([w_up, jnp.zeros_like(a_up)], axis=0).astype(BF16)
    aup = jnp.concatenate([jnp.zeros_like(w_up), a_up], axis=0).astype(BF16)
    i = np.arange(tm)
    j = np.arange(MXU_DIM)
    same = (j[:, None] // c) == (j[None, :] // c)
    ltri = jnp.asarray(same & (j[:, None] >= j[None, :]), BF16)
    lones = jnp.asarray(same, BF16)
    csum = jnp.asarray(np.arange(cpt)[:, None] == (i[None, :] // c), BF16)
    row = lambda w: pl.BlockSpec((tm, w), lambda i: (i, 0))
    full = lambda *sh: pl.BlockSpec(sh, lambda i: (0,) * len(sh))
    heads = lambda n: pl.BlockSpec((1, RWKV_HEADS, n, HEAD_DIM), lambda i: (i // tps, 0, i % tps, 0))
    hshape = lambda n, dt: jax.ShapeDtypeStruct((b, RWKV_HEADS, n, HEAD_DIM), dt)
    vec = lambda x: x.reshape(1, -1)
    return pl.pallas_call(
        functools.partial(_rwkv_prep_kernel, tiles_per_seq=tps),
        grid=(t // tm,),
        in_specs=[row(RWKV_PROJ),
                  pl.BlockSpec((8, RWKV_PROJ), lambda i: (jnp.maximum(i * (tm // 8) - 1, 0), 0)),
                  full(1, RWKV_PROJ), full(1, d), full(1, d), full(1, d), full(1, d), full(1, d),
                  full(LANES, d), full(LANES, d), full(LANES, d), full(MXU_DIM, MXU_DIM),
                  full(MXU_DIM, MXU_DIM), full(MXU_DIM, MXU_DIM), full(cpt, tm)],
        out_specs=[heads(tm)] * 7 + [heads(cpt), row(d), row(d)],
        out_shape=[hshape(s, BF16)] * 7 + [hshape(s // c, F32)] + [jax.ShapeDtypeStruct((t, d), F32)] * 2,
        compiler_params=_cparams("parallel"),
        name="rwkv_prep",
    )(p_rwkv, p_rwkv, vec(mu), vec(w0), vec(a0), vec(k_k), vec(k_a), vec(r_k), wup, aup, g_up.astype(BF16),
      _block_diag_ones(MXU_DIM, HEAD_DIM), ltri, lones, csum)


def _bdot(a, b):
    return lax.dot_general(a, b, (((2,), (1,)), ((0,), (0,))), preferred_element_type=F32)


def _bdot_nt(a, b):
    return lax.dot_general(a, b, (((2,), (2,)), ((0,), (0,))), preferred_element_type=F32)


def _bdot_tn(a, b):
    return lax.dot_general(a, b, (((1,), (1,)), ((0,), (0,))), preferred_element_type=F32)


def _rwkv_intra_kernel(at_ref, bt_ref, kt_ref, rt_ref, v_ref, ta_o, tr_o, arb_o, yv_o):
    c = RWKV_CHUNK
    _, nh, ts, dh = at_ref.shape
    n = nh * (ts // c)
    chunked = lambda ref: ref[0].reshape(n, c, dh)
    at, bt, kt, rt, v = (chunked(r) for r in (at_ref, bt_ref, kt_ref, rt_ref, v_ref))
    ri = lax.broadcasted_iota(jnp.int32, (1, c, c), 1)
    ci = lax.broadcasted_iota(jnp.int32, (1, c, c), 2)
    strict = ri > ci
    incl = ri >= ci
    ar = jnp.concatenate([at, rt], axis=1)
    xb = _bdot_nt(ar, bt)
    xk = _bdot_nt(ar, kt)
    l_ab = jnp.where(strict, xb[:, :c], 0.0)
    a_ak = jnp.where(strict, xk[:, :c], 0.0)
    a_rb = jnp.where(incl, xb[:, c:], 0.0)
    a_rk = jnp.where(incl, xk[:, c:], 0.0)
    pw = l_ab
    tinv = jnp.where(ri == ci, 1.0, 0.0) + l_ab
    for _ in range(int(np.log2(c)) - 1):
        pw_b = pw.astype(BF16)
        pw = _bdot(pw_b, pw_b)
        tinv = tinv + _bdot(tinv.astype(BF16), pw.astype(BF16))
    tinv_b = tinv.astype(BF16)

    def put(o, val):
        o[0] = val.reshape(nh, ts, val.shape[-1]).astype(o.dtype)

    put(ta_o, _bdot(tinv_b, at))
    put(tr_o, _bdot(tinv_b, _bdot(a_ak.astype(BF16), v).astype(BF16)))
    put(arb_o, a_rb)
    put(yv_o, _bdot(a_rk.astype(BF16), v))


def rwkv_intra(at, bt, kt, rt, v, ts):
    b, h, s, dh = at.shape
    seq = lambda: pl.BlockSpec((1, h, ts, dh), lambda bi, i: (bi, 0, i, 0))
    shp = lambda dt: jax.ShapeDtypeStruct((b, h, s, dh), dt)
    return pl.pallas_call(
        _rwkv_intra_kernel,
        grid=(b, s // ts),
        in_specs=[seq()] * 5,
        out_specs=[seq()] * 4,
        out_shape=[shp(BF16), shp(F32), shp(BF16), shp(F32)],
        compiler_params=_cparams("parallel", "parallel"),
        name="rwkv_intra",
    )(at, bt, kt, rt, v)


def _rwkv_scan_kernel(ta_ref, tr_ref, arb_ref, yv_ref, rt_ref, v_ref, bw_ref, kw_ref, wc_ref, y_ref, st_ref):
    c = RWKV_CHUNK
    nb, nh, ts, dh = ta_ref.shape
    n = nb * nh

    @pl.when(pl.program_id(0) == 0)
    def _():
        st_ref[...] = jnp.zeros_like(st_ref)

    def chunk_step(j, _):
        sl = (slice(None), slice(None), pl.ds(pl.multiple_of(j * c, c), c), slice(None))
        get = lambda ref: ref[sl].reshape(n, c, dh)
        st = st_ref[...]
        st_b = st.astype(BF16)
        u = _bdot_nt(get(ta_ref), st_b) + get(tr_ref)
        u_b = u.astype(BF16)
        y = _bdot_nt(get(rt_ref), st_b) + _bdot(get(arb_ref), u_b) + get(yv_ref)
        wc = wc_ref[:, :, pl.ds(pl.program_id(0) * (ts // c) + j, 1), :].reshape(n, 1, dh)
        st_ref[...] = st * wc + _bdot_tn(jnp.concatenate([u_b, get(v_ref)], axis=1),
                                         jnp.concatenate([get(bw_ref), get(kw_ref)], axis=1))
        y_ref[sl] = y.reshape(nb, nh, c, dh)
        return 0

    lax.fori_loop(0, ts // c, chunk_step, 0)


def rwkv_scan(ta, tr, arb, yv, rt, v, bw, kw, wc, ts):
    b, h, s, dh = ta.shape
    seq = lambda n: pl.BlockSpec((b, h, n, dh), lambda i: (0, 0, i, 0))
    return pl.pallas_call(
        _rwkv_scan_kernel,
        grid=(s // ts,),
        in_specs=[seq(ts)] * 8 + [pl.BlockSpec(wc.shape, lambda i: (0, 0, 0, 0))],
        out_specs=seq(ts),
        out_shape=jax.ShapeDtypeStruct((b, h, s, dh), F32),
        scratch_shapes=[pltpu.VMEM((b * h, dh, dh), F32)],
        compiler_params=_cparams("arbitrary"),
        name="rwkv_scan",
    )(ta, tr, arb, yv, rt, v, bw, kw, wc)


def _out_proj_kernel(x_ref, on_ref, y_ref, bonus_ref, g_ref, lnw_ref, lnb_ref, bd_ref, wn_ref, wr_ref, o_ref):
    y = jnp.concatenate([y_ref[0, h] for h in range(RWKV_HEADS)], axis=-1)
    bd = bd_ref[...]
    yc = y - _dot_x_sel(y, bd)
    yn = yc * lax.rsqrt(_dot_x_sel(yc * yc, bd) + GN_EPS)
    o_rwkv = (yn * lnw_ref[...] + lnb_ref[...] + bonus_ref[...]) * g_ref[...]
    tm = y.shape[0]
    o_nsa_t = on_ref[0].reshape(NSA_HEADS * HEAD_DIM, tm)
    o_ref[...] = (x_ref[...] + _dot_tn(o_nsa_t.astype(BF16), wn_ref[...])
                  + _dot(o_rwkv.astype(BF16), wr_ref[...]))


def out_proj(x, o_nsa_t, y, bonus, g, ln_w, ln_b, w_out, s, tm):
    t, d = x.shape
    dn = o_nsa_t.shape[1] * o_nsa_t.shape[2]
    dr = bonus.shape[1]
    tps = s // tm
    row = lambda w: pl.BlockSpec((tm, w), lambda i: (i, 0))
    full = lambda *sh: pl.BlockSpec(sh, lambda i: (0,) * len(sh))
    return pl.pallas_call(
        _out_proj_kernel,
        grid=(t // tm,),
        in_specs=[row(d), pl.BlockSpec((1, NSA_HEADS, HEAD_DIM, tm), lambda i: (i // tps, 0, 0, i % tps)),
                  pl.BlockSpec((1, RWKV_HEADS, tm, HEAD_DIM), lambda i: (i // tps, 0, i % tps, 0)),
                  row(dr), row(dr), full(1, dr), full(1, dr), full(MXU_DIM, MXU_DIM), full(dn, d), full(dr, d)],
        out_specs=row(d),
        out_shape=jax.ShapeDtypeStruct((t, d), F32),
        compiler_params=_cparams("parallel"),
        name="out_proj",
    )(x, o_nsa_t, y, bonus, g, ln_w.reshape(1, dr), ln_b.reshape(1, dr),
      _block_diag_ones(MXU_DIM, HEAD_DIM, 1.0 / HEAD_DIM), w_out[:dn].astype(BF16), w_out[dn:].astype(BF16))


def _cross_attn_kernel(h_ref, g_ref, wq_ref, qg_ref, kv_ref, kg_ref, wo_ref, o_ref):
    h = h_ref[...]
    d = h.shape[1]
    xd = d // X_HEADS
    q = _dot(_rms(h, g_ref[...]).astype(BF16), wq_ref[...])
    kv = kv_ref[0]
    outs = []
    for hd in range(X_HEADS):
        qh = _rms(q[:, hd * xd:(hd + 1) * xd], qg_ref[...]) * (xd ** -0.5)
        kh = _rms(kv[:, hd * xd:(hd + 1) * xd], kg_ref[...])
        vh = kv[:, d + hd * xd:d + (hd + 1) * xd]
        s = _dot_nt(qh.astype(BF16), kh.astype(BF16))
        p = jnp.exp(s - jnp.max(s, axis=-1, keepdims=True))
        p = p / jnp.sum(p, axis=-1, keepdims=True)
        outs.append(_dot(p.astype(BF16), vh.astype(BF16)))
    o = jnp.concatenate(outs, axis=-1)
    o_ref[...] = h + _dot(o.astype(BF16), wo_ref[...])


def cross_attention(h, kv, norm_g, xq_w, xq_g, xk_g, xo_w, b, s, tm):
    t, d = h.shape
    m = kv.shape[1]
    xd = d // X_HEADS
    tiles = s // tm
    full = lambda *sh: pl.BlockSpec(sh, lambda i: (0,) * len(sh))
    return pl.pallas_call(
        _cross_attn_kernel,
        grid=(t // tm,),
        in_specs=[pl.BlockSpec((tm, d), lambda i: (i, 0)), full(1, d), full(d, d), full(1, xd),
                  pl.BlockSpec((1, m, 2 * d), lambda i: (i // tiles, 0, 0)), full(1, xd), full(d, d)],
        out_specs=pl.BlockSpec((tm, d), lambda i: (i, 0)),
        out_shape=jax.ShapeDtypeStruct((t, d), F32),
        compiler_params=_cparams("parallel"),
        name="cross_attention",
    )(h, norm_g.reshape(1, d), xq_w.astype(BF16), xq_g.reshape(1, xd), kv, xk_g.reshape(1, xd),
      xo_w.astype(BF16))


def _router_kernel(h_ref, g_ref, rw_ref, rb_ref, ltri_ref, xn_o, idx_o, gate_o, rank_o, count_o, seen_ref):
    @pl.when(pl.program_id(0) == 0)
    def _():
        seen_ref[...] = jnp.zeros_like(seen_ref)

    xn = _rms(h_ref[...], g_ref[...])
    xn_o[...] = _pack_bf16_pairs(xn)
    logits = _dot(xn, rw_ref[...], precision=HIGHEST) + rb_ref[...]
    tm = logits.shape[0]
    lane = lax.broadcasted_iota(jnp.int32, (tm, LANES), 1)
    lanef = lane.astype(F32)
    logits = jnp.where(lane < N_EXPERTS, logits, REMOVED)
    idx_acc = jnp.zeros((tm, LANES), F32)
    val_acc = jnp.zeros((tm, LANES), F32)
    chosen = jnp.zeros((tm, LANES), F32)
    picks = []
    top = None
    for k in range(TOP_K):
        m = jnp.max(logits, axis=-1, keepdims=True)
        idx = jnp.min(jnp.where(logits == m, lanef, float(LANES)), axis=-1, keepdims=True)
        pick = lanef == idx
        picks.append(pick)
        chosen = jnp.where(pick, 1.0, chosen)
        logits = jnp.where(pick, REMOVED, logits)
        top = m if top is None else top
        idx_acc = jnp.where(lane == k, idx, idx_acc)
        val_acc = jnp.where(lane == k, jnp.exp(m - top), val_acc)
    idx_o[...] = idx_acc.astype(jnp.int32)
    gate_o[...] = val_acc / jnp.sum(val_acc, axis=-1, keepdims=True)
    before = seen_ref[0:1, :] + _dot(ltri_ref[...], chosen.astype(BF16))
    rank_acc = jnp.zeros((tm, LANES), F32)
    for k, pick in enumerate(picks):
        rank_acc = jnp.where(lane == k, jnp.sum(jnp.where(pick, before, 0.0), axis=-1, keepdims=True), rank_acc)
    rank_o[...] = rank_acc.astype(jnp.int32)
    seen_ref[...] = seen_ref[...] + jnp.sum(chosen, axis=0, keepdims=True)
    count_o[...] = seen_ref[...]


def moe_router(h, norm_g, router_w, router_b, tm):
    t, d = h.shape
    rw = jnp.zeros((d, LANES), F32).at[:, :N_EXPERTS].set(router_w)
    rb = jnp.zeros((1, LANES), F32).at[0, :N_EXPERTS].set(router_b)
    i = np.arange(tm)
    ltri = jnp.asarray(i[:, None] > i[None, :], BF16)
    row = lambda w: pl.BlockSpec((tm, w), lambda i: (i, 0))
    full = lambda *s: pl.BlockSpec(s, lambda i: (0,) * len(s))
    return pl.pallas_call(
        _router_kernel,
        grid=(t // tm,),
        in_specs=[row(d), full(1, d), full(d, LANES), full(1, LANES), full(tm, tm)],
        out_specs=[row(d // 2), row(LANES), row(LANES), row(LANES), full(8, LANES)],
        out_shape=[jax.ShapeDtypeStruct((t, d // 2), F32), jax.ShapeDtypeStruct((t, LANES), jnp.int32),
                   jax.ShapeDtypeStruct((t, LANES), F32), jax.ShapeDtypeStruct((t, LANES), jnp.int32),
                   jax.ShapeDtypeStruct((8, LANES), F32)],
        scratch_shapes=[pltpu.VMEM((8, LANES), F32)],
        compiler_params=_cparams("arbitrary"),
        name="moe_router",
    )(h, norm_g.reshape(1, d), rw, rb, ltri)


def _expert_kernel(blk_e_ref, n_used_ref, x_ref, w1_ref, b1_ref, w2_ref, b2_ref, o_ref, w2x_ref):
    i = pl.program_id(0)
    f = w2_ref.shape[1]
    half_lanes = LANES // 2
    n_merged = f // LANES

    @pl.when((i == 0) | (blk_e_ref[i] != blk_e_ref[jnp.maximum(i - 1, 0)]))
    def _():
        for s in range(w2x_ref.shape[0]):
            lanes = slice(s * LANES, (s + 1) * LANES)
            for c in range(n_merged):
                lo = w2_ref[0, c * half_lanes:(c + 1) * half_lanes, lanes]
                hi = w2_ref[0, (c + n_merged) * half_lanes:(c + n_merged + 1) * half_lanes, lanes]
                w2x_ref[s, pl.ds(c * LANES, half_lanes, stride=2), :] = lo
                w2x_ref[s, pl.ds(c * LANES + 1, half_lanes, stride=2), :] = hi

    @pl.when(i < n_used_ref[0])
    def _():
        x = _unpack_bf16_pairs(x_ref[...]).astype(BF16)
        h = _dot(x, w1_ref[0].astype(BF16)) + b1_ref[0]
        hg = jnp.minimum(h, SWIGLU_LIMIT)
        gate = hg * jax.nn.sigmoid(SWIGLU_ALPHA * hg)
        lin = jnp.clip(h, -SWIGLU_LIMIT, SWIGLU_LIMIT) + 1.0
        even = lax.broadcasted_iota(jnp.int32, (1, LANES), 1) % 2 == 0
        def swiglu(c):
            cols = slice(c * LANES, (c + 1) * LANES)
            nxt = pltpu.roll(lin[:, cols], LANES - 1, 1)
            return jnp.where(even, gate[:, cols] * nxt, 0.0)

        act = jnp.concatenate([(swiglu(c) + pltpu.roll(swiglu(c + n_merged), 1, 1)).astype(BF16)
                               for c in range(n_merged)], axis=1)
        w2x = jnp.concatenate([w2x_ref[s] for s in range(w2x_ref.shape[0])], axis=1)
        o_ref[...] = _pack_bf16_pairs(_dot(act, w2x.astype(BF16)) + b2_ref[0])

    @pl.when(i >= n_used_ref[0])
    def _():
        o_ref[...] = jnp.zeros_like(o_ref)


def moe_experts(xs, blk_e, n_used, w1, b1, w2, b2):
    r = xs.shape[0]
    d, f2 = w1.shape[1:]
    m = MOE_ROW_BLOCK
    ex = lambda *s: pl.BlockSpec((1,) + s, lambda i, be, nu: (be[i],) + (0,) * len(s))
    grid_spec = pltpu.PrefetchScalarGridSpec(
        num_scalar_prefetch=2,
        grid=(r // m,),
        in_specs=[pl.BlockSpec((m, d // 2), lambda i, be, nu: (i, 0)),
                  ex(d, f2), ex(1, f2), ex(f2 // 2, d), ex(1, d)],
        out_specs=pl.BlockSpec((m, d // 2), lambda i, be, nu: (i, 0)),
        scratch_shapes=[pltpu.VMEM((d // LANES, f2 // 2, LANES), F32)],
    )
    return pl.pallas_call(
        _expert_kernel,
        grid_spec=grid_spec,
        out_shape=jax.ShapeDtypeStruct((r, d // 2), F32),
        compiler_params=_cparams("arbitrary"),
        name="moe_experts",
    )(blk_e, n_used, xs, w1, b1, w2, b2)


def _combine_kernel(h_ref, gate_ref, *refs):
    *y_refs, o_ref = refs
    acc = h_ref[...]
    for k, y_ref in enumerate(y_refs):
        acc = acc + gate_ref[:, k:k + 1] * _unpack_bf16_pairs(y_ref[...])
    o_ref[...] = acc


def moe_combine(h, gate, ys_k, tm):
    t, d = h.shape
    row = lambda w: pl.BlockSpec((tm, w), lambda i: (i, 0))
    return pl.pallas_call(
        _combine_kernel,
        grid=(t // tm,),
        in_specs=[row(d), row(LANES)] + [row(d // 2)] * len(ys_k),
        out_specs=row(d),
        out_shape=jax.ShapeDtypeStruct((t, d), F32),
        compiler_params=_cparams("parallel"),
        name="moe_combine",
    )(h, gate, *ys_k)


def _layer(x, mem, norm_mix_g, w_in, q_norm_g, k_cmp_norm_g, k_slc_norm_g, k_win_norm_g,
           cmp_pe_k, cmp_pe_v, cmp_k_w1, cmp_k_w2, cmp_v_w1, cmp_v_w2,
           rwkv_mu, rwkv_w0, rwkv_w_up, rwkv_a0, rwkv_a_up, rwkv_g_up, rwkv_k_k, rwkv_k_a,
           rwkv_r_k, rwkv_ln_w, rwkv_ln_b, w_out,
           norm_x_g, norm_mem_g, xq_w, xk_w, xv_w, xq_norm_g, xk_norm_g, xo_w,
           norm_ffn_g, router_w, router_b, mlp1_w, mlp1_b, mlp2_w, mlp2_b):
    b, s, d = x.shape
    t = b * s
    tm = 512
    xt = x.reshape(t, d)

    w_nsa = jnp.pad(w_in[:, :NSA_PROJ], ((0, 0), (0, NSA_PROJ_PAD - NSA_PROJ)))
    p_nsa, p_rwkv = norm_matmul(xt, norm_mix_g, [w_nsa, w_in[:, NSA_PROJ:]], tm)

    qn, ks, vs, kw, vw, gates = nsa_prep(p_nsa, q_norm_g, k_slc_norm_g, k_win_norm_g, b, s, tm)
    kcv = nsa_compress(p_nsa, cmp_pe_k, cmp_pe_v, cmp_k_w1, cmp_k_w2, cmp_v_w1, cmp_v_w2, k_cmp_norm_g, b, s)
    o_nsa = nsa_attention(qn, kcv, ks, vs, kw, vw, gates, b, s)

    at, bt, kt, rt, v, bw, kwd, wc, g_gate, bonus = rwkv_prep(
        p_rwkv, rwkv_mu, rwkv_w0, rwkv_w_up, rwkv_a0, rwkv_a_up, rwkv_g_up, rwkv_k_k, rwkv_k_a, rwkv_r_k, b, s, tm)
    ta, tr, arb, yv = rwkv_intra(at, bt, kt, rt, v, ts=256)
    y = rwkv_scan(ta, tr, arb, yv, rt, v, bw, kwd, wc, ts=256)

    h1 = out_proj(xt, o_nsa, y, bonus, g_gate, rwkv_ln_w, rwkv_ln_b, w_out, s, tm)
    m = mem.shape[1]
    kv, = norm_matmul(mem.reshape(b * m, d), norm_mem_g, [jnp.concatenate([xk_w, xv_w], axis=1)], m)
    h2 = cross_attention(h1, kv.reshape(b, m, 2 * d), norm_x_g, xq_w, xq_norm_g, xk_norm_g, xo_w, b, s, tm)

    xn, top_i, gate, rank, seen = moe_router(h2, norm_ffn_g, router_w, router_b, tm)
    top_i = top_i[:, :TOP_K]
    a = t * TOP_K
    mb = MOE_ROW_BLOCK
    idx_bits = max(a - 1, 1).bit_length()
    assert N_EXPERTS << idx_bits < 2 ** 31
    packed = (top_i.reshape(a) << idx_bits) | jnp.arange(a, dtype=jnp.int32)
    order = jnp.sort(packed) & ((1 << idx_bits) - 1)
    counts = seen[0, :N_EXPERTS].astype(jnp.int32)
    starts = jnp.cumsum(counts) - counts
    padded = (counts + mb - 1) // mb * mb
    pends = jnp.cumsum(padded)
    pstarts = pends - padded
    pos = pstarts[top_i] + rank[:, :TOP_K]
    n_blocks = -(-a // mb) + N_EXPERTS
    r = n_blocks * mb
    blk_start = jnp.arange(n_blocks, dtype=jnp.int32) * mb
    blk_e = jnp.minimum(jnp.sum(pends[None, :] <= blk_start[:, None], axis=1), N_EXPERTS - 1).astype(jnp.int32)
    src_i = blk_start[:, None] + jnp.arange(mb, dtype=jnp.int32)[None, :] - (pstarts - starts)[blk_e][:, None]
    valid = src_i < (starts + counts)[blk_e][:, None]
    row_src = jnp.where(valid, order[jnp.minimum(src_i, a - 1)] // TOP_K, 0).astype(jnp.int32).reshape(r)
    n_used = (pends[-1] // mb).astype(jnp.int32).reshape(1)
    xs = xn.at[row_src].get(mode="promise_in_bounds")
    f2 = mlp1_w.shape[2]
    ys = moe_experts(xs, blk_e, n_used, mlp1_w, mlp1_b.reshape(N_EXPERTS, 1, f2), mlp2_w,
                     mlp2_b.reshape(N_EXPERTS, 1, d))
    ys_k = [ys.at[pos[:, k]].get(mode="promise_in_bounds") for k in range(TOP_K)]
    out = moe_combine(h2, gate, ys_k, 256)
    return out.reshape(b, s, d)


def kernel(x, mem, norm_mix_g, w_in, q_norm_g, k_cmp_norm_g, k_slc_norm_g, k_win_norm_g, cmp_pe_k, cmp_pe_v, cmp_k_w1, cmp_k_w2, cmp_v_w1, cmp_v_w2, rwkv_mu, rwkv_w0, rwkv_w_up, rwkv_a0, rwkv_a_up, rwkv_g_up, rwkv_k_k, rwkv_k_a, rwkv_r_k, rwkv_ln_w, rwkv_ln_b, w_out, norm_x_g, norm_mem_g, xq_w, xk_w, xv_w, xq_norm_g, xk_norm_g, xo_w, norm_ffn_g, router_w, router_b, mlp1_w, mlp1_b, mlp2_w, mlp2_b):
    params = (norm_mix_g, w_in, q_norm_g, k_cmp_norm_g, k_slc_norm_g, k_win_norm_g, cmp_pe_k, cmp_pe_v,
              cmp_k_w1, cmp_k_w2, cmp_v_w1, cmp_v_w2, rwkv_mu, rwkv_w0, rwkv_w_up, rwkv_a0, rwkv_a_up,
              rwkv_g_up, rwkv_k_k, rwkv_k_a, rwkv_r_k, rwkv_ln_w, rwkv_ln_b, w_out, norm_x_g, norm_mem_g,
              xq_w, xk_w, xv_w, xq_norm_g, xk_norm_g, xo_w, norm_ffn_g, router_w, router_b,
              mlp1_w, mlp1_b, mlp2_w, mlp2_b)
    h = x
    for layer in range(norm_mix_g.shape[0]):
        h = _layer(h, mem, *[prm[layer] for prm in params])
    return h
```

```python
import functools

import numpy as np
import jax
import jax.numpy as jnp
from jax import lax
from jax.experimental import pallas as pl
from jax.experimental.pallas import tpu as pltpu

F32 = jnp.float32
BF16 = jnp.bfloat16
HIGHEST = lax.Precision.HIGHEST

V7X_VMEM_BYTES = 64 * 1024 * 1024
VMEM_LIMIT = V7X_VMEM_BYTES * 3 // 4

HEAD_DIM = 64
NSA_HEADS = 8
NSA_GROUPS = 2
NSA_HPG = NSA_HEADS // NSA_GROUPS
GROUP_W = NSA_HPG * HEAD_DIM
CMP_BLOCK = 32
CMP_STRIDE = 16
SLC_BLOCK = 64
SLC_TOPK = 16
WINDOW = 512
Q_BLOCK = 128
SEL_CHUNK = 512
RWKV_HEADS = 8
RWKV_DIM = RWKV_HEADS * HEAD_DIM
RWKV_CHUNK = 64
GN_EPS = HEAD_DIM * 1e-5
X_HEADS = 4
N_EXPERTS = 32
TOP_K = 4
SWIGLU_LIMIT = 7.0
SWIGLU_ALPHA = 1.702
MOE_ROW_BLOCK = 256
RMS_EPS = 1e-6
NEG_INF = -1e30
BIG = 1e9
REMOVED = -3e38
LANES = 128
MXU_DIM = 256

NSA_PROJ = NSA_HEADS * HEAD_DIM + 6 * NSA_GROUPS * HEAD_DIM + NSA_HEADS * 3
NSA_PROJ_PAD = -(-NSA_PROJ // LANES) * LANES
RWKV_PROJ = 3 * RWKV_DIM + 64 + 64 + 128


def _cparams(*sem):
    return pltpu.CompilerParams(dimension_semantics=sem, vmem_limit_bytes=VMEM_LIMIT)


def _dot(a, b, **kw):
    return jnp.dot(a, b, preferred_element_type=F32, **kw)


def _dot_nt(a, b, **kw):
    return lax.dot_general(a, b, (((1,), (1,)), ((), ())), preferred_element_type=F32, **kw)


def _dot_tn(a, b, **kw):
    return lax.dot_general(a, b, (((0,), (0,)), ((), ())), preferred_element_type=F32, **kw)


def _rms(x, g):
    return x * lax.rsqrt(jnp.mean(x * x, axis=-1, keepdims=True) + RMS_EPS) * g


def _split_bf16(x, terms=3):
    parts = []
    for _ in range(terms):
        hi = x.astype(BF16)
        parts.append(hi)
        x = x - hi.astype(F32)
    return parts


def _dot_x_sel(x, sel):
    w = sel.shape[0]
    parts = _split_bf16(x)
    slabs = [sum(_dot(p[:, c:c + w], sel) for p in parts) for c in range(0, x.shape[1], w)]
    return slabs[0] if len(slabs) == 1 else jnp.concatenate(slabs, axis=1)


def _dot_sel_x(sel, x):
    w = sel.shape[1]
    parts = _split_bf16(x)
    slabs = [sum(_dot(sel, p[r:r + w]) for p in parts) for r in range(0, x.shape[0], w)]
    return slabs[0] if len(slabs) == 1 else jnp.concatenate(slabs, axis=0)


def _pack_bf16_pairs(x):
    n = x.shape[1] // 2
    bits = lambda v: lax.bitcast_convert_type(v.astype(BF16).astype(F32), jnp.uint32)
    return lax.bitcast_convert_type((bits(x[:, :n]) >> 16) | bits(x[:, n:]), F32)


def _unpack_bf16_pairs(w):
    u = lax.bitcast_convert_type(w, jnp.uint32)
    lo = lax.bitcast_convert_type(u << 16, F32)
    hi = lax.bitcast_convert_type(u & jnp.uint32(0xFFFF0000), F32)
    return jnp.concatenate([lo, hi], axis=1)


def _block_diag_ones(n, blk, scale=1.0):
    i = np.arange(n)
    return jnp.asarray(((i[:, None] // blk) == (i[None, :] // blk)).astype(np.float32) * scale, BF16)


def _norm_matmul_kernel(x_ref, g_ref, *refs):
    n = len(refs) // 2
    xn = _rms(x_ref[...], g_ref[...]).astype(BF16)
    for w_ref, o_ref in zip(refs[:n], refs[n:]):
        o_ref[...] = _dot(xn, w_ref[...])


def norm_matmul(x, g, ws, tm):
    m, d = x.shape
    return pl.pallas_call(
        _norm_matmul_kernel,
        grid=(m // tm,),
        in_specs=[pl.BlockSpec((tm, d), lambda i: (i, 0)), pl.BlockSpec((1, d), lambda i: (0, 0))]
                 + [pl.BlockSpec((d, w.shape[1]), lambda i: (0, 0)) for w in ws],
        out_specs=[pl.BlockSpec((tm, w.shape[1]), lambda i: (i, 0)) for w in ws],
        out_shape=[jax.ShapeDtypeStruct((m, w.shape[1]), F32) for w in ws],
        compiler_params=_cparams("parallel"),
        name="norm_matmul",
    )(x, g.reshape(1, d), *[w.astype(BF16) for w in ws])


def _nsa_prep_kernel(p_ref, qg_ref, ksg_ref, kwg_ref, bdq_ref, bdk_ref, place_ref, feat_ref,
                     qt_o, ks_o, vst_o, kw_o, vwt_o, gate_o):
    p = p_ref[...]
    tm = p.shape[0]
    nq = NSA_HEADS * HEAD_DIM
    gw = NSA_GROUPS * HEAD_DIM
    q = p[:, :nq]
    msq = _dot_x_sel(q * q, bdq_ref[...])
    qn = q * lax.rsqrt(msq + RMS_EPS) * qg_ref[...] * (HEAD_DIM ** -0.5)
    qt_o[0] = qn.T.reshape(NSA_HEADS, HEAD_DIM, tm).astype(BF16)

    def seg(k):
        return p[:, nq + k * gw: nq + (k + 1) * gw]

    def head_norm(t, g):
        ms = _dot_x_sel(t * t, bdk_ref[...])
        return t * lax.rsqrt(ms + RMS_EPS) * g

    feat = feat_ref[...].astype(F32)
    ks = head_norm(seg(2), ksg_ref[...]).astype(BF16)
    kw = head_norm(seg(4), kwg_ref[...]).astype(BF16)
    vs_t = seg(3).T
    vw_t = seg(5).T
    for g in range(NSA_GROUPS):
        ks_o[0, g] = (_dot(ks, place_ref[g]) + feat).astype(BF16)
        kw_o[0, g] = (_dot(kw, place_ref[g])[:, :2 * HEAD_DIM] + feat[:, :2 * HEAD_DIM]).astype(BF16)
        for j in range(tm // Q_BLOCK):
            tile = (slice(g * HEAD_DIM, (g + 1) * HEAD_DIM), slice(j * Q_BLOCK, (j + 1) * Q_BLOCK))
            vst_o[0, g, j] = vs_t[tile].astype(BF16)
            vwt_o[0, g, j] = vw_t[tile].astype(BF16)
    gate_o[...] = jax.nn.sigmoid(p[:, nq + 6 * gw: nq + 6 * gw + LANES])


def nsa_prep(p_nsa, q_g, ks_g, kw_g, b, s, tm):
    t = p_nsa.shape[0]
    nq = NSA_HEADS * HEAD_DIM
    gw = NSA_GROUPS * HEAD_DIM
    grp, dh = NSA_GROUPS, HEAD_DIM
    tps = s // tm
    kx = 2 * dh + LANES
    assert s // SLC_BLOCK <= LANES
    tok = np.arange(s)
    feat = np.zeros((s, kx), np.float32)
    feat[:, dh] = tok // SLC_BLOCK * SLC_BLOCK
    feat[:, dh + 1] = tok % SLC_BLOCK
    feat[:, dh + 2:dh + 4] = 1.0
    feat[tok, 2 * dh + tok // SLC_BLOCK] = 1.0
    place = np.zeros((grp, gw, kx), np.float32)
    for g in range(grp):
        place[g, g * dh + np.arange(dh), np.arange(dh)] = 1.0
    tile = lambda v, n: jnp.tile(v.reshape(1, HEAD_DIM), (1, n))
    row = lambda w: pl.BlockSpec((tm, w), lambda i: (i, 0))
    full = lambda *sh: pl.BlockSpec(sh, lambda i: (0,) * len(sh))
    seq = lambda *sh: pl.BlockSpec((1, grp) + sh, lambda i: (i // tps, 0, i % tps) + (0,) * (len(sh) - 1))
    return pl.pallas_call(
        _nsa_prep_kernel,
        grid=(t // tm,),
        in_specs=[row(NSA_PROJ_PAD), full(1, nq), full(1, gw), full(1, gw), full(MXU_DIM, MXU_DIM), full(gw, gw),
                  full(grp, gw, kx), pl.BlockSpec((tm, kx), lambda i: (i % tps, 0))],
        out_specs=[pl.BlockSpec((1, NSA_HEADS, dh, tm), lambda i: (i // tps, 0, 0, i % tps)),
                   seq(tm, kx), seq(tm // Q_BLOCK, dh, Q_BLOCK), seq(tm, 2 * dh), seq(tm // Q_BLOCK, dh, Q_BLOCK),
                   row(LANES)],
        out_shape=[jax.ShapeDtypeStruct((b, NSA_HEADS, dh, s), BF16),
                   jax.ShapeDtypeStruct((b, grp, s, kx), BF16),
                   jax.ShapeDtypeStruct((b, grp, s // Q_BLOCK, dh, Q_BLOCK), BF16),
                   jax.ShapeDtypeStruct((b, grp, s, 2 * dh), BF16),
                   jax.ShapeDtypeStruct((b, grp, s // Q_BLOCK, dh, Q_BLOCK), BF16),
                   jax.ShapeDtypeStruct((t, LANES), F32)],
        compiler_params=_cparams("parallel"),
        name="nsa_prep",
    )(p_nsa, tile(q_g, NSA_HEADS), tile(ks_g, NSA_GROUPS), tile(kw_g, NSA_GROUPS),
      _block_diag_ones(MXU_DIM, HEAD_DIM, 1.0 / HEAD_DIM), _block_diag_ones(gw, HEAD_DIM, 1.0 / HEAD_DIM),
      jnp.asarray(place, BF16), jnp.asarray(feat, BF16))


def _compress_kernel(x_ref, pe_ref, w1_ref, w2_ref, g_ref, bd_ref, o_ref):
    nc = x_ref.shape[0] // CMP_STRIDE
    first = jnp.zeros((nc, x_ref.shape[1]), F32)
    second = jnp.zeros_like(first)
    for l in range(CMP_STRIDE):
        x_l = x_ref[pl.ds(l, nc, stride=CMP_STRIDE), :]
        first = first + _dot(x_l + pe_ref[0, l:l + 1], w1_ref[0, l], precision=HIGHEST)
        second = second + _dot(x_l + pe_ref[0, CMP_STRIDE + l:CMP_STRIDE + l + 1], w1_ref[0, CMP_STRIDE + l],
                               precision=HIGHEST)
    h1 = first + pltpu.roll(second, nc - 1, 0)
    out = _dot(jax.nn.silu(h1), w2_ref[0], precision=HIGHEST)
    normed = out * lax.rsqrt(_dot_x_sel(out * out, bd_ref[...]) + RMS_EPS) * g_ref[...]
    o_ref[0, 0] = jnp.where(pl.program_id(0) == 0, normed, out).astype(BF16)


def nsa_compress(p_nsa, pe_k, pe_v, kw1, kw2, vw1, vw2, kc_g, b, s):
    nc = s // CMP_STRIDE
    gw = NSA_GROUPS * HEAD_DIM
    first_col_block = NSA_HEADS * HEAD_DIM // gw
    eye = jnp.eye(NSA_GROUPS, dtype=F32)

    def both(w):
        return jnp.einsum('gh,...ij->...gihj', eye, w).reshape(w.shape[:-2] + (gw, gw))

    w1 = both(jnp.stack([kw1, vw1]).reshape(2, CMP_BLOCK, HEAD_DIM, HEAD_DIM))
    w2 = both(jnp.stack([kw2, vw2]))
    pe = jnp.stack([jnp.tile(pe_k, (1, NSA_GROUPS)), jnp.tile(pe_v, (1, NSA_GROUPS))])
    return pl.pallas_call(
        _compress_kernel,
        grid=(2, b),
        in_specs=[pl.BlockSpec((s, gw), lambda kv, bi: (bi, first_col_block + kv)),
                  pl.BlockSpec((1, CMP_BLOCK, gw), lambda kv, bi: (kv, 0, 0)),
                  pl.BlockSpec((1, CMP_BLOCK, gw, gw), lambda kv, bi: (kv, 0, 0, 0)),
                  pl.BlockSpec((1, gw, gw), lambda kv, bi: (kv, 0, 0)),
                  pl.BlockSpec((1, gw), lambda kv, bi: (0, 0)),
                  pl.BlockSpec((gw, gw), lambda kv, bi: (0, 0))],
        out_specs=pl.BlockSpec((1, 1, nc, gw), lambda kv, bi: (kv, bi, 0, 0)),
        out_shape=jax.ShapeDtypeStruct((2, b, nc, gw), BF16),
        compiler_params=_cparams("parallel", "parallel"),
        name="nsa_compress",
    )(p_nsa, pe, w1, w2, jnp.tile(kc_g.reshape(1, HEAD_DIM), (1, NSA_GROUPS)),
      _block_diag_ones(gw, HEAD_DIM, 1.0 / HEAD_DIM))


def _masked_exp_cols(s, mask):
    sm = jnp.where(mask, s, NEG_INF)
    m = jnp.max(sm, axis=0, keepdims=True)
    p = jnp.exp(sm - jnp.where(m > 0.5 * NEG_INF, m, 0.0))
    l = jnp.sum(p, axis=0, keepdims=True)
    return p, 1.0 / jnp.where(l > 0.0, l, 1.0)


def _nsa_attn_kernel(qt_ref, kc_ref, vct_ref, ovt_ref, ks_ref, vst_ref, kw_ref, vwt_ref, gate_ref, o_ref,
                     *, n_sel, n_top):
    g = pl.program_id(1)
    t0 = pl.program_id(2) * Q_BLOCK
    ks_ref, vst_ref, kw_ref, vwt_ref = (r.at[0, 0] for r in (ks_ref, vst_ref, kw_ref, vwt_ref))
    cols = NSA_HPG * Q_BLOCK
    col = lax.broadcasted_iota(jnp.int32, (1, cols), 1)
    tq = t0 + col % Q_BLOCK
    head = g * NSA_HPG + col // Q_BLOCK
    slope = lax.bitcast_convert_type((127 - (head + 1)) << 23, F32)
    qt = jnp.concatenate([qt_ref[0, h] for h in range(NSA_HPG)], axis=1)
    frow = lax.broadcasted_iota(jnp.int32, (HEAD_DIM, cols), 0)
    tq_hi = (tq // SLC_BLOCK * SLC_BLOCK).astype(F32)
    tq_lo = (tq % SLC_BLOCK).astype(F32)
    qpos = jnp.where(frow < 2, slope, jnp.where(frow == 2, -slope * tq_hi, jnp.where(frow == 3, -slope * tq_lo, 0.0)))
    qc = jnp.concatenate([qt, qpos.astype(BF16)], axis=0)

    def heads_sum(x):
        acc = x[:, :Q_BLOCK]
        for h in range(1, NSA_HPG):
            acc = acc + x[:, h * Q_BLOCK:(h + 1) * Q_BLOCK]
        return acc

    kc = kc_ref[0, 0]
    nc = kc.shape[0]
    c_last = lax.broadcasted_iota(jnp.int32, (nc, 1), 0) * CMP_STRIDE + (CMP_BLOCK - 1)
    p_c, inv_c = _masked_exp_cols(_dot(kc, qc), c_last <= tq)
    p_c = p_c * inv_c
    o_c = _dot(vct_ref[0, 0], p_c.astype(BF16))
    imp = _dot_sel_x(ovt_ref[...], heads_sum(p_c))

    bid = lax.broadcasted_iota(jnp.int32, (LANES, Q_BLOCK), 0)
    bidf = bid.astype(F32)
    tq1 = t0 + lax.broadcasted_iota(jnp.int32, (1, Q_BLOCK), 1)
    cur = tq1 // SLC_BLOCK
    forced = (bid == 0) | (bid == cur) | (bid == cur - 1)
    score = jnp.where(forced, BIG, jnp.where(bid * SLC_BLOCK <= tq1, imp, -BIG))
    score = jnp.where(bid < n_sel, score, REMOVED)
    picked = jnp.zeros((LANES, Q_BLOCK), jnp.bool_)
    for _ in range(n_top):
        m = jnp.max(score, axis=0, keepdims=True)
        pick = bidf == jnp.min(jnp.where(score == m, bidf, float(LANES)), axis=0, keepdims=True)
        picked = picked | pick
        score = jnp.where(pick, REMOVED, score)
    sel_neg = jnp.where(picked, 0.0, NEG_INF).astype(BF16)
    qx = jnp.concatenate([qc, jnp.concatenate([sel_neg] * NSA_HPG, axis=1)], axis=0)

    def values_t(ref, first_tile, n_tiles):
        return jnp.concatenate([ref[first_tile + i] for i in range(n_tiles)], axis=1)

    span = WINDOW + Q_BLOCK
    ws = pl.multiple_of(jnp.maximum(t0 - WINDOW, 0), Q_BLOCK)
    dw = tq - (ws + lax.broadcasted_iota(jnp.int32, (span, 1), 0))
    in_window = dw.astype(jnp.uint32) < WINDOW
    p_w, inv_w = _masked_exp_cols(_dot(kw_ref[pl.ds(ws, span), :], qc), in_window)
    o_w = _dot(values_t(vwt_ref, ws // Q_BLOCK, span // Q_BLOCK), p_w.astype(BF16)) * inv_w

    def scores(chunk):
        return _dot(ks_ref[pl.ds(pl.multiple_of(chunk * SEL_CHUNK, SEL_CHUNK), SEL_CHUNK), :], qx)

    def flash(s, chunk, carry):
        m, l, acc = carry
        m_new = jnp.maximum(m, jnp.max(s, axis=0, keepdims=True))
        alpha = jnp.exp(m - m_new)
        p = jnp.exp(s - m_new)
        l = alpha * l + jnp.sum(p, axis=0, keepdims=True)
        v = values_t(vst_ref, chunk * (SEL_CHUNK // Q_BLOCK), SEL_CHUNK // Q_BLOCK)
        return m_new, l, alpha * acc + _dot(v, p.astype(BF16))

    def flash_pair(chunk_a, chunk_b, carry, keep_b=None):
        s_a, s_b = scores(chunk_a), scores(chunk_b)
        if keep_b is not None:
            s_b = jnp.where(keep_b, s_b, NEG_INF)
        return flash(s_b, chunk_b, flash(s_a, chunk_a, carry))

    bpc = SEL_CHUNK // SLC_BLOCK
    blk_any = jnp.max(jnp.where(picked, 1.0, 0.0), axis=1, keepdims=True)
    chunk_bit = lax.bitcast_convert_type((bid[:, :1] // bpc + 127) << 23, F32)
    bits = jnp.max((blk_any * chunk_bit).reshape(LANES // bpc, bpc, 1), axis=1)
    active = jnp.sum(bits, axis=0, keepdims=True)[0, 0].astype(jnp.int32)

    def full_step(j, state):
        def on_active(st):
            return lax.cond(st[0] >= 0,
                            lambda s2: (jnp.int32(-1), flash_pair(s2[0], j, s2[1])),
                            lambda s2: (j, s2[1]), st)
        return lax.cond((active >> j) & 1 == 1, on_active, lambda st: st, state)

    n_full = t0 // SEL_CHUNK
    init = (jnp.full((1, cols), NEG_INF, F32), jnp.zeros((1, cols), F32), jnp.zeros((HEAD_DIM, cols), F32))
    waiting, carry = lax.fori_loop(0, n_full, full_step, (jnp.int32(-1), init))
    keep = n_full * SEL_CHUNK + lax.broadcasted_iota(jnp.int32, (SEL_CHUNK, 1), 0) <= tq
    _, l_s, acc_s = lax.cond(
        waiting >= 0,
        lambda c: flash_pair(waiting, n_full, c, keep_b=keep),
        lambda c: flash(jnp.where(keep, scores(n_full), NEG_INF), n_full, c), carry)
    o_s = acc_s / l_s

    for h in range(NSA_HPG):
        hs = slice(h * Q_BLOCK, (h + 1) * Q_BLOCK)
        gate = lambda br: gate_ref[0, 0, br, h:h + 1, :]
        o_ref[0, h] = gate(0) * o_c[:, hs] + gate(1) * o_s[:, hs] + gate(2) * o_w[:, hs]


def nsa_attention(qt, kcv, ks_x, vs_t, kw_x, vw_t, gates, b, s):
    nq_blocks = s // Q_BLOCK
    nc = s // CMP_STRIDE
    n_cmp = nc - CMP_BLOCK // CMP_STRIDE + 1
    n_sel = s // SLC_BLOCK
    n_top = min(SLC_TOPK, n_sel)
    grp, dh = NSA_GROUPS, HEAD_DIM
    assert n_sel <= LANES and s % SEL_CHUNK == 0 and s >= WINDOW + Q_BLOCK
    c_start = np.arange(n_cmp) * CMP_STRIDE
    s_start = np.arange(n_sel) * SLC_BLOCK
    ovt = np.zeros((LANES, nc), np.float32)
    ovt[:n_sel, :n_cmp] = (np.clip(np.minimum((c_start + CMP_BLOCK)[:, None], s_start[None] + SLC_BLOCK)
                                   - np.maximum(c_start[:, None], s_start[None]), 0, None) / CMP_BLOCK).T

    c_pos = np.arange(nc) * CMP_STRIDE + (CMP_BLOCK - 1)
    c_feat = np.zeros((nc, dh), np.float32)
    c_feat[:, 0] = c_pos // SLC_BLOCK * SLC_BLOCK
    c_feat[:, 1] = c_pos % SLC_BLOCK
    c_feat[:, 2:4] = 1.0
    kc, vc = (a.reshape(b, nc, grp, dh) for a in kcv)
    kc = kc.transpose(0, 2, 1, 3)
    kc_x = jnp.concatenate([kc, jnp.broadcast_to(jnp.asarray(c_feat, BF16), kc.shape)], axis=-1)
    vct = vc.transpose(0, 2, 3, 1)
    gates_t = gates[:, :NSA_HEADS * 3].reshape(b, s, grp, NSA_HPG, 3).transpose(0, 2, 4, 3, 1)

    grp_spec = lambda *shape: pl.BlockSpec((1, 1) + shape, lambda bi, g, i: (bi, g) + (0,) * len(shape),
                                           pipeline_mode=pl.Buffered(1))
    return pl.pallas_call(
        functools.partial(_nsa_attn_kernel, n_sel=n_sel, n_top=n_top),
        grid=(b, grp, nq_blocks),
        in_specs=[pl.BlockSpec((1, NSA_HPG, dh, Q_BLOCK), lambda bi, g, i: (bi, g, 0, i)),
                  grp_spec(nc, 2 * dh), grp_spec(dh, nc),
                  pl.BlockSpec((LANES, nc), lambda bi, g, i: (0, 0)),
                  grp_spec(s, 2 * dh + LANES), grp_spec(s // Q_BLOCK, dh, Q_BLOCK),
                  grp_spec(s, 2 * dh), grp_spec(s // Q_BLOCK, dh, Q_BLOCK),
                  pl.BlockSpec((1, 1, 3, NSA_HPG, Q_BLOCK), lambda bi, g, i: (bi, g, 0, 0, i))],
        out_specs=pl.BlockSpec((1, NSA_HPG, dh, Q_BLOCK), lambda bi, g, i: (bi, g, 0, i)),
        out_shape=jax.ShapeDtypeStruct((b, NSA_HEADS, dh, s), F32),
        compiler_params=_cparams("parallel", "parallel", "arbitrary"),
        name="nsa_attention",
    )(qt, kc_x, vct, jnp.asarray(ovt, BF16), ks_x, vs_t, kw_x, vw_t, gates_t)


def _rwkv_prep_kernel(p_ref, prev_ref, mu_ref, w0_ref, a0_ref, kk_ref, ka_ref, rk_ref,
                      wup_ref, aup_ref, gup_ref, bd_ref, ltri_ref, lones_ref, csum_ref,
                      at_o, bt_o, kt_o, rt_o, v_o, bw_o, kw_o, wc_o, g_o, bonus_o, *, tiles_per_seq):
    p = p_ref[...]
    tm = p.shape[0]
    first = pl.program_id(0) % tiles_per_seq == 0
    last_prev = jnp.where(first, 0.0, prev_ref[7:8, :])
    prev = pltpu.roll(p, 1, 0)
    prev = jnp.where(lax.broadcasted_iota(jnp.int32, (tm, 1), 0) == 0, last_prev, prev)
    pm = p + (prev - p) * mu_ref[...]
    d = RWKV_DIM
    r, k, v = pm[:, :d], pm[:, d:2 * d], pm[:, 2 * d:3 * d]
    lora = pm[:, 3 * d:3 * d + LANES]
    gd = pm[:, 3 * d + LANES:3 * d + 2 * LANES]
    z = -(w0_ref[...] + _dot(jnp.tanh(lora).astype(BF16), wup_ref[...]))
    softplus = jnp.maximum(z, 0.0) + jnp.log(1.0 + jnp.exp(-jnp.abs(z)))
    w = -softplus - 0.5
    a = jax.nn.sigmoid(a0_ref[...] + _dot(lora.astype(BF16), aup_ref[...]))
    g_o[...] = _dot(jax.nn.sigmoid(gd).astype(BF16), gup_ref[...])
    bd = bd_ref[...]
    kkr = k * kk_ref[...]
    kk = kkr / jnp.maximum(jnp.sqrt(_dot_x_sel(kkr * kkr, bd)), 1e-12)
    k2 = k * (1.0 + (a - 1.0) * ka_ref[...])
    bonus_o[...] = _dot_x_sel(r * k2 * rk_ref[...], bd) * v
    lw = -jnp.exp(w)
    lw_parts = _split_bf16(lw)
    cum = _dot_sel_x(ltri_ref[...], lw)
    tot = _dot_sel_x(lones_ref[...], lw)
    e_in = jnp.exp(cum)
    e_out = jnp.exp(-cum)
    e_end = jnp.exp(tot - cum)

    def put_heads(o, val):
        for h in range(RWKV_HEADS):
            o[0, h] = val[:, h * HEAD_DIM:(h + 1) * HEAD_DIM].astype(o.dtype)

    put_heads(at_o, -kk * jnp.exp(cum - lw))
    put_heads(bt_o, kk * a * e_out)
    put_heads(kt_o, k2 * e_out)
    put_heads(rt_o, r * e_in)
    put_heads(v_o, v)
    put_heads(bw_o, kk * a * e_end)
    put_heads(kw_o, k2 * e_end)
    put_heads(wc_o, jnp.exp(sum(_dot(csum_ref[...], p) for p in lw_parts)))


def rwkv_prep(p_rwkv, mu, w0, w_up, a0, a_up, g_up, k_k, k_a, r_k, b, s, tm):
    t = p_rwkv.shape[0]
    d = RWKV_DIM
    c = RWKV_CHUNK
    tps = s // tm
    cpt = tm // c
    wup = jnp.concatenate([w_up, jnp.zeros_like(a_up)], axis=0).astype(BF16)
    aup = jnp.concatenate([jnp.zeros_like(w_up), a_up], axis=0).astype(BF16)
    i = np.arange(tm)
    j = np.arange(MXU_DIM)
    same = (j[:, None] // c) == (j[None, :] // c)
    ltri = jnp.asarray(same & (j[:, None] >= j[None, :]), BF16)
    lones = jnp.asarray(same, BF16)
    csum = jnp.asarray(np.arange(cpt)[:, None] == (i[None, :] // c), BF16)
    row = lambda w: pl.BlockSpec((tm, w), lambda i: (i, 0))
    full = lambda *sh: pl.BlockSpec(sh, lambda i: (0,) * len(sh))
    heads = lambda n: pl.BlockSpec((1, RWKV_HEADS, n, HEAD_DIM), lambda i: (i // tps, 0, i % tps, 0))
    hshape = lambda n, dt: jax.ShapeDtypeStruct((b, RWKV_HEADS, n, HEAD_DIM), dt)
    vec = lambda x: x.reshape(1, -1)
    return pl.pallas_call(
        functools.partial(_rwkv_prep_kernel, tiles_per_seq=tps),
        grid=(t // tm,),
        in_specs=[row(RWKV_PROJ),
                  pl.BlockSpec((8, RWKV_PROJ), lambda i: (jnp.maximum(i * (tm // 8) - 1, 0), 0)),
                  full(1, RWKV_PROJ), full(1, d), full(1, d), full(1, d), full(1, d), full(1, d),
                  full(LANES, d), full(LANES, d), full(LANES, d), full(MXU_DIM, MXU_DIM),
                  full(MXU_DIM, MXU_DIM), full(MXU_DIM, MXU_DIM), full(cpt, tm)],
        out_specs=[heads(tm)] * 7 + [heads(cpt), row(d), row(d)],
        out_shape=[hshape(s, BF16)] * 7 + [hshape(s // c, F32)] + [jax.ShapeDtypeStruct((t, d), F32)] * 2,
        compiler_params=_cparams("parallel"),
        name="rwkv_prep",
    )(p_rwkv, p_rwkv, vec(mu), vec(w0), vec(a0), vec(k_k), vec(k_a), vec(r_k), wup, aup, g_up.astype(BF16),
      _block_diag_ones(MXU_DIM, HEAD_DIM), ltri, lones, csum)


def _bdot(a, b):
    return lax.dot_general(a, b, (((2,), (1,)), ((0,), (0,))), preferred_element_type=F32)


def _bdot_nt(a, b):
    return lax.dot_general(a, b, (((2,), (2,)), ((0,), (0,))), preferred_element_type=F32)


def _bdot_tn(a, b):
    return lax.dot_general(a, b, (((1,), (1,)), ((0,), (0,))), preferred_element_type=F32)


def _rwkv_intra_kernel(at_ref, bt_ref, kt_ref, rt_ref, v_ref, ta_o, tr_o, arb_o, yv_o):
    c = RWKV_CHUNK
    _, nh, ts, dh = at_ref.shape
    n = nh * (ts // c)
    chunked = lambda ref: ref[0].reshape(n, c, dh)
    at, bt, kt, rt, v = (chunked(r) for r in (at_ref, bt_ref, kt_ref, rt_ref, v_ref))
    ri = lax.broadcasted_iota(jnp.int32, (1, c, c), 1)
    ci = lax.broadcasted_iota(jnp.int32, (1, c, c), 2)
    strict = ri > ci
    incl = ri >= ci
    ar = jnp.concatenate([at, rt], axis=1)
    xb = _bdot_nt(ar, bt)
    xk = _bdot_nt(ar, kt)
    l_ab = jnp.where(strict, xb[:, :c], 0.0)
    a_ak = jnp.where(strict, xk[:, :c], 0.0)
    a_rb = jnp.where(incl, xb[:, c:], 0.0)
    a_rk = jnp.where(incl, xk[:, c:], 0.0)
    pw = l_ab
    tinv = jnp.where(ri == ci, 1.0, 0.0) + l_ab
    for _ in range(int(np.log2(c)) - 1):
        pw_b = pw.astype(BF16)
        pw = _bdot(pw_b, pw_b)
        tinv = tinv + _bdot(tinv.astype(BF16), pw.astype(BF16))
    tinv_b = tinv.astype(BF16)

    def put(o, val):
        o[0] = val.reshape(nh, ts, val.shape[-1]).astype(o.dtype)

    put(ta_o, _bdot(tinv_b, at))
    put(tr_o, _bdot(tinv_b, _bdot(a_ak.astype(BF16), v).astype(BF16)))
    put(arb_o, a_rb)
    put(yv_o, _bdot(a_rk.astype(BF16), v))


def rwkv_intra(at, bt, kt, rt, v, ts):
    b, h, s, dh = at.shape
    seq = lambda: pl.BlockSpec((1, h, ts, dh), lambda bi, i: (bi, 0, i, 0))
    shp = lambda dt: jax.ShapeDtypeStruct((b, h, s, dh), dt)
    return pl.pallas_call(
        _rwkv_intra_kernel,
        grid=(b, s // ts),
        in_specs=[seq()] * 5,
        out_specs=[seq()] * 4,
        out_shape=[shp(BF16), shp(F32), shp(BF16), shp(F32)],
        compiler_params=_cparams("parallel", "parallel"),
        name="rwkv_intra",
    )(at, bt, kt, rt, v)


def _rwkv_scan_kernel(ta_ref, tr_ref, arb_ref, yv_ref, rt_ref, v_ref, bw_ref, kw_ref, wc_ref, y_ref, st_ref):
    c = RWKV_CHUNK
    nb, nh, ts, dh = ta_ref.shape
    n = nb * nh

    @pl.when(pl.program_id(0) == 0)
    def _():
        st_ref[...] = jnp.zeros_like(st_ref)

    def chunk_step(j, _):
        sl = (slice(None), slice(None), pl.ds(pl.multiple_of(j * c, c), c), slice(None))
        get = lambda ref: ref[sl].reshape(n, c, dh)
        st = st_ref[...]
        st_b = st.astype(BF16)
        u = _bdot_nt(get(ta_ref), st_b) + get(tr_ref)
        u_b = u.astype(BF16)
        y = _bdot_nt(get(rt_ref), st_b) + _bdot(get(arb_ref), u_b) + get(yv_ref)
        wc = wc_ref[:, :, pl.ds(pl.program_id(0) * (ts // c) + j, 1), :].reshape(n, 1, dh)
        st_ref[...] = st * wc + _bdot_tn(jnp.concatenate([u_b, get(v_ref)], axis=1),
                                         jnp.concatenate([get(bw_ref), get(kw_ref)], axis=1))
        y_ref[sl] = y.reshape(nb, nh, c, dh)
        return 0

    lax.fori_loop(0, ts // c, chunk_step, 0)


def rwkv_scan(ta, tr, arb, yv, rt, v, bw, kw, wc, ts):
    b, h, s, dh = ta.shape
    seq = lambda n: pl.BlockSpec((b, h, n, dh), lambda i: (0, 0, i, 0))
    return pl.pallas_call(
        _rwkv_scan_kernel,
        grid=(s // ts,),
        in_specs=[seq(ts)] * 8 + [pl.BlockSpec(wc.shape, lambda i: (0, 0, 0, 0))],
        out_specs=seq(ts),
        out_shape=jax.ShapeDtypeStruct((b, h, s, dh), F32),
        scratch_shapes=[pltpu.VMEM((b * h, dh, dh), F32)],
        compiler_params=_cparams("arbitrary"),
        name="rwkv_scan",
    )(ta, tr, arb, yv, rt, v, bw, kw, wc)


def _out_proj_kernel(x_ref, on_ref, y_ref, bonus_ref, g_ref, lnw_ref, lnb_ref, bd_ref, wn_ref, wr_ref, o_ref):
    y = jnp.concatenate([y_ref[0, h] for h in range(RWKV_HEADS)], axis=-1)
    bd = bd_ref[...]
    yc = y - _dot_x_sel(y, bd)
    yn = yc * lax.rsqrt(_dot_x_sel(yc * yc, bd) + GN_EPS)
    o_rwkv = (yn * lnw_ref[...] + lnb_ref[...] + bonus_ref[...]) * g_ref[...]
    tm = y.shape[0]
    o_nsa_t = on_ref[0].reshape(NSA_HEADS * HEAD_DIM, tm)
    o_ref[...] = (x_ref[...] + _dot_tn(o_nsa_t.astype(BF16), wn_ref[...])
                  + _dot(o_rwkv.astype(BF16), wr_ref[...]))


def out_proj(x, o_nsa_t, y, bonus, g, ln_w, ln_b, w_out, s, tm):
    t, d = x.shape
    dn = o_nsa_t.shape[1] * o_nsa_t.shape[2]
    dr = bonus.shape[1]
    tps = s // tm
    row = lambda w: pl.BlockSpec((tm, w), lambda i: (i, 0))
    full = lambda *sh: pl.BlockSpec(sh, lambda i: (0,) * len(sh))
    return pl.pallas_call(
        _out_proj_kernel,
        grid=(t // tm,),
        in_specs=[row(d), pl.BlockSpec((1, NSA_HEADS, HEAD_DIM, tm), lambda i: (i // tps, 0, 0, i % tps)),
                  pl.BlockSpec((1, RWKV_HEADS, tm, HEAD_DIM), lambda i: (i // tps, 0, i % tps, 0)),
                  row(dr), row(dr), full(1, dr), full(1, dr), full(MXU_DIM, MXU_DIM), full(dn, d), full(dr, d)],
        out_specs=row(d),
        out_shape=jax.ShapeDtypeStruct((t, d), F32),
        compiler_params=_cparams("parallel"),
        name="out_proj",
    )(x, o_nsa_t, y, bonus, g, ln_w.reshape(1, dr), ln_b.reshape(1, dr),
      _block_diag_ones(MXU_DIM, HEAD_DIM, 1.0 / HEAD_DIM), w_out[:dn].astype(BF16), w_out[dn:].astype(BF16))


def _cross_attn_kernel(h_ref, g_ref, wq_ref, qg_ref, kv_ref, kg_ref, wo_ref, o_ref):
    h = h_ref[...]
    d = h.shape[1]
    xd = d // X_HEADS
    q = _dot(_rms(h, g_ref[...]).astype(BF16), wq_ref[...])
    kv = kv_ref[0]
    outs = []
    for hd in range(X_HEADS):
        qh = _rms(q[:, hd * xd:(hd + 1) * xd], qg_ref[...]) * (xd ** -0.5)
        kh = _rms(kv[:, hd * xd:(hd + 1) * xd], kg_ref[...])
        vh = kv[:, d + hd * xd:d + (hd + 1) * xd]
        s = _dot_nt(qh.astype(BF16), kh.astype(BF16))
        p = jnp.exp(s - jnp.max(s, axis=-1, keepdims=True))
        p = p / jnp.sum(p, axis=-1, keepdims=True)
        outs.append(_dot(p.astype(BF16), vh.astype(BF16)))
    o = jnp.concatenate(outs, axis=-1)
    o_ref[...] = h + _dot(o.astype(BF16), wo_ref[...])


def cross_attention(h, kv, norm_g, xq_w, xq_g, xk_g, xo_w, b, s, tm):
    t, d = h.shape
    m = kv.shape[1]
    xd = d // X_HEADS
    tiles = s // tm
    full = lambda *sh: pl.BlockSpec(sh, lambda i: (0,) * len(sh))
    return pl.pallas_call(
        _cross_attn_kernel,
        grid=(t // tm,),
        in_specs=[pl.BlockSpec((tm, d), lambda i: (i, 0)), full(1, d), full(d, d), full(1, xd),
                  pl.BlockSpec((1, m, 2 * d), lambda i: (i // tiles, 0, 0)), full(1, xd), full(d, d)],
        out_specs=pl.BlockSpec((tm, d), lambda i: (i, 0)),
        out_shape=jax.ShapeDtypeStruct((t, d), F32),
        compiler_params=_cparams("parallel"),
        name="cross_attention",
    )(h, norm_g.reshape(1, d), xq_w.astype(BF16), xq_g.reshape(1, xd), kv, xk_g.reshape(1, xd),
      xo_w.astype(BF16))


def _router_kernel(h_ref, g_ref, rw_ref, rb_ref, ltri_ref, xn_o, idx_o, gate_o, rank_o, count_o, seen_ref):
    @pl.when(pl.program_id(0) == 0)
    def _():
        seen_ref[...] = jnp.zeros_like(seen_ref)

    xn = _rms(h_ref[...], g_ref[...])
    xn_o[...] = xn
    logits = _dot(xn, rw_ref[...], precision=HIGHEST) + rb_ref[...]
    tm = logits.shape[0]
    lane = lax.broadcasted_iota(jnp.int32, (tm, LANES), 1)
    lanef = lane.astype(F32)
    logits = jnp.where(lane < N_EXPERTS, logits, REMOVED)
    idx_acc = jnp.zeros((tm, LANES), F32)
    val_acc = jnp.zeros((tm, LANES), F32)
    chosen = jnp.zeros((tm, LANES), F32)
    picks = []
    top = None
    for k in range(TOP_K):
        m = jnp.max(logits, axis=-1, keepdims=True)
        idx = jnp.min(jnp.where(logits == m, lanef, float(LANES)), axis=-1, keepdims=True)
        pick = lanef == idx
        picks.append(pick)
        chosen = jnp.where(pick, 1.0, chosen)
        logits = jnp.where(pick, REMOVED, logits)
        top = m if top is None else top
        idx_acc = jnp.where(lane == k, idx, idx_acc)
        val_acc = jnp.where(lane == k, jnp.exp(m - top), val_acc)
    idx_o[...] = idx_acc.astype(jnp.int32)
    gate_o[...] = val_acc / jnp.sum(val_acc, axis=-1, keepdims=True)
    before = seen_ref[0:1, :] + _dot(ltri_ref[...], chosen.astype(BF16))
    rank_acc = jnp.zeros((tm, LANES), F32)
    for k, pick in enumerate(picks):
        rank_acc = jnp.where(lane == k, jnp.sum(jnp.where(pick, before, 0.0), axis=-1, keepdims=True), rank_acc)
    rank_o[...] = rank_acc.astype(jnp.int32)
    seen_ref[...] = seen_ref[...] + jnp.sum(chosen, axis=0, keepdims=True)
    count_o[...] = seen_ref[...]


def moe_router(h, norm_g, router_w, router_b, tm):
    t, d = h.shape
    rw = jnp.zeros((d, LANES), F32).at[:, :N_EXPERTS].set(router_w)
    rb = jnp.zeros((1, LANES), F32).at[0, :N_EXPERTS].set(router_b)
    i = np.arange(tm)
    ltri = jnp.asarray(i[:, None] > i[None, :], BF16)
    row = lambda w: pl.BlockSpec((tm, w), lambda i: (i, 0))
    full = lambda *s: pl.BlockSpec(s, lambda i: (0,) * len(s))
    return pl.pallas_call(
        _router_kernel,
        grid=(t // tm,),
        in_specs=[row(d), full(1, d), full(d, LANES), full(1, LANES), full(tm, tm)],
        out_specs=[row(d), row(LANES), row(LANES), row(LANES), full(8, LANES)],
        out_shape=[jax.ShapeDtypeStruct((t, d), F32), jax.ShapeDtypeStruct((t, LANES), jnp.int32),
                   jax.ShapeDtypeStruct((t, LANES), F32), jax.ShapeDtypeStruct((t, LANES), jnp.int32),
                   jax.ShapeDtypeStruct((8, LANES), F32)],
        scratch_shapes=[pltpu.VMEM((8, LANES), F32)],
        compiler_params=_cparams("arbitrary"),
        name="moe_router",
    )(h, norm_g.reshape(1, d), rw, rb, ltri)


def _expert_kernel(blk_e_ref, n_used_ref, x_ref, w1_ref, b1_ref, w2_ref, b2_ref, o_ref, w2x_ref):
    i = pl.program_id(0)
    f = w2_ref.shape[1]
    half_lanes = LANES // 2
    n_merged = f // LANES

    @pl.when((i == 0) | (blk_e_ref[i] != blk_e_ref[jnp.maximum(i - 1, 0)]))
    def _():
        for s in range(w2x_ref.shape[0]):
            lanes = slice(s * LANES, (s + 1) * LANES)
            for c in range(n_merged):
                lo = w2_ref[0, c * half_lanes:(c + 1) * half_lanes, lanes]
                hi = w2_ref[0, (c + n_merged) * half_lanes:(c + n_merged + 1) * half_lanes, lanes]
                w2x_ref[s, pl.ds(c * LANES, half_lanes, stride=2), :] = lo
                w2x_ref[s, pl.ds(c * LANES + 1, half_lanes, stride=2), :] = hi

    @pl.when(i < n_used_ref[0])
    def _():
        x = x_ref[...].astype(BF16)
        h = _dot(x, w1_ref[0].astype(BF16)) + b1_ref[0]
        hg = jnp.minimum(h, SWIGLU_LIMIT)
        gate = hg * jax.nn.sigmoid(SWIGLU_ALPHA * hg)
        lin = jnp.clip(h, -SWIGLU_LIMIT, SWIGLU_LIMIT) + 1.0
        even = lax.broadcasted_iota(jnp.int32, (1, LANES), 1) % 2 == 0
        def swiglu(c):
            cols = slice(c * LANES, (c + 1) * LANES)
            nxt = pltpu.roll(lin[:, cols], LANES - 1, 1)
            return jnp.where(even, gate[:, cols] * nxt, 0.0)

        act = jnp.concatenate([(swiglu(c) + pltpu.roll(swiglu(c + n_merged), 1, 1)).astype(BF16)
                               for c in range(n_merged)], axis=1)
        w2x = jnp.concatenate([w2x_ref[s] for s in range(w2x_ref.shape[0])], axis=1)
        o_ref[...] = _pack_bf16_pairs(_dot(act, w2x.astype(BF16)) + b2_ref[0])

    @pl.when(i >= n_used_ref[0])
    def _():
        o_ref[...] = jnp.zeros_like(o_ref)


def moe_experts(xs, blk_e, n_used, w1, b1, w2, b2):
    r = xs.shape[0]
    d, f2 = w1.shape[1:]
    m = MOE_ROW_BLOCK
    ex = lambda *s: pl.BlockSpec((1,) + s, lambda i, be, nu: (be[i],) + (0,) * len(s))
    grid_spec = pltpu.PrefetchScalarGridSpec(
        num_scalar_prefetch=2,
        grid=(r // m,),
        in_specs=[pl.BlockSpec((m, d), lambda i, be, nu: (i, 0)),
                  ex(d, f2), ex(1, f2), ex(f2 // 2, d), ex(1, d)],
        out_specs=pl.BlockSpec((m, d // 2), lambda i, be, nu: (i, 0)),
        scratch_shapes=[pltpu.VMEM((d // LANES, f2 // 2, LANES), F32)],
    )
    return pl.pallas_call(
        _expert_kernel,
        grid_spec=grid_spec,
        out_shape=jax.ShapeDtypeStruct((r, d // 2), F32),
        compiler_params=_cparams("arbitrary"),
        name="moe_experts",
    )(blk_e, n_used, xs, w1, b1, w2, b2)


def _combine_kernel(h_ref, gate_ref, *refs):
    *y_refs, o_ref = refs
    acc = h_ref[...]
    for k, y_ref in enumerate(y_refs):
        acc = acc + gate_ref[:, k:k + 1] * _unpack_bf16_pairs(y_ref[...])
    o_ref[...] = acc


def moe_combine(h, gate, ys_k, tm):
    t, d = h.shape
    row = lambda w: pl.BlockSpec((tm, w), lambda i: (i, 0))
    return pl.pallas_call(
        _combine_kernel,
        grid=(t // tm,),
        in_specs=[row(d), row(LANES)] + [row(d // 2)] * len(ys_k),
        out_specs=row(d),
        out_shape=jax.ShapeDtypeStruct((t, d), F32),
        compiler_params=_cparams("parallel"),
        name="moe_combine",
    )(h, gate, *ys_k)


def _layer(x, mem, norm_mix_g, w_in, q_norm_g, k_cmp_norm_g, k_slc_norm_g, k_win_norm_g,
           cmp_pe_k, cmp_pe_v, cmp_k_w1, cmp_k_w2, cmp_v_w1, cmp_v_w2,
           rwkv_mu, rwkv_w0, rwkv_w_up, rwkv_a0, rwkv_a_up, rwkv_g_up, rwkv_k_k, rwkv_k_a,
           rwkv_r_k, rwkv_ln_w, rwkv_ln_b, w_out,
           norm_x_g, norm_mem_g, xq_w, xk_w, xv_w, xq_norm_g, xk_norm_g, xo_w,
           norm_ffn_g, router_w, router_b, mlp1_w, mlp1_b, mlp2_w, mlp2_b):
    b, s, d = x.shape
    t = b * s
    tm = 512
    xt = x.reshape(t, d)

    w_nsa = jnp.pad(w_in[:, :NSA_PROJ], ((0, 0), (0, NSA_PROJ_PAD - NSA_PROJ)))
    p_nsa, p_rwkv = norm_matmul(xt, norm_mix_g, [w_nsa, w_in[:, NSA_PROJ:]], tm)

    qn, ks, vs, kw, vw, gates = nsa_prep(p_nsa, q_norm_g, k_slc_norm_g, k_win_norm_g, b, s, tm)
    kcv = nsa_compress(p_nsa, cmp_pe_k, cmp_pe_v, cmp_k_w1, cmp_k_w2, cmp_v_w1, cmp_v_w2, k_cmp_norm_g, b, s)
    o_nsa = nsa_attention(qn, kcv, ks, vs, kw, vw, gates, b, s)

    at, bt, kt, rt, v, bw, kwd, wc, g_gate, bonus = rwkv_prep(
        p_rwkv, rwkv_mu, rwkv_w0, rwkv_w_up, rwkv_a0, rwkv_a_up, rwkv_g_up, rwkv_k_k, rwkv_k_a, rwkv_r_k, b, s, tm)
    ta, tr, arb, yv = rwkv_intra(at, bt, kt, rt, v, ts=256)
    y = rwkv_scan(ta, tr, arb, yv, rt, v, bw, kwd, wc, ts=256)

    h1 = out_proj(xt, o_nsa, y, bonus, g_gate, rwkv_ln_w, rwkv_ln_b, w_out, s, tm)
    m = mem.shape[1]
    kv, = norm_matmul(mem.reshape(b * m, d), norm_mem_g, [jnp.concatenate([xk_w, xv_w], axis=1)], m)
    h2 = cross_attention(h1, kv.reshape(b, m, 2 * d), norm_x_g, xq_w, xq_norm_g, xk_norm_g, xo_w, b, s, tm)

    xn, top_i, gate, rank, seen = moe_router(h2, norm_ffn_g, router_w, router_b, tm)
    top_i = top_i[:, :TOP_K]
    a = t * TOP_K
    mb = MOE_ROW_BLOCK
    idx_bits = max(a - 1, 1).bit_length()
    assert N_EXPERTS << idx_bits < 2 ** 31
    packed = (top_i.reshape(a) << idx_bits) | jnp.arange(a, dtype=jnp.int32)
    order = jnp.sort(packed) & ((1 << idx_bits) - 1)
    counts = seen[0, :N_EXPERTS].astype(jnp.int32)
    starts = jnp.cumsum(counts) - counts
    padded = (counts + mb - 1) // mb * mb
    pends = jnp.cumsum(padded)
    pstarts = pends - padded
    pos = pstarts[top_i] + rank[:, :TOP_K]
    n_blocks = -(-a // mb) + N_EXPERTS
    r = n_blocks * mb
    blk_start = jnp.arange(n_blocks, dtype=jnp.int32) * mb
    blk_e = jnp.minimum(jnp.sum(pends[None, :] <= blk_start[:, None], axis=1), N_EXPERTS - 1).astype(jnp.int32)
    src_i = blk_start[:, None] + jnp.arange(mb, dtype=jnp.int32)[None, :] - (pstarts - starts)[blk_e][:, None]
    valid = src_i < (starts + counts)[blk_e][:, None]
    row_src = jnp.where(valid, order[jnp.minimum(src_i, a - 1)] // TOP_K, 0).astype(jnp.int32).reshape(r)
    n_used = (pends[-1] // mb).astype(jnp.int32).reshape(1)
    xs = xn.at[row_src].get(mode="promise_in_bounds")
    f2 = mlp1_w.shape[2]
    ys = moe_experts(xs, blk_e, n_used, mlp1_w, mlp1_b.reshape(N_EXPERTS, 1, f2), mlp2_w,
                     mlp2_b.reshape(N_EXPERTS, 1, d))
    ys_k = [ys.at[pos[:, k]].get(mode="promise_in_bounds") for k in range(TOP_K)]
    out = moe_combine(h2, gate, ys_k, 256)
    return out.reshape(b, s, d)


def kernel(x, mem, norm_mix_g, w_in, q_norm_g, k_cmp_norm_g, k_slc_norm_g, k_win_norm_g, cmp_pe_k, cmp_pe_v, cmp_k_w1, cmp_k_w2, cmp_v_w1, cmp_v_w2, rwkv_mu, rwkv_w0, rwkv_w_up, rwkv_a0, rwkv_a_up, rwkv_g_up, rwkv_k_k, rwkv_k_a, rwkv_r_k, rwkv_ln_w, rwkv_ln_b, w_out, norm_x_g, norm_mem_g, xq_w, xk_w, xv_w, xq_norm_g, xk_norm_g, xo_w, norm_ffn_g, router_w, router_b, mlp1_w, mlp1_b, mlp2_w, mlp2_b):
    params = (norm_mix_g, w_in, q_norm_g, k_cmp_norm_g, k_slc_norm_g, k_win_norm_g, cmp_pe_k, cmp_pe_v,
              cmp_k_w1, cmp_k_w2, cmp_v_w1, cmp_v_w2, rwkv_mu, rwkv_w0, rwkv_w_up, rwkv_a0, rwkv_a_up,
              rwkv_g_up, rwkv_k_k, rwkv_k_a, rwkv_r_k, rwkv_ln_w, rwkv_ln_b, w_out, norm_x_g, norm_mem_g,
              xq_w, xk_w, xv_w, xq_norm_g, xk_norm_g, xo_w, norm_ffn_g, router_w, router_b,
              mlp1_w, mlp1_b, mlp2_w, mlp2_b)
    h = x
    for layer in range(norm_mix_g.shape[0]):
        h = _layer(h, mem, *[prm[layer] for prm in params])
    return h
```

```python
import functools

import numpy as np
import jax
import jax.numpy as jnp
from jax import lax
from jax.experimental import pallas as pl
from jax.experimental.pallas import tpu as pltpu

F32 = jnp.float32
BF16 = jnp.bfloat16
HIGHEST = lax.Precision.HIGHEST

V7X_VMEM_BYTES = 64 * 1024 * 1024
VMEM_LIMIT = V7X_VMEM_BYTES * 3 // 4

HEAD_DIM = 64
NSA_HEADS = 8
NSA_GROUPS = 2
NSA_HPG = NSA_HEADS // NSA_GROUPS
CMP_BLOCK = 32
CMP_STRIDE = 16
SLC_BLOCK = 64
SLC_TOPK = 16
WINDOW = 512
Q_BLOCK = 128
SEL_CHUNK = 512
RWKV_HEADS = 8
RWKV_DIM = RWKV_HEADS * HEAD_DIM
RWKV_CHUNK = 64
GN_EPS = HEAD_DIM * 1e-5
X_HEADS = 4
N_EXPERTS = 32
TOP_K = 4
SWIGLU_LIMIT = 7.0
SWIGLU_ALPHA = 1.702
MOE_ROW_BLOCK = 256
RMS_EPS = 1e-6
NEG_INF = -1e30
BIG = 1e9
REMOVED = -3e38
LANES = 128
MXU_DIM = 256

NSA_PROJ = NSA_HEADS * HEAD_DIM + 6 * NSA_GROUPS * HEAD_DIM + NSA_HEADS * 3
NSA_PROJ_PAD = -(-NSA_PROJ // LANES) * LANES
RWKV_PROJ = 3 * RWKV_DIM + 64 + 64 + 128


def _cparams(*sem):
    return pltpu.CompilerParams(dimension_semantics=sem, vmem_limit_bytes=VMEM_LIMIT)


def _dot(a, b, **kw):
    return jnp.dot(a, b, preferred_element_type=F32, **kw)


def _dot_nt(a, b, **kw):
    return lax.dot_general(a, b, (((1,), (1,)), ((), ())), preferred_element_type=F32, **kw)


def _dot_tn(a, b, **kw):
    return lax.dot_general(a, b, (((0,), (0,)), ((), ())), preferred_element_type=F32, **kw)


def _rms(x, g):
    return x * lax.rsqrt(jnp.mean(x * x, axis=-1, keepdims=True) + RMS_EPS) * g


def _split_bf16(x, terms=3):
    parts = []
    for _ in range(terms):
        hi = x.astype(BF16)
        parts.append(hi)
        x = x - hi.astype(F32)
    return parts


def _dot_x_sel(x, sel):
    w = sel.shape[0]
    parts = _split_bf16(x)
    slabs = [sum(_dot(p[:, c:c + w], sel) for p in parts) for c in range(0, x.shape[1], w)]
    return slabs[0] if len(slabs) == 1 else jnp.concatenate(slabs, axis=1)


def _dot_sel_x(sel, x):
    w = sel.shape[1]
    parts = _split_bf16(x)
    slabs = [sum(_dot(sel, p[r:r + w]) for p in parts) for r in range(0, x.shape[0], w)]
    return slabs[0] if len(slabs) == 1 else jnp.concatenate(slabs, axis=0)


def _pack_bf16_pairs(x):
    n = x.shape[1] // 2
    bits = lambda v: lax.bitcast_convert_type(v.astype(BF16).astype(F32), jnp.uint32)
    return lax.bitcast_convert_type((bits(x[:, :n]) >> 16) | bits(x[:, n:]), F32)


def _unpack_bf16_pairs(w):
    u = lax.bitcast_convert_type(w, jnp.uint32)
    lo = lax.bitcast_convert_type(u << 16, F32)
    hi = lax.bitcast_convert_type(u & jnp.uint32(0xFFFF0000), F32)
    return jnp.concatenate([lo, hi], axis=1)


def _block_diag_ones(n, blk, scale=1.0):
    i = np.arange(n)
    return jnp.asarray(((i[:, None] // blk) == (i[None, :] // blk)).astype(np.float32) * scale, BF16)


def _norm_matmul_kernel(x_ref, g_ref, *refs):
    n = len(refs) // 2
    xn = _rms(x_ref[...], g_ref[...]).astype(BF16)
    for w_ref, o_ref in zip(refs[:n], refs[n:]):
        o_ref[...] = _dot(xn, w_ref[...])


def norm_matmul(x, g, ws, tm):
    m, d = x.shape
    return pl.pallas_call(
        _norm_matmul_kernel,
        grid=(m // tm,),
        in_specs=[pl.BlockSpec((tm, d), lambda i: (i, 0)), pl.BlockSpec((1, d), lambda i: (0, 0))]
                 + [pl.BlockSpec((d, w.shape[1]), lambda i: (0, 0)) for w in ws],
        out_specs=[pl.BlockSpec((tm, w.shape[1]), lambda i: (i, 0)) for w in ws],
        out_shape=[jax.ShapeDtypeStruct((m, w.shape[1]), F32) for w in ws],
        compiler_params=_cparams("parallel"),
        name="norm_matmul",
    )(x, g.reshape(1, d), *[w.astype(BF16) for w in ws])


def _nsa_prep_kernel(p_ref, qg_ref, ksg_ref, kwg_ref, bdq_ref, bdk_ref, place_ref, feat_ref,
                     qt_o, ks_o, vst_o, kw_o, vwt_o, gate_o):
    p = p_ref[...]
    tm = p.shape[0]
    nq = NSA_HEADS * HEAD_DIM
    gw = NSA_GROUPS * HEAD_DIM
    q = p[:, :nq]
    msq = _dot_x_sel(q * q, bdq_ref[...])
    qn = q * lax.rsqrt(msq + RMS_EPS) * qg_ref[...] * (HEAD_DIM ** -0.5)
    qt_o[0] = qn.T.reshape(NSA_HEADS, HEAD_DIM, tm).astype(BF16)

    def seg(k):
        return p[:, nq + k * gw: nq + (k + 1) * gw]

    def head_norm(t, g):
        ms = _dot_x_sel(t * t, bdk_ref[...])
        return t * lax.rsqrt(ms + RMS_EPS) * g

    feat = feat_ref[...].astype(F32)
    ks = head_norm(seg(2), ksg_ref[...]).astype(BF16)
    kw = head_norm(seg(4), kwg_ref[...]).astype(BF16)
    vs_t = seg(3).T
    vw_t = seg(5).T
    for g in range(NSA_GROUPS):
        ks_o[0, g] = (_dot(ks, place_ref[g]) + feat).astype(BF16)
        kw_o[0, g] = (_dot(kw, place_ref[g])[:, :2 * HEAD_DIM] + feat[:, :2 * HEAD_DIM]).astype(BF16)
        for j in range(tm // Q_BLOCK):
            tile = (slice(g * HEAD_DIM, (g + 1) * HEAD_DIM), slice(j * Q_BLOCK, (j + 1) * Q_BLOCK))
            vst_o[0, g, j] = vs_t[tile].astype(BF16)
            vwt_o[0, g, j] = vw_t[tile].astype(BF16)
    gate_o[...] = jax.nn.sigmoid(p[:, nq + 6 * gw: nq + 6 * gw + LANES])


def nsa_prep(p_nsa, q_g, ks_g, kw_g, b, s, tm):
    t = p_nsa.shape[0]
    nq = NSA_HEADS * HEAD_DIM
    gw = NSA_GROUPS * HEAD_DIM
    grp, dh = NSA_GROUPS, HEAD_DIM
    tps = s // tm
    kx = 2 * dh + LANES
    assert s // SLC_BLOCK <= LANES
    tok = np.arange(s)
    feat = np.zeros((s, kx), np.float32)
    feat[:, dh] = tok // SLC_BLOCK * SLC_BLOCK
    feat[:, dh + 1] = tok % SLC_BLOCK
    feat[:, dh + 2:dh + 4] = 1.0
    feat[tok, 2 * dh + tok // SLC_BLOCK] = 1.0
    place = np.zeros((grp, gw, kx), np.float32)
    for g in range(grp):
        place[g, g * dh + np.arange(dh), np.arange(dh)] = 1.0
    tile = lambda v, n: jnp.tile(v.reshape(1, HEAD_DIM), (1, n))
    row = lambda w: pl.BlockSpec((tm, w), lambda i: (i, 0))
    full = lambda *sh: pl.BlockSpec(sh, lambda i: (0,) * len(sh))
    seq = lambda *sh: pl.BlockSpec((1, grp) + sh, lambda i: (i // tps, 0, i % tps) + (0,) * (len(sh) - 1))
    return pl.pallas_call(
        _nsa_prep_kernel,
        grid=(t // tm,),
        in_specs=[row(NSA_PROJ_PAD), full(1, nq), full(1, gw), full(1, gw), full(MXU_DIM, MXU_DIM), full(gw, gw),
                  full(grp, gw, kx), pl.BlockSpec((tm, kx), lambda i: (i % tps, 0))],
        out_specs=[pl.BlockSpec((1, NSA_HEADS, dh, tm), lambda i: (i // tps, 0, 0, i % tps)),
                   seq(tm, kx), seq(tm // Q_BLOCK, dh, Q_BLOCK), seq(tm, 2 * dh), seq(tm // Q_BLOCK, dh, Q_BLOCK),
                   row(LANES)],
        out_shape=[jax.ShapeDtypeStruct((b, NSA_HEADS, dh, s), BF16),
                   jax.ShapeDtypeStruct((b, grp, s, kx), BF16),
                   jax.ShapeDtypeStruct((b, grp, s // Q_BLOCK, dh, Q_BLOCK), BF16),
                   jax.ShapeDtypeStruct((b, grp, s, 2 * dh), BF16),
                   jax.ShapeDtypeStruct((b, grp, s // Q_BLOCK, dh, Q_BLOCK), BF16),
                   jax.ShapeDtypeStruct((t, LANES), F32)],
        compiler_params=_cparams("parallel"),
        name="nsa_prep",
    )(p_nsa, tile(q_g, NSA_HEADS), tile(ks_g, NSA_GROUPS), tile(kw_g, NSA_GROUPS),
      _block_diag_ones(MXU_DIM, HEAD_DIM, 1.0 / HEAD_DIM), _block_diag_ones(gw, HEAD_DIM, 1.0 / HEAD_DIM),
      jnp.asarray(place, BF16), jnp.asarray(feat, BF16))


def _compress_kernel(x_ref, pe_ref, w1_ref, w2_ref, g_ref, bd_ref, o_ref):
    nc = x_ref.shape[0] // CMP_STRIDE
    first = jnp.zeros((nc, x_ref.shape[1]), F32)
    second = jnp.zeros_like(first)
    for l in range(CMP_STRIDE):
        x_l = x_ref[pl.ds(l, nc, stride=CMP_STRIDE), :]
        first = first + _dot(x_l + pe_ref[0, l:l + 1], w1_ref[0, l], precision=HIGHEST)
        second = second + _dot(x_l + pe_ref[0, CMP_STRIDE + l:CMP_STRIDE + l + 1], w1_ref[0, CMP_STRIDE + l],
                               precision=HIGHEST)
    h1 = first + pltpu.roll(second, nc - 1, 0)
    out = _dot(jax.nn.silu(h1), w2_ref[0], precision=HIGHEST)
    normed = out * lax.rsqrt(_dot_x_sel(out * out, bd_ref[...]) + RMS_EPS) * g_ref[...]
    o_ref[0, 0] = jnp.where(pl.program_id(0) == 0, normed, out).astype(BF16)


def nsa_compress(p_nsa, pe_k, pe_v, kw1, kw2, vw1, vw2, kc_g, b, s):
    nc = s // CMP_STRIDE
    gw = NSA_GROUPS * HEAD_DIM
    first_col_block = NSA_HEADS * HEAD_DIM // gw
    eye = jnp.eye(NSA_GROUPS, dtype=F32)

    def both(w):
        return jnp.einsum('gh,...ij->...gihj', eye, w).reshape(w.shape[:-2] + (gw, gw))

    w1 = both(jnp.stack([kw1, vw1]).reshape(2, CMP_BLOCK, HEAD_DIM, HEAD_DIM))
    w2 = both(jnp.stack([kw2, vw2]))
    pe = jnp.stack([jnp.tile(pe_k, (1, NSA_GROUPS)), jnp.tile(pe_v, (1, NSA_GROUPS))])
    return pl.pallas_call(
        _compress_kernel,
        grid=(2, b),
        in_specs=[pl.BlockSpec((s, gw), lambda kv, bi: (bi, first_col_block + kv)),
                  pl.BlockSpec((1, CMP_BLOCK, gw), lambda kv, bi: (kv, 0, 0)),
                  pl.BlockSpec((1, CMP_BLOCK, gw, gw), lambda kv, bi: (kv, 0, 0, 0)),
                  pl.BlockSpec((1, gw, gw), lambda kv, bi: (kv, 0, 0)),
                  pl.BlockSpec((1, gw), lambda kv, bi: (0, 0)),
                  pl.BlockSpec((gw, gw), lambda kv, bi: (0, 0))],
        out_specs=pl.BlockSpec((1, 1, nc, gw), lambda kv, bi: (kv, bi, 0, 0)),
        out_shape=jax.ShapeDtypeStruct((2, b, nc, gw), BF16),
        compiler_params=_cparams("parallel", "parallel"),
        name="nsa_compress",
    )(p_nsa, pe, w1, w2, jnp.tile(kc_g.reshape(1, HEAD_DIM), (1, NSA_GROUPS)),
      _block_diag_ones(gw, HEAD_DIM, 1.0 / HEAD_DIM))


def _masked_exp_cols(s, mask):
    sm = jnp.where(mask, s, NEG_INF)
    m = jnp.max(sm, axis=0, keepdims=True)
    p = jnp.exp(sm - jnp.where(m > 0.5 * NEG_INF, m, 0.0))
    l = jnp.sum(p, axis=0, keepdims=True)
    return p, 1.0 / jnp.where(l > 0.0, l, 1.0)


def _nsa_attn_kernel(qt_ref, kc_ref, vct_ref, ovt_ref, ks_ref, vst_ref, kw_ref, vwt_ref, gate_ref, o_ref,
                     *, n_sel, n_top):
    g = pl.program_id(1)
    t0 = pl.program_id(2) * Q_BLOCK
    ks_ref, vst_ref, kw_ref, vwt_ref = (r.at[0, 0] for r in (ks_ref, vst_ref, kw_ref, vwt_ref))
    cols = NSA_HPG * Q_BLOCK
    col = lax.broadcasted_iota(jnp.int32, (1, cols), 1)
    tq = t0 + col % Q_BLOCK
    head = g * NSA_HPG + col // Q_BLOCK
    slope = lax.bitcast_convert_type((127 - (head + 1)) << 23, F32)
    qt = jnp.concatenate([qt_ref[0, h] for h in range(NSA_HPG)], axis=1)
    frow = lax.broadcasted_iota(jnp.int32, (HEAD_DIM, cols), 0)
    tq_hi = (tq // SLC_BLOCK * SLC_BLOCK).astype(F32)
    tq_lo = (tq % SLC_BLOCK).astype(F32)
    qpos = jnp.where(frow < 2, slope, jnp.where(frow == 2, -slope * tq_hi, jnp.where(frow == 3, -slope * tq_lo, 0.0)))
    qc = jnp.concatenate([qt, qpos.astype(BF16)], axis=0)

    def heads_sum(x):
        acc = x[:, :Q_BLOCK]
        for h in range(1, NSA_HPG):
            acc = acc + x[:, h * Q_BLOCK:(h + 1) * Q_BLOCK]
        return acc

    kc = kc_ref[0, 0]
    nc = kc.shape[0]
    c_last = lax.broadcasted_iota(jnp.int32, (nc, 1), 0) * CMP_STRIDE + (CMP_BLOCK - 1)
    p_c, inv_c = _masked_exp_cols(_dot(kc, qc), c_last <= tq)
    p_c = p_c * inv_c
    o_c = _dot(vct_ref[0, 0], p_c.astype(BF16))
    imp = _dot_sel_x(ovt_ref[...], heads_sum(p_c))

    bid = lax.broadcasted_iota(jnp.int32, (LANES, Q_BLOCK), 0)
    bidf = bid.astype(F32)
    tq1 = t0 + lax.broadcasted_iota(jnp.int32, (1, Q_BLOCK), 1)
    cur = tq1 // SLC_BLOCK
    forced = (bid == 0) | (bid == cur) | (bid == cur - 1)
    score = jnp.where(forced, BIG, jnp.where(bid * SLC_BLOCK <= tq1, imp, -BIG))
    score = jnp.where(bid < n_sel, score, REMOVED)
    picked = jnp.zeros((LANES, Q_BLOCK), jnp.bool_)
    for _ in range(n_top):
        m = jnp.max(score, axis=0, keepdims=True)
        pick = bidf == jnp.min(jnp.where(score == m, bidf, float(LANES)), axis=0, keepdims=True)
        picked = picked | pick
        score = jnp.where(pick, REMOVED, score)
    sel_neg = jnp.where(picked, 0.0, NEG_INF).astype(BF16)
    qx = jnp.concatenate([qc, jnp.concatenate([sel_neg] * NSA_HPG, axis=1)], axis=0)

    def values_t(ref, first_tile, n_tiles):
        return jnp.concatenate([ref[first_tile + i] for i in range(n_tiles)], axis=1)

    span = WINDOW + Q_BLOCK
    ws = pl.multiple_of(jnp.maximum(t0 - WINDOW, 0), Q_BLOCK)
    dw = tq - (ws + lax.broadcasted_iota(jnp.int32, (span, 1), 0))
    in_window = dw.astype(jnp.uint32) < WINDOW
    p_w, inv_w = _masked_exp_cols(_dot(kw_ref[pl.ds(ws, span), :], qc), in_window)
    o_w = _dot(values_t(vwt_ref, ws // Q_BLOCK, span // Q_BLOCK), p_w.astype(BF16)) * inv_w

    def scores(chunk):
        return _dot(ks_ref[pl.ds(pl.multiple_of(chunk * SEL_CHUNK, SEL_CHUNK), SEL_CHUNK), :], qx)

    def flash(s, chunk, carry):
        m, l, acc = carry
        m_new = jnp.maximum(m, jnp.max(s, axis=0, keepdims=True))
        alpha = jnp.exp(m - m_new)
        p = jnp.exp(s - m_new)
        l = alpha * l + jnp.sum(p, axis=0, keepdims=True)
        v = values_t(vst_ref, chunk * (SEL_CHUNK // Q_BLOCK), SEL_CHUNK // Q_BLOCK)
        return m_new, l, alpha * acc + _dot(v, p.astype(BF16))

    def flash_pair(chunk_a, chunk_b, carry, keep_b=None):
        s_a, s_b = scores(chunk_a), scores(chunk_b)
        if keep_b is not None:
            s_b = jnp.where(keep_b, s_b, NEG_INF)
        return flash(s_b, chunk_b, flash(s_a, chunk_a, carry))

    bpc = SEL_CHUNK // SLC_BLOCK
    blk_any = jnp.max(jnp.where(picked, 1.0, 0.0), axis=1, keepdims=True)
    chunk_bit = lax.bitcast_convert_type((bid[:, :1] // bpc + 127) << 23, F32)
    bits = jnp.max((blk_any * chunk_bit).reshape(LANES // bpc, bpc, 1), axis=1)
    active = jnp.sum(bits, axis=0, keepdims=True)[0, 0].astype(jnp.int32)

    def full_step(j, state):
        def on_active(st):
            return lax.cond(st[0] >= 0,
                            lambda s2: (jnp.int32(-1), flash_pair(s2[0], j, s2[1])),
                            lambda s2: (j, s2[1]), st)
        return lax.cond((active >> j) & 1 == 1, on_active, lambda st: st, state)

    n_full = t0 // SEL_CHUNK
    init = (jnp.full((1, cols), NEG_INF, F32), jnp.zeros((1, cols), F32), jnp.zeros((HEAD_DIM, cols), F32))
    waiting, carry = lax.fori_loop(0, n_full, full_step, (jnp.int32(-1), init))
    keep = n_full * SEL_CHUNK + lax.broadcasted_iota(jnp.int32, (SEL_CHUNK, 1), 0) <= tq
    _, l_s, acc_s = lax.cond(
        waiting >= 0,
        lambda c: flash_pair(waiting, n_full, c, keep_b=keep),
        lambda c: flash(jnp.where(keep, scores(n_full), NEG_INF), n_full, c), carry)
    o_s = acc_s / l_s

    for h in range(NSA_HPG):
        hs = slice(h * Q_BLOCK, (h + 1) * Q_BLOCK)
        gate = lambda br: gate_ref[0, 0, br, h:h + 1, :]
        o_ref[0, h] = gate(0) * o_c[:, hs] + gate(1) * o_s[:, hs] + gate(2) * o_w[:, hs]


def nsa_attention(qt, kcv, ks_x, vs_t, kw_x, vw_t, gates, b, s):
    nq_blocks = s // Q_BLOCK
    nc = s // CMP_STRIDE
    n_cmp = nc - CMP_BLOCK // CMP_STRIDE + 1
    n_sel = s // SLC_BLOCK
    n_top = min(SLC_TOPK, n_sel)
    grp, dh = NSA_GROUPS, HEAD_DIM
    assert n_sel <= LANES and s % SEL_CHUNK == 0 and s >= WINDOW + Q_BLOCK
    c_start = np.arange(n_cmp) * CMP_STRIDE
    s_start = np.arange(n_sel) * SLC_BLOCK
    ovt = np.zeros((LANES, nc), np.float32)
    ovt[:n_sel, :n_cmp] = (np.clip(np.minimum((c_start + CMP_BLOCK)[:, None], s_start[None] + SLC_BLOCK)
                                   - np.maximum(c_start[:, None], s_start[None]), 0, None) / CMP_BLOCK).T

    c_pos = np.arange(nc) * CMP_STRIDE + (CMP_BLOCK - 1)
    c_feat = np.zeros((nc, dh), np.float32)
    c_feat[:, 0] = c_pos // SLC_BLOCK * SLC_BLOCK
    c_feat[:, 1] = c_pos % SLC_BLOCK
    c_feat[:, 2:4] = 1.0
    kc, vc = (a.reshape(b, nc, grp, dh) for a in kcv)
    kc = kc.transpose(0, 2, 1, 3)
    kc_x = jnp.concatenate([kc, jnp.broadcast_to(jnp.asarray(c_feat, BF16), kc.shape)], axis=-1)
    vct = vc.transpose(0, 2, 3, 1)
    gates_t = gates[:, :NSA_HEADS * 3].reshape(b, s, grp, NSA_HPG, 3).transpose(0, 2, 4, 3, 1)

    grp_spec = lambda *shape: pl.BlockSpec((1, 1) + shape, lambda bi, g, i: (bi, g) + (0,) * len(shape),
                                           pipeline_mode=pl.Buffered(1))
    return pl.pallas_call(
        functools.partial(_nsa_attn_kernel, n_sel=n_sel, n_top=n_top),
        grid=(b, grp, nq_blocks),
        in_specs=[pl.BlockSpec((1, NSA_HPG, dh, Q_BLOCK), lambda bi, g, i: (bi, g, 0, i)),
                  grp_spec(nc, 2 * dh), grp_spec(dh, nc),
                  pl.BlockSpec((LANES, nc), lambda bi, g, i: (0, 0)),
                  grp_spec(s, 2 * dh + LANES), grp_spec(s // Q_BLOCK, dh, Q_BLOCK),
                  grp_spec(s, 2 * dh), grp_spec(s // Q_BLOCK, dh, Q_BLOCK),
                  pl.BlockSpec((1, 1, 3, NSA_HPG, Q_BLOCK), lambda bi, g, i: (bi, g, 0, 0, i))],
        out_specs=pl.BlockSpec((1, NSA_HPG, dh, Q_BLOCK), lambda bi, g, i: (bi, g, 0, i)),
        out_shape=jax.ShapeDtypeStruct((b, NSA_HEADS, dh, s), F32),
        compiler_params=_cparams("parallel", "parallel", "arbitrary"),
        name="nsa_attention",
    )(qt, kc_x, vct, jnp.asarray(ovt, BF16), ks_x, vs_t, kw_x, vw_t, gates_t)


def _rwkv_prep_kernel(p_ref, prev_ref, mu_ref, w0_ref, a0_ref, kk_ref, ka_ref, rk_ref,
                      wup_ref, aup_ref, gup_ref, bd_ref, ltri_ref, lones_ref, csum_ref,
                      at_o, bt_o, kt_o, rt_o, v_o, bw_o, kw_o, wc_o, g_o, bonus_o, *, tiles_per_seq):
    p = p_ref[...]
    tm = p.shape[0]
    first = pl.program_id(0) % tiles_per_seq == 0
    last_prev = jnp.where(first, 0.0, prev_ref[7:8, :])
    prev = pltpu.roll(p, 1, 0)
    prev = jnp.where(lax.broadcasted_iota(jnp.int32, (tm, 1), 0) == 0, last_prev, prev)
    pm = p + (prev - p) * mu_ref[...]
    d = RWKV_DIM
    r, k, v = pm[:, :d], pm[:, d:2 * d], pm[:, 2 * d:3 * d]
    lora = pm[:, 3 * d:3 * d + LANES]
    gd = pm[:, 3 * d + LANES:3 * d + 2 * LANES]
    z = -(w0_ref[...] + _dot(jnp.tanh(lora).astype(BF16), wup_ref[...]))
    softplus = jnp.maximum(z, 0.0) + jnp.log(1.0 + jnp.exp(-jnp.abs(z)))
    w = -softplus - 0.5
    a = jax.nn.sigmoid(a0_ref[...] + _dot(lora.astype(BF16), aup_ref[...]))
    g_o[...] = _dot(jax.nn.sigmoid(gd).astype(BF16), gup_ref[...])
    bd = bd_ref[...]
    kkr = k * kk_ref[...]
    kk = kkr / jnp.maximum(jnp.sqrt(_dot_x_sel(kkr * kkr, bd)), 1e-12)
    k2 = k * (1.0 + (a - 1.0) * ka_ref[...])
    bonus_o[...] = _dot_x_sel(r * k2 * rk_ref[...], bd) * v
    lw = -jnp.exp(w)
    lw_parts = _split_bf16(lw)
    cum = _dot_sel_x(ltri_ref[...], lw)
    tot = _dot_sel_x(lones_ref[...], lw)
    e_in = jnp.exp(cum)
    e_out = jnp.exp(-cum)
    e_end = jnp.exp(tot - cum)

    def put_heads(o, val):
        for h in range(RWKV_HEADS):
            o[0, h] = val[:, h * HEAD_DIM:(h + 1) * HEAD_DIM].astype(o.dtype)

    put_heads(at_o, -kk * jnp.exp(cum - lw))
    put_heads(bt_o, kk * a * e_out)
    put_heads(kt_o, k2 * e_out)
    put_heads(rt_o, r * e_in)
    put_heads(v_o, v)
    put_heads(bw_o, kk * a * e_end)
    put_heads(kw_o, k2 * e_end)
    put_heads(wc_o, jnp.exp(sum(_dot(csum_ref[...], p) for p in lw_parts)))


def rwkv_prep(p_rwkv, mu, w0, w_up, a0, a_up, g_up, k_k, k_a, r_k, b, s, tm):
    t = p_rwkv.shape[0]
    d = RWKV_DIM
    c = RWKV_CHUNK
    tps = s // tm
    cpt = tm // c
    wup = jnp.concatenate([w_up, jnp.zeros_like(a_up)], axis=0).astype(BF16)
    aup = jnp.concatenate([jnp.zeros_like(w_up), a_up], axis=0).astype(BF16)
    i = np.arange(tm)
    j = np.arange(MXU_DIM)
    same = (j[:, None] // c) == (j[None, :] // c)
    ltri = jnp.asarray(same & (j[:, None] >= j[None, :]), BF16)
    lones = jnp.asarray(same, BF16)
    csum = jnp.asarray(np.arange(cpt)[:, None] == (i[None, :] // c), BF16)
    row = lambda w: pl.BlockSpec((tm, w), lambda i: (i, 0))
    full = lambda *sh: pl.BlockSpec(sh, lambda i: (0,) * len(sh))
    heads = lambda n: pl.BlockSpec((1, RWKV_HEADS, n, HEAD_DIM), lambda i: (i // tps, 0, i % tps, 0))
    hshape = lambda n, dt: jax.ShapeDtypeStruct((b, RWKV_HEADS, n, HEAD_DIM), dt)
    vec = lambda x: x.reshape(1, -1)
    return pl.pallas_call(
        functools.partial(_rwkv_prep_kernel, tiles_per_seq=tps),
        grid=(t // tm,),
        in_specs=[row(RWKV_PROJ),
                  pl.BlockSpec((8, RWKV_PROJ), lambda i: (jnp.maximum(i * (tm // 8) - 1, 0), 0)),
                  full(1, RWKV_PROJ), full(1, d), full(1, d), full(1, d), full(1, d), full(1, d),
                  full(LANES, d), full(LANES, d), full(LANES, d), full(MXU_DIM, MXU_DIM),
                  full(MXU_DIM, MXU_DIM), full(MXU_DIM, MXU_DIM), full(cpt, tm)],
        out_specs=[heads(tm)] * 7 + [heads(cpt), row(d), row(d)],
        out_shape=[hshape(s, BF16)] * 7 + [hshape(s // c, F32)] + [jax.ShapeDtypeStruct((t, d), F32)] * 2,
        compiler_params=_cparams("parallel"),
        name="rwkv_prep",
    )(p_rwkv, p_rwkv, vec(mu), vec(w0), vec(a0), vec(k_k), vec(k_a), vec(r_k), wup, aup, g_up.astype(BF16),
      _block_diag_ones(MXU_DIM, HEAD_DIM), ltri, lones, csum)


def _bdot(a, b):
    return lax.dot_general(a, b, (((2,), (1,)), ((0,), (0,))), preferred_element_type=F32)


def _bdot_nt(a, b):
    return lax.dot_general(a, b, (((2,), (2,)), ((0,), (0,))), preferred_element_type=F32)


def _bdot_tn(a, b):
    return lax.dot_general(a, b, (((1,), (1,)), ((0,), (0,))), preferred_element_type=F32)


def _rwkv_intra_kernel(at_ref, bt_ref, kt_ref, rt_ref, v_ref, ta_o, tr_o, arb_o, yv_o):
    c = RWKV_CHUNK
    _, nh, ts, dh = at_ref.shape
    n = nh * (ts // c)
    chunked = lambda ref: ref[0].reshape(n, c, dh)
    at, bt, kt, rt, v = (chunked(r) for r in (at_ref, bt_ref, kt_ref, rt_ref, v_ref))
    ri = lax.broadcasted_iota(jnp.int32, (1, c, c), 1)
    ci = lax.broadcasted_iota(jnp.int32, (1, c, c), 2)
    strict = ri > ci
    incl = ri >= ci
    ar = jnp.concatenate([at, rt], axis=1)
    xb = _bdot_nt(ar, bt)
    xk = _bdot_nt(ar, kt)
    l_ab = jnp.where(strict, xb[:, :c], 0.0)
    a_ak = jnp.where(strict, xk[:, :c], 0.0)
    a_rb = jnp.where(incl, xb[:, c:], 0.0)
    a_rk = jnp.where(incl, xk[:, c:], 0.0)
    pw = l_ab
    tinv = jnp.where(ri == ci, 1.0, 0.0) + l_ab
    for _ in range(int(np.log2(c)) - 1):
        pw_b = pw.astype(BF16)
        pw = _bdot(pw_b, pw_b)
        tinv = tinv + _bdot(tinv.astype(BF16), pw.astype(BF16))
    tinv_b = tinv.astype(BF16)

    def put(o, val):
        o[0] = val.reshape(nh, ts, val.shape[-1]).astype(o.dtype)

    put(ta_o, _bdot(tinv_b, at))
    put(tr_o, _bdot(tinv_b, _bdot(a_ak.astype(BF16), v).astype(BF16)))
    put(arb_o, a_rb)
    put(yv_o, _bdot(a_rk.astype(BF16), v))


def rwkv_intra(at, bt, kt, rt, v, ts):
    b, h, s, dh = at.shape
    seq = lambda: pl.BlockSpec((1, h, ts, dh), lambda bi, i: (bi, 0, i, 0))
    shp = lambda dt: jax.ShapeDtypeStruct((b, h, s, dh), dt)
    return pl.pallas_call(
        _rwkv_intra_kernel,
        grid=(b, s // ts),
        in_specs=[seq()] * 5,
        out_specs=[seq()] * 4,
        out_shape=[shp(BF16), shp(F32), shp(BF16), shp(F32)],
        compiler_params=_cparams("parallel", "parallel"),
        name="rwkv_intra",
    )(at, bt, kt, rt, v)


def _rwkv_scan_kernel(ta_ref, tr_ref, arb_ref, yv_ref, rt_ref, v_ref, bw_ref, kw_ref, wc_ref, y_ref, st_ref):
    c = RWKV_CHUNK
    nb, nh, ts, dh = ta_ref.shape
    n = nb * nh

    @pl.when(pl.program_id(0) == 0)
    def _():
        st_ref[...] = jnp.zeros_like(st_ref)

    def chunk_step(j, _):
        sl = (slice(None), slice(None), pl.ds(pl.multiple_of(j * c, c), c), slice(None))
        get = lambda ref: ref[sl].reshape(n, c, dh)
        st = st_ref[...]
        st_b = st.astype(BF16)
        u = _bdot_nt(get(ta_ref), st_b) + get(tr_ref)
        u_b = u.astype(BF16)
        y = _bdot_nt(get(rt_ref), st_b) + _bdot(get(arb_ref), u_b) + get(yv_ref)
        wc = wc_ref[:, :, pl.ds(pl.program_id(0) * (ts // c) + j, 1), :].reshape(n, 1, dh)
        st_ref[...] = st * wc + _bdot_tn(jnp.concatenate([u_b, get(v_ref)], axis=1),
                                         jnp.concatenate([get(bw_ref), get(kw_ref)], axis=1))
        y_ref[sl] = y.reshape(nb, nh, c, dh)
        return 0

    lax.fori_loop(0, ts // c, chunk_step, 0)


def rwkv_scan(ta, tr, arb, yv, rt, v, bw, kw, wc, ts):
    b, h, s, dh = ta.shape
    seq = lambda n: pl.BlockSpec((b, h, n, dh), lambda i: (0, 0, i, 0))
    return pl.pallas_call(
        _rwkv_scan_kernel,
        grid=(s // ts,),
        in_specs=[seq(ts)] * 8 + [pl.BlockSpec(wc.shape, lambda i: (0, 0, 0, 0))],
        out_specs=seq(ts),
        out_shape=jax.ShapeDtypeStruct((b, h, s, dh), F32),
        scratch_shapes=[pltpu.VMEM((b * h, dh, dh), F32)],
        compiler_params=_cparams("arbitrary"),
        name="rwkv_scan",
    )(ta, tr, arb, yv, rt, v, bw, kw, wc)


def _out_proj_kernel(x_ref, on_ref, y_ref, bonus_ref, g_ref, lnw_ref, lnb_ref, bd_ref, wn_ref, wr_ref, o_ref):
    y = jnp.concatenate([y_ref[0, h] for h in range(RWKV_HEADS)], axis=-1)
    bd = bd_ref[...]
    yc = y - _dot_x_sel(y, bd)
    yn = yc * lax.rsqrt(_dot_x_sel(yc * yc, bd) + GN_EPS)
    o_rwkv = (yn * lnw_ref[...] + lnb_ref[...] + bonus_ref[...]) * g_ref[...]
    tm = y.shape[0]
    o_nsa_t = on_ref[0].reshape(NSA_HEADS * HEAD_DIM, tm)
    o_ref[...] = (x_ref[...] + _dot_tn(o_nsa_t.astype(BF16), wn_ref[...])
                  + _dot(o_rwkv.astype(BF16), wr_ref[...]))


def out_proj(x, o_nsa_t, y, bonus, g, ln_w, ln_b, w_out, s, tm):
    t, d = x.shape
    dn = o_nsa_t.shape[1] * o_nsa_t.shape[2]
    dr = bonus.shape[1]
    tps = s // tm
    row = lambda w: pl.BlockSpec((tm, w), lambda i: (i, 0))
    full = lambda *sh: pl.BlockSpec(sh, lambda i: (0,) * len(sh))
    return pl.pallas_call(
        _out_proj_kernel,
        grid=(t // tm,),
        in_specs=[row(d), pl.BlockSpec((1, NSA_HEADS, HEAD_DIM, tm), lambda i: (i // tps, 0, 0, i % tps)),
                  pl.BlockSpec((1, RWKV_HEADS, tm, HEAD_DIM), lambda i: (i // tps, 0, i % tps, 0)),
                  row(dr), row(dr), full(1, dr), full(1, dr), full(MXU_DIM, MXU_DIM), full(dn, d), full(dr, d)],
        out_specs=row(d),
        out_shape=jax.ShapeDtypeStruct((t, d), F32),
        compiler_params=_cparams("parallel"),
        name="out_proj",
    )(x, o_nsa_t, y, bonus, g, ln_w.reshape(1, dr), ln_b.reshape(1, dr),
      _block_diag_ones(MXU_DIM, HEAD_DIM, 1.0 / HEAD_DIM), w_out[:dn].astype(BF16), w_out[dn:].astype(BF16))


def _cross_attn_kernel(h_ref, g_ref, wq_ref, qg_ref, kv_ref, kg_ref, wo_ref, o_ref):
    h = h_ref[...]
    d = h.shape[1]
    xd = d // X_HEADS
    q = _dot(_rms(h, g_ref[...]).astype(BF16), wq_ref[...])
    kv = kv_ref[0]
    outs = []
    for hd in range(X_HEADS):
        qh = _rms(q[:, hd * xd:(hd + 1) * xd], qg_ref[...]) * (xd ** -0.5)
        kh = _rms(kv[:, hd * xd:(hd + 1) * xd], kg_ref[...])
        vh = kv[:, d + hd * xd:d + (hd + 1) * xd]
        s = _dot_nt(qh.astype(BF16), kh.astype(BF16))
        p = jnp.exp(s - jnp.max(s, axis=-1, keepdims=True))
        p = p / jnp.sum(p, axis=-1, keepdims=True)
        outs.append(_dot(p.astype(BF16), vh.astype(BF16)))
    o = jnp.concatenate(outs, axis=-1)
    o_ref[...] = h + _dot(o.astype(BF16), wo_ref[...])


def cross_attention(h, kv, norm_g, xq_w, xq_g, xk_g, xo_w, b, s, tm):
    t, d = h.shape
    m = kv.shape[1]
    xd = d // X_HEADS
    tiles = s // tm
    full = lambda *sh: pl.BlockSpec(sh, lambda i: (0,) * len(sh))
    return pl.pallas_call(
        _cross_attn_kernel,
        grid=(t // tm,),
        in_specs=[pl.BlockSpec((tm, d), lambda i: (i, 0)), full(1, d), full(d, d), full(1, xd),
                  pl.BlockSpec((1, m, 2 * d), lambda i: (i // tiles, 0, 0)), full(1, xd), full(d, d)],
        out_specs=pl.BlockSpec((tm, d), lambda i: (i, 0)),
        out_shape=jax.ShapeDtypeStruct((t, d), F32),
        compiler_params=_cparams("parallel"),
        name="cross_attention",
    )(h, norm_g.reshape(1, d), xq_w.astype(BF16), xq_g.reshape(1, xd), kv, xk_g.reshape(1, xd),
      xo_w.astype(BF16))


def _router_kernel(h_ref, g_ref, rw_ref, rb_ref, ltri_ref, xn_o, idx_o, gate_o, rank_o, count_o, seen_ref):
    @pl.when(pl.program_id(0) == 0)
    def _():
        seen_ref[...] = jnp.zeros_like(seen_ref)

    xn = _rms(h_ref[...], g_ref[...])
    xn_o[...] = xn
    logits = _dot(xn, rw_ref[...], precision=HIGHEST) + rb_ref[...]
    tm = logits.shape[0]
    lane = lax.broadcasted_iota(jnp.int32, (tm, LANES), 1)
    lanef = lane.astype(F32)
    logits = jnp.where(lane < N_EXPERTS, logits, REMOVED)
    idx_acc = jnp.zeros((tm, LANES), F32)
    val_acc = jnp.zeros((tm, LANES), F32)
    chosen = jnp.zeros((tm, LANES), F32)
    picks = []
    top = None
    for k in range(TOP_K):
        m = jnp.max(logits, axis=-1, keepdims=True)
        idx = jnp.min(jnp.where(logits == m, lanef, float(LANES)), axis=-1, keepdims=True)
        pick = lanef == idx
        picks.append(pick)
        chosen = jnp.where(pick, 1.0, chosen)
        logits = jnp.where(pick, REMOVED, logits)
        top = m if top is None else top
        idx_acc = jnp.where(lane == k, idx, idx_acc)
        val_acc = jnp.where(lane == k, jnp.exp(m - top), val_acc)
    idx_o[...] = idx_acc.astype(jnp.int32)
    gate_o[...] = val_acc / jnp.sum(val_acc, axis=-1, keepdims=True)
    before = seen_ref[0:1, :] + _dot(ltri_ref[...], chosen.astype(BF16))
    rank_acc = jnp.zeros((tm, LANES), F32)
    for k, pick in enumerate(picks):
        rank_acc = jnp.where(lane == k, jnp.sum(jnp.where(pick, before, 0.0), axis=-1, keepdims=True), rank_acc)
    rank_o[...] = rank_acc.astype(jnp.int32)
    seen_ref[...] = seen_ref[...] + jnp.sum(chosen, axis=0, keepdims=True)
    count_o[...] = seen_ref[...]


def moe_router(h, norm_g, router_w, router_b, tm):
    t, d = h.shape
    rw = jnp.zeros((d, LANES), F32).at[:, :N_EXPERTS].set(router_w)
    rb = jnp.zeros((1, LANES), F32).at[0, :N_EXPERTS].set(router_b)
    i = np.arange(tm)
    ltri = jnp.asarray(i[:, None] > i[None, :], BF16)
    row = lambda w: pl.BlockSpec((tm, w), lambda i: (i, 0))
    full = lambda *s: pl.BlockSpec(s, lambda i: (0,) * len(s))
    return pl.pallas_call(
        _router_kernel,
        grid=(t // tm,),
        in_specs=[row(d), full(1, d), full(d, LANES), full(1, LANES), full(tm, tm)],
        out_specs=[row(d), row(LANES), row(LANES), row(LANES), full(8, LANES)],
        out_shape=[jax.ShapeDtypeStruct((t, d), F32), jax.ShapeDtypeStruct((t, LANES), jnp.int32),
                   jax.ShapeDtypeStruct((t, LANES), F32), jax.ShapeDtypeStruct((t, LANES), jnp.int32),
                   jax.ShapeDtypeStruct((8, LANES), F32)],
        scratch_shapes=[pltpu.VMEM((8, LANES), F32)],
        compiler_params=_cparams("arbitrary"),
        name="moe_router",
    )(h, norm_g.reshape(1, d), rw, rb, ltri)


def _expert_kernel(blk_e_ref, n_used_ref, x_ref, w1_ref, b1_ref, w2_ref, b2_ref, o_ref, w2x_ref):
    i = pl.program_id(0)
    f = w2_ref.shape[1]
    half_lanes = LANES // 2
    n_merged = f // LANES

    @pl.when((i == 0) | (blk_e_ref[i] != blk_e_ref[jnp.maximum(i - 1, 0)]))
    def _():
        for s in range(w2x_ref.shape[0]):
            lanes = slice(s * LANES, (s + 1) * LANES)
            for c in range(n_merged):
                lo = w2_ref[0, c * half_lanes:(c + 1) * half_lanes, lanes]
                hi = w2_ref[0, (c + n_merged) * half_lanes:(c + n_merged + 1) * half_lanes, lanes]
                w2x_ref[s, pl.ds(c * LANES, half_lanes, stride=2), :] = lo
                w2x_ref[s, pl.ds(c * LANES + 1, half_lanes, stride=2), :] = hi

    @pl.when(i < n_used_ref[0])
    def _():
        x = x_ref[...].astype(BF16)
        h = _dot(x, w1_ref[0].astype(BF16)) + b1_ref[0]
        hg = jnp.minimum(h, SWIGLU_LIMIT)
        gate = hg * jax.nn.sigmoid(SWIGLU_ALPHA * hg)
        lin = jnp.clip(h, -SWIGLU_LIMIT, SWIGLU_LIMIT) + 1.0
        even = lax.broadcasted_iota(jnp.int32, (1, LANES), 1) % 2 == 0
        def swiglu(c):
            cols = slice(c * LANES, (c + 1) * LANES)
            nxt = pltpu.roll(lin[:, cols], LANES - 1, 1)
            return jnp.where(even, gate[:, cols] * nxt, 0.0)

        act = jnp.concatenate([(swiglu(c) + pltpu.roll(swiglu(c + n_merged), 1, 1)).astype(BF16)
                               for c in range(n_merged)], axis=1)
        w2x = jnp.concatenate([w2x_ref[s] for s in range(w2x_ref.shape[0])], axis=1)
        o_ref[...] = _pack_bf16_pairs(_dot(act, w2x.astype(BF16)) + b2_ref[0])

    @pl.when(i >= n_used_ref[0])
    def _():
        o_ref[...] = jnp.zeros_like(o_ref)


def moe_experts(xs, blk_e, n_used, w1, b1, w2, b2):
    r = xs.shape[0]
    d, f2 = w1.shape[1:]
    m = MOE_ROW_BLOCK
    ex = lambda *s: pl.BlockSpec((1,) + s, lambda i, be, nu: (be[i],) + (0,) * len(s))
    grid_spec = pltpu.PrefetchScalarGridSpec(
        num_scalar_prefetch=2,
        grid=(r // m,),
        in_specs=[pl.BlockSpec((m, d), lambda i, be, nu: (i, 0)),
                  ex(d, f2), ex(1, f2), ex(f2 // 2, d), ex(1, d)],
        out_specs=pl.BlockSpec((m, d // 2), lambda i, be, nu: (i, 0)),
        scratch_shapes=[pltpu.VMEM((d // LANES, f2 // 2, LANES), F32)],
    )
    return pl.pallas_call(
        _expert_kernel,
        grid_spec=grid_spec,
        out_shape=jax.ShapeDtypeStruct((r, d // 2), F32),
        compiler_params=_cparams("arbitrary"),
        name="moe_experts",
    )(blk_e, n_used, xs, w1, b1, w2, b2)


def _combine_kernel(h_ref, gate_ref, *refs):
    *y_refs, o_ref = refs
    acc = h_ref[...]
    for k, y_ref in enumerate(y_refs):
        acc = acc + gate_ref[:, k:k + 1] * _unpack_bf16_pairs(y_ref[...])
    o_ref[...] = acc


def moe_combine(h, gate, ys_k, tm):
    t, d = h.shape
    row = lambda w: pl.BlockSpec((tm, w), lambda i: (i, 0))
    return pl.pallas_call(
        _combine_kernel,
        grid=(t // tm,),
        in_specs=[row(d), row(LANES)] + [row(d // 2)] * len(ys_k),
        out_specs=row(d),
        out_shape=jax.ShapeDtypeStruct((t, d), F32),
        compiler_params=_cparams("parallel"),
        name="moe_combine",
    )(h, gate, *ys_k)


def _layer(x, mem, norm_mix_g, w_in, q_norm_g, k_cmp_norm_g, k_slc_norm_g, k_win_norm_g,
           cmp_pe_k, cmp_pe_v, cmp_k_w1, cmp_k_w2, cmp_v_w1, cmp_v_w2,
           rwkv_mu, rwkv_w0, rwkv_w_up, rwkv_a0, rwkv_a_up, rwkv_g_up, rwkv_k_k, rwkv_k_a,
           rwkv_r_k, rwkv_ln_w, rwkv_ln_b, w_out,
           norm_x_g, norm_mem_g, xq_w, xk_w, xv_w, xq_norm_g, xk_norm_g, xo_w,
           norm_ffn_g, router_w, router_b, mlp1_w, mlp1_b, mlp2_w, mlp2_b):
    b, s, d = x.shape
    t = b * s
    tm = 512
    xt = x.reshape(t, d)

    w_nsa = jnp.pad(w_in[:, :NSA_PROJ], ((0, 0), (0, NSA_PROJ_PAD - NSA_PROJ)))
    p_nsa, p_rwkv = norm_matmul(xt, norm_mix_g, [w_nsa, w_in[:, NSA_PROJ:]], tm)

    qn, ks, vs, kw, vw, gates = nsa_prep(p_nsa, q_norm_g, k_slc_norm_g, k_win_norm_g, b, s, tm)
    kcv = nsa_compress(p_nsa, cmp_pe_k, cmp_pe_v, cmp_k_w1, cmp_k_w2, cmp_v_w1, cmp_v_w2, k_cmp_norm_g, b, s)
    o_nsa = nsa_attention(qn, kcv, ks, vs, kw, vw, gates, b, s)

    at, bt, kt, rt, v, bw, kwd, wc, g_gate, bonus = rwkv_prep(
        p_rwkv, rwkv_mu, rwkv_w0, rwkv_w_up, rwkv_a0, rwkv_a_up, rwkv_g_up, rwkv_k_k, rwkv_k_a, rwkv_r_k, b, s, tm)
    ta, tr, arb, yv = rwkv_intra(at, bt, kt, rt, v, ts=256)
    y = rwkv_scan(ta, tr, arb, yv, rt, v, bw, kwd, wc, ts=256)

    h1 = out_proj(xt, o_nsa, y, bonus, g_gate, rwkv_ln_w, rwkv_ln_b, w_out, s, tm)
    m = mem.shape[1]
    kv, = norm_matmul(mem.reshape(b * m, d), norm_mem_g, [jnp.concatenate([xk_w, xv_w], axis=1)], m)
    h2 = cross_attention(h1, kv.reshape(b, m, 2 * d), norm_x_g, xq_w, xq_norm_g, xk_norm_g, xo_w, b, s, tm)

    xn, top_i, gate, rank, seen = moe_router(h2, norm_ffn_g, router_w, router_b, tm)
    top_i = top_i[:, :TOP_K]
    a = t * TOP_K
    mb = MOE_ROW_BLOCK
    idx_bits = max(a - 1, 1).bit_length()
    assert N_EXPERTS << idx_bits < 2 ** 31
    packed = (top_i.reshape(a) << idx_bits) | jnp.arange(a, dtype=jnp.int32)
    order = jnp.sort(packed) & ((1 << idx_bits) - 1)
    counts = seen[0, :N_EXPERTS].astype(jnp.int32)
    starts = jnp.cumsum(counts) - counts
    padded = (counts + mb - 1) // mb * mb
    pends = jnp.cumsum(padded)
    pstarts = pends - padded
    pos = pstarts[top_i] + rank[:, :TOP_K]
    n_blocks = -(-a // mb) + N_EXPERTS
    r = n_blocks * mb
    blk_start = jnp.arange(n_blocks, dtype=jnp.int32) * mb
    blk_e = jnp.minimum(jnp.sum(pends[None, :] <= blk_start[:, None], axis=1), N_EXPERTS - 1).astype(jnp.int32)
    src_i = blk_start[:, None] + jnp.arange(mb, dtype=jnp.int32)[None, :] - (pstarts - starts)[blk_e][:, None]
    valid = src_i < (starts + counts)[blk_e][:, None]
    row_src = jnp.where(valid, order[jnp.minimum(src_i, a - 1)] // TOP_K, 0).astype(jnp.int32).reshape(r)
    n_used = (pends[-1] // mb).astype(jnp.int32).reshape(1)
    xs = xn.at[row_src].get(mode="promise_in_bounds")
    f2 = mlp1_w.shape[2]
    ys = moe_experts(xs, blk_e, n_used, mlp1_w, mlp1_b.reshape(N_EXPERTS, 1, f2), mlp2_w,
                     mlp2_b.reshape(N_EXPERTS, 1, d))
    ys_k = [ys.at[pos[:, k]].get(mode="promise_in_bounds") for k in range(TOP_K)]
    out = moe_combine(h2, gate, ys_k, 256)
    return out.reshape(b, s, d)


def kernel(x, mem, norm_mix_g, w_in, q_norm_g, k_cmp_norm_g, k_slc_norm_g, k_win_norm_g, cmp_pe_k, cmp_pe_v, cmp_k_w1, cmp_k_w2, cmp_v_w1, cmp_v_w2, rwkv_mu, rwkv_w0, rwkv_w_up, rwkv_a0, rwkv_a_up, rwkv_g_up, rwkv_k_k, rwkv_k_a, rwkv_r_k, rwkv_ln_w, rwkv_ln_b, w_out, norm_x_g, norm_mem_g, xq_w, xk_w, xv_w, xq_norm_g, xk_norm_g, xo_w, norm_ffn_g, router_w, router_b, mlp1_w, mlp1_b, mlp2_w, mlp2_b):
    params = (norm_mix_g, w_in, q_norm_g, k_cmp_norm_g, k_slc_norm_g, k_win_norm_g, cmp_pe_k, cmp_pe_v,
              cmp_k_w1, cmp_k_w2, cmp_v_w1, cmp_v_w2, rwkv_mu, rwkv_w0, rwkv_w_up, rwkv_a0, rwkv_a_up,
              rwkv_g_up, rwkv_k_k, rwkv_k_a, rwkv_r_k, rwkv_ln_w, rwkv_ln_b, w_out, norm_x_g, norm_mem_g,
              xq_w, xk_w, xv_w, xq_norm_g, xk_norm_g, xo_w, norm_ffn_g, router_w, router_b,
              mlp1_w, mlp1_b, mlp2_w, mlp2_b)
    h = x
    for layer in range(norm_mix_g.shape[0]):
        h = _layer(h, mem, *[prm[layer] for prm in params])
    return h
```
